```python
import jax, jax.numpy as jnp
from jax import lax
import numpy as np

D_MODEL = 1024
BATCH = 8
SEQ = 8192
DEPTH = 4

N_A_LAYERS = DEPTH // 2
N_B_LAYERS = DEPTH - N_A_LAYERS
CHUNK = 128
GMLP_WIDTH = 2 * D_MODEL
GMLP_GROUPS = 16
GMLP_GROUP_DIM = GMLP_WIDTH // GMLP_GROUPS
N_HEADS = 16
HEAD_DIM = D_MODEL // N_HEADS
Q_BLOCK = 128
FFN_HIDDEN = -(-8 * D_MODEL // (3 * 256)) * 256
EPS = 1e-6

kernel_name = "yoco_gmlp_fox_adaln_sandwich"


def rms_norm(x, g):
    xf = x.astype(jnp.float32)
    y = xf * lax.rsqrt(jnp.mean(xf * xf, axis=-1, keepdims=True) + EPS)
    return (y * g.astype(jnp.float32)).astype(x.dtype)


def layer_norm(x, g, b):
    xf = x.astype(jnp.float32)
    mu = jnp.mean(xf, axis=-1, keepdims=True)
    xc = xf - mu
    y = xc * lax.rsqrt(jnp.mean(xc * xc, axis=-1, keepdims=True) + EPS)
    return (y * g.astype(jnp.float32) + b.astype(jnp.float32)).astype(x.dtype)


def modulate(h, shift, scale):
    return h * (1 + scale[:, None, :]) + shift[:, None, :]


def swiglu(h, w_gu, w_down):
    gu = h @ w_gu
    g, u = jnp.split(gu, 2, axis=-1)
    return (jax.nn.silu(g) * u) @ w_down


def gmlp_mixer(h, w_in, b_in, ln_g, ln_b, w_s, b_s, w_out):
    B, S, _ = h.shape
    z = jax.nn.gelu(h @ w_in + b_in)
    u, v = jnp.split(z, 2, axis=-1)
    v = layer_norm(v, ln_g, ln_b)
    v = v.reshape(B, S // CHUNK, CHUNK, GMLP_GROUPS, GMLP_GROUP_DIM)
    causal = jnp.tril(jnp.ones((CHUNK, CHUNK), dtype=w_s.dtype))
    ws = w_s * causal[None]
    v = jnp.einsum('gts,bnsgc->bntgc', ws, v) + b_s.T[:, :, None]
    y = u * v.reshape(B, S, GMLP_WIDTH)
    return y @ w_out


def shared_kv(x, mod_kv, kv_norm_g, kv_w, kv_b_f, k_norm_g):
    B, S, _ = x.shape
    shift, scale = jnp.split(mod_kv, 2, axis=-1)
    h = modulate(rms_norm(x, kv_norm_g), shift, scale)
    kvf = h @ kv_w
    k = kvf[..., :D_MODEL].reshape(B, S, N_HEADS, HEAD_DIM)
    k = rms_norm(k, k_norm_g).transpose(0, 2, 1, 3)
    v = kvf[..., D_MODEL:2 * D_MODEL].reshape(B, S, N_HEADS, HEAD_DIM).transpose(0, 2, 1, 3)
    f_logit = kvf[..., 2 * D_MODEL:].astype(jnp.float32) + kv_b_f.astype(jnp.float32)
    dcum = jnp.cumsum(jax.nn.log_sigmoid(f_logit), axis=1).transpose(0, 2, 1)
    return k, v, dcum


def fox_attention(q, k, v, dcum):
    B, H, S, Dh = q.shape
    nb = S // Q_BLOCK
    qb = q.reshape(B, H, nb, Q_BLOCK, Dh).transpose(2, 0, 1, 3, 4)
    db = dcum.reshape(B, H, nb, Q_BLOCK).transpose(2, 0, 1, 3)
    kpos = jnp.arange(S)
    scale = HEAD_DIM ** -0.5

    def block(args):
        qi, di, i = args
        qpos = i * Q_BLOCK + jnp.arange(Q_BLOCK)
        logits = jnp.einsum('bhqd,bhkd->bhqk', qi, k).astype(jnp.float32) * scale
        logits = logits + di[..., :, None] - dcum[..., None, :]
        logits = jnp.where(kpos[None, :] <= qpos[:, None], logits, -jnp.inf)
        p = jax.nn.softmax(logits, axis=-1)
        return jnp.einsum('bhqk,bhkd->bhqd', p.astype(v.dtype), v)

    o = lax.map(block, (qb, db, jnp.arange(nb)))
    return o.transpose(1, 2, 0, 3, 4).reshape(B, H, S, Dh)


def fox_mixer(h, w_qg, q_norm_g, w_o, k, v, dcum):
    B, S, _ = h.shape
    qg = h @ w_qg
    q = qg[..., :D_MODEL].reshape(B, S, N_HEADS, HEAD_DIM)
    q = rms_norm(q, q_norm_g).transpose(0, 2, 1, 3)
    gate = jax.nn.sigmoid(qg[..., D_MODEL:])
    o = fox_attention(q, k, v, dcum).transpose(0, 2, 1, 3).reshape(B, S, D_MODEL)
    return (o * gate) @ w_o


def _fwd_setup_inputs(seed: int = 0) -> dict:
    key = jax.random.key(seed)
    ks = jax.random.split(key, 26)
    D, F, GW = D_MODEL, FFN_HIDDEN, GMLP_WIDTH

    def nrm(k, shape, scale):
        return jax.random.normal(k, shape, jnp.float32) * scale

    def gain(k, shape):
        return 1.0 + nrm(k, shape, 0.05)

    return {
        "x": nrm(ks[0], (BATCH, SEQ, D), 1.0),
        "c": nrm(ks[1], (BATCH, D), 1.0),
        "ada_w": nrm(ks[2], (DEPTH, D, 6 * D), 0.5 * D ** -0.5),
        "ada_b": nrm(ks[3], (DEPTH, 6 * D), 0.01),
        "pre_mix_g": gain(ks[4], (DEPTH, D)),
        "post_mix_g": gain(ks[5], (DEPTH, D)),
        "pre_ffn_g": gain(ks[6], (DEPTH, D)),
        "post_ffn_g": gain(ks[7], (DEPTH, D)),
        "ffn_w_gu": nrm(ks[8], (DEPTH, D, 2 * F), D ** -0.5),
        "ffn_w_down": nrm(ks[9], (DEPTH, F, D), F ** -0.5),
        "a_w_in": nrm(ks[10], (N_A_LAYERS, D, 2 * GW), D ** -0.5),
        "a_b_in": nrm(ks[11], (N_A_LAYERS, 2 * GW), 0.01),
        "a_ln_g": gain(ks[12], (N_A_LAYERS, GW)),
        "a_ln_b": nrm(ks[13], (N_A_LAYERS, GW), 0.01),
        "a_w_s": nrm(ks[14], (N_A_LAYERS, GMLP_GROUPS, CHUNK, CHUNK), 0.5 * CHUNK ** -0.5),
        "a_b_s": 1.0 + nrm(ks[15], (N_A_LAYERS, GMLP_GROUPS, CHUNK), 0.1),
        "a_w_out": nrm(ks[16], (N_A_LAYERS, GW, D), GW ** -0.5),
        "kv_ada_w": nrm(ks[17], (D, 2 * D), 0.5 * D ** -0.5),
        "kv_ada_b": nrm(ks[18], (2 * D,), 0.01),
        "kv_norm_g": gain(ks[19], (D,)),
        "kv_w": nrm(ks[20], (D, 2 * D + N_HEADS), D ** -0.5),
        "kv_b_f": jax.random.uniform(ks[21], (N_HEADS,), jnp.float32, 1.0, 5.0),
        "k_norm_g": gain(ks[22], (HEAD_DIM,)),
        "b_w_qg": nrm(ks[23], (N_B_LAYERS, D, 2 * D), D ** -0.5),
        "b_q_norm_g": gain(ks[24], (N_B_LAYERS, HEAD_DIM)),
        "b_w_o": nrm(ks[25], (N_B_LAYERS, D, D), D ** -0.5),
    }


def _fwd_reference(x, c, ada_w, ada_b, pre_mix_g, post_mix_g, pre_ffn_g, post_ffn_g,
              ffn_w_gu, ffn_w_down, a_w_in, a_b_in, a_ln_g, a_ln_b, a_w_s, a_b_s,
              a_w_out, kv_ada_w, kv_ada_b, kv_norm_g, kv_w, kv_b_f, k_norm_g,
              b_w_qg, b_q_norm_g, b_w_o):
    c_act = jax.nn.silu(c)
    k = v = dcum = None
    for layer in range(DEPTH):
        mod = c_act @ ada_w[layer] + ada_b[layer]
        sh_m, sc_m, g_m, sh_f, sc_f, g_f = jnp.split(mod, 6, axis=-1)
        h = modulate(rms_norm(x, pre_mix_g[layer]), sh_m, sc_m)
        if layer < N_A_LAYERS:
            i = layer
            y = gmlp_mixer(h, a_w_in[i], a_b_in[i], a_ln_g[i], a_ln_b[i],
                           a_w_s[i], a_b_s[i], a_w_out[i])
        else:
            j = layer - N_A_LAYERS
            y = fox_mixer(h, b_w_qg[j], b_q_norm_g[j], b_w_o[j], k, v, dcum)
        x = x + g_m[:, None, :] * rms_norm(y, post_mix_g[layer])
        h = modulate(rms_norm(x, pre_ffn_g[layer]), sh_f, sc_f)
        y = swiglu(h, ffn_w_gu[layer], ffn_w_down[layer])
        x = x + g_f[:, None, :] * rms_norm(y, post_ffn_g[layer])
        if layer == N_A_LAYERS - 1:
            k, v, dcum = shared_kv(x, c_act @ kv_ada_w + kv_ada_b, kv_norm_g,
                                   kv_w, kv_b_f, k_norm_g)
    return x


import jax as _jax
import jax.numpy as _jnp

TWIN_FORMAT = 'train_step'
FWD_PARAMS = ['x', 'c', 'ada_w', 'ada_b', 'pre_mix_g', 'post_mix_g', 'pre_ffn_g', 'post_ffn_g', 'ffn_w_gu', 'ffn_w_down', 'a_w_in', 'a_b_in', 'a_ln_g', 'a_ln_b', 'a_w_s', 'a_b_s', 'a_w_out', 'kv_ada_w', 'kv_ada_b', 'kv_norm_g', 'kv_w', 'kv_b_f', 'k_norm_g', 'b_w_qg', 'b_q_norm_g', 'b_w_o']
TWIN_WEIGHTS = ['ada_w', 'ada_b', 'pre_mix_g', 'post_mix_g', 'pre_ffn_g', 'post_ffn_g', 'ffn_w_gu', 'ffn_w_down', 'a_w_in', 'a_b_in', 'a_ln_g', 'a_ln_b', 'a_w_s', 'a_b_s', 'a_w_out', 'kv_ada_w', 'kv_ada_b', 'kv_norm_g', 'kv_w', 'kv_b_f', 'k_norm_g', 'b_w_qg', 'b_q_norm_g', 'b_w_o']
TWIN_DIFF_INPUT = 'x'
TWIN_INPUTS = ['x', 'c', 'ada_w', 'ada_b', 'pre_mix_g', 'post_mix_g', 'pre_ffn_g', 'post_ffn_g', 'ffn_w_gu', 'ffn_w_down', 'a_w_in', 'a_b_in', 'a_ln_g', 'a_ln_b', 'a_w_s', 'a_b_s', 'a_w_out', 'kv_ada_w', 'kv_ada_b', 'kv_norm_g', 'kv_w', 'kv_b_f', 'k_norm_g', 'b_w_qg', 'b_q_norm_g', 'b_w_o', 'loss_target', 'm_ada_w', 'm_ada_b', 'm_pre_mix_g', 'm_post_mix_g', 'm_pre_ffn_g', 'm_post_ffn_g', 'm_ffn_w_gu', 'm_ffn_w_down', 'm_a_w_in', 'm_a_b_in', 'm_a_ln_g', 'm_a_ln_b', 'm_a_w_s', 'm_a_b_s', 'm_a_w_out', 'm_kv_ada_w', 'm_kv_ada_b', 'm_kv_norm_g', 'm_kv_w', 'm_kv_b_f', 'm_k_norm_g', 'm_b_w_qg', 'm_b_q_norm_g', 'm_b_w_o', 'v_ada_w', 'v_ada_b', 'v_pre_mix_g', 'v_post_mix_g', 'v_pre_ffn_g', 'v_post_ffn_g', 'v_ffn_w_gu', 'v_ffn_w_down', 'v_a_w_in', 'v_a_b_in', 'v_a_ln_g', 'v_a_ln_b', 'v_a_w_s', 'v_a_b_s', 'v_a_w_out', 'v_kv_ada_w', 'v_kv_ada_b', 'v_kv_norm_g', 'v_kv_w', 'v_kv_b_f', 'v_k_norm_g', 'v_b_w_qg', 'v_b_q_norm_g', 'v_b_w_o']
TWIN_OUTPUTS = ['loss', 'grad_x', 'grad_ada_w', 'grad_ada_b', 'grad_pre_mix_g', 'grad_post_mix_g', 'grad_pre_ffn_g', 'grad_post_ffn_g', 'grad_ffn_w_gu', 'grad_ffn_w_down', 'grad_a_w_in', 'grad_a_b_in', 'grad_a_ln_g', 'grad_a_ln_b', 'grad_a_w_s', 'grad_a_b_s', 'grad_a_w_out', 'grad_kv_ada_w', 'grad_kv_ada_b', 'grad_kv_norm_g', 'grad_kv_w', 'grad_kv_b_f', 'grad_k_norm_g', 'grad_b_w_qg', 'grad_b_q_norm_g', 'grad_b_w_o', 'delta_ada_w', 'delta_ada_b', 'delta_pre_mix_g', 'delta_post_mix_g', 'delta_pre_ffn_g', 'delta_post_ffn_g', 'delta_ffn_w_gu', 'delta_ffn_w_down', 'delta_a_w_in', 'delta_a_b_in', 'delta_a_ln_g', 'delta_a_ln_b', 'delta_a_w_s', 'delta_a_b_s', 'delta_a_w_out', 'delta_kv_ada_w', 'delta_kv_ada_b', 'delta_kv_norm_g', 'delta_kv_w', 'delta_kv_b_f', 'delta_k_norm_g', 'delta_b_w_qg', 'delta_b_q_norm_g', 'delta_b_w_o', 'new_m_ada_w', 'new_m_ada_b', 'new_m_pre_mix_g', 'new_m_post_mix_g', 'new_m_pre_ffn_g', 'new_m_post_ffn_g', 'new_m_ffn_w_gu', 'new_m_ffn_w_down', 'new_m_a_w_in', 'new_m_a_b_in', 'new_m_a_ln_g', 'new_m_a_ln_b', 'new_m_a_w_s', 'new_m_a_b_s', 'new_m_a_w_out', 'new_m_kv_ada_w', 'new_m_kv_ada_b', 'new_m_kv_norm_g', 'new_m_kv_w', 'new_m_kv_b_f', 'new_m_k_norm_g', 'new_m_b_w_qg', 'new_m_b_q_norm_g', 'new_m_b_w_o', 'new_v_ada_w', 'new_v_ada_b', 'new_v_pre_mix_g', 'new_v_post_mix_g', 'new_v_pre_ffn_g', 'new_v_post_ffn_g', 'new_v_ffn_w_gu', 'new_v_ffn_w_down', 'new_v_a_w_in', 'new_v_a_b_in', 'new_v_a_ln_g', 'new_v_a_ln_b', 'new_v_a_w_s', 'new_v_a_b_s', 'new_v_a_w_out', 'new_v_kv_ada_w', 'new_v_kv_ada_b', 'new_v_kv_norm_g', 'new_v_kv_w', 'new_v_kv_b_f', 'new_v_k_norm_g', 'new_v_b_w_qg', 'new_v_b_q_norm_g', 'new_v_b_w_o']
TWIN_LEAF_KINDS = {'loss': 'loss', 'grad_x': 'grad_x', 'grad_ada_w': 'grad_w', 'grad_ada_b': 'grad_w', 'grad_pre_mix_g': 'grad_w', 'grad_post_mix_g': 'grad_w', 'grad_pre_ffn_g': 'grad_w', 'grad_post_ffn_g': 'grad_w', 'grad_ffn_w_gu': 'grad_w', 'grad_ffn_w_down': 'grad_w', 'grad_a_w_in': 'grad_w', 'grad_a_b_in': 'grad_w', 'grad_a_ln_g': 'grad_w', 'grad_a_ln_b': 'grad_w', 'grad_a_w_s': 'grad_w', 'grad_a_b_s': 'grad_w', 'grad_a_w_out': 'grad_w', 'grad_kv_ada_w': 'grad_w', 'grad_kv_ada_b': 'grad_w', 'grad_kv_norm_g': 'grad_w', 'grad_kv_w': 'grad_w', 'grad_kv_b_f': 'grad_w', 'grad_k_norm_g': 'grad_w', 'grad_b_w_qg': 'grad_w', 'grad_b_q_norm_g': 'grad_w', 'grad_b_w_o': 'grad_w', 'delta_ada_w': 'delta_w', 'delta_ada_b': 'delta_w', 'delta_pre_mix_g': 'delta_w', 'delta_post_mix_g': 'delta_w', 'delta_pre_ffn_g': 'delta_w', 'delta_post_ffn_g': 'delta_w', 'delta_ffn_w_gu': 'delta_w', 'delta_ffn_w_down': 'delta_w', 'delta_a_w_in': 'delta_w', 'delta_a_b_in': 'delta_w', 'delta_a_ln_g': 'delta_w', 'delta_a_ln_b': 'delta_w', 'delta_a_w_s': 'delta_w', 'delta_a_b_s': 'delta_w', 'delta_a_w_out': 'delta_w', 'delta_kv_ada_w': 'delta_w', 'delta_kv_ada_b': 'delta_w', 'delta_kv_norm_g': 'delta_w', 'delta_kv_w': 'delta_w', 'delta_kv_b_f': 'delta_w', 'delta_k_norm_g': 'delta_w', 'delta_b_w_qg': 'delta_w', 'delta_b_q_norm_g': 'delta_w', 'delta_b_w_o': 'delta_w', 'new_m_ada_w': 'new_m', 'new_m_ada_b': 'new_m', 'new_m_pre_mix_g': 'new_m', 'new_m_post_mix_g': 'new_m', 'new_m_pre_ffn_g': 'new_m', 'new_m_post_ffn_g': 'new_m', 'new_m_ffn_w_gu': 'new_m', 'new_m_ffn_w_down': 'new_m', 'new_m_a_w_in': 'new_m', 'new_m_a_b_in': 'new_m', 'new_m_a_ln_g': 'new_m', 'new_m_a_ln_b': 'new_m', 'new_m_a_w_s': 'new_m', 'new_m_a_b_s': 'new_m', 'new_m_a_w_out': 'new_m', 'new_m_kv_ada_w': 'new_m', 'new_m_kv_ada_b': 'new_m', 'new_m_kv_norm_g': 'new_m', 'new_m_kv_w': 'new_m', 'new_m_kv_b_f': 'new_m', 'new_m_k_norm_g': 'new_m', 'new_m_b_w_qg': 'new_m', 'new_m_b_q_norm_g': 'new_m', 'new_m_b_w_o': 'new_m', 'new_v_ada_w': 'new_v', 'new_v_ada_b': 'new_v', 'new_v_pre_mix_g': 'new_v', 'new_v_post_mix_g': 'new_v', 'new_v_pre_ffn_g': 'new_v', 'new_v_post_ffn_g': 'new_v', 'new_v_ffn_w_gu': 'new_v', 'new_v_ffn_w_down': 'new_v', 'new_v_a_w_in': 'new_v', 'new_v_a_b_in': 'new_v', 'new_v_a_ln_g': 'new_v', 'new_v_a_ln_b': 'new_v', 'new_v_a_w_s': 'new_v', 'new_v_a_b_s': 'new_v', 'new_v_a_w_out': 'new_v', 'new_v_kv_ada_w': 'new_v', 'new_v_kv_ada_b': 'new_v', 'new_v_kv_norm_g': 'new_v', 'new_v_kv_w': 'new_v', 'new_v_kv_b_f': 'new_v', 'new_v_k_norm_g': 'new_v', 'new_v_b_w_qg': 'new_v', 'new_v_b_q_norm_g': 'new_v', 'new_v_b_w_o': 'new_v'}


def _forward(args):
    return _fwd_reference(*[args[k] for k in FWD_PARAMS])


def _output_shape():
    def fwd():
        inp = _fwd_setup_inputs(0)
        return _fwd_reference(*[inp[k] for k in FWD_PARAMS])
    out = _jax.eval_shape(fwd)
    return out.shape, out.dtype

N_MICROBATCH = 1
ADAM_LR = 0.001
ADAM_B1 = 0.9
ADAM_B2 = 0.999
ADAM_EPS = 1e-08
ADAM_WD = 0.01
ADAM_STEP = 10
PER_EXAMPLE_BATCH_AXIS = {'x': 0, 'c': 0, 'loss_target': 0}
SHARED_INPUTS = []
_WEIGHT_DTYPES = {'ada_w': _jnp.float32, 'ada_b': _jnp.float32, 'pre_mix_g': _jnp.float32, 'post_mix_g': _jnp.float32, 'pre_ffn_g': _jnp.float32, 'post_ffn_g': _jnp.float32, 'ffn_w_gu': _jnp.float32, 'ffn_w_down': _jnp.float32, 'a_w_in': _jnp.float32, 'a_b_in': _jnp.float32, 'a_ln_g': _jnp.float32, 'a_ln_b': _jnp.float32, 'a_w_s': _jnp.float32, 'a_b_s': _jnp.float32, 'a_w_out': _jnp.float32, 'kv_ada_w': _jnp.float32, 'kv_ada_b': _jnp.float32, 'kv_norm_g': _jnp.float32, 'kv_w': _jnp.float32, 'kv_b_f': _jnp.float32, 'k_norm_g': _jnp.float32, 'b_w_qg': _jnp.float32, 'b_q_norm_g': _jnp.float32, 'b_w_o': _jnp.float32}
MOMENT_SCALE = {'ada_w': 2.847766e+00, 'ada_b': 5.873007e+00, 'pre_mix_g': 2.099483e-01, 'post_mix_g': 7.044674e+00, 'pre_ffn_g': 3.134363e-01, 'post_ffn_g': 6.595633e+00, 'ffn_w_gu': 1.809417e-01, 'ffn_w_down': 3.588150e-01, 'a_w_in': 2.822518e-01, 'a_b_in': 1.120169e+00, 'a_ln_g': 4.112368e-02, 'a_ln_b': 4.633767e-02, 'a_w_s': 8.168638e-02, 'a_b_s': 1.214286e-01, 'a_w_out': 1.442919e+00, 'kv_ada_w': 3.047124e+00, 'kv_ada_b': 5.015026e+00, 'kv_norm_g': 9.907451e-01, 'kv_w': 1.737189e+00, 'kv_b_f': 2.635750e+00, 'k_norm_g': 6.672326e-01, 'b_w_qg': 1.463078e-01, 'b_q_norm_g': 4.751763e-01, 'b_w_o': 1.727670e+00}


def _to_microbatches(a, axis):
    t = _jnp.moveaxis(a, axis, 0)
    t = t.reshape((N_MICROBATCH, t.shape[0] // N_MICROBATCH) + t.shape[1:])
    return _jnp.moveaxis(t, 1, axis + 1)


def setup_inputs(seed: int = 0) -> dict:
    inp = _fwd_setup_inputs(seed)
    key = _jax.random.fold_in(_jax.random.key(seed), 7919)
    shape, _ = _output_shape()
    out = dict(inp)
    out["loss_target"] = _jax.random.normal(_jax.random.fold_in(key, 0), shape, _jnp.float32)
    for i, name in enumerate(TWIN_WEIGHTS):
        w = inp[name].astype(_jnp.float32)
        if MOMENT_SCALE is None:
            s = _jnp.sqrt(_jnp.mean(_jnp.square(w)) + 1e-30)
        else:
            s = MOMENT_SCALE[name]
        km, kv = _jax.random.split(_jax.random.fold_in(key, i + 1))
        out[name] = w
        out["m_" + name] = s * _jax.random.normal(km, w.shape, _jnp.float32)
        out["v_" + name] = (s * s) * _jax.random.uniform(kv, w.shape, _jnp.float32, 0.5, 1.5)
    if N_MICROBATCH > 1:
        for name, axis in PER_EXAMPLE_BATCH_AXIS.items():
            out[name] = _to_microbatches(out[name], axis)
    return {'x': out['x'], 'c': out['c'], 'ada_w': out['ada_w'], 'ada_b': out['ada_b'], 'pre_mix_g': out['pre_mix_g'], 'post_mix_g': out['post_mix_g'], 'pre_ffn_g': out['pre_ffn_g'], 'post_ffn_g': out['post_ffn_g'], 'ffn_w_gu': out['ffn_w_gu'], 'ffn_w_down': out['ffn_w_down'], 'a_w_in': out['a_w_in'], 'a_b_in': out['a_b_in'], 'a_ln_g': out['a_ln_g'], 'a_ln_b': out['a_ln_b'], 'a_w_s': out['a_w_s'], 'a_b_s': out['a_b_s'], 'a_w_out': out['a_w_out'], 'kv_ada_w': out['kv_ada_w'], 'kv_ada_b': out['kv_ada_b'], 'kv_norm_g': out['kv_norm_g'], 'kv_w': out['kv_w'], 'kv_b_f': out['kv_b_f'], 'k_norm_g': out['k_norm_g'], 'b_w_qg': out['b_w_qg'], 'b_q_norm_g': out['b_q_norm_g'], 'b_w_o': out['b_w_o'], 'loss_target': out['loss_target'], 'm_ada_w': out['m_ada_w'], 'm_ada_b': out['m_ada_b'], 'm_pre_mix_g': out['m_pre_mix_g'], 'm_post_mix_g': out['m_post_mix_g'], 'm_pre_ffn_g': out['m_pre_ffn_g'], 'm_post_ffn_g': out['m_post_ffn_g'], 'm_ffn_w_gu': out['m_ffn_w_gu'], 'm_ffn_w_down': out['m_ffn_w_down'], 'm_a_w_in': out['m_a_w_in'], 'm_a_b_in': out['m_a_b_in'], 'm_a_ln_g': out['m_a_ln_g'], 'm_a_ln_b': out['m_a_ln_b'], 'm_a_w_s': out['m_a_w_s'], 'm_a_b_s': out['m_a_b_s'], 'm_a_w_out': out['m_a_w_out'], 'm_kv_ada_w': out['m_kv_ada_w'], 'm_kv_ada_b': out['m_kv_ada_b'], 'm_kv_norm_g': out['m_kv_norm_g'], 'm_kv_w': out['m_kv_w'], 'm_kv_b_f': out['m_kv_b_f'], 'm_k_norm_g': out['m_k_norm_g'], 'm_b_w_qg': out['m_b_w_qg'], 'm_b_q_norm_g': out['m_b_q_norm_g'], 'm_b_w_o': out['m_b_w_o'], 'v_ada_w': out['v_ada_w'], 'v_ada_b': out['v_ada_b'], 'v_pre_mix_g': out['v_pre_mix_g'], 'v_post_mix_g': out['v_post_mix_g'], 'v_pre_ffn_g': out['v_pre_ffn_g'], 'v_post_ffn_g': out['v_post_ffn_g'], 'v_ffn_w_gu': out['v_ffn_w_gu'], 'v_ffn_w_down': out['v_ffn_w_down'], 'v_a_w_in': out['v_a_w_in'], 'v_a_b_in': out['v_a_b_in'], 'v_a_ln_g': out['v_a_ln_g'], 'v_a_ln_b': out['v_a_ln_b'], 'v_a_w_s': out['v_a_w_s'], 'v_a_b_s': out['v_a_b_s'], 'v_a_w_out': out['v_a_w_out'], 'v_kv_ada_w': out['v_kv_ada_w'], 'v_kv_ada_b': out['v_kv_ada_b'], 'v_kv_norm_g': out['v_kv_norm_g'], 'v_kv_w': out['v_kv_w'], 'v_kv_b_f': out['v_kv_b_f'], 'v_k_norm_g': out['v_k_norm_g'], 'v_b_w_qg': out['v_b_w_qg'], 'v_b_q_norm_g': out['v_b_q_norm_g'], 'v_b_w_o': out['v_b_w_o']}


def _loss(weights, diff, rest, loss_target):
    with _jax.named_scope("forward"):
        args = {**rest, TWIN_DIFF_INPUT: diff, **{k: w.astype(_WEIGHT_DTYPES[k]) for k, w in weights.items()}}
        y = _forward(args)
    with _jax.named_scope("loss_head"):
        err = _jnp.square(y.astype(_jnp.float32) - loss_target)
        return 0.5 * _jnp.sum(_jnp.mean(err, axis=-1)) if err.ndim else 0.5 * err


def _adamw(w, g, m, v):
    m = ADAM_B1 * m + (1.0 - ADAM_B1) * g
    v = ADAM_B2 * v + (1.0 - ADAM_B2) * _jnp.square(g)
    m_hat = m / (1.0 - ADAM_B1 ** ADAM_STEP)
    v_hat = v / (1.0 - ADAM_B2 ** ADAM_STEP)
    delta = -ADAM_LR * (m_hat / (_jnp.sqrt(v_hat) + ADAM_EPS) + ADAM_WD * w)
    return delta, m, v


def reference(x, c, ada_w, ada_b, pre_mix_g, post_mix_g, pre_ffn_g, post_ffn_g, ffn_w_gu, ffn_w_down, a_w_in, a_b_in, a_ln_g, a_ln_b, a_w_s, a_b_s, a_w_out, kv_ada_w, kv_ada_b, kv_norm_g, kv_w, kv_b_f, k_norm_g, b_w_qg, b_q_norm_g, b_w_o, loss_target, m_ada_w, m_ada_b, m_pre_mix_g, m_post_mix_g, m_pre_ffn_g, m_post_ffn_g, m_ffn_w_gu, m_ffn_w_down, m_a_w_in, m_a_b_in, m_a_ln_g, m_a_ln_b, m_a_w_s, m_a_b_s, m_a_w_out, m_kv_ada_w, m_kv_ada_b, m_kv_norm_g, m_kv_w, m_kv_b_f, m_k_norm_g, m_b_w_qg, m_b_q_norm_g, m_b_w_o, v_ada_w, v_ada_b, v_pre_mix_g, v_post_mix_g, v_pre_ffn_g, v_post_ffn_g, v_ffn_w_gu, v_ffn_w_down, v_a_w_in, v_a_b_in, v_a_ln_g, v_a_ln_b, v_a_w_s, v_a_b_s, v_a_w_out, v_kv_ada_w, v_kv_ada_b, v_kv_norm_g, v_kv_w, v_kv_b_f, v_k_norm_g, v_b_w_qg, v_b_q_norm_g, v_b_w_o):
    given = dict(x=x, c=c, ada_w=ada_w, ada_b=ada_b, pre_mix_g=pre_mix_g, post_mix_g=post_mix_g, pre_ffn_g=pre_ffn_g, post_ffn_g=post_ffn_g, ffn_w_gu=ffn_w_gu, ffn_w_down=ffn_w_down, a_w_in=a_w_in, a_b_in=a_b_in, a_ln_g=a_ln_g, a_ln_b=a_ln_b, a_w_s=a_w_s, a_b_s=a_b_s, a_w_out=a_w_out, kv_ada_w=kv_ada_w, kv_ada_b=kv_ada_b, kv_norm_g=kv_norm_g, kv_w=kv_w, kv_b_f=kv_b_f, k_norm_g=k_norm_g, b_w_qg=b_w_qg, b_q_norm_g=b_q_norm_g, b_w_o=b_w_o, loss_target=loss_target, m_ada_w=m_ada_w, m_ada_b=m_ada_b, m_pre_mix_g=m_pre_mix_g, m_post_mix_g=m_post_mix_g, m_pre_ffn_g=m_pre_ffn_g, m_post_ffn_g=m_post_ffn_g, m_ffn_w_gu=m_ffn_w_gu, m_ffn_w_down=m_ffn_w_down, m_a_w_in=m_a_w_in, m_a_b_in=m_a_b_in, m_a_ln_g=m_a_ln_g, m_a_ln_b=m_a_ln_b, m_a_w_s=m_a_w_s, m_a_b_s=m_a_b_s, m_a_w_out=m_a_w_out, m_kv_ada_w=m_kv_ada_w, m_kv_ada_b=m_kv_ada_b, m_kv_norm_g=m_kv_norm_g, m_kv_w=m_kv_w, m_kv_b_f=m_kv_b_f, m_k_norm_g=m_k_norm_g, m_b_w_qg=m_b_w_qg, m_b_q_norm_g=m_b_q_norm_g, m_b_w_o=m_b_w_o, v_ada_w=v_ada_w, v_ada_b=v_ada_b, v_pre_mix_g=v_pre_mix_g, v_post_mix_g=v_post_mix_g, v_pre_ffn_g=v_pre_ffn_g, v_post_ffn_g=v_post_ffn_g, v_ffn_w_gu=v_ffn_w_gu, v_ffn_w_down=v_ffn_w_down, v_a_w_in=v_a_w_in, v_a_b_in=v_a_b_in, v_a_ln_g=v_a_ln_g, v_a_ln_b=v_a_ln_b, v_a_w_s=v_a_w_s, v_a_b_s=v_a_b_s, v_a_w_out=v_a_w_out, v_kv_ada_w=v_kv_ada_w, v_kv_ada_b=v_kv_ada_b, v_kv_norm_g=v_kv_norm_g, v_kv_w=v_kv_w, v_kv_b_f=v_kv_b_f, v_k_norm_g=v_k_norm_g, v_b_w_qg=v_b_w_qg, v_b_q_norm_g=v_b_q_norm_g, v_b_w_o=v_b_w_o)
    weights = {n: given[n] for n in TWIN_WEIGHTS}
    shared = {n: given[n] for n in SHARED_INPUTS}
    per_example = {n: given[n] for n in ['x', 'c']}
    grad_fn = _jax.value_and_grad(_loss, argnums=(0, 1))

    def one_microbatch(ex, loss_target):
        ex = dict(ex)
        diff = ex.pop(TWIN_DIFF_INPUT)
        return grad_fn(weights, diff, {**shared, **ex}, loss_target)

    if N_MICROBATCH == 1:
        loss, (grad_w, grad_x) = one_microbatch(per_example, given["loss_target"])
    else:
        def body(carry, xs):
            loss_sum, grad_sum = carry
            l_k, (gw_k, gx_k) = one_microbatch(xs[0], xs[1])
            with _jax.named_scope("update"):
                return (loss_sum + l_k, _jax.tree.map(_jnp.add, grad_sum, gw_k)), gx_k

        init = (_jnp.zeros((), _jnp.float32), _jax.tree.map(_jnp.zeros_like, weights))
        (loss, grad_w), grad_x = _jax.lax.scan(body, init, (per_example, given["loss_target"]))
    with _jax.named_scope("update"):
        delta_w, new_m, new_v = {}, {}, {}
        for n in TWIN_WEIGHTS:
            delta_w[n], new_m[n], new_v[n] = _adamw(weights[n], grad_w[n], given["m_" + n], given["v_" + n])
    return (loss, grad_x, *[grad_w[n] for n in TWIN_WEIGHTS], *[delta_w[n] for n in TWIN_WEIGHTS],
            *[new_m[n] for n in TWIN_WEIGHTS], *[new_v[n] for n in TWIN_WEIGHTS])
```

```python
import functools

import jax
import jax.numpy as jnp
from jax import lax
from jax.experimental import pallas as pl
from jax.experimental.pallas import tpu as pltpu

F32 = jnp.float32
BF16 = jnp.bfloat16
MESH = pl.DeviceIdType.MESH
NORM_EPS = 1e-6
MASKED = -1e30
LANES = 128
BF16_ROWS = 16
ROW_BLOCK_BYTES = 6 << 20
ADAM_LR, ADAM_B1, ADAM_B2, ADAM_EPS, ADAM_WD, ADAM_STEP = 0.001, 0.9, 0.999, 1e-08, 0.01, 10
N_CHIPS, N_CORES, N_DEV = 4, 2, 8
ANY = pl.BlockSpec(memory_space=pl.ANY)


def _tile(n, cap, quantum):
    best = None
    d = quantum
    while d <= min(n, cap):
        if n % d == 0:
            best = d
        d += quantum
    return n if best is None else best


def _call(body, *, name, out_shape, grid=(), in_specs=None, out_specs=None, scratch=(), sem=None, aliases=None):
    params = {} if sem is None else {"dimension_semantics": sem}
    return pl.pallas_call(
        body, name=name, grid=grid, in_specs=in_specs, out_specs=out_specs, out_shape=out_shape,
        scratch_shapes=list(scratch), input_output_aliases=aliases or {},
        compiler_params=pltpu.CompilerParams(**params))


def _place():
    x, y, c = lax.axis_index("x"), lax.axis_index("y"), lax.axis_index("c")
    return x, y, c


def _mm(a, b, mode, out_dtype, name):
    b_arr, b_idx = b if isinstance(b, tuple) else (b, None)
    bs = b_arr.shape[-2:]
    if mode == "nn":
        (M, K), (K2, N) = a.shape, bs
        dims = (((1,), (0,)), ((), ()))
    elif mode == "nt":
        (M, K), (N, K2) = a.shape, bs
        dims = (((1,), (1,)), ((), ()))
    else:
        (K, M), (K2, N) = a.shape, bs
        dims = (((0,), (0,)), ((), ()))
    assert K == K2, (name, a.shape, b_arr.shape)
    if mode == "tn":
        tm = _tile(M, 1408, LANES)
        tk = _tile(K, 1024, BF16_ROWS)
    else:
        tm = _tile(M, 512, BF16_ROWS)
        tk = K if K <= 2816 else _tile(K, 2816, LANES)
    tn = _tile(N, 512, LANES)
    if tn < 256:
        tn = N
        tm = _tile(M, 512, LANES if mode == "tn" else BF16_ROWS)
    nk = K // tk
    grid = (M // tm, N // tn, nk)

    if mode == "tn":
        a_spec = pl.BlockSpec((tk, tm), lambda i, j, k: (k, i))
    else:
        a_spec = pl.BlockSpec((tm, tk), lambda i, j, k: (i, k))
    if mode == "nt":
        b_blk, b_map = (tn, tk), (lambda i, j, k: (j, k))
    else:
        b_blk, b_map = (tk, tn), (lambda i, j, k: (k, j))
    if b_idx is None:
        b_spec = pl.BlockSpec(b_blk, b_map)
    else:
        b_spec = pl.BlockSpec((None,) + b_blk, lambda i, j, k: (b_idx,) + b_map(i, j, k))

    def body(a_ref, b_ref, o_ref, *acc):
        r = lax.dot_general(a_ref[...].astype(BF16), b_ref[...].astype(BF16), dims, preferred_element_type=F32)
        if nk == 1:
            o_ref[...] = r.astype(o_ref.dtype)
        else:
            k = pl.program_id(2)

            @pl.when(k == 0)
            def _():
                acc[0][...] = r

            @pl.when(k > 0)
            def _():
                acc[0][...] += r

            @pl.when(k == nk - 1)
            def _():
                o_ref[...] = acc[0][...].astype(o_ref.dtype)

    return _call(
        body, name=name, grid=grid, in_specs=[a_spec, b_spec],
        out_specs=pl.BlockSpec((tm, tn), lambda i, j, k: (i, j)),
        out_shape=jax.ShapeDtypeStruct((M, N), out_dtype),
        scratch=[pltpu.VMEM((tm, tn), F32)] if nk > 1 else [],
        sem=("parallel", "parallel", "arbitrary"))(a, b_arr)


def _rowwise(fn, rows, pars, outs, pouts, name):
    R = rows[0].shape[0]
    row_bytes = 4 * (sum(max(r.shape[1], LANES) for r in rows) + sum(max(c, LANES) for c, _ in outs))
    tb = _tile(R, max(BF16_ROWS, ROW_BLOCK_BYTES // row_bytes), BF16_ROWS)
    nr, npar, no = len(rows), len(pars), len(outs)

    def body(*refs):
        r_in, p_in = refs[:nr], refs[nr:nr + npar]
        r_out, p_out = refs[nr + npar:nr + npar + no], refs[nr + npar + no:]
        ro, po = fn([r[...] for r in r_in], [p[...] for p in p_in])
        for ref, val in zip(r_out, ro):
            if isinstance(val, (tuple, list)):
                off = 0
                for piece in val:
                    w = piece.shape[1]
                    ref[:, off:off + w] = piece.astype(ref.dtype)
                    off += w
            else:
                ref[...] = val.astype(ref.dtype)
        if p_out:
            first = pl.program_id(0) == 0

            @pl.when(first)
            def _():
                for ref, val in zip(p_out, po):
                    ref[...] = val

            @pl.when(jnp.logical_not(first))
            def _():
                for ref, val in zip(p_out, po):
                    ref[...] += val

    res = _call(
        body, name=name, grid=(R // tb,),
        in_specs=[pl.BlockSpec((tb, r.shape[1]), lambda i: (i, 0)) for r in rows]
        + [pl.BlockSpec(p.shape, lambda i: (0, 0)) for p in pars],
        out_specs=[pl.BlockSpec((tb, c), lambda i: (i, 0)) for c, _ in outs]
        + [pl.BlockSpec(s, lambda i: (0, 0)) for s in pouts],
        out_shape=[jax.ShapeDtypeStruct((R, c), dt) for c, dt in outs]
        + [jax.ShapeDtypeStruct(s, F32) for s in pouts],
        sem=("arbitrary",) if pouts else ("parallel",))(*rows, *pars)
    return list(res)


def _rms(x, g):
    return x * lax.rsqrt(jnp.mean(x * x, axis=-1, keepdims=True) + NORM_EPS) * g


def _norm_mod(x, g, sh, sc):
    return _rms(x, g) * (1.0 + sc) + sh


def _gated_post(y, g, gate):
    return gate * _rms(y, g)


def _norm_mod_fwd(x, g, sh, sc, name):
    return _rowwise(lambda r, p: ([_norm_mod(r[0], *p)], []), [x], [g, sh, sc], [(x.shape[1], BF16)], [], name)[0]


def _norm_mod_bwd(dxo, dh, x, g, sh, sc, name):
    def fn(r, p):
        _, vjp = jax.vjp(_norm_mod, r[2], *p)
        dx, dg, dsh, dsc = vjp(r[1].astype(F32))
        return [r[0] + dx], [dg, dsh, dsc]
    c = x.shape[1]
    return _rowwise(fn, [dxo, dh, x], [g, sh, sc], [(c, F32)], [(1, c)] * 3, name)


def _post_fwd(x, y, g, gate, name):
    return _rowwise(lambda r, p: ([r[0] + _gated_post(r[1].astype(F32), *p)], []), [x, y], [g, gate],
                    [(x.shape[1], F32)], [], name)[0]


def _post_bwd(dxo, y, g, gate, name):
    def fn(r, p):
        _, vjp = jax.vjp(_gated_post, r[1].astype(F32), *p)
        dy, dg, dgate = vjp(r[0])
        return [dy], [dg, dgate]
    c = y.shape[1]
    return _rowwise(fn, [dxo, y], [g, gate], [(c, BF16)], [(1, c)] * 2, name)


def _swiglu(g, u):
    return jax.nn.silu(g) * u


def _swiglu_fwd(gu, name):
    f = gu.shape[1] // 2
    return _rowwise(lambda r, p: ([_swiglu(r[0][:, :f].astype(F32), r[0][:, f:].astype(F32))], []), [gu], [],
                    [(f, BF16)], [], name)[0]


def _swiglu_bwd(da, gu, name):
    f = gu.shape[1] // 2

    def fn(r, p):
        _, vjp = jax.vjp(_swiglu, r[1][:, :f].astype(F32), r[1][:, f:].astype(F32))
        return [vjp(r[0].astype(F32))], []
    return _rowwise(fn, [da, gu], [], [(2 * f, BF16)], [], name)[0]


def _silu_rows(c, name):
    return _rowwise(lambda r, p: ([jax.nn.silu(r[0])], []), [c], [], [(c.shape[1], F32)], [], name)[0]


def _head_norm(x, g, scale):
    return _rms(x, g) * scale


def _head_norm_fwd(x, g, scale, name):
    return _rowwise(lambda r, p: ([_head_norm(r[0].astype(F32), p[0], scale)], []), [x], [g],
                    [(x.shape[1], BF16)], [], name)[0]


def _head_norm_bwd(dy, x, g, scale, name):
    def fn(r, p):
        _, vjp = jax.vjp(lambda t, gg: _head_norm(t, gg, scale), r[1].astype(F32), p[0])
        dx, dg = vjp(r[0])
        return [dx], [dg]
    c = x.shape[1]
    return _rowwise(fn, [dy, x], [g], [(c, F32)], [(1, c)], name)


def _out_gate_fwd(o, qg, name):
    d = o.shape[1]
    return _rowwise(lambda r, p: ([r[0] * jax.nn.sigmoid(r[1][:, d:].astype(F32))], []), [o, qg], [],
                    [(d, BF16)], [], name)[0]


def _out_gate_bwd(dog, o, qg, name):
    d = o.shape[1]

    def fn(r, p):
        _, vjp = jax.vjp(lambda oo, gl: oo * jax.nn.sigmoid(gl), r[1], r[2][:, d:].astype(F32))
        do, dgl = vjp(r[0])
        return [do, dgl], []
    return _rowwise(fn, [dog, o, qg], [], [(d, BF16), (d, BF16)], [], name)


def _loss_bwd(y, tgt, name):
    n = y.shape[1]

    def fn(r, p):
        e = r[0] - r[1]
        part = jnp.sum(jnp.sum(e * e, axis=1, keepdims=True), axis=0, keepdims=True) * (0.5 / n)
        return [e * (1.0 / n)], [part]
    return _rowwise(fn, [y, tgt], [], [(n, F32)], [(1, 1)], name)


def _adamw(w, g, m, v, name):
    shape = w.shape
    c = shape[-1]
    flat = [t.reshape(-1, c) for t in (w, g, m, v)]

    def fn(r, p):
        w_, g_, m_, v_ = r
        m2 = ADAM_B1 * m_ + (1.0 - ADAM_B1) * g_
        v2 = ADAM_B2 * v_ + (1.0 - ADAM_B2) * (g_ * g_)
        m_hat = m2 / (1.0 - ADAM_B1 ** ADAM_STEP)
        v_hat = v2 / (1.0 - ADAM_B2 ** ADAM_STEP)
        delta = -ADAM_LR * (m_hat / (jnp.sqrt(v_hat) + ADAM_EPS) + ADAM_WD * w_)
        return [delta, m2, v2], []
    res = _rowwise(fn, flat, [], [(c, F32)] * 3, [], name)
    return [t.reshape(shape) for t in res]


def _sum_slots(recv, name):
    n = recv.shape[0]
    shape = recv.shape[1:]
    c = shape[-1]
    r3 = recv.reshape(n, -1, c)
    rows = r3.shape[1]
    tb = _tile(rows, max(BF16_ROWS, ROW_BLOCK_BYTES // (4 * c * (n + 1))), BF16_ROWS)

    def body(r_ref, o_ref):
        acc = r_ref[0].astype(F32)
        for s in range(1, n):
            acc = acc + r_ref[s].astype(F32)
        o_ref[...] = acc

    out = _call(body, name=name, grid=(rows // tb,),
                in_specs=[pl.BlockSpec((n, tb, c), lambda i: (0, i, 0))],
                out_specs=pl.BlockSpec((tb, c), lambda i: (i, 0)),
                out_shape=jax.ShapeDtypeStruct((rows, c), F32), sem=("parallel",))(r3)
    return out.reshape(shape)


def _gmlp_pre(zu, zv, b_u, b_v, ln_g, ln_b):
    u = jax.nn.gelu(zu + b_u, approximate=True)
    v = jax.nn.gelu(zv + b_v, approximate=True)
    xc = v - jnp.mean(v, axis=-1, keepdims=True)
    vn = xc * lax.rsqrt(jnp.mean(xc * xc, axis=-1, keepdims=True) + NORM_EPS) * ln_g + ln_b
    return u, vn


def _gmlp_fwd(zp, b_in, ln_g, ln_b, ws, bs_t, name):
    S, gw2 = zp.shape
    gw = gw2 // 2
    G, ch, _ = ws.shape
    gd = gw // G
    tb = 2 * ch

    def body(zp_ref, bin_ref, lg_ref, lb_ref, ws_ref, bs_ref, o_ref):
        u, vn = _gmlp_pre(zp_ref[:, :gw].astype(F32), zp_ref[:, gw:].astype(F32), bin_ref[:, :gw], bin_ref[:, gw:],
                          lg_ref[...], lb_ref[...])
        vnb = vn.astype(BF16)
        for c in range(tb // ch):
            for g in range(G):
                rs, cs = slice(c * ch, (c + 1) * ch), slice(g * gd, (g + 1) * gd)
                vv = jnp.dot(ws_ref[g], vnb[rs, cs], preferred_element_type=F32) + bs_ref[:, g:g + 1]
                o_ref[rs, cs] = (u[rs, cs] * vv).astype(o_ref.dtype)

    full = lambda a: pl.BlockSpec(a.shape, lambda i: (0,) * a.ndim)
    return _call(body, name=name, grid=(S // tb,),
                 in_specs=[pl.BlockSpec((tb, gw2), lambda i: (i, 0)), full(b_in), full(ln_g), full(ln_b), full(ws), full(bs_t)],
                 out_specs=pl.BlockSpec((tb, gw), lambda i: (i, 0)),
                 out_shape=jax.ShapeDtypeStruct((S, gw), BF16), sem=("parallel",))(zp, b_in, ln_g, ln_b, ws, bs_t)


def _gmlp_bwd(dyg, zp, b_in, ln_g, ln_b, ws, ws_t, bs_t, name):
    S, gw2 = zp.shape
    gw = gw2 // 2
    G, ch, _ = ws.shape
    gd = gw // G
    tb = 2 * ch

    def body(dy_ref, zp_ref, bin_ref, lg_ref, lb_ref, ws_ref, wst_ref, bs_ref,
             dzp_ref, dbin_ref, dlg_ref, dlb_ref, dws_ref, dbs_ref, du_sc, dvn_sc):
        (u, vn), vjp = jax.vjp(_gmlp_pre, zp_ref[:, :gw].astype(F32), zp_ref[:, gw:].astype(F32), bin_ref[:, :gw],
                               bin_ref[:, gw:], lg_ref[...], lb_ref[...])
        vnb = vn.astype(BF16)
        first = pl.program_id(0) == 0

        @pl.when(first)
        def _():
            dws_ref[...] = jnp.zeros_like(dws_ref)

        lane = lax.broadcasted_iota(jnp.int32, (ch, G), 1)
        dbs = jnp.zeros((ch, G), F32)
        for g in range(G):
            cs = slice(g * gd, (g + 1) * gd)
            dws_g = jnp.zeros((ch, ch), F32)
            col = jnp.zeros((ch, 1), F32)
            for c in range(tb // ch):
                rs = slice(c * ch, (c + 1) * ch)
                vnp = vnb[rs, cs]
                vv = jnp.dot(ws_ref[g], vnp, preferred_element_type=F32) + bs_ref[:, g:g + 1]
                dy = dy_ref[rs, cs].astype(F32)
                du_sc[rs, cs] = dy * vv
                dvv = dy * u[rs, cs]
                dvvb = dvv.astype(BF16)
                dvn_sc[rs, cs] = jnp.dot(wst_ref[g], dvvb, preferred_element_type=F32)
                dws_g = dws_g + lax.dot_general(dvvb, vnp, (((1,), (1,)), ((), ())), preferred_element_type=F32)
                col = col + jnp.sum(dvv, axis=1, keepdims=True)
            dws_ref[g] += dws_g
            dbs = jnp.where(lane == g, col, dbs)
        dzu, dzv, dbu, dbv, dlg, dlb = vjp((du_sc[...], dvn_sc[...]))
        dzp_ref[:, :gw] = dzu.astype(dzp_ref.dtype)
        dzp_ref[:, gw:] = dzv.astype(dzp_ref.dtype)

        @pl.when(first)
        def _():
            dbin_ref[:, :gw] = dbu
            dbin_ref[:, gw:] = dbv
            dlg_ref[...] = dlg
            dlb_ref[...] = dlb
            dbs_ref[...] = dbs

        @pl.when(jnp.logical_not(first))
        def _():
            dbin_ref[:, :gw] += dbu
            dbin_ref[:, gw:] += dbv
            dlg_ref[...] += dlg
            dlb_ref[...] += dlb
            dbs_ref[...] += dbs

    full = lambda a: pl.BlockSpec(a.shape, lambda i: (0,) * a.ndim)
    fshape = lambda s: pl.BlockSpec(s, lambda i: (0,) * len(s))
    return _call(
        body, name=name, grid=(S // tb,),
        in_specs=[pl.BlockSpec((tb, gw), lambda i: (i, 0)), pl.BlockSpec((tb, gw2), lambda i: (i, 0)),
                  full(b_in), full(ln_g), full(ln_b), full(ws), full(ws_t), full(bs_t)],
        out_specs=[pl.BlockSpec((tb, gw2), lambda i: (i, 0)), fshape((1, gw2)), fshape((1, gw)), fshape((1, gw)),
                   fshape((G, ch, ch)), fshape((ch, G))],
        out_shape=[jax.ShapeDtypeStruct((S, gw2), BF16), jax.ShapeDtypeStruct((1, gw2), F32),
                   jax.ShapeDtypeStruct((1, gw), F32), jax.ShapeDtypeStruct((1, gw), F32),
                   jax.ShapeDtypeStruct((G, ch, ch), F32), jax.ShapeDtypeStruct((ch, G), F32)],
        scratch=[pltpu.VMEM((tb, gw), F32), pltpu.VMEM((tb, gw), F32)],
        sem=("arbitrary",))(dyg, zp, b_in, ln_g, ln_b, ws, ws_t, bs_t)


def _dot_01(x, ones_bf16):
    hi = x.astype(BF16)
    r1 = x - hi.astype(F32)
    mid = r1.astype(BF16)
    lo = (r1 - mid.astype(F32)).astype(BF16)
    dot = lambda t: jnp.dot(t, ones_bf16, preferred_element_type=F32)
    return dot(hi) + dot(mid) + dot(lo)


def _log_sigmoid(x):
    return jnp.minimum(x, 0.0) - jnp.log1p(jnp.exp(-jnp.abs(x)))


def _dcum_fwd(f_t, b_col, name):
    H, S = f_t.shape
    tb = _tile(S, 512, LANES)

    def body(f_ref, b_ref, o_ref, carry):
        @pl.when(pl.program_id(0) == 0)
        def _():
            carry[...] = jnp.zeros_like(carry)

        ls = _log_sigmoid(f_ref[...] + b_ref[...])
        r = lax.broadcasted_iota(jnp.int32, (tb, tb), 0)
        c = lax.broadcasted_iota(jnp.int32, (tb, tb), 1)
        upper = (r <= c).astype(BF16)
        o_ref[...] = _dot_01(ls, upper) + carry[...]
        carry[...] += jnp.sum(ls, axis=1, keepdims=True)

    return _call(body, name=name, grid=(S // tb,),
                 in_specs=[pl.BlockSpec((H, tb), lambda i: (0, i)), pl.BlockSpec((H, 1), lambda i: (0, 0))],
                 out_specs=pl.BlockSpec((H, tb), lambda i: (0, i)),
                 out_shape=jax.ShapeDtypeStruct((H, S), F32),
                 scratch=[pltpu.VMEM((H, 1), F32)], sem=("arbitrary",))(f_t, b_col)


def _dcum_bwd(dd_t, f_t, b_col, name):
    H, S = f_t.shape
    tb = _tile(S, 512, LANES)
    nb = S // tb

    def body(dd_ref, f_ref, b_ref, df_ref, db_ref, carry):
        first = pl.program_id(0) == 0

        @pl.when(first)
        def _():
            carry[...] = jnp.zeros_like(carry)

        dd = dd_ref[...]
        r = lax.broadcasted_iota(jnp.int32, (tb, tb), 0)
        c = lax.broadcasted_iota(jnp.int32, (tb, tb), 1)
        lower = (r >= c).astype(BF16)
        rev = _dot_01(dd, lower) + carry[...]
        carry[...] += jnp.sum(dd, axis=1, keepdims=True)
        df = rev * jax.nn.sigmoid(-(f_ref[...] + b_ref[...]))
        df_ref[...] = df
        part = jnp.sum(df, axis=1, keepdims=True)

        @pl.when(first)
        def _():
            db_ref[...] = part

        @pl.when(jnp.logical_not(first))
        def _():
            db_ref[...] += part

    return _call(body, name=name, grid=(nb,),
                 in_specs=[pl.BlockSpec((H, tb), lambda i: (0, nb - 1 - i)), pl.BlockSpec((H, tb), lambda i: (0, nb - 1 - i)),
                           pl.BlockSpec((H, 1), lambda i: (0, 0))],
                 out_specs=[pl.BlockSpec((H, tb), lambda i: (0, nb - 1 - i)), pl.BlockSpec((H, 1), lambda i: (0, 0))],
                 out_shape=[jax.ShapeDtypeStruct((H, S), F32), jax.ShapeDtypeStruct((H, 1), F32)],
                 scratch=[pltpu.VMEM((H, 1), F32)], sem=("arbitrary",))(dd_t, f_t, b_col)


def _attn_tile(S):
    return _tile(S, 512, LANES)


def _causal(t, transposed):
    r = lax.broadcasted_iota(jnp.int32, (t, t), 0)
    c = lax.broadcasted_iota(jnp.int32, (t, t), 1)
    return (r <= c) if transposed else (c <= r)


def _flash_fwd(q, k, v, dc, dr, name):
    H, S, hd = q.shape
    t = _attn_tile(S)
    n = S // t

    def body(q_ref, k_ref, v_ref, dc_ref, dr_ref, o_ref, lse_ref, m_sc, l_sc, acc_sc):
        i, j = pl.program_id(1), pl.program_id(2)

        @pl.when(j == 0)
        def _():
            m_sc[...] = jnp.full_like(m_sc, MASKED)
            l_sc[...] = jnp.zeros_like(l_sc)
            acc_sc[...] = jnp.zeros_like(acc_sc)

        @pl.when(j <= i)
        def _():
            s = lax.dot_general(q_ref[...], k_ref[...], (((1,), (1,)), ((), ())), preferred_element_type=F32)
            s = s + dc_ref[...] - dr_ref[...]
            s = jnp.where(jnp.logical_or(j < i, _causal(t, False)), s, MASKED)
            m_prev = m_sc[...]
            m_new = jnp.maximum(m_prev, jnp.max(s, axis=1, keepdims=True))
            p = jnp.exp(s - m_new)
            alpha = jnp.exp(m_prev - m_new)
            l_sc[...] = alpha * l_sc[...] + jnp.sum(p, axis=1, keepdims=True)
            acc_sc[...] = alpha * acc_sc[...] + jnp.dot(p.astype(BF16), v_ref[...], preferred_element_type=F32)
            m_sc[...] = m_new

        @pl.when(j == n - 1)
        def _():
            o_ref[...] = acc_sc[...] / l_sc[...]
            lse_ref[...] = m_sc[...] + jnp.log(l_sc[...])

    qmap = lambda h, i, j: (h, i, 0)
    kmap = lambda h, i, j: (h, jnp.minimum(j, i), 0)
    return _call(
        body, name=name, grid=(H, n, n),
        in_specs=[pl.BlockSpec((None, t, hd), qmap), pl.BlockSpec((None, t, hd), kmap), pl.BlockSpec((None, t, hd), kmap),
                  pl.BlockSpec((None, t, 1), qmap), pl.BlockSpec((None, 1, t), lambda h, i, j: (h, 0, jnp.minimum(j, i)))],
        out_specs=[pl.BlockSpec((None, t, hd), qmap), pl.BlockSpec((None, t, 1), qmap)],
        out_shape=[jax.ShapeDtypeStruct((H, S, hd), F32), jax.ShapeDtypeStruct((H, S, 1), F32)],
        scratch=[pltpu.VMEM((t, 1), F32), pltpu.VMEM((t, 1), F32), pltpu.VMEM((t, hd), F32)],
        sem=("parallel", "parallel", "arbitrary"))(q, k, v, dc, dr)


def _flash_bwd_q(q, k, v, dc, dr, do, o, lse, name):
    H, S, hd = q.shape
    t = _attn_tile(S)
    n = S // t

    def body(q_ref, k_ref, v_ref, dc_ref, dr_ref, do_ref, o_ref, lse_ref, dq_ref, dl_ref, acc_sc, dl_sc, pdp_sc):
        i, j = pl.program_id(1), pl.program_id(2)

        @pl.when(j == 0)
        def _():
            acc_sc[...] = jnp.zeros_like(acc_sc)
            pdp_sc[...] = jnp.zeros_like(pdp_sc)
            dl_sc[...] = jnp.sum(do_ref[...].astype(F32) * o_ref[...], axis=1, keepdims=True)

        @pl.when(j <= i)
        def _():
            s = lax.dot_general(q_ref[...], k_ref[...], (((1,), (1,)), ((), ())), preferred_element_type=F32)
            s = s + dc_ref[...] - dr_ref[...]
            s = jnp.where(jnp.logical_or(j < i, _causal(t, False)), s, MASKED)
            p = jnp.exp(s - lse_ref[...])
            dp = lax.dot_general(do_ref[...], v_ref[...], (((1,), (1,)), ((), ())), preferred_element_type=F32)
            pdp_sc[...] += jnp.sum(p * dp, axis=1, keepdims=True)
            ds = p * (dp - dl_sc[...])
            acc_sc[...] += jnp.dot(ds.astype(BF16), k_ref[...], preferred_element_type=F32)

        @pl.when(j == n - 1)
        def _():
            dq_ref[...] = acc_sc[...]
            dl_ref[...] = pdp_sc[...]

    qmap = lambda h, i, j: (h, i, 0)
    kmap = lambda h, i, j: (h, jnp.minimum(j, i), 0)
    return _call(
        body, name=name, grid=(H, n, n),
        in_specs=[pl.BlockSpec((None, t, hd), qmap), pl.BlockSpec((None, t, hd), kmap), pl.BlockSpec((None, t, hd), kmap),
                  pl.BlockSpec((None, t, 1), qmap), pl.BlockSpec((None, 1, t), lambda h, i, j: (h, 0, jnp.minimum(j, i))),
                  pl.BlockSpec((None, t, hd), qmap), pl.BlockSpec((None, t, hd), qmap), pl.BlockSpec((None, t, 1), qmap)],
        out_specs=[pl.BlockSpec((None, t, hd), qmap), pl.BlockSpec((None, t, 1), qmap)],
        out_shape=[jax.ShapeDtypeStruct((H, S, hd), F32), jax.ShapeDtypeStruct((H, S, 1), F32)],
        scratch=[pltpu.VMEM((t, hd), F32), pltpu.VMEM((t, 1), F32), pltpu.VMEM((t, 1), F32)],
        sem=("parallel", "parallel", "arbitrary"))(q, k, v, dc, dr, do, o, lse)


def _flash_bwd_kv(q, k, v, dc, dr, do, lse_r, dl_r, name):
    H, S, hd = q.shape
    t = _attn_tile(S)
    n = S // t

    def body(q_ref, k_ref, v_ref, dck_ref, drq_ref, do_ref, lse_ref, dl_ref, dk_ref, dv_ref, dd_ref, dk_sc, dv_sc, dd_sc):
        j, i = pl.program_id(1), pl.program_id(2)

        @pl.when(i == 0)
        def _():
            dk_sc[...] = jnp.zeros_like(dk_sc)
            dv_sc[...] = jnp.zeros_like(dv_sc)
            dd_sc[...] = jnp.zeros_like(dd_sc)

        @pl.when(i >= j)
        def _():
            st = lax.dot_general(k_ref[...], q_ref[...], (((1,), (1,)), ((), ())), preferred_element_type=F32)
            st = st + drq_ref[...] - dck_ref[...]
            st = jnp.where(jnp.logical_or(i > j, _causal(t, True)), st, MASKED)
            pt = jnp.exp(st - lse_ref[...])
            dv_sc[...] += jnp.dot(pt.astype(BF16), do_ref[...], preferred_element_type=F32)
            dpt = lax.dot_general(v_ref[...], do_ref[...], (((1,), (1,)), ((), ())), preferred_element_type=F32)
            dst = pt * (dpt - dl_ref[...])
            dk_sc[...] += jnp.dot(dst.astype(BF16), q_ref[...], preferred_element_type=F32)
            dd_sc[...] -= jnp.sum(dst, axis=1, keepdims=True)

        @pl.when(i == n - 1)
        def _():
            dk_ref[...] = dk_sc[...]
            dv_ref[...] = dv_sc[...]
            dd_ref[...] = dd_sc[...]

    kmap = lambda h, j, i: (h, j, 0)
    qmap = lambda h, j, i: (h, jnp.maximum(i, j), 0)
    qrow = lambda h, j, i: (h, 0, jnp.maximum(i, j))
    return _call(
        body, name=name, grid=(H, n, n),
        in_specs=[pl.BlockSpec((None, t, hd), qmap), pl.BlockSpec((None, t, hd), kmap), pl.BlockSpec((None, t, hd), kmap),
                  pl.BlockSpec((None, t, 1), kmap), pl.BlockSpec((None, 1, t), qrow),
                  pl.BlockSpec((None, t, hd), qmap), pl.BlockSpec((None, 1, t), qrow), pl.BlockSpec((None, 1, t), qrow)],
        out_specs=[pl.BlockSpec((None, t, hd), kmap), pl.BlockSpec((None, t, hd), kmap), pl.BlockSpec((None, t, 1), kmap)],
        out_shape=[jax.ShapeDtypeStruct((H, S, hd), F32), jax.ShapeDtypeStruct((H, S, hd), F32),
                   jax.ShapeDtypeStruct((H, S, 1), F32)],
        scratch=[pltpu.VMEM((t, hd), F32), pltpu.VMEM((t, hd), F32), pltpu.VMEM((t, 1), F32)],
        sem=("parallel", "parallel", "arbitrary"))(q, k, v, dc, dr, do, lse_r, dl_r)


def _offsets(n_bits):
    return [tuple((k >> b) & 1 for b in reversed(range(n_bits))) for k in range(1, 1 << n_bits)]


def _gather8(arrs, name):
    n = len(arrs)
    offs = _offsets(3)

    def body(*refs):
        ins, outs = refs[:n], refs[n:2 * n]
        ssem, rsem, lsem = refs[2 * n:]
        x, y, c = _place()
        me = 4 * x + 2 * y + c
        copies = []
        for a in range(n):
            lc = pltpu.make_async_copy(ins[a], outs[a].at[me], lsem.at[a])
            lc.start()
            copies.append(lc)
            for k, (dx, dy, dcc) in enumerate(offs):
                cp = pltpu.make_async_remote_copy(
                    src_ref=ins[a], dst_ref=outs[a].at[me], send_sem=ssem.at[a, k], recv_sem=rsem.at[a, k],
                    device_id=((x + dx) % 2, (y + dy) % 2, (c + dcc) % 2), device_id_type=MESH)
                cp.start()
                copies.append(cp)
        for cp in copies:
            cp.wait()

    return _call(body, name=name, in_specs=[ANY] * n, out_specs=[ANY] * n,
                 out_shape=[jax.ShapeDtypeStruct((N_DEV,) + a.shape, a.dtype) for a in arrs],
                 scratch=[pltpu.SemaphoreType.DMA((n, 7)), pltpu.SemaphoreType.DMA((n, 7)), pltpu.SemaphoreType.DMA((n,))])(*arrs)


def _chip_gather(arrs, halved, name):
    n = len(arrs)
    offs = _offsets(2)

    def body(*refs):
        ins, outs = refs[:n], refs[n:2 * n]
        ssem, rsem, lsem = refs[2 * n:]
        x, y, c = _place()
        chip = 2 * x + y
        copies = []
        for a in range(n):
            lc = pltpu.make_async_copy(ins[a], outs[a].at[chip], lsem.at[a])
            lc.start()
            copies.append(lc)
            if halved:
                hn = arrs[a].shape[0] // 2
                src = ins[a].at[pl.ds(c * hn, hn)]
                dst = outs[a].at[chip, pl.ds(c * hn, hn)]
            else:
                src, dst = ins[a], outs[a].at[chip]
            for k, (dx, dy) in enumerate(offs):
                cp = pltpu.make_async_remote_copy(
                    src_ref=src, dst_ref=dst, send_sem=ssem.at[a, k], recv_sem=rsem.at[a, k],
                    device_id=((x + dx) % 2, (y + dy) % 2, c), device_id_type=MESH)
                cp.start()
                copies.append(cp)
        for cp in copies:
            cp.wait()

    return _call(body, name=name, in_specs=[ANY] * n, out_specs=[ANY] * n,
                 out_shape=[jax.ShapeDtypeStruct((N_CHIPS,) + a.shape, a.dtype) for a in arrs],
                 scratch=[pltpu.SemaphoreType.DMA((n, 3)), pltpu.SemaphoreType.DMA((n, 3)), pltpu.SemaphoreType.DMA((n,))])(*arrs)


def _sibling_fill(bufs, name):
    n = len(bufs)
    offs = _offsets(2)

    def body(*refs):
        ins, outs = refs[:n], refs[n:2 * n]
        ssem, rsem = refs[2 * n:]
        x, y, c = _place()
        copies = []
        for a in range(n):
            hn = bufs[a].shape[1] // 2
            for k, (dx, dy) in enumerate(offs):
                chip = 2 * ((x + dx) % 2) + (y + dy) % 2
                cp = pltpu.make_async_remote_copy(
                    src_ref=ins[a].at[chip, pl.ds(c * hn, hn)], dst_ref=outs[a].at[chip, pl.ds(c * hn, hn)],
                    send_sem=ssem.at[a, k], recv_sem=rsem.at[a, k],
                    device_id=(x, y, 1 - c), device_id_type=MESH)
                cp.start()
                copies.append(cp)
        for cp in copies:
            cp.wait()

    return _call(body, name=name, in_specs=[ANY] * n, out_specs=[ANY] * n,
                 out_shape=[jax.ShapeDtypeStruct(b.shape, b.dtype) for b in bufs],
                 scratch=[pltpu.SemaphoreType.DMA((n, 3)), pltpu.SemaphoreType.DMA((n, 3))],
                 aliases={a: a for a in range(n)})(*bufs)


def _sibling_pair(arrs, name):
    n = len(arrs)

    def body(*refs):
        ins, outs = refs[:n], refs[n:2 * n]
        ssem, rsem, lsem = refs[2 * n:]
        x, y, c = _place()
        copies = []
        for a in range(n):
            lc = pltpu.make_async_copy(ins[a], outs[a].at[c], lsem.at[a])
            lc.start()
            cp = pltpu.make_async_remote_copy(
                src_ref=ins[a], dst_ref=outs[a].at[c], send_sem=ssem.at[a], recv_sem=rsem.at[a],
                device_id=(x, y, 1 - c), device_id_type=MESH)
            cp.start()
            copies += [lc, cp]
        for cp in copies:
            cp.wait()

    return _call(body, name=name, in_specs=[ANY] * n, out_specs=[ANY] * n,
                 out_shape=[jax.ShapeDtypeStruct((N_CORES,) + a.shape, a.dtype) for a in arrs],
                 scratch=[pltpu.SemaphoreType.DMA((n,)), pltpu.SemaphoreType.DMA((n,)), pltpu.SemaphoreType.DMA((n,))])(*arrs)


def _piece(shape, spec, j, h):
    shard_ax, half_ax = spec
    w = shape[shard_ax] // N_CHIPS
    idx = [slice(None)] * len(shape)
    idx[shard_ax] = pl.ds(j * w, w)
    hn = (w if half_ax == shard_ax else shape[half_ax]) // 2
    assert half_ax != shard_ax
    idx[half_ax] = pl.ds(h * hn, hn)
    return tuple(idx)


def _piece_shape(shape, spec):
    shard_ax, half_ax = spec
    s = list(shape)
    s[shard_ax] //= N_CHIPS
    s[half_ax] //= 2
    return tuple(s)


def _scatter8(arrs, specs, name):
    n = len(arrs)
    targets = [(jx, jy, h) for jx in range(2) for jy in range(2) for h in range(2)]

    def body(*refs):
        ins, outs = refs[:n], refs[n:2 * n]
        ssem, rsem, lsem = refs[2 * n:]
        x, y, c = _place()
        me = 4 * x + 2 * y + c
        for a in range(n):
            for t, (jx, jy, h) in enumerate(targets):
                src = ins[a].at[_piece(arrs[a].shape, specs[a], 2 * jx + jy, h)]
                dst = outs[a].at[me]

                @pl.when(t != me)
                def _():
                    pltpu.make_async_remote_copy(src_ref=src, dst_ref=dst, send_sem=ssem.at[a, t], recv_sem=rsem.at[a, me],
                                                 device_id=(jx, jy, h), device_id_type=MESH).start()

                @pl.when(t == me)
                def _():
                    pltpu.make_async_copy(src, dst, lsem.at[a]).start()
        for a in range(n):
            for t, (jx, jy, h) in enumerate(targets):
                src = ins[a].at[_piece(arrs[a].shape, specs[a], 2 * jx + jy, h)]
                dst = outs[a].at[t]
                cp = pltpu.make_async_remote_copy(src_ref=src, dst_ref=dst, send_sem=ssem.at[a, t], recv_sem=rsem.at[a, t],
                                                  device_id=(jx, jy, h), device_id_type=MESH)

                @pl.when(t != me)
                def _():
                    cp.wait_send()
                    cp.wait_recv()

                @pl.when(t == me)
                def _():
                    pltpu.make_async_copy(src, dst, lsem.at[a]).wait()

    return _call(body, name=name, in_specs=[ANY] * n, out_specs=[ANY] * n,
                 out_shape=[jax.ShapeDtypeStruct((N_DEV,) + _piece_shape(a.shape, s), a.dtype) for a, s in zip(arrs, specs)],
                 scratch=[pltpu.SemaphoreType.DMA((n, 8)), pltpu.SemaphoreType.DMA((n, 8)), pltpu.SemaphoreType.DMA((n,))])(*arrs)


def kernel(x, c, ada_w, ada_b, pre_mix_g, post_mix_g, pre_ffn_g, post_ffn_g, ffn_w_gu, ffn_w_down, a_w_in, a_b_in, a_ln_g, a_ln_b, a_w_s, a_b_s, a_w_out, kv_ada_w, kv_ada_b, kv_norm_g, kv_w, kv_b_f, k_norm_g, b_w_qg, b_q_norm_g, b_w_o, loss_target, m_ada_w, m_ada_b, m_pre_mix_g, m_post_mix_g, m_pre_ffn_g, m_post_ffn_g, m_ffn_w_gu, m_ffn_w_down, m_a_w_in, m_a_b_in, m_a_ln_g, m_a_ln_b, m_a_w_s, m_a_b_s, m_a_w_out, m_kv_ada_w, m_kv_ada_b, m_kv_norm_g, m_kv_w, m_kv_b_f, m_k_norm_g, m_b_w_qg, m_b_q_norm_g, m_b_w_o, v_ada_w, v_ada_b, v_pre_mix_g, v_post_mix_g, v_pre_ffn_g, v_post_ffn_g, v_ffn_w_gu, v_ffn_w_down, v_a_w_in, v_a_b_in, v_a_ln_g, v_a_ln_b, v_a_w_s, v_a_b_s, v_a_w_out, v_kv_ada_w, v_kv_ada_b, v_kv_norm_g, v_kv_w, v_kv_b_f, v_k_norm_g, v_b_w_qg, v_b_q_norm_g, v_b_w_o):
    weights = dict(ada_w=ada_w, ada_b=ada_b, pre_mix_g=pre_mix_g, post_mix_g=post_mix_g, pre_ffn_g=pre_ffn_g,
                   post_ffn_g=post_ffn_g, ffn_w_gu=ffn_w_gu, ffn_w_down=ffn_w_down, a_w_in=a_w_in, a_b_in=a_b_in,
                   a_ln_g=a_ln_g, a_ln_b=a_ln_b, a_w_s=a_w_s, a_b_s=a_b_s, a_w_out=a_w_out, kv_ada_w=kv_ada_w,
                   kv_ada_b=kv_ada_b, kv_norm_g=kv_norm_g, kv_w=kv_w, kv_b_f=kv_b_f, k_norm_g=k_norm_g, b_w_qg=b_w_qg,
                   b_q_norm_g=b_q_norm_g, b_w_o=b_w_o)
    m_in = dict(ada_w=m_ada_w, ada_b=m_ada_b, pre_mix_g=m_pre_mix_g, post_mix_g=m_post_mix_g, pre_ffn_g=m_pre_ffn_g,
                post_ffn_g=m_post_ffn_g, ffn_w_gu=m_ffn_w_gu, ffn_w_down=m_ffn_w_down, a_w_in=m_a_w_in, a_b_in=m_a_b_in,
                a_ln_g=m_a_ln_g, a_ln_b=m_a_ln_b, a_w_s=m_a_w_s, a_b_s=m_a_b_s, a_w_out=m_a_w_out, kv_ada_w=m_kv_ada_w,
                kv_ada_b=m_kv_ada_b, kv_norm_g=m_kv_norm_g, kv_w=m_kv_w, kv_b_f=m_kv_b_f, k_norm_g=m_k_norm_g,
                b_w_qg=m_b_w_qg, b_q_norm_g=m_b_q_norm_g, b_w_o=m_b_w_o)
    v_in = dict(ada_w=v_ada_w, ada_b=v_ada_b, pre_mix_g=v_pre_mix_g, post_mix_g=v_post_mix_g, pre_ffn_g=v_pre_ffn_g,
                post_ffn_g=v_post_ffn_g, ffn_w_gu=v_ffn_w_gu, ffn_w_down=v_ffn_w_down, a_w_in=v_a_w_in, a_b_in=v_a_b_in,
                a_ln_g=v_a_ln_g, a_ln_b=v_a_ln_b, a_w_s=v_a_w_s, a_b_s=v_a_b_s, a_w_out=v_a_w_out, kv_ada_w=v_kv_ada_w,
                kv_ada_b=v_kv_ada_b, kv_norm_g=v_kv_norm_g, kv_w=v_kv_w, kv_b_f=v_kv_b_f, k_norm_g=v_k_norm_g,
                b_w_qg=v_b_w_qg, b_q_norm_g=v_b_q_norm_g, b_w_o=v_b_w_o)
    names = list(weights)

    S, D = x.shape[1], x.shape[2]
    L, NA, NB = ada_w.shape[0], a_w_in.shape[0], b_w_qg.shape[0]
    H = kv_b_f.shape[0]
    hd = D // H
    G, CH = a_w_s.shape[1], a_w_s.shape[2]
    GW = a_w_out.shape[1] * N_CHIPS
    F = ffn_w_down.shape[1] * N_CHIPS
    ada_cols = ada_w.shape[2]
    kvada_cols = kv_ada_w.shape[1]
    kv_cols = kv_w.shape[1]
    kv_pad = -(-(2 * D + H) // LANES) * LANES
    xi, yi, ci = _place()
    chip = 2 * xi + yi
    me = 2 * chip + ci
    x0 = x[0]
    tgt = loss_target[0]
    row = lambda t: t.reshape(1, -1)

    c_all = _gather8([c], "gather_c")[0].reshape(N_DEV, D)
    c_act = _silu_rows(jnp.pad(c_all, ((0, BF16_ROWS - N_DEV), (0, 0))), "silu_c")
    mod_sh = [_mm(c_act, (ada_w, l), "nn", F32, f"mod_proj_{l}") for l in range(L)]
    mod_sh.append(_mm(c_act, kv_ada_w, "nn", F32, "mod_proj_kv"))
    mod_sh = jnp.concatenate(mod_sh, axis=1)
    mod_all, b_in_all, ln_g_all, ln_b_all = _chip_gather([mod_sh, a_b_in, a_ln_g, a_ln_b], False, "gather_mod")
    mine = lax.dynamic_index_in_dim(mod_all, me, axis=1, keepdims=False)
    mod = [jnp.concatenate([mine[j, l * ada_cols:(l + 1) * ada_cols] for j in range(N_CHIPS)]) + ada_b[l] for l in range(L)]
    mod = [[row(t) for t in jnp.split(m_, 6)] for m_ in mod]
    mod_kv = jnp.concatenate([mine[j, L * ada_cols:] for j in range(N_CHIPS)]) + kv_ada_b
    kv_sh, kv_sc = [row(t) for t in jnp.split(mod_kv, 2)]
    cat_chips = lambda t, ax: jnp.concatenate([t[j] for j in range(N_CHIPS)], axis=ax)
    b_in_f = cat_chips(b_in_all, 1)
    ln_g_f, ln_b_f = cat_chips(ln_g_all, 1), cat_chips(ln_b_all, 1)

    big = ["ffn_w_gu", "ffn_w_down", "a_w_in", "a_w_out", "kv_w", "b_w_qg", "b_w_o"]
    gathered = _chip_gather([weights[n].astype(BF16) for n in big], True, "gather_w")
    gathered = dict(zip(big, _sibling_fill(gathered, "fill_w")))
    w_gu = cat_chips(gathered["ffn_w_gu"], 2)
    w_dn = cat_chips(gathered["ffn_w_down"], 1)
    w_in = cat_chips(gathered["a_w_in"], 2)
    w_out = cat_chips(gathered["a_w_out"], 1)
    w_kv = jnp.pad(cat_chips(gathered["kv_w"], 1), ((0, 0), (0, kv_pad - (2 * D + H))))
    w_qg = cat_chips(gathered["b_w_qg"], 2)
    w_o = cat_chips(gathered["b_w_o"], 1)

    causal = jnp.tril(jnp.ones((CH, CH), F32))
    ws_m = [(a_w_s[i] * causal).astype(BF16) for i in range(NA)]
    ws_mt = [jnp.swapaxes(w, 1, 2) for w in ws_m]
    bs_t = [a_b_s[i].T for i in range(NA)]

    heads = lambda t: t.reshape(S, H, hd).transpose(1, 0, 2)
    unheads = lambda t: t.transpose(1, 0, 2).reshape(S, D)

    saved = []
    kv = None
    xc = x0
    for l in range(L):
        sh_m, sc_m, g_m, sh_f, sc_f, g_f = mod[l]
        st = {"x0": xc}
        h1 = _norm_mod_fwd(xc, row(pre_mix_g[l]), sh_m, sc_m, f"pre_mix_{l}")
        st["h1"] = h1
        if l < NA:
            zp = _mm(h1, (w_in, l), "nn", BF16, f"gmlp_in_{l}")
            yg = _gmlp_fwd(zp, row(b_in_f[l]), row(ln_g_f[l]), row(ln_b_f[l]), ws_m[l], bs_t[l], f"gmlp_gate_{l}")
            y = _mm(yg, (w_out, l), "nn", F32, f"gmlp_out_{l}")
            st.update(zp=zp, yg=yg)
        else:
            jb = l - NA
            qg = _mm(h1, (w_qg, jb), "nn", BF16, f"fox_qg_{jb}")
            q_raw = heads(qg[:, :D]).reshape(H * S, hd)
            qn = _head_norm_fwd(q_raw, row(b_q_norm_g[jb]), hd ** -0.5, f"fox_qnorm_{jb}").reshape(H, S, hd)
            o, lse = _flash_fwd(qn, kv["kn"], kv["vb"], kv["dc"], kv["dr"], f"fox_attn_{jb}")
            o_t = unheads(o)
            og = _out_gate_fwd(o_t, qg, f"fox_gate_{jb}")
            y = _mm(og, (w_o, jb), "nn", F32, f"fox_out_{jb}")
            st.update(qg=qg, q_raw=q_raw, qn=qn, o=o, lse=lse, o_t=o_t, og=og)
        st["y"] = y
        x1 = _post_fwd(xc, y, row(post_mix_g[l]), g_m, f"post_mix_{l}")
        st["x1"] = x1
        h2 = _norm_mod_fwd(x1, row(pre_ffn_g[l]), sh_f, sc_f, f"pre_ffn_{l}")
        gu = _mm(h2, (w_gu, l), "nn", BF16, f"ffn_gu_{l}")
        act = _swiglu_fwd(gu, f"ffn_act_{l}")
        y2 = _mm(act, (w_dn, l), "nn", F32, f"ffn_down_{l}")
        xc = _post_fwd(x1, y2, row(post_ffn_g[l]), g_f, f"post_ffn_{l}")
        st.update(h2=h2, gu=gu, act=act, y2=y2)
        saved.append(st)
        if l == NA - 1:
            hk = _norm_mod_fwd(xc, row(kv_norm_g), kv_sh, kv_sc, "kv_pre")
            kvf = _mm(hk, w_kv, "nn", F32, "kv_proj")
            k_raw = heads(kvf[:, :D]).reshape(H * S, hd)
            kn = _head_norm_fwd(k_raw, row(k_norm_g), 1.0, "kv_knorm").reshape(H, S, hd)
            vb = heads(kvf[:, D:2 * D]).astype(BF16)
            f_t = kvf[:, 2 * D:2 * D + H].T
            b_col = kv_b_f.reshape(H, 1)
            dcum = _dcum_fwd(f_t, b_col, "kv_dcum")
            kv = dict(x=xc, hk=hk, k_raw=k_raw, kn=kn, vb=vb, f_t=f_t, b_col=b_col,
                      dc=dcum.reshape(H, S, 1), dr=dcum.reshape(H, 1, S))

    dx, loss_part = _loss_bwd(xc, tgt, "loss")
    loss = lax.psum(loss_part[0, 0], ("x", "y", "c"))

    gl = {n: [None] * weights[n].shape[0] for n in
          ["pre_mix_g", "post_mix_g", "pre_ffn_g", "post_ffn_g", "ffn_w_gu", "ffn_w_down", "a_w_in", "a_b_in", "a_ln_g",
           "a_ln_b", "a_w_s", "a_b_s", "a_w_out", "b_w_qg", "b_q_norm_g", "b_w_o"]}
    dmod = [None] * L
    dkn = dvb = ddc = None
    gkv = {}
    for l in reversed(range(L)):
        st = saved[l]
        sh_m, sc_m, g_m, sh_f, sc_f, g_f = mod[l]
        if l == NA - 1:
            dk_raw, gkv["k_norm_g"] = _head_norm_bwd(dkn.reshape(H * S, hd), kv["k_raw"], row(k_norm_g), 1.0, "kv_knorm_bwd")
            df_t, db_f = _dcum_bwd(ddc.reshape(H, S), kv["f_t"], kv["b_col"], "kv_dcum_bwd")
            dkvf = jnp.concatenate([unheads(dk_raw.reshape(H, S, hd)), unheads(dvb), df_t.T,
                                    jnp.zeros((S, kv_pad - (2 * D + H)), F32)], axis=1).astype(BF16)
            gkv["kv_w"] = _mm(kv["hk"], dkvf, "tn", BF16, "kv_proj_dw")[:, :2 * D + H]
            dhk = _mm(dkvf, w_kv, "nt", F32, "kv_proj_dx")
            dx, gkv["kv_norm_g"], dsh, dsc = _norm_mod_bwd(dx, dhk, kv["x"], row(kv_norm_g), kv_sh, kv_sc, "kv_pre_bwd")
            gkv["kv_b_f"] = db_f.reshape(H)
            dmod_kv = jnp.concatenate([dsh, dsc], axis=1)
        dy2, gl["post_ffn_g"][l], dg_f = _post_bwd(dx, st["y2"], row(post_ffn_g[l]), g_f, f"post_ffn_bwd_{l}")
        gl["ffn_w_down"][l] = _mm(st["act"], dy2, "tn", BF16, f"ffn_down_dw_{l}")
        dact = _mm(dy2, (w_dn, l), "nt", BF16, f"ffn_down_dx_{l}")
        dgu = _swiglu_bwd(dact, st["gu"], f"ffn_act_bwd_{l}")
        gl["ffn_w_gu"][l] = _mm(st["h2"], dgu, "tn", BF16, f"ffn_gu_dw_{l}")
        dh2 = _mm(dgu, (w_gu, l), "nt", F32, f"ffn_gu_dx_{l}")
        dx, gl["pre_ffn_g"][l], dsh_f, dsc_f = _norm_mod_bwd(dx, dh2, st["x1"], row(pre_ffn_g[l]), sh_f, sc_f, f"pre_ffn_bwd_{l}")
        dy, gl["post_mix_g"][l], dg_m = _post_bwd(dx, st["y"], row(post_mix_g[l]), g_m, f"post_mix_bwd_{l}")
        if l < NA:
            gl["a_w_out"][l] = _mm(st["yg"], dy, "tn", BF16, f"gmlp_out_dw_{l}")
            dyg = _mm(dy, (w_out, l), "nt", BF16, f"gmlp_out_dx_{l}")
            dzp, db_in, dlg, dlb, dws, dbs_t = _gmlp_bwd(dyg, st["zp"], row(b_in_f[l]), row(ln_g_f[l]), row(ln_b_f[l]),
                                                           ws_m[l], ws_mt[l], bs_t[l], f"gmlp_gate_bwd_{l}")
            gl["a_b_in"][l], gl["a_ln_g"][l], gl["a_ln_b"][l] = db_in[0], dlg[0], dlb[0]
            gl["a_w_s"][l], gl["a_b_s"][l] = dws * causal, dbs_t.T
            gl["a_w_in"][l] = _mm(st["h1"], dzp, "tn", BF16, f"gmlp_in_dw_{l}")
            dh1 = _mm(dzp, (w_in, l), "nt", F32, f"gmlp_in_dx_{l}")
        else:
            jb = l - NA
            gl["b_w_o"][jb] = _mm(st["og"], dy, "tn", BF16, f"fox_out_dw_{jb}")
            dog = _mm(dy, (w_o, jb), "nt", F32, f"fox_out_dx_{jb}")
            do_t, dgl = _out_gate_bwd(dog, st["o_t"], st["qg"], f"fox_gate_bwd_{jb}")
            do = heads(do_t)
            dqn, delta = _flash_bwd_q(st["qn"], kv["kn"], kv["vb"], kv["dc"], kv["dr"], do, st["o"], st["lse"], f"fox_attn_dq_{jb}")
            dk_j, dv_j, dd_j = _flash_bwd_kv(st["qn"], kv["kn"], kv["vb"], kv["dc"], kv["dr"], do,
                                             st["lse"].reshape(H, 1, S), delta.reshape(H, 1, S), f"fox_attn_dkv_{jb}")
            dkn = dk_j if dkn is None else dkn + dk_j
            dvb = dv_j if dvb is None else dvb + dv_j
            ddc = dd_j if ddc is None else ddc + dd_j
            dq_raw, dgq = _head_norm_bwd(dqn.reshape(H * S, hd), st["q_raw"], row(b_q_norm_g[jb]), hd ** -0.5, f"fox_qnorm_bwd_{jb}")
            gl["b_q_norm_g"][jb] = dgq[0]
            dqg = jnp.concatenate([unheads(dq_raw.reshape(H, S, hd)).astype(BF16), dgl], axis=1)
            gl["b_w_qg"][jb] = _mm(st["h1"], dqg, "tn", BF16, f"fox_qg_dw_{jb}")
            dh1 = _mm(dqg, (w_qg, jb), "nt", F32, f"fox_qg_dx_{jb}")
        dx, gl["pre_mix_g"][l], dsh_m, dsc_m = _norm_mod_bwd(dx, dh1, st["x0"], row(pre_mix_g[l]), sh_m, sc_m, f"pre_mix_bwd_{l}")
        dmod[l] = jnp.concatenate([dsh_m, dsc_m, dg_m, dsh_f, dsc_f, dg_f], axis=1)
    grad_x = dx[None]

    stack = lambda n: jnp.stack([t.reshape(weights[n].shape[1:]) for t in gl[n]])
    small = {"dmod": jnp.concatenate(dmod, axis=1), "dmod_kv": dmod_kv}
    for n in ["pre_mix_g", "post_mix_g", "pre_ffn_g", "post_ffn_g", "a_w_s", "a_b_s", "b_q_norm_g"]:
        small[n] = stack(n)
    for n in ["a_b_in", "a_ln_g", "a_ln_b"]:
        small[n] = jnp.stack(gl[n])
    for n in ["kv_norm_g", "kv_b_f", "k_norm_g"]:
        small[n] = gkv[n]
    sizes = {n: t.size for n, t in small.items()}
    flat = jnp.concatenate([t.reshape(-1).astype(F32) for t in small.values()])
    rows_small = -(-flat.size // (LANES * BF16_ROWS)) * BF16_ROWS
    flat = jnp.pad(flat, (0, rows_small * LANES - flat.size)).reshape(rows_small, LANES)
    flat_all = _gather8([flat], "gather_small")[0]
    flat_sum = _sum_slots(flat_all, "sum_small").reshape(-1)
    offs, o_ = {}, 0
    for n, sz in sizes.items():
        offs[n] = o_
        o_ += sz
    take = lambda n, shape: flat_sum[offs[n]:offs[n] + sizes[n]].reshape(shape)
    dmod_rows = flat_all.reshape(N_DEV, -1)[:, offs["dmod"]:offs["dmod"] + sizes["dmod"] + sizes["dmod_kv"]]
    dmod_rows = jnp.pad(dmod_rows, ((0, BF16_ROWS - N_DEV), (0, 0)))

    grads = {}
    grads["ada_b"] = take("dmod", (L, 6 * D))
    grads["kv_ada_b"] = take("dmod_kv", (2 * D,))
    for n in ["pre_mix_g", "post_mix_g", "pre_ffn_g", "post_ffn_g", "a_w_s", "a_b_s", "b_q_norm_g", "kv_norm_g", "kv_b_f", "k_norm_g"]:
        grads[n] = take(n, weights[n].shape)
    for n in ["a_b_in", "a_ln_g", "a_ln_b"]:
        full = take(n, small[n].shape)
        w = weights[n].shape[1]
        grads[n] = lax.dynamic_slice_in_dim(full, chip * w, w, axis=1)
    ada_g = []
    for l in range(L):
        cols = lax.dynamic_slice_in_dim(dmod_rows[:, l * 6 * D:(l + 1) * 6 * D], chip * ada_cols, ada_cols, axis=1)
        ada_g.append(_mm(c_act, cols, "tn", F32, f"mod_proj_dw_{l}"))
    grads["ada_w"] = jnp.stack(ada_g)
    cols = lax.dynamic_slice_in_dim(dmod_rows[:, L * 6 * D:], chip * kvada_cols, kvada_cols, axis=1)
    grads["kv_ada_w"] = _mm(c_act, cols, "tn", F32, "mod_proj_kv_dw")

    specs = {"ffn_w_gu": (2, 0), "ffn_w_down": (1, 0), "a_w_in": (2, 0), "a_w_out": (1, 0), "kv_w": (0, 1),
             "b_w_qg": (2, 0), "b_w_o": (1, 0)}
    full_g = {n: jnp.stack(gl[n]) for n in big if n != "kv_w"}
    full_g["kv_w"] = gkv["kv_w"].reshape(D, N_CHIPS, kv_cols).transpose(1, 0, 2)
    recv = _scatter8([full_g[n] for n in big], [specs[n] for n in big], "scatter_g")
    halves = [_sum_slots(r, f"sum_g_{n}") for n, r in zip(big, recv)]
    pairs = _sibling_pair(halves, "pair_g")
    for n, p in zip(big, pairs):
        grads[n] = p.reshape(weights[n].shape)

    outs_d, outs_m, outs_v = {}, {}, {}
    for n in names:
        w2 = weights[n] if weights[n].ndim > 1 else weights[n].reshape(1, -1)
        shp = w2.shape
        d_, m_, v_ = _adamw(w2, grads[n].reshape(shp), m_in[n].reshape(shp), v_in[n].reshape(shp), f"adamw_{n}")
        outs_d[n], outs_m[n], outs_v[n] = (t.reshape(weights[n].shape) for t in (d_, m_, v_))
    return (loss, grad_x, *[grads[n] for n in names], *[outs_d[n] for n in names],
            *[outs_m[n] for n in names], *[outs_v[n] for n in names])
```

```python
import functools

import jax
import jax.numpy as jnp
from jax import lax
from jax.experimental import pallas as pl
from jax.experimental.pallas import tpu as pltpu

F32 = jnp.float32
BF16 = jnp.bfloat16
MESH = pl.DeviceIdType.MESH
NORM_EPS = 1e-6
MASKED = -1e30
LANES = 128
BF16_ROWS = 16
ROW_BLOCK_BYTES = 6 << 20
ADAM_LR, ADAM_B1, ADAM_B2, ADAM_EPS, ADAM_WD, ADAM_STEP = 0.001, 0.9, 0.999, 1e-08, 0.01, 10
N_CHIPS, N_CORES, N_DEV = 4, 2, 8
ATTN_HEADS_PER_STEP = 2
ANY = pl.BlockSpec(memory_space=pl.ANY)


def _tile(n, cap, quantum):
    best = None
    d = quantum
    while d <= min(n, cap):
        if n % d == 0:
            best = d
        d += quantum
    return n if best is None else best


def _call(body, *, name, out_shape, grid=(), in_specs=None, out_specs=None, scratch=(), sem=None, aliases=None):
    params = {} if sem is None else {"dimension_semantics": sem}
    return pl.pallas_call(
        body, name=name, grid=grid, in_specs=in_specs, out_specs=out_specs, out_shape=out_shape,
        scratch_shapes=list(scratch), input_output_aliases=aliases or {},
        compiler_params=pltpu.CompilerParams(**params))


def _call_prefetch(body, *, name, out_shape, grid, n_prefetch, in_specs, out_specs, scratch, sem):
    spec = pltpu.PrefetchScalarGridSpec(num_scalar_prefetch=n_prefetch, grid=grid, in_specs=in_specs,
                                        out_specs=out_specs, scratch_shapes=list(scratch))
    return pl.pallas_call(
        body, name=name, grid_spec=spec, out_shape=out_shape,
        compiler_params=pltpu.CompilerParams(dimension_semantics=sem))


def _place():
    x, y, c = lax.axis_index("x"), lax.axis_index("y"), lax.axis_index("c")
    return x, y, c


def _mm(a, b, mode, out_dtype, name):
    b_arr, b_idx = b if isinstance(b, tuple) else (b, None)
    bs = b_arr.shape[-2:]
    if mode == "nn":
        (M, K), (K2, N) = a.shape, bs
        dims = (((1,), (0,)), ((), ()))
    elif mode == "nt":
        (M, K), (N, K2) = a.shape, bs
        dims = (((1,), (1,)), ((), ()))
    else:
        (K, M), (K2, N) = a.shape, bs
        dims = (((0,), (0,)), ((), ()))
    assert K == K2, (name, a.shape, b_arr.shape)
    if mode == "tn":
        tm = _tile(M, 1408, LANES)
        tk = _tile(K, 1024, BF16_ROWS)
    else:
        tm = _tile(M, 512, BF16_ROWS)
        tk = K if K <= 2816 else _tile(K, 2816, LANES)
    tn = _tile(N, 512, LANES)
    if tn < 256:
        tn = N
        tm = _tile(M, 512, LANES if mode == "tn" else BF16_ROWS)
    nk = K // tk
    grid = (M // tm, N // tn, nk)

    if mode == "tn":
        a_spec = pl.BlockSpec((tk, tm), lambda i, j, k: (k, i))
    else:
        a_spec = pl.BlockSpec((tm, tk), lambda i, j, k: (i, k))
    if mode == "nt":
        b_blk, b_map = (tn, tk), (lambda i, j, k: (j, k))
    else:
        b_blk, b_map = (tk, tn), (lambda i, j, k: (k, j))
    if b_idx is None:
        b_spec = pl.BlockSpec(b_blk, b_map)
    else:
        b_spec = pl.BlockSpec((None,) + b_blk, lambda i, j, k: (b_idx,) + b_map(i, j, k))

    def body(a_ref, b_ref, o_ref, *acc):
        r = lax.dot_general(a_ref[...].astype(BF16), b_ref[...].astype(BF16), dims, preferred_element_type=F32)
        if nk == 1:
            o_ref[...] = r.astype(o_ref.dtype)
        else:
            k = pl.program_id(2)

            @pl.when(k == 0)
            def _():
                acc[0][...] = r

            @pl.when(k > 0)
            def _():
                acc[0][...] += r

            @pl.when(k == nk - 1)
            def _():
                o_ref[...] = acc[0][...].astype(o_ref.dtype)

    return _call(
        body, name=name, grid=grid, in_specs=[a_spec, b_spec],
        out_specs=pl.BlockSpec((tm, tn), lambda i, j, k: (i, j)),
        out_shape=jax.ShapeDtypeStruct((M, N), out_dtype),
        scratch=[pltpu.VMEM((tm, tn), F32)] if nk > 1 else [],
        sem=("parallel", "parallel", "arbitrary"))(a, b_arr)


def _rowwise(fn, rows, pars, outs, pouts, name):
    R = rows[0].shape[0]
    row_bytes = 4 * (sum(max(r.shape[1], LANES) for r in rows) + sum(max(c, LANES) for c, _ in outs))
    tb = _tile(R, max(BF16_ROWS, ROW_BLOCK_BYTES // row_bytes), BF16_ROWS)
    nr, npar, no = len(rows), len(pars), len(outs)

    def body(*refs):
        r_in, p_in = refs[:nr], refs[nr:nr + npar]
        r_out, p_out = refs[nr + npar:nr + npar + no], refs[nr + npar + no:]
        ro, po = fn([r[...] for r in r_in], [p[...] for p in p_in])
        for ref, val in zip(r_out, ro):
            if isinstance(val, (tuple, list)):
                off = 0
                for piece in val:
                    w = piece.shape[1]
                    ref[:, off:off + w] = piece.astype(ref.dtype)
                    off += w
            else:
                ref[...] = val.astype(ref.dtype)
        if p_out:
            first = pl.program_id(0) == 0

            @pl.when(first)
            def _():
                for ref, val in zip(p_out, po):
                    ref[...] = val

            @pl.when(jnp.logical_not(first))
            def _():
                for ref, val in zip(p_out, po):
                    ref[...] += val

    res = _call(
        body, name=name, grid=(R // tb,),
        in_specs=[pl.BlockSpec((tb, r.shape[1]), lambda i: (i, 0)) for r in rows]
        + [pl.BlockSpec(p.shape, lambda i: (0, 0)) for p in pars],
        out_specs=[pl.BlockSpec((tb, c), lambda i: (i, 0)) for c, _ in outs]
        + [pl.BlockSpec(s, lambda i: (0, 0)) for s in pouts],
        out_shape=[jax.ShapeDtypeStruct((R, c), dt) for c, dt in outs]
        + [jax.ShapeDtypeStruct(s, F32) for s in pouts],
        sem=("arbitrary",) if pouts else ("parallel",))(*rows, *pars)
    return list(res)


def _rms(x, g):
    return x * lax.rsqrt(jnp.mean(x * x, axis=-1, keepdims=True) + NORM_EPS) * g


def _norm_mod(x, g, sh, sc):
    return _rms(x, g) * (1.0 + sc) + sh


def _gated_post(y, g, gate):
    return gate * _rms(y, g)


def _norm_mod_fwd(x, g, sh, sc, name):
    return _rowwise(lambda r, p: ([_norm_mod(r[0], *p)], []), [x], [g, sh, sc], [(x.shape[1], BF16)], [], name)[0]


def _norm_mod_bwd(dxo, dh, x, g, sh, sc, name):
    def fn(r, p):
        _, vjp = jax.vjp(_norm_mod, r[2], *p)
        dx, dg, dsh, dsc = vjp(r[1].astype(F32))
        return [r[0] + dx], [dg, dsh, dsc]
    c = x.shape[1]
    return _rowwise(fn, [dxo, dh, x], [g, sh, sc], [(c, F32)], [(1, c)] * 3, name)


def _post_fwd(x, y, g, gate, name):
    return _rowwise(lambda r, p: ([r[0] + _gated_post(r[1].astype(F32), *p)], []), [x, y], [g, gate],
                    [(x.shape[1], F32)], [], name)[0]


def _post_bwd(dxo, y, g, gate, name):
    def fn(r, p):
        _, vjp = jax.vjp(_gated_post, r[1].astype(F32), *p)
        dy, dg, dgate = vjp(r[0])
        return [dy], [dg, dgate]
    c = y.shape[1]
    return _rowwise(fn, [dxo, y], [g, gate], [(c, BF16)], [(1, c)] * 2, name)


def _swiglu(g, u):
    return jax.nn.silu(g) * u


def _swiglu_fwd(gu, name):
    f = gu.shape[1] // 2
    return _rowwise(lambda r, p: ([_swiglu(r[0][:, :f].astype(F32), r[0][:, f:].astype(F32))], []), [gu], [],
                    [(f, BF16)], [], name)[0]


def _swiglu_bwd(da, gu, name):
    f = gu.shape[1] // 2

    def fn(r, p):
        _, vjp = jax.vjp(_swiglu, r[1][:, :f].astype(F32), r[1][:, f:].astype(F32))
        return [vjp(r[0].astype(F32))], []
    return _rowwise(fn, [da, gu], [], [(2 * f, BF16)], [], name)[0]


def _silu_rows(c, name):
    return _rowwise(lambda r, p: ([jax.nn.silu(r[0])], []), [c], [], [(c.shape[1], F32)], [], name)[0]


def _head_norm(x, g, scale):
    return _rms(x, g) * scale


def _head_norm_fwd(x, g, scale, name):
    return _rowwise(lambda r, p: ([_head_norm(r[0].astype(F32), p[0], scale)], []), [x], [g],
                    [(x.shape[1], BF16)], [], name)[0]


def _head_norm_bwd(dy, x, g, scale, name):
    def fn(r, p):
        _, vjp = jax.vjp(lambda t, gg: _head_norm(t, gg, scale), r[1].astype(F32), p[0])
        dx, dg = vjp(r[0])
        return [dx], [dg]
    c = x.shape[1]
    return _rowwise(fn, [dy, x], [g], [(c, F32)], [(1, c)], name)


def _out_gate_fwd(o, qg, name):
    d = o.shape[1]
    return _rowwise(lambda r, p: ([r[0] * jax.nn.sigmoid(r[1][:, d:].astype(F32))], []), [o, qg], [],
                    [(d, BF16)], [], name)[0]


def _out_gate_bwd(dog, o, qg, name):
    d = o.shape[1]

    def fn(r, p):
        _, vjp = jax.vjp(lambda oo, gl: oo * jax.nn.sigmoid(gl), r[1], r[2][:, d:].astype(F32))
        do, dgl = vjp(r[0])
        return [do, dgl], []
    return _rowwise(fn, [dog, o, qg], [], [(d, BF16), (d, BF16)], [], name)


def _loss_bwd(y, tgt, name):
    n = y.shape[1]

    def fn(r, p):
        e = r[0] - r[1]
        part = jnp.sum(jnp.sum(e * e, axis=1, keepdims=True), axis=0, keepdims=True) * (0.5 / n)
        return [e * (1.0 / n)], [part]
    return _rowwise(fn, [y, tgt], [], [(n, F32)], [(1, 1)], name)


def _adamw(w, g, m, v, name):
    shape = w.shape
    c = shape[-1]
    flat = [t.reshape(-1, c) for t in (w, g, m, v)]

    def fn(r, p):
        w_, g_, m_, v_ = r
        m2 = ADAM_B1 * m_ + (1.0 - ADAM_B1) * g_
        v2 = ADAM_B2 * v_ + (1.0 - ADAM_B2) * (g_ * g_)
        m_hat = m2 / (1.0 - ADAM_B1 ** ADAM_STEP)
        v_hat = v2 / (1.0 - ADAM_B2 ** ADAM_STEP)
        delta = -ADAM_LR * (m_hat / (jnp.sqrt(v_hat) + ADAM_EPS) + ADAM_WD * w_)
        return [delta, m2, v2], []
    res = _rowwise(fn, flat, [], [(c, F32)] * 3, [], name)
    return [t.reshape(shape) for t in res]


def _sum_slots(recv, name):
    n = recv.shape[0]
    shape = recv.shape[1:]
    c = shape[-1]
    r3 = recv.reshape(n, -1, c)
    rows = r3.shape[1]
    tb = _tile(rows, max(BF16_ROWS, ROW_BLOCK_BYTES // (4 * c * (n + 1))), BF16_ROWS)

    def body(r_ref, o_ref):
        acc = r_ref[0].astype(F32)
        for s in range(1, n):
            acc = acc + r_ref[s].astype(F32)
        o_ref[...] = acc

    out = _call(body, name=name, grid=(rows // tb,),
                in_specs=[pl.BlockSpec((n, tb, c), lambda i: (0, i, 0))],
                out_specs=pl.BlockSpec((tb, c), lambda i: (i, 0)),
                out_shape=jax.ShapeDtypeStruct((rows, c), F32), sem=("parallel",))(r3)
    return out.reshape(shape)


def _gmlp_pre(zu, zv, b_u, b_v, ln_g, ln_b):
    u = jax.nn.gelu(zu + b_u, approximate=True)
    v = jax.nn.gelu(zv + b_v, approximate=True)
    xc = v - jnp.mean(v, axis=-1, keepdims=True)
    vn = xc * lax.rsqrt(jnp.mean(xc * xc, axis=-1, keepdims=True) + NORM_EPS) * ln_g + ln_b
    return u, vn


def _gmlp_fwd(zp, b_in, ln_g, ln_b, ws, bs_t, name):
    S, gw2 = zp.shape
    gw = gw2 // 2
    G, ch, _ = ws.shape
    gd = gw // G
    tb = 2 * ch

    def body(zp_ref, bin_ref, lg_ref, lb_ref, ws_ref, bs_ref, o_ref):
        u, vn = _gmlp_pre(zp_ref[:, :gw].astype(F32), zp_ref[:, gw:].astype(F32), bin_ref[:, :gw], bin_ref[:, gw:],
                          lg_ref[...], lb_ref[...])
        vnb = vn.astype(BF16)
        for c in range(tb // ch):
            for g in range(G):
                rs, cs = slice(c * ch, (c + 1) * ch), slice(g * gd, (g + 1) * gd)
                vv = jnp.dot(ws_ref[g], vnb[rs, cs], preferred_element_type=F32) + bs_ref[:, g:g + 1]
                o_ref[rs, cs] = (u[rs, cs] * vv).astype(o_ref.dtype)

    full = lambda a: pl.BlockSpec(a.shape, lambda i: (0,) * a.ndim)
    return _call(body, name=name, grid=(S // tb,),
                 in_specs=[pl.BlockSpec((tb, gw2), lambda i: (i, 0)), full(b_in), full(ln_g), full(ln_b), full(ws), full(bs_t)],
                 out_specs=pl.BlockSpec((tb, gw), lambda i: (i, 0)),
                 out_shape=jax.ShapeDtypeStruct((S, gw), BF16), sem=("parallel",))(zp, b_in, ln_g, ln_b, ws, bs_t)


def _gmlp_bwd(dyg, zp, b_in, ln_g, ln_b, ws, ws_t, bs_t, name):
    S, gw2 = zp.shape
    gw = gw2 // 2
    G, ch, _ = ws.shape
    gd = gw // G
    tb = 2 * ch

    def body(dy_ref, zp_ref, bin_ref, lg_ref, lb_ref, ws_ref, wst_ref, bs_ref,
             dzp_ref, dbin_ref, dlg_ref, dlb_ref, dws_ref, dbs_ref, du_sc, dvn_sc):
        (u, vn), vjp = jax.vjp(_gmlp_pre, zp_ref[:, :gw].astype(F32), zp_ref[:, gw:].astype(F32), bin_ref[:, :gw],
                               bin_ref[:, gw:], lg_ref[...], lb_ref[...])
        vnb = vn.astype(BF16)
        first = pl.program_id(0) == 0

        @pl.when(first)
        def _():
            dws_ref[...] = jnp.zeros_like(dws_ref)

        lane = lax.broadcasted_iota(jnp.int32, (ch, G), 1)
        dbs = jnp.zeros((ch, G), F32)
        for g in range(G):
            cs = slice(g * gd, (g + 1) * gd)
            dws_g = jnp.zeros((ch, ch), F32)
            col = jnp.zeros((ch, 1), F32)
            for c in range(tb // ch):
                rs = slice(c * ch, (c + 1) * ch)
                vnp = vnb[rs, cs]
                vv = jnp.dot(ws_ref[g], vnp, preferred_element_type=F32) + bs_ref[:, g:g + 1]
                dy = dy_ref[rs, cs].astype(F32)
                du_sc[rs, cs] = dy * vv
                dvv = dy * u[rs, cs]
                dvvb = dvv.astype(BF16)
                dvn_sc[rs, cs] = jnp.dot(wst_ref[g], dvvb, preferred_element_type=F32)
                dws_g = dws_g + lax.dot_general(dvvb, vnp, (((1,), (1,)), ((), ())), preferred_element_type=F32)
                col = col + jnp.sum(dvv, axis=1, keepdims=True)
            dws_ref[g] += dws_g
            dbs = jnp.where(lane == g, col, dbs)
        dzu, dzv, dbu, dbv, dlg, dlb = vjp((du_sc[...], dvn_sc[...]))
        dzp_ref[:, :gw] = dzu.astype(dzp_ref.dtype)
        dzp_ref[:, gw:] = dzv.astype(dzp_ref.dtype)

        @pl.when(first)
        def _():
            dbin_ref[:, :gw] = dbu
            dbin_ref[:, gw:] = dbv
            dlg_ref[...] = dlg
            dlb_ref[...] = dlb
            dbs_ref[...] = dbs

        @pl.when(jnp.logical_not(first))
        def _():
            dbin_ref[:, :gw] += dbu
            dbin_ref[:, gw:] += dbv
            dlg_ref[...] += dlg
            dlb_ref[...] += dlb
            dbs_ref[...] += dbs

    full = lambda a: pl.BlockSpec(a.shape, lambda i: (0,) * a.ndim)
    fshape = lambda s: pl.BlockSpec(s, lambda i: (0,) * len(s))
    return _call(
        body, name=name, grid=(S // tb,),
        in_specs=[pl.BlockSpec((tb, gw), lambda i: (i, 0)), pl.BlockSpec((tb, gw2), lambda i: (i, 0)),
                  full(b_in), full(ln_g), full(ln_b), full(ws), full(ws_t), full(bs_t)],
        out_specs=[pl.BlockSpec((tb, gw2), lambda i: (i, 0)), fshape((1, gw2)), fshape((1, gw)), fshape((1, gw)),
                   fshape((G, ch, ch)), fshape((ch, G))],
        out_shape=[jax.ShapeDtypeStruct((S, gw2), BF16), jax.ShapeDtypeStruct((1, gw2), F32),
                   jax.ShapeDtypeStruct((1, gw), F32), jax.ShapeDtypeStruct((1, gw), F32),
                   jax.ShapeDtypeStruct((G, ch, ch), F32), jax.ShapeDtypeStruct((ch, G), F32)],
        scratch=[pltpu.VMEM((tb, gw), F32), pltpu.VMEM((tb, gw), F32)],
        sem=("arbitrary",))(dyg, zp, b_in, ln_g, ln_b, ws, ws_t, bs_t)


def _dot_01(x, ones_bf16):
    hi = x.astype(BF16)
    r1 = x - hi.astype(F32)
    mid = r1.astype(BF16)
    lo = (r1 - mid.astype(F32)).astype(BF16)
    dot = lambda t: jnp.dot(t, ones_bf16, preferred_element_type=F32)
    return dot(hi) + dot(mid) + dot(lo)


def _log_sigmoid(x):
    return jnp.minimum(x, 0.0) - jnp.log1p(jnp.exp(-jnp.abs(x)))


def _dcum_fwd(f_t, b_col, name):
    H, S = f_t.shape
    tb = _tile(S, 512, LANES)

    def body(f_ref, b_ref, o_ref, carry):
        @pl.when(pl.program_id(0) == 0)
        def _():
            carry[...] = jnp.zeros_like(carry)

        ls = _log_sigmoid(f_ref[...] + b_ref[...])
        r = lax.broadcasted_iota(jnp.int32, (tb, tb), 0)
        c = lax.broadcasted_iota(jnp.int32, (tb, tb), 1)
        upper = (r <= c).astype(BF16)
        o_ref[...] = _dot_01(ls, upper) + carry[...]
        carry[...] += jnp.sum(ls, axis=1, keepdims=True)

    return _call(body, name=name, grid=(S // tb,),
                 in_specs=[pl.BlockSpec((H, tb), lambda i: (0, i)), pl.BlockSpec((H, 1), lambda i: (0, 0))],
                 out_specs=pl.BlockSpec((H, tb), lambda i: (0, i)),
                 out_shape=jax.ShapeDtypeStruct((H, S), F32),
                 scratch=[pltpu.VMEM((H, 1), F32)], sem=("arbitrary",))(f_t, b_col)


def _dcum_bwd(dd_t, f_t, b_col, name):
    H, S = f_t.shape
    tb = _tile(S, 512, LANES)
    nb = S // tb

    def body(dd_ref, f_ref, b_ref, df_ref, db_ref, carry):
        first = pl.program_id(0) == 0

        @pl.when(first)
        def _():
            carry[...] = jnp.zeros_like(carry)

        dd = dd_ref[...]
        r = lax.broadcasted_iota(jnp.int32, (tb, tb), 0)
        c = lax.broadcasted_iota(jnp.int32, (tb, tb), 1)
        lower = (r >= c).astype(BF16)
        rev = _dot_01(dd, lower) + carry[...]
        carry[...] += jnp.sum(dd, axis=1, keepdims=True)
        df = rev * jax.nn.sigmoid(-(f_ref[...] + b_ref[...]))
        df_ref[...] = df
        part = jnp.sum(df, axis=1, keepdims=True)

        @pl.when(first)
        def _():
            db_ref[...] = part

        @pl.when(jnp.logical_not(first))
        def _():
            db_ref[...] += part

    return _call(body, name=name, grid=(nb,),
                 in_specs=[pl.BlockSpec((H, tb), lambda i: (0, nb - 1 - i)), pl.BlockSpec((H, tb), lambda i: (0, nb - 1 - i)),
                           pl.BlockSpec((H, 1), lambda i: (0, 0))],
                 out_specs=[pl.BlockSpec((H, tb), lambda i: (0, nb - 1 - i)), pl.BlockSpec((H, 1), lambda i: (0, 0))],
                 out_shape=[jax.ShapeDtypeStruct((H, S), F32), jax.ShapeDtypeStruct((H, 1), F32)],
                 scratch=[pltpu.VMEM((H, 1), F32)], sem=("arbitrary",))(dd_t, f_t, b_col)


def _attn_tile(S):
    return _tile(S, 512, LANES)


def _causal(t, transposed):
    r = lax.broadcasted_iota(jnp.int32, (t, t), 0)
    c = lax.broadcasted_iota(jnp.int32, (t, t), 1)
    return (r <= c) if transposed else (c <= r)


def _tri_pairs(n, key_major):
    if key_major:
        pairs = [(i, j) for j in range(n) for i in range(j, n)]
    else:
        pairs = [(i, j) for i in range(n) for j in range(i + 1)]
    return jnp.asarray([p[0] for p in pairs], jnp.int32), jnp.asarray([p[1] for p in pairs], jnp.int32)


def _split3(x):
    hi = lax.reduce_precision(x, 8, 7)
    r = x - hi
    mid = lax.reduce_precision(r, 8, 7)
    lo = lax.reduce_precision(r - mid, 8, 7)
    return hi.astype(BF16), mid.astype(BF16), lo.astype(BF16)


def _augment(xn, dcum, query):
    H, S, hd = xn.shape
    parts = [p[..., None] for p in _split3(dcum)]
    ones = [jnp.ones((H, S, 3), BF16)]
    extra = parts + ones if query else ones + [-p for p in parts]
    pad = [jnp.zeros((H, S, LANES - hd - 6), BF16)]
    return jnp.concatenate([xn] + extra + pad, axis=-1)


def _flash_fwd(qa, ka, va, hd, name):
    H, S, da = qa.shape
    t = _attn_tile(S)
    hb = ATTN_HEADS_PER_STEP
    it, jt = _tri_pairs(S // t, False)

    def body(it_ref, jt_ref, q_ref, k_ref, v_ref, o_ref, lse_ref, m_sc, acc_sc):
        i, j = it_ref[pl.program_id(1)], jt_ref[pl.program_id(1)]

        @pl.when(j == 0)
        def _():
            m_sc[...] = jnp.full_like(m_sc, MASKED)
            acc_sc[...] = jnp.zeros_like(acc_sc)

        def step(diag):
            for h in range(hb):
                s = lax.dot_general(q_ref[h], k_ref[h], (((1,), (1,)), ((), ())), preferred_element_type=F32)
                if diag:
                    s = jnp.where(_causal(t, False), s, MASKED)
                m_prev = m_sc[h]
                m_new = jnp.maximum(m_prev, jnp.max(s, axis=1, keepdims=True))
                p = jnp.exp(s - m_new).astype(BF16)
                pv = jnp.dot(p, v_ref[h], preferred_element_type=F32)
                acc_sc[h] = jnp.exp(m_prev - m_new) * acc_sc[h] + pv
                m_sc[h] = m_new

        @pl.when(j < i)
        def _():
            step(False)

        @pl.when(j == i)
        def _():
            step(True)
            for h in range(hb):
                acc = acc_sc[h]
                l = acc[:, hd:hd + 1]
                o_ref[h] = acc[:, :hd] / l
                lse_ref[h] = m_sc[h] + jnp.log(l)

    qmap = lambda h, p, it_, jt_: (h, it_[p], 0)
    kmap = lambda h, p, it_, jt_: (h, jt_[p], 0)
    return _call_prefetch(
        body, name=name, grid=(H // hb, it.shape[0]), n_prefetch=2,
        in_specs=[pl.BlockSpec((hb, t, da), qmap), pl.BlockSpec((hb, t, da), kmap), pl.BlockSpec((hb, t, da), kmap)],
        out_specs=[pl.BlockSpec((hb, t, hd), qmap), pl.BlockSpec((hb, t, 1), qmap)],
        out_shape=[jax.ShapeDtypeStruct((H, S, hd), F32), jax.ShapeDtypeStruct((H, S, 1), F32)],
        scratch=[pltpu.VMEM((hb, t, 1), F32), pltpu.VMEM((hb, t, da), F32)],
        sem=("parallel", "arbitrary"))(it, jt, qa, ka, va)


def _flash_bwd_q(qa, ka, v, do, o, lse, name):
    H, S, hd = v.shape
    da = qa.shape[2]
    t = _attn_tile(S)
    hb = ATTN_HEADS_PER_STEP
    it, jt = _tri_pairs(S // t, False)

    def body(it_ref, jt_ref, q_ref, k_ref, v_ref, do_ref, o_ref, lse_ref, dq_ref, dl_ref, acc_sc, dl_sc):
        i, j = it_ref[pl.program_id(1)], jt_ref[pl.program_id(1)]

        @pl.when(j == 0)
        def _():
            acc_sc[...] = jnp.zeros_like(acc_sc)
            for h in range(hb):
                dl_sc[h] = jnp.sum(do_ref[h].astype(F32) * o_ref[h], axis=1, keepdims=True)

        def step(diag):
            for h in range(hb):
                s = lax.dot_general(q_ref[h], k_ref[h], (((1,), (1,)), ((), ())), preferred_element_type=F32)
                if diag:
                    s = jnp.where(_causal(t, False), s, MASKED)
                p = jnp.exp(s - lse_ref[h])
                dp = lax.dot_general(do_ref[h], v_ref[h], (((1,), (1,)), ((), ())), preferred_element_type=F32)
                ds = p * (dp - dl_sc[h])
                acc_sc[h] += jnp.dot(ds.astype(BF16), k_ref[h], preferred_element_type=F32)

        @pl.when(j < i)
        def _():
            step(False)

        @pl.when(j == i)
        def _():
            step(True)
            dq_ref[...] = acc_sc[...]
            dl_ref[...] = dl_sc[...]

    qmap = lambda h, p, it_, jt_: (h, it_[p], 0)
    kmap = lambda h, p, it_, jt_: (h, jt_[p], 0)
    return _call_prefetch(
        body, name=name, grid=(H // hb, it.shape[0]), n_prefetch=2,
        in_specs=[pl.BlockSpec((hb, t, da), qmap), pl.BlockSpec((hb, t, da), kmap), pl.BlockSpec((hb, t, hd), kmap),
                  pl.BlockSpec((hb, t, hd), qmap), pl.BlockSpec((hb, t, hd), qmap), pl.BlockSpec((hb, t, 1), qmap)],
        out_specs=[pl.BlockSpec((hb, t, da), qmap), pl.BlockSpec((hb, t, 1), qmap)],
        out_shape=[jax.ShapeDtypeStruct((H, S, da), F32), jax.ShapeDtypeStruct((H, S, 1), F32)],
        scratch=[pltpu.VMEM((hb, t, da), F32), pltpu.VMEM((hb, t, 1), F32)],
        sem=("parallel", "arbitrary"))(it, jt, qa, ka, v, do, o, lse)


def _flash_bwd_kv(qa, ka, v, do, lse_r, dl_r, name):
    H, S, hd = v.shape
    da = qa.shape[2]
    t = _attn_tile(S)
    n = S // t
    hb = ATTN_HEADS_PER_STEP
    it, jt = _tri_pairs(n, True)

    def body(it_ref, jt_ref, q_ref, k_ref, v_ref, do_ref, lse_ref, dl_ref, dk_ref, dv_ref, dd_ref, ddq_ref, dk_sc, dv_sc, dd_sc):
        i, j = it_ref[pl.program_id(1)], jt_ref[pl.program_id(1)]

        @pl.when(pl.program_id(1) == 0)
        def _():
            ddq_ref[...] = jnp.zeros_like(ddq_ref)

        def step(diag):
            for h in range(hb):
                st = lax.dot_general(k_ref[h], q_ref[h], (((1,), (1,)), ((), ())), preferred_element_type=F32)
                if diag:
                    st = jnp.where(_causal(t, True), st, MASKED)
                pt = jnp.exp(st - lse_ref[h])
                dv_sc[h] += jnp.dot(pt.astype(BF16), do_ref[h], preferred_element_type=F32)
                dpt = lax.dot_general(v_ref[h], do_ref[h], (((1,), (1,)), ((), ())), preferred_element_type=F32)
                dst = pt * (dpt - dl_ref[h])
                dk_sc[h] += jnp.dot(dst.astype(BF16), q_ref[h], preferred_element_type=F32)
                dd_sc[h] -= jnp.sum(dst, axis=1, keepdims=True)
                ddq_ref[h, i] += jnp.sum(dst, axis=0, keepdims=True)

        @pl.when(i == j)
        def _():
            dk_sc[...] = jnp.zeros_like(dk_sc)
            dv_sc[...] = jnp.zeros_like(dv_sc)
            dd_sc[...] = jnp.zeros_like(dd_sc)
            step(True)

        @pl.when(i > j)
        def _():
            step(False)

        @pl.when(i == n - 1)
        def _():
            dk_ref[...] = dk_sc[...]
            dv_ref[...] = dv_sc[...]
            dd_ref[...] = dd_sc[...]

    kmap = lambda h, p, it_, jt_: (h, jt_[p], 0)
    qmap = lambda h, p, it_, jt_: (h, it_[p], 0)
    qrow = lambda h, p, it_, jt_: (h, 0, it_[p])
    return _call_prefetch(
        body, name=name, grid=(H // hb, it.shape[0]), n_prefetch=2,
        in_specs=[pl.BlockSpec((hb, t, da), qmap), pl.BlockSpec((hb, t, da), kmap), pl.BlockSpec((hb, t, hd), kmap),
                  pl.BlockSpec((hb, t, hd), qmap), pl.BlockSpec((hb, 1, t), qrow), pl.BlockSpec((hb, 1, t), qrow)],
        out_specs=[pl.BlockSpec((hb, t, da), kmap), pl.BlockSpec((hb, t, hd), kmap), pl.BlockSpec((hb, t, 1), kmap),
                   pl.BlockSpec((hb, n, 1, t), lambda h, p, it_, jt_: (h, 0, 0, 0))],
        out_shape=[jax.ShapeDtypeStruct((H, S, da), F32), jax.ShapeDtypeStruct((H, S, hd), F32),
                   jax.ShapeDtypeStruct((H, S, 1), F32), jax.ShapeDtypeStruct((H, n, 1, t), F32)],
        scratch=[pltpu.VMEM((hb, t, da), F32), pltpu.VMEM((hb, t, hd), F32), pltpu.VMEM((hb, t, 1), F32)],
        sem=("parallel", "arbitrary"))(it, jt, qa, ka, v, do, lse_r, dl_r)


def _offsets(n_bits):
    return [tuple((k >> b) & 1 for b in reversed(range(n_bits))) for k in range(1, 1 << n_bits)]


def _gather8(arrs, name):
    n = len(arrs)
    offs = _offsets(3)

    def body(*refs):
        ins, outs = refs[:n], refs[n:2 * n]
        ssem, rsem, lsem = refs[2 * n:]
        x, y, c = _place()
        me = 4 * x + 2 * y + c
        copies = []
        for a in range(n):
            lc = pltpu.make_async_copy(ins[a], outs[a].at[me], lsem.at[a])
            lc.start()
            copies.append(lc)
            for k, (dx, dy, dcc) in enumerate(offs):
                cp = pltpu.make_async_remote_copy(
                    src_ref=ins[a], dst_ref=outs[a].at[me], send_sem=ssem.at[a, k], recv_sem=rsem.at[a, k],
                    device_id=((x + dx) % 2, (y + dy) % 2, (c + dcc) % 2), device_id_type=MESH)
                cp.start()
                copies.append(cp)
        for cp in copies:
            cp.wait()

    return _call(body, name=name, in_specs=[ANY] * n, out_specs=[ANY] * n,
                 out_shape=[jax.ShapeDtypeStruct((N_DEV,) + a.shape, a.dtype) for a in arrs],
                 scratch=[pltpu.SemaphoreType.DMA((n, 7)), pltpu.SemaphoreType.DMA((n, 7)), pltpu.SemaphoreType.DMA((n,))])(*arrs)


def _chip_gather(arrs, halved, name):
    n = len(arrs)
    offs = _offsets(2)

    def body(*refs):
        ins, outs = refs[:n], refs[n:2 * n]
        ssem, rsem, lsem = refs[2 * n:]
        x, y, c = _place()
        chip = 2 * x + y
        copies = []
        for a in range(n):
            lc = pltpu.make_async_copy(ins[a], outs[a].at[chip], lsem.at[a])
            lc.start()
            copies.append(lc)
            if halved:
                hn = arrs[a].shape[0] // 2
                src = ins[a].at[pl.ds(c * hn, hn)]
                dst = outs[a].at[chip, pl.ds(c * hn, hn)]
            else:
                src, dst = ins[a], outs[a].at[chip]
            for k, (dx, dy) in enumerate(offs):
                cp = pltpu.make_async_remote_copy(
                    src_ref=src, dst_ref=dst, send_sem=ssem.at[a, k], recv_sem=rsem.at[a, k],
                    device_id=((x + dx) % 2, (y + dy) % 2, c), device_id_type=MESH)
                cp.start()
                copies.append(cp)
        for cp in copies:
            cp.wait()

    return _call(body, name=name, in_specs=[ANY] * n, out_specs=[ANY] * n,
                 out_shape=[jax.ShapeDtypeStruct((N_CHIPS,) + a.shape, a.dtype) for a in arrs],
                 scratch=[pltpu.SemaphoreType.DMA((n, 3)), pltpu.SemaphoreType.DMA((n, 3)), pltpu.SemaphoreType.DMA((n,))])(*arrs)


def _sibling_fill(bufs, name):
    n = len(bufs)
    offs = _offsets(2)

    def body(*refs):
        ins, outs = refs[:n], refs[n:2 * n]
        ssem, rsem = refs[2 * n:]
        x, y, c = _place()
        copies = []
        for a in range(n):
            hn = bufs[a].shape[1] // 2
            for k, (dx, dy) in enumerate(offs):
                chip = 2 * ((x + dx) % 2) + (y + dy) % 2
                cp = pltpu.make_async_remote_copy(
                    src_ref=ins[a].at[chip, pl.ds(c * hn, hn)], dst_ref=outs[a].at[chip, pl.ds(c * hn, hn)],
                    send_sem=ssem.at[a, k], recv_sem=rsem.at[a, k],
                    device_id=(x, y, 1 - c), device_id_type=MESH)
                cp.start()
                copies.append(cp)
        for cp in copies:
            cp.wait()

    return _call(body, name=name, in_specs=[ANY] * n, out_specs=[ANY] * n,
                 out_shape=[jax.ShapeDtypeStruct(b.shape, b.dtype) for b in bufs],
                 scratch=[pltpu.SemaphoreType.DMA((n, 3)), pltpu.SemaphoreType.DMA((n, 3))],
                 aliases={a: a for a in range(n)})(*bufs)


def _sibling_pair(arrs, name):
    n = len(arrs)

    def body(*refs):
        ins, outs = refs[:n], refs[n:2 * n]
        ssem, rsem, lsem = refs[2 * n:]
        x, y, c = _place()
        copies = []
        for a in range(n):
            lc = pltpu.make_async_copy(ins[a], outs[a].at[c], lsem.at[a])
            lc.start()
            cp = pltpu.make_async_remote_copy(
                src_ref=ins[a], dst_ref=outs[a].at[c], send_sem=ssem.at[a], recv_sem=rsem.at[a],
                device_id=(x, y, 1 - c), device_id_type=MESH)
            cp.start()
            copies += [lc, cp]
        for cp in copies:
            cp.wait()

    return _call(body, name=name, in_specs=[ANY] * n, out_specs=[ANY] * n,
                 out_shape=[jax.ShapeDtypeStruct((N_CORES,) + a.shape, a.dtype) for a in arrs],
                 scratch=[pltpu.SemaphoreType.DMA((n,)), pltpu.SemaphoreType.DMA((n,)), pltpu.SemaphoreType.DMA((n,))])(*arrs)


def _piece(shape, spec, j, h):
    shard_ax, half_ax = spec
    w = shape[shard_ax] // N_CHIPS
    idx = [slice(None)] * len(shape)
    idx[shard_ax] = pl.ds(j * w, w)
    hn = (w if half_ax == shard_ax else shape[half_ax]) // 2
    assert half_ax != shard_ax
    idx[half_ax] = pl.ds(h * hn, hn)
    return tuple(idx)


def _piece_shape(shape, spec):
    shard_ax, half_ax = spec
    s = list(shape)
    s[shard_ax] //= N_CHIPS
    s[half_ax] //= 2
    return tuple(s)


def _scatter8(arrs, specs, name):
    n = len(arrs)
    targets = [(jx, jy, h) for jx in range(2) for jy in range(2) for h in range(2)]

    def body(*refs):
        ins, outs = refs[:n], refs[n:2 * n]
        ssem, rsem, lsem = refs[2 * n:]
        x, y, c = _place()
        me = 4 * x + 2 * y + c
        for a in range(n):
            for t, (jx, jy, h) in enumerate(targets):
                src = ins[a].at[_piece(arrs[a].shape, specs[a], 2 * jx + jy, h)]
                dst = outs[a].at[me]

                @pl.when(t != me)
                def _():
                    pltpu.make_async_remote_copy(src_ref=src, dst_ref=dst, send_sem=ssem.at[a, t], recv_sem=rsem.at[a, me],
                                                 device_id=(jx, jy, h), device_id_type=MESH).start()

                @pl.when(t == me)
                def _():
                    pltpu.make_async_copy(src, dst, lsem.at[a]).start()
        for a in range(n):
            for t, (jx, jy, h) in enumerate(targets):
                src = ins[a].at[_piece(arrs[a].shape, specs[a], 2 * jx + jy, h)]
                dst = outs[a].at[t]
                cp = pltpu.make_async_remote_copy(src_ref=src, dst_ref=dst, send_sem=ssem.at[a, t], recv_sem=rsem.at[a, t],
                                                  device_id=(jx, jy, h), device_id_type=MESH)

                @pl.when(t != me)
                def _():
                    cp.wait_send()
                    cp.wait_recv()

                @pl.when(t == me)
                def _():
                    pltpu.make_async_copy(src, dst, lsem.at[a]).wait()

    return _call(body, name=name, in_specs=[ANY] * n, out_specs=[ANY] * n,
                 out_shape=[jax.ShapeDtypeStruct((N_DEV,) + _piece_shape(a.shape, s), a.dtype) for a, s in zip(arrs, specs)],
                 scratch=[pltpu.SemaphoreType.DMA((n, 8)), pltpu.SemaphoreType.DMA((n, 8)), pltpu.SemaphoreType.DMA((n,))])(*arrs)


def kernel(x, c, ada_w, ada_b, pre_mix_g, post_mix_g, pre_ffn_g, post_ffn_g, ffn_w_gu, ffn_w_down, a_w_in, a_b_in, a_ln_g, a_ln_b, a_w_s, a_b_s, a_w_out, kv_ada_w, kv_ada_b, kv_norm_g, kv_w, kv_b_f, k_norm_g, b_w_qg, b_q_norm_g, b_w_o, loss_target, m_ada_w, m_ada_b, m_pre_mix_g, m_post_mix_g, m_pre_ffn_g, m_post_ffn_g, m_ffn_w_gu, m_ffn_w_down, m_a_w_in, m_a_b_in, m_a_ln_g, m_a_ln_b, m_a_w_s, m_a_b_s, m_a_w_out, m_kv_ada_w, m_kv_ada_b, m_kv_norm_g, m_kv_w, m_kv_b_f, m_k_norm_g, m_b_w_qg, m_b_q_norm_g, m_b_w_o, v_ada_w, v_ada_b, v_pre_mix_g, v_post_mix_g, v_pre_ffn_g, v_post_ffn_g, v_ffn_w_gu, v_ffn_w_down, v_a_w_in, v_a_b_in, v_a_ln_g, v_a_ln_b, v_a_w_s, v_a_b_s, v_a_w_out, v_kv_ada_w, v_kv_ada_b, v_kv_norm_g, v_kv_w, v_kv_b_f, v_k_norm_g, v_b_w_qg, v_b_q_norm_g, v_b_w_o):
    weights = dict(ada_w=ada_w, ada_b=ada_b, pre_mix_g=pre_mix_g, post_mix_g=post_mix_g, pre_ffn_g=pre_ffn_g,
                   post_ffn_g=post_ffn_g, ffn_w_gu=ffn_w_gu, ffn_w_down=ffn_w_down, a_w_in=a_w_in, a_b_in=a_b_in,
                   a_ln_g=a_ln_g, a_ln_b=a_ln_b, a_w_s=a_w_s, a_b_s=a_b_s, a_w_out=a_w_out, kv_ada_w=kv_ada_w,
                   kv_ada_b=kv_ada_b, kv_norm_g=kv_norm_g, kv_w=kv_w, kv_b_f=kv_b_f, k_norm_g=k_norm_g, b_w_qg=b_w_qg,
                   b_q_norm_g=b_q_norm_g, b_w_o=b_w_o)
    m_in = dict(ada_w=m_ada_w, ada_b=m_ada_b, pre_mix_g=m_pre_mix_g, post_mix_g=m_post_mix_g, pre_ffn_g=m_pre_ffn_g,
                post_ffn_g=m_post_ffn_g, ffn_w_gu=m_ffn_w_gu, ffn_w_down=m_ffn_w_down, a_w_in=m_a_w_in, a_b_in=m_a_b_in,
                a_ln_g=m_a_ln_g, a_ln_b=m_a_ln_b, a_w_s=m_a_w_s, a_b_s=m_a_b_s, a_w_out=m_a_w_out, kv_ada_w=m_kv_ada_w,
                kv_ada_b=m_kv_ada_b, kv_norm_g=m_kv_norm_g, kv_w=m_kv_w, kv_b_f=m_kv_b_f, k_norm_g=m_k_norm_g,
                b_w_qg=m_b_w_qg, b_q_norm_g=m_b_q_norm_g, b_w_o=m_b_w_o)
    v_in = dict(ada_w=v_ada_w, ada_b=v_ada_b, pre_mix_g=v_pre_mix_g, post_mix_g=v_post_mix_g, pre_ffn_g=v_pre_ffn_g,
                post_ffn_g=v_post_ffn_g, ffn_w_gu=v_ffn_w_gu, ffn_w_down=v_ffn_w_down, a_w_in=v_a_w_in, a_b_in=v_a_b_in,
                a_ln_g=v_a_ln_g, a_ln_b=v_a_ln_b, a_w_s=v_a_w_s, a_b_s=v_a_b_s, a_w_out=v_a_w_out, kv_ada_w=v_kv_ada_w,
                kv_ada_b=v_kv_ada_b, kv_norm_g=v_kv_norm_g, kv_w=v_kv_w, kv_b_f=v_kv_b_f, k_norm_g=v_k_norm_g,
                b_w_qg=v_b_w_qg, b_q_norm_g=v_b_q_norm_g, b_w_o=v_b_w_o)
    names = list(weights)

    S, D = x.shape[1], x.shape[2]
    L, NA, NB = ada_w.shape[0], a_w_in.shape[0], b_w_qg.shape[0]
    H = kv_b_f.shape[0]
    hd = D // H
    G, CH = a_w_s.shape[1], a_w_s.shape[2]
    GW = a_w_out.shape[1] * N_CHIPS
    F = ffn_w_down.shape[1] * N_CHIPS
    ada_cols = ada_w.shape[2]
    kvada_cols = kv_ada_w.shape[1]
    kv_cols = kv_w.shape[1]
    kv_pad = -(-(2 * D + H) // LANES) * LANES
    xi, yi, ci = _place()
    chip = 2 * xi + yi
    me = 2 * chip + ci
    x0 = x[0]
    tgt = loss_target[0]
    row = lambda t: t.reshape(1, -1)

    c_all = _gather8([c], "gather_c")[0].reshape(N_DEV, D)
    c_act = _silu_rows(jnp.pad(c_all, ((0, BF16_ROWS - N_DEV), (0, 0))), "silu_c")
    mod_sh = [_mm(c_act, (ada_w, l), "nn", F32, f"mod_proj_{l}") for l in range(L)]
    mod_sh.append(_mm(c_act, kv_ada_w, "nn", F32, "mod_proj_kv"))
    mod_sh = jnp.concatenate(mod_sh, axis=1)
    mod_all, b_in_all, ln_g_all, ln_b_all = _chip_gather([mod_sh, a_b_in, a_ln_g, a_ln_b], False, "gather_mod")
    mine = lax.dynamic_index_in_dim(mod_all, me, axis=1, keepdims=False)
    mod = [jnp.concatenate([mine[j, l * ada_cols:(l + 1) * ada_cols] for j in range(N_CHIPS)]) + ada_b[l] for l in range(L)]
    mod = [[row(t) for t in jnp.split(m_, 6)] for m_ in mod]
    mod_kv = jnp.concatenate([mine[j, L * ada_cols:] for j in range(N_CHIPS)]) + kv_ada_b
    kv_sh, kv_sc = [row(t) for t in jnp.split(mod_kv, 2)]
    cat_chips = lambda t, ax: jnp.concatenate([t[j] for j in range(N_CHIPS)], axis=ax)
    b_in_f = cat_chips(b_in_all, 1)
    ln_g_f, ln_b_f = cat_chips(ln_g_all, 1), cat_chips(ln_b_all, 1)

    big = ["ffn_w_gu", "ffn_w_down", "a_w_in", "a_w_out", "kv_w", "b_w_qg", "b_w_o"]
    gathered = _chip_gather([weights[n].astype(BF16) for n in big], True, "gather_w")
    gathered = dict(zip(big, _sibling_fill(gathered, "fill_w")))
    w_gu = cat_chips(gathered["ffn_w_gu"], 2)
    w_dn = cat_chips(gathered["ffn_w_down"], 1)
    w_in = cat_chips(gathered["a_w_in"], 2)
    w_out = cat_chips(gathered["a_w_out"], 1)
    w_kv = jnp.pad(cat_chips(gathered["kv_w"], 1), ((0, 0), (0, kv_pad - (2 * D + H))))
    w_qg = cat_chips(gathered["b_w_qg"], 2)
    w_o = cat_chips(gathered["b_w_o"], 1)

    causal = jnp.tril(jnp.ones((CH, CH), F32))
    ws_m = [(a_w_s[i] * causal).astype(BF16) for i in range(NA)]
    ws_mt = [jnp.swapaxes(w, 1, 2) for w in ws_m]
    bs_t = [a_b_s[i].T for i in range(NA)]

    heads = lambda t: t.reshape(S, H, hd).transpose(1, 0, 2)
    unheads = lambda t: t.transpose(1, 0, 2).reshape(S, D)

    saved = []
    kv = None
    xc = x0
    for l in range(L):
        sh_m, sc_m, g_m, sh_f, sc_f, g_f = mod[l]
        st = {"x0": xc}
        h1 = _norm_mod_fwd(xc, row(pre_mix_g[l]), sh_m, sc_m, f"pre_mix_{l}")
        st["h1"] = h1
        if l < NA:
            zp = _mm(h1, (w_in, l), "nn", BF16, f"gmlp_in_{l}")
            yg = _gmlp_fwd(zp, row(b_in_f[l]), row(ln_g_f[l]), row(ln_b_f[l]), ws_m[l], bs_t[l], f"gmlp_gate_{l}")
            y = _mm(yg, (w_out, l), "nn", F32, f"gmlp_out_{l}")
            st.update(zp=zp, yg=yg)
        else:
            jb = l - NA
            qg = _mm(h1, (w_qg, jb), "nn", BF16, f"fox_qg_{jb}")
            q_raw = heads(qg[:, :D]).reshape(H * S, hd)
            qn = _head_norm_fwd(q_raw, row(b_q_norm_g[jb]), hd ** -0.5, f"fox_qnorm_{jb}").reshape(H, S, hd)
            qn = _augment(qn, kv["dcum"], True)
            o, lse = _flash_fwd(qn, kv["ka"], kv["va"], hd, f"fox_attn_{jb}")
            o_t = unheads(o)
            og = _out_gate_fwd(o_t, qg, f"fox_gate_{jb}")
            y = _mm(og, (w_o, jb), "nn", F32, f"fox_out_{jb}")
            st.update(qg=qg, q_raw=q_raw, qn=qn, o=o, lse=lse, o_t=o_t, og=og)
        st["y"] = y
        x1 = _post_fwd(xc, y, row(post_mix_g[l]), g_m, f"post_mix_{l}")
        st["x1"] = x1
        h2 = _norm_mod_fwd(x1, row(pre_ffn_g[l]), sh_f, sc_f, f"pre_ffn_{l}")
        gu = _mm(h2, (w_gu, l), "nn", BF16, f"ffn_gu_{l}")
        act = _swiglu_fwd(gu, f"ffn_act_{l}")
        y2 = _mm(act, (w_dn, l), "nn", F32, f"ffn_down_{l}")
        xc = _post_fwd(x1, y2, row(post_ffn_g[l]), g_f, f"post_ffn_{l}")
        st.update(h2=h2, gu=gu, act=act, y2=y2)
        saved.append(st)
        if l == NA - 1:
            hk = _norm_mod_fwd(xc, row(kv_norm_g), kv_sh, kv_sc, "kv_pre")
            kvf = _mm(hk, w_kv, "nn", F32, "kv_proj")
            k_raw = heads(kvf[:, :D]).reshape(H * S, hd)
            kn = _head_norm_fwd(k_raw, row(k_norm_g), 1.0, "kv_knorm").reshape(H, S, hd)
            vb = heads(kvf[:, D:2 * D]).astype(BF16)
            f_t = kvf[:, 2 * D:2 * D + H].T
            b_col = kv_b_f.reshape(H, 1)
            dcum = _dcum_fwd(f_t, b_col, "kv_dcum")
            va = jnp.concatenate([vb, jnp.ones((H, S, 1), BF16), jnp.zeros((H, S, LANES - hd - 1), BF16)], axis=-1)
            kv = dict(x=xc, hk=hk, k_raw=k_raw, ka=_augment(kn, dcum, False), vb=vb, va=va, f_t=f_t, b_col=b_col, dcum=dcum)

    dx, loss_part = _loss_bwd(xc, tgt, "loss")
    loss = lax.psum(loss_part[0, 0], ("x", "y", "c"))

    gl = {n: [None] * weights[n].shape[0] for n in
          ["pre_mix_g", "post_mix_g", "pre_ffn_g", "post_ffn_g", "ffn_w_gu", "ffn_w_down", "a_w_in", "a_b_in", "a_ln_g",
           "a_ln_b", "a_w_s", "a_b_s", "a_w_out", "b_w_qg", "b_q_norm_g", "b_w_o"]}
    dmod = [None] * L
    dkn = dvb = ddc = None
    gkv = {}
    for l in reversed(range(L)):
        st = saved[l]
        sh_m, sc_m, g_m, sh_f, sc_f, g_f = mod[l]
        if l == NA - 1:
            dk_raw, gkv["k_norm_g"] = _head_norm_bwd(dkn.reshape(H * S, hd), kv["k_raw"], row(k_norm_g), 1.0, "kv_knorm_bwd")
            df_t, db_f = _dcum_bwd(ddc.reshape(H, S), kv["f_t"], kv["b_col"], "kv_dcum_bwd")
            dkvf = jnp.concatenate([unheads(dk_raw.reshape(H, S, hd)), unheads(dvb), df_t.T,
                                    jnp.zeros((S, kv_pad - (2 * D + H)), F32)], axis=1).astype(BF16)
            gkv["kv_w"] = _mm(kv["hk"], dkvf, "tn", BF16, "kv_proj_dw")[:, :2 * D + H]
            dhk = _mm(dkvf, w_kv, "nt", F32, "kv_proj_dx")
            dx, gkv["kv_norm_g"], dsh, dsc = _norm_mod_bwd(dx, dhk, kv["x"], row(kv_norm_g), kv_sh, kv_sc, "kv_pre_bwd")
            gkv["kv_b_f"] = db_f.reshape(H)
            dmod_kv = jnp.concatenate([dsh, dsc], axis=1)
        dy2, gl["post_ffn_g"][l], dg_f = _post_bwd(dx, st["y2"], row(post_ffn_g[l]), g_f, f"post_ffn_bwd_{l}")
        gl["ffn_w_down"][l] = _mm(st["act"], dy2, "tn", BF16, f"ffn_down_dw_{l}")
        dact = _mm(dy2, (w_dn, l), "nt", BF16, f"ffn_down_dx_{l}")
        dgu = _swiglu_bwd(dact, st["gu"], f"ffn_act_bwd_{l}")
        gl["ffn_w_gu"][l] = _mm(st["h2"], dgu, "tn", BF16, f"ffn_gu_dw_{l}")
        dh2 = _mm(dgu, (w_gu, l), "nt", F32, f"ffn_gu_dx_{l}")
        dx, gl["pre_ffn_g"][l], dsh_f, dsc_f = _norm_mod_bwd(dx, dh2, st["x1"], row(pre_ffn_g[l]), sh_f, sc_f, f"pre_ffn_bwd_{l}")
        dy, gl["post_mix_g"][l], dg_m = _post_bwd(dx, st["y"], row(post_mix_g[l]), g_m, f"post_mix_bwd_{l}")
        if l < NA:
            gl["a_w_out"][l] = _mm(st["yg"], dy, "tn", BF16, f"gmlp_out_dw_{l}")
            dyg = _mm(dy, (w_out, l), "nt", BF16, f"gmlp_out_dx_{l}")
            dzp, db_in, dlg, dlb, dws, dbs_t = _gmlp_bwd(dyg, st["zp"], row(b_in_f[l]), row(ln_g_f[l]), row(ln_b_f[l]),
                                                           ws_m[l], ws_mt[l], bs_t[l], f"gmlp_gate_bwd_{l}")
            gl["a_b_in"][l], gl["a_ln_g"][l], gl["a_ln_b"][l] = db_in[0], dlg[0], dlb[0]
            gl["a_w_s"][l], gl["a_b_s"][l] = dws * causal, dbs_t.T
            gl["a_w_in"][l] = _mm(st["h1"], dzp, "tn", BF16, f"gmlp_in_dw_{l}")
            dh1 = _mm(dzp, (w_in, l), "nt", F32, f"gmlp_in_dx_{l}")
        else:
            jb = l - NA
            gl["b_w_o"][jb] = _mm(st["og"], dy, "tn", BF16, f"fox_out_dw_{jb}")
            dog = _mm(dy, (w_o, jb), "nt", F32, f"fox_out_dx_{jb}")
            do_t, dgl = _out_gate_bwd(dog, st["o_t"], st["qg"], f"fox_gate_bwd_{jb}")
            do = heads(do_t)
            dqa, delta = _flash_bwd_q(st["qn"], kv["ka"], kv["vb"], do, st["o"], st["lse"], f"fox_attn_dq_{jb}")
            dqn = dqa[:, :, :hd]
            dk_j, dv_j, dd_k, dd_q = _flash_bwd_kv(st["qn"], kv["ka"], kv["vb"], do,
                                                   st["lse"].reshape(H, 1, S), delta.reshape(H, 1, S), f"fox_attn_dkv_{jb}")
            dk_j = dk_j[:, :, :hd]
            dd_j = dd_k.reshape(H, S) + dd_q.reshape(H, S)
            dkn = dk_j if dkn is None else dkn + dk_j
            dvb = dv_j if dvb is None else dvb + dv_j
            ddc = dd_j if ddc is None else ddc + dd_j
            dq_raw, dgq = _head_norm_bwd(dqn.reshape(H * S, hd), st["q_raw"], row(b_q_norm_g[jb]), hd ** -0.5, f"fox_qnorm_bwd_{jb}")
            gl["b_q_norm_g"][jb] = dgq[0]
            dqg = jnp.concatenate([unheads(dq_raw.reshape(H, S, hd)).astype(BF16), dgl], axis=1)
            gl["b_w_qg"][jb] = _mm(st["h1"], dqg, "tn", BF16, f"fox_qg_dw_{jb}")
            dh1 = _mm(dqg, (w_qg, jb), "nt", F32, f"fox_qg_dx_{jb}")
        dx, gl["pre_mix_g"][l], dsh_m, dsc_m = _norm_mod_bwd(dx, dh1, st["x0"], row(pre_mix_g[l]), sh_m, sc_m, f"pre_mix_bwd_{l}")
        dmod[l] = jnp.concatenate([dsh_m, dsc_m, dg_m, dsh_f, dsc_f, dg_f], axis=1)
    grad_x = dx[None]

    stack = lambda n: jnp.stack([t.reshape(weights[n].shape[1:]) for t in gl[n]])
    small = {"dmod": jnp.concatenate(dmod, axis=1), "dmod_kv": dmod_kv}
    for n in ["pre_mix_g", "post_mix_g", "pre_ffn_g", "post_ffn_g", "a_w_s", "a_b_s", "b_q_norm_g"]:
        small[n] = stack(n)
    for n in ["a_b_in", "a_ln_g", "a_ln_b"]:
        small[n] = jnp.stack(gl[n])
    for n in ["kv_norm_g", "kv_b_f", "k_norm_g"]:
        small[n] = gkv[n]
    sizes = {n: t.size for n, t in small.items()}
    flat = jnp.concatenate([t.reshape(-1).astype(F32) for t in small.values()])
    rows_small = -(-flat.size // (LANES * BF16_ROWS)) * BF16_ROWS
    flat = jnp.pad(flat, (0, rows_small * LANES - flat.size)).reshape(rows_small, LANES)
    flat_all = _gather8([flat], "gather_small")[0]
    flat_sum = _sum_slots(flat_all, "sum_small").reshape(-1)
    offs, o_ = {}, 0
    for n, sz in sizes.items():
        offs[n] = o_
        o_ += sz
    take = lambda n, shape: flat_sum[offs[n]:offs[n] + sizes[n]].reshape(shape)
    dmod_rows = flat_all.reshape(N_DEV, -1)[:, offs["dmod"]:offs["dmod"] + sizes["dmod"] + sizes["dmod_kv"]]
    dmod_rows = jnp.pad(dmod_rows, ((0, BF16_ROWS - N_DEV), (0, 0)))

    grads = {}
    grads["ada_b"] = take("dmod", (L, 6 * D))
    grads["kv_ada_b"] = take("dmod_kv", (2 * D,))
    for n in ["pre_mix_g", "post_mix_g", "pre_ffn_g", "post_ffn_g", "a_w_s", "a_b_s", "b_q_norm_g", "kv_norm_g", "kv_b_f", "k_norm_g"]:
        grads[n] = take(n, weights[n].shape)
    for n in ["a_b_in", "a_ln_g", "a_ln_b"]:
        full = take(n, small[n].shape)
        w = weights[n].shape[1]
        grads[n] = lax.dynamic_slice_in_dim(full, chip * w, w, axis=1)
    ada_g = []
    for l in range(L):
        cols = lax.dynamic_slice_in_dim(dmod_rows[:, l * 6 * D:(l + 1) * 6 * D], chip * ada_cols, ada_cols, axis=1)
        ada_g.append(_mm(c_act, cols, "tn", F32, f"mod_proj_dw_{l}"))
    grads["ada_w"] = jnp.stack(ada_g)
    cols = lax.dynamic_slice_in_dim(dmod_rows[:, L * 6 * D:], chip * kvada_cols, kvada_cols, axis=1)
    grads["kv_ada_w"] = _mm(c_act, cols, "tn", F32, "mod_proj_kv_dw")

    specs = {"ffn_w_gu": (2, 0), "ffn_w_down": (1, 0), "a_w_in": (2, 0), "a_w_out": (1, 0), "kv_w": (0, 1),
             "b_w_qg": (2, 0), "b_w_o": (1, 0)}
    full_g = {n: jnp.stack(gl[n]) for n in big if n != "kv_w"}
    full_g["kv_w"] = gkv["kv_w"].reshape(D, N_CHIPS, kv_cols).transpose(1, 0, 2)
    recv = _scatter8([full_g[n] for n in big], [specs[n] for n in big], "scatter_g")
    halves = [_sum_slots(r, f"sum_g_{n}") for n, r in zip(big, recv)]
    pairs = _sibling_pair(halves, "pair_g")
    for n, p in zip(big, pairs):
        grads[n] = p.reshape(weights[n].shape)

    outs_d, outs_m, outs_v = {}, {}, {}
    for n in names:
        w2 = weights[n] if weights[n].ndim > 1 else weights[n].reshape(1, -1)
        shp = w2.shape
        d_, m_, v_ = _adamw(w2, grads[n].reshape(shp), m_in[n].reshape(shp), v_in[n].reshape(shp), f"adamw_{n}")
        outs_d[n], outs_m[n], outs_v[n] = (t.reshape(weights[n].shape) for t in (d_, m_, v_))
    return (loss, grad_x, *[grads[n] for n in names], *[outs_d[n] for n in names],
            *[outs_m[n] for n in names], *[outs_v[n] for n in names])
```

```python
import functools

import jax
import jax.numpy as jnp
from jax import lax
from jax.experimental import pallas as pl
from jax.experimental.pallas import tpu as pltpu

F32 = jnp.float32
BF16 = jnp.bfloat16
MESH = pl.DeviceIdType.MESH
NORM_EPS = 1e-6
MASKED = -1e30
LANES = 128
BF16_ROWS = 16
ROW_BLOCK_BYTES = 12 << 20
ADAM_LR, ADAM_B1, ADAM_B2, ADAM_EPS, ADAM_WD, ADAM_STEP = 0.001, 0.9, 0.999, 1e-08, 0.01, 10
N_CHIPS, N_CORES, N_DEV = 4, 2, 8
ATTN_HEADS_PER_STEP = 2
ANY = pl.BlockSpec(memory_space=pl.ANY)


def _tile(n, cap, quantum):
    best = None
    d = quantum
    while d <= min(n, cap):
        if n % d == 0:
            best = d
        d += quantum
    return n if best is None else best


def _call(body, *, name, out_shape, grid=(), in_specs=None, out_specs=None, scratch=(), sem=None, aliases=None):
    params = {} if sem is None else {"dimension_semantics": sem}
    return pl.pallas_call(
        body, name=name, grid=grid, in_specs=in_specs, out_specs=out_specs, out_shape=out_shape,
        scratch_shapes=list(scratch), input_output_aliases=aliases or {},
        compiler_params=pltpu.CompilerParams(**params))


def _call_prefetch(body, *, name, out_shape, grid, n_prefetch, in_specs, out_specs, scratch, sem):
    spec = pltpu.PrefetchScalarGridSpec(num_scalar_prefetch=n_prefetch, grid=grid, in_specs=in_specs,
                                        out_specs=out_specs, scratch_shapes=list(scratch))
    return pl.pallas_call(
        body, name=name, grid_spec=spec, out_shape=out_shape,
        compiler_params=pltpu.CompilerParams(dimension_semantics=sem))


def _place():
    x, y, c = lax.axis_index("x"), lax.axis_index("y"), lax.axis_index("c")
    return x, y, c


def _mm(a, b, mode, out_dtype, name):
    b_arr, b_idx = b if isinstance(b, tuple) else (b, None)
    bs = b_arr.shape[-2:]
    if mode == "nn":
        (M, K), (K2, N) = a.shape, bs
        dims = (((1,), (0,)), ((), ()))
    elif mode == "nt":
        (M, K), (N, K2) = a.shape, bs
        dims = (((1,), (1,)), ((), ()))
    else:
        (K, M), (K2, N) = a.shape, bs
        dims = (((0,), (0,)), ((), ()))
    assert K == K2, (name, a.shape, b_arr.shape)
    if mode == "tn":
        tm = _tile(M, 1408, LANES)
        tk = _tile(K, 2048, BF16_ROWS)
        tn = _tile(N, 512, LANES)
    else:
        tm = _tile(M, 1024, BF16_ROWS)
        tk = K if K <= 2816 else _tile(K, 2816, LANES)
        tn = _tile(N, 1408 if tk <= 1024 else 512, LANES)
    if tn < 256:
        tn = N
        tm = _tile(M, 512, LANES if mode == "tn" else BF16_ROWS)
    nk = K // tk
    grid = (M // tm, N // tn, nk)

    if mode == "tn":
        a_spec = pl.BlockSpec((tk, tm), lambda i, j, k: (k, i))
    else:
        a_spec = pl.BlockSpec((tm, tk), lambda i, j, k: (i, k))
    if mode == "nt":
        b_blk, b_map = (tn, tk), (lambda i, j, k: (j, k))
    else:
        b_blk, b_map = (tk, tn), (lambda i, j, k: (k, j))
    if b_idx is None:
        b_spec = pl.BlockSpec(b_blk, b_map)
    else:
        b_spec = pl.BlockSpec((None,) + b_blk, lambda i, j, k: (b_idx,) + b_map(i, j, k))

    def body(a_ref, b_ref, o_ref, *acc):
        r = lax.dot_general(a_ref[...].astype(BF16), b_ref[...].astype(BF16), dims, preferred_element_type=F32)
        if nk == 1:
            o_ref[...] = r.astype(o_ref.dtype)
        else:
            k = pl.program_id(2)

            @pl.when(k == 0)
            def _():
                acc[0][...] = r

            @pl.when(k > 0)
            def _():
                acc[0][...] += r

            @pl.when(k == nk - 1)
            def _():
                o_ref[...] = acc[0][...].astype(o_ref.dtype)

    return _call(
        body, name=name, grid=grid, in_specs=[a_spec, b_spec],
        out_specs=pl.BlockSpec((tm, tn), lambda i, j, k: (i, j)),
        out_shape=jax.ShapeDtypeStruct((M, N), out_dtype),
        scratch=[pltpu.VMEM((tm, tn), F32)] if nk > 1 else [],
        sem=("parallel", "parallel", "arbitrary"))(a, b_arr)


def _rowwise(fn, rows, pars, outs, pouts, name):
    R = rows[0].shape[0]
    row_bytes = 4 * (sum(max(r.shape[1], LANES) for r in rows) + sum(max(c, LANES) for c, _ in outs))
    tb = _tile(R, max(BF16_ROWS, ROW_BLOCK_BYTES // row_bytes), BF16_ROWS)
    nr, npar, no = len(rows), len(pars), len(outs)

    def body(*refs):
        r_in, p_in = refs[:nr], refs[nr:nr + npar]
        r_out, p_out = refs[nr + npar:nr + npar + no], refs[nr + npar + no:]
        ro, po = fn([r[...] for r in r_in], [p[...] for p in p_in])
        for ref, val in zip(r_out, ro):
            if isinstance(val, (tuple, list)):
                off = 0
                for piece in val:
                    w = piece.shape[1]
                    ref[:, off:off + w] = piece.astype(ref.dtype)
                    off += w
            else:
                ref[...] = val.astype(ref.dtype)
        if p_out:
            first = pl.program_id(0) == 0

            @pl.when(first)
            def _():
                for ref, val in zip(p_out, po):
                    ref[...] = val

            @pl.when(jnp.logical_not(first))
            def _():
                for ref, val in zip(p_out, po):
                    ref[...] += val

    res = _call(
        body, name=name, grid=(R // tb,),
        in_specs=[pl.BlockSpec((tb, r.shape[1]), lambda i: (i, 0)) for r in rows]
        + [pl.BlockSpec(p.shape, lambda i: (0, 0)) for p in pars],
        out_specs=[pl.BlockSpec((tb, c), lambda i: (i, 0)) for c, _ in outs]
        + [pl.BlockSpec(s, lambda i: (0, 0)) for s in pouts],
        out_shape=[jax.ShapeDtypeStruct((R, c), dt) for c, dt in outs]
        + [jax.ShapeDtypeStruct(s, F32) for s in pouts],
        sem=("arbitrary",) if pouts else ("parallel",))(*rows, *pars)
    return list(res)


def _rms(x, g):
    return x * lax.rsqrt(jnp.mean(x * x, axis=-1, keepdims=True) + NORM_EPS) * g


def _norm_mod(x, g, sh, sc):
    return _rms(x, g) * (1.0 + sc) + sh


def _gated_post(y, g, gate):
    return gate * _rms(y, g)


def _norm_mod_fwd(x, g, sh, sc, name):
    return _rowwise(lambda r, p: ([_norm_mod(r[0], *p)], []), [x], [g, sh, sc], [(x.shape[1], BF16)], [], name)[0]


def _norm_mod_bwd(dxo, dh, x, g, sh, sc, name):
    def fn(r, p):
        _, vjp = jax.vjp(_norm_mod, r[2], *p)
        dx, dg, dsh, dsc = vjp(r[1].astype(F32))
        return [r[0] + dx], [dg, dsh, dsc]
    c = x.shape[1]
    return _rowwise(fn, [dxo, dh, x], [g, sh, sc], [(c, F32)], [(1, c)] * 3, name)


def _post_fwd(x, y, g, gate, name):
    return _rowwise(lambda r, p: ([r[0] + _gated_post(r[1].astype(F32), *p)], []), [x, y], [g, gate],
                    [(x.shape[1], F32)], [], name)[0]


def _post_bwd(dxo, y, g, gate, name):
    def fn(r, p):
        _, vjp = jax.vjp(_gated_post, r[1].astype(F32), *p)
        dy, dg, dgate = vjp(r[0])
        return [dy], [dg, dgate]
    c = y.shape[1]
    return _rowwise(fn, [dxo, y], [g, gate], [(c, BF16)], [(1, c)] * 2, name)


def _swiglu(g, u):
    return jax.nn.silu(g) * u


def _swiglu_fwd(gu, name):
    f = gu.shape[1] // 2
    return _rowwise(lambda r, p: ([_swiglu(r[0][:, :f].astype(F32), r[0][:, f:].astype(F32))], []), [gu], [],
                    [(f, BF16)], [], name)[0]


def _swiglu_bwd(da, gu, name):
    f = gu.shape[1] // 2

    def fn(r, p):
        _, vjp = jax.vjp(_swiglu, r[1][:, :f].astype(F32), r[1][:, f:].astype(F32))
        return [vjp(r[0].astype(F32))], []
    return _rowwise(fn, [da, gu], [], [(2 * f, BF16)], [], name)[0]


def _silu_rows(c, name):
    return _rowwise(lambda r, p: ([jax.nn.silu(r[0])], []), [c], [], [(c.shape[1], F32)], [], name)[0]


def _head_norm(x, g, scale):
    return _rms(x, g) * scale


def _head_norm_fwd(x, g, scale, name):
    return _rowwise(lambda r, p: ([_head_norm(r[0].astype(F32), p[0], scale)], []), [x], [g],
                    [(x.shape[1], BF16)], [], name)[0]


def _head_norm_bwd(dy, x, g, scale, name):
    def fn(r, p):
        _, vjp = jax.vjp(lambda t, gg: _head_norm(t, gg, scale), r[1].astype(F32), p[0])
        dx, dg = vjp(r[0])
        return [dx], [dg]
    c = x.shape[1]
    return _rowwise(fn, [dy, x], [g], [(c, F32)], [(1, c)], name)


def _out_gate_fwd(o, qg, name):
    d = o.shape[1]
    return _rowwise(lambda r, p: ([r[0] * jax.nn.sigmoid(r[1][:, d:].astype(F32))], []), [o, qg], [],
                    [(d, BF16)], [], name)[0]


def _out_gate_bwd(dog, o, qg, name):
    d = o.shape[1]

    def fn(r, p):
        _, vjp = jax.vjp(lambda oo, gl: oo * jax.nn.sigmoid(gl), r[1], r[2][:, d:].astype(F32))
        do, dgl = vjp(r[0])
        return [do, dgl], []
    return _rowwise(fn, [dog, o, qg], [], [(d, BF16), (d, BF16)], [], name)


def _loss_bwd(y, tgt, name):
    n = y.shape[1]

    def fn(r, p):
        e = r[0] - r[1]
        part = jnp.sum(jnp.sum(e * e, axis=1, keepdims=True), axis=0, keepdims=True) * (0.5 / n)
        return [e * (1.0 / n)], [part]
    return _rowwise(fn, [y, tgt], [], [(n, F32)], [(1, 1)], name)


def _adamw(w, g, m, v, name):
    shape = w.shape
    c = shape[-1]
    flat = [t.reshape(-1, c) for t in (w, g, m, v)]

    def fn(r, p):
        w_, g_, m_, v_ = r
        m2 = ADAM_B1 * m_ + (1.0 - ADAM_B1) * g_
        v2 = ADAM_B2 * v_ + (1.0 - ADAM_B2) * (g_ * g_)
        m_hat = m2 / (1.0 - ADAM_B1 ** ADAM_STEP)
        v_hat = v2 / (1.0 - ADAM_B2 ** ADAM_STEP)
        delta = -ADAM_LR * (m_hat / (jnp.sqrt(v_hat) + ADAM_EPS) + ADAM_WD * w_)
        return [delta, m2, v2], []
    res = _rowwise(fn, flat, [], [(c, F32)] * 3, [], name)
    return [t.reshape(shape) for t in res]


def _sum_slots(recv, name):
    n = recv.shape[0]
    shape = recv.shape[1:]
    c = shape[-1]
    r3 = recv.reshape(n, -1, c)
    rows = r3.shape[1]
    tb = _tile(rows, max(BF16_ROWS, ROW_BLOCK_BYTES // (4 * c * (n + 1))), BF16_ROWS)

    def body(r_ref, o_ref):
        acc = r_ref[0].astype(F32)
        for s in range(1, n):
            acc = acc + r_ref[s].astype(F32)
        o_ref[...] = acc

    out = _call(body, name=name, grid=(rows // tb,),
                in_specs=[pl.BlockSpec((n, tb, c), lambda i: (0, i, 0))],
                out_specs=pl.BlockSpec((tb, c), lambda i: (i, 0)),
                out_shape=jax.ShapeDtypeStruct((rows, c), F32), sem=("parallel",))(r3)
    return out.reshape(shape)


def _gmlp_pre(zu, zv, b_u, b_v, ln_g, ln_b):
    u = jax.nn.gelu(zu + b_u, approximate=True)
    v = jax.nn.gelu(zv + b_v, approximate=True)
    xc = v - jnp.mean(v, axis=-1, keepdims=True)
    vn = xc * lax.rsqrt(jnp.mean(xc * xc, axis=-1, keepdims=True) + NORM_EPS) * ln_g + ln_b
    return u, vn


def _gmlp_fwd(zp, b_in, ln_g, ln_b, ws, bs_t, name):
    S, gw2 = zp.shape
    gw = gw2 // 2
    G, ch, _ = ws.shape
    gd = gw // G
    tb = 2 * ch

    def body(zp_ref, bin_ref, lg_ref, lb_ref, ws_ref, bs_ref, o_ref):
        u, vn = _gmlp_pre(zp_ref[:, :gw].astype(F32), zp_ref[:, gw:].astype(F32), bin_ref[:, :gw], bin_ref[:, gw:],
                          lg_ref[...], lb_ref[...])
        vnb = vn.astype(BF16)
        for c in range(tb // ch):
            for g in range(G):
                rs, cs = slice(c * ch, (c + 1) * ch), slice(g * gd, (g + 1) * gd)
                vv = jnp.dot(ws_ref[g], vnb[rs, cs], preferred_element_type=F32) + bs_ref[:, g:g + 1]
                o_ref[rs, cs] = (u[rs, cs] * vv).astype(o_ref.dtype)

    full = lambda a: pl.BlockSpec(a.shape, lambda i: (0,) * a.ndim)
    return _call(body, name=name, grid=(S // tb,),
                 in_specs=[pl.BlockSpec((tb, gw2), lambda i: (i, 0)), full(b_in), full(ln_g), full(ln_b), full(ws), full(bs_t)],
                 out_specs=pl.BlockSpec((tb, gw), lambda i: (i, 0)),
                 out_shape=jax.ShapeDtypeStruct((S, gw), BF16), sem=("parallel",))(zp, b_in, ln_g, ln_b, ws, bs_t)


def _gmlp_bwd(dyg, zp, b_in, ln_g, ln_b, ws, ws_t, bs_t, name):
    S, gw2 = zp.shape
    gw = gw2 // 2
    G, ch, _ = ws.shape
    gd = gw // G
    tb = 2 * ch

    def body(dy_ref, zp_ref, bin_ref, lg_ref, lb_ref, ws_ref, wst_ref, bs_ref,
             dzp_ref, dbin_ref, dlg_ref, dlb_ref, dws_ref, dbs_ref, du_sc, dvn_sc):
        (u, vn), vjp = jax.vjp(_gmlp_pre, zp_ref[:, :gw].astype(F32), zp_ref[:, gw:].astype(F32), bin_ref[:, :gw],
                               bin_ref[:, gw:], lg_ref[...], lb_ref[...])
        vnb = vn.astype(BF16)
        first = pl.program_id(0) == 0

        @pl.when(first)
        def _():
            dws_ref[...] = jnp.zeros_like(dws_ref)

        lane = lax.broadcasted_iota(jnp.int32, (ch, G), 1)
        dbs = jnp.zeros((ch, G), F32)
        for g in range(G):
            cs = slice(g * gd, (g + 1) * gd)
            dws_g = jnp.zeros((ch, ch), F32)
            col = jnp.zeros((ch, 1), F32)
            for c in range(tb // ch):
                rs = slice(c * ch, (c + 1) * ch)
                vnp = vnb[rs, cs]
                vv = jnp.dot(ws_ref[g], vnp, preferred_element_type=F32) + bs_ref[:, g:g + 1]
                dy = dy_ref[rs, cs].astype(F32)
                du_sc[rs, cs] = dy * vv
                dvv = dy * u[rs, cs]
                dvvb = dvv.astype(BF16)
                dvn_sc[rs, cs] = jnp.dot(wst_ref[g], dvvb, preferred_element_type=F32)
                dws_g = dws_g + lax.dot_general(dvvb, vnp, (((1,), (1,)), ((), ())), preferred_element_type=F32)
                col = col + jnp.sum(dvv, axis=1, keepdims=True)
            dws_ref[g] += dws_g
            dbs = jnp.where(lane == g, col, dbs)
        dzu, dzv, dbu, dbv, dlg, dlb = vjp((du_sc[...], dvn_sc[...]))
        dzp_ref[:, :gw] = dzu.astype(dzp_ref.dtype)
        dzp_ref[:, gw:] = dzv.astype(dzp_ref.dtype)

        @pl.when(first)
        def _():
            dbin_ref[:, :gw] = dbu
            dbin_ref[:, gw:] = dbv
            dlg_ref[...] = dlg
            dlb_ref[...] = dlb
            dbs_ref[...] = dbs

        @pl.when(jnp.logical_not(first))
        def _():
            dbin_ref[:, :gw] += dbu
            dbin_ref[:, gw:] += dbv
            dlg_ref[...] += dlg
            dlb_ref[...] += dlb
            dbs_ref[...] += dbs

    full = lambda a: pl.BlockSpec(a.shape, lambda i: (0,) * a.ndim)
    fshape = lambda s: pl.BlockSpec(s, lambda i: (0,) * len(s))
    return _call(
        body, name=name, grid=(S // tb,),
        in_specs=[pl.BlockSpec((tb, gw), lambda i: (i, 0)), pl.BlockSpec((tb, gw2), lambda i: (i, 0)),
                  full(b_in), full(ln_g), full(ln_b), full(ws), full(ws_t), full(bs_t)],
        out_specs=[pl.BlockSpec((tb, gw2), lambda i: (i, 0)), fshape((1, gw2)), fshape((1, gw)), fshape((1, gw)),
                   fshape((G, ch, ch)), fshape((ch, G))],
        out_shape=[jax.ShapeDtypeStruct((S, gw2), BF16), jax.ShapeDtypeStruct((1, gw2), F32),
                   jax.ShapeDtypeStruct((1, gw), F32), jax.ShapeDtypeStruct((1, gw), F32),
                   jax.ShapeDtypeStruct((G, ch, ch), F32), jax.ShapeDtypeStruct((ch, G), F32)],
        scratch=[pltpu.VMEM((tb, gw), F32), pltpu.VMEM((tb, gw), F32)],
        sem=("arbitrary",))(dyg, zp, b_in, ln_g, ln_b, ws, ws_t, bs_t)


def _dot_01(x, ones_bf16):
    hi = x.astype(BF16)
    r1 = x - hi.astype(F32)
    mid = r1.astype(BF16)
    lo = (r1 - mid.astype(F32)).astype(BF16)
    dot = lambda t: jnp.dot(t, ones_bf16, preferred_element_type=F32)
    return dot(hi) + dot(mid) + dot(lo)


def _log_sigmoid(x):
    return jnp.minimum(x, 0.0) - jnp.log1p(jnp.exp(-jnp.abs(x)))


def _dcum_fwd(f_t, b_col, name):
    H, S = f_t.shape
    tb = _tile(S, 512, LANES)

    def body(f_ref, b_ref, o_ref, carry):
        @pl.when(pl.program_id(0) == 0)
        def _():
            carry[...] = jnp.zeros_like(carry)

        ls = _log_sigmoid(f_ref[...] + b_ref[...])
        r = lax.broadcasted_iota(jnp.int32, (tb, tb), 0)
        c = lax.broadcasted_iota(jnp.int32, (tb, tb), 1)
        upper = (r <= c).astype(BF16)
        o_ref[...] = _dot_01(ls, upper) + carry[...]
        carry[...] += jnp.sum(ls, axis=1, keepdims=True)

    return _call(body, name=name, grid=(S // tb,),
                 in_specs=[pl.BlockSpec((H, tb), lambda i: (0, i)), pl.BlockSpec((H, 1), lambda i: (0, 0))],
                 out_specs=pl.BlockSpec((H, tb), lambda i: (0, i)),
                 out_shape=jax.ShapeDtypeStruct((H, S), F32),
                 scratch=[pltpu.VMEM((H, 1), F32)], sem=("arbitrary",))(f_t, b_col)


def _dcum_bwd(dd_t, f_t, b_col, name):
    H, S = f_t.shape
    tb = _tile(S, 512, LANES)
    nb = S // tb

    def body(dd_ref, f_ref, b_ref, df_ref, db_ref, carry):
        first = pl.program_id(0) == 0

        @pl.when(first)
        def _():
            carry[...] = jnp.zeros_like(carry)

        dd = dd_ref[...]
        r = lax.broadcasted_iota(jnp.int32, (tb, tb), 0)
        c = lax.broadcasted_iota(jnp.int32, (tb, tb), 1)
        lower = (r >= c).astype(BF16)
        rev = _dot_01(dd, lower) + carry[...]
        carry[...] += jnp.sum(dd, axis=1, keepdims=True)
        df = rev * jax.nn.sigmoid(-(f_ref[...] + b_ref[...]))
        df_ref[...] = df
        part = jnp.sum(df, axis=1, keepdims=True)

        @pl.when(first)
        def _():
            db_ref[...] = part

        @pl.when(jnp.logical_not(first))
        def _():
            db_ref[...] += part

    return _call(body, name=name, grid=(nb,),
                 in_specs=[pl.BlockSpec((H, tb), lambda i: (0, nb - 1 - i)), pl.BlockSpec((H, tb), lambda i: (0, nb - 1 - i)),
                           pl.BlockSpec((H, 1), lambda i: (0, 0))],
                 out_specs=[pl.BlockSpec((H, tb), lambda i: (0, nb - 1 - i)), pl.BlockSpec((H, 1), lambda i: (0, 0))],
                 out_shape=[jax.ShapeDtypeStruct((H, S), F32), jax.ShapeDtypeStruct((H, 1), F32)],
                 scratch=[pltpu.VMEM((H, 1), F32)], sem=("arbitrary",))(dd_t, f_t, b_col)


def _attn_tile(S):
    return _tile(S, 512, LANES)


def _causal(t, transposed):
    r = lax.broadcasted_iota(jnp.int32, (t, t), 0)
    c = lax.broadcasted_iota(jnp.int32, (t, t), 1)
    return (r <= c) if transposed else (c <= r)


def _tri_pairs(n, key_major):
    if key_major:
        pairs = [(i, j) for j in range(n) for i in range(j, n)]
    else:
        pairs = [(i, j) for i in range(n) for j in range(i + 1)]
    return jnp.asarray([p[0] for p in pairs], jnp.int32), jnp.asarray([p[1] for p in pairs], jnp.int32)


def _split3(x):
    hi = lax.reduce_precision(x, 8, 7)
    r = x - hi
    mid = lax.reduce_precision(r, 8, 7)
    lo = lax.reduce_precision(r - mid, 8, 7)
    return hi.astype(BF16), mid.astype(BF16), lo.astype(BF16)


def _augment(xn, dcum, query):
    H, S, hd = xn.shape
    parts = list(_split3(dcum))
    ones = [jnp.ones((H, S), BF16)] * 3
    extra = jnp.stack(parts + ones if query else ones + [-p for p in parts], axis=-1)
    return (jnp.pad(xn, ((0, 0), (0, 0), (0, LANES - hd)))
            + jnp.pad(extra, ((0, 0), (0, 0), (hd, LANES - hd - 6))))


def _flash_fwd(qa, ka, va, hd, name):
    H, S, da = qa.shape
    t = _attn_tile(S)
    hb = ATTN_HEADS_PER_STEP
    it, jt = _tri_pairs(S // t, False)

    def body(it_ref, jt_ref, q_ref, k_ref, v_ref, o_ref, lse_ref, m_sc, acc_sc):
        i, j = it_ref[pl.program_id(1)], jt_ref[pl.program_id(1)]

        @pl.when(j == 0)
        def _():
            m_sc[...] = jnp.full_like(m_sc, MASKED)
            acc_sc[...] = jnp.zeros_like(acc_sc)

        def step(diag):
            for h in range(hb):
                s = lax.dot_general(q_ref[h], k_ref[h], (((1,), (1,)), ((), ())), preferred_element_type=F32)
                if diag:
                    s = jnp.where(_causal(t, False), s, MASKED)
                m_prev = m_sc[h]
                m_new = jnp.maximum(m_prev, jnp.max(s, axis=1, keepdims=True))
                p = jnp.exp(s - m_new).astype(BF16)
                pv = jnp.dot(p, v_ref[h], preferred_element_type=F32)
                acc_sc[h] = jnp.exp(m_prev - m_new) * acc_sc[h] + pv
                m_sc[h] = m_new

        @pl.when(j < i)
        def _():
            step(False)

        @pl.when(j == i)
        def _():
            step(True)
            for h in range(hb):
                acc = acc_sc[h]
                l = acc[:, hd:hd + 1]
                o_ref[h] = acc[:, :hd] / l
                lse_ref[h] = m_sc[h] + jnp.log(l)

    qmap = lambda h, p, it_, jt_: (h, it_[p], 0)
    kmap = lambda h, p, it_, jt_: (h, jt_[p], 0)
    return _call_prefetch(
        body, name=name, grid=(H // hb, it.shape[0]), n_prefetch=2,
        in_specs=[pl.BlockSpec((hb, t, da), qmap), pl.BlockSpec((hb, t, da), kmap), pl.BlockSpec((hb, t, da), kmap)],
        out_specs=[pl.BlockSpec((hb, t, hd), qmap), pl.BlockSpec((hb, t, 1), qmap)],
        out_shape=[jax.ShapeDtypeStruct((H, S, hd), F32), jax.ShapeDtypeStruct((H, S, 1), F32)],
        scratch=[pltpu.VMEM((hb, t, 1), F32), pltpu.VMEM((hb, t, da), F32)],
        sem=("parallel", "arbitrary"))(it, jt, qa, ka, va)


def _flash_bwd_q(qa, ka, v, do, o, lse, name):
    H, S, hd = v.shape
    da = qa.shape[2]
    t = _attn_tile(S)
    hb = ATTN_HEADS_PER_STEP
    it, jt = _tri_pairs(S // t, False)

    def body(it_ref, jt_ref, q_ref, k_ref, v_ref, do_ref, o_ref, lse_ref, dq_ref, dl_ref, acc_sc, dl_sc):
        i, j = it_ref[pl.program_id(1)], jt_ref[pl.program_id(1)]

        @pl.when(j == 0)
        def _():
            acc_sc[...] = jnp.zeros_like(acc_sc)
            for h in range(hb):
                dl_sc[h] = jnp.sum(do_ref[h].astype(F32) * o_ref[h], axis=1, keepdims=True)

        def step(diag):
            for h in range(hb):
                s = lax.dot_general(q_ref[h], k_ref[h], (((1,), (1,)), ((), ())), preferred_element_type=F32)
                if diag:
                    s = jnp.where(_causal(t, False), s, MASKED)
                p = jnp.exp(s - lse_ref[h])
                dp = lax.dot_general(do_ref[h], v_ref[h], (((1,), (1,)), ((), ())), preferred_element_type=F32)
                ds = p * (dp - dl_sc[h])
                acc_sc[h] += jnp.dot(ds.astype(BF16), k_ref[h], preferred_element_type=F32)

        @pl.when(j < i)
        def _():
            step(False)

        @pl.when(j == i)
        def _():
            step(True)
            dq_ref[...] = acc_sc[...]
            dl_ref[...] = dl_sc[...]

    qmap = lambda h, p, it_, jt_: (h, it_[p], 0)
    kmap = lambda h, p, it_, jt_: (h, jt_[p], 0)
    return _call_prefetch(
        body, name=name, grid=(H // hb, it.shape[0]), n_prefetch=2,
        in_specs=[pl.BlockSpec((hb, t, da), qmap), pl.BlockSpec((hb, t, da), kmap), pl.BlockSpec((hb, t, hd), kmap),
                  pl.BlockSpec((hb, t, hd), qmap), pl.BlockSpec((hb, t, hd), qmap), pl.BlockSpec((hb, t, 1), qmap)],
        out_specs=[pl.BlockSpec((hb, t, da), qmap), pl.BlockSpec((hb, t, 1), qmap)],
        out_shape=[jax.ShapeDtypeStruct((H, S, da), F32), jax.ShapeDtypeStruct((H, S, 1), F32)],
        scratch=[pltpu.VMEM((hb, t, da), F32), pltpu.VMEM((hb, t, 1), F32)],
        sem=("parallel", "arbitrary"))(it, jt, qa, ka, v, do, o, lse)


def _flash_bwd_kv(qa, ka, v, do, lse_r, dl_r, name):
    H, S, hd = v.shape
    da = qa.shape[2]
    t = _attn_tile(S)
    n = S // t
    hb = ATTN_HEADS_PER_STEP
    it, jt = _tri_pairs(n, True)

    def body(it_ref, jt_ref, q_ref, k_ref, v_ref, do_ref, lse_ref, dl_ref, dk_ref, dv_ref, dd_ref, ddq_ref, dk_sc, dv_sc, dd_sc):
        i, j = it_ref[pl.program_id(1)], jt_ref[pl.program_id(1)]

        @pl.when(pl.program_id(1) == 0)
        def _():
            ddq_ref[...] = jnp.zeros_like(ddq_ref)

        def step(diag):
            for h in range(hb):
                st = lax.dot_general(k_ref[h], q_ref[h], (((1,), (1,)), ((), ())), preferred_element_type=F32)
                if diag:
                    st = jnp.where(_causal(t, True), st, MASKED)
                pt = jnp.exp(st - lse_ref[h])
                dv_sc[h] += jnp.dot(pt.astype(BF16), do_ref[h], preferred_element_type=F32)
                dpt = lax.dot_general(v_ref[h], do_ref[h], (((1,), (1,)), ((), ())), preferred_element_type=F32)
                dst = pt * (dpt - dl_ref[h])
                dk_sc[h] += jnp.dot(dst.astype(BF16), q_ref[h], preferred_element_type=F32)
                dd_sc[h] -= jnp.sum(dst, axis=1, keepdims=True)
                ddq_ref[h, i] += jnp.sum(dst, axis=0, keepdims=True)

        @pl.when(i == j)
        def _():
            dk_sc[...] = jnp.zeros_like(dk_sc)
            dv_sc[...] = jnp.zeros_like(dv_sc)
            dd_sc[...] = jnp.zeros_like(dd_sc)
            step(True)

        @pl.when(i > j)
        def _():
            step(False)

        @pl.when(i == n - 1)
        def _():
            dk_ref[...] = dk_sc[...]
            dv_ref[...] = dv_sc[...]
            dd_ref[...] = dd_sc[...]

    kmap = lambda h, p, it_, jt_: (h, jt_[p], 0)
    qmap = lambda h, p, it_, jt_: (h, it_[p], 0)
    qrow = lambda h, p, it_, jt_: (h, 0, it_[p])
    return _call_prefetch(
        body, name=name, grid=(H // hb, it.shape[0]), n_prefetch=2,
        in_specs=[pl.BlockSpec((hb, t, da), qmap), pl.BlockSpec((hb, t, da), kmap), pl.BlockSpec((hb, t, hd), kmap),
                  pl.BlockSpec((hb, t, hd), qmap), pl.BlockSpec((hb, 1, t), qrow), pl.BlockSpec((hb, 1, t), qrow)],
        out_specs=[pl.BlockSpec((hb, t, da), kmap), pl.BlockSpec((hb, t, hd), kmap), pl.BlockSpec((hb, t, 1), kmap),
                   pl.BlockSpec((hb, n, 1, t), lambda h, p, it_, jt_: (h, 0, 0, 0))],
        out_shape=[jax.ShapeDtypeStruct((H, S, da), F32), jax.ShapeDtypeStruct((H, S, hd), F32),
                   jax.ShapeDtypeStruct((H, S, 1), F32), jax.ShapeDtypeStruct((H, n, 1, t), F32)],
        scratch=[pltpu.VMEM((hb, t, da), F32), pltpu.VMEM((hb, t, hd), F32), pltpu.VMEM((hb, t, 1), F32)],
        sem=("parallel", "arbitrary"))(it, jt, qa, ka, v, do, lse_r, dl_r)


def _offsets(n_bits):
    return [tuple((k >> b) & 1 for b in reversed(range(n_bits))) for k in range(1, 1 << n_bits)]


def _gather8(arrs, name):
    n = len(arrs)
    offs = _offsets(3)

    def body(*refs):
        ins, outs = refs[:n], refs[n:2 * n]
        ssem, rsem, lsem = refs[2 * n:]
        x, y, c = _place()
        me = 4 * x + 2 * y + c
        copies = []
        for a in range(n):
            lc = pltpu.make_async_copy(ins[a], outs[a].at[me], lsem.at[a])
            lc.start()
            copies.append(lc)
            for k, (dx, dy, dcc) in enumerate(offs):
                cp = pltpu.make_async_remote_copy(
                    src_ref=ins[a], dst_ref=outs[a].at[me], send_sem=ssem.at[a, k], recv_sem=rsem.at[a, k],
                    device_id=((x + dx) % 2, (y + dy) % 2, (c + dcc) % 2), device_id_type=MESH)
                cp.start()
                copies.append(cp)
        for cp in copies:
            cp.wait()

    return _call(body, name=name, in_specs=[ANY] * n, out_specs=[ANY] * n,
                 out_shape=[jax.ShapeDtypeStruct((N_DEV,) + a.shape, a.dtype) for a in arrs],
                 scratch=[pltpu.SemaphoreType.DMA((n, 7)), pltpu.SemaphoreType.DMA((n, 7)), pltpu.SemaphoreType.DMA((n,))])(*arrs)


def _chip_gather(arrs, halved, name):
    n = len(arrs)
    offs = _offsets(2)

    def body(*refs):
        ins, outs = refs[:n], refs[n:2 * n]
        ssem, rsem, lsem = refs[2 * n:]
        x, y, c = _place()
        chip = 2 * x + y
        copies = []
        for a in range(n):
            lc = pltpu.make_async_copy(ins[a], outs[a].at[chip], lsem.at[a])
            lc.start()
            copies.append(lc)
            if halved:
                hn = arrs[a].shape[0] // 2
                src = ins[a].at[pl.ds(c * hn, hn)]
                dst = outs[a].at[chip, pl.ds(c * hn, hn)]
            else:
                src, dst = ins[a], outs[a].at[chip]
            for k, (dx, dy) in enumerate(offs):
                cp = pltpu.make_async_remote_copy(
                    src_ref=src, dst_ref=dst, send_sem=ssem.at[a, k], recv_sem=rsem.at[a, k],
                    device_id=((x + dx) % 2, (y + dy) % 2, c), device_id_type=MESH)
                cp.start()
                copies.append(cp)
        for cp in copies:
            cp.wait()

    return _call(body, name=name, in_specs=[ANY] * n, out_specs=[ANY] * n,
                 out_shape=[jax.ShapeDtypeStruct((N_CHIPS,) + a.shape, a.dtype) for a in arrs],
                 scratch=[pltpu.SemaphoreType.DMA((n, 3)), pltpu.SemaphoreType.DMA((n, 3)), pltpu.SemaphoreType.DMA((n,))])(*arrs)


def _sibling_fill(bufs, name):
    n = len(bufs)
    offs = _offsets(2)

    def body(*refs):
        ins, outs = refs[:n], refs[n:2 * n]
        ssem, rsem = refs[2 * n:]
        x, y, c = _place()
        copies = []
        for a in range(n):
            hn = bufs[a].shape[1] // 2
            for k, (dx, dy) in enumerate(offs):
                chip = 2 * ((x + dx) % 2) + (y + dy) % 2
                cp = pltpu.make_async_remote_copy(
                    src_ref=ins[a].at[chip, pl.ds(c * hn, hn)], dst_ref=outs[a].at[chip, pl.ds(c * hn, hn)],
                    send_sem=ssem.at[a, k], recv_sem=rsem.at[a, k],
                    device_id=(x, y, 1 - c), device_id_type=MESH)
                cp.start()
                copies.append(cp)
        for cp in copies:
            cp.wait()

    return _call(body, name=name, in_specs=[ANY] * n, out_specs=[ANY] * n,
                 out_shape=[jax.ShapeDtypeStruct(b.shape, b.dtype) for b in bufs],
                 scratch=[pltpu.SemaphoreType.DMA((n, 3)), pltpu.SemaphoreType.DMA((n, 3))],
                 aliases={a: a for a in range(n)})(*bufs)


def _sibling_pair(arrs, name):
    n = len(arrs)

    def body(*refs):
        ins, outs = refs[:n], refs[n:2 * n]
        ssem, rsem, lsem = refs[2 * n:]
        x, y, c = _place()
        copies = []
        for a in range(n):
            lc = pltpu.make_async_copy(ins[a], outs[a].at[c], lsem.at[a])
            lc.start()
            cp = pltpu.make_async_remote_copy(
                src_ref=ins[a], dst_ref=outs[a].at[c], send_sem=ssem.at[a], recv_sem=rsem.at[a],
                device_id=(x, y, 1 - c), device_id_type=MESH)
            cp.start()
            copies += [lc, cp]
        for cp in copies:
            cp.wait()

    return _call(body, name=name, in_specs=[ANY] * n, out_specs=[ANY] * n,
                 out_shape=[jax.ShapeDtypeStruct((N_CORES,) + a.shape, a.dtype) for a in arrs],
                 scratch=[pltpu.SemaphoreType.DMA((n,)), pltpu.SemaphoreType.DMA((n,)), pltpu.SemaphoreType.DMA((n,))])(*arrs)


def _piece(shape, spec, j, h):
    shard_ax, half_ax = spec
    w = shape[shard_ax] // N_CHIPS
    idx = [slice(None)] * len(shape)
    idx[shard_ax] = pl.ds(j * w, w)
    hn = (w if half_ax == shard_ax else shape[half_ax]) // 2
    assert half_ax != shard_ax
    idx[half_ax] = pl.ds(h * hn, hn)
    return tuple(idx)


def _piece_shape(shape, spec):
    shard_ax, half_ax = spec
    s = list(shape)
    s[shard_ax] //= N_CHIPS
    s[half_ax] //= 2
    return tuple(s)


def _scatter8(arrs, specs, name):
    n = len(arrs)
    targets = [(jx, jy, h) for jx in range(2) for jy in range(2) for h in range(2)]

    def body(*refs):
        ins, outs = refs[:n], refs[n:2 * n]
        ssem, rsem, lsem = refs[2 * n:]
        x, y, c = _place()
        me = 4 * x + 2 * y + c
        for a in range(n):
            for t, (jx, jy, h) in enumerate(targets):
                src = ins[a].at[_piece(arrs[a].shape, specs[a], 2 * jx + jy, h)]
                dst = outs[a].at[me]

                @pl.when(t != me)
                def _():
                    pltpu.make_async_remote_copy(src_ref=src, dst_ref=dst, send_sem=ssem.at[a, t], recv_sem=rsem.at[a, me],
                                                 device_id=(jx, jy, h), device_id_type=MESH).start()

                @pl.when(t == me)
                def _():
                    pltpu.make_async_copy(src, dst, lsem.at[a]).start()
        for a in range(n):
            for t, (jx, jy, h) in enumerate(targets):
                src = ins[a].at[_piece(arrs[a].shape, specs[a], 2 * jx + jy, h)]
                dst = outs[a].at[t]
                cp = pltpu.make_async_remote_copy(src_ref=src, dst_ref=dst, send_sem=ssem.at[a, t], recv_sem=rsem.at[a, t],
                                                  device_id=(jx, jy, h), device_id_type=MESH)

                @pl.when(t != me)
                def _():
                    cp.wait_send()
                    cp.wait_recv()

                @pl.when(t == me)
                def _():
                    pltpu.make_async_copy(src, dst, lsem.at[a]).wait()

    return _call(body, name=name, in_specs=[ANY] * n, out_specs=[ANY] * n,
                 out_shape=[jax.ShapeDtypeStruct((N_DEV,) + _piece_shape(a.shape, s), a.dtype) for a, s in zip(arrs, specs)],
                 scratch=[pltpu.SemaphoreType.DMA((n, 8)), pltpu.SemaphoreType.DMA((n, 8)), pltpu.SemaphoreType.DMA((n,))])(*arrs)


def kernel(x, c, ada_w, ada_b, pre_mix_g, post_mix_g, pre_ffn_g, post_ffn_g, ffn_w_gu, ffn_w_down, a_w_in, a_b_in, a_ln_g, a_ln_b, a_w_s, a_b_s, a_w_out, kv_ada_w, kv_ada_b, kv_norm_g, kv_w, kv_b_f, k_norm_g, b_w_qg, b_q_norm_g, b_w_o, loss_target, m_ada_w, m_ada_b, m_pre_mix_g, m_post_mix_g, m_pre_ffn_g, m_post_ffn_g, m_ffn_w_gu, m_ffn_w_down, m_a_w_in, m_a_b_in, m_a_ln_g, m_a_ln_b, m_a_w_s, m_a_b_s, m_a_w_out, m_kv_ada_w, m_kv_ada_b, m_kv_norm_g, m_kv_w, m_kv_b_f, m_k_norm_g, m_b_w_qg, m_b_q_norm_g, m_b_w_o, v_ada_w, v_ada_b, v_pre_mix_g, v_post_mix_g, v_pre_ffn_g, v_post_ffn_g, v_ffn_w_gu, v_ffn_w_down, v_a_w_in, v_a_b_in, v_a_ln_g, v_a_ln_b, v_a_w_s, v_a_b_s, v_a_w_out, v_kv_ada_w, v_kv_ada_b, v_kv_norm_g, v_kv_w, v_kv_b_f, v_k_norm_g, v_b_w_qg, v_b_q_norm_g, v_b_w_o):
    weights = dict(ada_w=ada_w, ada_b=ada_b, pre_mix_g=pre_mix_g, post_mix_g=post_mix_g, pre_ffn_g=pre_ffn_g,
                   post_ffn_g=post_ffn_g, ffn_w_gu=ffn_w_gu, ffn_w_down=ffn_w_down, a_w_in=a_w_in, a_b_in=a_b_in,
                   a_ln_g=a_ln_g, a_ln_b=a_ln_b, a_w_s=a_w_s, a_b_s=a_b_s, a_w_out=a_w_out, kv_ada_w=kv_ada_w,
                   kv_ada_b=kv_ada_b, kv_norm_g=kv_norm_g, kv_w=kv_w, kv_b_f=kv_b_f, k_norm_g=k_norm_g, b_w_qg=b_w_qg,
                   b_q_norm_g=b_q_norm_g, b_w_o=b_w_o)
    m_in = dict(ada_w=m_ada_w, ada_b=m_ada_b, pre_mix_g=m_pre_mix_g, post_mix_g=m_post_mix_g, pre_ffn_g=m_pre_ffn_g,
                post_ffn_g=m_post_ffn_g, ffn_w_gu=m_ffn_w_gu, ffn_w_down=m_ffn_w_down, a_w_in=m_a_w_in, a_b_in=m_a_b_in,
                a_ln_g=m_a_ln_g, a_ln_b=m_a_ln_b, a_w_s=m_a_w_s, a_b_s=m_a_b_s, a_w_out=m_a_w_out, kv_ada_w=m_kv_ada_w,
                kv_ada_b=m_kv_ada_b, kv_norm_g=m_kv_norm_g, kv_w=m_kv_w, kv_b_f=m_kv_b_f, k_norm_g=m_k_norm_g,
                b_w_qg=m_b_w_qg, b_q_norm_g=m_b_q_norm_g, b_w_o=m_b_w_o)
    v_in = dict(ada_w=v_ada_w, ada_b=v_ada_b, pre_mix_g=v_pre_mix_g, post_mix_g=v_post_mix_g, pre_ffn_g=v_pre_ffn_g,
                post_ffn_g=v_post_ffn_g, ffn_w_gu=v_ffn_w_gu, ffn_w_down=v_ffn_w_down, a_w_in=v_a_w_in, a_b_in=v_a_b_in,
                a_ln_g=v_a_ln_g, a_ln_b=v_a_ln_b, a_w_s=v_a_w_s, a_b_s=v_a_b_s, a_w_out=v_a_w_out, kv_ada_w=v_kv_ada_w,
                kv_ada_b=v_kv_ada_b, kv_norm_g=v_kv_norm_g, kv_w=v_kv_w, kv_b_f=v_kv_b_f, k_norm_g=v_k_norm_g,
                b_w_qg=v_b_w_qg, b_q_norm_g=v_b_q_norm_g, b_w_o=v_b_w_o)
    names = list(weights)

    S, D = x.shape[1], x.shape[2]
    L, NA, NB = ada_w.shape[0], a_w_in.shape[0], b_w_qg.shape[0]
    H = kv_b_f.shape[0]
    hd = D // H
    G, CH = a_w_s.shape[1], a_w_s.shape[2]
    GW = a_w_out.shape[1] * N_CHIPS
    F = ffn_w_down.shape[1] * N_CHIPS
    ada_cols = ada_w.shape[2]
    kvada_cols = kv_ada_w.shape[1]
    kv_cols = kv_w.shape[1]
    kv_pad = -(-(2 * D + H) // LANES) * LANES
    xi, yi, ci = _place()
    chip = 2 * xi + yi
    me = 2 * chip + ci
    x0 = x[0]
    tgt = loss_target[0]
    row = lambda t: t.reshape(1, -1)

    c_all = _gather8([c], "gather_c")[0].reshape(N_DEV, D)
    c_act = _silu_rows(jnp.pad(c_all, ((0, BF16_ROWS - N_DEV), (0, 0))), "silu_c")
    mod_sh = [_mm(c_act, (ada_w, l), "nn", F32, f"mod_proj_{l}") for l in range(L)]
    mod_sh.append(_mm(c_act, kv_ada_w, "nn", F32, "mod_proj_kv"))
    mod_sh = jnp.concatenate(mod_sh, axis=1)
    mod_all, b_in_all, ln_g_all, ln_b_all = _chip_gather([mod_sh, a_b_in, a_ln_g, a_ln_b], False, "gather_mod")
    mine = lax.dynamic_index_in_dim(mod_all, me, axis=1, keepdims=False)
    mod = [jnp.concatenate([mine[j, l * ada_cols:(l + 1) * ada_cols] for j in range(N_CHIPS)]) + ada_b[l] for l in range(L)]
    mod = [[row(t) for t in jnp.split(m_, 6)] for m_ in mod]
    mod_kv = jnp.concatenate([mine[j, L * ada_cols:] for j in range(N_CHIPS)]) + kv_ada_b
    kv_sh, kv_sc = [row(t) for t in jnp.split(mod_kv, 2)]
    cat_chips = lambda t, ax: jnp.concatenate([t[j] for j in range(N_CHIPS)], axis=ax)
    b_in_f = cat_chips(b_in_all, 1)
    ln_g_f, ln_b_f = cat_chips(ln_g_all, 1), cat_chips(ln_b_all, 1)

    big = ["ffn_w_gu", "ffn_w_down", "a_w_in", "a_w_out", "kv_w", "b_w_qg", "b_w_o"]
    gathered = _chip_gather([weights[n].astype(BF16) for n in big], True, "gather_w")
    gathered = dict(zip(big, _sibling_fill(gathered, "fill_w")))
    w_gu = cat_chips(gathered["ffn_w_gu"], 2)
    w_dn = cat_chips(gathered["ffn_w_down"], 1)
    w_in = cat_chips(gathered["a_w_in"], 2)
    w_out = cat_chips(gathered["a_w_out"], 1)
    w_kv = jnp.pad(cat_chips(gathered["kv_w"], 1), ((0, 0), (0, kv_pad - (2 * D + H))))
    w_qg = cat_chips(gathered["b_w_qg"], 2)
    w_o = cat_chips(gathered["b_w_o"], 1)

    causal = jnp.tril(jnp.ones((CH, CH), F32))
    ws_m = [(a_w_s[i] * causal).astype(BF16) for i in range(NA)]
    ws_mt = [jnp.swapaxes(w, 1, 2) for w in ws_m]
    bs_t = [a_b_s[i].T for i in range(NA)]

    heads = lambda t: t.reshape(S, H, hd).transpose(1, 0, 2)
    unheads = lambda t: t.transpose(1, 0, 2).reshape(S, D)

    saved = []
    kv = None
    xc = x0
    for l in range(L):
        sh_m, sc_m, g_m, sh_f, sc_f, g_f = mod[l]
        st = {"x0": xc}
        h1 = _norm_mod_fwd(xc, row(pre_mix_g[l]), sh_m, sc_m, f"pre_mix_{l}")
        st["h1"] = h1
        if l < NA:
            zp = _mm(h1, (w_in, l), "nn", BF16, f"gmlp_in_{l}")
            yg = _gmlp_fwd(zp, row(b_in_f[l]), row(ln_g_f[l]), row(ln_b_f[l]), ws_m[l], bs_t[l], f"gmlp_gate_{l}")
            y = _mm(yg, (w_out, l), "nn", F32, f"gmlp_out_{l}")
            st.update(zp=zp, yg=yg)
        else:
            jb = l - NA
            qg = _mm(h1, (w_qg, jb), "nn", BF16, f"fox_qg_{jb}")
            q_raw = heads(qg[:, :D]).reshape(H * S, hd)
            qn = _head_norm_fwd(q_raw, row(b_q_norm_g[jb]), hd ** -0.5, f"fox_qnorm_{jb}").reshape(H, S, hd)
            qn = _augment(qn, kv["dcum"], True)
            o, lse = _flash_fwd(qn, kv["ka"], kv["va"], hd, f"fox_attn_{jb}")
            o_t = unheads(o)
            og = _out_gate_fwd(o_t, qg, f"fox_gate_{jb}")
            y = _mm(og, (w_o, jb), "nn", F32, f"fox_out_{jb}")
            st.update(qg=qg, q_raw=q_raw, qn=qn, o=o, lse=lse, o_t=o_t, og=og)
        st["y"] = y
        x1 = _post_fwd(xc, y, row(post_mix_g[l]), g_m, f"post_mix_{l}")
        st["x1"] = x1
        h2 = _norm_mod_fwd(x1, row(pre_ffn_g[l]), sh_f, sc_f, f"pre_ffn_{l}")
        gu = _mm(h2, (w_gu, l), "nn", BF16, f"ffn_gu_{l}")
        act = _swiglu_fwd(gu, f"ffn_act_{l}")
        y2 = _mm(act, (w_dn, l), "nn", F32, f"ffn_down_{l}")
        xc = _post_fwd(x1, y2, row(post_ffn_g[l]), g_f, f"post_ffn_{l}")
        st.update(h2=h2, gu=gu, act=act, y2=y2)
        saved.append(st)
        if l == NA - 1:
            hk = _norm_mod_fwd(xc, row(kv_norm_g), kv_sh, kv_sc, "kv_pre")
            kvf = _mm(hk, w_kv, "nn", F32, "kv_proj")
            k_raw = heads(kvf[:, :D]).reshape(H * S, hd)
            kn = _head_norm_fwd(k_raw, row(k_norm_g), 1.0, "kv_knorm").reshape(H, S, hd)
            vb = heads(kvf[:, D:2 * D]).astype(BF16)
            f_t = kvf[:, 2 * D:2 * D + H].T
            b_col = kv_b_f.reshape(H, 1)
            dcum = _dcum_fwd(f_t, b_col, "kv_dcum")
            va = (jnp.pad(vb, ((0, 0), (0, 0), (0, LANES - hd)))
                  + jnp.pad(jnp.ones((H, S, 1), BF16), ((0, 0), (0, 0), (hd, LANES - hd - 1))))
            kv = dict(x=xc, hk=hk, k_raw=k_raw, ka=_augment(kn, dcum, False), vb=vb, va=va, f_t=f_t, b_col=b_col, dcum=dcum)

    dx, loss_part = _loss_bwd(xc, tgt, "loss")
    loss = lax.psum(loss_part[0, 0], ("x", "y", "c"))

    gl = {n: [None] * weights[n].shape[0] for n in
          ["pre_mix_g", "post_mix_g", "pre_ffn_g", "post_ffn_g", "ffn_w_gu", "ffn_w_down", "a_w_in", "a_b_in", "a_ln_g",
           "a_ln_b", "a_w_s", "a_b_s", "a_w_out", "b_w_qg", "b_q_norm_g", "b_w_o"]}
    dmod = [None] * L
    dkn = dvb = ddc = None
    gkv = {}
    for l in reversed(range(L)):
        st = saved[l]
        sh_m, sc_m, g_m, sh_f, sc_f, g_f = mod[l]
        if l == NA - 1:
            dk_raw, gkv["k_norm_g"] = _head_norm_bwd(dkn.reshape(H * S, hd), kv["k_raw"], row(k_norm_g), 1.0, "kv_knorm_bwd")
            df_t, db_f = _dcum_bwd(ddc.reshape(H, S), kv["f_t"], kv["b_col"], "kv_dcum_bwd")
            dkvf = jnp.concatenate([unheads(dk_raw.reshape(H, S, hd)), unheads(dvb), df_t.T,
                                    jnp.zeros((S, kv_pad - (2 * D + H)), F32)], axis=1).astype(BF16)
            gkv["kv_w"] = _mm(kv["hk"], dkvf, "tn", BF16, "kv_proj_dw")[:, :2 * D + H]
            dhk = _mm(dkvf, w_kv, "nt", F32, "kv_proj_dx")
            dx, gkv["kv_norm_g"], dsh, dsc = _norm_mod_bwd(dx, dhk, kv["x"], row(kv_norm_g), kv_sh, kv_sc, "kv_pre_bwd")
            gkv["kv_b_f"] = db_f.reshape(H)
            dmod_kv = jnp.concatenate([dsh, dsc], axis=1)
        dy2, gl["post_ffn_g"][l], dg_f = _post_bwd(dx, st["y2"], row(post_ffn_g[l]), g_f, f"post_ffn_bwd_{l}")
        gl["ffn_w_down"][l] = _mm(st["act"], dy2, "tn", BF16, f"ffn_down_dw_{l}")
        dact = _mm(dy2, (w_dn, l), "nt", BF16, f"ffn_down_dx_{l}")
        dgu = _swiglu_bwd(dact, st["gu"], f"ffn_act_bwd_{l}")
        gl["ffn_w_gu"][l] = _mm(st["h2"], dgu, "tn", BF16, f"ffn_gu_dw_{l}")
        dh2 = _mm(dgu, (w_gu, l), "nt", F32, f"ffn_gu_dx_{l}")
        dx, gl["pre_ffn_g"][l], dsh_f, dsc_f = _norm_mod_bwd(dx, dh2, st["x1"], row(pre_ffn_g[l]), sh_f, sc_f, f"pre_ffn_bwd_{l}")
        dy, gl["post_mix_g"][l], dg_m = _post_bwd(dx, st["y"], row(post_mix_g[l]), g_m, f"post_mix_bwd_{l}")
        if l < NA:
            gl["a_w_out"][l] = _mm(st["yg"], dy, "tn", BF16, f"gmlp_out_dw_{l}")
            dyg = _mm(dy, (w_out, l), "nt", BF16, f"gmlp_out_dx_{l}")
            dzp, db_in, dlg, dlb, dws, dbs_t = _gmlp_bwd(dyg, st["zp"], row(b_in_f[l]), row(ln_g_f[l]), row(ln_b_f[l]),
                                                           ws_m[l], ws_mt[l], bs_t[l], f"gmlp_gate_bwd_{l}")
            gl["a_b_in"][l], gl["a_ln_g"][l], gl["a_ln_b"][l] = db_in[0], dlg[0], dlb[0]
            gl["a_w_s"][l], gl["a_b_s"][l] = dws * causal, dbs_t.T
            gl["a_w_in"][l] = _mm(st["h1"], dzp, "tn", BF16, f"gmlp_in_dw_{l}")
            dh1 = _mm(dzp, (w_in, l), "nt", F32, f"gmlp_in_dx_{l}")
        else:
            jb = l - NA
            gl["b_w_o"][jb] = _mm(st["og"], dy, "tn", BF16, f"fox_out_dw_{jb}")
            dog = _mm(dy, (w_o, jb), "nt", F32, f"fox_out_dx_{jb}")
            do_t, dgl = _out_gate_bwd(dog, st["o_t"], st["qg"], f"fox_gate_bwd_{jb}")
            do = heads(do_t)
            dqa, delta = _flash_bwd_q(st["qn"], kv["ka"], kv["vb"], do, st["o"], st["lse"], f"fox_attn_dq_{jb}")
            dqn = dqa[:, :, :hd]
            dk_j, dv_j, dd_k, dd_q = _flash_bwd_kv(st["qn"], kv["ka"], kv["vb"], do,
                                                   st["lse"].reshape(H, 1, S), delta.reshape(H, 1, S), f"fox_attn_dkv_{jb}")
            dk_j = dk_j[:, :, :hd]
            dd_j = dd_k.reshape(H, S) + dd_q.reshape(H, S)
            dkn = dk_j if dkn is None else dkn + dk_j
            dvb = dv_j if dvb is None else dvb + dv_j
            ddc = dd_j if ddc is None else ddc + dd_j
            dq_raw, dgq = _head_norm_bwd(dqn.reshape(H * S, hd), st["q_raw"], row(b_q_norm_g[jb]), hd ** -0.5, f"fox_qnorm_bwd_{jb}")
            gl["b_q_norm_g"][jb] = dgq[0]
            dqg = jnp.concatenate([unheads(dq_raw.reshape(H, S, hd)).astype(BF16), dgl], axis=1)
            gl["b_w_qg"][jb] = _mm(st["h1"], dqg, "tn", BF16, f"fox_qg_dw_{jb}")
            dh1 = _mm(dqg, (w_qg, jb), "nt", F32, f"fox_qg_dx_{jb}")
        dx, gl["pre_mix_g"][l], dsh_m, dsc_m = _norm_mod_bwd(dx, dh1, st["x0"], row(pre_mix_g[l]), sh_m, sc_m, f"pre_mix_bwd_{l}")
        dmod[l] = jnp.concatenate([dsh_m, dsc_m, dg_m, dsh_f, dsc_f, dg_f], axis=1)
    grad_x = dx[None]

    stack = lambda n: jnp.stack([t.reshape(weights[n].shape[1:]) for t in gl[n]])
    small = {"dmod": jnp.concatenate(dmod, axis=1), "dmod_kv": dmod_kv}
    for n in ["pre_mix_g", "post_mix_g", "pre_ffn_g", "post_ffn_g", "a_w_s", "a_b_s", "b_q_norm_g"]:
        small[n] = stack(n)
    for n in ["a_b_in", "a_ln_g", "a_ln_b"]:
        small[n] = jnp.stack(gl[n])
    for n in ["kv_norm_g", "kv_b_f", "k_norm_g"]:
        small[n] = gkv[n]
    sizes = {n: t.size for n, t in small.items()}
    flat = jnp.concatenate([t.reshape(-1).astype(F32) for t in small.values()])
    rows_small = -(-flat.size // (LANES * BF16_ROWS)) * BF16_ROWS
    flat = jnp.pad(flat, (0, rows_small * LANES - flat.size)).reshape(rows_small, LANES)
    flat_all = _gather8([flat], "gather_small")[0]
    flat_sum = _sum_slots(flat_all, "sum_small").reshape(-1)
    offs, o_ = {}, 0
    for n, sz in sizes.items():
        offs[n] = o_
        o_ += sz
    take = lambda n, shape: flat_sum[offs[n]:offs[n] + sizes[n]].reshape(shape)
    dmod_rows = flat_all.reshape(N_DEV, -1)[:, offs["dmod"]:offs["dmod"] + sizes["dmod"] + sizes["dmod_kv"]]
    dmod_rows = jnp.pad(dmod_rows, ((0, BF16_ROWS - N_DEV), (0, 0)))

    grads = {}
    grads["ada_b"] = take("dmod", (L, 6 * D))
    grads["kv_ada_b"] = take("dmod_kv", (2 * D,))
    for n in ["pre_mix_g", "post_mix_g", "pre_ffn_g", "post_ffn_g", "a_w_s", "a_b_s", "b_q_norm_g", "kv_norm_g", "kv_b_f", "k_norm_g"]:
        grads[n] = take(n, weights[n].shape)
    for n in ["a_b_in", "a_ln_g", "a_ln_b"]:
        full = take(n, small[n].shape)
        w = weights[n].shape[1]
        grads[n] = lax.dynamic_slice_in_dim(full, chip * w, w, axis=1)
    ada_g = []
    for l in range(L):
        cols = lax.dynamic_slice_in_dim(dmod_rows[:, l * 6 * D:(l + 1) * 6 * D], chip * ada_cols, ada_cols, axis=1)
        ada_g.append(_mm(c_act, cols, "tn", F32, f"mod_proj_dw_{l}"))
    grads["ada_w"] = jnp.stack(ada_g)
    cols = lax.dynamic_slice_in_dim(dmod_rows[:, L * 6 * D:], chip * kvada_cols, kvada_cols, axis=1)
    grads["kv_ada_w"] = _mm(c_act, cols, "tn", F32, "mod_proj_kv_dw")

    specs = {"ffn_w_gu": (2, 0), "ffn_w_down": (1, 0), "a_w_in": (2, 0), "a_w_out": (1, 0), "kv_w": (0, 1),
             "b_w_qg": (2, 0), "b_w_o": (1, 0)}
    full_g = {n: jnp.stack(gl[n]) for n in big if n != "kv_w"}
    full_g["kv_w"] = gkv["kv_w"].reshape(D, N_CHIPS, kv_cols).transpose(1, 0, 2)
    recv = _scatter8([full_g[n] for n in big], [specs[n] for n in big], "scatter_g")
    halves = [_sum_slots(r, f"sum_g_{n}") for n, r in zip(big, recv)]
    pairs = _sibling_pair(halves, "pair_g")
    for n, p in zip(big, pairs):
        grads[n] = p.reshape(weights[n].shape)

    outs_d, outs_m, outs_v = {}, {}, {}
    for n in names:
        w2 = weights[n] if weights[n].ndim > 1 else weights[n].reshape(1, -1)
        shp = w2.shape
        d_, m_, v_ = _adamw(w2, grads[n].reshape(shp), m_in[n].reshape(shp), v_in[n].reshape(shp), f"adamw_{n}")
        outs_d[n], outs_m[n], outs_v[n] = (t.reshape(weights[n].shape) for t in (d_, m_, v_))
    return (loss, grad_x, *[grads[n] for n in names], *[outs_d[n] for n in names],
            *[outs_m[n] for n in names], *[outs_v[n] for n in names])
```

```python
import functools

import jax
import jax.numpy as jnp
from jax import lax
from jax.experimental import pallas as pl
from jax.experimental.pallas import tpu as pltpu

F32 = jnp.float32
BF16 = jnp.bfloat16
MESH = pl.DeviceIdType.MESH
NORM_EPS = 1e-6
MASKED = -1e30
LANES = 128
BF16_ROWS = 16
ROW_BLOCK_BYTES = 12 << 20
ADAM_LR, ADAM_B1, ADAM_B2, ADAM_EPS, ADAM_WD, ADAM_STEP = 0.001, 0.9, 0.999, 1e-08, 0.01, 10
N_CHIPS, N_CORES, N_DEV = 4, 2, 8
ATTN_HEADS_PER_STEP = 2
ANY = pl.BlockSpec(memory_space=pl.ANY)


def _tile(n, cap, quantum):
    best = None
    d = quantum
    while d <= min(n, cap):
        if n % d == 0:
            best = d
        d += quantum
    return n if best is None else best


def _call(body, *, name, out_shape, grid=(), in_specs=None, out_specs=None, scratch=(), sem=None, aliases=None):
    params = {} if sem is None else {"dimension_semantics": sem}
    return pl.pallas_call(
        body, name=name, grid=grid, in_specs=in_specs, out_specs=out_specs, out_shape=out_shape,
        scratch_shapes=list(scratch), input_output_aliases=aliases or {},
        compiler_params=pltpu.CompilerParams(**params))


def _call_prefetch(body, *, name, out_shape, grid, n_prefetch, in_specs, out_specs, scratch, sem):
    spec = pltpu.PrefetchScalarGridSpec(num_scalar_prefetch=n_prefetch, grid=grid, in_specs=in_specs,
                                        out_specs=out_specs, scratch_shapes=list(scratch))
    return pl.pallas_call(
        body, name=name, grid_spec=spec, out_shape=out_shape,
        compiler_params=pltpu.CompilerParams(dimension_semantics=sem))


def _place():
    x, y, c = lax.axis_index("x"), lax.axis_index("y"), lax.axis_index("c")
    return x, y, c


def _mm(a, b, mode, out_dtype, name):
    b_arr, b_idx = b if isinstance(b, tuple) else (b, None)
    bs = b_arr.shape[-2:]
    if mode == "nn":
        (M, K), (K2, N) = a.shape, bs
        dims = (((1,), (0,)), ((), ()))
    elif mode == "nt":
        (M, K), (N, K2) = a.shape, bs
        dims = (((1,), (1,)), ((), ()))
    else:
        (K, M), (K2, N) = a.shape, bs
        dims = (((0,), (0,)), ((), ()))
    assert K == K2, (name, a.shape, b_arr.shape)
    if mode == "tn":
        tm = _tile(M, 1408, LANES)
        tk = _tile(K, 2048, BF16_ROWS)
        tn = _tile(N, 512, LANES)
    else:
        tm = _tile(M, 1024, BF16_ROWS)
        tk = K if K <= 2816 else _tile(K, 2816, LANES)
        tn = _tile(N, 1408 if tk <= 1024 else 512, LANES)
    if tn < 256:
        tn = N
        tm = _tile(M, 512, LANES if mode == "tn" else BF16_ROWS)
    nk = K // tk
    grid = (M // tm, N // tn, nk)

    if mode == "tn":
        a_spec = pl.BlockSpec((tk, tm), lambda i, j, k: (k, i))
    else:
        a_spec = pl.BlockSpec((tm, tk), lambda i, j, k: (i, k))
    if mode == "nt":
        b_blk, b_map = (tn, tk), (lambda i, j, k: (j, k))
    else:
        b_blk, b_map = (tk, tn), (lambda i, j, k: (k, j))
    if b_idx is None:
        b_spec = pl.BlockSpec(b_blk, b_map)
    else:
        b_spec = pl.BlockSpec((None,) + b_blk, lambda i, j, k: (b_idx,) + b_map(i, j, k))

    def body(a_ref, b_ref, o_ref, *acc):
        r = lax.dot_general(a_ref[...].astype(BF16), b_ref[...].astype(BF16), dims, preferred_element_type=F32)
        if nk == 1:
            o_ref[...] = r.astype(o_ref.dtype)
        else:
            k = pl.program_id(2)

            @pl.when(k == 0)
            def _():
                acc[0][...] = r

            @pl.when(k > 0)
            def _():
                acc[0][...] += r

            @pl.when(k == nk - 1)
            def _():
                o_ref[...] = acc[0][...].astype(o_ref.dtype)

    return _call(
        body, name=name, grid=grid, in_specs=[a_spec, b_spec],
        out_specs=pl.BlockSpec((tm, tn), lambda i, j, k: (i, j)),
        out_shape=jax.ShapeDtypeStruct((M, N), out_dtype),
        scratch=[pltpu.VMEM((tm, tn), F32)] if nk > 1 else [],
        sem=("parallel", "parallel", "arbitrary"))(a, b_arr)


def _rowwise(fn, rows, pars, outs, pouts, name):
    R = rows[0].shape[0]
    row_bytes = 4 * (sum(max(r.shape[1], LANES) for r in rows) + sum(max(c, LANES) for c, _ in outs))
    tb = _tile(R, max(BF16_ROWS, ROW_BLOCK_BYTES // row_bytes), BF16_ROWS)
    nr, npar, no = len(rows), len(pars), len(outs)

    def body(*refs):
        r_in, p_in = refs[:nr], refs[nr:nr + npar]
        r_out, p_out = refs[nr + npar:nr + npar + no], refs[nr + npar + no:]
        ro, po = fn([r[...] for r in r_in], [p[...] for p in p_in])
        for ref, val in zip(r_out, ro):
            if isinstance(val, (tuple, list)):
                off = 0
                for piece in val:
                    w = piece.shape[1]
                    ref[:, off:off + w] = piece.astype(ref.dtype)
                    off += w
            else:
                ref[...] = val.astype(ref.dtype)
        if p_out:
            first = pl.program_id(0) == 0

            @pl.when(first)
            def _():
                for ref, val in zip(p_out, po):
                    ref[...] = val

            @pl.when(jnp.logical_not(first))
            def _():
                for ref, val in zip(p_out, po):
                    ref[...] += val

    res = _call(
        body, name=name, grid=(R // tb,),
        in_specs=[pl.BlockSpec((tb, r.shape[1]), lambda i: (i, 0)) for r in rows]
        + [pl.BlockSpec(p.shape, lambda i: (0, 0)) for p in pars],
        out_specs=[pl.BlockSpec((tb, c), lambda i: (i, 0)) for c, _ in outs]
        + [pl.BlockSpec(s, lambda i: (0, 0)) for s in pouts],
        out_shape=[jax.ShapeDtypeStruct((R, c), dt) for c, dt in outs]
        + [jax.ShapeDtypeStruct(s, F32) for s in pouts],
        sem=("arbitrary",) if pouts else ("parallel",))(*rows, *pars)
    return list(res)


def _rms(x, g):
    return x * lax.rsqrt(jnp.mean(x * x, axis=-1, keepdims=True) + NORM_EPS) * g


def _norm_mod(x, g, sh, sc):
    return _rms(x, g) * (1.0 + sc) + sh


def _gated_post(y, g, gate):
    return gate * _rms(y, g)


def _norm_mod_fwd(x, g, sh, sc, name):
    return _rowwise(lambda r, p: ([_norm_mod(r[0], *p)], []), [x], [g, sh, sc], [(x.shape[1], BF16)], [], name)[0]


def _norm_mod_bwd(dxo, dh, x, g, sh, sc, name):
    def fn(r, p):
        _, vjp = jax.vjp(_norm_mod, r[2], *p)
        dx, dg, dsh, dsc = vjp(r[1].astype(F32))
        return [r[0] + dx], [dg, dsh, dsc]
    c = x.shape[1]
    return _rowwise(fn, [dxo, dh, x], [g, sh, sc], [(c, F32)], [(1, c)] * 3, name)


def _post_fwd(x, y, g, gate, name):
    return _rowwise(lambda r, p: ([r[0] + _gated_post(r[1].astype(F32), *p)], []), [x, y], [g, gate],
                    [(x.shape[1], F32)], [], name)[0]


def _post_bwd(dxo, y, g, gate, name):
    def fn(r, p):
        _, vjp = jax.vjp(_gated_post, r[1].astype(F32), *p)
        dy, dg, dgate = vjp(r[0])
        return [dy], [dg, dgate]
    c = y.shape[1]
    return _rowwise(fn, [dxo, y], [g, gate], [(c, BF16)], [(1, c)] * 2, name)


def _swiglu(g, u):
    return jax.nn.silu(g) * u


def _swiglu_fwd(gu, name):
    f = gu.shape[1] // 2
    return _rowwise(lambda r, p: ([_swiglu(r[0][:, :f].astype(F32), r[0][:, f:].astype(F32))], []), [gu], [],
                    [(f, BF16)], [], name)[0]


def _swiglu_bwd(da, gu, name):
    f = gu.shape[1] // 2

    def fn(r, p):
        _, vjp = jax.vjp(_swiglu, r[1][:, :f].astype(F32), r[1][:, f:].astype(F32))
        return [vjp(r[0].astype(F32))], []
    return _rowwise(fn, [da, gu], [], [(2 * f, BF16)], [], name)[0]


def _silu_rows(c, name):
    return _rowwise(lambda r, p: ([jax.nn.silu(r[0])], []), [c], [], [(c.shape[1], F32)], [], name)[0]


def _head_norm(x, g, scale):
    return _rms(x, g) * scale


def _head_norm_fwd(x, g, scale, name):
    return _rowwise(lambda r, p: ([_head_norm(r[0].astype(F32), p[0], scale)], []), [x], [g],
                    [(x.shape[1], BF16)], [], name)[0]


def _head_norm_bwd(dy, x, g, scale, name):
    def fn(r, p):
        _, vjp = jax.vjp(lambda t, gg: _head_norm(t, gg, scale), r[1].astype(F32), p[0])
        dx, dg = vjp(r[0])
        return [dx], [dg]
    c = x.shape[1]
    return _rowwise(fn, [dy, x], [g], [(c, F32)], [(1, c)], name)


def _out_gate_fwd(o, qg, name):
    d = o.shape[1]
    return _rowwise(lambda r, p: ([r[0] * jax.nn.sigmoid(r[1][:, d:].astype(F32))], []), [o, qg], [],
                    [(d, BF16)], [], name)[0]


def _out_gate_bwd(dog, o, qg, name):
    d = o.shape[1]

    def fn(r, p):
        _, vjp = jax.vjp(lambda oo, gl: oo * jax.nn.sigmoid(gl), r[1], r[2][:, d:].astype(F32))
        do, dgl = vjp(r[0])
        return [do, dgl], []
    return _rowwise(fn, [dog, o, qg], [], [(d, BF16), (d, BF16)], [], name)


def _loss_bwd(y, tgt, name):
    n = y.shape[1]

    def fn(r, p):
        e = r[0] - r[1]
        part = jnp.sum(jnp.sum(e * e, axis=1, keepdims=True), axis=0, keepdims=True) * (0.5 / n)
        return [e * (1.0 / n)], [part]
    return _rowwise(fn, [y, tgt], [], [(n, F32)], [(1, 1)], name)


def _adamw(w, g, m, v, name):
    shape = w.shape
    c = shape[-1]
    flat = [t.reshape(-1, c) for t in (w, g, m, v)]

    def fn(r, p):
        w_, g_, m_, v_ = r
        m2 = ADAM_B1 * m_ + (1.0 - ADAM_B1) * g_
        v2 = ADAM_B2 * v_ + (1.0 - ADAM_B2) * (g_ * g_)
        m_hat = m2 / (1.0 - ADAM_B1 ** ADAM_STEP)
        v_hat = v2 / (1.0 - ADAM_B2 ** ADAM_STEP)
        delta = -ADAM_LR * (m_hat / (jnp.sqrt(v_hat) + ADAM_EPS) + ADAM_WD * w_)
        return [delta, m2, v2], []
    res = _rowwise(fn, flat, [], [(c, F32)] * 3, [], name)
    return [t.reshape(shape) for t in res]


def _sum_slots(recv, name):
    n = recv.shape[0]
    shape = recv.shape[1:]
    c = shape[-1]
    r3 = recv.reshape(n, -1, c)
    rows = r3.shape[1]
    tb = _tile(rows, max(BF16_ROWS, ROW_BLOCK_BYTES // (4 * c * (n + 1))), BF16_ROWS)

    def body(r_ref, o_ref):
        acc = r_ref[0].astype(F32)
        for s in range(1, n):
            acc = acc + r_ref[s].astype(F32)
        o_ref[...] = acc

    out = _call(body, name=name, grid=(rows // tb,),
                in_specs=[pl.BlockSpec((n, tb, c), lambda i: (0, i, 0))],
                out_specs=pl.BlockSpec((tb, c), lambda i: (i, 0)),
                out_shape=jax.ShapeDtypeStruct((rows, c), F32), sem=("parallel",))(r3)
    return out.reshape(shape)


def _gmlp_pre(zu, zv, b_u, b_v, ln_g, ln_b):
    u = jax.nn.gelu(zu + b_u, approximate=True)
    v = jax.nn.gelu(zv + b_v, approximate=True)
    xc = v - jnp.mean(v, axis=-1, keepdims=True)
    vn = xc * lax.rsqrt(jnp.mean(xc * xc, axis=-1, keepdims=True) + NORM_EPS) * ln_g + ln_b
    return u, vn


def _gmlp_fwd(zp, b_in, ln_g, ln_b, ws, bs_t, name):
    S, gw2 = zp.shape
    gw = gw2 // 2
    G, ch, _ = ws.shape
    gd = gw // G
    tb = 2 * ch

    def body(zp_ref, bin_ref, lg_ref, lb_ref, ws_ref, bs_ref, o_ref):
        u, vn = _gmlp_pre(zp_ref[:, :gw].astype(F32), zp_ref[:, gw:].astype(F32), bin_ref[:, :gw], bin_ref[:, gw:],
                          lg_ref[...], lb_ref[...])
        vnb = vn.astype(BF16)
        for c in range(tb // ch):
            for g in range(G):
                rs, cs = slice(c * ch, (c + 1) * ch), slice(g * gd, (g + 1) * gd)
                vv = jnp.dot(ws_ref[g], vnb[rs, cs], preferred_element_type=F32) + bs_ref[:, g:g + 1]
                o_ref[rs, cs] = (u[rs, cs] * vv).astype(o_ref.dtype)

    full = lambda a: pl.BlockSpec(a.shape, lambda i: (0,) * a.ndim)
    return _call(body, name=name, grid=(S // tb,),
                 in_specs=[pl.BlockSpec((tb, gw2), lambda i: (i, 0)), full(b_in), full(ln_g), full(ln_b), full(ws), full(bs_t)],
                 out_specs=pl.BlockSpec((tb, gw), lambda i: (i, 0)),
                 out_shape=jax.ShapeDtypeStruct((S, gw), BF16), sem=("parallel",))(zp, b_in, ln_g, ln_b, ws, bs_t)


def _gmlp_bwd(dyg, zp, b_in, ln_g, ln_b, ws, ws_t, bs_t, name):
    S, gw2 = zp.shape
    gw = gw2 // 2
    G, ch, _ = ws.shape
    gd = gw // G
    tb = 2 * ch

    def body(dy_ref, zp_ref, bin_ref, lg_ref, lb_ref, ws_ref, wst_ref, bs_ref,
             dzp_ref, dbin_ref, dlg_ref, dlb_ref, dws_ref, dbs_ref, du_sc, dvn_sc):
        (u, vn), vjp = jax.vjp(_gmlp_pre, zp_ref[:, :gw].astype(F32), zp_ref[:, gw:].astype(F32), bin_ref[:, :gw],
                               bin_ref[:, gw:], lg_ref[...], lb_ref[...])
        vnb = vn.astype(BF16)
        first = pl.program_id(0) == 0

        @pl.when(first)
        def _():
            dws_ref[...] = jnp.zeros_like(dws_ref)

        lane = lax.broadcasted_iota(jnp.int32, (ch, G), 1)
        dbs = jnp.zeros((ch, G), F32)
        for g in range(G):
            cs = slice(g * gd, (g + 1) * gd)
            dws_g = jnp.zeros((ch, ch), F32)
            col = jnp.zeros((ch, 1), F32)
            for c in range(tb // ch):
                rs = slice(c * ch, (c + 1) * ch)
                vnp = vnb[rs, cs]
                vv = jnp.dot(ws_ref[g], vnp, preferred_element_type=F32) + bs_ref[:, g:g + 1]
                dy = dy_ref[rs, cs].astype(F32)
                du_sc[rs, cs] = dy * vv
                dvv = dy * u[rs, cs]
                dvvb = dvv.astype(BF16)
                dvn_sc[rs, cs] = jnp.dot(wst_ref[g], dvvb, preferred_element_type=F32)
                dws_g = dws_g + lax.dot_general(dvvb, vnp, (((1,), (1,)), ((), ())), preferred_element_type=F32)
                col = col + jnp.sum(dvv, axis=1, keepdims=True)
            dws_ref[g] += dws_g
            dbs = jnp.where(lane == g, col, dbs)
        dzu, dzv, dbu, dbv, dlg, dlb = vjp((du_sc[...], dvn_sc[...]))
        dzp_ref[:, :gw] = dzu.astype(dzp_ref.dtype)
        dzp_ref[:, gw:] = dzv.astype(dzp_ref.dtype)

        @pl.when(first)
        def _():
            dbin_ref[:, :gw] = dbu
            dbin_ref[:, gw:] = dbv
            dlg_ref[...] = dlg
            dlb_ref[...] = dlb
            dbs_ref[...] = dbs

        @pl.when(jnp.logical_not(first))
        def _():
            dbin_ref[:, :gw] += dbu
            dbin_ref[:, gw:] += dbv
            dlg_ref[...] += dlg
            dlb_ref[...] += dlb
            dbs_ref[...] += dbs

    full = lambda a: pl.BlockSpec(a.shape, lambda i: (0,) * a.ndim)
    fshape = lambda s: pl.BlockSpec(s, lambda i: (0,) * len(s))
    return _call(
        body, name=name, grid=(S // tb,),
        in_specs=[pl.BlockSpec((tb, gw), lambda i: (i, 0)), pl.BlockSpec((tb, gw2), lambda i: (i, 0)),
                  full(b_in), full(ln_g), full(ln_b), full(ws), full(ws_t), full(bs_t)],
        out_specs=[pl.BlockSpec((tb, gw2), lambda i: (i, 0)), fshape((1, gw2)), fshape((1, gw)), fshape((1, gw)),
                   fshape((G, ch, ch)), fshape((ch, G))],
        out_shape=[jax.ShapeDtypeStruct((S, gw2), BF16), jax.ShapeDtypeStruct((1, gw2), F32),
                   jax.ShapeDtypeStruct((1, gw), F32), jax.ShapeDtypeStruct((1, gw), F32),
                   jax.ShapeDtypeStruct((G, ch, ch), F32), jax.ShapeDtypeStruct((ch, G), F32)],
        scratch=[pltpu.VMEM((tb, gw), F32), pltpu.VMEM((tb, gw), F32)],
        sem=("arbitrary",))(dyg, zp, b_in, ln_g, ln_b, ws, ws_t, bs_t)


def _dot_01(x, ones_bf16):
    hi = x.astype(BF16)
    r1 = x - hi.astype(F32)
    mid = r1.astype(BF16)
    lo = (r1 - mid.astype(F32)).astype(BF16)
    dot = lambda t: jnp.dot(t, ones_bf16, preferred_element_type=F32)
    return dot(hi) + dot(mid) + dot(lo)


def _log_sigmoid(x):
    return jnp.minimum(x, 0.0) - jnp.log1p(jnp.exp(-jnp.abs(x)))


def _dcum_fwd(f_t, b_col, name):
    H, S = f_t.shape
    tb = _tile(S, 512, LANES)

    def body(f_ref, b_ref, o_ref, carry):
        @pl.when(pl.program_id(0) == 0)
        def _():
            carry[...] = jnp.zeros_like(carry)

        ls = _log_sigmoid(f_ref[...] + b_ref[...])
        r = lax.broadcasted_iota(jnp.int32, (tb, tb), 0)
        c = lax.broadcasted_iota(jnp.int32, (tb, tb), 1)
        upper = (r <= c).astype(BF16)
        o_ref[...] = _dot_01(ls, upper) + carry[...]
        carry[...] += jnp.sum(ls, axis=1, keepdims=True)

    return _call(body, name=name, grid=(S // tb,),
                 in_specs=[pl.BlockSpec((H, tb), lambda i: (0, i)), pl.BlockSpec((H, 1), lambda i: (0, 0))],
                 out_specs=pl.BlockSpec((H, tb), lambda i: (0, i)),
                 out_shape=jax.ShapeDtypeStruct((H, S), F32),
                 scratch=[pltpu.VMEM((H, 1), F32)], sem=("arbitrary",))(f_t, b_col)


def _dcum_bwd(dd_t, f_t, b_col, name):
    H, S = f_t.shape
    tb = _tile(S, 512, LANES)
    nb = S // tb

    def body(dd_ref, f_ref, b_ref, df_ref, db_ref, carry):
        first = pl.program_id(0) == 0

        @pl.when(first)
        def _():
            carry[...] = jnp.zeros_like(carry)

        dd = dd_ref[...]
        r = lax.broadcasted_iota(jnp.int32, (tb, tb), 0)
        c = lax.broadcasted_iota(jnp.int32, (tb, tb), 1)
        lower = (r >= c).astype(BF16)
        rev = _dot_01(dd, lower) + carry[...]
        carry[...] += jnp.sum(dd, axis=1, keepdims=True)
        df = rev * jax.nn.sigmoid(-(f_ref[...] + b_ref[...]))
        df_ref[...] = df
        part = jnp.sum(df, axis=1, keepdims=True)

        @pl.when(first)
        def _():
            db_ref[...] = part

        @pl.when(jnp.logical_not(first))
        def _():
            db_ref[...] += part

    return _call(body, name=name, grid=(nb,),
                 in_specs=[pl.BlockSpec((H, tb), lambda i: (0, nb - 1 - i)), pl.BlockSpec((H, tb), lambda i: (0, nb - 1 - i)),
                           pl.BlockSpec((H, 1), lambda i: (0, 0))],
                 out_specs=[pl.BlockSpec((H, tb), lambda i: (0, nb - 1 - i)), pl.BlockSpec((H, 1), lambda i: (0, 0))],
                 out_shape=[jax.ShapeDtypeStruct((H, S), F32), jax.ShapeDtypeStruct((H, 1), F32)],
                 scratch=[pltpu.VMEM((H, 1), F32)], sem=("arbitrary",))(dd_t, f_t, b_col)


def _attn_tile(S):
    return _tile(S, 512, LANES)


def _causal(t, transposed):
    r = lax.broadcasted_iota(jnp.int32, (t, t), 0)
    c = lax.broadcasted_iota(jnp.int32, (t, t), 1)
    return (r <= c) if transposed else (c <= r)


def _tri_pairs(n, key_major):
    if key_major:
        pairs = [(i, j) for j in range(n) for i in range(j, n)]
    else:
        pairs = [(i, j) for i in range(n) for j in range(i + 1)]
    return jnp.asarray([p[0] for p in pairs], jnp.int32), jnp.asarray([p[1] for p in pairs], jnp.int32)


def _split3(x):
    hi = lax.reduce_precision(x, 8, 7)
    r = x - hi
    mid = lax.reduce_precision(r, 8, 7)
    lo = lax.reduce_precision(r - mid, 8, 7)
    return hi.astype(BF16), mid.astype(BF16), lo.astype(BF16)


def _augment(xn, dcum, query):
    H, S, hd = xn.shape
    parts = list(_split3(dcum))
    vals = parts + [1.0] * 3 if query else [1.0] * 3 + [-p for p in parts]
    lane = lax.broadcasted_iota(jnp.int32, (1, 1, LANES), 2)
    out = jnp.pad(xn, ((0, 0), (0, 0), (0, LANES - hd)))
    for k, val in enumerate(vals):
        val = jnp.asarray(val, BF16)
        out = jnp.where(lane == hd + k, val[..., None] if val.ndim else val, out)
    return out


def _scores_t(k_ref, qt_ref, h, t, diag):
    st = jnp.dot(k_ref[h], qt_ref[h], preferred_element_type=F32)
    return jnp.where(_causal(t, True), st, MASKED) if diag else st


def _flash_fwd(ka, qat, vat, hd, name):
    H, S, da = ka.shape
    t = _attn_tile(S)
    hb = ATTN_HEADS_PER_STEP
    it, jt = _tri_pairs(S // t, False)

    def body(it_ref, jt_ref, k_ref, qt_ref, vt_ref, o_ref, lse_ref, m_sc, acc_sc):
        i, j = it_ref[pl.program_id(1)], jt_ref[pl.program_id(1)]

        @pl.when(j == 0)
        def _():
            m_sc[...] = jnp.full_like(m_sc, MASKED)
            acc_sc[...] = jnp.zeros_like(acc_sc)

        def step(diag):
            for h in range(hb):
                st = _scores_t(k_ref, qt_ref, h, t, diag)
                m_prev = m_sc[h]
                m_new = jnp.maximum(m_prev, jnp.max(st, axis=0, keepdims=True))
                pt = jnp.exp(st - m_new).astype(BF16)
                acc_sc[h] = jnp.exp(m_prev - m_new) * acc_sc[h] + jnp.dot(vt_ref[h], pt, preferred_element_type=F32)
                m_sc[h] = m_new

        @pl.when(j < i)
        def _():
            step(False)

        @pl.when(j == i)
        def _():
            step(True)
            for h in range(hb):
                l = acc_sc[h, hd:hd + 1, :]
                o_ref[h] = acc_sc[h, :hd, :] / l
                lse_ref[h] = m_sc[h] + jnp.log(l)

    qcol = lambda h, p, it_, jt_: (h, 0, it_[p])
    kcol = lambda h, p, it_, jt_: (h, 0, jt_[p])
    krow = lambda h, p, it_, jt_: (h, jt_[p], 0)
    return _call_prefetch(
        body, name=name, grid=(H // hb, it.shape[0]), n_prefetch=2,
        in_specs=[pl.BlockSpec((hb, t, da), krow), pl.BlockSpec((hb, da, t), qcol), pl.BlockSpec((hb, da, t), kcol)],
        out_specs=[pl.BlockSpec((hb, hd, t), qcol), pl.BlockSpec((hb, 1, t), qcol)],
        out_shape=[jax.ShapeDtypeStruct((H, hd, S), F32), jax.ShapeDtypeStruct((H, 1, S), F32)],
        scratch=[pltpu.VMEM((hb, 1, t), F32), pltpu.VMEM((hb, da, t), F32)],
        sem=("parallel", "arbitrary"))(it, jt, ka, qat, vat)


def _ds_t(k_ref, qt_ref, v_ref, dot_ref, lse_ref, dl, h, t, diag):
    pt = jnp.exp(_scores_t(k_ref, qt_ref, h, t, diag) - lse_ref[h])
    dpt = jnp.dot(v_ref[h], dot_ref[h], preferred_element_type=F32)
    return pt, pt * (dpt - dl)


def _flash_bwd_q(ka, kat, qat, v, dot, o_tr, lse_r, name):
    H, S, hd = v.shape
    da = ka.shape[2]
    t = _attn_tile(S)
    hb = ATTN_HEADS_PER_STEP
    it, jt = _tri_pairs(S // t, False)

    def body(it_ref, jt_ref, k_ref, kt_ref, qt_ref, v_ref, dot_ref, o_ref, lse_ref, dq_ref, dl_ref, acc_sc, dl_sc):
        i, j = it_ref[pl.program_id(1)], jt_ref[pl.program_id(1)]

        @pl.when(j == 0)
        def _():
            acc_sc[...] = jnp.zeros_like(acc_sc)
            for h in range(hb):
                dl_sc[h] = jnp.sum(dot_ref[h].astype(F32) * o_ref[h], axis=0, keepdims=True)

        def step(diag):
            for h in range(hb):
                _, dst = _ds_t(k_ref, qt_ref, v_ref, dot_ref, lse_ref, dl_sc[h], h, t, diag)
                acc_sc[h] += jnp.dot(kt_ref[h], dst.astype(BF16), preferred_element_type=F32)

        @pl.when(j < i)
        def _():
            step(False)

        @pl.when(j == i)
        def _():
            step(True)
            dq_ref[...] = acc_sc[...]
            dl_ref[...] = dl_sc[...]

    qcol = lambda h, p, it_, jt_: (h, 0, it_[p])
    kcol = lambda h, p, it_, jt_: (h, 0, jt_[p])
    krow = lambda h, p, it_, jt_: (h, jt_[p], 0)
    return _call_prefetch(
        body, name=name, grid=(H // hb, it.shape[0]), n_prefetch=2,
        in_specs=[pl.BlockSpec((hb, t, da), krow), pl.BlockSpec((hb, da, t), kcol), pl.BlockSpec((hb, da, t), qcol),
                  pl.BlockSpec((hb, t, hd), krow), pl.BlockSpec((hb, hd, t), qcol), pl.BlockSpec((hb, hd, t), qcol),
                  pl.BlockSpec((hb, 1, t), qcol)],
        out_specs=[pl.BlockSpec((hb, da, t), qcol), pl.BlockSpec((hb, 1, t), qcol)],
        out_shape=[jax.ShapeDtypeStruct((H, da, S), F32), jax.ShapeDtypeStruct((H, 1, S), F32)],
        scratch=[pltpu.VMEM((hb, da, t), F32), pltpu.VMEM((hb, 1, t), F32)],
        sem=("parallel", "arbitrary"))(it, jt, ka, kat, qat, v, dot, o_tr, lse_r)


def _flash_bwd_kv(ka, qa, qat, v, do, dot, lse_r, dl_r, name):
    H, S, hd = v.shape
    da = ka.shape[2]
    t = _attn_tile(S)
    n = S // t
    hb = ATTN_HEADS_PER_STEP
    it, jt = _tri_pairs(n, True)

    def body(it_ref, jt_ref, k_ref, q_ref, qt_ref, v_ref, do_ref, dot_ref, lse_ref, dl_ref,
             dk_ref, dv_ref, dd_ref, ddq_ref, dk_sc, dv_sc, dd_sc):
        i, j = it_ref[pl.program_id(1)], jt_ref[pl.program_id(1)]

        @pl.when(pl.program_id(1) == 0)
        def _():
            ddq_ref[...] = jnp.zeros_like(ddq_ref)

        def step(diag):
            for h in range(hb):
                pt, dst = _ds_t(k_ref, qt_ref, v_ref, dot_ref, lse_ref, dl_ref[h], h, t, diag)
                dv_sc[h] += jnp.dot(pt.astype(BF16), do_ref[h], preferred_element_type=F32)
                dk_sc[h] += jnp.dot(dst.astype(BF16), q_ref[h], preferred_element_type=F32)
                part = dst[:, :LANES]
                for c in range(1, t // LANES):
                    part = part + dst[:, c * LANES:(c + 1) * LANES]
                dd_sc[h] += part
                ddq_ref[h, i] += jnp.sum(dst, axis=0, keepdims=True)

        @pl.when(i == j)
        def _():
            dk_sc[...] = jnp.zeros_like(dk_sc)
            dv_sc[...] = jnp.zeros_like(dv_sc)
            dd_sc[...] = jnp.zeros_like(dd_sc)
            step(True)

        @pl.when(i > j)
        def _():
            step(False)

        @pl.when(i == n - 1)
        def _():
            dk_ref[...] = dk_sc[...]
            dv_ref[...] = dv_sc[...]
            for h in range(hb):
                dd_ref[h] = -jnp.sum(dd_sc[h], axis=1, keepdims=True)

    krow = lambda h, p, it_, jt_: (h, jt_[p], 0)
    qrow = lambda h, p, it_, jt_: (h, it_[p], 0)
    qcol = lambda h, p, it_, jt_: (h, 0, it_[p])
    return _call_prefetch(
        body, name=name, grid=(H // hb, it.shape[0]), n_prefetch=2,
        in_specs=[pl.BlockSpec((hb, t, da), krow), pl.BlockSpec((hb, t, da), qrow), pl.BlockSpec((hb, da, t), qcol),
                  pl.BlockSpec((hb, t, hd), krow), pl.BlockSpec((hb, t, hd), qrow), pl.BlockSpec((hb, hd, t), qcol),
                  pl.BlockSpec((hb, 1, t), qcol), pl.BlockSpec((hb, 1, t), qcol)],
        out_specs=[pl.BlockSpec((hb, t, da), krow), pl.BlockSpec((hb, t, hd), krow), pl.BlockSpec((hb, t, 1), krow),
                   pl.BlockSpec((hb, n, 1, t), lambda h, p, it_, jt_: (h, 0, 0, 0))],
        out_shape=[jax.ShapeDtypeStruct((H, S, da), F32), jax.ShapeDtypeStruct((H, S, hd), F32),
                   jax.ShapeDtypeStruct((H, S, 1), F32), jax.ShapeDtypeStruct((H, n, 1, t), F32)],
        scratch=[pltpu.VMEM((hb, t, da), F32), pltpu.VMEM((hb, t, hd), F32), pltpu.VMEM((hb, t, LANES), F32)],
        sem=("parallel", "arbitrary"))(it, jt, ka, qa, qat, v, do, dot, lse_r, dl_r)


def _offsets(n_bits):
    return [tuple((k >> b) & 1 for b in reversed(range(n_bits))) for k in range(1, 1 << n_bits)]


def _gather8(arrs, name):
    n = len(arrs)
    offs = _offsets(3)

    def body(*refs):
        ins, outs = refs[:n], refs[n:2 * n]
        ssem, rsem, lsem = refs[2 * n:]
        x, y, c = _place()
        me = 4 * x + 2 * y + c
        copies = []
        for a in range(n):
            lc = pltpu.make_async_copy(ins[a], outs[a].at[me], lsem.at[a])
            lc.start()
            copies.append(lc)
            for k, (dx, dy, dcc) in enumerate(offs):
                cp = pltpu.make_async_remote_copy(
                    src_ref=ins[a], dst_ref=outs[a].at[me], send_sem=ssem.at[a, k], recv_sem=rsem.at[a, k],
                    device_id=((x + dx) % 2, (y + dy) % 2, (c + dcc) % 2), device_id_type=MESH)
                cp.start()
                copies.append(cp)
        for cp in copies:
            cp.wait()

    return _call(body, name=name, in_specs=[ANY] * n, out_specs=[ANY] * n,
                 out_shape=[jax.ShapeDtypeStruct((N_DEV,) + a.shape, a.dtype) for a in arrs],
                 scratch=[pltpu.SemaphoreType.DMA((n, 7)), pltpu.SemaphoreType.DMA((n, 7)), pltpu.SemaphoreType.DMA((n,))])(*arrs)


def _chip_gather(arrs, halved, name):
    n = len(arrs)
    offs = _offsets(2)

    def body(*refs):
        ins, outs = refs[:n], refs[n:2 * n]
        ssem, rsem, lsem = refs[2 * n:]
        x, y, c = _place()
        chip = 2 * x + y
        copies = []
        for a in range(n):
            lc = pltpu.make_async_copy(ins[a], outs[a].at[chip], lsem.at[a])
            lc.start()
            copies.append(lc)
            if halved:
                hn = arrs[a].shape[0] // 2
                src = ins[a].at[pl.ds(c * hn, hn)]
                dst = outs[a].at[chip, pl.ds(c * hn, hn)]
            else:
                src, dst = ins[a], outs[a].at[chip]
            for k, (dx, dy) in enumerate(offs):
                cp = pltpu.make_async_remote_copy(
                    src_ref=src, dst_ref=dst, send_sem=ssem.at[a, k], recv_sem=rsem.at[a, k],
                    device_id=((x + dx) % 2, (y + dy) % 2, c), device_id_type=MESH)
                cp.start()
                copies.append(cp)
        for cp in copies:
            cp.wait()

    return _call(body, name=name, in_specs=[ANY] * n, out_specs=[ANY] * n,
                 out_shape=[jax.ShapeDtypeStruct((N_CHIPS,) + a.shape, a.dtype) for a in arrs],
                 scratch=[pltpu.SemaphoreType.DMA((n, 3)), pltpu.SemaphoreType.DMA((n, 3)), pltpu.SemaphoreType.DMA((n,))])(*arrs)


def _sibling_fill(bufs, name):
    n = len(bufs)
    offs = _offsets(2)

    def body(*refs):
        ins, outs = refs[:n], refs[n:2 * n]
        ssem, rsem = refs[2 * n:]
        x, y, c = _place()
        copies = []
        for a in range(n):
            hn = bufs[a].shape[1] // 2
            for k, (dx, dy) in enumerate(offs):
                chip = 2 * ((x + dx) % 2) + (y + dy) % 2
                cp = pltpu.make_async_remote_copy(
                    src_ref=ins[a].at[chip, pl.ds(c * hn, hn)], dst_ref=outs[a].at[chip, pl.ds(c * hn, hn)],
                    send_sem=ssem.at[a, k], recv_sem=rsem.at[a, k],
                    device_id=(x, y, 1 - c), device_id_type=MESH)
                cp.start()
                copies.append(cp)
        for cp in copies:
            cp.wait()

    return _call(body, name=name, in_specs=[ANY] * n, out_specs=[ANY] * n,
                 out_shape=[jax.ShapeDtypeStruct(b.shape, b.dtype) for b in bufs],
                 scratch=[pltpu.SemaphoreType.DMA((n, 3)), pltpu.SemaphoreType.DMA((n, 3))],
                 aliases={a: a for a in range(n)})(*bufs)


def _sibling_pair(arrs, name):
    n = len(arrs)

    def body(*refs):
        ins, outs = refs[:n], refs[n:2 * n]
        ssem, rsem, lsem = refs[2 * n:]
        x, y, c = _place()
        copies = []
        for a in range(n):
            lc = pltpu.make_async_copy(ins[a], outs[a].at[c], lsem.at[a])
            lc.start()
            cp = pltpu.make_async_remote_copy(
                src_ref=ins[a], dst_ref=outs[a].at[c], send_sem=ssem.at[a], recv_sem=rsem.at[a],
                device_id=(x, y, 1 - c), device_id_type=MESH)
            cp.start()
            copies += [lc, cp]
        for cp in copies:
            cp.wait()

    return _call(body, name=name, in_specs=[ANY] * n, out_specs=[ANY] * n,
                 out_shape=[jax.ShapeDtypeStruct((N_CORES,) + a.shape, a.dtype) for a in arrs],
                 scratch=[pltpu.SemaphoreType.DMA((n,)), pltpu.SemaphoreType.DMA((n,)), pltpu.SemaphoreType.DMA((n,))])(*arrs)


def _piece(shape, spec, j, h):
    shard_ax, half_ax = spec
    w = shape[shard_ax] // N_CHIPS
    idx = [slice(None)] * len(shape)
    idx[shard_ax] = pl.ds(j * w, w)
    hn = (w if half_ax == shard_ax else shape[half_ax]) // 2
    assert half_ax != shard_ax
    idx[half_ax] = pl.ds(h * hn, hn)
    return tuple(idx)


def _piece_shape(shape, spec):
    shard_ax, half_ax = spec
    s = list(shape)
    s[shard_ax] //= N_CHIPS
    s[half_ax] //= 2
    return tuple(s)


def _scatter8(arrs, specs, name):
    n = len(arrs)
    targets = [(jx, jy, h) for jx in range(2) for jy in range(2) for h in range(2)]

    def body(*refs):
        ins, outs = refs[:n], refs[n:2 * n]
        ssem, rsem, lsem = refs[2 * n:]
        x, y, c = _place()
        me = 4 * x + 2 * y + c
        for a in range(n):
            for t, (jx, jy, h) in enumerate(targets):
                src = ins[a].at[_piece(arrs[a].shape, specs[a], 2 * jx + jy, h)]
                dst = outs[a].at[me]

                @pl.when(t != me)
                def _():
                    pltpu.make_async_remote_copy(src_ref=src, dst_ref=dst, send_sem=ssem.at[a, t], recv_sem=rsem.at[a, me],
                                                 device_id=(jx, jy, h), device_id_type=MESH).start()

                @pl.when(t == me)
                def _():
                    pltpu.make_async_copy(src, dst, lsem.at[a]).start()
        for a in range(n):
            for t, (jx, jy, h) in enumerate(targets):
                src = ins[a].at[_piece(arrs[a].shape, specs[a], 2 * jx + jy, h)]
                dst = outs[a].at[t]
                cp = pltpu.make_async_remote_copy(src_ref=src, dst_ref=dst, send_sem=ssem.at[a, t], recv_sem=rsem.at[a, t],
                                                  device_id=(jx, jy, h), device_id_type=MESH)

                @pl.when(t != me)
                def _():
                    cp.wait_send()
                    cp.wait_recv()

                @pl.when(t == me)
                def _():
                    pltpu.make_async_copy(src, dst, lsem.at[a]).wait()

    return _call(body, name=name, in_specs=[ANY] * n, out_specs=[ANY] * n,
                 out_shape=[jax.ShapeDtypeStruct((N_DEV,) + _piece_shape(a.shape, s), a.dtype) for a, s in zip(arrs, specs)],
                 scratch=[pltpu.SemaphoreType.DMA((n, 8)), pltpu.SemaphoreType.DMA((n, 8)), pltpu.SemaphoreType.DMA((n,))])(*arrs)


def kernel(x, c, ada_w, ada_b, pre_mix_g, post_mix_g, pre_ffn_g, post_ffn_g, ffn_w_gu, ffn_w_down, a_w_in, a_b_in, a_ln_g, a_ln_b, a_w_s, a_b_s, a_w_out, kv_ada_w, kv_ada_b, kv_norm_g, kv_w, kv_b_f, k_norm_g, b_w_qg, b_q_norm_g, b_w_o, loss_target, m_ada_w, m_ada_b, m_pre_mix_g, m_post_mix_g, m_pre_ffn_g, m_post_ffn_g, m_ffn_w_gu, m_ffn_w_down, m_a_w_in, m_a_b_in, m_a_ln_g, m_a_ln_b, m_a_w_s, m_a_b_s, m_a_w_out, m_kv_ada_w, m_kv_ada_b, m_kv_norm_g, m_kv_w, m_kv_b_f, m_k_norm_g, m_b_w_qg, m_b_q_norm_g, m_b_w_o, v_ada_w, v_ada_b, v_pre_mix_g, v_post_mix_g, v_pre_ffn_g, v_post_ffn_g, v_ffn_w_gu, v_ffn_w_down, v_a_w_in, v_a_b_in, v_a_ln_g, v_a_ln_b, v_a_w_s, v_a_b_s, v_a_w_out, v_kv_ada_w, v_kv_ada_b, v_kv_norm_g, v_kv_w, v_kv_b_f, v_k_norm_g, v_b_w_qg, v_b_q_norm_g, v_b_w_o):
    weights = dict(ada_w=ada_w, ada_b=ada_b, pre_mix_g=pre_mix_g, post_mix_g=post_mix_g, pre_ffn_g=pre_ffn_g,
                   post_ffn_g=post_ffn_g, ffn_w_gu=ffn_w_gu, ffn_w_down=ffn_w_down, a_w_in=a_w_in, a_b_in=a_b_in,
                   a_ln_g=a_ln_g, a_ln_b=a_ln_b, a_w_s=a_w_s, a_b_s=a_b_s, a_w_out=a_w_out, kv_ada_w=kv_ada_w,
                   kv_ada_b=kv_ada_b, kv_norm_g=kv_norm_g, kv_w=kv_w, kv_b_f=kv_b_f, k_norm_g=k_norm_g, b_w_qg=b_w_qg,
                   b_q_norm_g=b_q_norm_g, b_w_o=b_w_o)
    m_in = dict(ada_w=m_ada_w, ada_b=m_ada_b, pre_mix_g=m_pre_mix_g, post_mix_g=m_post_mix_g, pre_ffn_g=m_pre_ffn_g,
                post_ffn_g=m_post_ffn_g, ffn_w_gu=m_ffn_w_gu, ffn_w_down=m_ffn_w_down, a_w_in=m_a_w_in, a_b_in=m_a_b_in,
                a_ln_g=m_a_ln_g, a_ln_b=m_a_ln_b, a_w_s=m_a_w_s, a_b_s=m_a_b_s, a_w_out=m_a_w_out, kv_ada_w=m_kv_ada_w,
                kv_ada_b=m_kv_ada_b, kv_norm_g=m_kv_norm_g, kv_w=m_kv_w, kv_b_f=m_kv_b_f, k_norm_g=m_k_norm_g,
                b_w_qg=m_b_w_qg, b_q_norm_g=m_b_q_norm_g, b_w_o=m_b_w_o)
    v_in = dict(ada_w=v_ada_w, ada_b=v_ada_b, pre_mix_g=v_pre_mix_g, post_mix_g=v_post_mix_g, pre_ffn_g=v_pre_ffn_g,
                post_ffn_g=v_post_ffn_g, ffn_w_gu=v_ffn_w_gu, ffn_w_down=v_ffn_w_down, a_w_in=v_a_w_in, a_b_in=v_a_b_in,
                a_ln_g=v_a_ln_g, a_ln_b=v_a_ln_b, a_w_s=v_a_w_s, a_b_s=v_a_b_s, a_w_out=v_a_w_out, kv_ada_w=v_kv_ada_w,
                kv_ada_b=v_kv_ada_b, kv_norm_g=v_kv_norm_g, kv_w=v_kv_w, kv_b_f=v_kv_b_f, k_norm_g=v_k_norm_g,
                b_w_qg=v_b_w_qg, b_q_norm_g=v_b_q_norm_g, b_w_o=v_b_w_o)
    names = list(weights)

    S, D = x.shape[1], x.shape[2]
    L, NA, NB = ada_w.shape[0], a_w_in.shape[0], b_w_qg.shape[0]
    H = kv_b_f.shape[0]
    hd = D // H
    G, CH = a_w_s.shape[1], a_w_s.shape[2]
    GW = a_w_out.shape[1] * N_CHIPS
    F = ffn_w_down.shape[1] * N_CHIPS
    ada_cols = ada_w.shape[2]
    kvada_cols = kv_ada_w.shape[1]
    kv_cols = kv_w.shape[1]
    kv_pad = -(-(2 * D + H) // LANES) * LANES
    xi, yi, ci = _place()
    chip = 2 * xi + yi
    me = 2 * chip + ci
    x0 = x[0]
    tgt = loss_target[0]
    row = lambda t: t.reshape(1, -1)

    c_all = _gather8([c], "gather_c")[0].reshape(N_DEV, D)
    c_act = _silu_rows(jnp.pad(c_all, ((0, BF16_ROWS - N_DEV), (0, 0))), "silu_c")
    mod_sh = [_mm(c_act, (ada_w, l), "nn", F32, f"mod_proj_{l}") for l in range(L)]
    mod_sh.append(_mm(c_act, kv_ada_w, "nn", F32, "mod_proj_kv"))
    mod_sh = jnp.concatenate(mod_sh, axis=1)
    mod_all, b_in_all, ln_g_all, ln_b_all = _chip_gather([mod_sh, a_b_in, a_ln_g, a_ln_b], False, "gather_mod")
    mine = lax.dynamic_index_in_dim(mod_all, me, axis=1, keepdims=False)
    mod = [jnp.concatenate([mine[j, l * ada_cols:(l + 1) * ada_cols] for j in range(N_CHIPS)]) + ada_b[l] for l in range(L)]
    mod = [[row(t) for t in jnp.split(m_, 6)] for m_ in mod]
    mod_kv = jnp.concatenate([mine[j, L * ada_cols:] for j in range(N_CHIPS)]) + kv_ada_b
    kv_sh, kv_sc = [row(t) for t in jnp.split(mod_kv, 2)]
    cat_chips = lambda t, ax: jnp.concatenate([t[j] for j in range(N_CHIPS)], axis=ax)
    b_in_f = cat_chips(b_in_all, 1)
    ln_g_f, ln_b_f = cat_chips(ln_g_all, 1), cat_chips(ln_b_all, 1)

    big = ["ffn_w_gu", "ffn_w_down", "a_w_in", "a_w_out", "kv_w", "b_w_qg", "b_w_o"]
    gathered = _chip_gather([weights[n].astype(BF16) for n in big], True, "gather_w")
    gathered = dict(zip(big, _sibling_fill(gathered, "fill_w")))
    w_gu = cat_chips(gathered["ffn_w_gu"], 2)
    w_dn = cat_chips(gathered["ffn_w_down"], 1)
    w_in = cat_chips(gathered["a_w_in"], 2)
    w_out = cat_chips(gathered["a_w_out"], 1)
    w_kv = jnp.pad(cat_chips(gathered["kv_w"], 1), ((0, 0), (0, kv_pad - (2 * D + H))))
    w_qg = cat_chips(gathered["b_w_qg"], 2)
    w_o = cat_chips(gathered["b_w_o"], 1)

    causal = jnp.tril(jnp.ones((CH, CH), F32))
    ws_m = [(a_w_s[i] * causal).astype(BF16) for i in range(NA)]
    ws_mt = [jnp.swapaxes(w, 1, 2) for w in ws_m]
    bs_t = [a_b_s[i].T for i in range(NA)]

    heads = lambda t: t.reshape(S, H, hd).transpose(1, 0, 2)
    unheads = lambda t: t.transpose(1, 0, 2).reshape(S, D)

    saved = []
    kv = None
    xc = x0
    for l in range(L):
        sh_m, sc_m, g_m, sh_f, sc_f, g_f = mod[l]
        st = {"x0": xc}
        h1 = _norm_mod_fwd(xc, row(pre_mix_g[l]), sh_m, sc_m, f"pre_mix_{l}")
        st["h1"] = h1
        if l < NA:
            zp = _mm(h1, (w_in, l), "nn", BF16, f"gmlp_in_{l}")
            yg = _gmlp_fwd(zp, row(b_in_f[l]), row(ln_g_f[l]), row(ln_b_f[l]), ws_m[l], bs_t[l], f"gmlp_gate_{l}")
            y = _mm(yg, (w_out, l), "nn", F32, f"gmlp_out_{l}")
            st.update(zp=zp, yg=yg)
        else:
            jb = l - NA
            qg = _mm(h1, (w_qg, jb), "nn", BF16, f"fox_qg_{jb}")
            q_raw = heads(qg[:, :D]).reshape(H * S, hd)
            qn = _head_norm_fwd(q_raw, row(b_q_norm_g[jb]), hd ** -0.5, f"fox_qnorm_{jb}").reshape(H, S, hd)
            qa = _augment(qn, kv["dcum"], True)
            qat = jnp.swapaxes(qa, 1, 2)
            o_tr, lse_r = _flash_fwd(kv["ka"], qat, kv["vat"], hd, f"fox_attn_{jb}")
            o_t = o_tr.transpose(2, 0, 1).reshape(S, D)
            og = _out_gate_fwd(o_t, qg, f"fox_gate_{jb}")
            y = _mm(og, (w_o, jb), "nn", F32, f"fox_out_{jb}")
            st.update(qg=qg, q_raw=q_raw, qa=qa, qat=qat, o_tr=o_tr, lse_r=lse_r, o_t=o_t, og=og)
        st["y"] = y
        x1 = _post_fwd(xc, y, row(post_mix_g[l]), g_m, f"post_mix_{l}")
        st["x1"] = x1
        h2 = _norm_mod_fwd(x1, row(pre_ffn_g[l]), sh_f, sc_f, f"pre_ffn_{l}")
        gu = _mm(h2, (w_gu, l), "nn", BF16, f"ffn_gu_{l}")
        act = _swiglu_fwd(gu, f"ffn_act_{l}")
        y2 = _mm(act, (w_dn, l), "nn", F32, f"ffn_down_{l}")
        xc = _post_fwd(x1, y2, row(post_ffn_g[l]), g_f, f"post_ffn_{l}")
        st.update(h2=h2, gu=gu, act=act, y2=y2)
        saved.append(st)
        if l == NA - 1:
            hk = _norm_mod_fwd(xc, row(kv_norm_g), kv_sh, kv_sc, "kv_pre")
            kvf = _mm(hk, w_kv, "nn", F32, "kv_proj")
            k_raw = heads(kvf[:, :D]).reshape(H * S, hd)
            kn = _head_norm_fwd(k_raw, row(k_norm_g), 1.0, "kv_knorm").reshape(H, S, hd)
            vb = heads(kvf[:, D:2 * D]).astype(BF16)
            f_t = kvf[:, 2 * D:2 * D + H].T
            b_col = kv_b_f.reshape(H, 1)
            dcum = _dcum_fwd(f_t, b_col, "kv_dcum")
            vt = kvf[:, D:2 * D].astype(BF16).reshape(S, H, hd).transpose(1, 2, 0)
            vat = jnp.where(lax.broadcasted_iota(jnp.int32, (1, LANES, 1), 1) == hd, jnp.asarray(1, BF16),
                            jnp.pad(vt, ((0, 0), (0, LANES - hd), (0, 0))))
            ka = _augment(kn, dcum, False)
            kv = dict(x=xc, hk=hk, k_raw=k_raw, ka=ka, kat=jnp.swapaxes(ka, 1, 2), vb=vb, vat=vat,
                      f_t=f_t, b_col=b_col, dcum=dcum)

    dx, loss_part = _loss_bwd(xc, tgt, "loss")
    loss = lax.psum(loss_part[0, 0], ("x", "y", "c"))

    gl = {n: [None] * weights[n].shape[0] for n in
          ["pre_mix_g", "post_mix_g", "pre_ffn_g", "post_ffn_g", "ffn_w_gu", "ffn_w_down", "a_w_in", "a_b_in", "a_ln_g",
           "a_ln_b", "a_w_s", "a_b_s", "a_w_out", "b_w_qg", "b_q_norm_g", "b_w_o"]}
    dmod = [None] * L
    dkn = dvb = ddc = None
    gkv = {}
    for l in reversed(range(L)):
        st = saved[l]
        sh_m, sc_m, g_m, sh_f, sc_f, g_f = mod[l]
        if l == NA - 1:
            dk_raw, gkv["k_norm_g"] = _head_norm_bwd(dkn.reshape(H * S, hd), kv["k_raw"], row(k_norm_g), 1.0, "kv_knorm_bwd")
            df_t, db_f = _dcum_bwd(ddc.reshape(H, S), kv["f_t"], kv["b_col"], "kv_dcum_bwd")
            dkvf = jnp.concatenate([unheads(dk_raw.reshape(H, S, hd)), unheads(dvb), df_t.T,
                                    jnp.zeros((S, kv_pad - (2 * D + H)), F32)], axis=1).astype(BF16)
            gkv["kv_w"] = _mm(kv["hk"], dkvf, "tn", BF16, "kv_proj_dw")[:, :2 * D + H]
            dhk = _mm(dkvf, w_kv, "nt", F32, "kv_proj_dx")
            dx, gkv["kv_norm_g"], dsh, dsc = _norm_mod_bwd(dx, dhk, kv["x"], row(kv_norm_g), kv_sh, kv_sc, "kv_pre_bwd")
            gkv["kv_b_f"] = db_f.reshape(H)
            dmod_kv = jnp.concatenate([dsh, dsc], axis=1)
        dy2, gl["post_ffn_g"][l], dg_f = _post_bwd(dx, st["y2"], row(post_ffn_g[l]), g_f, f"post_ffn_bwd_{l}")
        gl["ffn_w_down"][l] = _mm(st["act"], dy2, "tn", BF16, f"ffn_down_dw_{l}")
        dact = _mm(dy2, (w_dn, l), "nt", BF16, f"ffn_down_dx_{l}")
        dgu = _swiglu_bwd(dact, st["gu"], f"ffn_act_bwd_{l}")
        gl["ffn_w_gu"][l] = _mm(st["h2"], dgu, "tn", BF16, f"ffn_gu_dw_{l}")
        dh2 = _mm(dgu, (w_gu, l), "nt", F32, f"ffn_gu_dx_{l}")
        dx, gl["pre_ffn_g"][l], dsh_f, dsc_f = _norm_mod_bwd(dx, dh2, st["x1"], row(pre_ffn_g[l]), sh_f, sc_f, f"pre_ffn_bwd_{l}")
        dy, gl["post_mix_g"][l], dg_m = _post_bwd(dx, st["y"], row(post_mix_g[l]), g_m, f"post_mix_bwd_{l}")
        if l < NA:
            gl["a_w_out"][l] = _mm(st["yg"], dy, "tn", BF16, f"gmlp_out_dw_{l}")
            dyg = _mm(dy, (w_out, l), "nt", BF16, f"gmlp_out_dx_{l}")
            dzp, db_in, dlg, dlb, dws, dbs_t = _gmlp_bwd(dyg, st["zp"], row(b_in_f[l]), row(ln_g_f[l]), row(ln_b_f[l]),
                                                           ws_m[l], ws_mt[l], bs_t[l], f"gmlp_gate_bwd_{l}")
            gl["a_b_in"][l], gl["a_ln_g"][l], gl["a_ln_b"][l] = db_in[0], dlg[0], dlb[0]
            gl["a_w_s"][l], gl["a_b_s"][l] = dws * causal, dbs_t.T
            gl["a_w_in"][l] = _mm(st["h1"], dzp, "tn", BF16, f"gmlp_in_dw_{l}")
            dh1 = _mm(dzp, (w_in, l), "nt", F32, f"gmlp_in_dx_{l}")
        else:
            jb = l - NA
            gl["b_w_o"][jb] = _mm(st["og"], dy, "tn", BF16, f"fox_out_dw_{jb}")
            dog = _mm(dy, (w_o, jb), "nt", F32, f"fox_out_dx_{jb}")
            do_t, dgl = _out_gate_bwd(dog, st["o_t"], st["qg"], f"fox_gate_bwd_{jb}")
            do = heads(do_t)
            dot = do_t.reshape(S, H, hd).transpose(1, 2, 0)
            dqa_tr, delta = _flash_bwd_q(kv["ka"], kv["kat"], st["qat"], kv["vb"], dot, st["o_tr"], st["lse_r"],
                                         f"fox_attn_dq_{jb}")
            dqn = jnp.swapaxes(dqa_tr[:, :hd, :], 1, 2)
            dk_j, dv_j, dd_k, dd_q = _flash_bwd_kv(kv["ka"], st["qa"], st["qat"], kv["vb"], do, dot, st["lse_r"], delta,
                                                   f"fox_attn_dkv_{jb}")
            dk_j = dk_j[:, :, :hd]
            dd_j = dd_k.reshape(H, S) + dd_q.reshape(H, S)
            dkn = dk_j if dkn is None else dkn + dk_j
            dvb = dv_j if dvb is None else dvb + dv_j
            ddc = dd_j if ddc is None else ddc + dd_j
            dq_raw, dgq = _head_norm_bwd(dqn.reshape(H * S, hd), st["q_raw"], row(b_q_norm_g[jb]), hd ** -0.5, f"fox_qnorm_bwd_{jb}")
            gl["b_q_norm_g"][jb] = dgq[0]
            dqg = jnp.concatenate([unheads(dq_raw.reshape(H, S, hd)).astype(BF16), dgl], axis=1)
            gl["b_w_qg"][jb] = _mm(st["h1"], dqg, "tn", BF16, f"fox_qg_dw_{jb}")
            dh1 = _mm(dqg, (w_qg, jb), "nt", F32, f"fox_qg_dx_{jb}")
        dx, gl["pre_mix_g"][l], dsh_m, dsc_m = _norm_mod_bwd(dx, dh1, st["x0"], row(pre_mix_g[l]), sh_m, sc_m, f"pre_mix_bwd_{l}")
        dmod[l] = jnp.concatenate([dsh_m, dsc_m, dg_m, dsh_f, dsc_f, dg_f], axis=1)
    grad_x = dx[None]

    stack = lambda n: jnp.stack([t.reshape(weights[n].shape[1:]) for t in gl[n]])
    small = {"dmod": jnp.concatenate(dmod, axis=1), "dmod_kv": dmod_kv}
    for n in ["pre_mix_g", "post_mix_g", "pre_ffn_g", "post_ffn_g", "a_w_s", "a_b_s", "b_q_norm_g"]:
        small[n] = stack(n)
    for n in ["a_b_in", "a_ln_g", "a_ln_b"]:
        small[n] = jnp.stack(gl[n])
    for n in ["kv_norm_g", "kv_b_f", "k_norm_g"]:
        small[n] = gkv[n]
    sizes = {n: t.size for n, t in small.items()}
    flat = jnp.concatenate([t.reshape(-1).astype(F32) for t in small.values()])
    rows_small = -(-flat.size // (LANES * BF16_ROWS)) * BF16_ROWS
    flat = jnp.pad(flat, (0, rows_small * LANES - flat.size)).reshape(rows_small, LANES)
    flat_all = _gather8([flat], "gather_small")[0]
    flat_sum = _sum_slots(flat_all, "sum_small").reshape(-1)
    offs, o_ = {}, 0
    for n, sz in sizes.items():
        offs[n] = o_
        o_ += sz
    take = lambda n, shape: flat_sum[offs[n]:offs[n] + sizes[n]].reshape(shape)
    dmod_rows = flat_all.reshape(N_DEV, -1)[:, offs["dmod"]:offs["dmod"] + sizes["dmod"] + sizes["dmod_kv"]]
    dmod_rows = jnp.pad(dmod_rows, ((0, BF16_ROWS - N_DEV), (0, 0)))

    grads = {}
    grads["ada_b"] = take("dmod", (L, 6 * D))
    grads["kv_ada_b"] = take("dmod_kv", (2 * D,))
    for n in ["pre_mix_g", "post_mix_g", "pre_ffn_g", "post_ffn_g", "a_w_s", "a_b_s", "b_q_norm_g", "kv_norm_g", "kv_b_f", "k_norm_g"]:
        grads[n] = take(n, weights[n].shape)
    for n in ["a_b_in", "a_ln_g", "a_ln_b"]:
        full = take(n, small[n].shape)
        w = weights[n].shape[1]
        grads[n] = lax.dynamic_slice_in_dim(full, chip * w, w, axis=1)
    ada_g = []
    for l in range(L):
        cols = lax.dynamic_slice_in_dim(dmod_rows[:, l * 6 * D:(l + 1) * 6 * D], chip * ada_cols, ada_cols, axis=1)
        ada_g.append(_mm(c_act, cols, "tn", F32, f"mod_proj_dw_{l}"))
    grads["ada_w"] = jnp.stack(ada_g)
    cols = lax.dynamic_slice_in_dim(dmod_rows[:, L * 6 * D:], chip * kvada_cols, kvada_cols, axis=1)
    grads["kv_ada_w"] = _mm(c_act, cols, "tn", F32, "mod_proj_kv_dw")

    specs = {"ffn_w_gu": (2, 0), "ffn_w_down": (1, 0), "a_w_in": (2, 0), "a_w_out": (1, 0), "kv_w": (0, 1),
             "b_w_qg": (2, 0), "b_w_o": (1, 0)}
    full_g = {n: jnp.stack(gl[n]) for n in big if n != "kv_w"}
    full_g["kv_w"] = gkv["kv_w"].reshape(D, N_CHIPS, kv_cols).transpose(1, 0, 2)
    recv = _scatter8([full_g[n] for n in big], [specs[n] for n in big], "scatter_g")
    halves = [_sum_slots(r, f"sum_g_{n}") for n, r in zip(big, recv)]
    pairs = _sibling_pair(halves, "pair_g")
    for n, p in zip(big, pairs):
        grads[n] = p.reshape(weights[n].shape)

    outs_d, outs_m, outs_v = {}, {}, {}
    for n in names:
        w2 = weights[n] if weights[n].ndim > 1 else weights[n].reshape(1, -1)
        shp = w2.shape
        d_, m_, v_ = _adamw(w2, grads[n].reshape(shp), m_in[n].reshape(shp), v_in[n].reshape(shp), f"adamw_{n}")
        outs_d[n], outs_m[n], outs_v[n] = (t.reshape(weights[n].shape) for t in (d_, m_, v_))
    return (loss, grad_x, *[grads[n] for n in names], *[outs_d[n] for n in names],
            *[outs_m[n] for n in names], *[outs_v[n] for n in names])
```

```python
import functools

import jax
import jax.numpy as jnp
from jax import lax
from jax.experimental import pallas as pl
from jax.experimental.pallas import tpu as pltpu

F32 = jnp.float32
BF16 = jnp.bfloat16
MESH = pl.DeviceIdType.MESH
NORM_EPS = 1e-6
MASKED = -1e30
LANES = 128
BF16_ROWS = 16
ROW_BLOCK_BYTES = 12 << 20
ADAM_LR, ADAM_B1, ADAM_B2, ADAM_EPS, ADAM_WD, ADAM_STEP = 0.001, 0.9, 0.999, 1e-08, 0.01, 10
N_CHIPS, N_CORES, N_DEV = 4, 2, 8
ATTN_HEADS_PER_STEP = 2
ANY = pl.BlockSpec(memory_space=pl.ANY)


def _tile(n, cap, quantum):
    best = None
    d = quantum
    while d <= min(n, cap):
        if n % d == 0:
            best = d
        d += quantum
    return n if best is None else best


def _call(body, *, name, out_shape, grid=(), in_specs=None, out_specs=None, scratch=(), sem=None, aliases=None):
    params = {} if sem is None else {"dimension_semantics": sem}
    return pl.pallas_call(
        body, name=name, grid=grid, in_specs=in_specs, out_specs=out_specs, out_shape=out_shape,
        scratch_shapes=list(scratch), input_output_aliases=aliases or {},
        compiler_params=pltpu.CompilerParams(**params))


def _call_prefetch(body, *, name, out_shape, grid, n_prefetch, in_specs, out_specs, scratch, sem):
    spec = pltpu.PrefetchScalarGridSpec(num_scalar_prefetch=n_prefetch, grid=grid, in_specs=in_specs,
                                        out_specs=out_specs, scratch_shapes=list(scratch))
    return pl.pallas_call(
        body, name=name, grid_spec=spec, out_shape=out_shape,
        compiler_params=pltpu.CompilerParams(dimension_semantics=sem))


def _place():
    x, y, c = lax.axis_index("x"), lax.axis_index("y"), lax.axis_index("c")
    return x, y, c


def _mm(a, b, mode, out_dtype, name):
    b_arr, b_idx = b if isinstance(b, tuple) else (b, None)
    bs = b_arr.shape[-2:]
    if mode == "nn":
        (M, K), (K2, N) = a.shape, bs
        dims = (((1,), (0,)), ((), ()))
    elif mode == "nt":
        (M, K), (N, K2) = a.shape, bs
        dims = (((1,), (1,)), ((), ()))
    else:
        (K, M), (K2, N) = a.shape, bs
        dims = (((0,), (0,)), ((), ()))
    assert K == K2, (name, a.shape, b_arr.shape)
    if mode == "tn":
        tm = _tile(M, 1408, LANES)
        tk = _tile(K, 2048, BF16_ROWS)
        tn = _tile(N, 512, LANES)
    else:
        tm = _tile(M, 1024, BF16_ROWS)
        tk = K if K <= 2816 else _tile(K, 2816, LANES)
        tn = _tile(N, 1408 if tk <= 1024 else 512, LANES)
    if tn < 256:
        tn = N
        tm = _tile(M, 512, LANES if mode == "tn" else BF16_ROWS)
    nk = K // tk
    grid = (M // tm, N // tn, nk)

    if mode == "tn":
        a_spec = pl.BlockSpec((tk, tm), lambda i, j, k: (k, i))
    else:
        a_spec = pl.BlockSpec((tm, tk), lambda i, j, k: (i, k))
    if mode == "nt":
        b_blk, b_map = (tn, tk), (lambda i, j, k: (j, k))
    else:
        b_blk, b_map = (tk, tn), (lambda i, j, k: (k, j))
    if b_idx is None:
        b_spec = pl.BlockSpec(b_blk, b_map)
    else:
        b_spec = pl.BlockSpec((None,) + b_blk, lambda i, j, k: (b_idx,) + b_map(i, j, k))

    def body(a_ref, b_ref, o_ref, *acc):
        r = lax.dot_general(a_ref[...].astype(BF16), b_ref[...].astype(BF16), dims, preferred_element_type=F32)
        if nk == 1:
            o_ref[...] = r.astype(o_ref.dtype)
        else:
            k = pl.program_id(2)

            @pl.when(k == 0)
            def _():
                acc[0][...] = r

            @pl.when(k > 0)
            def _():
                acc[0][...] += r

            @pl.when(k == nk - 1)
            def _():
                o_ref[...] = acc[0][...].astype(o_ref.dtype)

    return _call(
        body, name=name, grid=grid, in_specs=[a_spec, b_spec],
        out_specs=pl.BlockSpec((tm, tn), lambda i, j, k: (i, j)),
        out_shape=jax.ShapeDtypeStruct((M, N), out_dtype),
        scratch=[pltpu.VMEM((tm, tn), F32)] if nk > 1 else [],
        sem=("parallel", "parallel", "arbitrary"))(a, b_arr)


def _rowwise(fn, rows, pars, outs, pouts, name):
    R = rows[0].shape[0]
    row_bytes = 4 * (sum(max(r.shape[1], LANES) for r in rows) + sum(max(c, LANES) for c, _ in outs))
    tb = _tile(R, max(BF16_ROWS, ROW_BLOCK_BYTES // row_bytes), BF16_ROWS)
    nr, npar, no = len(rows), len(pars), len(outs)

    def body(*refs):
        r_in, p_in = refs[:nr], refs[nr:nr + npar]
        r_out, p_out = refs[nr + npar:nr + npar + no], refs[nr + npar + no:]
        ro, po = fn([r[...] for r in r_in], [p[...] for p in p_in])
        for ref, val in zip(r_out, ro):
            if isinstance(val, (tuple, list)):
                off = 0
                for piece in val:
                    w = piece.shape[1]
                    ref[:, off:off + w] = piece.astype(ref.dtype)
                    off += w
            else:
                ref[...] = val.astype(ref.dtype)
        if p_out:
            first = pl.program_id(0) == 0

            @pl.when(first)
            def _():
                for ref, val in zip(p_out, po):
                    ref[...] = val

            @pl.when(jnp.logical_not(first))
            def _():
                for ref, val in zip(p_out, po):
                    ref[...] += val

    res = _call(
        body, name=name, grid=(R // tb,),
        in_specs=[pl.BlockSpec((tb, r.shape[1]), lambda i: (i, 0)) for r in rows]
        + [pl.BlockSpec(p.shape, lambda i: (0, 0)) for p in pars],
        out_specs=[pl.BlockSpec((tb, c), lambda i: (i, 0)) for c, _ in outs]
        + [pl.BlockSpec(s, lambda i: (0, 0)) for s in pouts],
        out_shape=[jax.ShapeDtypeStruct((R, c), dt) for c, dt in outs]
        + [jax.ShapeDtypeStruct(s, F32) for s in pouts],
        sem=("arbitrary",) if pouts else ("parallel",))(*rows, *pars)
    return list(res)


def _rms(x, g):
    return x * lax.rsqrt(jnp.mean(x * x, axis=-1, keepdims=True) + NORM_EPS) * g


def _norm_mod(x, g, sh, sc):
    return _rms(x, g) * (1.0 + sc) + sh


def _gated_post(y, g, gate):
    return gate * _rms(y, g)


def _norm_mod_fwd(x, g, sh, sc, name):
    return _rowwise(lambda r, p: ([_norm_mod(r[0], *p)], []), [x], [g, sh, sc], [(x.shape[1], BF16)], [], name)[0]


def _norm_mod_bwd(dxo, dh, x, g, sh, sc, name):
    def fn(r, p):
        _, vjp = jax.vjp(_norm_mod, r[2], *p)
        dx, dg, dsh, dsc = vjp(r[1].astype(F32))
        return [r[0] + dx], [dg, dsh, dsc]
    c = x.shape[1]
    return _rowwise(fn, [dxo, dh, x], [g, sh, sc], [(c, F32)], [(1, c)] * 3, name)


def _post_fwd(x, y, g, gate, name):
    return _rowwise(lambda r, p: ([r[0] + _gated_post(r[1].astype(F32), *p)], []), [x, y], [g, gate],
                    [(x.shape[1], F32)], [], name)[0]


def _post_bwd(dxo, y, g, gate, name):
    def fn(r, p):
        _, vjp = jax.vjp(_gated_post, r[1].astype(F32), *p)
        dy, dg, dgate = vjp(r[0])
        return [dy], [dg, dgate]
    c = y.shape[1]
    return _rowwise(fn, [dxo, y], [g, gate], [(c, BF16)], [(1, c)] * 2, name)


def _swiglu(g, u):
    return jax.nn.silu(g) * u


def _swiglu_fwd(gu, name):
    f = gu.shape[1] // 2
    return _rowwise(lambda r, p: ([_swiglu(r[0][:, :f].astype(F32), r[0][:, f:].astype(F32))], []), [gu], [],
                    [(f, BF16)], [], name)[0]


def _swiglu_bwd(da, gu, name):
    f = gu.shape[1] // 2

    def fn(r, p):
        _, vjp = jax.vjp(_swiglu, r[1][:, :f].astype(F32), r[1][:, f:].astype(F32))
        return [vjp(r[0].astype(F32))], []
    return _rowwise(fn, [da, gu], [], [(2 * f, BF16)], [], name)[0]


def _silu_rows(c, name):
    return _rowwise(lambda r, p: ([jax.nn.silu(r[0])], []), [c], [], [(c.shape[1], F32)], [], name)[0]


def _head_norm(x, g, scale):
    return _rms(x, g) * scale


def _head_norm_fwd(x, g, scale, name):
    return _rowwise(lambda r, p: ([_head_norm(r[0].astype(F32), p[0], scale)], []), [x], [g],
                    [(x.shape[1], BF16)], [], name)[0]


def _head_norm_bwd(dy, x, g, scale, name):
    def fn(r, p):
        _, vjp = jax.vjp(lambda t, gg: _head_norm(t, gg, scale), r[1].astype(F32), p[0])
        dx, dg = vjp(r[0])
        return [dx], [dg]
    c = x.shape[1]
    return _rowwise(fn, [dy, x], [g], [(c, F32)], [(1, c)], name)


def _out_gate_fwd(o, qg, name):
    d = o.shape[1]
    return _rowwise(lambda r, p: ([r[0] * jax.nn.sigmoid(r[1][:, d:].astype(F32))], []), [o, qg], [],
                    [(d, BF16)], [], name)[0]


def _out_gate_bwd(dog, o, qg, name):
    d = o.shape[1]

    def fn(r, p):
        _, vjp = jax.vjp(lambda oo, gl: oo * jax.nn.sigmoid(gl), r[1], r[2][:, d:].astype(F32))
        do, dgl = vjp(r[0])
        return [do, dgl], []
    return _rowwise(fn, [dog, o, qg], [], [(d, BF16), (d, BF16)], [], name)


def _loss_bwd(y, tgt, name):
    n = y.shape[1]

    def fn(r, p):
        e = r[0] - r[1]
        part = jnp.sum(jnp.sum(e * e, axis=1, keepdims=True), axis=0, keepdims=True) * (0.5 / n)
        return [e * (1.0 / n)], [part]
    return _rowwise(fn, [y, tgt], [], [(n, F32)], [(1, 1)], name)


def _adamw(w, g, m, v, name):
    shape = w.shape
    c = shape[-1]
    flat = [t.reshape(-1, c) for t in (w, g, m, v)]

    def fn(r, p):
        w_, g_, m_, v_ = r
        m2 = ADAM_B1 * m_ + (1.0 - ADAM_B1) * g_
        v2 = ADAM_B2 * v_ + (1.0 - ADAM_B2) * (g_ * g_)
        m_hat = m2 / (1.0 - ADAM_B1 ** ADAM_STEP)
        v_hat = v2 / (1.0 - ADAM_B2 ** ADAM_STEP)
        delta = -ADAM_LR * (m_hat / (jnp.sqrt(v_hat) + ADAM_EPS) + ADAM_WD * w_)
        return [delta, m2, v2], []
    res = _rowwise(fn, flat, [], [(c, F32)] * 3, [], name)
    return [t.reshape(shape) for t in res]


def _sum_slots(recv, name):
    n = recv.shape[0]
    shape = recv.shape[1:]
    c = shape[-1]
    r3 = recv.reshape(n, -1, c)
    rows = r3.shape[1]
    tb = _tile(rows, max(BF16_ROWS, ROW_BLOCK_BYTES // (4 * c * (n + 1))), BF16_ROWS)

    def body(r_ref, o_ref):
        acc = r_ref[0].astype(F32)
        for s in range(1, n):
            acc = acc + r_ref[s].astype(F32)
        o_ref[...] = acc

    out = _call(body, name=name, grid=(rows // tb,),
                in_specs=[pl.BlockSpec((n, tb, c), lambda i: (0, i, 0))],
                out_specs=pl.BlockSpec((tb, c), lambda i: (i, 0)),
                out_shape=jax.ShapeDtypeStruct((rows, c), F32), sem=("parallel",))(r3)
    return out.reshape(shape)


def _gmlp_pre(zu, zv, b_u, b_v, ln_g, ln_b):
    u = jax.nn.gelu(zu + b_u, approximate=True)
    v = jax.nn.gelu(zv + b_v, approximate=True)
    xc = v - jnp.mean(v, axis=-1, keepdims=True)
    vn = xc * lax.rsqrt(jnp.mean(xc * xc, axis=-1, keepdims=True) + NORM_EPS) * ln_g + ln_b
    return u, vn


def _gmlp_fwd(zp, b_in, ln_g, ln_b, ws, bs_t, name):
    S, gw2 = zp.shape
    gw = gw2 // 2
    G, ch, _ = ws.shape
    gd = gw // G
    tb = 2 * ch

    def body(zp_ref, bin_ref, lg_ref, lb_ref, ws_ref, bs_ref, o_ref):
        u, vn = _gmlp_pre(zp_ref[:, :gw].astype(F32), zp_ref[:, gw:].astype(F32), bin_ref[:, :gw], bin_ref[:, gw:],
                          lg_ref[...], lb_ref[...])
        vnb = vn.astype(BF16)
        for c in range(tb // ch):
            for g in range(G):
                rs, cs = slice(c * ch, (c + 1) * ch), slice(g * gd, (g + 1) * gd)
                vv = jnp.dot(ws_ref[g], vnb[rs, cs], preferred_element_type=F32) + bs_ref[:, g:g + 1]
                o_ref[rs, cs] = (u[rs, cs] * vv).astype(o_ref.dtype)

    full = lambda a: pl.BlockSpec(a.shape, lambda i: (0,) * a.ndim)
    return _call(body, name=name, grid=(S // tb,),
                 in_specs=[pl.BlockSpec((tb, gw2), lambda i: (i, 0)), full(b_in), full(ln_g), full(ln_b), full(ws), full(bs_t)],
                 out_specs=pl.BlockSpec((tb, gw), lambda i: (i, 0)),
                 out_shape=jax.ShapeDtypeStruct((S, gw), BF16), sem=("parallel",))(zp, b_in, ln_g, ln_b, ws, bs_t)


def _gmlp_bwd(dyg, zp, b_in, ln_g, ln_b, ws, ws_t, bs_t, name):
    S, gw2 = zp.shape
    gw = gw2 // 2
    G, ch, _ = ws.shape
    gd = gw // G
    tb = 2 * ch

    def body(dy_ref, zp_ref, bin_ref, lg_ref, lb_ref, ws_ref, wst_ref, bs_ref,
             dzp_ref, dbin_ref, dlg_ref, dlb_ref, dws_ref, dbs_ref, du_sc, dvn_sc):
        (u, vn), vjp = jax.vjp(_gmlp_pre, zp_ref[:, :gw].astype(F32), zp_ref[:, gw:].astype(F32), bin_ref[:, :gw],
                               bin_ref[:, gw:], lg_ref[...], lb_ref[...])
        vnb = vn.astype(BF16)
        first = pl.program_id(0) == 0

        @pl.when(first)
        def _():
            dws_ref[...] = jnp.zeros_like(dws_ref)

        lane = lax.broadcasted_iota(jnp.int32, (ch, G), 1)
        dbs = jnp.zeros((ch, G), F32)
        for g in range(G):
            cs = slice(g * gd, (g + 1) * gd)
            dws_g = jnp.zeros((ch, ch), F32)
            col = jnp.zeros((ch, 1), F32)
            for c in range(tb // ch):
                rs = slice(c * ch, (c + 1) * ch)
                vnp = vnb[rs, cs]
                vv = jnp.dot(ws_ref[g], vnp, preferred_element_type=F32) + bs_ref[:, g:g + 1]
                dy = dy_ref[rs, cs].astype(F32)
                du_sc[rs, cs] = dy * vv
                dvv = dy * u[rs, cs]
                dvvb = dvv.astype(BF16)
                dvn_sc[rs, cs] = jnp.dot(wst_ref[g], dvvb, preferred_element_type=F32)
                dws_g = dws_g + lax.dot_general(dvvb, vnp, (((1,), (1,)), ((), ())), preferred_element_type=F32)
                col = col + jnp.sum(dvv, axis=1, keepdims=True)
            dws_ref[g] += dws_g
            dbs = jnp.where(lane == g, col, dbs)
        dzu, dzv, dbu, dbv, dlg, dlb = vjp((du_sc[...], dvn_sc[...]))
        dzp_ref[:, :gw] = dzu.astype(dzp_ref.dtype)
        dzp_ref[:, gw:] = dzv.astype(dzp_ref.dtype)

        @pl.when(first)
        def _():
            dbin_ref[:, :gw] = dbu
            dbin_ref[:, gw:] = dbv
            dlg_ref[...] = dlg
            dlb_ref[...] = dlb
            dbs_ref[...] = dbs

        @pl.when(jnp.logical_not(first))
        def _():
            dbin_ref[:, :gw] += dbu
            dbin_ref[:, gw:] += dbv
            dlg_ref[...] += dlg
            dlb_ref[...] += dlb
            dbs_ref[...] += dbs

    full = lambda a: pl.BlockSpec(a.shape, lambda i: (0,) * a.ndim)
    fshape = lambda s: pl.BlockSpec(s, lambda i: (0,) * len(s))
    return _call(
        body, name=name, grid=(S // tb,),
        in_specs=[pl.BlockSpec((tb, gw), lambda i: (i, 0)), pl.BlockSpec((tb, gw2), lambda i: (i, 0)),
                  full(b_in), full(ln_g), full(ln_b), full(ws), full(ws_t), full(bs_t)],
        out_specs=[pl.BlockSpec((tb, gw2), lambda i: (i, 0)), fshape((1, gw2)), fshape((1, gw)), fshape((1, gw)),
                   fshape((G, ch, ch)), fshape((ch, G))],
        out_shape=[jax.ShapeDtypeStruct((S, gw2), BF16), jax.ShapeDtypeStruct((1, gw2), F32),
                   jax.ShapeDtypeStruct((1, gw), F32), jax.ShapeDtypeStruct((1, gw), F32),
                   jax.ShapeDtypeStruct((G, ch, ch), F32), jax.ShapeDtypeStruct((ch, G), F32)],
        scratch=[pltpu.VMEM((tb, gw), F32), pltpu.VMEM((tb, gw), F32)],
        sem=("arbitrary",))(dyg, zp, b_in, ln_g, ln_b, ws, ws_t, bs_t)


def _dot_01(x, ones_bf16):
    hi = x.astype(BF16)
    r1 = x - hi.astype(F32)
    mid = r1.astype(BF16)
    lo = (r1 - mid.astype(F32)).astype(BF16)
    dot = lambda t: jnp.dot(t, ones_bf16, preferred_element_type=F32)
    return dot(hi) + dot(mid) + dot(lo)


def _log_sigmoid(x):
    return jnp.minimum(x, 0.0) - jnp.log1p(jnp.exp(-jnp.abs(x)))


def _dcum_fwd(f_t, b_col, name):
    H, S = f_t.shape
    tb = _tile(S, 512, LANES)

    def body(f_ref, b_ref, o_ref, carry):
        @pl.when(pl.program_id(0) == 0)
        def _():
            carry[...] = jnp.zeros_like(carry)

        ls = _log_sigmoid(f_ref[...] + b_ref[...])
        r = lax.broadcasted_iota(jnp.int32, (tb, tb), 0)
        c = lax.broadcasted_iota(jnp.int32, (tb, tb), 1)
        upper = (r <= c).astype(BF16)
        o_ref[...] = _dot_01(ls, upper) + carry[...]
        carry[...] += jnp.sum(ls, axis=1, keepdims=True)

    return _call(body, name=name, grid=(S // tb,),
                 in_specs=[pl.BlockSpec((H, tb), lambda i: (0, i)), pl.BlockSpec((H, 1), lambda i: (0, 0))],
                 out_specs=pl.BlockSpec((H, tb), lambda i: (0, i)),
                 out_shape=jax.ShapeDtypeStruct((H, S), F32),
                 scratch=[pltpu.VMEM((H, 1), F32)], sem=("arbitrary",))(f_t, b_col)


def _dcum_bwd(dd_t, f_t, b_col, name):
    H, S = f_t.shape
    tb = _tile(S, 512, LANES)
    nb = S // tb

    def body(dd_ref, f_ref, b_ref, df_ref, db_ref, carry):
        first = pl.program_id(0) == 0

        @pl.when(first)
        def _():
            carry[...] = jnp.zeros_like(carry)

        dd = dd_ref[...]
        r = lax.broadcasted_iota(jnp.int32, (tb, tb), 0)
        c = lax.broadcasted_iota(jnp.int32, (tb, tb), 1)
        lower = (r >= c).astype(BF16)
        rev = _dot_01(dd, lower) + carry[...]
        carry[...] += jnp.sum(dd, axis=1, keepdims=True)
        df = rev * jax.nn.sigmoid(-(f_ref[...] + b_ref[...]))
        df_ref[...] = df
        part = jnp.sum(df, axis=1, keepdims=True)

        @pl.when(first)
        def _():
            db_ref[...] = part

        @pl.when(jnp.logical_not(first))
        def _():
            db_ref[...] += part

    return _call(body, name=name, grid=(nb,),
                 in_specs=[pl.BlockSpec((H, tb), lambda i: (0, nb - 1 - i)), pl.BlockSpec((H, tb), lambda i: (0, nb - 1 - i)),
                           pl.BlockSpec((H, 1), lambda i: (0, 0))],
                 out_specs=[pl.BlockSpec((H, tb), lambda i: (0, nb - 1 - i)), pl.BlockSpec((H, 1), lambda i: (0, 0))],
                 out_shape=[jax.ShapeDtypeStruct((H, S), F32), jax.ShapeDtypeStruct((H, 1), F32)],
                 scratch=[pltpu.VMEM((H, 1), F32)], sem=("arbitrary",))(dd_t, f_t, b_col)


def _attn_tile(S):
    return _tile(S, 512, LANES)


def _causal(t, transposed):
    r = lax.broadcasted_iota(jnp.int32, (t, t), 0)
    c = lax.broadcasted_iota(jnp.int32, (t, t), 1)
    return (r <= c) if transposed else (c <= r)


def _tri_pairs(n, key_major):
    if key_major:
        pairs = [(i, j) for j in range(n) for i in range(j, n)]
    else:
        pairs = [(i, j) for i in range(n) for j in range(i + 1)]
    return jnp.asarray([p[0] for p in pairs], jnp.int32), jnp.asarray([p[1] for p in pairs], jnp.int32)


def _split3(x):
    hi = lax.reduce_precision(x, 8, 7)
    r = x - hi
    mid = lax.reduce_precision(r, 8, 7)
    lo = lax.reduce_precision(r - mid, 8, 7)
    return hi.astype(BF16), mid.astype(BF16), lo.astype(BF16)


def _augment(xn, dcum, query):
    H, S, hd = xn.shape
    parts = list(_split3(dcum))
    vals = parts + [1.0] * 3 if query else [1.0] * 3 + [-p for p in parts]
    lane = lax.broadcasted_iota(jnp.int32, (1, 1, LANES), 2)
    out = jnp.pad(xn, ((0, 0), (0, 0), (0, LANES - hd)))
    for k, val in enumerate(vals):
        val = jnp.asarray(val, BF16)
        out = jnp.where(lane == hd + k, val[..., None] if val.ndim else val, out)
    return out


def _scores_t(k_ref, qt_ref, h, t, diag):
    st = jnp.dot(k_ref[h], qt_ref[h], preferred_element_type=F32)
    return jnp.where(_causal(t, True), st, MASKED) if diag else st


def _flash_fwd(ka, qat, vat, hd, name):
    H, S, da = ka.shape
    t = _attn_tile(S)
    hb = ATTN_HEADS_PER_STEP
    it, jt = _tri_pairs(S // t, False)

    def body(it_ref, jt_ref, k_ref, qt_ref, vt_ref, o_ref, lse_ref, m_sc, acc_sc):
        i, j = it_ref[pl.program_id(1)], jt_ref[pl.program_id(1)]

        @pl.when(j == 0)
        def _():
            m_sc[...] = jnp.full_like(m_sc, MASKED)
            acc_sc[...] = jnp.zeros_like(acc_sc)

        def step(diag):
            for h in range(hb):
                st = _scores_t(k_ref, qt_ref, h, t, diag)
                m_prev = m_sc[h]
                m_new = jnp.maximum(m_prev, jnp.max(st, axis=0, keepdims=True))
                pt = jnp.exp(st - m_new).astype(BF16)
                acc_sc[h] = jnp.exp(m_prev - m_new) * acc_sc[h] + jnp.dot(vt_ref[h], pt, preferred_element_type=F32)
                m_sc[h] = m_new

        @pl.when(j < i)
        def _():
            step(False)

        @pl.when(j == i)
        def _():
            step(True)
            for h in range(hb):
                l = acc_sc[h, hd:hd + 1, :]
                o_ref[h] = acc_sc[h, :hd, :] / l
                lse_ref[h] = m_sc[h] + jnp.log(l)

    qcol = lambda h, p, it_, jt_: (h, 0, it_[p])
    kcol = lambda h, p, it_, jt_: (h, 0, jt_[p])
    krow = lambda h, p, it_, jt_: (h, jt_[p], 0)
    return _call_prefetch(
        body, name=name, grid=(H // hb, it.shape[0]), n_prefetch=2,
        in_specs=[pl.BlockSpec((hb, t, da), krow), pl.BlockSpec((hb, da, t), qcol), pl.BlockSpec((hb, da, t), kcol)],
        out_specs=[pl.BlockSpec((hb, hd, t), qcol), pl.BlockSpec((hb, 1, t), qcol)],
        out_shape=[jax.ShapeDtypeStruct((H, hd, S), F32), jax.ShapeDtypeStruct((H, 1, S), F32)],
        scratch=[pltpu.VMEM((hb, 1, t), F32), pltpu.VMEM((hb, da, t), F32)],
        sem=("parallel", "arbitrary"))(it, jt, ka, qat, vat)


def _ds_t(k_ref, qt_ref, v_ref, dot_ref, lse_ref, dl, h, t, diag):
    pt = jnp.exp(_scores_t(k_ref, qt_ref, h, t, diag) - lse_ref[h])
    dpt = jnp.dot(v_ref[h], dot_ref[h], preferred_element_type=F32)
    return pt, pt * (dpt - dl)


def _flash_bwd(ka, kat, qa, qat, v, do, dot, o_tr, lse_r, name):
    H, S, hd = v.shape
    da = ka.shape[2]
    t = _attn_tile(S)
    n = S // t
    hb = ATTN_HEADS_PER_STEP
    it, jt = _tri_pairs(n, True)

    def body(it_ref, jt_ref, k_ref, kt_ref, q_ref, qt_ref, v_ref, do_ref, dot_ref, o_ref, lse_ref,
             dq_ref, dk_ref, dv_ref, dd_ref, ddq_ref, dk_sc, dv_sc, dd_sc):
        i, j = it_ref[pl.program_id(1)], jt_ref[pl.program_id(1)]

        @pl.when(pl.program_id(1) == 0)
        def _():
            ddq_ref[...] = jnp.zeros_like(ddq_ref)
            dq_ref[...] = jnp.zeros_like(dq_ref)

        def step(diag):
            for h in range(hb):
                dl = jnp.sum(dot_ref[h].astype(F32) * o_ref[h], axis=0, keepdims=True)
                pt, dst = _ds_t(k_ref, qt_ref, v_ref, dot_ref, lse_ref, dl, h, t, diag)
                dsb = dst.astype(BF16)
                dv_sc[h] += jnp.dot(pt.astype(BF16), do_ref[h], preferred_element_type=F32)
                dk_sc[h] += jnp.dot(dsb, q_ref[h], preferred_element_type=F32)
                dq_ref[h, i] += jnp.dot(kt_ref[h], dsb, preferred_element_type=F32)
                part = dst[:, :LANES]
                for c in range(1, t // LANES):
                    part = part + dst[:, c * LANES:(c + 1) * LANES]
                dd_sc[h] += part
                ddq_ref[h, i] += jnp.sum(dst, axis=0, keepdims=True)

        @pl.when(i == j)
        def _():
            dk_sc[...] = jnp.zeros_like(dk_sc)
            dv_sc[...] = jnp.zeros_like(dv_sc)
            dd_sc[...] = jnp.zeros_like(dd_sc)
            step(True)

        @pl.when(i > j)
        def _():
            step(False)

        @pl.when(i == n - 1)
        def _():
            dk_ref[...] = dk_sc[...]
            dv_ref[...] = dv_sc[...]
            for h in range(hb):
                dd_ref[h] = -jnp.sum(dd_sc[h], axis=1, keepdims=True)

    krow = lambda h, p, it_, jt_: (h, jt_[p], 0)
    kcol = lambda h, p, it_, jt_: (h, 0, jt_[p])
    qrow = lambda h, p, it_, jt_: (h, it_[p], 0)
    qcol = lambda h, p, it_, jt_: (h, 0, it_[p])
    whole = lambda h, p, it_, jt_: (h, 0, 0, 0)
    return _call_prefetch(
        body, name=name, grid=(H // hb, it.shape[0]), n_prefetch=2,
        in_specs=[pl.BlockSpec((hb, t, da), krow), pl.BlockSpec((hb, da, t), kcol), pl.BlockSpec((hb, t, da), qrow),
                  pl.BlockSpec((hb, da, t), qcol), pl.BlockSpec((hb, t, hd), krow), pl.BlockSpec((hb, t, hd), qrow),
                  pl.BlockSpec((hb, hd, t), qcol), pl.BlockSpec((hb, hd, t), qcol), pl.BlockSpec((hb, 1, t), qcol)],
        out_specs=[pl.BlockSpec((hb, n, da, t), whole), pl.BlockSpec((hb, t, da), krow), pl.BlockSpec((hb, t, hd), krow),
                   pl.BlockSpec((hb, t, 1), krow), pl.BlockSpec((hb, n, 1, t), whole)],
        out_shape=[jax.ShapeDtypeStruct((H, n, da, t), F32), jax.ShapeDtypeStruct((H, S, da), F32),
                   jax.ShapeDtypeStruct((H, S, hd), F32), jax.ShapeDtypeStruct((H, S, 1), F32),
                   jax.ShapeDtypeStruct((H, n, 1, t), F32)],
        scratch=[pltpu.VMEM((hb, t, da), F32), pltpu.VMEM((hb, t, hd), F32), pltpu.VMEM((hb, t, LANES), F32)],
        sem=("parallel", "arbitrary"))(it, jt, ka, kat, qa, qat, v, do, dot, o_tr, lse_r)


def _offsets(n_bits):
    return [tuple((k >> b) & 1 for b in reversed(range(n_bits))) for k in range(1, 1 << n_bits)]


def _gather8(arrs, name):
    n = len(arrs)
    offs = _offsets(3)

    def body(*refs):
        ins, outs = refs[:n], refs[n:2 * n]
        ssem, rsem, lsem = refs[2 * n:]
        x, y, c = _place()
        me = 4 * x + 2 * y + c
        copies = []
        for a in range(n):
            lc = pltpu.make_async_copy(ins[a], outs[a].at[me], lsem.at[a])
            lc.start()
            copies.append(lc)
            for k, (dx, dy, dcc) in enumerate(offs):
                cp = pltpu.make_async_remote_copy(
                    src_ref=ins[a], dst_ref=outs[a].at[me], send_sem=ssem.at[a, k], recv_sem=rsem.at[a, k],
                    device_id=((x + dx) % 2, (y + dy) % 2, (c + dcc) % 2), device_id_type=MESH)
                cp.start()
                copies.append(cp)
        for cp in copies:
            cp.wait()

    return _call(body, name=name, in_specs=[ANY] * n, out_specs=[ANY] * n,
                 out_shape=[jax.ShapeDtypeStruct((N_DEV,) + a.shape, a.dtype) for a in arrs],
                 scratch=[pltpu.SemaphoreType.DMA((n, 7)), pltpu.SemaphoreType.DMA((n, 7)), pltpu.SemaphoreType.DMA((n,))])(*arrs)


def _chip_gather(arrs, halved, name):
    n = len(arrs)
    offs = _offsets(2)

    def body(*refs):
        ins, outs = refs[:n], refs[n:2 * n]
        ssem, rsem, lsem = refs[2 * n:]
        x, y, c = _place()
        chip = 2 * x + y
        copies = []
        for a in range(n):
            lc = pltpu.make_async_copy(ins[a], outs[a].at[chip], lsem.at[a])
            lc.start()
            copies.append(lc)
            if halved:
                hn = arrs[a].shape[0] // 2
                src = ins[a].at[pl.ds(c * hn, hn)]
                dst = outs[a].at[chip, pl.ds(c * hn, hn)]
            else:
                src, dst = ins[a], outs[a].at[chip]
            for k, (dx, dy) in enumerate(offs):
                cp = pltpu.make_async_remote_copy(
                    src_ref=src, dst_ref=dst, send_sem=ssem.at[a, k], recv_sem=rsem.at[a, k],
                    device_id=((x + dx) % 2, (y + dy) % 2, c), device_id_type=MESH)
                cp.start()
                copies.append(cp)
        for cp in copies:
            cp.wait()

    return _call(body, name=name, in_specs=[ANY] * n, out_specs=[ANY] * n,
                 out_shape=[jax.ShapeDtypeStruct((N_CHIPS,) + a.shape, a.dtype) for a in arrs],
                 scratch=[pltpu.SemaphoreType.DMA((n, 3)), pltpu.SemaphoreType.DMA((n, 3)), pltpu.SemaphoreType.DMA((n,))])(*arrs)


def _sibling_fill(bufs, name):
    n = len(bufs)
    offs = _offsets(2)

    def body(*refs):
        ins, outs = refs[:n], refs[n:2 * n]
        ssem, rsem = refs[2 * n:]
        x, y, c = _place()
        copies = []
        for a in range(n):
            hn = bufs[a].shape[1] // 2
            for k, (dx, dy) in enumerate(offs):
                chip = 2 * ((x + dx) % 2) + (y + dy) % 2
                cp = pltpu.make_async_remote_copy(
                    src_ref=ins[a].at[chip, pl.ds(c * hn, hn)], dst_ref=outs[a].at[chip, pl.ds(c * hn, hn)],
                    send_sem=ssem.at[a, k], recv_sem=rsem.at[a, k],
                    device_id=(x, y, 1 - c), device_id_type=MESH)
                cp.start()
                copies.append(cp)
        for cp in copies:
            cp.wait()

    return _call(body, name=name, in_specs=[ANY] * n, out_specs=[ANY] * n,
                 out_shape=[jax.ShapeDtypeStruct(b.shape, b.dtype) for b in bufs],
                 scratch=[pltpu.SemaphoreType.DMA((n, 3)), pltpu.SemaphoreType.DMA((n, 3))],
                 aliases={a: a for a in range(n)})(*bufs)


def _sibling_pair(arrs, name):
    n = len(arrs)

    def body(*refs):
        ins, outs = refs[:n], refs[n:2 * n]
        ssem, rsem, lsem = refs[2 * n:]
        x, y, c = _place()
        copies = []
        for a in range(n):
            lc = pltpu.make_async_copy(ins[a], outs[a].at[c], lsem.at[a])
            lc.start()
            cp = pltpu.make_async_remote_copy(
                src_ref=ins[a], dst_ref=outs[a].at[c], send_sem=ssem.at[a], recv_sem=rsem.at[a],
                device_id=(x, y, 1 - c), device_id_type=MESH)
            cp.start()
            copies += [lc, cp]
        for cp in copies:
            cp.wait()

    return _call(body, name=name, in_specs=[ANY] * n, out_specs=[ANY] * n,
                 out_shape=[jax.ShapeDtypeStruct((N_CORES,) + a.shape, a.dtype) for a in arrs],
                 scratch=[pltpu.SemaphoreType.DMA((n,)), pltpu.SemaphoreType.DMA((n,)), pltpu.SemaphoreType.DMA((n,))])(*arrs)


def _piece(shape, spec, j, h):
    shard_ax, half_ax = spec
    w = shape[shard_ax] // N_CHIPS
    idx = [slice(None)] * len(shape)
    idx[shard_ax] = pl.ds(j * w, w)
    hn = (w if half_ax == shard_ax else shape[half_ax]) // 2
    assert half_ax != shard_ax
    idx[half_ax] = pl.ds(h * hn, hn)
    return tuple(idx)


def _piece_shape(shape, spec):
    shard_ax, half_ax = spec
    s = list(shape)
    s[shard_ax] //= N_CHIPS
    s[half_ax] //= 2
    return tuple(s)


def _scatter8(arrs, specs, name):
    n = len(arrs)
    targets = [(jx, jy, h) for jx in range(2) for jy in range(2) for h in range(2)]

    def body(*refs):
        ins, outs = refs[:n], refs[n:2 * n]
        ssem, rsem, lsem = refs[2 * n:]
        x, y, c = _place()
        me = 4 * x + 2 * y + c
        for a in range(n):
            for t, (jx, jy, h) in enumerate(targets):
                src = ins[a].at[_piece(arrs[a].shape, specs[a], 2 * jx + jy, h)]
                dst = outs[a].at[me]

                @pl.when(t != me)
                def _():
                    pltpu.make_async_remote_copy(src_ref=src, dst_ref=dst, send_sem=ssem.at[a, t], recv_sem=rsem.at[a, me],
                                                 device_id=(jx, jy, h), device_id_type=MESH).start()

                @pl.when(t == me)
                def _():
                    pltpu.make_async_copy(src, dst, lsem.at[a]).start()
        for a in range(n):
            for t, (jx, jy, h) in enumerate(targets):
                src = ins[a].at[_piece(arrs[a].shape, specs[a], 2 * jx + jy, h)]
                dst = outs[a].at[t]
                cp = pltpu.make_async_remote_copy(src_ref=src, dst_ref=dst, send_sem=ssem.at[a, t], recv_sem=rsem.at[a, t],
                                                  device_id=(jx, jy, h), device_id_type=MESH)

                @pl.when(t != me)
                def _():
                    cp.wait_send()
                    cp.wait_recv()

                @pl.when(t == me)
                def _():
                    pltpu.make_async_copy(src, dst, lsem.at[a]).wait()

    return _call(body, name=name, in_specs=[ANY] * n, out_specs=[ANY] * n,
                 out_shape=[jax.ShapeDtypeStruct((N_DEV,) + _piece_shape(a.shape, s), a.dtype) for a, s in zip(arrs, specs)],
                 scratch=[pltpu.SemaphoreType.DMA((n, 8)), pltpu.SemaphoreType.DMA((n, 8)), pltpu.SemaphoreType.DMA((n,))])(*arrs)


def kernel(x, c, ada_w, ada_b, pre_mix_g, post_mix_g, pre_ffn_g, post_ffn_g, ffn_w_gu, ffn_w_down, a_w_in, a_b_in, a_ln_g, a_ln_b, a_w_s, a_b_s, a_w_out, kv_ada_w, kv_ada_b, kv_norm_g, kv_w, kv_b_f, k_norm_g, b_w_qg, b_q_norm_g, b_w_o, loss_target, m_ada_w, m_ada_b, m_pre_mix_g, m_post_mix_g, m_pre_ffn_g, m_post_ffn_g, m_ffn_w_gu, m_ffn_w_down, m_a_w_in, m_a_b_in, m_a_ln_g, m_a_ln_b, m_a_w_s, m_a_b_s, m_a_w_out, m_kv_ada_w, m_kv_ada_b, m_kv_norm_g, m_kv_w, m_kv_b_f, m_k_norm_g, m_b_w_qg, m_b_q_norm_g, m_b_w_o, v_ada_w, v_ada_b, v_pre_mix_g, v_post_mix_g, v_pre_ffn_g, v_post_ffn_g, v_ffn_w_gu, v_ffn_w_down, v_a_w_in, v_a_b_in, v_a_ln_g, v_a_ln_b, v_a_w_s, v_a_b_s, v_a_w_out, v_kv_ada_w, v_kv_ada_b, v_kv_norm_g, v_kv_w, v_kv_b_f, v_k_norm_g, v_b_w_qg, v_b_q_norm_g, v_b_w_o):
    weights = dict(ada_w=ada_w, ada_b=ada_b, pre_mix_g=pre_mix_g, post_mix_g=post_mix_g, pre_ffn_g=pre_ffn_g,
                   post_ffn_g=post_ffn_g, ffn_w_gu=ffn_w_gu, ffn_w_down=ffn_w_down, a_w_in=a_w_in, a_b_in=a_b_in,
                   a_ln_g=a_ln_g, a_ln_b=a_ln_b, a_w_s=a_w_s, a_b_s=a_b_s, a_w_out=a_w_out, kv_ada_w=kv_ada_w,
                   kv_ada_b=kv_ada_b, kv_norm_g=kv_norm_g, kv_w=kv_w, kv_b_f=kv_b_f, k_norm_g=k_norm_g, b_w_qg=b_w_qg,
                   b_q_norm_g=b_q_norm_g, b_w_o=b_w_o)
    m_in = dict(ada_w=m_ada_w, ada_b=m_ada_b, pre_mix_g=m_pre_mix_g, post_mix_g=m_post_mix_g, pre_ffn_g=m_pre_ffn_g,
                post_ffn_g=m_post_ffn_g, ffn_w_gu=m_ffn_w_gu, ffn_w_down=m_ffn_w_down, a_w_in=m_a_w_in, a_b_in=m_a_b_in,
                a_ln_g=m_a_ln_g, a_ln_b=m_a_ln_b, a_w_s=m_a_w_s, a_b_s=m_a_b_s, a_w_out=m_a_w_out, kv_ada_w=m_kv_ada_w,
                kv_ada_b=m_kv_ada_b, kv_norm_g=m_kv_norm_g, kv_w=m_kv_w, kv_b_f=m_kv_b_f, k_norm_g=m_k_norm_g,
                b_w_qg=m_b_w_qg, b_q_norm_g=m_b_q_norm_g, b_w_o=m_b_w_o)
    v_in = dict(ada_w=v_ada_w, ada_b=v_ada_b, pre_mix_g=v_pre_mix_g, post_mix_g=v_post_mix_g, pre_ffn_g=v_pre_ffn_g,
                post_ffn_g=v_post_ffn_g, ffn_w_gu=v_ffn_w_gu, ffn_w_down=v_ffn_w_down, a_w_in=v_a_w_in, a_b_in=v_a_b_in,
                a_ln_g=v_a_ln_g, a_ln_b=v_a_ln_b, a_w_s=v_a_w_s, a_b_s=v_a_b_s, a_w_out=v_a_w_out, kv_ada_w=v_kv_ada_w,
                kv_ada_b=v_kv_ada_b, kv_norm_g=v_kv_norm_g, kv_w=v_kv_w, kv_b_f=v_kv_b_f, k_norm_g=v_k_norm_g,
                b_w_qg=v_b_w_qg, b_q_norm_g=v_b_q_norm_g, b_w_o=v_b_w_o)
    names = list(weights)

    S, D = x.shape[1], x.shape[2]
    L, NA, NB = ada_w.shape[0], a_w_in.shape[0], b_w_qg.shape[0]
    H = kv_b_f.shape[0]
    hd = D // H
    G, CH = a_w_s.shape[1], a_w_s.shape[2]
    GW = a_w_out.shape[1] * N_CHIPS
    F = ffn_w_down.shape[1] * N_CHIPS
    ada_cols = ada_w.shape[2]
    kvada_cols = kv_ada_w.shape[1]
    kv_cols = kv_w.shape[1]
    kv_pad = -(-(2 * D + H) // LANES) * LANES
    xi, yi, ci = _place()
    chip = 2 * xi + yi
    me = 2 * chip + ci
    x0 = x[0]
    tgt = loss_target[0]
    row = lambda t: t.reshape(1, -1)

    c_all = _gather8([c], "gather_c")[0].reshape(N_DEV, D)
    c_act = _silu_rows(jnp.pad(c_all, ((0, BF16_ROWS - N_DEV), (0, 0))), "silu_c")
    mod_sh = [_mm(c_act, (ada_w, l), "nn", F32, f"mod_proj_{l}") for l in range(L)]
    mod_sh.append(_mm(c_act, kv_ada_w, "nn", F32, "mod_proj_kv"))
    mod_sh = jnp.concatenate(mod_sh, axis=1)
    mod_all, b_in_all, ln_g_all, ln_b_all = _chip_gather([mod_sh, a_b_in, a_ln_g, a_ln_b], False, "gather_mod")
    mine = lax.dynamic_index_in_dim(mod_all, me, axis=1, keepdims=False)
    mod = [jnp.concatenate([mine[j, l * ada_cols:(l + 1) * ada_cols] for j in range(N_CHIPS)]) + ada_b[l] for l in range(L)]
    mod = [[row(t) for t in jnp.split(m_, 6)] for m_ in mod]
    mod_kv = jnp.concatenate([mine[j, L * ada_cols:] for j in range(N_CHIPS)]) + kv_ada_b
    kv_sh, kv_sc = [row(t) for t in jnp.split(mod_kv, 2)]
    cat_chips = lambda t, ax: jnp.concatenate([t[j] for j in range(N_CHIPS)], axis=ax)
    b_in_f = cat_chips(b_in_all, 1)
    ln_g_f, ln_b_f = cat_chips(ln_g_all, 1), cat_chips(ln_b_all, 1)

    big = ["ffn_w_gu", "ffn_w_down", "a_w_in", "a_w_out", "kv_w", "b_w_qg", "b_w_o"]
    gathered = _chip_gather([weights[n].astype(BF16) for n in big], True, "gather_w")
    gathered = dict(zip(big, _sibling_fill(gathered, "fill_w")))
    w_gu = cat_chips(gathered["ffn_w_gu"], 2)
    w_dn = cat_chips(gathered["ffn_w_down"], 1)
    w_in = cat_chips(gathered["a_w_in"], 2)
    w_out = cat_chips(gathered["a_w_out"], 1)
    w_kv = jnp.pad(cat_chips(gathered["kv_w"], 1), ((0, 0), (0, kv_pad - (2 * D + H))))
    w_qg = cat_chips(gathered["b_w_qg"], 2)
    w_o = cat_chips(gathered["b_w_o"], 1)

    causal = jnp.tril(jnp.ones((CH, CH), F32))
    ws_m = [(a_w_s[i] * causal).astype(BF16) for i in range(NA)]
    ws_mt = [jnp.swapaxes(w, 1, 2) for w in ws_m]
    bs_t = [a_b_s[i].T for i in range(NA)]

    heads = lambda t: t.reshape(S, H, hd).transpose(1, 0, 2)
    unheads = lambda t: t.transpose(1, 0, 2).reshape(S, D)

    saved = []
    kv = None
    xc = x0
    for l in range(L):
        sh_m, sc_m, g_m, sh_f, sc_f, g_f = mod[l]
        st = {"x0": xc}
        h1 = _norm_mod_fwd(xc, row(pre_mix_g[l]), sh_m, sc_m, f"pre_mix_{l}")
        st["h1"] = h1
        if l < NA:
            zp = _mm(h1, (w_in, l), "nn", BF16, f"gmlp_in_{l}")
            yg = _gmlp_fwd(zp, row(b_in_f[l]), row(ln_g_f[l]), row(ln_b_f[l]), ws_m[l], bs_t[l], f"gmlp_gate_{l}")
            y = _mm(yg, (w_out, l), "nn", F32, f"gmlp_out_{l}")
            st.update(zp=zp, yg=yg)
        else:
            jb = l - NA
            qg = _mm(h1, (w_qg, jb), "nn", BF16, f"fox_qg_{jb}")
            q_raw = heads(qg[:, :D]).reshape(H * S, hd)
            qn = _head_norm_fwd(q_raw, row(b_q_norm_g[jb]), hd ** -0.5, f"fox_qnorm_{jb}").reshape(H, S, hd)
            qa = _augment(qn, kv["dcum"], True)
            qat = jnp.swapaxes(qa, 1, 2)
            o_tr, lse_r = _flash_fwd(kv["ka"], qat, kv["vat"], hd, f"fox_attn_{jb}")
            o_t = o_tr.transpose(2, 0, 1).reshape(S, D)
            og = _out_gate_fwd(o_t, qg, f"fox_gate_{jb}")
            y = _mm(og, (w_o, jb), "nn", F32, f"fox_out_{jb}")
            st.update(qg=qg, q_raw=q_raw, qa=qa, qat=qat, o_tr=o_tr, lse_r=lse_r, o_t=o_t, og=og)
        st["y"] = y
        x1 = _post_fwd(xc, y, row(post_mix_g[l]), g_m, f"post_mix_{l}")
        st["x1"] = x1
        h2 = _norm_mod_fwd(x1, row(pre_ffn_g[l]), sh_f, sc_f, f"pre_ffn_{l}")
        gu = _mm(h2, (w_gu, l), "nn", BF16, f"ffn_gu_{l}")
        act = _swiglu_fwd(gu, f"ffn_act_{l}")
        y2 = _mm(act, (w_dn, l), "nn", F32, f"ffn_down_{l}")
        xc = _post_fwd(x1, y2, row(post_ffn_g[l]), g_f, f"post_ffn_{l}")
        st.update(h2=h2, gu=gu, act=act, y2=y2)
        saved.append(st)
        if l == NA - 1:
            hk = _norm_mod_fwd(xc, row(kv_norm_g), kv_sh, kv_sc, "kv_pre")
            kvf = _mm(hk, w_kv, "nn", F32, "kv_proj")
            k_raw = heads(kvf[:, :D]).reshape(H * S, hd)
            kn = _head_norm_fwd(k_raw, row(k_norm_g), 1.0, "kv_knorm").reshape(H, S, hd)
            vb = heads(kvf[:, D:2 * D]).astype(BF16)
            f_t = kvf[:, 2 * D:2 * D + H].T
            b_col = kv_b_f.reshape(H, 1)
            dcum = _dcum_fwd(f_t, b_col, "kv_dcum")
            vt = kvf[:, D:2 * D].astype(BF16).reshape(S, H, hd).transpose(1, 2, 0)
            vat = jnp.where(lax.broadcasted_iota(jnp.int32, (1, LANES, 1), 1) == hd, jnp.asarray(1, BF16),
                            jnp.pad(vt, ((0, 0), (0, LANES - hd), (0, 0))))
            ka = _augment(kn, dcum, False)
            kv = dict(x=xc, hk=hk, k_raw=k_raw, ka=ka, kat=jnp.swapaxes(ka, 1, 2), vb=vb, vat=vat,
                      f_t=f_t, b_col=b_col, dcum=dcum)

    dx, loss_part = _loss_bwd(xc, tgt, "loss")
    loss = lax.psum(loss_part[0, 0], ("x", "y", "c"))

    gl = {n: [None] * weights[n].shape[0] for n in
          ["pre_mix_g", "post_mix_g", "pre_ffn_g", "post_ffn_g", "ffn_w_gu", "ffn_w_down", "a_w_in", "a_b_in", "a_ln_g",
           "a_ln_b", "a_w_s", "a_b_s", "a_w_out", "b_w_qg", "b_q_norm_g", "b_w_o"]}
    dmod = [None] * L
    dkn = dvb = ddc = None
    gkv = {}
    for l in reversed(range(L)):
        st = saved[l]
        sh_m, sc_m, g_m, sh_f, sc_f, g_f = mod[l]
        if l == NA - 1:
            dk_raw, gkv["k_norm_g"] = _head_norm_bwd(dkn.reshape(H * S, hd), kv["k_raw"], row(k_norm_g), 1.0, "kv_knorm_bwd")
            df_t, db_f = _dcum_bwd(ddc.reshape(H, S), kv["f_t"], kv["b_col"], "kv_dcum_bwd")
            dkvf = jnp.concatenate([unheads(dk_raw.reshape(H, S, hd)), unheads(dvb), df_t.T,
                                    jnp.zeros((S, kv_pad - (2 * D + H)), F32)], axis=1).astype(BF16)
            gkv["kv_w"] = _mm(kv["hk"], dkvf, "tn", BF16, "kv_proj_dw")[:, :2 * D + H]
            dhk = _mm(dkvf, w_kv, "nt", F32, "kv_proj_dx")
            dx, gkv["kv_norm_g"], dsh, dsc = _norm_mod_bwd(dx, dhk, kv["x"], row(kv_norm_g), kv_sh, kv_sc, "kv_pre_bwd")
            gkv["kv_b_f"] = db_f.reshape(H)
            dmod_kv = jnp.concatenate([dsh, dsc], axis=1)
        dy2, gl["post_ffn_g"][l], dg_f = _post_bwd(dx, st["y2"], row(post_ffn_g[l]), g_f, f"post_ffn_bwd_{l}")
        gl["ffn_w_down"][l] = _mm(st["act"], dy2, "tn", BF16, f"ffn_down_dw_{l}")
        dact = _mm(dy2, (w_dn, l), "nt", BF16, f"ffn_down_dx_{l}")
        dgu = _swiglu_bwd(dact, st["gu"], f"ffn_act_bwd_{l}")
        gl["ffn_w_gu"][l] = _mm(st["h2"], dgu, "tn", BF16, f"ffn_gu_dw_{l}")
        dh2 = _mm(dgu, (w_gu, l), "nt", F32, f"ffn_gu_dx_{l}")
        dx, gl["pre_ffn_g"][l], dsh_f, dsc_f = _norm_mod_bwd(dx, dh2, st["x1"], row(pre_ffn_g[l]), sh_f, sc_f, f"pre_ffn_bwd_{l}")
        dy, gl["post_mix_g"][l], dg_m = _post_bwd(dx, st["y"], row(post_mix_g[l]), g_m, f"post_mix_bwd_{l}")
        if l < NA:
            gl["a_w_out"][l] = _mm(st["yg"], dy, "tn", BF16, f"gmlp_out_dw_{l}")
            dyg = _mm(dy, (w_out, l), "nt", BF16, f"gmlp_out_dx_{l}")
            dzp, db_in, dlg, dlb, dws, dbs_t = _gmlp_bwd(dyg, st["zp"], row(b_in_f[l]), row(ln_g_f[l]), row(ln_b_f[l]),
                                                           ws_m[l], ws_mt[l], bs_t[l], f"gmlp_gate_bwd_{l}")
            gl["a_b_in"][l], gl["a_ln_g"][l], gl["a_ln_b"][l] = db_in[0], dlg[0], dlb[0]
            gl["a_w_s"][l], gl["a_b_s"][l] = dws * causal, dbs_t.T
            gl["a_w_in"][l] = _mm(st["h1"], dzp, "tn", BF16, f"gmlp_in_dw_{l}")
            dh1 = _mm(dzp, (w_in, l), "nt", F32, f"gmlp_in_dx_{l}")
        else:
            jb = l - NA
            gl["b_w_o"][jb] = _mm(st["og"], dy, "tn", BF16, f"fox_out_dw_{jb}")
            dog = _mm(dy, (w_o, jb), "nt", F32, f"fox_out_dx_{jb}")
            do_t, dgl = _out_gate_bwd(dog, st["o_t"], st["qg"], f"fox_gate_bwd_{jb}")
            do = heads(do_t)
            dot = do_t.reshape(S, H, hd).transpose(1, 2, 0)
            dqa_tr, dk_j, dv_j, dd_k, dd_q = _flash_bwd(kv["ka"], kv["kat"], st["qa"], st["qat"], kv["vb"], do, dot,
                                                        st["o_tr"], st["lse_r"], f"fox_attn_bwd_{jb}")
            dqn = dqa_tr[:, :, :hd, :].transpose(0, 1, 3, 2).reshape(H, S, hd)
            dk_j = dk_j[:, :, :hd]
            dd_j = dd_k.reshape(H, S) + dd_q.reshape(H, S)
            dkn = dk_j if dkn is None else dkn + dk_j
            dvb = dv_j if dvb is None else dvb + dv_j
            ddc = dd_j if ddc is None else ddc + dd_j
            dq_raw, dgq = _head_norm_bwd(dqn.reshape(H * S, hd), st["q_raw"], row(b_q_norm_g[jb]), hd ** -0.5, f"fox_qnorm_bwd_{jb}")
            gl["b_q_norm_g"][jb] = dgq[0]
            dqg = jnp.concatenate([unheads(dq_raw.reshape(H, S, hd)).astype(BF16), dgl], axis=1)
            gl["b_w_qg"][jb] = _mm(st["h1"], dqg, "tn", BF16, f"fox_qg_dw_{jb}")
            dh1 = _mm(dqg, (w_qg, jb), "nt", F32, f"fox_qg_dx_{jb}")
        dx, gl["pre_mix_g"][l], dsh_m, dsc_m = _norm_mod_bwd(dx, dh1, st["x0"], row(pre_mix_g[l]), sh_m, sc_m, f"pre_mix_bwd_{l}")
        dmod[l] = jnp.concatenate([dsh_m, dsc_m, dg_m, dsh_f, dsc_f, dg_f], axis=1)
    grad_x = dx[None]

    stack = lambda n: jnp.stack([t.reshape(weights[n].shape[1:]) for t in gl[n]])
    small = {"dmod": jnp.concatenate(dmod, axis=1), "dmod_kv": dmod_kv}
    for n in ["pre_mix_g", "post_mix_g", "pre_ffn_g", "post_ffn_g", "a_w_s", "a_b_s", "b_q_norm_g"]:
        small[n] = stack(n)
    for n in ["a_b_in", "a_ln_g", "a_ln_b"]:
        small[n] = jnp.stack(gl[n])
    for n in ["kv_norm_g", "kv_b_f", "k_norm_g"]:
        small[n] = gkv[n]
    sizes = {n: t.size for n, t in small.items()}
    flat = jnp.concatenate([t.reshape(-1).astype(F32) for t in small.values()])
    rows_small = -(-flat.size // (LANES * BF16_ROWS)) * BF16_ROWS
    flat = jnp.pad(flat, (0, rows_small * LANES - flat.size)).reshape(rows_small, LANES)
    flat_all = _gather8([flat], "gather_small")[0]
    flat_sum = _sum_slots(flat_all, "sum_small").reshape(-1)
    offs, o_ = {}, 0
    for n, sz in sizes.items():
        offs[n] = o_
        o_ += sz
    take = lambda n, shape: flat_sum[offs[n]:offs[n] + sizes[n]].reshape(shape)
    dmod_rows = flat_all.reshape(N_DEV, -1)[:, offs["dmod"]:offs["dmod"] + sizes["dmod"] + sizes["dmod_kv"]]
    dmod_rows = jnp.pad(dmod_rows, ((0, BF16_ROWS - N_DEV), (0, 0)))

    grads = {}
    grads["ada_b"] = take("dmod", (L, 6 * D))
    grads["kv_ada_b"] = take("dmod_kv", (2 * D,))
    for n in ["pre_mix_g", "post_mix_g", "pre_ffn_g", "post_ffn_g", "a_w_s", "a_b_s", "b_q_norm_g", "kv_norm_g", "kv_b_f", "k_norm_g"]:
        grads[n] = take(n, weights[n].shape)
    for n in ["a_b_in", "a_ln_g", "a_ln_b"]:
        full = take(n, small[n].shape)
        w = weights[n].shape[1]
        grads[n] = lax.dynamic_slice_in_dim(full, chip * w, w, axis=1)
    ada_g = []
    for l in range(L):
        cols = lax.dynamic_slice_in_dim(dmod_rows[:, l * 6 * D:(l + 1) * 6 * D], chip * ada_cols, ada_cols, axis=1)
        ada_g.append(_mm(c_act, cols, "tn", F32, f"mod_proj_dw_{l}"))
    grads["ada_w"] = jnp.stack(ada_g)
    cols = lax.dynamic_slice_in_dim(dmod_rows[:, L * 6 * D:], chip * kvada_cols, kvada_cols, axis=1)
    grads["kv_ada_w"] = _mm(c_act, cols, "tn", F32, "mod_proj_kv_dw")

    specs = {"ffn_w_gu": (2, 0), "ffn_w_down": (1, 0), "a_w_in": (2, 0), "a_w_out": (1, 0), "kv_w": (0, 1),
             "b_w_qg": (2, 0), "b_w_o": (1, 0)}
    full_g = {n: jnp.stack(gl[n]) for n in big if n != "kv_w"}
    full_g["kv_w"] = gkv["kv_w"].reshape(D, N_CHIPS, kv_cols).transpose(1, 0, 2)
    recv = _scatter8([full_g[n] for n in big], [specs[n] for n in big], "scatter_g")
    halves = [_sum_slots(r, f"sum_g_{n}") for n, r in zip(big, recv)]
    pairs = _sibling_pair(halves, "pair_g")
    for n, p in zip(big, pairs):
        grads[n] = p.reshape(weights[n].shape)

    outs_d, outs_m, outs_v = {}, {}, {}
    for n in names:
        w2 = weights[n] if weights[n].ndim > 1 else weights[n].reshape(1, -1)
        shp = w2.shape
        d_, m_, v_ = _adamw(w2, grads[n].reshape(shp), m_in[n].reshape(shp), v_in[n].reshape(shp), f"adamw_{n}")
        outs_d[n], outs_m[n], outs_v[n] = (t.reshape(weights[n].shape) for t in (d_, m_, v_))
    return (loss, grad_x, *[grads[n] for n in names], *[outs_d[n] for n in names],
            *[outs_m[n] for n in names], *[outs_v[n] for n in names])
```

```python
import functools

import jax
import jax.numpy as jnp
from jax import lax
from jax.experimental import pallas as pl
from jax.experimental.pallas import tpu as pltpu

F32 = jnp.float32
BF16 = jnp.bfloat16
MESH = pl.DeviceIdType.MESH
NORM_EPS = 1e-6
MASKED = -1e30
LANES = 128
BF16_ROWS = 16
ROW_BLOCK_BYTES = 12 << 20
ADAM_LR, ADAM_B1, ADAM_B2, ADAM_EPS, ADAM_WD, ADAM_STEP = 0.001, 0.9, 0.999, 1e-08, 0.01, 10
N_CHIPS, N_CORES, N_DEV = 4, 2, 8
ATTN_HEADS_PER_STEP = 2
ATTN_FWD_HEADS_PER_STEP = 4
ANY = pl.BlockSpec(memory_space=pl.ANY)


def _tile(n, cap, quantum):
    best = None
    d = quantum
    while d <= min(n, cap):
        if n % d == 0:
            best = d
        d += quantum
    return n if best is None else best


def _call(body, *, name, out_shape, grid=(), in_specs=None, out_specs=None, scratch=(), sem=None, aliases=None):
    params = {} if sem is None else {"dimension_semantics": sem}
    return pl.pallas_call(
        body, name=name, grid=grid, in_specs=in_specs, out_specs=out_specs, out_shape=out_shape,
        scratch_shapes=list(scratch), input_output_aliases=aliases or {},
        compiler_params=pltpu.CompilerParams(**params))


def _call_prefetch(body, *, name, out_shape, grid, n_prefetch, in_specs, out_specs, scratch, sem):
    spec = pltpu.PrefetchScalarGridSpec(num_scalar_prefetch=n_prefetch, grid=grid, in_specs=in_specs,
                                        out_specs=out_specs, scratch_shapes=list(scratch))
    return pl.pallas_call(
        body, name=name, grid_spec=spec, out_shape=out_shape,
        compiler_params=pltpu.CompilerParams(dimension_semantics=sem))


def _place():
    x, y, c = lax.axis_index("x"), lax.axis_index("y"), lax.axis_index("c")
    return x, y, c


def _mm(a, b, mode, out_dtype, name):
    b_arr, b_idx = b if isinstance(b, tuple) else (b, None)
    bs = b_arr.shape[-2:]
    if mode == "nn":
        (M, K), (K2, N) = a.shape, bs
        dims = (((1,), (0,)), ((), ()))
    elif mode == "nt":
        (M, K), (N, K2) = a.shape, bs
        dims = (((1,), (1,)), ((), ()))
    else:
        (K, M), (K2, N) = a.shape, bs
        dims = (((0,), (0,)), ((), ()))
    assert K == K2, (name, a.shape, b_arr.shape)
    if mode == "tn":
        tm = _tile(M, 1408, LANES)
        tk = _tile(K, 2048, BF16_ROWS)
        tn = _tile(N, 512, LANES)
    else:
        tm = _tile(M, 1024, BF16_ROWS)
        tk = K if K <= 2816 else _tile(K, 2816, LANES)
        tn = _tile(N, 1408 if tk <= 1024 else 512, LANES)
    if tn < 256:
        tn = N
        tm = _tile(M, 512, LANES if mode == "tn" else BF16_ROWS)
    nk = K // tk
    grid = (M // tm, N // tn, nk)

    if mode == "tn":
        a_spec = pl.BlockSpec((tk, tm), lambda i, j, k: (k, i))
    else:
        a_spec = pl.BlockSpec((tm, tk), lambda i, j, k: (i, k))
    if mode == "nt":
        b_blk, b_map = (tn, tk), (lambda i, j, k: (j, k))
    else:
        b_blk, b_map = (tk, tn), (lambda i, j, k: (k, j))
    if b_idx is None:
        b_spec = pl.BlockSpec(b_blk, b_map)
    else:
        b_spec = pl.BlockSpec((None,) + b_blk, lambda i, j, k: (b_idx,) + b_map(i, j, k))

    def body(a_ref, b_ref, o_ref, *acc):
        r = lax.dot_general(a_ref[...].astype(BF16), b_ref[...].astype(BF16), dims, preferred_element_type=F32)
        if nk == 1:
            o_ref[...] = r.astype(o_ref.dtype)
        else:
            k = pl.program_id(2)

            @pl.when(k == 0)
            def _():
                acc[0][...] = r

            @pl.when(k > 0)
            def _():
                acc[0][...] += r

            @pl.when(k == nk - 1)
            def _():
                o_ref[...] = acc[0][...].astype(o_ref.dtype)

    return _call(
        body, name=name, grid=grid, in_specs=[a_spec, b_spec],
        out_specs=pl.BlockSpec((tm, tn), lambda i, j, k: (i, j)),
        out_shape=jax.ShapeDtypeStruct((M, N), out_dtype),
        scratch=[pltpu.VMEM((tm, tn), F32)] if nk > 1 else [],
        sem=("parallel", "parallel", "arbitrary"))(a, b_arr)


def _rowwise(fn, rows, pars, outs, pouts, name):
    R = rows[0].shape[0]
    row_bytes = 4 * (sum(max(r.shape[1], LANES) for r in rows) + sum(max(c, LANES) for c, _ in outs))
    tb = _tile(R, max(BF16_ROWS, ROW_BLOCK_BYTES // row_bytes), BF16_ROWS)
    nr, npar, no = len(rows), len(pars), len(outs)

    def body(*refs):
        r_in, p_in = refs[:nr], refs[nr:nr + npar]
        r_out, p_out = refs[nr + npar:nr + npar + no], refs[nr + npar + no:]
        ro, po = fn([r[...] for r in r_in], [p[...] for p in p_in])
        for ref, val in zip(r_out, ro):
            if isinstance(val, (tuple, list)):
                off = 0
                for piece in val:
                    w = piece.shape[1]
                    ref[:, off:off + w] = piece.astype(ref.dtype)
                    off += w
            else:
                ref[...] = val.astype(ref.dtype)
        if p_out:
            first = pl.program_id(0) == 0

            @pl.when(first)
            def _():
                for ref, val in zip(p_out, po):
                    ref[...] = val

            @pl.when(jnp.logical_not(first))
            def _():
                for ref, val in zip(p_out, po):
                    ref[...] += val

    res = _call(
        body, name=name, grid=(R // tb,),
        in_specs=[pl.BlockSpec((tb, r.shape[1]), lambda i: (i, 0)) for r in rows]
        + [pl.BlockSpec(p.shape, lambda i: (0, 0)) for p in pars],
        out_specs=[pl.BlockSpec((tb, c), lambda i: (i, 0)) for c, _ in outs]
        + [pl.BlockSpec(s, lambda i: (0, 0)) for s in pouts],
        out_shape=[jax.ShapeDtypeStruct((R, c), dt) for c, dt in outs]
        + [jax.ShapeDtypeStruct(s, F32) for s in pouts],
        sem=("arbitrary",) if pouts else ("parallel",))(*rows, *pars)
    return list(res)


def _rms(x, g):
    return x * lax.rsqrt(jnp.mean(x * x, axis=-1, keepdims=True) + NORM_EPS) * g


def _norm_mod(x, g, sh, sc):
    return _rms(x, g) * (1.0 + sc) + sh


def _gated_post(y, g, gate):
    return gate * _rms(y, g)


def _norm_mod_fwd(x, g, sh, sc, name):
    return _rowwise(lambda r, p: ([_norm_mod(r[0], *p)], []), [x], [g, sh, sc], [(x.shape[1], BF16)], [], name)[0]


def _norm_mod_bwd(dxo, dh, x, g, sh, sc, name):
    def fn(r, p):
        _, vjp = jax.vjp(_norm_mod, r[2], *p)
        dx, dg, dsh, dsc = vjp(r[1].astype(F32))
        return [r[0] + dx], [dg, dsh, dsc]
    c = x.shape[1]
    return _rowwise(fn, [dxo, dh, x], [g, sh, sc], [(c, F32)], [(1, c)] * 3, name)


def _post_fwd(x, y, g, gate, name):
    return _rowwise(lambda r, p: ([r[0] + _gated_post(r[1].astype(F32), *p)], []), [x, y], [g, gate],
                    [(x.shape[1], F32)], [], name)[0]


def _post_bwd(dxo, y, g, gate, name):
    def fn(r, p):
        _, vjp = jax.vjp(_gated_post, r[1].astype(F32), *p)
        dy, dg, dgate = vjp(r[0])
        return [dy], [dg, dgate]
    c = y.shape[1]
    return _rowwise(fn, [dxo, y], [g, gate], [(c, BF16)], [(1, c)] * 2, name)


def _swiglu(g, u):
    return jax.nn.silu(g) * u


def _swiglu_fwd(gu, name):
    f = gu.shape[1] // 2
    return _rowwise(lambda r, p: ([_swiglu(r[0][:, :f].astype(F32), r[0][:, f:].astype(F32))], []), [gu], [],
                    [(f, BF16)], [], name)[0]


def _swiglu_bwd(da, gu, name):
    f = gu.shape[1] // 2

    def fn(r, p):
        _, vjp = jax.vjp(_swiglu, r[1][:, :f].astype(F32), r[1][:, f:].astype(F32))
        return [vjp(r[0].astype(F32))], []
    return _rowwise(fn, [da, gu], [], [(2 * f, BF16)], [], name)[0]


def _silu_rows(c, name):
    return _rowwise(lambda r, p: ([jax.nn.silu(r[0])], []), [c], [], [(c.shape[1], F32)], [], name)[0]


def _head_norm(x, g, scale):
    return _rms(x, g) * scale


def _head_norm_fwd(x, g, scale, name):
    return _rowwise(lambda r, p: ([_head_norm(r[0].astype(F32), p[0], scale)], []), [x], [g],
                    [(x.shape[1], BF16)], [], name)[0]


def _head_norm_bwd(dy, x, g, scale, name):
    def fn(r, p):
        _, vjp = jax.vjp(lambda t, gg: _head_norm(t, gg, scale), r[1].astype(F32), p[0])
        dx, dg = vjp(r[0])
        return [dx], [dg]
    c = x.shape[1]
    return _rowwise(fn, [dy, x], [g], [(c, F32)], [(1, c)], name)


def _out_gate_fwd(o, qg, name):
    d = o.shape[1]
    return _rowwise(lambda r, p: ([r[0] * jax.nn.sigmoid(r[1][:, d:].astype(F32))], []), [o, qg], [],
                    [(d, BF16)], [], name)[0]


def _out_gate_bwd(dog, o, qg, name):
    d = o.shape[1]

    def fn(r, p):
        _, vjp = jax.vjp(lambda oo, gl: oo * jax.nn.sigmoid(gl), r[1], r[2][:, d:].astype(F32))
        do, dgl = vjp(r[0])
        return [do, dgl], []
    return _rowwise(fn, [dog, o, qg], [], [(d, BF16), (d, BF16)], [], name)


def _loss_bwd(y, tgt, name):
    n = y.shape[1]

    def fn(r, p):
        e = r[0] - r[1]
        part = jnp.sum(jnp.sum(e * e, axis=1, keepdims=True), axis=0, keepdims=True) * (0.5 / n)
        return [e * (1.0 / n)], [part]
    return _rowwise(fn, [y, tgt], [], [(n, F32)], [(1, 1)], name)


def _adamw(w, g, m, v, name):
    shape = w.shape
    c = shape[-1]
    flat = [t.reshape(-1, c) for t in (w, g, m, v)]

    def fn(r, p):
        w_, g_, m_, v_ = r
        m2 = ADAM_B1 * m_ + (1.0 - ADAM_B1) * g_
        v2 = ADAM_B2 * v_ + (1.0 - ADAM_B2) * (g_ * g_)
        m_hat = m2 / (1.0 - ADAM_B1 ** ADAM_STEP)
        v_hat = v2 / (1.0 - ADAM_B2 ** ADAM_STEP)
        delta = -ADAM_LR * (m_hat / (jnp.sqrt(v_hat) + ADAM_EPS) + ADAM_WD * w_)
        return [delta, m2, v2], []
    res = _rowwise(fn, flat, [], [(c, F32)] * 3, [], name)
    return [t.reshape(shape) for t in res]


def _sum_slots(recv, name, out_dtype=F32):
    n = recv.shape[0]
    shape = recv.shape[1:]
    c = shape[-1]
    r3 = recv.reshape(n, -1, c)
    rows = r3.shape[1]
    tb = _tile(rows, max(BF16_ROWS, ROW_BLOCK_BYTES // (4 * c * (n + 1))), BF16_ROWS)

    def body(r_ref, o_ref):
        acc = r_ref[0].astype(F32)
        for s in range(1, n):
            acc = acc + r_ref[s].astype(F32)
        o_ref[...] = acc.astype(o_ref.dtype)

    out = _call(body, name=name, grid=(rows // tb,),
                in_specs=[pl.BlockSpec((n, tb, c), lambda i: (0, i, 0))],
                out_specs=pl.BlockSpec((tb, c), lambda i: (i, 0)),
                out_shape=jax.ShapeDtypeStruct((rows, c), out_dtype), sem=("parallel",))(r3)
    return out.reshape(shape)


def _gmlp_pre(zu, zv, b_u, b_v, ln_g, ln_b):
    u = jax.nn.gelu(zu + b_u, approximate=True)
    v = jax.nn.gelu(zv + b_v, approximate=True)
    xc = v - jnp.mean(v, axis=-1, keepdims=True)
    vn = xc * lax.rsqrt(jnp.mean(xc * xc, axis=-1, keepdims=True) + NORM_EPS) * ln_g + ln_b
    return u, vn


def _gmlp_fwd(zp, b_in, ln_g, ln_b, ws, bs_t, name):
    S, gw2 = zp.shape
    gw = gw2 // 2
    G, ch, _ = ws.shape
    gd = gw // G
    tb = 2 * ch

    def body(zp_ref, bin_ref, lg_ref, lb_ref, ws_ref, bs_ref, o_ref):
        u, vn = _gmlp_pre(zp_ref[:, :gw].astype(F32), zp_ref[:, gw:].astype(F32), bin_ref[:, :gw], bin_ref[:, gw:],
                          lg_ref[...], lb_ref[...])
        vnb = vn.astype(BF16)
        for c in range(tb // ch):
            for g in range(G):
                rs, cs = slice(c * ch, (c + 1) * ch), slice(g * gd, (g + 1) * gd)
                vv = jnp.dot(ws_ref[g], vnb[rs, cs], preferred_element_type=F32) + bs_ref[:, g:g + 1]
                o_ref[rs, cs] = (u[rs, cs] * vv).astype(o_ref.dtype)

    full = lambda a: pl.BlockSpec(a.shape, lambda i: (0,) * a.ndim)
    return _call(body, name=name, grid=(S // tb,),
                 in_specs=[pl.BlockSpec((tb, gw2), lambda i: (i, 0)), full(b_in), full(ln_g), full(ln_b), full(ws), full(bs_t)],
                 out_specs=pl.BlockSpec((tb, gw), lambda i: (i, 0)),
                 out_shape=jax.ShapeDtypeStruct((S, gw), BF16), sem=("parallel",))(zp, b_in, ln_g, ln_b, ws, bs_t)


def _gmlp_bwd(dyg, zp, b_in, ln_g, ln_b, ws, ws_t, bs_t, name):
    S, gw2 = zp.shape
    gw = gw2 // 2
    G, ch, _ = ws.shape
    gd = gw // G
    tb = 2 * ch

    def body(dy_ref, zp_ref, bin_ref, lg_ref, lb_ref, ws_ref, wst_ref, bs_ref,
             dzp_ref, dbin_ref, dlg_ref, dlb_ref, dws_ref, dbs_ref, du_sc, dvn_sc):
        (u, vn), vjp = jax.vjp(_gmlp_pre, zp_ref[:, :gw].astype(F32), zp_ref[:, gw:].astype(F32), bin_ref[:, :gw],
                               bin_ref[:, gw:], lg_ref[...], lb_ref[...])
        vnb = vn.astype(BF16)
        first = pl.program_id(0) == 0

        @pl.when(first)
        def _():
            dws_ref[...] = jnp.zeros_like(dws_ref)

        lane = lax.broadcasted_iota(jnp.int32, (ch, G), 1)
        dbs = jnp.zeros((ch, G), F32)
        for g in range(G):
            cs = slice(g * gd, (g + 1) * gd)
            dws_g = jnp.zeros((ch, ch), F32)
            col = jnp.zeros((ch, 1), F32)
            for c in range(tb // ch):
                rs = slice(c * ch, (c + 1) * ch)
                vnp = vnb[rs, cs]
                vv = jnp.dot(ws_ref[g], vnp, preferred_element_type=F32) + bs_ref[:, g:g + 1]
                dy = dy_ref[rs, cs].astype(F32)
                du_sc[rs, cs] = dy * vv
                dvv = dy * u[rs, cs]
                dvvb = dvv.astype(BF16)
                dvn_sc[rs, cs] = jnp.dot(wst_ref[g], dvvb, preferred_element_type=F32)
                dws_g = dws_g + lax.dot_general(dvvb, vnp, (((1,), (1,)), ((), ())), preferred_element_type=F32)
                col = col + jnp.sum(dvv, axis=1, keepdims=True)
            dws_ref[g] += dws_g
            dbs = jnp.where(lane == g, col, dbs)
        dzu, dzv, dbu, dbv, dlg, dlb = vjp((du_sc[...], dvn_sc[...]))
        dzp_ref[:, :gw] = dzu.astype(dzp_ref.dtype)
        dzp_ref[:, gw:] = dzv.astype(dzp_ref.dtype)

        @pl.when(first)
        def _():
            dbin_ref[:, :gw] = dbu
            dbin_ref[:, gw:] = dbv
            dlg_ref[...] = dlg
            dlb_ref[...] = dlb
            dbs_ref[...] = dbs

        @pl.when(jnp.logical_not(first))
        def _():
            dbin_ref[:, :gw] += dbu
            dbin_ref[:, gw:] += dbv
            dlg_ref[...] += dlg
            dlb_ref[...] += dlb
            dbs_ref[...] += dbs

    full = lambda a: pl.BlockSpec(a.shape, lambda i: (0,) * a.ndim)
    fshape = lambda s: pl.BlockSpec(s, lambda i: (0,) * len(s))
    return _call(
        body, name=name, grid=(S // tb,),
        in_specs=[pl.BlockSpec((tb, gw), lambda i: (i, 0)), pl.BlockSpec((tb, gw2), lambda i: (i, 0)),
                  full(b_in), full(ln_g), full(ln_b), full(ws), full(ws_t), full(bs_t)],
        out_specs=[pl.BlockSpec((tb, gw2), lambda i: (i, 0)), fshape((1, gw2)), fshape((1, gw)), fshape((1, gw)),
                   fshape((G, ch, ch)), fshape((ch, G))],
        out_shape=[jax.ShapeDtypeStruct((S, gw2), BF16), jax.ShapeDtypeStruct((1, gw2), F32),
                   jax.ShapeDtypeStruct((1, gw), F32), jax.ShapeDtypeStruct((1, gw), F32),
                   jax.ShapeDtypeStruct((G, ch, ch), F32), jax.ShapeDtypeStruct((ch, G), F32)],
        scratch=[pltpu.VMEM((tb, gw), F32), pltpu.VMEM((tb, gw), F32)],
        sem=("arbitrary",))(dyg, zp, b_in, ln_g, ln_b, ws, ws_t, bs_t)


def _dot_01(x, ones_bf16):
    hi = x.astype(BF16)
    r1 = x - hi.astype(F32)
    mid = r1.astype(BF16)
    lo = (r1 - mid.astype(F32)).astype(BF16)
    dot = lambda t: jnp.dot(t, ones_bf16, preferred_element_type=F32)
    return dot(hi) + dot(mid) + dot(lo)


def _log_sigmoid(x):
    return jnp.minimum(x, 0.0) - jnp.log1p(jnp.exp(-jnp.abs(x)))


def _dcum_fwd(f_t, b_col, name):
    H, S = f_t.shape
    tb = _tile(S, 512, LANES)

    def body(f_ref, b_ref, o_ref, carry):
        @pl.when(pl.program_id(0) == 0)
        def _():
            carry[...] = jnp.zeros_like(carry)

        ls = _log_sigmoid(f_ref[...] + b_ref[...])
        r = lax.broadcasted_iota(jnp.int32, (tb, tb), 0)
        c = lax.broadcasted_iota(jnp.int32, (tb, tb), 1)
        upper = (r <= c).astype(BF16)
        o_ref[...] = _dot_01(ls, upper) + carry[...]
        carry[...] += jnp.sum(ls, axis=1, keepdims=True)

    return _call(body, name=name, grid=(S // tb,),
                 in_specs=[pl.BlockSpec((H, tb), lambda i: (0, i)), pl.BlockSpec((H, 1), lambda i: (0, 0))],
                 out_specs=pl.BlockSpec((H, tb), lambda i: (0, i)),
                 out_shape=jax.ShapeDtypeStruct((H, S), F32),
                 scratch=[pltpu.VMEM((H, 1), F32)], sem=("arbitrary",))(f_t, b_col)


def _dcum_bwd(dd_t, f_t, b_col, name):
    H, S = f_t.shape
    tb = _tile(S, 512, LANES)
    nb = S // tb

    def body(dd_ref, f_ref, b_ref, df_ref, db_ref, carry):
        first = pl.program_id(0) == 0

        @pl.when(first)
        def _():
            carry[...] = jnp.zeros_like(carry)

        dd = dd_ref[...]
        r = lax.broadcasted_iota(jnp.int32, (tb, tb), 0)
        c = lax.broadcasted_iota(jnp.int32, (tb, tb), 1)
        lower = (r >= c).astype(BF16)
        rev = _dot_01(dd, lower) + carry[...]
        carry[...] += jnp.sum(dd, axis=1, keepdims=True)
        df = rev * jax.nn.sigmoid(-(f_ref[...] + b_ref[...]))
        df_ref[...] = df
        part = jnp.sum(df, axis=1, keepdims=True)

        @pl.when(first)
        def _():
            db_ref[...] = part

        @pl.when(jnp.logical_not(first))
        def _():
            db_ref[...] += part

    return _call(body, name=name, grid=(nb,),
                 in_specs=[pl.BlockSpec((H, tb), lambda i: (0, nb - 1 - i)), pl.BlockSpec((H, tb), lambda i: (0, nb - 1 - i)),
                           pl.BlockSpec((H, 1), lambda i: (0, 0))],
                 out_specs=[pl.BlockSpec((H, tb), lambda i: (0, nb - 1 - i)), pl.BlockSpec((H, 1), lambda i: (0, 0))],
                 out_shape=[jax.ShapeDtypeStruct((H, S), F32), jax.ShapeDtypeStruct((H, 1), F32)],
                 scratch=[pltpu.VMEM((H, 1), F32)], sem=("arbitrary",))(dd_t, f_t, b_col)


def _attn_tile(S):
    return _tile(S, 512, LANES)


def _causal(t, transposed):
    r = lax.broadcasted_iota(jnp.int32, (t, t), 0)
    c = lax.broadcasted_iota(jnp.int32, (t, t), 1)
    return (r <= c) if transposed else (c <= r)


def _tri_pairs(n, key_major):
    if key_major:
        pairs = [(i, j) for j in range(n) for i in range(j, n)]
    else:
        pairs = [(i, j) for i in range(n) for j in range(i + 1)]
    return jnp.asarray([p[0] for p in pairs], jnp.int32), jnp.asarray([p[1] for p in pairs], jnp.int32)


def _split3(x):
    hi = lax.reduce_precision(x, 8, 7)
    r = x - hi
    mid = lax.reduce_precision(r, 8, 7)
    lo = lax.reduce_precision(r - mid, 8, 7)
    return hi.astype(BF16), mid.astype(BF16), lo.astype(BF16)


def _augment(xn, dcum, query):
    H, S, hd = xn.shape
    parts = list(_split3(dcum))
    vals = parts + [1.0] * 3 if query else [1.0] * 3 + [-p for p in parts]
    lane = lax.broadcasted_iota(jnp.int32, (1, 1, LANES), 2)
    out = jnp.pad(xn, ((0, 0), (0, 0), (0, LANES - hd)))
    for k, val in enumerate(vals):
        val = jnp.asarray(val, BF16)
        out = jnp.where(lane == hd + k, val[..., None] if val.ndim else val, out)
    return out


def _scores_t(k_ref, qt_ref, h, t, diag):
    st = jnp.dot(k_ref[h], qt_ref[h], preferred_element_type=F32)
    return jnp.where(_causal(t, True), st, MASKED) if diag else st


def _flash_fwd(ka, qat, vat, hd, name):
    H, S, da = ka.shape
    t = _attn_tile(S)
    hb = ATTN_FWD_HEADS_PER_STEP
    it, jt = _tri_pairs(S // t, False)

    def body(it_ref, jt_ref, k_ref, qt_ref, vt_ref, o_ref, lse_ref, m_sc, acc_sc):
        i, j = it_ref[pl.program_id(1)], jt_ref[pl.program_id(1)]

        @pl.when(j == 0)
        def _():
            m_sc[...] = jnp.full_like(m_sc, MASKED)
            acc_sc[...] = jnp.zeros_like(acc_sc)

        def step(diag):
            for h in range(hb):
                st = _scores_t(k_ref, qt_ref, h, t, diag)
                m_prev = m_sc[h]
                m_new = jnp.maximum(m_prev, jnp.max(st, axis=0, keepdims=True))
                pt = jnp.exp(st - m_new).astype(BF16)
                acc_sc[h] = jnp.exp(m_prev - m_new) * acc_sc[h] + jnp.dot(vt_ref[h], pt, preferred_element_type=F32)
                m_sc[h] = m_new

        @pl.when(j < i)
        def _():
            step(False)

        @pl.when(j == i)
        def _():
            step(True)
            for h in range(hb):
                l = acc_sc[h, hd:hd + 1, :]
                o_ref[h] = acc_sc[h, :hd, :] / l
                lse_ref[h] = m_sc[h] + jnp.log(l)

    qcol = lambda h, p, it_, jt_: (h, 0, it_[p])
    kcol = lambda h, p, it_, jt_: (h, 0, jt_[p])
    krow = lambda h, p, it_, jt_: (h, jt_[p], 0)
    return _call_prefetch(
        body, name=name, grid=(H // hb, it.shape[0]), n_prefetch=2,
        in_specs=[pl.BlockSpec((hb, t, da), krow), pl.BlockSpec((hb, da, t), qcol), pl.BlockSpec((hb, da, t), kcol)],
        out_specs=[pl.BlockSpec((hb, hd, t), qcol), pl.BlockSpec((hb, 1, t), qcol)],
        out_shape=[jax.ShapeDtypeStruct((H, hd, S), F32), jax.ShapeDtypeStruct((H, 1, S), F32)],
        scratch=[pltpu.VMEM((hb, 1, t), F32), pltpu.VMEM((hb, da, t), F32)],
        sem=("parallel", "arbitrary"))(it, jt, ka, qat, vat)


def _ds_t(k_ref, qt_ref, v_ref, dot_ref, lse_ref, dl, h, t, diag):
    pt = jnp.exp(_scores_t(k_ref, qt_ref, h, t, diag) - lse_ref[h])
    dpt = jnp.dot(v_ref[h], dot_ref[h], preferred_element_type=F32)
    return pt, pt * (dpt - dl)


def _flash_bwd(ka, kat, qa, qat, v, do, dot, o_tr, lse_r, name):
    H, S, hd = v.shape
    da = ka.shape[2]
    t = _attn_tile(S)
    n = S // t
    hb = ATTN_HEADS_PER_STEP
    it, jt = _tri_pairs(n, True)

    def body(it_ref, jt_ref, k_ref, kt_ref, q_ref, qt_ref, v_ref, do_ref, dot_ref, o_ref, lse_ref,
             dq_ref, dk_ref, dv_ref, dd_ref, ddq_ref, dk_sc, dv_sc, dd_sc):
        i, j = it_ref[pl.program_id(1)], jt_ref[pl.program_id(1)]

        @pl.when(pl.program_id(1) == 0)
        def _():
            ddq_ref[...] = jnp.zeros_like(ddq_ref)
            dq_ref[...] = jnp.zeros_like(dq_ref)

        def step(diag):
            for h in range(hb):
                dl = jnp.sum(dot_ref[h].astype(F32) * o_ref[h], axis=0, keepdims=True)
                pt, dst = _ds_t(k_ref, qt_ref, v_ref, dot_ref, lse_ref, dl, h, t, diag)
                dsb = dst.astype(BF16)
                dv_sc[h] += jnp.dot(pt.astype(BF16), do_ref[h], preferred_element_type=F32)
                dk_sc[h] += jnp.dot(dsb, q_ref[h], preferred_element_type=F32)
                dq_ref[h, i] += jnp.dot(kt_ref[h], dsb, preferred_element_type=F32)
                part = dst[:, :LANES]
                for c in range(1, t // LANES):
                    part = part + dst[:, c * LANES:(c + 1) * LANES]
                dd_sc[h] += part
                ddq_ref[h, i] += jnp.sum(dst, axis=0, keepdims=True)

        @pl.when(i == j)
        def _():
            dk_sc[...] = jnp.zeros_like(dk_sc)
            dv_sc[...] = jnp.zeros_like(dv_sc)
            dd_sc[...] = jnp.zeros_like(dd_sc)
            step(True)

        @pl.when(i > j)
        def _():
            step(False)

        @pl.when(i == n - 1)
        def _():
            dk_ref[...] = dk_sc[...]
            dv_ref[...] = dv_sc[...]
            for h in range(hb):
                dd_ref[h] = -jnp.sum(dd_sc[h], axis=1, keepdims=True)

    krow = lambda h, p, it_, jt_: (h, jt_[p], 0)
    kcol = lambda h, p, it_, jt_: (h, 0, jt_[p])
    qrow = lambda h, p, it_, jt_: (h, it_[p], 0)
    qcol = lambda h, p, it_, jt_: (h, 0, it_[p])
    whole = lambda h, p, it_, jt_: (h, 0, 0, 0)
    return _call_prefetch(
        body, name=name, grid=(H // hb, it.shape[0]), n_prefetch=2,
        in_specs=[pl.BlockSpec((hb, t, da), krow), pl.BlockSpec((hb, da, t), kcol), pl.BlockSpec((hb, t, da), qrow),
                  pl.BlockSpec((hb, da, t), qcol), pl.BlockSpec((hb, t, hd), krow), pl.BlockSpec((hb, t, hd), qrow),
                  pl.BlockSpec((hb, hd, t), qcol), pl.BlockSpec((hb, hd, t), qcol), pl.BlockSpec((hb, 1, t), qcol)],
        out_specs=[pl.BlockSpec((hb, n, da, t), whole), pl.BlockSpec((hb, t, da), krow), pl.BlockSpec((hb, t, hd), krow),
                   pl.BlockSpec((hb, t, 1), krow), pl.BlockSpec((hb, n, 1, t), whole)],
        out_shape=[jax.ShapeDtypeStruct((H, n, da, t), F32), jax.ShapeDtypeStruct((H, S, da), F32),
                   jax.ShapeDtypeStruct((H, S, hd), F32), jax.ShapeDtypeStruct((H, S, 1), F32),
                   jax.ShapeDtypeStruct((H, n, 1, t), F32)],
        scratch=[pltpu.VMEM((hb, t, da), F32), pltpu.VMEM((hb, t, hd), F32), pltpu.VMEM((hb, t, LANES), F32)],
        sem=("parallel", "arbitrary"))(it, jt, ka, kat, qa, qat, v, do, dot, o_tr, lse_r)


def _offsets(n_bits):
    return [tuple((k >> b) & 1 for b in reversed(range(n_bits))) for k in range(1, 1 << n_bits)]


def _gather8(arrs, name):
    n = len(arrs)
    offs = _offsets(3)

    def body(*refs):
        ins, outs = refs[:n], refs[n:2 * n]
        ssem, rsem, lsem = refs[2 * n:]
        x, y, c = _place()
        me = 4 * x + 2 * y + c
        copies = []
        for a in range(n):
            lc = pltpu.make_async_copy(ins[a], outs[a].at[me], lsem.at[a])
            lc.start()
            copies.append(lc)
            for k, (dx, dy, dcc) in enumerate(offs):
                cp = pltpu.make_async_remote_copy(
                    src_ref=ins[a], dst_ref=outs[a].at[me], send_sem=ssem.at[a, k], recv_sem=rsem.at[a, k],
                    device_id=((x + dx) % 2, (y + dy) % 2, (c + dcc) % 2), device_id_type=MESH)
                cp.start()
                copies.append(cp)
        for cp in copies:
            cp.wait()

    return _call(body, name=name, in_specs=[ANY] * n, out_specs=[ANY] * n,
                 out_shape=[jax.ShapeDtypeStruct((N_DEV,) + a.shape, a.dtype) for a in arrs],
                 scratch=[pltpu.SemaphoreType.DMA((n, 7)), pltpu.SemaphoreType.DMA((n, 7)), pltpu.SemaphoreType.DMA((n,))])(*arrs)


def _chip_gather(arrs, halved, name):
    n = len(arrs)
    offs = _offsets(2)

    def body(*refs):
        ins, outs = refs[:n], refs[n:2 * n]
        ssem, rsem, lsem = refs[2 * n:]
        x, y, c = _place()
        chip = 2 * x + y
        copies = []
        for a in range(n):
            lc = pltpu.make_async_copy(ins[a], outs[a].at[chip], lsem.at[a])
            lc.start()
            copies.append(lc)
            if halved:
                hn = arrs[a].shape[0] // 2
                src = ins[a].at[pl.ds(c * hn, hn)]
                dst = outs[a].at[chip, pl.ds(c * hn, hn)]
            else:
                src, dst = ins[a], outs[a].at[chip]
            for k, (dx, dy) in enumerate(offs):
                cp = pltpu.make_async_remote_copy(
                    src_ref=src, dst_ref=dst, send_sem=ssem.at[a, k], recv_sem=rsem.at[a, k],
                    device_id=((x + dx) % 2, (y + dy) % 2, c), device_id_type=MESH)
                cp.start()
                copies.append(cp)
        for cp in copies:
            cp.wait()

    return _call(body, name=name, in_specs=[ANY] * n, out_specs=[ANY] * n,
                 out_shape=[jax.ShapeDtypeStruct((N_CHIPS,) + a.shape, a.dtype) for a in arrs],
                 scratch=[pltpu.SemaphoreType.DMA((n, 3)), pltpu.SemaphoreType.DMA((n, 3)), pltpu.SemaphoreType.DMA((n,))])(*arrs)


def _sibling_fill(bufs, name):
    n = len(bufs)
    offs = _offsets(2)

    def body(*refs):
        ins, outs = refs[:n], refs[n:2 * n]
        ssem, rsem = refs[2 * n:]
        x, y, c = _place()
        copies = []
        for a in range(n):
            hn = bufs[a].shape[1] // 2
            for k, (dx, dy) in enumerate(offs):
                chip = 2 * ((x + dx) % 2) + (y + dy) % 2
                cp = pltpu.make_async_remote_copy(
                    src_ref=ins[a].at[chip, pl.ds(c * hn, hn)], dst_ref=outs[a].at[chip, pl.ds(c * hn, hn)],
                    send_sem=ssem.at[a, k], recv_sem=rsem.at[a, k],
                    device_id=(x, y, 1 - c), device_id_type=MESH)
                cp.start()
                copies.append(cp)
        for cp in copies:
            cp.wait()

    return _call(body, name=name, in_specs=[ANY] * n, out_specs=[ANY] * n,
                 out_shape=[jax.ShapeDtypeStruct(b.shape, b.dtype) for b in bufs],
                 scratch=[pltpu.SemaphoreType.DMA((n, 3)), pltpu.SemaphoreType.DMA((n, 3))],
                 aliases={a: a for a in range(n)})(*bufs)


def _sibling_pair(arrs, name):
    n = len(arrs)

    def body(*refs):
        ins, outs = refs[:n], refs[n:2 * n]
        ssem, rsem, lsem = refs[2 * n:]
        x, y, c = _place()
        copies = []
        for a in range(n):
            lc = pltpu.make_async_copy(ins[a], outs[a].at[c], lsem.at[a])
            lc.start()
            cp = pltpu.make_async_remote_copy(
                src_ref=ins[a], dst_ref=outs[a].at[c], send_sem=ssem.at[a], recv_sem=rsem.at[a],
                device_id=(x, y, 1 - c), device_id_type=MESH)
            cp.start()
            copies += [lc, cp]
        for cp in copies:
            cp.wait()

    return _call(body, name=name, in_specs=[ANY] * n, out_specs=[ANY] * n,
                 out_shape=[jax.ShapeDtypeStruct((N_CORES,) + a.shape, a.dtype) for a in arrs],
                 scratch=[pltpu.SemaphoreType.DMA((n,)), pltpu.SemaphoreType.DMA((n,)), pltpu.SemaphoreType.DMA((n,))])(*arrs)


def _piece(shape, spec, j, h):
    shard_ax, half_ax = spec
    w = shape[shard_ax] // N_CHIPS
    idx = [slice(None)] * len(shape)
    idx[shard_ax] = pl.ds(j * w, w)
    hn = (w if half_ax == shard_ax else shape[half_ax]) // 2
    assert half_ax != shard_ax
    idx[half_ax] = pl.ds(h * hn, hn)
    return tuple(idx)


def _piece_shape(shape, spec):
    shard_ax, half_ax = spec
    s = list(shape)
    s[shard_ax] //= N_CHIPS
    s[half_ax] //= 2
    return tuple(s)


def _sibling_scatter(arrs, specs, name):
    n = len(arrs)

    def body(*refs):
        ins, outs = refs[:n], refs[n:2 * n]
        ssem, rsem, lsem = refs[2 * n:]
        x, y, c = _place()
        for mine in range(N_CORES):
            @pl.when(c == mine)
            def _():
                copies = []
                for a in range(n):
                    for j in range(N_CHIPS):
                        keep = ins[a].at[_piece(arrs[a].shape, specs[a], j, mine)]
                        give = ins[a].at[_piece(arrs[a].shape, specs[a], j, 1 - mine)]
                        lc = pltpu.make_async_copy(keep, outs[a].at[0, j], lsem.at[a, j])
                        cp = pltpu.make_async_remote_copy(
                            src_ref=give, dst_ref=outs[a].at[1, j], send_sem=ssem.at[a, j], recv_sem=rsem.at[a, j],
                            device_id=(x, y, 1 - mine), device_id_type=MESH)
                        lc.start()
                        cp.start()
                        copies += [lc, cp]
                for cp in copies:
                    cp.wait()

    return _call(body, name=name, in_specs=[ANY] * n, out_specs=[ANY] * n,
                 out_shape=[jax.ShapeDtypeStruct((N_CORES, N_CHIPS) + _piece_shape(a.shape, s), a.dtype)
                            for a, s in zip(arrs, specs)],
                 scratch=[pltpu.SemaphoreType.DMA((n, N_CHIPS))] * 3)(*arrs)


def _chip_scatter(arrs, name):
    n = len(arrs)
    offs = _offsets(2)

    def body(*refs):
        ins, outs = refs[:n], refs[n:2 * n]
        ssem, rsem, lsem = refs[2 * n:]
        x, y, c = _place()
        chip = 2 * x + y
        copies = []
        for a in range(n):
            lc = pltpu.make_async_copy(ins[a].at[chip], outs[a].at[chip], lsem.at[a])
            lc.start()
            copies.append(lc)
            for k, (dx, dy) in enumerate(offs):
                tx, ty = (x + dx) % 2, (y + dy) % 2
                cp = pltpu.make_async_remote_copy(
                    src_ref=ins[a].at[2 * tx + ty], dst_ref=outs[a].at[chip], send_sem=ssem.at[a, k], recv_sem=rsem.at[a, k],
                    device_id=(tx, ty, c), device_id_type=MESH)
                cp.start()
                copies.append(cp)
        for cp in copies:
            cp.wait()

    return _call(body, name=name, in_specs=[ANY] * n, out_specs=[ANY] * n,
                 out_shape=[jax.ShapeDtypeStruct(a.shape, a.dtype) for a in arrs],
                 scratch=[pltpu.SemaphoreType.DMA((n, 3)), pltpu.SemaphoreType.DMA((n, 3)), pltpu.SemaphoreType.DMA((n,))])(*arrs)


def kernel(x, c, ada_w, ada_b, pre_mix_g, post_mix_g, pre_ffn_g, post_ffn_g, ffn_w_gu, ffn_w_down, a_w_in, a_b_in, a_ln_g, a_ln_b, a_w_s, a_b_s, a_w_out, kv_ada_w, kv_ada_b, kv_norm_g, kv_w, kv_b_f, k_norm_g, b_w_qg, b_q_norm_g, b_w_o, loss_target, m_ada_w, m_ada_b, m_pre_mix_g, m_post_mix_g, m_pre_ffn_g, m_post_ffn_g, m_ffn_w_gu, m_ffn_w_down, m_a_w_in, m_a_b_in, m_a_ln_g, m_a_ln_b, m_a_w_s, m_a_b_s, m_a_w_out, m_kv_ada_w, m_kv_ada_b, m_kv_norm_g, m_kv_w, m_kv_b_f, m_k_norm_g, m_b_w_qg, m_b_q_norm_g, m_b_w_o, v_ada_w, v_ada_b, v_pre_mix_g, v_post_mix_g, v_pre_ffn_g, v_post_ffn_g, v_ffn_w_gu, v_ffn_w_down, v_a_w_in, v_a_b_in, v_a_ln_g, v_a_ln_b, v_a_w_s, v_a_b_s, v_a_w_out, v_kv_ada_w, v_kv_ada_b, v_kv_norm_g, v_kv_w, v_kv_b_f, v_k_norm_g, v_b_w_qg, v_b_q_norm_g, v_b_w_o):
    weights = dict(ada_w=ada_w, ada_b=ada_b, pre_mix_g=pre_mix_g, post_mix_g=post_mix_g, pre_ffn_g=pre_ffn_g,
                   post_ffn_g=post_ffn_g, ffn_w_gu=ffn_w_gu, ffn_w_down=ffn_w_down, a_w_in=a_w_in, a_b_in=a_b_in,
                   a_ln_g=a_ln_g, a_ln_b=a_ln_b, a_w_s=a_w_s, a_b_s=a_b_s, a_w_out=a_w_out, kv_ada_w=kv_ada_w,
                   kv_ada_b=kv_ada_b, kv_norm_g=kv_norm_g, kv_w=kv_w, kv_b_f=kv_b_f, k_norm_g=k_norm_g, b_w_qg=b_w_qg,
                   b_q_norm_g=b_q_norm_g, b_w_o=b_w_o)
    m_in = dict(ada_w=m_ada_w, ada_b=m_ada_b, pre_mix_g=m_pre_mix_g, post_mix_g=m_post_mix_g, pre_ffn_g=m_pre_ffn_g,
                post_ffn_g=m_post_ffn_g, ffn_w_gu=m_ffn_w_gu, ffn_w_down=m_ffn_w_down, a_w_in=m_a_w_in, a_b_in=m_a_b_in,
                a_ln_g=m_a_ln_g, a_ln_b=m_a_ln_b, a_w_s=m_a_w_s, a_b_s=m_a_b_s, a_w_out=m_a_w_out, kv_ada_w=m_kv_ada_w,
                kv_ada_b=m_kv_ada_b, kv_norm_g=m_kv_norm_g, kv_w=m_kv_w, kv_b_f=m_kv_b_f, k_norm_g=m_k_norm_g,
                b_w_qg=m_b_w_qg, b_q_norm_g=m_b_q_norm_g, b_w_o=m_b_w_o)
    v_in = dict(ada_w=v_ada_w, ada_b=v_ada_b, pre_mix_g=v_pre_mix_g, post_mix_g=v_post_mix_g, pre_ffn_g=v_pre_ffn_g,
                post_ffn_g=v_post_ffn_g, ffn_w_gu=v_ffn_w_gu, ffn_w_down=v_ffn_w_down, a_w_in=v_a_w_in, a_b_in=v_a_b_in,
                a_ln_g=v_a_ln_g, a_ln_b=v_a_ln_b, a_w_s=v_a_w_s, a_b_s=v_a_b_s, a_w_out=v_a_w_out, kv_ada_w=v_kv_ada_w,
                kv_ada_b=v_kv_ada_b, kv_norm_g=v_kv_norm_g, kv_w=v_kv_w, kv_b_f=v_kv_b_f, k_norm_g=v_k_norm_g,
                b_w_qg=v_b_w_qg, b_q_norm_g=v_b_q_norm_g, b_w_o=v_b_w_o)
    names = list(weights)

    S, D = x.shape[1], x.shape[2]
    L, NA, NB = ada_w.shape[0], a_w_in.shape[0], b_w_qg.shape[0]
    H = kv_b_f.shape[0]
    hd = D // H
    G, CH = a_w_s.shape[1], a_w_s.shape[2]
    GW = a_w_out.shape[1] * N_CHIPS
    F = ffn_w_down.shape[1] * N_CHIPS
    ada_cols = ada_w.shape[2]
    kvada_cols = kv_ada_w.shape[1]
    kv_cols = kv_w.shape[1]
    kv_pad = -(-(2 * D + H) // LANES) * LANES
    xi, yi, ci = _place()
    chip = 2 * xi + yi
    me = 2 * chip + ci
    x0 = x[0]
    tgt = loss_target[0]
    row = lambda t: t.reshape(1, -1)

    c_all = _gather8([c], "gather_c")[0].reshape(N_DEV, D)
    c_act = _silu_rows(jnp.pad(c_all, ((0, BF16_ROWS - N_DEV), (0, 0))), "silu_c")
    mod_sh = [_mm(c_act, (ada_w, l), "nn", F32, f"mod_proj_{l}") for l in range(L)]
    mod_sh.append(_mm(c_act, kv_ada_w, "nn", F32, "mod_proj_kv"))
    mod_sh = jnp.concatenate(mod_sh, axis=1)
    mod_all, b_in_all, ln_g_all, ln_b_all = _chip_gather([mod_sh, a_b_in, a_ln_g, a_ln_b], False, "gather_mod")
    mine = lax.dynamic_index_in_dim(mod_all, me, axis=1, keepdims=False)
    mod = [jnp.concatenate([mine[j, l * ada_cols:(l + 1) * ada_cols] for j in range(N_CHIPS)]) + ada_b[l] for l in range(L)]
    mod = [[row(t) for t in jnp.split(m_, 6)] for m_ in mod]
    mod_kv = jnp.concatenate([mine[j, L * ada_cols:] for j in range(N_CHIPS)]) + kv_ada_b
    kv_sh, kv_sc = [row(t) for t in jnp.split(mod_kv, 2)]
    cat_chips = lambda t, ax: jnp.concatenate([t[j] for j in range(N_CHIPS)], axis=ax)
    b_in_f = cat_chips(b_in_all, 1)
    ln_g_f, ln_b_f = cat_chips(ln_g_all, 1), cat_chips(ln_b_all, 1)

    big = ["ffn_w_gu", "ffn_w_down", "a_w_in", "a_w_out", "kv_w", "b_w_qg", "b_w_o"]
    gathered = _chip_gather([weights[n].astype(BF16) for n in big], True, "gather_w")
    gathered = dict(zip(big, _sibling_fill(gathered, "fill_w")))
    w_gu = cat_chips(gathered["ffn_w_gu"], 2)
    w_dn = cat_chips(gathered["ffn_w_down"], 1)
    w_in = cat_chips(gathered["a_w_in"], 2)
    w_out = cat_chips(gathered["a_w_out"], 1)
    w_kv = jnp.pad(cat_chips(gathered["kv_w"], 1), ((0, 0), (0, kv_pad - (2 * D + H))))
    w_qg = cat_chips(gathered["b_w_qg"], 2)
    w_o = cat_chips(gathered["b_w_o"], 1)

    causal = jnp.tril(jnp.ones((CH, CH), F32))
    ws_m = [(a_w_s[i] * causal).astype(BF16) for i in range(NA)]
    ws_mt = [jnp.swapaxes(w, 1, 2) for w in ws_m]
    bs_t = [a_b_s[i].T for i in range(NA)]

    heads = lambda t: t.reshape(S, H, hd).transpose(1, 0, 2)
    unheads = lambda t: t.transpose(1, 0, 2).reshape(S, D)

    saved = []
    kv = None
    xc = x0
    for l in range(L):
        sh_m, sc_m, g_m, sh_f, sc_f, g_f = mod[l]
        st = {"x0": xc}
        h1 = _norm_mod_fwd(xc, row(pre_mix_g[l]), sh_m, sc_m, f"pre_mix_{l}")
        st["h1"] = h1
        if l < NA:
            zp = _mm(h1, (w_in, l), "nn", BF16, f"gmlp_in_{l}")
            yg = _gmlp_fwd(zp, row(b_in_f[l]), row(ln_g_f[l]), row(ln_b_f[l]), ws_m[l], bs_t[l], f"gmlp_gate_{l}")
            y = _mm(yg, (w_out, l), "nn", F32, f"gmlp_out_{l}")
            st.update(zp=zp, yg=yg)
        else:
            jb = l - NA
            qg = _mm(h1, (w_qg, jb), "nn", BF16, f"fox_qg_{jb}")
            q_raw = heads(qg[:, :D]).reshape(H * S, hd)
            qn = _head_norm_fwd(q_raw, row(b_q_norm_g[jb]), hd ** -0.5, f"fox_qnorm_{jb}").reshape(H, S, hd)
            qa = _augment(qn, kv["dcum"], True)
            qat = jnp.swapaxes(qa, 1, 2)
            o_tr, lse_r = _flash_fwd(kv["ka"], qat, kv["vat"], hd, f"fox_attn_{jb}")
            o_t = o_tr.transpose(2, 0, 1).reshape(S, D)
            og = _out_gate_fwd(o_t, qg, f"fox_gate_{jb}")
            y = _mm(og, (w_o, jb), "nn", F32, f"fox_out_{jb}")
            st.update(qg=qg, q_raw=q_raw, qa=qa, qat=qat, o_tr=o_tr, lse_r=lse_r, o_t=o_t, og=og)
        st["y"] = y
        x1 = _post_fwd(xc, y, row(post_mix_g[l]), g_m, f"post_mix_{l}")
        st["x1"] = x1
        h2 = _norm_mod_fwd(x1, row(pre_ffn_g[l]), sh_f, sc_f, f"pre_ffn_{l}")
        gu = _mm(h2, (w_gu, l), "nn", BF16, f"ffn_gu_{l}")
        act = _swiglu_fwd(gu, f"ffn_act_{l}")
        y2 = _mm(act, (w_dn, l), "nn", F32, f"ffn_down_{l}")
        xc = _post_fwd(x1, y2, row(post_ffn_g[l]), g_f, f"post_ffn_{l}")
        st.update(h2=h2, gu=gu, act=act, y2=y2)
        saved.append(st)
        if l == NA - 1:
            hk = _norm_mod_fwd(xc, row(kv_norm_g), kv_sh, kv_sc, "kv_pre")
            kvf = _mm(hk, w_kv, "nn", F32, "kv_proj")
            k_raw = heads(kvf[:, :D]).reshape(H * S, hd)
            kn = _head_norm_fwd(k_raw, row(k_norm_g), 1.0, "kv_knorm").reshape(H, S, hd)
            vb = heads(kvf[:, D:2 * D]).astype(BF16)
            f_t = kvf[:, 2 * D:2 * D + H].T
            b_col = kv_b_f.reshape(H, 1)
            dcum = _dcum_fwd(f_t, b_col, "kv_dcum")
            vt = kvf[:, D:2 * D].astype(BF16).reshape(S, H, hd).transpose(1, 2, 0)
            vat = jnp.where(lax.broadcasted_iota(jnp.int32, (1, LANES, 1), 1) == hd, jnp.asarray(1, BF16),
                            jnp.pad(vt, ((0, 0), (0, LANES - hd), (0, 0))))
            ka = _augment(kn, dcum, False)
            kv = dict(x=xc, hk=hk, k_raw=k_raw, ka=ka, kat=jnp.swapaxes(ka, 1, 2), vb=vb, vat=vat,
                      f_t=f_t, b_col=b_col, dcum=dcum)

    dx, loss_part = _loss_bwd(xc, tgt, "loss")
    loss = lax.psum(loss_part[0, 0], ("x", "y", "c"))

    gl = {n: [None] * weights[n].shape[0] for n in
          ["pre_mix_g", "post_mix_g", "pre_ffn_g", "post_ffn_g", "ffn_w_gu", "ffn_w_down", "a_w_in", "a_b_in", "a_ln_g",
           "a_ln_b", "a_w_s", "a_b_s", "a_w_out", "b_w_qg", "b_q_norm_g", "b_w_o"]}
    dmod = [None] * L
    dkn = dvb = ddc = None
    gkv = {}
    for l in reversed(range(L)):
        st = saved[l]
        sh_m, sc_m, g_m, sh_f, sc_f, g_f = mod[l]
        if l == NA - 1:
            dk_raw, gkv["k_norm_g"] = _head_norm_bwd(dkn.reshape(H * S, hd), kv["k_raw"], row(k_norm_g), 1.0, "kv_knorm_bwd")
            df_t, db_f = _dcum_bwd(ddc.reshape(H, S), kv["f_t"], kv["b_col"], "kv_dcum_bwd")
            dkvf = jnp.concatenate([unheads(dk_raw.reshape(H, S, hd)), unheads(dvb), df_t.T,
                                    jnp.zeros((S, kv_pad - (2 * D + H)), F32)], axis=1).astype(BF16)
            gkv["kv_w"] = _mm(kv["hk"], dkvf, "tn", BF16, "kv_proj_dw")[:, :2 * D + H]
            dhk = _mm(dkvf, w_kv, "nt", F32, "kv_proj_dx")
            dx, gkv["kv_norm_g"], dsh, dsc = _norm_mod_bwd(dx, dhk, kv["x"], row(kv_norm_g), kv_sh, kv_sc, "kv_pre_bwd")
            gkv["kv_b_f"] = db_f.reshape(H)
            dmod_kv = jnp.concatenate([dsh, dsc], axis=1)
        dy2, gl["post_ffn_g"][l], dg_f = _post_bwd(dx, st["y2"], row(post_ffn_g[l]), g_f, f"post_ffn_bwd_{l}")
        gl["ffn_w_down"][l] = _mm(st["act"], dy2, "tn", BF16, f"ffn_down_dw_{l}")
        dact = _mm(dy2, (w_dn, l), "nt", BF16, f"ffn_down_dx_{l}")
        dgu = _swiglu_bwd(dact, st["gu"], f"ffn_act_bwd_{l}")
        gl["ffn_w_gu"][l] = _mm(st["h2"], dgu, "tn", BF16, f"ffn_gu_dw_{l}")
        dh2 = _mm(dgu, (w_gu, l), "nt", F32, f"ffn_gu_dx_{l}")
        dx, gl["pre_ffn_g"][l], dsh_f, dsc_f = _norm_mod_bwd(dx, dh2, st["x1"], row(pre_ffn_g[l]), sh_f, sc_f, f"pre_ffn_bwd_{l}")
        dy, gl["post_mix_g"][l], dg_m = _post_bwd(dx, st["y"], row(post_mix_g[l]), g_m, f"post_mix_bwd_{l}")
        if l < NA:
            gl["a_w_out"][l] = _mm(st["yg"], dy, "tn", BF16, f"gmlp_out_dw_{l}")
            dyg = _mm(dy, (w_out, l), "nt", BF16, f"gmlp_out_dx_{l}")
            dzp, db_in, dlg, dlb, dws, dbs_t = _gmlp_bwd(dyg, st["zp"], row(b_in_f[l]), row(ln_g_f[l]), row(ln_b_f[l]),
                                                           ws_m[l], ws_mt[l], bs_t[l], f"gmlp_gate_bwd_{l}")
            gl["a_b_in"][l], gl["a_ln_g"][l], gl["a_ln_b"][l] = db_in[0], dlg[0], dlb[0]
            gl["a_w_s"][l], gl["a_b_s"][l] = dws * causal, dbs_t.T
            gl["a_w_in"][l] = _mm(st["h1"], dzp, "tn", BF16, f"gmlp_in_dw_{l}")
            dh1 = _mm(dzp, (w_in, l), "nt", F32, f"gmlp_in_dx_{l}")
        else:
            jb = l - NA
            gl["b_w_o"][jb] = _mm(st["og"], dy, "tn", BF16, f"fox_out_dw_{jb}")
            dog = _mm(dy, (w_o, jb), "nt", F32, f"fox_out_dx_{jb}")
            do_t, dgl = _out_gate_bwd(dog, st["o_t"], st["qg"], f"fox_gate_bwd_{jb}")
            do = heads(do_t)
            dot = do_t.reshape(S, H, hd).transpose(1, 2, 0)
            dqa_tr, dk_j, dv_j, dd_k, dd_q = _flash_bwd(kv["ka"], kv["kat"], st["qa"], st["qat"], kv["vb"], do, dot,
                                                        st["o_tr"], st["lse_r"], f"fox_attn_bwd_{jb}")
            dqn = dqa_tr[:, :, :hd, :].transpose(0, 1, 3, 2).reshape(H, S, hd)
            dk_j = dk_j[:, :, :hd]
            dd_j = dd_k.reshape(H, S) + dd_q.reshape(H, S)
            dkn = dk_j if dkn is None else dkn + dk_j
            dvb = dv_j if dvb is None else dvb + dv_j
            ddc = dd_j if ddc is None else ddc + dd_j
            dq_raw, dgq = _head_norm_bwd(dqn.reshape(H * S, hd), st["q_raw"], row(b_q_norm_g[jb]), hd ** -0.5, f"fox_qnorm_bwd_{jb}")
            gl["b_q_norm_g"][jb] = dgq[0]
            dqg = jnp.concatenate([unheads(dq_raw.reshape(H, S, hd)).astype(BF16), dgl], axis=1)
            gl["b_w_qg"][jb] = _mm(st["h1"], dqg, "tn", BF16, f"fox_qg_dw_{jb}")
            dh1 = _mm(dqg, (w_qg, jb), "nt", F32, f"fox_qg_dx_{jb}")
        dx, gl["pre_mix_g"][l], dsh_m, dsc_m = _norm_mod_bwd(dx, dh1, st["x0"], row(pre_mix_g[l]), sh_m, sc_m, f"pre_mix_bwd_{l}")
        dmod[l] = jnp.concatenate([dsh_m, dsc_m, dg_m, dsh_f, dsc_f, dg_f], axis=1)
    grad_x = dx[None]

    stack = lambda n: jnp.stack([t.reshape(weights[n].shape[1:]) for t in gl[n]])
    small = {"dmod": jnp.concatenate(dmod, axis=1), "dmod_kv": dmod_kv}
    for n in ["pre_mix_g", "post_mix_g", "pre_ffn_g", "post_ffn_g", "a_w_s", "a_b_s", "b_q_norm_g"]:
        small[n] = stack(n)
    for n in ["a_b_in", "a_ln_g", "a_ln_b"]:
        small[n] = jnp.stack(gl[n])
    for n in ["kv_norm_g", "kv_b_f", "k_norm_g"]:
        small[n] = gkv[n]
    sizes = {n: t.size for n, t in small.items()}
    flat = jnp.concatenate([t.reshape(-1).astype(F32) for t in small.values()])
    rows_small = -(-flat.size // (LANES * BF16_ROWS)) * BF16_ROWS
    flat = jnp.pad(flat, (0, rows_small * LANES - flat.size)).reshape(rows_small, LANES)
    flat_all = _gather8([flat], "gather_small")[0]
    flat_sum = _sum_slots(flat_all, "sum_small").reshape(-1)
    offs, o_ = {}, 0
    for n, sz in sizes.items():
        offs[n] = o_
        o_ += sz
    take = lambda n, shape: flat_sum[offs[n]:offs[n] + sizes[n]].reshape(shape)
    dmod_rows = flat_all.reshape(N_DEV, -1)[:, offs["dmod"]:offs["dmod"] + sizes["dmod"] + sizes["dmod_kv"]]
    dmod_rows = jnp.pad(dmod_rows, ((0, BF16_ROWS - N_DEV), (0, 0)))

    grads = {}
    grads["ada_b"] = take("dmod", (L, 6 * D))
    grads["kv_ada_b"] = take("dmod_kv", (2 * D,))
    for n in ["pre_mix_g", "post_mix_g", "pre_ffn_g", "post_ffn_g", "a_w_s", "a_b_s", "b_q_norm_g", "kv_norm_g", "kv_b_f", "k_norm_g"]:
        grads[n] = take(n, weights[n].shape)
    for n in ["a_b_in", "a_ln_g", "a_ln_b"]:
        full = take(n, small[n].shape)
        w = weights[n].shape[1]
        grads[n] = lax.dynamic_slice_in_dim(full, chip * w, w, axis=1)
    ada_g = []
    for l in range(L):
        cols = lax.dynamic_slice_in_dim(dmod_rows[:, l * 6 * D:(l + 1) * 6 * D], chip * ada_cols, ada_cols, axis=1)
        ada_g.append(_mm(c_act, cols, "tn", F32, f"mod_proj_dw_{l}"))
    grads["ada_w"] = jnp.stack(ada_g)
    cols = lax.dynamic_slice_in_dim(dmod_rows[:, L * 6 * D:], chip * kvada_cols, kvada_cols, axis=1)
    grads["kv_ada_w"] = _mm(c_act, cols, "tn", F32, "mod_proj_kv_dw")

    specs = {"ffn_w_gu": (2, 0), "ffn_w_down": (1, 0), "a_w_in": (2, 0), "a_w_out": (1, 0), "kv_w": (0, 1),
             "b_w_qg": (2, 0), "b_w_o": (1, 0)}
    full_g = {n: jnp.stack(gl[n]) for n in big if n != "kv_w"}
    full_g["kv_w"] = gkv["kv_w"].reshape(D, N_CHIPS, kv_cols).transpose(1, 0, 2)
    both = _sibling_scatter([full_g[n] for n in big], [specs[n] for n in big], "scatter_g_core")
    chip_sums = [_sum_slots(r, f"sum_g_core_{n}", BF16) for n, r in zip(big, both)]
    recv = _chip_scatter(chip_sums, "scatter_g_chip")
    halves = [_sum_slots(r, f"sum_g_{n}") for n, r in zip(big, recv)]
    pairs = _sibling_pair(halves, "pair_g")
    for n, p in zip(big, pairs):
        grads[n] = p.reshape(weights[n].shape)

    outs_d, outs_m, outs_v = {}, {}, {}
    for n in names:
        w2 = weights[n] if weights[n].ndim > 1 else weights[n].reshape(1, -1)
        shp = w2.shape
        d_, m_, v_ = _adamw(w2, grads[n].reshape(shp), m_in[n].reshape(shp), v_in[n].reshape(shp), f"adamw_{n}")
        outs_d[n], outs_m[n], outs_v[n] = (t.reshape(weights[n].shape) for t in (d_, m_, v_))
    return (loss, grad_x, *[grads[n] for n in names], *[outs_d[n] for n in names],
            *[outs_m[n] for n in names], *[outs_v[n] for n in names])
```

```python
import functools

import jax
import jax.numpy as jnp
from jax import lax
from jax.experimental import pallas as pl
from jax.experimental.pallas import tpu as pltpu

F32 = jnp.float32
BF16 = jnp.bfloat16
MESH = pl.DeviceIdType.MESH
NORM_EPS = 1e-6
MASKED = -1e30
LANES = 128
BF16_ROWS = 16
ROW_BLOCK_BYTES = 12 << 20
ADAM_LR, ADAM_B1, ADAM_B2, ADAM_EPS, ADAM_WD, ADAM_STEP = 0.001, 0.9, 0.999, 1e-08, 0.01, 10
N_CHIPS, N_CORES, N_DEV = 4, 2, 8
ATTN_HEADS_PER_STEP = 2
ATTN_FWD_HEADS_PER_STEP = 4
ANY = pl.BlockSpec(memory_space=pl.ANY)


def _tile(n, cap, quantum):
    best = None
    d = quantum
    while d <= min(n, cap):
        if n % d == 0:
            best = d
        d += quantum
    return n if best is None else best


def _call(body, *, name, out_shape, grid=(), in_specs=None, out_specs=None, scratch=(), sem=None, aliases=None):
    params = {} if sem is None else {"dimension_semantics": sem}
    return pl.pallas_call(
        body, name=name, grid=grid, in_specs=in_specs, out_specs=out_specs, out_shape=out_shape,
        scratch_shapes=list(scratch), input_output_aliases=aliases or {},
        compiler_params=pltpu.CompilerParams(**params))


def _call_prefetch(body, *, name, out_shape, grid, n_prefetch, in_specs, out_specs, scratch, sem):
    spec = pltpu.PrefetchScalarGridSpec(num_scalar_prefetch=n_prefetch, grid=grid, in_specs=in_specs,
                                        out_specs=out_specs, scratch_shapes=list(scratch))
    return pl.pallas_call(
        body, name=name, grid_spec=spec, out_shape=out_shape,
        compiler_params=pltpu.CompilerParams(dimension_semantics=sem))


def _place():
    x, y, c = lax.axis_index("x"), lax.axis_index("y"), lax.axis_index("c")
    return x, y, c


def _mm(a, b, mode, out_dtype, name):
    b_arr, b_idx = b if isinstance(b, tuple) else (b, None)
    bs = b_arr.shape[-2:]
    if mode == "nn":
        (M, K), (K2, N) = a.shape, bs
        dims = (((1,), (0,)), ((), ()))
    elif mode == "nt":
        (M, K), (N, K2) = a.shape, bs
        dims = (((1,), (1,)), ((), ()))
    else:
        (K, M), (K2, N) = a.shape, bs
        dims = (((0,), (0,)), ((), ()))
    assert K == K2, (name, a.shape, b_arr.shape)
    if mode == "tn":
        tm = _tile(M, 1408, LANES)
        tk = _tile(K, 2048, BF16_ROWS)
        tn = _tile(N, 512, LANES)
    else:
        tm = _tile(M, 1024, BF16_ROWS)
        tk = K if K <= 2816 else _tile(K, 2816, LANES)
        tn = _tile(N, 1408 if tk <= 1024 else 512, LANES)
    if tn < 256:
        tn = N
        tm = _tile(M, 512, LANES if mode == "tn" else BF16_ROWS)
    nk = K // tk
    grid = (M // tm, N // tn, nk)

    if mode == "tn":
        a_spec = pl.BlockSpec((tk, tm), lambda i, j, k: (k, i))
    else:
        a_spec = pl.BlockSpec((tm, tk), lambda i, j, k: (i, k))
    if mode == "nt":
        b_blk, b_map = (tn, tk), (lambda i, j, k: (j, k))
    else:
        b_blk, b_map = (tk, tn), (lambda i, j, k: (k, j))
    if b_idx is None:
        b_spec = pl.BlockSpec(b_blk, b_map)
    else:
        b_spec = pl.BlockSpec((None,) + b_blk, lambda i, j, k: (b_idx,) + b_map(i, j, k))

    def body(a_ref, b_ref, o_ref, *acc):
        r = lax.dot_general(a_ref[...].astype(BF16), b_ref[...].astype(BF16), dims, preferred_element_type=F32)
        if nk == 1:
            o_ref[...] = r.astype(o_ref.dtype)
        else:
            k = pl.program_id(2)

            @pl.when(k == 0)
            def _():
                acc[0][...] = r

            @pl.when(k > 0)
            def _():
                acc[0][...] += r

            @pl.when(k == nk - 1)
            def _():
                o_ref[...] = acc[0][...].astype(o_ref.dtype)

    return _call(
        body, name=name, grid=grid, in_specs=[a_spec, b_spec],
        out_specs=pl.BlockSpec((tm, tn), lambda i, j, k: (i, j)),
        out_shape=jax.ShapeDtypeStruct((M, N), out_dtype),
        scratch=[pltpu.VMEM((tm, tn), F32)] if nk > 1 else [],
        sem=("parallel", "parallel", "arbitrary"))(a, b_arr)


def _rowwise(fn, rows, pars, outs, pouts, name):
    R = rows[0].shape[0]
    row_bytes = 4 * (sum(max(r.shape[1], LANES) for r in rows) + sum(max(c, LANES) for c, _ in outs))
    tb = _tile(R, max(BF16_ROWS, ROW_BLOCK_BYTES // row_bytes), BF16_ROWS)
    nr, npar, no = len(rows), len(pars), len(outs)

    def body(*refs):
        r_in, p_in = refs[:nr], refs[nr:nr + npar]
        r_out, p_out = refs[nr + npar:nr + npar + no], refs[nr + npar + no:]
        ro, po = fn([r[...] for r in r_in], [p[...] for p in p_in])
        for ref, val in zip(r_out, ro):
            if isinstance(val, (tuple, list)):
                off = 0
                for piece in val:
                    w = piece.shape[1]
                    ref[:, off:off + w] = piece.astype(ref.dtype)
                    off += w
            else:
                ref[...] = val.astype(ref.dtype)
        if p_out:
            first = pl.program_id(0) == 0

            @pl.when(first)
            def _():
                for ref, val in zip(p_out, po):
                    ref[...] = val

            @pl.when(jnp.logical_not(first))
            def _():
                for ref, val in zip(p_out, po):
                    ref[...] += val

    res = _call(
        body, name=name, grid=(R // tb,),
        in_specs=[pl.BlockSpec((tb, r.shape[1]), lambda i: (i, 0)) for r in rows]
        + [pl.BlockSpec(p.shape, lambda i: (0, 0)) for p in pars],
        out_specs=[pl.BlockSpec((tb, c), lambda i: (i, 0)) for c, _ in outs]
        + [pl.BlockSpec(s, lambda i: (0, 0)) for s in pouts],
        out_shape=[jax.ShapeDtypeStruct((R, c), dt) for c, dt in outs]
        + [jax.ShapeDtypeStruct(s, F32) for s in pouts],
        sem=("arbitrary",) if pouts else ("parallel",))(*rows, *pars)
    return list(res)


def _rms(x, g):
    return x * lax.rsqrt(jnp.mean(x * x, axis=-1, keepdims=True) + NORM_EPS) * g


def _norm_mod(x, g, sh, sc):
    return _rms(x, g) * (1.0 + sc) + sh


def _gated_post(y, g, gate):
    return gate * _rms(y, g)


def _norm_mod_fwd(x, g, sh, sc, name):
    return _rowwise(lambda r, p: ([_norm_mod(r[0], *p)], []), [x], [g, sh, sc], [(x.shape[1], BF16)], [], name)[0]


def _norm_mod_bwd(dxo, dh, x, g, sh, sc, name):
    def fn(r, p):
        _, vjp = jax.vjp(_norm_mod, r[2], *p)
        dx, dg, dsh, dsc = vjp(r[1].astype(F32))
        return [r[0] + dx], [dg, dsh, dsc]
    c = x.shape[1]
    return _rowwise(fn, [dxo, dh, x], [g, sh, sc], [(c, F32)], [(1, c)] * 3, name)


def _post_fwd(x, y, g, gate, name):
    return _rowwise(lambda r, p: ([r[0] + _gated_post(r[1].astype(F32), *p)], []), [x, y], [g, gate],
                    [(x.shape[1], F32)], [], name)[0]


def _post_bwd(dxo, y, g, gate, name):
    def fn(r, p):
        _, vjp = jax.vjp(_gated_post, r[1].astype(F32), *p)
        dy, dg, dgate = vjp(r[0])
        return [dy], [dg, dgate]
    c = y.shape[1]
    return _rowwise(fn, [dxo, y], [g, gate], [(c, BF16)], [(1, c)] * 2, name)


def _swiglu(g, u):
    return jax.nn.silu(g) * u


def _swiglu_fwd(gu, name):
    f = gu.shape[1] // 2
    return _rowwise(lambda r, p: ([_swiglu(r[0][:, :f].astype(F32), r[0][:, f:].astype(F32))], []), [gu], [],
                    [(f, BF16)], [], name)[0]


def _swiglu_bwd(da, gu, name):
    f = gu.shape[1] // 2

    def fn(r, p):
        _, vjp = jax.vjp(_swiglu, r[1][:, :f].astype(F32), r[1][:, f:].astype(F32))
        return [vjp(r[0].astype(F32))], []
    return _rowwise(fn, [da, gu], [], [(2 * f, BF16)], [], name)[0]


def _silu_rows(c, name):
    return _rowwise(lambda r, p: ([jax.nn.silu(r[0])], []), [c], [], [(c.shape[1], F32)], [], name)[0]


def _head_norm(x, g, scale):
    return _rms(x, g) * scale


def _head_norm_fwd(x, g, scale, name):
    return _rowwise(lambda r, p: ([_head_norm(r[0].astype(F32), p[0], scale)], []), [x], [g],
                    [(x.shape[1], BF16)], [], name)[0]


def _head_norm_bwd(dy, x, g, scale, name):
    def fn(r, p):
        _, vjp = jax.vjp(lambda t, gg: _head_norm(t, gg, scale), r[1].astype(F32), p[0])
        dx, dg = vjp(r[0])
        return [dx], [dg]
    c = x.shape[1]
    return _rowwise(fn, [dy, x], [g], [(c, F32)], [(1, c)], name)


def _out_gate_fwd(o, qg, name):
    d = o.shape[1]
    return _rowwise(lambda r, p: ([r[0] * jax.nn.sigmoid(r[1][:, d:].astype(F32))], []), [o, qg], [],
                    [(d, BF16)], [], name)[0]


def _out_gate_bwd(dog, o, qg, name):
    d = o.shape[1]

    def fn(r, p):
        _, vjp = jax.vjp(lambda oo, gl: oo * jax.nn.sigmoid(gl), r[1], r[2][:, d:].astype(F32))
        do, dgl = vjp(r[0])
        return [do, dgl], []
    return _rowwise(fn, [dog, o, qg], [], [(d, BF16), (d, BF16)], [], name)


def _loss_bwd(y, tgt, name):
    n = y.shape[1]

    def fn(r, p):
        e = r[0] - r[1]
        part = jnp.sum(jnp.sum(e * e, axis=1, keepdims=True), axis=0, keepdims=True) * (0.5 / n)
        return [e * (1.0 / n)], [part]
    return _rowwise(fn, [y, tgt], [], [(n, F32)], [(1, 1)], name)


def _adamw(w, g, m, v, name):
    shape = w.shape
    c = shape[-1]
    flat = [t.reshape(-1, c) for t in (w, g, m, v)]

    def fn(r, p):
        w_, g_, m_, v_ = r
        m2 = ADAM_B1 * m_ + (1.0 - ADAM_B1) * g_
        v2 = ADAM_B2 * v_ + (1.0 - ADAM_B2) * (g_ * g_)
        m_hat = m2 / (1.0 - ADAM_B1 ** ADAM_STEP)
        v_hat = v2 / (1.0 - ADAM_B2 ** ADAM_STEP)
        delta = -ADAM_LR * (m_hat / (jnp.sqrt(v_hat) + ADAM_EPS) + ADAM_WD * w_)
        return [delta, m2, v2], []
    res = _rowwise(fn, flat, [], [(c, F32)] * 3, [], name)
    return [t.reshape(shape) for t in res]


def _sum_pair(a, b, name):
    c = a.shape[-1]
    out = _rowwise(lambda r, p: ([r[0].astype(F32) + r[1].astype(F32)], []), [a.reshape(-1, c), b.reshape(-1, c)], [],
                   [(c, BF16)], [], name)[0]
    return out.reshape(a.shape)


def _sum_slots(recv, name, out_dtype=F32):
    n = recv.shape[0]
    shape = recv.shape[1:]
    c = shape[-1]
    r3 = recv.reshape(n, -1, c)
    rows = r3.shape[1]
    tb = _tile(rows, max(BF16_ROWS, ROW_BLOCK_BYTES // (4 * c * (n + 1))), BF16_ROWS)

    def body(r_ref, o_ref):
        acc = r_ref[0].astype(F32)
        for s in range(1, n):
            acc = acc + r_ref[s].astype(F32)
        o_ref[...] = acc.astype(o_ref.dtype)

    out = _call(body, name=name, grid=(rows // tb,),
                in_specs=[pl.BlockSpec((n, tb, c), lambda i: (0, i, 0))],
                out_specs=pl.BlockSpec((tb, c), lambda i: (i, 0)),
                out_shape=jax.ShapeDtypeStruct((rows, c), out_dtype), sem=("parallel",))(r3)
    return out.reshape(shape)


def _gmlp_pre(zu, zv, b_u, b_v, ln_g, ln_b):
    u = jax.nn.gelu(zu + b_u, approximate=True)
    v = jax.nn.gelu(zv + b_v, approximate=True)
    xc = v - jnp.mean(v, axis=-1, keepdims=True)
    vn = xc * lax.rsqrt(jnp.mean(xc * xc, axis=-1, keepdims=True) + NORM_EPS) * ln_g + ln_b
    return u, vn


def _gmlp_fwd(zp, b_in, ln_g, ln_b, ws, bs_t, name):
    S, gw2 = zp.shape
    gw = gw2 // 2
    G, ch, _ = ws.shape
    gd = gw // G
    tb = 2 * ch

    def body(zp_ref, bin_ref, lg_ref, lb_ref, ws_ref, bs_ref, o_ref):
        u, vn = _gmlp_pre(zp_ref[:, :gw].astype(F32), zp_ref[:, gw:].astype(F32), bin_ref[:, :gw], bin_ref[:, gw:],
                          lg_ref[...], lb_ref[...])
        vnb = vn.astype(BF16)
        for c in range(tb // ch):
            for g in range(G):
                rs, cs = slice(c * ch, (c + 1) * ch), slice(g * gd, (g + 1) * gd)
                vv = jnp.dot(ws_ref[g], vnb[rs, cs], preferred_element_type=F32) + bs_ref[:, g:g + 1]
                o_ref[rs, cs] = (u[rs, cs] * vv).astype(o_ref.dtype)

    full = lambda a: pl.BlockSpec(a.shape, lambda i: (0,) * a.ndim)
    return _call(body, name=name, grid=(S // tb,),
                 in_specs=[pl.BlockSpec((tb, gw2), lambda i: (i, 0)), full(b_in), full(ln_g), full(ln_b), full(ws), full(bs_t)],
                 out_specs=pl.BlockSpec((tb, gw), lambda i: (i, 0)),
                 out_shape=jax.ShapeDtypeStruct((S, gw), BF16), sem=("parallel",))(zp, b_in, ln_g, ln_b, ws, bs_t)


def _gmlp_bwd(dyg, zp, b_in, ln_g, ln_b, ws, ws_t, bs_t, name):
    S, gw2 = zp.shape
    gw = gw2 // 2
    G, ch, _ = ws.shape
    gd = gw // G
    tb = 2 * ch

    def body(dy_ref, zp_ref, bin_ref, lg_ref, lb_ref, ws_ref, wst_ref, bs_ref,
             dzp_ref, dbin_ref, dlg_ref, dlb_ref, dws_ref, dbs_ref, du_sc, dvn_sc):
        (u, vn), vjp = jax.vjp(_gmlp_pre, zp_ref[:, :gw].astype(F32), zp_ref[:, gw:].astype(F32), bin_ref[:, :gw],
                               bin_ref[:, gw:], lg_ref[...], lb_ref[...])
        vnb = vn.astype(BF16)
        first = pl.program_id(0) == 0

        @pl.when(first)
        def _():
            dws_ref[...] = jnp.zeros_like(dws_ref)

        lane = lax.broadcasted_iota(jnp.int32, (ch, G), 1)
        dbs = jnp.zeros((ch, G), F32)
        for g in range(G):
            cs = slice(g * gd, (g + 1) * gd)
            dws_g = jnp.zeros((ch, ch), F32)
            col = jnp.zeros((ch, 1), F32)
            for c in range(tb // ch):
                rs = slice(c * ch, (c + 1) * ch)
                vnp = vnb[rs, cs]
                vv = jnp.dot(ws_ref[g], vnp, preferred_element_type=F32) + bs_ref[:, g:g + 1]
                dy = dy_ref[rs, cs].astype(F32)
                du_sc[rs, cs] = dy * vv
                dvv = dy * u[rs, cs]
                dvvb = dvv.astype(BF16)
                dvn_sc[rs, cs] = jnp.dot(wst_ref[g], dvvb, preferred_element_type=F32)
                dws_g = dws_g + lax.dot_general(dvvb, vnp, (((1,), (1,)), ((), ())), preferred_element_type=F32)
                col = col + jnp.sum(dvv, axis=1, keepdims=True)
            dws_ref[g] += dws_g
            dbs = jnp.where(lane == g, col, dbs)
        dzu, dzv, dbu, dbv, dlg, dlb = vjp((du_sc[...], dvn_sc[...]))
        dzp_ref[:, :gw] = dzu.astype(dzp_ref.dtype)
        dzp_ref[:, gw:] = dzv.astype(dzp_ref.dtype)

        @pl.when(first)
        def _():
            dbin_ref[:, :gw] = dbu
            dbin_ref[:, gw:] = dbv
            dlg_ref[...] = dlg
            dlb_ref[...] = dlb
            dbs_ref[...] = dbs

        @pl.when(jnp.logical_not(first))
        def _():
            dbin_ref[:, :gw] += dbu
            dbin_ref[:, gw:] += dbv
            dlg_ref[...] += dlg
            dlb_ref[...] += dlb
            dbs_ref[...] += dbs

    full = lambda a: pl.BlockSpec(a.shape, lambda i: (0,) * a.ndim)
    fshape = lambda s: pl.BlockSpec(s, lambda i: (0,) * len(s))
    return _call(
        body, name=name, grid=(S // tb,),
        in_specs=[pl.BlockSpec((tb, gw), lambda i: (i, 0)), pl.BlockSpec((tb, gw2), lambda i: (i, 0)),
                  full(b_in), full(ln_g), full(ln_b), full(ws), full(ws_t), full(bs_t)],
        out_specs=[pl.BlockSpec((tb, gw2), lambda i: (i, 0)), fshape((1, gw2)), fshape((1, gw)), fshape((1, gw)),
                   fshape((G, ch, ch)), fshape((ch, G))],
        out_shape=[jax.ShapeDtypeStruct((S, gw2), BF16), jax.ShapeDtypeStruct((1, gw2), F32),
                   jax.ShapeDtypeStruct((1, gw), F32), jax.ShapeDtypeStruct((1, gw), F32),
                   jax.ShapeDtypeStruct((G, ch, ch), F32), jax.ShapeDtypeStruct((ch, G), F32)],
        scratch=[pltpu.VMEM((tb, gw), F32), pltpu.VMEM((tb, gw), F32)],
        sem=("arbitrary",))(dyg, zp, b_in, ln_g, ln_b, ws, ws_t, bs_t)


def _dot_01(x, ones_bf16):
    hi = x.astype(BF16)
    r1 = x - hi.astype(F32)
    mid = r1.astype(BF16)
    lo = (r1 - mid.astype(F32)).astype(BF16)
    dot = lambda t: jnp.dot(t, ones_bf16, preferred_element_type=F32)
    return dot(hi) + dot(mid) + dot(lo)


def _log_sigmoid(x):
    return jnp.minimum(x, 0.0) - jnp.log1p(jnp.exp(-jnp.abs(x)))


def _dcum_fwd(f_t, b_col, name):
    H, S = f_t.shape
    tb = _tile(S, 512, LANES)

    def body(f_ref, b_ref, o_ref, carry):
        @pl.when(pl.program_id(0) == 0)
        def _():
            carry[...] = jnp.zeros_like(carry)

        ls = _log_sigmoid(f_ref[...] + b_ref[...])
        r = lax.broadcasted_iota(jnp.int32, (tb, tb), 0)
        c = lax.broadcasted_iota(jnp.int32, (tb, tb), 1)
        upper = (r <= c).astype(BF16)
        o_ref[...] = _dot_01(ls, upper) + carry[...]
        carry[...] += jnp.sum(ls, axis=1, keepdims=True)

    return _call(body, name=name, grid=(S // tb,),
                 in_specs=[pl.BlockSpec((H, tb), lambda i: (0, i)), pl.BlockSpec((H, 1), lambda i: (0, 0))],
                 out_specs=pl.BlockSpec((H, tb), lambda i: (0, i)),
                 out_shape=jax.ShapeDtypeStruct((H, S), F32),
                 scratch=[pltpu.VMEM((H, 1), F32)], sem=("arbitrary",))(f_t, b_col)


def _dcum_bwd(dd_t, f_t, b_col, name):
    H, S = f_t.shape
    tb = _tile(S, 512, LANES)
    nb = S // tb

    def body(dd_ref, f_ref, b_ref, df_ref, db_ref, carry):
        first = pl.program_id(0) == 0

        @pl.when(first)
        def _():
            carry[...] = jnp.zeros_like(carry)

        dd = dd_ref[...]
        r = lax.broadcasted_iota(jnp.int32, (tb, tb), 0)
        c = lax.broadcasted_iota(jnp.int32, (tb, tb), 1)
        lower = (r >= c).astype(BF16)
        rev = _dot_01(dd, lower) + carry[...]
        carry[...] += jnp.sum(dd, axis=1, keepdims=True)
        df = rev * jax.nn.sigmoid(-(f_ref[...] + b_ref[...]))
        df_ref[...] = df
        part = jnp.sum(df, axis=1, keepdims=True)

        @pl.when(first)
        def _():
            db_ref[...] = part

        @pl.when(jnp.logical_not(first))
        def _():
            db_ref[...] += part

    return _call(body, name=name, grid=(nb,),
                 in_specs=[pl.BlockSpec((H, tb), lambda i: (0, nb - 1 - i)), pl.BlockSpec((H, tb), lambda i: (0, nb - 1 - i)),
                           pl.BlockSpec((H, 1), lambda i: (0, 0))],
                 out_specs=[pl.BlockSpec((H, tb), lambda i: (0, nb - 1 - i)), pl.BlockSpec((H, 1), lambda i: (0, 0))],
                 out_shape=[jax.ShapeDtypeStruct((H, S), F32), jax.ShapeDtypeStruct((H, 1), F32)],
                 scratch=[pltpu.VMEM((H, 1), F32)], sem=("arbitrary",))(dd_t, f_t, b_col)


def _attn_tile(S):
    return _tile(S, 512, LANES)


def _causal(t, transposed):
    r = lax.broadcasted_iota(jnp.int32, (t, t), 0)
    c = lax.broadcasted_iota(jnp.int32, (t, t), 1)
    return (r <= c) if transposed else (c <= r)


def _tri_pairs(n, key_major):
    if key_major:
        pairs = [(i, j) for j in range(n) for i in range(j, n)]
    else:
        pairs = [(i, j) for i in range(n) for j in range(i + 1)]
    return jnp.asarray([p[0] for p in pairs], jnp.int32), jnp.asarray([p[1] for p in pairs], jnp.int32)


def _split3(x):
    hi = lax.reduce_precision(x, 8, 7)
    r = x - hi
    mid = lax.reduce_precision(r, 8, 7)
    lo = lax.reduce_precision(r - mid, 8, 7)
    return hi.astype(BF16), mid.astype(BF16), lo.astype(BF16)


def _augment(xn, dcum, query):
    H, S, hd = xn.shape
    parts = list(_split3(dcum))
    vals = parts + [1.0] * 3 if query else [1.0] * 3 + [-p for p in parts]
    lane = lax.broadcasted_iota(jnp.int32, (1, 1, LANES), 2)
    out = jnp.pad(xn, ((0, 0), (0, 0), (0, LANES - hd)))
    for k, val in enumerate(vals):
        val = jnp.asarray(val, BF16)
        out = jnp.where(lane == hd + k, val[..., None] if val.ndim else val, out)
    return out


def _scores_t(k_ref, qt_ref, h, t, diag):
    st = jnp.dot(k_ref[h], qt_ref[h], preferred_element_type=F32)
    return jnp.where(_causal(t, True), st, MASKED) if diag else st


def _flash_fwd(ka, qat, vat, hd, name):
    H, S, da = ka.shape
    t = _attn_tile(S)
    hb = ATTN_FWD_HEADS_PER_STEP
    it, jt = _tri_pairs(S // t, False)

    def body(it_ref, jt_ref, k_ref, qt_ref, vt_ref, o_ref, lse_ref, m_sc, acc_sc):
        i, j = it_ref[pl.program_id(1)], jt_ref[pl.program_id(1)]

        @pl.when(j == 0)
        def _():
            m_sc[...] = jnp.full_like(m_sc, MASKED)
            acc_sc[...] = jnp.zeros_like(acc_sc)

        def step(diag):
            for h in range(hb):
                st = _scores_t(k_ref, qt_ref, h, t, diag)
                m_prev = m_sc[h]
                m_new = jnp.maximum(m_prev, jnp.max(st, axis=0, keepdims=True))
                pt = jnp.exp(st - m_new).astype(BF16)
                acc_sc[h] = jnp.exp(m_prev - m_new) * acc_sc[h] + jnp.dot(vt_ref[h], pt, preferred_element_type=F32)
                m_sc[h] = m_new

        @pl.when(j < i)
        def _():
            step(False)

        @pl.when(j == i)
        def _():
            step(True)
            for h in range(hb):
                l = acc_sc[h, hd:hd + 1, :]
                o_ref[h] = acc_sc[h, :hd, :] / l
                lse_ref[h] = m_sc[h] + jnp.log(l)

    qcol = lambda h, p, it_, jt_: (h, 0, it_[p])
    kcol = lambda h, p, it_, jt_: (h, 0, jt_[p])
    krow = lambda h, p, it_, jt_: (h, jt_[p], 0)
    return _call_prefetch(
        body, name=name, grid=(H // hb, it.shape[0]), n_prefetch=2,
        in_specs=[pl.BlockSpec((hb, t, da), krow), pl.BlockSpec((hb, da, t), qcol), pl.BlockSpec((hb, da, t), kcol)],
        out_specs=[pl.BlockSpec((hb, hd, t), qcol), pl.BlockSpec((hb, 1, t), qcol)],
        out_shape=[jax.ShapeDtypeStruct((H, hd, S), F32), jax.ShapeDtypeStruct((H, 1, S), F32)],
        scratch=[pltpu.VMEM((hb, 1, t), F32), pltpu.VMEM((hb, da, t), F32)],
        sem=("parallel", "arbitrary"))(it, jt, ka, qat, vat)


def _ds_t(k_ref, qt_ref, v_ref, dot_ref, lse_ref, dl, h, t, diag):
    pt = jnp.exp(_scores_t(k_ref, qt_ref, h, t, diag) - lse_ref[h])
    dpt = jnp.dot(v_ref[h], dot_ref[h], preferred_element_type=F32)
    return pt, pt * (dpt - dl)


def _flash_bwd(ka, kat, qa, qat, v, do, dot, o_tr, lse_r, name):
    H, S, hd = v.shape
    da = ka.shape[2]
    t = _attn_tile(S)
    n = S // t
    hb = ATTN_HEADS_PER_STEP
    it, jt = _tri_pairs(n, True)

    def body(it_ref, jt_ref, k_ref, kt_ref, q_ref, qt_ref, v_ref, do_ref, dot_ref, o_ref, lse_ref,
             dq_ref, dk_ref, dv_ref, dd_ref, ddq_ref, dk_sc, dv_sc, dd_sc):
        i, j = it_ref[pl.program_id(1)], jt_ref[pl.program_id(1)]

        @pl.when(pl.program_id(1) == 0)
        def _():
            ddq_ref[...] = jnp.zeros_like(ddq_ref)
            dq_ref[...] = jnp.zeros_like(dq_ref)

        def step(diag):
            for h in range(hb):
                dl = jnp.sum(dot_ref[h].astype(F32) * o_ref[h], axis=0, keepdims=True)
                pt, dst = _ds_t(k_ref, qt_ref, v_ref, dot_ref, lse_ref, dl, h, t, diag)
                dsb = dst.astype(BF16)
                dv_sc[h] += jnp.dot(pt.astype(BF16), do_ref[h], preferred_element_type=F32)
                dk_sc[h] += jnp.dot(dsb, q_ref[h], preferred_element_type=F32)
                dq_ref[h, i] += jnp.dot(kt_ref[h], dsb, preferred_element_type=F32)
                part = dst[:, :LANES]
                for c in range(1, t // LANES):
                    part = part + dst[:, c * LANES:(c + 1) * LANES]
                dd_sc[h] += part
                ddq_ref[h, i] += jnp.sum(dst, axis=0, keepdims=True)

        @pl.when(i == j)
        def _():
            dk_sc[...] = jnp.zeros_like(dk_sc)
            dv_sc[...] = jnp.zeros_like(dv_sc)
            dd_sc[...] = jnp.zeros_like(dd_sc)
            step(True)

        @pl.when(i > j)
        def _():
            step(False)

        @pl.when(i == n - 1)
        def _():
            dk_ref[...] = dk_sc[...]
            dv_ref[...] = dv_sc[...]
            for h in range(hb):
                dd_ref[h] = -jnp.sum(dd_sc[h], axis=1, keepdims=True)

    krow = lambda h, p, it_, jt_: (h, jt_[p], 0)
    kcol = lambda h, p, it_, jt_: (h, 0, jt_[p])
    qrow = lambda h, p, it_, jt_: (h, it_[p], 0)
    qcol = lambda h, p, it_, jt_: (h, 0, it_[p])
    whole = lambda h, p, it_, jt_: (h, 0, 0, 0)
    return _call_prefetch(
        body, name=name, grid=(H // hb, it.shape[0]), n_prefetch=2,
        in_specs=[pl.BlockSpec((hb, t, da), krow), pl.BlockSpec((hb, da, t), kcol), pl.BlockSpec((hb, t, da), qrow),
                  pl.BlockSpec((hb, da, t), qcol), pl.BlockSpec((hb, t, hd), krow), pl.BlockSpec((hb, t, hd), qrow),
                  pl.BlockSpec((hb, hd, t), qcol), pl.BlockSpec((hb, hd, t), qcol), pl.BlockSpec((hb, 1, t), qcol)],
        out_specs=[pl.BlockSpec((hb, n, da, t), whole), pl.BlockSpec((hb, t, da), krow), pl.BlockSpec((hb, t, hd), krow),
                   pl.BlockSpec((hb, t, 1), krow), pl.BlockSpec((hb, n, 1, t), whole)],
        out_shape=[jax.ShapeDtypeStruct((H, n, da, t), F32), jax.ShapeDtypeStruct((H, S, da), F32),
                   jax.ShapeDtypeStruct((H, S, hd), F32), jax.ShapeDtypeStruct((H, S, 1), F32),
                   jax.ShapeDtypeStruct((H, n, 1, t), F32)],
        scratch=[pltpu.VMEM((hb, t, da), F32), pltpu.VMEM((hb, t, hd), F32), pltpu.VMEM((hb, t, LANES), F32)],
        sem=("parallel", "arbitrary"))(it, jt, ka, kat, qa, qat, v, do, dot, o_tr, lse_r)


def _offsets(n_bits):
    return [tuple((k >> b) & 1 for b in reversed(range(n_bits))) for k in range(1, 1 << n_bits)]


def _own_slot(out, own, idx):
    return lax.dynamic_update_index_in_dim(out, own.astype(out.dtype), idx, 0)


def _gather8(arrs, name):
    n = len(arrs)
    offs = _offsets(3)

    def body(*refs):
        ins, outs = refs[:n], refs[n:2 * n]
        ssem, rsem = refs[2 * n:]
        x, y, c = _place()
        me = 4 * x + 2 * y + c
        copies = []
        for a in range(n):
            for k, (dx, dy, dcc) in enumerate(offs):
                cp = pltpu.make_async_remote_copy(
                    src_ref=ins[a], dst_ref=outs[a].at[me], send_sem=ssem.at[a, k], recv_sem=rsem.at[a, k],
                    device_id=((x + dx) % 2, (y + dy) % 2, (c + dcc) % 2), device_id_type=MESH)
                cp.start()
                copies.append(cp)
        for cp in copies:
            cp.wait()

    return _call(body, name=name, in_specs=[ANY] * n, out_specs=[ANY] * n,
                 out_shape=[jax.ShapeDtypeStruct((N_DEV,) + a.shape, a.dtype) for a in arrs],
                 scratch=[pltpu.SemaphoreType.DMA((n, 7)), pltpu.SemaphoreType.DMA((n, 7))])(*arrs)


def _chip_gather(arrs, halved, name):
    n = len(arrs)
    offs = _offsets(2)

    def body(*refs):
        ins, outs = refs[:n], refs[n:2 * n]
        ssem, rsem = refs[2 * n:]
        x, y, c = _place()
        chip = 2 * x + y
        copies = []
        for a in range(n):
            if halved:
                hn = arrs[a].shape[0] // 2
                src = ins[a].at[pl.ds(c * hn, hn)]
                dst = outs[a].at[chip, pl.ds(c * hn, hn)]
            else:
                src, dst = ins[a], outs[a].at[chip]
            for k, (dx, dy) in enumerate(offs):
                cp = pltpu.make_async_remote_copy(
                    src_ref=src, dst_ref=dst, send_sem=ssem.at[a, k], recv_sem=rsem.at[a, k],
                    device_id=((x + dx) % 2, (y + dy) % 2, c), device_id_type=MESH)
                cp.start()
                copies.append(cp)
        for cp in copies:
            cp.wait()

    return _call(body, name=name, in_specs=[ANY] * n, out_specs=[ANY] * n,
                 out_shape=[jax.ShapeDtypeStruct((N_CHIPS,) + a.shape, a.dtype) for a in arrs],
                 scratch=[pltpu.SemaphoreType.DMA((n, 3)), pltpu.SemaphoreType.DMA((n, 3))])(*arrs)


def _sibling_fill(bufs, name):
    n = len(bufs)
    offs = _offsets(2)

    def body(*refs):
        ins, outs = refs[:n], refs[n:2 * n]
        ssem, rsem = refs[2 * n:]
        x, y, c = _place()
        copies = []
        for a in range(n):
            hn = bufs[a].shape[1] // 2
            for k, (dx, dy) in enumerate(offs):
                chip = 2 * ((x + dx) % 2) + (y + dy) % 2
                cp = pltpu.make_async_remote_copy(
                    src_ref=ins[a].at[chip, pl.ds(c * hn, hn)], dst_ref=outs[a].at[chip, pl.ds(c * hn, hn)],
                    send_sem=ssem.at[a, k], recv_sem=rsem.at[a, k],
                    device_id=(x, y, 1 - c), device_id_type=MESH)
                cp.start()
                copies.append(cp)
        for cp in copies:
            cp.wait()

    return _call(body, name=name, in_specs=[ANY] * n, out_specs=[ANY] * n,
                 out_shape=[jax.ShapeDtypeStruct(b.shape, b.dtype) for b in bufs],
                 scratch=[pltpu.SemaphoreType.DMA((n, 3)), pltpu.SemaphoreType.DMA((n, 3))],
                 aliases={a: a for a in range(n)})(*bufs)


def _sibling_pair(arrs, name):
    n = len(arrs)

    def body(*refs):
        ins, outs = refs[:n], refs[n:2 * n]
        ssem, rsem = refs[2 * n:]
        x, y, c = _place()
        copies = []
        for a in range(n):
            cp = pltpu.make_async_remote_copy(
                src_ref=ins[a], dst_ref=outs[a].at[c], send_sem=ssem.at[a], recv_sem=rsem.at[a],
                device_id=(x, y, 1 - c), device_id_type=MESH)
            cp.start()
            copies.append(cp)
        for cp in copies:
            cp.wait()

    return _call(body, name=name, in_specs=[ANY] * n, out_specs=[ANY] * n,
                 out_shape=[jax.ShapeDtypeStruct((N_CORES,) + a.shape, a.dtype) for a in arrs],
                 scratch=[pltpu.SemaphoreType.DMA((n,)), pltpu.SemaphoreType.DMA((n,))])(*arrs)


def _piece(shape, spec, j, h):
    shard_ax, half_ax = spec
    w = shape[shard_ax] // N_CHIPS
    idx = [slice(None)] * len(shape)
    idx[shard_ax] = pl.ds(j * w, w)
    hn = (w if half_ax == shard_ax else shape[half_ax]) // 2
    assert half_ax != shard_ax
    idx[half_ax] = pl.ds(h * hn, hn)
    return tuple(idx)


def _piece_shape(shape, spec):
    shard_ax, half_ax = spec
    s = list(shape)
    s[shard_ax] //= N_CHIPS
    s[half_ax] //= 2
    return tuple(s)


def _own_pieces(g, spec, c):
    shard_ax, half_ax = spec
    hn = g.shape[half_ax] // 2
    half = lax.dynamic_slice_in_dim(g, c * hn, hn, axis=half_ax)
    shape = list(half.shape)
    shape[shard_ax:shard_ax + 1] = [N_CHIPS, shape[shard_ax] // N_CHIPS]
    return jnp.moveaxis(half.reshape(shape), shard_ax, 0)


def _sibling_scatter(arrs, specs, name):
    n = len(arrs)

    def body(*refs):
        ins, outs = refs[:n], refs[n:2 * n]
        ssem, rsem = refs[2 * n:]
        x, y, c = _place()
        for mine in range(N_CORES):
            @pl.when(c == mine)
            def _():
                copies = []
                for a in range(n):
                    for j in range(N_CHIPS):
                        cp = pltpu.make_async_remote_copy(
                            src_ref=ins[a].at[_piece(arrs[a].shape, specs[a], j, 1 - mine)], dst_ref=outs[a].at[j],
                            send_sem=ssem.at[a, j], recv_sem=rsem.at[a, j],
                            device_id=(x, y, 1 - mine), device_id_type=MESH)
                        cp.start()
                        copies.append(cp)
                for cp in copies:
                    cp.wait()

    return _call(body, name=name, in_specs=[ANY] * n, out_specs=[ANY] * n,
                 out_shape=[jax.ShapeDtypeStruct((N_CHIPS,) + _piece_shape(a.shape, s), a.dtype)
                            for a, s in zip(arrs, specs)],
                 scratch=[pltpu.SemaphoreType.DMA((n, N_CHIPS))] * 2)(*arrs)


def _chip_scatter(arrs, name):
    n = len(arrs)
    offs = _offsets(2)

    def body(*refs):
        ins, outs = refs[:n], refs[n:2 * n]
        ssem, rsem = refs[2 * n:]
        x, y, c = _place()
        chip = 2 * x + y
        copies = []
        for a in range(n):
            for k, (dx, dy) in enumerate(offs):
                tx, ty = (x + dx) % 2, (y + dy) % 2
                cp = pltpu.make_async_remote_copy(
                    src_ref=ins[a].at[2 * tx + ty], dst_ref=outs[a].at[chip], send_sem=ssem.at[a, k], recv_sem=rsem.at[a, k],
                    device_id=(tx, ty, c), device_id_type=MESH)
                cp.start()
                copies.append(cp)
        for cp in copies:
            cp.wait()

    return _call(body, name=name, in_specs=[ANY] * n, out_specs=[ANY] * n,
                 out_shape=[jax.ShapeDtypeStruct(a.shape, a.dtype) for a in arrs],
                 scratch=[pltpu.SemaphoreType.DMA((n, 3)), pltpu.SemaphoreType.DMA((n, 3))])(*arrs)


def kernel(x, c, ada_w, ada_b, pre_mix_g, post_mix_g, pre_ffn_g, post_ffn_g, ffn_w_gu, ffn_w_down, a_w_in, a_b_in, a_ln_g, a_ln_b, a_w_s, a_b_s, a_w_out, kv_ada_w, kv_ada_b, kv_norm_g, kv_w, kv_b_f, k_norm_g, b_w_qg, b_q_norm_g, b_w_o, loss_target, m_ada_w, m_ada_b, m_pre_mix_g, m_post_mix_g, m_pre_ffn_g, m_post_ffn_g, m_ffn_w_gu, m_ffn_w_down, m_a_w_in, m_a_b_in, m_a_ln_g, m_a_ln_b, m_a_w_s, m_a_b_s, m_a_w_out, m_kv_ada_w, m_kv_ada_b, m_kv_norm_g, m_kv_w, m_kv_b_f, m_k_norm_g, m_b_w_qg, m_b_q_norm_g, m_b_w_o, v_ada_w, v_ada_b, v_pre_mix_g, v_post_mix_g, v_pre_ffn_g, v_post_ffn_g, v_ffn_w_gu, v_ffn_w_down, v_a_w_in, v_a_b_in, v_a_ln_g, v_a_ln_b, v_a_w_s, v_a_b_s, v_a_w_out, v_kv_ada_w, v_kv_ada_b, v_kv_norm_g, v_kv_w, v_kv_b_f, v_k_norm_g, v_b_w_qg, v_b_q_norm_g, v_b_w_o):
    weights = dict(ada_w=ada_w, ada_b=ada_b, pre_mix_g=pre_mix_g, post_mix_g=post_mix_g, pre_ffn_g=pre_ffn_g,
                   post_ffn_g=post_ffn_g, ffn_w_gu=ffn_w_gu, ffn_w_down=ffn_w_down, a_w_in=a_w_in, a_b_in=a_b_in,
                   a_ln_g=a_ln_g, a_ln_b=a_ln_b, a_w_s=a_w_s, a_b_s=a_b_s, a_w_out=a_w_out, kv_ada_w=kv_ada_w,
                   kv_ada_b=kv_ada_b, kv_norm_g=kv_norm_g, kv_w=kv_w, kv_b_f=kv_b_f, k_norm_g=k_norm_g, b_w_qg=b_w_qg,
                   b_q_norm_g=b_q_norm_g, b_w_o=b_w_o)
    m_in = dict(ada_w=m_ada_w, ada_b=m_ada_b, pre_mix_g=m_pre_mix_g, post_mix_g=m_post_mix_g, pre_ffn_g=m_pre_ffn_g,
                post_ffn_g=m_post_ffn_g, ffn_w_gu=m_ffn_w_gu, ffn_w_down=m_ffn_w_down, a_w_in=m_a_w_in, a_b_in=m_a_b_in,
                a_ln_g=m_a_ln_g, a_ln_b=m_a_ln_b, a_w_s=m_a_w_s, a_b_s=m_a_b_s, a_w_out=m_a_w_out, kv_ada_w=m_kv_ada_w,
                kv_ada_b=m_kv_ada_b, kv_norm_g=m_kv_norm_g, kv_w=m_kv_w, kv_b_f=m_kv_b_f, k_norm_g=m_k_norm_g,
                b_w_qg=m_b_w_qg, b_q_norm_g=m_b_q_norm_g, b_w_o=m_b_w_o)
    v_in = dict(ada_w=v_ada_w, ada_b=v_ada_b, pre_mix_g=v_pre_mix_g, post_mix_g=v_post_mix_g, pre_ffn_g=v_pre_ffn_g,
                post_ffn_g=v_post_ffn_g, ffn_w_gu=v_ffn_w_gu, ffn_w_down=v_ffn_w_down, a_w_in=v_a_w_in, a_b_in=v_a_b_in,
                a_ln_g=v_a_ln_g, a_ln_b=v_a_ln_b, a_w_s=v_a_w_s, a_b_s=v_a_b_s, a_w_out=v_a_w_out, kv_ada_w=v_kv_ada_w,
                kv_ada_b=v_kv_ada_b, kv_norm_g=v_kv_norm_g, kv_w=v_kv_w, kv_b_f=v_kv_b_f, k_norm_g=v_k_norm_g,
                b_w_qg=v_b_w_qg, b_q_norm_g=v_b_q_norm_g, b_w_o=v_b_w_o)
    names = list(weights)

    S, D = x.shape[1], x.shape[2]
    L, NA, NB = ada_w.shape[0], a_w_in.shape[0], b_w_qg.shape[0]
    H = kv_b_f.shape[0]
    hd = D // H
    G, CH = a_w_s.shape[1], a_w_s.shape[2]
    GW = a_w_out.shape[1] * N_CHIPS
    F = ffn_w_down.shape[1] * N_CHIPS
    ada_cols = ada_w.shape[2]
    kvada_cols = kv_ada_w.shape[1]
    kv_cols = kv_w.shape[1]
    kv_pad = -(-(2 * D + H) // LANES) * LANES
    xi, yi, ci = _place()
    chip = 2 * xi + yi
    me = 2 * chip + ci
    x0 = x[0]
    tgt = loss_target[0]
    row = lambda t: t.reshape(1, -1)

    c_all = _own_slot(_gather8([c], "gather_c")[0], c, me).reshape(N_DEV, D)
    c_act = _silu_rows(jnp.pad(c_all, ((0, BF16_ROWS - N_DEV), (0, 0))), "silu_c")
    mod_sh = [_mm(c_act, (ada_w, l), "nn", F32, f"mod_proj_{l}") for l in range(L)]
    mod_sh.append(_mm(c_act, kv_ada_w, "nn", F32, "mod_proj_kv"))
    mod_sh = jnp.concatenate(mod_sh, axis=1)
    small_sh = [mod_sh, a_b_in, a_ln_g, a_ln_b]
    mod_all, b_in_all, ln_g_all, ln_b_all = [
        _own_slot(o, s, chip) for o, s in zip(_chip_gather(small_sh, False, "gather_mod"), small_sh)]
    mine = lax.dynamic_index_in_dim(mod_all, me, axis=1, keepdims=False)
    mod = [jnp.concatenate([mine[j, l * ada_cols:(l + 1) * ada_cols] for j in range(N_CHIPS)]) + ada_b[l] for l in range(L)]
    mod = [[row(t) for t in jnp.split(m_, 6)] for m_ in mod]
    mod_kv = jnp.concatenate([mine[j, L * ada_cols:] for j in range(N_CHIPS)]) + kv_ada_b
    kv_sh, kv_sc = [row(t) for t in jnp.split(mod_kv, 2)]
    cat_chips = lambda t, ax: jnp.concatenate([t[j] for j in range(N_CHIPS)], axis=ax)
    b_in_f = cat_chips(b_in_all, 1)
    ln_g_f, ln_b_f = cat_chips(ln_g_all, 1), cat_chips(ln_b_all, 1)

    big = ["ffn_w_gu", "ffn_w_down", "a_w_in", "a_w_out", "kv_w", "b_w_qg", "b_w_o"]
    own_w = [weights[n].astype(BF16) for n in big]
    gathered = _sibling_fill(_chip_gather(own_w, True, "gather_w"), "fill_w")
    gathered = {n: _own_slot(g, w, chip) for n, g, w in zip(big, gathered, own_w)}
    w_gu = cat_chips(gathered["ffn_w_gu"], 2)
    w_dn = cat_chips(gathered["ffn_w_down"], 1)
    w_in = cat_chips(gathered["a_w_in"], 2)
    w_out = cat_chips(gathered["a_w_out"], 1)
    w_kv = jnp.pad(cat_chips(gathered["kv_w"], 1), ((0, 0), (0, kv_pad - (2 * D + H))))
    w_qg = cat_chips(gathered["b_w_qg"], 2)
    w_o = cat_chips(gathered["b_w_o"], 1)

    causal = jnp.tril(jnp.ones((CH, CH), F32))
    ws_m = [(a_w_s[i] * causal).astype(BF16) for i in range(NA)]
    ws_mt = [jnp.swapaxes(w, 1, 2) for w in ws_m]
    bs_t = [a_b_s[i].T for i in range(NA)]

    heads = lambda t: t.reshape(S, H, hd).transpose(1, 0, 2)
    unheads = lambda t: t.transpose(1, 0, 2).reshape(S, D)

    saved = []
    kv = None
    xc = x0
    for l in range(L):
        sh_m, sc_m, g_m, sh_f, sc_f, g_f = mod[l]
        st = {"x0": xc}
        h1 = _norm_mod_fwd(xc, row(pre_mix_g[l]), sh_m, sc_m, f"pre_mix_{l}")
        st["h1"] = h1
        if l < NA:
            zp = _mm(h1, (w_in, l), "nn", BF16, f"gmlp_in_{l}")
            yg = _gmlp_fwd(zp, row(b_in_f[l]), row(ln_g_f[l]), row(ln_b_f[l]), ws_m[l], bs_t[l], f"gmlp_gate_{l}")
            y = _mm(yg, (w_out, l), "nn", F32, f"gmlp_out_{l}")
            st.update(zp=zp, yg=yg)
        else:
            jb = l - NA
            qg = _mm(h1, (w_qg, jb), "nn", BF16, f"fox_qg_{jb}")
            q_raw = heads(qg[:, :D]).reshape(H * S, hd)
            qn = _head_norm_fwd(q_raw, row(b_q_norm_g[jb]), hd ** -0.5, f"fox_qnorm_{jb}").reshape(H, S, hd)
            qa = _augment(qn, kv["dcum"], True)
            qat = jnp.swapaxes(qa, 1, 2)
            o_tr, lse_r = _flash_fwd(kv["ka"], qat, kv["vat"], hd, f"fox_attn_{jb}")
            o_t = o_tr.transpose(2, 0, 1).reshape(S, D)
            og = _out_gate_fwd(o_t, qg, f"fox_gate_{jb}")
            y = _mm(og, (w_o, jb), "nn", F32, f"fox_out_{jb}")
            st.update(qg=qg, q_raw=q_raw, qa=qa, qat=qat, o_tr=o_tr, lse_r=lse_r, o_t=o_t, og=og)
        st["y"] = y
        x1 = _post_fwd(xc, y, row(post_mix_g[l]), g_m, f"post_mix_{l}")
        st["x1"] = x1
        h2 = _norm_mod_fwd(x1, row(pre_ffn_g[l]), sh_f, sc_f, f"pre_ffn_{l}")
        gu = _mm(h2, (w_gu, l), "nn", BF16, f"ffn_gu_{l}")
        act = _swiglu_fwd(gu, f"ffn_act_{l}")
        y2 = _mm(act, (w_dn, l), "nn", F32, f"ffn_down_{l}")
        xc = _post_fwd(x1, y2, row(post_ffn_g[l]), g_f, f"post_ffn_{l}")
        st.update(h2=h2, gu=gu, act=act, y2=y2)
        saved.append(st)
        if l == NA - 1:
            hk = _norm_mod_fwd(xc, row(kv_norm_g), kv_sh, kv_sc, "kv_pre")
            kvf = _mm(hk, w_kv, "nn", F32, "kv_proj")
            k_raw = heads(kvf[:, :D]).reshape(H * S, hd)
            kn = _head_norm_fwd(k_raw, row(k_norm_g), 1.0, "kv_knorm").reshape(H, S, hd)
            vb = heads(kvf[:, D:2 * D]).astype(BF16)
            f_t = kvf[:, 2 * D:2 * D + H].T
            b_col = kv_b_f.reshape(H, 1)
            dcum = _dcum_fwd(f_t, b_col, "kv_dcum")
            vt = kvf[:, D:2 * D].astype(BF16).reshape(S, H, hd).transpose(1, 2, 0)
            vat = jnp.where(lax.broadcasted_iota(jnp.int32, (1, LANES, 1), 1) == hd, jnp.asarray(1, BF16),
                            jnp.pad(vt, ((0, 0), (0, LANES - hd), (0, 0))))
            ka = _augment(kn, dcum, False)
            kv = dict(x=xc, hk=hk, k_raw=k_raw, ka=ka, kat=jnp.swapaxes(ka, 1, 2), vb=vb, vat=vat,
                      f_t=f_t, b_col=b_col, dcum=dcum)

    dx, loss_part = _loss_bwd(xc, tgt, "loss")
    loss = lax.psum(loss_part[0, 0], ("x", "y", "c"))

    gl = {n: [None] * weights[n].shape[0] for n in
          ["pre_mix_g", "post_mix_g", "pre_ffn_g", "post_ffn_g", "ffn_w_gu", "ffn_w_down", "a_w_in", "a_b_in", "a_ln_g",
           "a_ln_b", "a_w_s", "a_b_s", "a_w_out", "b_w_qg", "b_q_norm_g", "b_w_o"]}
    dmod = [None] * L
    dkn = dvb = ddc = None
    gkv = {}
    for l in reversed(range(L)):
        st = saved[l]
        sh_m, sc_m, g_m, sh_f, sc_f, g_f = mod[l]
        if l == NA - 1:
            dk_raw, gkv["k_norm_g"] = _head_norm_bwd(dkn.reshape(H * S, hd), kv["k_raw"], row(k_norm_g), 1.0, "kv_knorm_bwd")
            df_t, db_f = _dcum_bwd(ddc.reshape(H, S), kv["f_t"], kv["b_col"], "kv_dcum_bwd")
            dkvf = jnp.concatenate([unheads(dk_raw.reshape(H, S, hd)), unheads(dvb), df_t.T,
                                    jnp.zeros((S, kv_pad - (2 * D + H)), F32)], axis=1).astype(BF16)
            gkv["kv_w"] = _mm(kv["hk"], dkvf, "tn", BF16, "kv_proj_dw")[:, :2 * D + H]
            dhk = _mm(dkvf, w_kv, "nt", F32, "kv_proj_dx")
            dx, gkv["kv_norm_g"], dsh, dsc = _norm_mod_bwd(dx, dhk, kv["x"], row(kv_norm_g), kv_sh, kv_sc, "kv_pre_bwd")
            gkv["kv_b_f"] = db_f.reshape(H)
            dmod_kv = jnp.concatenate([dsh, dsc], axis=1)
        dy2, gl["post_ffn_g"][l], dg_f = _post_bwd(dx, st["y2"], row(post_ffn_g[l]), g_f, f"post_ffn_bwd_{l}")
        gl["ffn_w_down"][l] = _mm(st["act"], dy2, "tn", BF16, f"ffn_down_dw_{l}")
        dact = _mm(dy2, (w_dn, l), "nt", BF16, f"ffn_down_dx_{l}")
        dgu = _swiglu_bwd(dact, st["gu"], f"ffn_act_bwd_{l}")
        gl["ffn_w_gu"][l] = _mm(st["h2"], dgu, "tn", BF16, f"ffn_gu_dw_{l}")
        dh2 = _mm(dgu, (w_gu, l), "nt", F32, f"ffn_gu_dx_{l}")
        dx, gl["pre_ffn_g"][l], dsh_f, dsc_f = _norm_mod_bwd(dx, dh2, st["x1"], row(pre_ffn_g[l]), sh_f, sc_f, f"pre_ffn_bwd_{l}")
        dy, gl["post_mix_g"][l], dg_m = _post_bwd(dx, st["y"], row(post_mix_g[l]), g_m, f"post_mix_bwd_{l}")
        if l < NA:
            gl["a_w_out"][l] = _mm(st["yg"], dy, "tn", BF16, f"gmlp_out_dw_{l}")
            dyg = _mm(dy, (w_out, l), "nt", BF16, f"gmlp_out_dx_{l}")
            dzp, db_in, dlg, dlb, dws, dbs_t = _gmlp_bwd(dyg, st["zp"], row(b_in_f[l]), row(ln_g_f[l]), row(ln_b_f[l]),
                                                           ws_m[l], ws_mt[l], bs_t[l], f"gmlp_gate_bwd_{l}")
            gl["a_b_in"][l], gl["a_ln_g"][l], gl["a_ln_b"][l] = db_in[0], dlg[0], dlb[0]
            gl["a_w_s"][l], gl["a_b_s"][l] = dws * causal, dbs_t.T
            gl["a_w_in"][l] = _mm(st["h1"], dzp, "tn", BF16, f"gmlp_in_dw_{l}")
            dh1 = _mm(dzp, (w_in, l), "nt", F32, f"gmlp_in_dx_{l}")
        else:
            jb = l - NA
            gl["b_w_o"][jb] = _mm(st["og"], dy, "tn", BF16, f"fox_out_dw_{jb}")
            dog = _mm(dy, (w_o, jb), "nt", F32, f"fox_out_dx_{jb}")
            do_t, dgl = _out_gate_bwd(dog, st["o_t"], st["qg"], f"fox_gate_bwd_{jb}")
            do = heads(do_t)
            dot = do_t.reshape(S, H, hd).transpose(1, 2, 0)
            dqa_tr, dk_j, dv_j, dd_k, dd_q = _flash_bwd(kv["ka"], kv["kat"], st["qa"], st["qat"], kv["vb"], do, dot,
                                                        st["o_tr"], st["lse_r"], f"fox_attn_bwd_{jb}")
            dqn = dqa_tr[:, :, :hd, :].transpose(0, 1, 3, 2).reshape(H, S, hd)
            dk_j = dk_j[:, :, :hd]
            dd_j = dd_k.reshape(H, S) + dd_q.reshape(H, S)
            dkn = dk_j if dkn is None else dkn + dk_j
            dvb = dv_j if dvb is None else dvb + dv_j
            ddc = dd_j if ddc is None else ddc + dd_j
            dq_raw, dgq = _head_norm_bwd(dqn.reshape(H * S, hd), st["q_raw"], row(b_q_norm_g[jb]), hd ** -0.5, f"fox_qnorm_bwd_{jb}")
            gl["b_q_norm_g"][jb] = dgq[0]
            dqg = jnp.concatenate([unheads(dq_raw.reshape(H, S, hd)).astype(BF16), dgl], axis=1)
            gl["b_w_qg"][jb] = _mm(st["h1"], dqg, "tn", BF16, f"fox_qg_dw_{jb}")
            dh1 = _mm(dqg, (w_qg, jb), "nt", F32, f"fox_qg_dx_{jb}")
        dx, gl["pre_mix_g"][l], dsh_m, dsc_m = _norm_mod_bwd(dx, dh1, st["x0"], row(pre_mix_g[l]), sh_m, sc_m, f"pre_mix_bwd_{l}")
        dmod[l] = jnp.concatenate([dsh_m, dsc_m, dg_m, dsh_f, dsc_f, dg_f], axis=1)
    grad_x = dx[None]

    stack = lambda n: jnp.stack([t.reshape(weights[n].shape[1:]) for t in gl[n]])
    small = {"dmod": jnp.concatenate(dmod, axis=1), "dmod_kv": dmod_kv}
    for n in ["pre_mix_g", "post_mix_g", "pre_ffn_g", "post_ffn_g", "a_w_s", "a_b_s", "b_q_norm_g"]:
        small[n] = stack(n)
    for n in ["a_b_in", "a_ln_g", "a_ln_b"]:
        small[n] = jnp.stack(gl[n])
    for n in ["kv_norm_g", "kv_b_f", "k_norm_g"]:
        small[n] = gkv[n]
    sizes = {n: t.size for n, t in small.items()}
    flat = jnp.concatenate([t.reshape(-1).astype(F32) for t in small.values()])
    rows_small = -(-flat.size // (LANES * BF16_ROWS)) * BF16_ROWS
    flat = jnp.pad(flat, (0, rows_small * LANES - flat.size)).reshape(rows_small, LANES)
    flat_all = _own_slot(_gather8([flat], "gather_small")[0], flat, me)
    flat_sum = _sum_slots(flat_all, "sum_small").reshape(-1)
    offs, o_ = {}, 0
    for n, sz in sizes.items():
        offs[n] = o_
        o_ += sz
    take = lambda n, shape: flat_sum[offs[n]:offs[n] + sizes[n]].reshape(shape)
    dmod_rows = flat_all.reshape(N_DEV, -1)[:, offs["dmod"]:offs["dmod"] + sizes["dmod"] + sizes["dmod_kv"]]
    dmod_rows = jnp.pad(dmod_rows, ((0, BF16_ROWS - N_DEV), (0, 0)))

    grads = {}
    grads["ada_b"] = take("dmod", (L, 6 * D))
    grads["kv_ada_b"] = take("dmod_kv", (2 * D,))
    for n in ["pre_mix_g", "post_mix_g", "pre_ffn_g", "post_ffn_g", "a_w_s", "a_b_s", "b_q_norm_g", "kv_norm_g", "kv_b_f", "k_norm_g"]:
        grads[n] = take(n, weights[n].shape)
    for n in ["a_b_in", "a_ln_g", "a_ln_b"]:
        full = take(n, small[n].shape)
        w = weights[n].shape[1]
        grads[n] = lax.dynamic_slice_in_dim(full, chip * w, w, axis=1)
    ada_g = []
    for l in range(L):
        cols = lax.dynamic_slice_in_dim(dmod_rows[:, l * 6 * D:(l + 1) * 6 * D], chip * ada_cols, ada_cols, axis=1)
        ada_g.append(_mm(c_act, cols, "tn", F32, f"mod_proj_dw_{l}"))
    grads["ada_w"] = jnp.stack(ada_g)
    cols = lax.dynamic_slice_in_dim(dmod_rows[:, L * 6 * D:], chip * kvada_cols, kvada_cols, axis=1)
    grads["kv_ada_w"] = _mm(c_act, cols, "tn", F32, "mod_proj_kv_dw")

    specs = {"ffn_w_gu": (2, 0), "ffn_w_down": (1, 0), "a_w_in": (2, 0), "a_w_out": (1, 0), "kv_w": (0, 1),
             "b_w_qg": (2, 0), "b_w_o": (1, 0)}
    full_g = {n: jnp.stack(gl[n]) for n in big if n != "kv_w"}
    full_g["kv_w"] = gkv["kv_w"].reshape(D, N_CHIPS, kv_cols).transpose(1, 0, 2)
    from_core = _sibling_scatter([full_g[n] for n in big], [specs[n] for n in big], "scatter_g_core")
    chip_sums = [_sum_pair(_own_pieces(full_g[n], specs[n], ci), r, f"sum_g_core_{n}") for n, r in zip(big, from_core)]
    recv = _chip_scatter(chip_sums, "scatter_g_chip")
    recv = [_own_slot(r, lax.dynamic_index_in_dim(p, chip, 0, keepdims=False), chip) for r, p in zip(recv, chip_sums)]
    halves = [_sum_slots(r, f"sum_g_{n}") for n, r in zip(big, recv)]
    pairs = _sibling_pair(halves, "pair_g")
    for n, p, hlf in zip(big, pairs, halves):
        grads[n] = _own_slot(p, hlf, ci).reshape(weights[n].shape)

    outs_d, outs_m, outs_v = {}, {}, {}
    for n in names:
        w2 = weights[n] if weights[n].ndim > 1 else weights[n].reshape(1, -1)
        shp = w2.shape
        d_, m_, v_ = _adamw(w2, grads[n].reshape(shp), m_in[n].reshape(shp), v_in[n].reshape(shp), f"adamw_{n}")
        outs_d[n], outs_m[n], outs_v[n] = (t.reshape(weights[n].shape) for t in (d_, m_, v_))
    return (loss, grad_x, *[grads[n] for n in names], *[outs_d[n] for n in names],
            *[outs_m[n] for n in names], *[outs_v[n] for n in names])
```

```python
import functools

import jax
import jax.numpy as jnp
from jax import lax
from jax.experimental import pallas as pl
from jax.experimental.pallas import tpu as pltpu

F32 = jnp.float32
BF16 = jnp.bfloat16
MESH = pl.DeviceIdType.MESH
NORM_EPS = 1e-6
MASKED = -1e30
LANES = 128
BF16_ROWS = 16
ROW_BLOCK_BYTES = 12 << 20
ADAM_LR, ADAM_B1, ADAM_B2, ADAM_EPS, ADAM_WD, ADAM_STEP = 0.001, 0.9, 0.999, 1e-08, 0.01, 10
N_CHIPS, N_CORES, N_DEV = 4, 2, 8
ATTN_HEADS_PER_STEP = 4
ANY = pl.BlockSpec(memory_space=pl.ANY)


def _tile(n, cap, quantum):
    best = None
    d = quantum
    while d <= min(n, cap):
        if n % d == 0:
            best = d
        d += quantum
    return n if best is None else best


def _call(body, *, name, out_shape, grid=(), in_specs=None, out_specs=None, scratch=(), sem=None, aliases=None):
    params = {} if sem is None else {"dimension_semantics": sem}
    return pl.pallas_call(
        body, name=name, grid=grid, in_specs=in_specs, out_specs=out_specs, out_shape=out_shape,
        scratch_shapes=list(scratch), input_output_aliases=aliases or {},
        compiler_params=pltpu.CompilerParams(**params))


def _call_prefetch(body, *, name, out_shape, grid, n_prefetch, in_specs, out_specs, scratch, sem):
    spec = pltpu.PrefetchScalarGridSpec(num_scalar_prefetch=n_prefetch, grid=grid, in_specs=in_specs,
                                        out_specs=out_specs, scratch_shapes=list(scratch))
    return pl.pallas_call(
        body, name=name, grid_spec=spec, out_shape=out_shape,
        compiler_params=pltpu.CompilerParams(dimension_semantics=sem))


def _place():
    x, y, c = lax.axis_index("x"), lax.axis_index("y"), lax.axis_index("c")
    return x, y, c


def _mm(a, b, mode, out_dtype, name):
    b_arr, b_idx = b if isinstance(b, tuple) else (b, None)
    bs = b_arr.shape[-2:]
    if mode == "nn":
        (M, K), (K2, N) = a.shape, bs
        dims = (((1,), (0,)), ((), ()))
    elif mode == "nt":
        (M, K), (N, K2) = a.shape, bs
        dims = (((1,), (1,)), ((), ()))
    else:
        (K, M), (K2, N) = a.shape, bs
        dims = (((0,), (0,)), ((), ()))
    assert K == K2, (name, a.shape, b_arr.shape)
    if mode == "tn":
        tm = _tile(M, 1408, LANES)
        tk = _tile(K, 2048, BF16_ROWS)
        tn = _tile(N, 512, LANES)
    else:
        tm = _tile(M, 1024, BF16_ROWS)
        tk = K if K <= 2816 else _tile(K, 2816, LANES)
        tn = _tile(N, 1408 if tk <= 1024 else 512, LANES)
    if tn < 256:
        tn = N
        tm = _tile(M, 512, LANES if mode == "tn" else BF16_ROWS)
    nk = K // tk
    grid = (M // tm, N // tn, nk)

    if mode == "tn":
        a_spec = pl.BlockSpec((tk, tm), lambda i, j, k: (k, i))
    else:
        a_spec = pl.BlockSpec((tm, tk), lambda i, j, k: (i, k))
    if mode == "nt":
        b_blk, b_map = (tn, tk), (lambda i, j, k: (j, k))
    else:
        b_blk, b_map = (tk, tn), (lambda i, j, k: (k, j))
    if b_idx is None:
        b_spec = pl.BlockSpec(b_blk, b_map)
    else:
        b_spec = pl.BlockSpec((None,) + b_blk, lambda i, j, k: (b_idx,) + b_map(i, j, k))

    def body(a_ref, b_ref, o_ref, *acc):
        r = lax.dot_general(a_ref[...].astype(BF16), b_ref[...].astype(BF16), dims, preferred_element_type=F32)
        if nk == 1:
            o_ref[...] = r.astype(o_ref.dtype)
        else:
            k = pl.program_id(2)

            @pl.when(k == 0)
            def _():
                acc[0][...] = r

            @pl.when(k > 0)
            def _():
                acc[0][...] += r

            @pl.when(k == nk - 1)
            def _():
                o_ref[...] = acc[0][...].astype(o_ref.dtype)

    return _call(
        body, name=name, grid=grid, in_specs=[a_spec, b_spec],
        out_specs=pl.BlockSpec((tm, tn), lambda i, j, k: (i, j)),
        out_shape=jax.ShapeDtypeStruct((M, N), out_dtype),
        scratch=[pltpu.VMEM((tm, tn), F32)] if nk > 1 else [],
        sem=("parallel", "parallel", "arbitrary"))(a, b_arr)


def _rowwise(fn, rows, pars, outs, pouts, name):
    R = rows[0].shape[0]
    row_bytes = 4 * (sum(max(r.shape[1], LANES) for r in rows) + sum(max(c, LANES) for c, _ in outs))
    tb = _tile(R, max(BF16_ROWS, ROW_BLOCK_BYTES // row_bytes), BF16_ROWS)
    nr, npar, no = len(rows), len(pars), len(outs)

    def body(*refs):
        r_in, p_in = refs[:nr], refs[nr:nr + npar]
        r_out, p_out = refs[nr + npar:nr + npar + no], refs[nr + npar + no:]
        ro, po = fn([r[...] for r in r_in], [p[...] for p in p_in])
        for ref, val in zip(r_out, ro):
            if isinstance(val, (tuple, list)):
                off = 0
                for piece in val:
                    w = piece.shape[1]
                    ref[:, off:off + w] = piece.astype(ref.dtype)
                    off += w
            else:
                ref[...] = val.astype(ref.dtype)
        if p_out:
            first = pl.program_id(0) == 0

            @pl.when(first)
            def _():
                for ref, val in zip(p_out, po):
                    ref[...] = val

            @pl.when(jnp.logical_not(first))
            def _():
                for ref, val in zip(p_out, po):
                    ref[...] += val

    res = _call(
        body, name=name, grid=(R // tb,),
        in_specs=[pl.BlockSpec((tb, r.shape[1]), lambda i: (i, 0)) for r in rows]
        + [pl.BlockSpec(p.shape, lambda i: (0, 0)) for p in pars],
        out_specs=[pl.BlockSpec((tb, c), lambda i: (i, 0)) for c, _ in outs]
        + [pl.BlockSpec(s, lambda i: (0, 0)) for s in pouts],
        out_shape=[jax.ShapeDtypeStruct((R, c), dt) for c, dt in outs]
        + [jax.ShapeDtypeStruct(s, F32) for s in pouts],
        sem=("arbitrary",) if pouts else ("parallel",))(*rows, *pars)
    return list(res)


def _rms(x, g):
    return x * lax.rsqrt(jnp.mean(x * x, axis=-1, keepdims=True) + NORM_EPS) * g


def _norm_mod(x, g, sh, sc):
    return _rms(x, g) * (1.0 + sc) + sh


def _gated_post(y, g, gate):
    return gate * _rms(y, g)


def _norm_mod_fwd(x, g, sh, sc, name):
    return _rowwise(lambda r, p: ([_norm_mod(r[0], *p)], []), [x], [g, sh, sc], [(x.shape[1], BF16)], [], name)[0]


def _norm_mod_bwd(dxo, dh, x, g, sh, sc, name):
    def fn(r, p):
        _, vjp = jax.vjp(_norm_mod, r[2], *p)
        dx, dg, dsh, dsc = vjp(r[1].astype(F32))
        return [r[0] + dx], [dg, dsh, dsc]
    c = x.shape[1]
    return _rowwise(fn, [dxo, dh, x], [g, sh, sc], [(c, F32)], [(1, c)] * 3, name)


def _post_fwd(x, y, g, gate, name):
    return _rowwise(lambda r, p: ([r[0] + _gated_post(r[1].astype(F32), *p)], []), [x, y], [g, gate],
                    [(x.shape[1], F32)], [], name)[0]


def _post_bwd(dxo, y, g, gate, name):
    def fn(r, p):
        _, vjp = jax.vjp(_gated_post, r[1].astype(F32), *p)
        dy, dg, dgate = vjp(r[0])
        return [dy], [dg, dgate]
    c = y.shape[1]
    return _rowwise(fn, [dxo, y], [g, gate], [(c, BF16)], [(1, c)] * 2, name)


def _swiglu(g, u):
    return jax.nn.silu(g) * u


def _swiglu_fwd(gu, name):
    f = gu.shape[1] // 2
    return _rowwise(lambda r, p: ([_swiglu(r[0][:, :f].astype(F32), r[0][:, f:].astype(F32))], []), [gu], [],
                    [(f, BF16)], [], name)[0]


def _swiglu_bwd(da, gu, name):
    f = gu.shape[1] // 2

    def fn(r, p):
        _, vjp = jax.vjp(_swiglu, r[1][:, :f].astype(F32), r[1][:, f:].astype(F32))
        return [vjp(r[0].astype(F32))], []
    return _rowwise(fn, [da, gu], [], [(2 * f, BF16)], [], name)[0]


def _silu_rows(c, name):
    return _rowwise(lambda r, p: ([jax.nn.silu(r[0])], []), [c], [], [(c.shape[1], F32)], [], name)[0]


def _head_norm(x, g, scale):
    return _rms(x, g) * scale


def _head_norm_fwd(x, g, scale, name):
    return _rowwise(lambda r, p: ([_head_norm(r[0].astype(F32), p[0], scale)], []), [x], [g],
                    [(x.shape[1], BF16)], [], name)[0]


def _head_norm_bwd(dy, x, g, scale, name):
    def fn(r, p):
        _, vjp = jax.vjp(lambda t, gg: _head_norm(t, gg, scale), r[1].astype(F32), p[0])
        dx, dg = vjp(r[0])
        return [dx], [dg]
    c = x.shape[1]
    return _rowwise(fn, [dy, x], [g], [(c, F32)], [(1, c)], name)


def _out_gate_fwd(o, qg, name):
    d = o.shape[1]
    return _rowwise(lambda r, p: ([r[0] * jax.nn.sigmoid(r[1][:, d:].astype(F32))], []), [o, qg], [],
                    [(d, BF16)], [], name)[0]


def _out_gate_bwd(dog, o, qg, name):
    d = o.shape[1]

    def fn(r, p):
        _, vjp = jax.vjp(lambda oo, gl: oo * jax.nn.sigmoid(gl), r[1], r[2][:, d:].astype(F32))
        do, dgl = vjp(r[0])
        return [do, dgl], []
    return _rowwise(fn, [dog, o, qg], [], [(d, BF16), (d, BF16)], [], name)


def _loss_bwd(y, tgt, name):
    n = y.shape[1]

    def fn(r, p):
        e = r[0] - r[1]
        part = jnp.sum(jnp.sum(e * e, axis=1, keepdims=True), axis=0, keepdims=True) * (0.5 / n)
        return [e * (1.0 / n)], [part]
    return _rowwise(fn, [y, tgt], [], [(n, F32)], [(1, 1)], name)


def _adamw(w, g, m, v, name):
    shape = w.shape
    c = shape[-1]
    flat = [t.reshape(-1, c) for t in (w, g, m, v)]

    def fn(r, p):
        w_, g_, m_, v_ = r
        m2 = ADAM_B1 * m_ + (1.0 - ADAM_B1) * g_
        v2 = ADAM_B2 * v_ + (1.0 - ADAM_B2) * (g_ * g_)
        m_hat = m2 / (1.0 - ADAM_B1 ** ADAM_STEP)
        v_hat = v2 / (1.0 - ADAM_B2 ** ADAM_STEP)
        delta = -ADAM_LR * (m_hat / (jnp.sqrt(v_hat) + ADAM_EPS) + ADAM_WD * w_)
        return [delta, m2, v2], []
    res = _rowwise(fn, flat, [], [(c, F32)] * 3, [], name)
    return [t.reshape(shape) for t in res]


def _sum_pair(a, b, name):
    c = a.shape[-1]
    out = _rowwise(lambda r, p: ([r[0].astype(F32) + r[1].astype(F32)], []), [a.reshape(-1, c), b.reshape(-1, c)], [],
                   [(c, BF16)], [], name)[0]
    return out.reshape(a.shape)


def _sum_slots(recv, name, out_dtype=F32):
    n = recv.shape[0]
    shape = recv.shape[1:]
    c = shape[-1]
    r3 = recv.reshape(n, -1, c)
    rows = r3.shape[1]
    tb = _tile(rows, max(BF16_ROWS, ROW_BLOCK_BYTES // (4 * c * (n + 1))), BF16_ROWS)

    def body(r_ref, o_ref):
        acc = r_ref[0].astype(F32)
        for s in range(1, n):
            acc = acc + r_ref[s].astype(F32)
        o_ref[...] = acc.astype(o_ref.dtype)

    out = _call(body, name=name, grid=(rows // tb,),
                in_specs=[pl.BlockSpec((n, tb, c), lambda i: (0, i, 0))],
                out_specs=pl.BlockSpec((tb, c), lambda i: (i, 0)),
                out_shape=jax.ShapeDtypeStruct((rows, c), out_dtype), sem=("parallel",))(r3)
    return out.reshape(shape)


def _gmlp_pre(zu, zv, b_u, b_v, ln_g, ln_b):
    u = jax.nn.gelu(zu + b_u, approximate=True)
    v = jax.nn.gelu(zv + b_v, approximate=True)
    xc = v - jnp.mean(v, axis=-1, keepdims=True)
    vn = xc * lax.rsqrt(jnp.mean(xc * xc, axis=-1, keepdims=True) + NORM_EPS) * ln_g + ln_b
    return u, vn


def _gmlp_fwd(zp, b_in, ln_g, ln_b, ws, bs_t, name):
    S, gw2 = zp.shape
    gw = gw2 // 2
    G, ch, _ = ws.shape
    gd = gw // G
    tb = 2 * ch

    def body(zp_ref, bin_ref, lg_ref, lb_ref, ws_ref, bs_ref, o_ref):
        u, vn = _gmlp_pre(zp_ref[:, :gw].astype(F32), zp_ref[:, gw:].astype(F32), bin_ref[:, :gw], bin_ref[:, gw:],
                          lg_ref[...], lb_ref[...])
        vnb = vn.astype(BF16)
        for c in range(tb // ch):
            for g in range(G):
                rs, cs = slice(c * ch, (c + 1) * ch), slice(g * gd, (g + 1) * gd)
                vv = jnp.dot(ws_ref[g], vnb[rs, cs], preferred_element_type=F32) + bs_ref[:, g:g + 1]
                o_ref[rs, cs] = (u[rs, cs] * vv).astype(o_ref.dtype)

    full = lambda a: pl.BlockSpec(a.shape, lambda i: (0,) * a.ndim)
    return _call(body, name=name, grid=(S // tb,),
                 in_specs=[pl.BlockSpec((tb, gw2), lambda i: (i, 0)), full(b_in), full(ln_g), full(ln_b), full(ws), full(bs_t)],
                 out_specs=pl.BlockSpec((tb, gw), lambda i: (i, 0)),
                 out_shape=jax.ShapeDtypeStruct((S, gw), BF16), sem=("parallel",))(zp, b_in, ln_g, ln_b, ws, bs_t)


def _gmlp_bwd(dyg, zp, b_in, ln_g, ln_b, ws, ws_t, bs_t, name):
    S, gw2 = zp.shape
    gw = gw2 // 2
    G, ch, _ = ws.shape
    gd = gw // G
    tb = 2 * ch

    def body(dy_ref, zp_ref, bin_ref, lg_ref, lb_ref, ws_ref, wst_ref, bs_ref,
             dzp_ref, dbin_ref, dlg_ref, dlb_ref, dws_ref, dbs_ref, du_sc, dvn_sc):
        (u, vn), vjp = jax.vjp(_gmlp_pre, zp_ref[:, :gw].astype(F32), zp_ref[:, gw:].astype(F32), bin_ref[:, :gw],
                               bin_ref[:, gw:], lg_ref[...], lb_ref[...])
        vnb = vn.astype(BF16)
        first = pl.program_id(0) == 0

        @pl.when(first)
        def _():
            dws_ref[...] = jnp.zeros_like(dws_ref)

        lane = lax.broadcasted_iota(jnp.int32, (ch, G), 1)
        dbs = jnp.zeros((ch, G), F32)
        for g in range(G):
            cs = slice(g * gd, (g + 1) * gd)
            dws_g = jnp.zeros((ch, ch), F32)
            col = jnp.zeros((ch, 1), F32)
            for c in range(tb // ch):
                rs = slice(c * ch, (c + 1) * ch)
                vnp = vnb[rs, cs]
                vv = jnp.dot(ws_ref[g], vnp, preferred_element_type=F32) + bs_ref[:, g:g + 1]
                dy = dy_ref[rs, cs].astype(F32)
                du_sc[rs, cs] = dy * vv
                dvv = dy * u[rs, cs]
                dvvb = dvv.astype(BF16)
                dvn_sc[rs, cs] = jnp.dot(wst_ref[g], dvvb, preferred_element_type=F32)
                dws_g = dws_g + lax.dot_general(dvvb, vnp, (((1,), (1,)), ((), ())), preferred_element_type=F32)
                col = col + jnp.sum(dvv, axis=1, keepdims=True)
            dws_ref[g] += dws_g
            dbs = jnp.where(lane == g, col, dbs)
        dzu, dzv, dbu, dbv, dlg, dlb = vjp((du_sc[...], dvn_sc[...]))
        dzp_ref[:, :gw] = dzu.astype(dzp_ref.dtype)
        dzp_ref[:, gw:] = dzv.astype(dzp_ref.dtype)

        @pl.when(first)
        def _():
            dbin_ref[:, :gw] = dbu
            dbin_ref[:, gw:] = dbv
            dlg_ref[...] = dlg
            dlb_ref[...] = dlb
            dbs_ref[...] = dbs

        @pl.when(jnp.logical_not(first))
        def _():
            dbin_ref[:, :gw] += dbu
            dbin_ref[:, gw:] += dbv
            dlg_ref[...] += dlg
            dlb_ref[...] += dlb
            dbs_ref[...] += dbs

    full = lambda a: pl.BlockSpec(a.shape, lambda i: (0,) * a.ndim)
    fshape = lambda s: pl.BlockSpec(s, lambda i: (0,) * len(s))
    return _call(
        body, name=name, grid=(S // tb,),
        in_specs=[pl.BlockSpec((tb, gw), lambda i: (i, 0)), pl.BlockSpec((tb, gw2), lambda i: (i, 0)),
                  full(b_in), full(ln_g), full(ln_b), full(ws), full(ws_t), full(bs_t)],
        out_specs=[pl.BlockSpec((tb, gw2), lambda i: (i, 0)), fshape((1, gw2)), fshape((1, gw)), fshape((1, gw)),
                   fshape((G, ch, ch)), fshape((ch, G))],
        out_shape=[jax.ShapeDtypeStruct((S, gw2), BF16), jax.ShapeDtypeStruct((1, gw2), F32),
                   jax.ShapeDtypeStruct((1, gw), F32), jax.ShapeDtypeStruct((1, gw), F32),
                   jax.ShapeDtypeStruct((G, ch, ch), F32), jax.ShapeDtypeStruct((ch, G), F32)],
        scratch=[pltpu.VMEM((tb, gw), F32), pltpu.VMEM((tb, gw), F32)],
        sem=("arbitrary",))(dyg, zp, b_in, ln_g, ln_b, ws, ws_t, bs_t)


def _dot_01(x, ones_bf16):
    hi = x.astype(BF16)
    r1 = x - hi.astype(F32)
    mid = r1.astype(BF16)
    lo = (r1 - mid.astype(F32)).astype(BF16)
    dot = lambda t: jnp.dot(t, ones_bf16, preferred_element_type=F32)
    return dot(hi) + dot(mid) + dot(lo)


def _log_sigmoid(x):
    return jnp.minimum(x, 0.0) - jnp.log1p(jnp.exp(-jnp.abs(x)))


def _dcum_fwd(f_t, b_col, name):
    H, S = f_t.shape
    tb = _tile(S, 512, LANES)

    def body(f_ref, b_ref, o_ref, carry):
        @pl.when(pl.program_id(0) == 0)
        def _():
            carry[...] = jnp.zeros_like(carry)

        ls = _log_sigmoid(f_ref[...] + b_ref[...])
        r = lax.broadcasted_iota(jnp.int32, (tb, tb), 0)
        c = lax.broadcasted_iota(jnp.int32, (tb, tb), 1)
        upper = (r <= c).astype(BF16)
        o_ref[...] = _dot_01(ls, upper) + carry[...]
        carry[...] += jnp.sum(ls, axis=1, keepdims=True)

    return _call(body, name=name, grid=(S // tb,),
                 in_specs=[pl.BlockSpec((H, tb), lambda i: (0, i)), pl.BlockSpec((H, 1), lambda i: (0, 0))],
                 out_specs=pl.BlockSpec((H, tb), lambda i: (0, i)),
                 out_shape=jax.ShapeDtypeStruct((H, S), F32),
                 scratch=[pltpu.VMEM((H, 1), F32)], sem=("arbitrary",))(f_t, b_col)


def _dcum_bwd(dd_t, f_t, b_col, name):
    H, S = f_t.shape
    tb = _tile(S, 512, LANES)
    nb = S // tb

    def body(dd_ref, f_ref, b_ref, df_ref, db_ref, carry):
        first = pl.program_id(0) == 0

        @pl.when(first)
        def _():
            carry[...] = jnp.zeros_like(carry)

        dd = dd_ref[...]
        r = lax.broadcasted_iota(jnp.int32, (tb, tb), 0)
        c = lax.broadcasted_iota(jnp.int32, (tb, tb), 1)
        lower = (r >= c).astype(BF16)
        rev = _dot_01(dd, lower) + carry[...]
        carry[...] += jnp.sum(dd, axis=1, keepdims=True)
        df = rev * jax.nn.sigmoid(-(f_ref[...] + b_ref[...]))
        df_ref[...] = df
        part = jnp.sum(df, axis=1, keepdims=True)

        @pl.when(first)
        def _():
            db_ref[...] = part

        @pl.when(jnp.logical_not(first))
        def _():
            db_ref[...] += part

    return _call(body, name=name, grid=(nb,),
                 in_specs=[pl.BlockSpec((H, tb), lambda i: (0, nb - 1 - i)), pl.BlockSpec((H, tb), lambda i: (0, nb - 1 - i)),
                           pl.BlockSpec((H, 1), lambda i: (0, 0))],
                 out_specs=[pl.BlockSpec((H, tb), lambda i: (0, nb - 1 - i)), pl.BlockSpec((H, 1), lambda i: (0, 0))],
                 out_shape=[jax.ShapeDtypeStruct((H, S), F32), jax.ShapeDtypeStruct((H, 1), F32)],
                 scratch=[pltpu.VMEM((H, 1), F32)], sem=("arbitrary",))(dd_t, f_t, b_col)


def _attn_tile(S):
    return _tile(S, 512, LANES)


def _causal(t, transposed):
    r = lax.broadcasted_iota(jnp.int32, (t, t), 0)
    c = lax.broadcasted_iota(jnp.int32, (t, t), 1)
    return (r <= c) if transposed else (c <= r)


def _tri_pairs(n, key_major):
    if key_major:
        pairs = [(i, j) for j in range(n) for i in range(j, n)]
    else:
        pairs = [(i, j) for i in range(n) for j in range(i + 1)]
    return jnp.asarray([p[0] for p in pairs], jnp.int32), jnp.asarray([p[1] for p in pairs], jnp.int32)


def _split3(x):
    hi = lax.reduce_precision(x, 8, 7)
    r = x - hi
    mid = lax.reduce_precision(r, 8, 7)
    lo = lax.reduce_precision(r - mid, 8, 7)
    return hi.astype(BF16), mid.astype(BF16), lo.astype(BF16)


def _augment(xn, dcum, query):
    H, S, hd = xn.shape
    parts = list(_split3(dcum))
    vals = parts + [1.0] * 3 if query else [1.0] * 3 + [-p for p in parts]
    lane = lax.broadcasted_iota(jnp.int32, (1, 1, LANES), 2)
    out = jnp.pad(xn, ((0, 0), (0, 0), (0, LANES - hd)))
    for k, val in enumerate(vals):
        val = jnp.asarray(val, BF16)
        out = jnp.where(lane == hd + k, val[..., None] if val.ndim else val, out)
    return out


def _scores_t(k_ref, qt_ref, h, t, diag):
    st = jnp.dot(k_ref[h], qt_ref[h], preferred_element_type=F32)
    return jnp.where(_causal(t, True), st, MASKED) if diag else st


def _flash_fwd(ka, qat, vat, hd, name):
    H, S, da = ka.shape
    t = _attn_tile(S)
    hb = ATTN_HEADS_PER_STEP
    it, jt = _tri_pairs(S // t, False)

    def body(it_ref, jt_ref, k_ref, qt_ref, vt_ref, o_ref, lse_ref, m_sc, acc_sc):
        i, j = it_ref[pl.program_id(1)], jt_ref[pl.program_id(1)]

        @pl.when(j == 0)
        def _():
            m_sc[...] = jnp.full_like(m_sc, MASKED)
            acc_sc[...] = jnp.zeros_like(acc_sc)

        def step(diag):
            for h in range(hb):
                st = _scores_t(k_ref, qt_ref, h, t, diag)
                m_prev = m_sc[h]
                m_new = jnp.maximum(m_prev, jnp.max(st, axis=0, keepdims=True))
                pt = jnp.exp(st - m_new).astype(BF16)
                acc_sc[h] = jnp.exp(m_prev - m_new) * acc_sc[h] + jnp.dot(vt_ref[h], pt, preferred_element_type=F32)
                m_sc[h] = m_new

        @pl.when(j < i)
        def _():
            step(False)

        @pl.when(j == i)
        def _():
            step(True)
            for h in range(hb):
                l = acc_sc[h, hd:hd + 1, :]
                o_ref[h] = acc_sc[h, :hd, :] / l
                lse_ref[h] = m_sc[h] + jnp.log(l)

    qcol = lambda h, p, it_, jt_: (h, 0, it_[p])
    kcol = lambda h, p, it_, jt_: (h, 0, jt_[p])
    krow = lambda h, p, it_, jt_: (h, jt_[p], 0)
    return _call_prefetch(
        body, name=name, grid=(H // hb, it.shape[0]), n_prefetch=2,
        in_specs=[pl.BlockSpec((hb, t, da), krow), pl.BlockSpec((hb, da, t), qcol), pl.BlockSpec((hb, da, t), kcol)],
        out_specs=[pl.BlockSpec((hb, hd, t), qcol), pl.BlockSpec((hb, 1, t), qcol)],
        out_shape=[jax.ShapeDtypeStruct((H, hd, S), F32), jax.ShapeDtypeStruct((H, 1, S), F32)],
        scratch=[pltpu.VMEM((hb, 1, t), F32), pltpu.VMEM((hb, da, t), F32)],
        sem=("parallel", "arbitrary"))(it, jt, ka, qat, vat)


def _ds_t(k_ref, qt_ref, v_ref, dot_ref, lse_ref, dl, h, t, diag):
    pt = jnp.exp(_scores_t(k_ref, qt_ref, h, t, diag) - lse_ref[h])
    dpt = jnp.dot(v_ref[h], dot_ref[h], preferred_element_type=F32)
    return pt, pt * (dpt - dl)


def _flash_bwd(ka, kat, qa, qat, v, do, dot, o_tr, lse_r, name):
    H, S, hd = v.shape
    da = ka.shape[2]
    t = _attn_tile(S)
    n = S // t
    hb = ATTN_HEADS_PER_STEP
    it, jt = _tri_pairs(n, True)

    def body(it_ref, jt_ref, k_ref, kt_ref, q_ref, qt_ref, v_ref, do_ref, dot_ref, o_ref, lse_ref,
             dq_ref, dk_ref, dv_ref, dd_ref, ddq_ref, dk_sc, dv_sc, dd_sc):
        i, j = it_ref[pl.program_id(1)], jt_ref[pl.program_id(1)]

        @pl.when(pl.program_id(1) == 0)
        def _():
            ddq_ref[...] = jnp.zeros_like(ddq_ref)
            dq_ref[...] = jnp.zeros_like(dq_ref)

        def step(diag):
            for h in range(hb):
                dl = jnp.sum(dot_ref[h].astype(F32) * o_ref[h], axis=0, keepdims=True)
                pt, dst = _ds_t(k_ref, qt_ref, v_ref, dot_ref, lse_ref, dl, h, t, diag)
                dsb = dst.astype(BF16)
                dv_sc[h] += jnp.dot(pt.astype(BF16), do_ref[h], preferred_element_type=F32)
                dk_sc[h] += jnp.dot(dsb, q_ref[h], preferred_element_type=F32)
                dq_ref[h, i] += jnp.dot(kt_ref[h], dsb, preferred_element_type=F32)
                part = dst[:, :LANES]
                for c in range(1, t // LANES):
                    part = part + dst[:, c * LANES:(c + 1) * LANES]
                dd_sc[h] += part
                ddq_ref[h, i] += jnp.sum(dst, axis=0, keepdims=True)

        @pl.when(i == j)
        def _():
            dk_sc[...] = jnp.zeros_like(dk_sc)
            dv_sc[...] = jnp.zeros_like(dv_sc)
            dd_sc[...] = jnp.zeros_like(dd_sc)
            step(True)

        @pl.when(i > j)
        def _():
            step(False)

        @pl.when(i == n - 1)
        def _():
            dk_ref[...] = dk_sc[...]
            dv_ref[...] = dv_sc[...]
            for h in range(hb):
                dd_ref[h] = -jnp.sum(dd_sc[h], axis=1, keepdims=True)

    krow = lambda h, p, it_, jt_: (h, jt_[p], 0)
    kcol = lambda h, p, it_, jt_: (h, 0, jt_[p])
    qrow = lambda h, p, it_, jt_: (h, it_[p], 0)
    qcol = lambda h, p, it_, jt_: (h, 0, it_[p])
    whole = lambda h, p, it_, jt_: (h, 0, 0, 0)
    return _call_prefetch(
        body, name=name, grid=(H // hb, it.shape[0]), n_prefetch=2,
        in_specs=[pl.BlockSpec((hb, t, da), krow), pl.BlockSpec((hb, da, t), kcol), pl.BlockSpec((hb, t, da), qrow),
                  pl.BlockSpec((hb, da, t), qcol), pl.BlockSpec((hb, t, hd), krow), pl.BlockSpec((hb, t, hd), qrow),
                  pl.BlockSpec((hb, hd, t), qcol), pl.BlockSpec((hb, hd, t), qcol), pl.BlockSpec((hb, 1, t), qcol)],
        out_specs=[pl.BlockSpec((hb, n, da, t), whole), pl.BlockSpec((hb, t, da), krow), pl.BlockSpec((hb, t, hd), krow),
                   pl.BlockSpec((hb, t, 1), krow), pl.BlockSpec((hb, n, 1, t), whole)],
        out_shape=[jax.ShapeDtypeStruct((H, n, da, t), F32), jax.ShapeDtypeStruct((H, S, da), F32),
                   jax.ShapeDtypeStruct((H, S, hd), F32), jax.ShapeDtypeStruct((H, S, 1), F32),
                   jax.ShapeDtypeStruct((H, n, 1, t), F32)],
        scratch=[pltpu.VMEM((hb, t, da), F32), pltpu.VMEM((hb, t, hd), F32), pltpu.VMEM((hb, t, LANES), F32)],
        sem=("parallel", "arbitrary"))(it, jt, ka, kat, qa, qat, v, do, dot, o_tr, lse_r)


def _offsets(n_bits):
    return [tuple((k >> b) & 1 for b in reversed(range(n_bits))) for k in range(1, 1 << n_bits)]


def _own_slot(out, own, idx):
    return lax.dynamic_update_index_in_dim(out, own.astype(out.dtype), idx, 0)


def _gather8(arrs, name):
    n = len(arrs)
    offs = _offsets(3)

    def body(*refs):
        ins, outs = refs[:n], refs[n:2 * n]
        ssem, rsem = refs[2 * n:]
        x, y, c = _place()
        me = 4 * x + 2 * y + c
        copies = []
        for a in range(n):
            for k, (dx, dy, dcc) in enumerate(offs):
                cp = pltpu.make_async_remote_copy(
                    src_ref=ins[a], dst_ref=outs[a].at[me], send_sem=ssem.at[a, k], recv_sem=rsem.at[a, k],
                    device_id=((x + dx) % 2, (y + dy) % 2, (c + dcc) % 2), device_id_type=MESH)
                cp.start()
                copies.append(cp)
        for cp in copies:
            cp.wait()

    return _call(body, name=name, in_specs=[ANY] * n, out_specs=[ANY] * n,
                 out_shape=[jax.ShapeDtypeStruct((N_DEV,) + a.shape, a.dtype) for a in arrs],
                 scratch=[pltpu.SemaphoreType.DMA((n, 7)), pltpu.SemaphoreType.DMA((n, 7))])(*arrs)


def _chip_gather(arrs, halved, name):
    n = len(arrs)
    offs = _offsets(2)

    def body(*refs):
        ins, outs = refs[:n], refs[n:2 * n]
        ssem, rsem = refs[2 * n:]
        x, y, c = _place()
        chip = 2 * x + y
        copies = []
        for a in range(n):
            if halved:
                hn = arrs[a].shape[0] // 2
                src = ins[a].at[pl.ds(c * hn, hn)]
                dst = outs[a].at[chip, pl.ds(c * hn, hn)]
            else:
                src, dst = ins[a], outs[a].at[chip]
            for k, (dx, dy) in enumerate(offs):
                cp = pltpu.make_async_remote_copy(
                    src_ref=src, dst_ref=dst, send_sem=ssem.at[a, k], recv_sem=rsem.at[a, k],
                    device_id=((x + dx) % 2, (y + dy) % 2, c), device_id_type=MESH)
                cp.start()
                copies.append(cp)
        for cp in copies:
            cp.wait()

    return _call(body, name=name, in_specs=[ANY] * n, out_specs=[ANY] * n,
                 out_shape=[jax.ShapeDtypeStruct((N_CHIPS,) + a.shape, a.dtype) for a in arrs],
                 scratch=[pltpu.SemaphoreType.DMA((n, 3)), pltpu.SemaphoreType.DMA((n, 3))])(*arrs)


def _sibling_fill(bufs, name):
    n = len(bufs)
    offs = _offsets(2)

    def body(*refs):
        ins, outs = refs[:n], refs[n:2 * n]
        ssem, rsem = refs[2 * n:]
        x, y, c = _place()
        copies = []
        for a in range(n):
            hn = bufs[a].shape[1] // 2
            for k, (dx, dy) in enumerate(offs):
                chip = 2 * ((x + dx) % 2) + (y + dy) % 2
                cp = pltpu.make_async_remote_copy(
                    src_ref=ins[a].at[chip, pl.ds(c * hn, hn)], dst_ref=outs[a].at[chip, pl.ds(c * hn, hn)],
                    send_sem=ssem.at[a, k], recv_sem=rsem.at[a, k],
                    device_id=(x, y, 1 - c), device_id_type=MESH)
                cp.start()
                copies.append(cp)
        for cp in copies:
            cp.wait()

    return _call(body, name=name, in_specs=[ANY] * n, out_specs=[ANY] * n,
                 out_shape=[jax.ShapeDtypeStruct(b.shape, b.dtype) for b in bufs],
                 scratch=[pltpu.SemaphoreType.DMA((n, 3)), pltpu.SemaphoreType.DMA((n, 3))],
                 aliases={a: a for a in range(n)})(*bufs)


def _sibling_pair(arrs, name):
    n = len(arrs)

    def body(*refs):
        ins, outs = refs[:n], refs[n:2 * n]
        ssem, rsem = refs[2 * n:]
        x, y, c = _place()
        copies = []
        for a in range(n):
            cp = pltpu.make_async_remote_copy(
                src_ref=ins[a], dst_ref=outs[a].at[c], send_sem=ssem.at[a], recv_sem=rsem.at[a],
                device_id=(x, y, 1 - c), device_id_type=MESH)
            cp.start()
            copies.append(cp)
        for cp in copies:
            cp.wait()

    return _call(body, name=name, in_specs=[ANY] * n, out_specs=[ANY] * n,
                 out_shape=[jax.ShapeDtypeStruct((N_CORES,) + a.shape, a.dtype) for a in arrs],
                 scratch=[pltpu.SemaphoreType.DMA((n,)), pltpu.SemaphoreType.DMA((n,))])(*arrs)


def _piece(shape, spec, j, h):
    shard_ax, half_ax = spec
    w = shape[shard_ax] // N_CHIPS
    idx = [slice(None)] * len(shape)
    idx[shard_ax] = pl.ds(j * w, w)
    hn = (w if half_ax == shard_ax else shape[half_ax]) // 2
    assert half_ax != shard_ax
    idx[half_ax] = pl.ds(h * hn, hn)
    return tuple(idx)


def _piece_shape(shape, spec):
    shard_ax, half_ax = spec
    s = list(shape)
    s[shard_ax] //= N_CHIPS
    s[half_ax] //= 2
    return tuple(s)


def _own_pieces(g, spec, c):
    shard_ax, half_ax = spec
    hn = g.shape[half_ax] // 2
    half = lax.dynamic_slice_in_dim(g, c * hn, hn, axis=half_ax)
    shape = list(half.shape)
    shape[shard_ax:shard_ax + 1] = [N_CHIPS, shape[shard_ax] // N_CHIPS]
    return jnp.moveaxis(half.reshape(shape), shard_ax, 0)


def _sibling_scatter(arrs, specs, name):
    n = len(arrs)

    def body(*refs):
        ins, outs = refs[:n], refs[n:2 * n]
        ssem, rsem = refs[2 * n:]
        x, y, c = _place()
        for mine in range(N_CORES):
            @pl.when(c == mine)
            def _():
                copies = []
                for a in range(n):
                    for j in range(N_CHIPS):
                        cp = pltpu.make_async_remote_copy(
                            src_ref=ins[a].at[_piece(arrs[a].shape, specs[a], j, 1 - mine)], dst_ref=outs[a].at[j],
                            send_sem=ssem.at[a, j], recv_sem=rsem.at[a, j],
                            device_id=(x, y, 1 - mine), device_id_type=MESH)
                        cp.start()
                        copies.append(cp)
                for cp in copies:
                    cp.wait()

    return _call(body, name=name, in_specs=[ANY] * n, out_specs=[ANY] * n,
                 out_shape=[jax.ShapeDtypeStruct((N_CHIPS,) + _piece_shape(a.shape, s), a.dtype)
                            for a, s in zip(arrs, specs)],
                 scratch=[pltpu.SemaphoreType.DMA((n, N_CHIPS))] * 2)(*arrs)


def _chip_scatter(arrs, name):
    n = len(arrs)
    offs = _offsets(2)

    def body(*refs):
        ins, outs = refs[:n], refs[n:2 * n]
        ssem, rsem = refs[2 * n:]
        x, y, c = _place()
        chip = 2 * x + y
        copies = []
        for a in range(n):
            for k, (dx, dy) in enumerate(offs):
                tx, ty = (x + dx) % 2, (y + dy) % 2
                cp = pltpu.make_async_remote_copy(
                    src_ref=ins[a].at[2 * tx + ty], dst_ref=outs[a].at[chip], send_sem=ssem.at[a, k], recv_sem=rsem.at[a, k],
                    device_id=(tx, ty, c), device_id_type=MESH)
                cp.start()
                copies.append(cp)
        for cp in copies:
            cp.wait()

    return _call(body, name=name, in_specs=[ANY] * n, out_specs=[ANY] * n,
                 out_shape=[jax.ShapeDtypeStruct(a.shape, a.dtype) for a in arrs],
                 scratch=[pltpu.SemaphoreType.DMA((n, 3)), pltpu.SemaphoreType.DMA((n, 3))])(*arrs)


def kernel(x, c, ada_w, ada_b, pre_mix_g, post_mix_g, pre_ffn_g, post_ffn_g, ffn_w_gu, ffn_w_down, a_w_in, a_b_in, a_ln_g, a_ln_b, a_w_s, a_b_s, a_w_out, kv_ada_w, kv_ada_b, kv_norm_g, kv_w, kv_b_f, k_norm_g, b_w_qg, b_q_norm_g, b_w_o, loss_target, m_ada_w, m_ada_b, m_pre_mix_g, m_post_mix_g, m_pre_ffn_g, m_post_ffn_g, m_ffn_w_gu, m_ffn_w_down, m_a_w_in, m_a_b_in, m_a_ln_g, m_a_ln_b, m_a_w_s, m_a_b_s, m_a_w_out, m_kv_ada_w, m_kv_ada_b, m_kv_norm_g, m_kv_w, m_kv_b_f, m_k_norm_g, m_b_w_qg, m_b_q_norm_g, m_b_w_o, v_ada_w, v_ada_b, v_pre_mix_g, v_post_mix_g, v_pre_ffn_g, v_post_ffn_g, v_ffn_w_gu, v_ffn_w_down, v_a_w_in, v_a_b_in, v_a_ln_g, v_a_ln_b, v_a_w_s, v_a_b_s, v_a_w_out, v_kv_ada_w, v_kv_ada_b, v_kv_norm_g, v_kv_w, v_kv_b_f, v_k_norm_g, v_b_w_qg, v_b_q_norm_g, v_b_w_o):
    weights = dict(ada_w=ada_w, ada_b=ada_b, pre_mix_g=pre_mix_g, post_mix_g=post_mix_g, pre_ffn_g=pre_ffn_g,
                   post_ffn_g=post_ffn_g, ffn_w_gu=ffn_w_gu, ffn_w_down=ffn_w_down, a_w_in=a_w_in, a_b_in=a_b_in,
                   a_ln_g=a_ln_g, a_ln_b=a_ln_b, a_w_s=a_w_s, a_b_s=a_b_s, a_w_out=a_w_out, kv_ada_w=kv_ada_w,
                   kv_ada_b=kv_ada_b, kv_norm_g=kv_norm_g, kv_w=kv_w, kv_b_f=kv_b_f, k_norm_g=k_norm_g, b_w_qg=b_w_qg,
                   b_q_norm_g=b_q_norm_g, b_w_o=b_w_o)
    m_in = dict(ada_w=m_ada_w, ada_b=m_ada_b, pre_mix_g=m_pre_mix_g, post_mix_g=m_post_mix_g, pre_ffn_g=m_pre_ffn_g,
                post_ffn_g=m_post_ffn_g, ffn_w_gu=m_ffn_w_gu, ffn_w_down=m_ffn_w_down, a_w_in=m_a_w_in, a_b_in=m_a_b_in,
                a_ln_g=m_a_ln_g, a_ln_b=m_a_ln_b, a_w_s=m_a_w_s, a_b_s=m_a_b_s, a_w_out=m_a_w_out, kv_ada_w=m_kv_ada_w,
                kv_ada_b=m_kv_ada_b, kv_norm_g=m_kv_norm_g, kv_w=m_kv_w, kv_b_f=m_kv_b_f, k_norm_g=m_k_norm_g,
                b_w_qg=m_b_w_qg, b_q_norm_g=m_b_q_norm_g, b_w_o=m_b_w_o)
    v_in = dict(ada_w=v_ada_w, ada_b=v_ada_b, pre_mix_g=v_pre_mix_g, post_mix_g=v_post_mix_g, pre_ffn_g=v_pre_ffn_g,
                post_ffn_g=v_post_ffn_g, ffn_w_gu=v_ffn_w_gu, ffn_w_down=v_ffn_w_down, a_w_in=v_a_w_in, a_b_in=v_a_b_in,
                a_ln_g=v_a_ln_g, a_ln_b=v_a_ln_b, a_w_s=v_a_w_s, a_b_s=v_a_b_s, a_w_out=v_a_w_out, kv_ada_w=v_kv_ada_w,
                kv_ada_b=v_kv_ada_b, kv_norm_g=v_kv_norm_g, kv_w=v_kv_w, kv_b_f=v_kv_b_f, k_norm_g=v_k_norm_g,
                b_w_qg=v_b_w_qg, b_q_norm_g=v_b_q_norm_g, b_w_o=v_b_w_o)
    names = list(weights)

    S, D = x.shape[1], x.shape[2]
    L, NA, NB = ada_w.shape[0], a_w_in.shape[0], b_w_qg.shape[0]
    H = kv_b_f.shape[0]
    hd = D // H
    G, CH = a_w_s.shape[1], a_w_s.shape[2]
    GW = a_w_out.shape[1] * N_CHIPS
    F = ffn_w_down.shape[1] * N_CHIPS
    ada_cols = ada_w.shape[2]
    kvada_cols = kv_ada_w.shape[1]
    kv_cols = kv_w.shape[1]
    kv_pad = -(-(2 * D + H) // LANES) * LANES
    xi, yi, ci = _place()
    chip = 2 * xi + yi
    me = 2 * chip + ci
    x0 = x[0]
    tgt = loss_target[0]
    row = lambda t: t.reshape(1, -1)

    c_all = _own_slot(_gather8([c], "gather_c")[0], c, me).reshape(N_DEV, D)
    c_act = _silu_rows(jnp.pad(c_all, ((0, BF16_ROWS - N_DEV), (0, 0))), "silu_c")
    mod_sh = [_mm(c_act, (ada_w, l), "nn", F32, f"mod_proj_{l}") for l in range(L)]
    mod_sh.append(_mm(c_act, kv_ada_w, "nn", F32, "mod_proj_kv"))
    mod_sh = jnp.concatenate(mod_sh, axis=1)
    small_sh = [mod_sh, a_b_in, a_ln_g, a_ln_b]
    mod_all, b_in_all, ln_g_all, ln_b_all = [
        _own_slot(o, s, chip) for o, s in zip(_chip_gather(small_sh, False, "gather_mod"), small_sh)]
    mine = lax.dynamic_index_in_dim(mod_all, me, axis=1, keepdims=False)
    mod = [jnp.concatenate([mine[j, l * ada_cols:(l + 1) * ada_cols] for j in range(N_CHIPS)]) + ada_b[l] for l in range(L)]
    mod = [[row(t) for t in jnp.split(m_, 6)] for m_ in mod]
    mod_kv = jnp.concatenate([mine[j, L * ada_cols:] for j in range(N_CHIPS)]) + kv_ada_b
    kv_sh, kv_sc = [row(t) for t in jnp.split(mod_kv, 2)]
    cat_chips = lambda t, ax: jnp.concatenate([t[j] for j in range(N_CHIPS)], axis=ax)
    b_in_f = cat_chips(b_in_all, 1)
    ln_g_f, ln_b_f = cat_chips(ln_g_all, 1), cat_chips(ln_b_all, 1)

    big = ["ffn_w_gu", "ffn_w_down", "a_w_in", "a_w_out", "kv_w", "b_w_qg", "b_w_o"]
    own_w = [weights[n].astype(BF16) for n in big]
    gathered = _sibling_fill(_chip_gather(own_w, True, "gather_w"), "fill_w")
    gathered = {n: _own_slot(g, w, chip) for n, g, w in zip(big, gathered, own_w)}
    w_gu = cat_chips(gathered["ffn_w_gu"], 2)
    w_dn = cat_chips(gathered["ffn_w_down"], 1)
    w_in = cat_chips(gathered["a_w_in"], 2)
    w_out = cat_chips(gathered["a_w_out"], 1)
    w_kv = jnp.pad(cat_chips(gathered["kv_w"], 1), ((0, 0), (0, kv_pad - (2 * D + H))))
    w_qg = cat_chips(gathered["b_w_qg"], 2)
    w_o = cat_chips(gathered["b_w_o"], 1)

    causal = jnp.tril(jnp.ones((CH, CH), F32))
    ws_m = [(a_w_s[i] * causal).astype(BF16) for i in range(NA)]
    ws_mt = [jnp.swapaxes(w, 1, 2) for w in ws_m]
    bs_t = [a_b_s[i].T for i in range(NA)]

    heads = lambda t: t.reshape(S, H, hd).transpose(1, 0, 2)
    unheads = lambda t: t.transpose(1, 0, 2).reshape(S, D)

    saved = []
    kv = None
    xc = x0
    for l in range(L):
        sh_m, sc_m, g_m, sh_f, sc_f, g_f = mod[l]
        st = {"x0": xc}
        h1 = _norm_mod_fwd(xc, row(pre_mix_g[l]), sh_m, sc_m, f"pre_mix_{l}")
        st["h1"] = h1
        if l < NA:
            zp = _mm(h1, (w_in, l), "nn", BF16, f"gmlp_in_{l}")
            yg = _gmlp_fwd(zp, row(b_in_f[l]), row(ln_g_f[l]), row(ln_b_f[l]), ws_m[l], bs_t[l], f"gmlp_gate_{l}")
            y = _mm(yg, (w_out, l), "nn", F32, f"gmlp_out_{l}")
            st.update(zp=zp, yg=yg)
        else:
            jb = l - NA
            qg = _mm(h1, (w_qg, jb), "nn", BF16, f"fox_qg_{jb}")
            q_raw = heads(qg[:, :D]).reshape(H * S, hd)
            qn = _head_norm_fwd(q_raw, row(b_q_norm_g[jb]), hd ** -0.5, f"fox_qnorm_{jb}").reshape(H, S, hd)
            qa = _augment(qn, kv["dcum"], True)
            qat = jnp.swapaxes(qa, 1, 2)
            o_tr, lse_r = _flash_fwd(kv["ka"], qat, kv["vat"], hd, f"fox_attn_{jb}")
            o_t = o_tr.transpose(2, 0, 1).reshape(S, D)
            og = _out_gate_fwd(o_t, qg, f"fox_gate_{jb}")
            y = _mm(og, (w_o, jb), "nn", F32, f"fox_out_{jb}")
            st.update(qg=qg, q_raw=q_raw, qa=qa, qat=qat, o_tr=o_tr, lse_r=lse_r, o_t=o_t, og=og)
        st["y"] = y
        x1 = _post_fwd(xc, y, row(post_mix_g[l]), g_m, f"post_mix_{l}")
        st["x1"] = x1
        h2 = _norm_mod_fwd(x1, row(pre_ffn_g[l]), sh_f, sc_f, f"pre_ffn_{l}")
        gu = _mm(h2, (w_gu, l), "nn", BF16, f"ffn_gu_{l}")
        act = _swiglu_fwd(gu, f"ffn_act_{l}")
        y2 = _mm(act, (w_dn, l), "nn", F32, f"ffn_down_{l}")
        xc = _post_fwd(x1, y2, row(post_ffn_g[l]), g_f, f"post_ffn_{l}")
        st.update(h2=h2, gu=gu, act=act, y2=y2)
        saved.append(st)
        if l == NA - 1:
            hk = _norm_mod_fwd(xc, row(kv_norm_g), kv_sh, kv_sc, "kv_pre")
            kvf = _mm(hk, w_kv, "nn", F32, "kv_proj")
            k_raw = heads(kvf[:, :D]).reshape(H * S, hd)
            kn = _head_norm_fwd(k_raw, row(k_norm_g), 1.0, "kv_knorm").reshape(H, S, hd)
            vb = heads(kvf[:, D:2 * D]).astype(BF16)
            f_t = kvf[:, 2 * D:2 * D + H].T
            b_col = kv_b_f.reshape(H, 1)
            dcum = _dcum_fwd(f_t, b_col, "kv_dcum")
            vt = kvf[:, D:2 * D].astype(BF16).reshape(S, H, hd).transpose(1, 2, 0)
            vat = jnp.where(lax.broadcasted_iota(jnp.int32, (1, LANES, 1), 1) == hd, jnp.asarray(1, BF16),
                            jnp.pad(vt, ((0, 0), (0, LANES - hd), (0, 0))))
            ka = _augment(kn, dcum, False)
            kv = dict(x=xc, hk=hk, k_raw=k_raw, ka=ka, kat=jnp.swapaxes(ka, 1, 2), vb=vb, vat=vat,
                      f_t=f_t, b_col=b_col, dcum=dcum)

    dx, loss_part = _loss_bwd(xc, tgt, "loss")
    loss = lax.psum(loss_part[0, 0], ("x", "y", "c"))

    gl = {n: [None] * weights[n].shape[0] for n in
          ["pre_mix_g", "post_mix_g", "pre_ffn_g", "post_ffn_g", "ffn_w_gu", "ffn_w_down", "a_w_in", "a_b_in", "a_ln_g",
           "a_ln_b", "a_w_s", "a_b_s", "a_w_out", "b_w_qg", "b_q_norm_g", "b_w_o"]}
    dmod = [None] * L
    dkn = dvb = ddc = None
    gkv = {}
    for l in reversed(range(L)):
        st = saved[l]
        sh_m, sc_m, g_m, sh_f, sc_f, g_f = mod[l]
        if l == NA - 1:
            dk_raw, gkv["k_norm_g"] = _head_norm_bwd(dkn.reshape(H * S, hd), kv["k_raw"], row(k_norm_g), 1.0, "kv_knorm_bwd")
            df_t, db_f = _dcum_bwd(ddc.reshape(H, S), kv["f_t"], kv["b_col"], "kv_dcum_bwd")
            dkvf = jnp.concatenate([unheads(dk_raw.reshape(H, S, hd)), unheads(dvb), df_t.T,
                                    jnp.zeros((S, kv_pad - (2 * D + H)), F32)], axis=1).astype(BF16)
            gkv["kv_w"] = _mm(kv["hk"], dkvf, "tn", BF16, "kv_proj_dw")[:, :2 * D + H]
            dhk = _mm(dkvf, w_kv, "nt", F32, "kv_proj_dx")
            dx, gkv["kv_norm_g"], dsh, dsc = _norm_mod_bwd(dx, dhk, kv["x"], row(kv_norm_g), kv_sh, kv_sc, "kv_pre_bwd")
            gkv["kv_b_f"] = db_f.reshape(H)
            dmod_kv = jnp.concatenate([dsh, dsc], axis=1)
        dy2, gl["post_ffn_g"][l], dg_f = _post_bwd(dx, st["y2"], row(post_ffn_g[l]), g_f, f"post_ffn_bwd_{l}")
        gl["ffn_w_down"][l] = _mm(st["act"], dy2, "tn", BF16, f"ffn_down_dw_{l}")
        dact = _mm(dy2, (w_dn, l), "nt", BF16, f"ffn_down_dx_{l}")
        dgu = _swiglu_bwd(dact, st["gu"], f"ffn_act_bwd_{l}")
        gl["ffn_w_gu"][l] = _mm(st["h2"], dgu, "tn", BF16, f"ffn_gu_dw_{l}")
        dh2 = _mm(dgu, (w_gu, l), "nt", F32, f"ffn_gu_dx_{l}")
        dx, gl["pre_ffn_g"][l], dsh_f, dsc_f = _norm_mod_bwd(dx, dh2, st["x1"], row(pre_ffn_g[l]), sh_f, sc_f, f"pre_ffn_bwd_{l}")
        dy, gl["post_mix_g"][l], dg_m = _post_bwd(dx, st["y"], row(post_mix_g[l]), g_m, f"post_mix_bwd_{l}")
        if l < NA:
            gl["a_w_out"][l] = _mm(st["yg"], dy, "tn", BF16, f"gmlp_out_dw_{l}")
            dyg = _mm(dy, (w_out, l), "nt", BF16, f"gmlp_out_dx_{l}")
            dzp, db_in, dlg, dlb, dws, dbs_t = _gmlp_bwd(dyg, st["zp"], row(b_in_f[l]), row(ln_g_f[l]), row(ln_b_f[l]),
                                                           ws_m[l], ws_mt[l], bs_t[l], f"gmlp_gate_bwd_{l}")
            gl["a_b_in"][l], gl["a_ln_g"][l], gl["a_ln_b"][l] = db_in[0], dlg[0], dlb[0]
            gl["a_w_s"][l], gl["a_b_s"][l] = dws * causal, dbs_t.T
            gl["a_w_in"][l] = _mm(st["h1"], dzp, "tn", BF16, f"gmlp_in_dw_{l}")
            dh1 = _mm(dzp, (w_in, l), "nt", F32, f"gmlp_in_dx_{l}")
        else:
            jb = l - NA
            gl["b_w_o"][jb] = _mm(st["og"], dy, "tn", BF16, f"fox_out_dw_{jb}")
            dog = _mm(dy, (w_o, jb), "nt", F32, f"fox_out_dx_{jb}")
            do_t, dgl = _out_gate_bwd(dog, st["o_t"], st["qg"], f"fox_gate_bwd_{jb}")
            do = heads(do_t)
            dot = do_t.reshape(S, H, hd).transpose(1, 2, 0)
            dqa_tr, dk_j, dv_j, dd_k, dd_q = _flash_bwd(kv["ka"], kv["kat"], st["qa"], st["qat"], kv["vb"], do, dot,
                                                        st["o_tr"], st["lse_r"], f"fox_attn_bwd_{jb}")
            dqn = dqa_tr[:, :, :hd, :].transpose(0, 1, 3, 2).reshape(H, S, hd)
            dk_j = dk_j[:, :, :hd]
            dd_j = dd_k.reshape(H, S) + dd_q.reshape(H, S)
            dkn = dk_j if dkn is None else dkn + dk_j
            dvb = dv_j if dvb is None else dvb + dv_j
            ddc = dd_j if ddc is None else ddc + dd_j
            dq_raw, dgq = _head_norm_bwd(dqn.reshape(H * S, hd), st["q_raw"], row(b_q_norm_g[jb]), hd ** -0.5, f"fox_qnorm_bwd_{jb}")
            gl["b_q_norm_g"][jb] = dgq[0]
            dqg = jnp.concatenate([unheads(dq_raw.reshape(H, S, hd)).astype(BF16), dgl], axis=1)
            gl["b_w_qg"][jb] = _mm(st["h1"], dqg, "tn", BF16, f"fox_qg_dw_{jb}")
            dh1 = _mm(dqg, (w_qg, jb), "nt", F32, f"fox_qg_dx_{jb}")
        dx, gl["pre_mix_g"][l], dsh_m, dsc_m = _norm_mod_bwd(dx, dh1, st["x0"], row(pre_mix_g[l]), sh_m, sc_m, f"pre_mix_bwd_{l}")
        dmod[l] = jnp.concatenate([dsh_m, dsc_m, dg_m, dsh_f, dsc_f, dg_f], axis=1)
    grad_x = dx[None]

    stack = lambda n: jnp.stack([t.reshape(weights[n].shape[1:]) for t in gl[n]])
    small = {"dmod": jnp.concatenate(dmod, axis=1), "dmod_kv": dmod_kv}
    for n in ["pre_mix_g", "post_mix_g", "pre_ffn_g", "post_ffn_g", "a_w_s", "a_b_s", "b_q_norm_g"]:
        small[n] = stack(n)
    for n in ["a_b_in", "a_ln_g", "a_ln_b"]:
        small[n] = jnp.stack(gl[n])
    for n in ["kv_norm_g", "kv_b_f", "k_norm_g"]:
        small[n] = gkv[n]
    sizes = {n: t.size for n, t in small.items()}
    flat = jnp.concatenate([t.reshape(-1).astype(F32) for t in small.values()])
    rows_small = -(-flat.size // (LANES * BF16_ROWS)) * BF16_ROWS
    flat = jnp.pad(flat, (0, rows_small * LANES - flat.size)).reshape(rows_small, LANES)
    flat_all = _own_slot(_gather8([flat], "gather_small")[0], flat, me)
    flat_sum = _sum_slots(flat_all, "sum_small").reshape(-1)
    offs, o_ = {}, 0
    for n, sz in sizes.items():
        offs[n] = o_
        o_ += sz
    take = lambda n, shape: flat_sum[offs[n]:offs[n] + sizes[n]].reshape(shape)
    dmod_rows = flat_all.reshape(N_DEV, -1)[:, offs["dmod"]:offs["dmod"] + sizes["dmod"] + sizes["dmod_kv"]]
    dmod_rows = jnp.pad(dmod_rows, ((0, BF16_ROWS - N_DEV), (0, 0)))

    grads = {}
    grads["ada_b"] = take("dmod", (L, 6 * D))
    grads["kv_ada_b"] = take("dmod_kv", (2 * D,))
    for n in ["pre_mix_g", "post_mix_g", "pre_ffn_g", "post_ffn_g", "a_w_s", "a_b_s", "b_q_norm_g", "kv_norm_g", "kv_b_f", "k_norm_g"]:
        grads[n] = take(n, weights[n].shape)
    for n in ["a_b_in", "a_ln_g", "a_ln_b"]:
        full = take(n, small[n].shape)
        w = weights[n].shape[1]
        grads[n] = lax.dynamic_slice_in_dim(full, chip * w, w, axis=1)
    ada_g = []
    for l in range(L):
        cols = lax.dynamic_slice_in_dim(dmod_rows[:, l * 6 * D:(l + 1) * 6 * D], chip * ada_cols, ada_cols, axis=1)
        ada_g.append(_mm(c_act, cols, "tn", F32, f"mod_proj_dw_{l}"))
    grads["ada_w"] = jnp.stack(ada_g)
    cols = lax.dynamic_slice_in_dim(dmod_rows[:, L * 6 * D:], chip * kvada_cols, kvada_cols, axis=1)
    grads["kv_ada_w"] = _mm(c_act, cols, "tn", F32, "mod_proj_kv_dw")

    specs = {"ffn_w_gu": (2, 0), "ffn_w_down": (1, 0), "a_w_in": (2, 0), "a_w_out": (1, 0), "kv_w": (0, 1),
             "b_w_qg": (2, 0), "b_w_o": (1, 0)}
    full_g = {n: jnp.stack(gl[n]) for n in big if n != "kv_w"}
    full_g["kv_w"] = gkv["kv_w"].reshape(D, N_CHIPS, kv_cols).transpose(1, 0, 2)
    from_core = _sibling_scatter([full_g[n] for n in big], [specs[n] for n in big], "scatter_g_core")
    chip_sums = [_sum_pair(_own_pieces(full_g[n], specs[n], ci), r, f"sum_g_core_{n}") for n, r in zip(big, from_core)]
    recv = _chip_scatter(chip_sums, "scatter_g_chip")
    recv = [_own_slot(r, lax.dynamic_index_in_dim(p, chip, 0, keepdims=False), chip) for r, p in zip(recv, chip_sums)]
    halves = [_sum_slots(r, f"sum_g_{n}") for n, r in zip(big, recv)]
    pairs = _sibling_pair(halves, "pair_g")
    for n, p, hlf in zip(big, pairs, halves):
        grads[n] = _own_slot(p, hlf, ci).reshape(weights[n].shape)

    outs_d, outs_m, outs_v = {}, {}, {}
    for n in names:
        w2 = weights[n] if weights[n].ndim > 1 else weights[n].reshape(1, -1)
        shp = w2.shape
        d_, m_, v_ = _adamw(w2, grads[n].reshape(shp), m_in[n].reshape(shp), v_in[n].reshape(shp), f"adamw_{n}")
        outs_d[n], outs_m[n], outs_v[n] = (t.reshape(weights[n].shape) for t in (d_, m_, v_))
    return (loss, grad_x, *[grads[n] for n in names], *[outs_d[n] for n in names],
            *[outs_m[n] for n in names], *[outs_v[n] for n in names])
```

```python
import functools

import jax
import jax.numpy as jnp
from jax import lax
from jax.experimental import pallas as pl
from jax.experimental.pallas import tpu as pltpu

F32 = jnp.float32
BF16 = jnp.bfloat16
MESH = pl.DeviceIdType.MESH
NORM_EPS = 1e-6
MASKED = -1e30
LANES = 128
BF16_ROWS = 16
ROW_BLOCK_BYTES = 12 << 20
ADAM_LR, ADAM_B1, ADAM_B2, ADAM_EPS, ADAM_WD, ADAM_STEP = 0.001, 0.9, 0.999, 1e-08, 0.01, 10
N_CHIPS, N_CORES, N_DEV = 4, 2, 8
ATTN_HEADS_PER_STEP = 4
ATTN_STAGED_HEADS = 2
ANY = pl.BlockSpec(memory_space=pl.ANY)


def _tile(n, cap, quantum):
    best = None
    d = quantum
    while d <= min(n, cap):
        if n % d == 0:
            best = d
        d += quantum
    return n if best is None else best


def _call(body, *, name, out_shape, grid=(), in_specs=None, out_specs=None, scratch=(), sem=None, aliases=None):
    params = {} if sem is None else {"dimension_semantics": sem}
    return pl.pallas_call(
        body, name=name, grid=grid, in_specs=in_specs, out_specs=out_specs, out_shape=out_shape,
        scratch_shapes=list(scratch), input_output_aliases=aliases or {},
        compiler_params=pltpu.CompilerParams(**params))


def _call_prefetch(body, *, name, out_shape, grid, n_prefetch, in_specs, out_specs, scratch, sem):
    spec = pltpu.PrefetchScalarGridSpec(num_scalar_prefetch=n_prefetch, grid=grid, in_specs=in_specs,
                                        out_specs=out_specs, scratch_shapes=list(scratch))
    return pl.pallas_call(
        body, name=name, grid_spec=spec, out_shape=out_shape,
        compiler_params=pltpu.CompilerParams(dimension_semantics=sem))


def _place():
    x, y, c = lax.axis_index("x"), lax.axis_index("y"), lax.axis_index("c")
    return x, y, c


def _mm(a, b, mode, out_dtype, name):
    b_arr, b_idx = b if isinstance(b, tuple) else (b, None)
    bs = b_arr.shape[-2:]
    if mode == "nn":
        (M, K), (K2, N) = a.shape, bs
        dims = (((1,), (0,)), ((), ()))
    elif mode == "nt":
        (M, K), (N, K2) = a.shape, bs
        dims = (((1,), (1,)), ((), ()))
    else:
        (K, M), (K2, N) = a.shape, bs
        dims = (((0,), (0,)), ((), ()))
    assert K == K2, (name, a.shape, b_arr.shape)
    if mode == "tn":
        tm = _tile(M, 1408, LANES)
        tk = _tile(K, 2048, BF16_ROWS)
        tn = _tile(N, 512, LANES)
    else:
        tm = _tile(M, 1024, BF16_ROWS)
        tk = K if K <= 2816 else _tile(K, 2816, LANES)
        tn = _tile(N, 1408 if tk <= 1024 else 512, LANES)
    if tn < 256:
        tn = N
        tm = _tile(M, 512, LANES if mode == "tn" else BF16_ROWS)
    nk = K // tk
    grid = (M // tm, N // tn, nk)

    if mode == "tn":
        a_spec = pl.BlockSpec((tk, tm), lambda i, j, k: (k, i))
    else:
        a_spec = pl.BlockSpec((tm, tk), lambda i, j, k: (i, k))
    if mode == "nt":
        b_blk, b_map = (tn, tk), (lambda i, j, k: (j, k))
    else:
        b_blk, b_map = (tk, tn), (lambda i, j, k: (k, j))
    if b_idx is None:
        b_spec = pl.BlockSpec(b_blk, b_map)
    else:
        b_spec = pl.BlockSpec((None,) + b_blk, lambda i, j, k: (b_idx,) + b_map(i, j, k))

    def body(a_ref, b_ref, o_ref, *acc):
        r = lax.dot_general(a_ref[...].astype(BF16), b_ref[...].astype(BF16), dims, preferred_element_type=F32)
        if nk == 1:
            o_ref[...] = r.astype(o_ref.dtype)
        else:
            k = pl.program_id(2)

            @pl.when(k == 0)
            def _():
                acc[0][...] = r

            @pl.when(k > 0)
            def _():
                acc[0][...] += r

            @pl.when(k == nk - 1)
            def _():
                o_ref[...] = acc[0][...].astype(o_ref.dtype)

    return _call(
        body, name=name, grid=grid, in_specs=[a_spec, b_spec],
        out_specs=pl.BlockSpec((tm, tn), lambda i, j, k: (i, j)),
        out_shape=jax.ShapeDtypeStruct((M, N), out_dtype),
        scratch=[pltpu.VMEM((tm, tn), F32)] if nk > 1 else [],
        sem=("parallel", "parallel", "arbitrary"))(a, b_arr)


def _rowwise(fn, rows, pars, outs, pouts, name):
    R = rows[0].shape[0]
    row_bytes = 4 * (sum(max(r.shape[1], LANES) for r in rows) + sum(max(c, LANES) for c, _ in outs))
    tb = _tile(R, max(BF16_ROWS, ROW_BLOCK_BYTES // row_bytes), BF16_ROWS)
    nr, npar, no = len(rows), len(pars), len(outs)

    def body(*refs):
        r_in, p_in = refs[:nr], refs[nr:nr + npar]
        r_out, p_out = refs[nr + npar:nr + npar + no], refs[nr + npar + no:]
        ro, po = fn([r[...] for r in r_in], [p[...] for p in p_in])
        for ref, val in zip(r_out, ro):
            if isinstance(val, (tuple, list)):
                off = 0
                for piece in val:
                    w = piece.shape[1]
                    ref[:, off:off + w] = piece.astype(ref.dtype)
                    off += w
            else:
                ref[...] = val.astype(ref.dtype)
        if p_out:
            first = pl.program_id(0) == 0

            @pl.when(first)
            def _():
                for ref, val in zip(p_out, po):
                    ref[...] = val

            @pl.when(jnp.logical_not(first))
            def _():
                for ref, val in zip(p_out, po):
                    ref[...] += val

    res = _call(
        body, name=name, grid=(R // tb,),
        in_specs=[pl.BlockSpec((tb, r.shape[1]), lambda i: (i, 0)) for r in rows]
        + [pl.BlockSpec(p.shape, lambda i: (0, 0)) for p in pars],
        out_specs=[pl.BlockSpec((tb, c), lambda i: (i, 0)) for c, _ in outs]
        + [pl.BlockSpec(s, lambda i: (0, 0)) for s in pouts],
        out_shape=[jax.ShapeDtypeStruct((R, c), dt) for c, dt in outs]
        + [jax.ShapeDtypeStruct(s, F32) for s in pouts],
        sem=("arbitrary",) if pouts else ("parallel",))(*rows, *pars)
    return list(res)


def _rms(x, g):
    return x * lax.rsqrt(jnp.mean(x * x, axis=-1, keepdims=True) + NORM_EPS) * g


def _norm_mod(x, g, sh, sc):
    return _rms(x, g) * (1.0 + sc) + sh


def _gated_post(y, g, gate):
    return gate * _rms(y, g)


def _norm_mod_fwd(x, g, sh, sc, name):
    return _rowwise(lambda r, p: ([_norm_mod(r[0], *p)], []), [x], [g, sh, sc], [(x.shape[1], BF16)], [], name)[0]


def _norm_mod_bwd(dxo, dh, x, g, sh, sc, name):
    def fn(r, p):
        _, vjp = jax.vjp(_norm_mod, r[2], *p)
        dx, dg, dsh, dsc = vjp(r[1].astype(F32))
        return [r[0] + dx], [dg, dsh, dsc]
    c = x.shape[1]
    return _rowwise(fn, [dxo, dh, x], [g, sh, sc], [(c, F32)], [(1, c)] * 3, name)


def _post_fwd(x, y, g, gate, name):
    return _rowwise(lambda r, p: ([r[0] + _gated_post(r[1].astype(F32), *p)], []), [x, y], [g, gate],
                    [(x.shape[1], F32)], [], name)[0]


def _post_bwd(dxo, y, g, gate, name):
    def fn(r, p):
        _, vjp = jax.vjp(_gated_post, r[1].astype(F32), *p)
        dy, dg, dgate = vjp(r[0])
        return [dy], [dg, dgate]
    c = y.shape[1]
    return _rowwise(fn, [dxo, y], [g, gate], [(c, BF16)], [(1, c)] * 2, name)


def _swiglu(g, u):
    return jax.nn.silu(g) * u


def _swiglu_fwd(gu, name):
    f = gu.shape[1] // 2
    return _rowwise(lambda r, p: ([_swiglu(r[0][:, :f].astype(F32), r[0][:, f:].astype(F32))], []), [gu], [],
                    [(f, BF16)], [], name)[0]


def _swiglu_bwd(da, gu, name):
    f = gu.shape[1] // 2

    def fn(r, p):
        _, vjp = jax.vjp(_swiglu, r[1][:, :f].astype(F32), r[1][:, f:].astype(F32))
        return [vjp(r[0].astype(F32))], []
    return _rowwise(fn, [da, gu], [], [(2 * f, BF16)], [], name)[0]


def _silu_rows(c, name):
    return _rowwise(lambda r, p: ([jax.nn.silu(r[0])], []), [c], [], [(c.shape[1], F32)], [], name)[0]


def _head_norm(x, g, scale):
    return _rms(x, g) * scale


def _head_norm_fwd(x, g, scale, name):
    return _rowwise(lambda r, p: ([_head_norm(r[0].astype(F32), p[0], scale)], []), [x], [g],
                    [(x.shape[1], BF16)], [], name)[0]


def _head_norm_bwd(dy, x, g, scale, name):
    def fn(r, p):
        _, vjp = jax.vjp(lambda t, gg: _head_norm(t, gg, scale), r[1].astype(F32), p[0])
        dx, dg = vjp(r[0])
        return [dx], [dg]
    c = x.shape[1]
    return _rowwise(fn, [dy, x], [g], [(c, F32)], [(1, c)], name)


def _out_gate_fwd(o, qg, name):
    d = o.shape[1]
    return _rowwise(lambda r, p: ([r[0] * jax.nn.sigmoid(r[1][:, d:].astype(F32))], []), [o, qg], [],
                    [(d, BF16)], [], name)[0]


def _out_gate_bwd(dog, o, qg, name):
    d = o.shape[1]

    def fn(r, p):
        _, vjp = jax.vjp(lambda oo, gl: oo * jax.nn.sigmoid(gl), r[1], r[2][:, d:].astype(F32))
        do, dgl = vjp(r[0])
        return [do, dgl], []
    return _rowwise(fn, [dog, o, qg], [], [(d, BF16), (d, BF16)], [], name)


def _loss_bwd(y, tgt, name):
    n = y.shape[1]

    def fn(r, p):
        e = r[0] - r[1]
        part = jnp.sum(jnp.sum(e * e, axis=1, keepdims=True), axis=0, keepdims=True) * (0.5 / n)
        return [e * (1.0 / n)], [part]
    return _rowwise(fn, [y, tgt], [], [(n, F32)], [(1, 1)], name)


def _adamw(w, g, m, v, name):
    shape = w.shape
    c = shape[-1]
    flat = [t.reshape(-1, c) for t in (w, g, m, v)]

    def fn(r, p):
        w_, g_, m_, v_ = r
        m2 = ADAM_B1 * m_ + (1.0 - ADAM_B1) * g_
        v2 = ADAM_B2 * v_ + (1.0 - ADAM_B2) * (g_ * g_)
        m_hat = m2 / (1.0 - ADAM_B1 ** ADAM_STEP)
        v_hat = v2 / (1.0 - ADAM_B2 ** ADAM_STEP)
        delta = -ADAM_LR * (m_hat / (jnp.sqrt(v_hat) + ADAM_EPS) + ADAM_WD * w_)
        return [delta, m2, v2], []
    res = _rowwise(fn, flat, [], [(c, F32)] * 3, [], name)
    return [t.reshape(shape) for t in res]


def _sum_pair(a, b, name):
    c = a.shape[-1]
    out = _rowwise(lambda r, p: ([r[0].astype(F32) + r[1].astype(F32)], []), [a.reshape(-1, c), b.reshape(-1, c)], [],
                   [(c, BF16)], [], name)[0]
    return out.reshape(a.shape)


def _sum_slots(recv, name, out_dtype=F32):
    n = recv.shape[0]
    shape = recv.shape[1:]
    c = shape[-1]
    r3 = recv.reshape(n, -1, c)
    rows = r3.shape[1]
    tb = _tile(rows, max(BF16_ROWS, ROW_BLOCK_BYTES // (4 * c * (n + 1))), BF16_ROWS)

    def body(r_ref, o_ref):
        acc = r_ref[0].astype(F32)
        for s in range(1, n):
            acc = acc + r_ref[s].astype(F32)
        o_ref[...] = acc.astype(o_ref.dtype)

    out = _call(body, name=name, grid=(rows // tb,),
                in_specs=[pl.BlockSpec((n, tb, c), lambda i: (0, i, 0))],
                out_specs=pl.BlockSpec((tb, c), lambda i: (i, 0)),
                out_shape=jax.ShapeDtypeStruct((rows, c), out_dtype), sem=("parallel",))(r3)
    return out.reshape(shape)


def _gmlp_pre(zu, zv, b_u, b_v, ln_g, ln_b):
    u = jax.nn.gelu(zu + b_u, approximate=True)
    v = jax.nn.gelu(zv + b_v, approximate=True)
    xc = v - jnp.mean(v, axis=-1, keepdims=True)
    vn = xc * lax.rsqrt(jnp.mean(xc * xc, axis=-1, keepdims=True) + NORM_EPS) * ln_g + ln_b
    return u, vn


def _gmlp_fwd(zp, b_in, ln_g, ln_b, ws, bs_t, name):
    S, gw2 = zp.shape
    gw = gw2 // 2
    G, ch, _ = ws.shape
    gd = gw // G
    tb = 2 * ch

    def body(zp_ref, bin_ref, lg_ref, lb_ref, ws_ref, bs_ref, o_ref):
        u, vn = _gmlp_pre(zp_ref[:, :gw].astype(F32), zp_ref[:, gw:].astype(F32), bin_ref[:, :gw], bin_ref[:, gw:],
                          lg_ref[...], lb_ref[...])
        vnb = vn.astype(BF16)
        for c in range(tb // ch):
            for g in range(G):
                rs, cs = slice(c * ch, (c + 1) * ch), slice(g * gd, (g + 1) * gd)
                vv = jnp.dot(ws_ref[g], vnb[rs, cs], preferred_element_type=F32) + bs_ref[:, g:g + 1]
                o_ref[rs, cs] = (u[rs, cs] * vv).astype(o_ref.dtype)

    full = lambda a: pl.BlockSpec(a.shape, lambda i: (0,) * a.ndim)
    return _call(body, name=name, grid=(S // tb,),
                 in_specs=[pl.BlockSpec((tb, gw2), lambda i: (i, 0)), full(b_in), full(ln_g), full(ln_b), full(ws), full(bs_t)],
                 out_specs=pl.BlockSpec((tb, gw), lambda i: (i, 0)),
                 out_shape=jax.ShapeDtypeStruct((S, gw), BF16), sem=("parallel",))(zp, b_in, ln_g, ln_b, ws, bs_t)


def _gmlp_bwd(dyg, zp, b_in, ln_g, ln_b, ws, ws_t, bs_t, name):
    S, gw2 = zp.shape
    gw = gw2 // 2
    G, ch, _ = ws.shape
    gd = gw // G
    tb = 2 * ch

    def body(dy_ref, zp_ref, bin_ref, lg_ref, lb_ref, ws_ref, wst_ref, bs_ref,
             dzp_ref, dbin_ref, dlg_ref, dlb_ref, dws_ref, dbs_ref, du_sc, dvn_sc):
        (u, vn), vjp = jax.vjp(_gmlp_pre, zp_ref[:, :gw].astype(F32), zp_ref[:, gw:].astype(F32), bin_ref[:, :gw],
                               bin_ref[:, gw:], lg_ref[...], lb_ref[...])
        vnb = vn.astype(BF16)
        first = pl.program_id(0) == 0

        @pl.when(first)
        def _():
            dws_ref[...] = jnp.zeros_like(dws_ref)

        lane = lax.broadcasted_iota(jnp.int32, (ch, G), 1)
        dbs = jnp.zeros((ch, G), F32)
        for g in range(G):
            cs = slice(g * gd, (g + 1) * gd)
            dws_g = jnp.zeros((ch, ch), F32)
            col = jnp.zeros((ch, 1), F32)
            for c in range(tb // ch):
                rs = slice(c * ch, (c + 1) * ch)
                vnp = vnb[rs, cs]
                vv = jnp.dot(ws_ref[g], vnp, preferred_element_type=F32) + bs_ref[:, g:g + 1]
                dy = dy_ref[rs, cs].astype(F32)
                du_sc[rs, cs] = dy * vv
                dvv = dy * u[rs, cs]
                dvvb = dvv.astype(BF16)
                dvn_sc[rs, cs] = jnp.dot(wst_ref[g], dvvb, preferred_element_type=F32)
                dws_g = dws_g + lax.dot_general(dvvb, vnp, (((1,), (1,)), ((), ())), preferred_element_type=F32)
                col = col + jnp.sum(dvv, axis=1, keepdims=True)
            dws_ref[g] += dws_g
            dbs = jnp.where(lane == g, col, dbs)
        dzu, dzv, dbu, dbv, dlg, dlb = vjp((du_sc[...], dvn_sc[...]))
        dzp_ref[:, :gw] = dzu.astype(dzp_ref.dtype)
        dzp_ref[:, gw:] = dzv.astype(dzp_ref.dtype)

        @pl.when(first)
        def _():
            dbin_ref[:, :gw] = dbu
            dbin_ref[:, gw:] = dbv
            dlg_ref[...] = dlg
            dlb_ref[...] = dlb
            dbs_ref[...] = dbs

        @pl.when(jnp.logical_not(first))
        def _():
            dbin_ref[:, :gw] += dbu
            dbin_ref[:, gw:] += dbv
            dlg_ref[...] += dlg
            dlb_ref[...] += dlb
            dbs_ref[...] += dbs

    full = lambda a: pl.BlockSpec(a.shape, lambda i: (0,) * a.ndim)
    fshape = lambda s: pl.BlockSpec(s, lambda i: (0,) * len(s))
    return _call(
        body, name=name, grid=(S // tb,),
        in_specs=[pl.BlockSpec((tb, gw), lambda i: (i, 0)), pl.BlockSpec((tb, gw2), lambda i: (i, 0)),
                  full(b_in), full(ln_g), full(ln_b), full(ws), full(ws_t), full(bs_t)],
        out_specs=[pl.BlockSpec((tb, gw2), lambda i: (i, 0)), fshape((1, gw2)), fshape((1, gw)), fshape((1, gw)),
                   fshape((G, ch, ch)), fshape((ch, G))],
        out_shape=[jax.ShapeDtypeStruct((S, gw2), BF16), jax.ShapeDtypeStruct((1, gw2), F32),
                   jax.ShapeDtypeStruct((1, gw), F32), jax.ShapeDtypeStruct((1, gw), F32),
                   jax.ShapeDtypeStruct((G, ch, ch), F32), jax.ShapeDtypeStruct((ch, G), F32)],
        scratch=[pltpu.VMEM((tb, gw), F32), pltpu.VMEM((tb, gw), F32)],
        sem=("arbitrary",))(dyg, zp, b_in, ln_g, ln_b, ws, ws_t, bs_t)


def _dot_01(x, ones_bf16):
    hi = x.astype(BF16)
    r1 = x - hi.astype(F32)
    mid = r1.astype(BF16)
    lo = (r1 - mid.astype(F32)).astype(BF16)
    dot = lambda t: jnp.dot(t, ones_bf16, preferred_element_type=F32)
    return dot(hi) + dot(mid) + dot(lo)


def _log_sigmoid(x):
    return jnp.minimum(x, 0.0) - jnp.log1p(jnp.exp(-jnp.abs(x)))


def _dcum_fwd(f_t, b_col, name):
    H, S = f_t.shape
    tb = _tile(S, 512, LANES)

    def body(f_ref, b_ref, o_ref, carry):
        @pl.when(pl.program_id(0) == 0)
        def _():
            carry[...] = jnp.zeros_like(carry)

        ls = _log_sigmoid(f_ref[...] + b_ref[...])
        r = lax.broadcasted_iota(jnp.int32, (tb, tb), 0)
        c = lax.broadcasted_iota(jnp.int32, (tb, tb), 1)
        upper = (r <= c).astype(BF16)
        o_ref[...] = _dot_01(ls, upper) + carry[...]
        carry[...] += jnp.sum(ls, axis=1, keepdims=True)

    return _call(body, name=name, grid=(S // tb,),
                 in_specs=[pl.BlockSpec((H, tb), lambda i: (0, i)), pl.BlockSpec((H, 1), lambda i: (0, 0))],
                 out_specs=pl.BlockSpec((H, tb), lambda i: (0, i)),
                 out_shape=jax.ShapeDtypeStruct((H, S), F32),
                 scratch=[pltpu.VMEM((H, 1), F32)], sem=("arbitrary",))(f_t, b_col)


def _dcum_bwd(dd_t, f_t, b_col, name):
    H, S = f_t.shape
    tb = _tile(S, 512, LANES)
    nb = S // tb

    def body(dd_ref, f_ref, b_ref, df_ref, db_ref, carry):
        first = pl.program_id(0) == 0

        @pl.when(first)
        def _():
            carry[...] = jnp.zeros_like(carry)

        dd = dd_ref[...]
        r = lax.broadcasted_iota(jnp.int32, (tb, tb), 0)
        c = lax.broadcasted_iota(jnp.int32, (tb, tb), 1)
        lower = (r >= c).astype(BF16)
        rev = _dot_01(dd, lower) + carry[...]
        carry[...] += jnp.sum(dd, axis=1, keepdims=True)
        df = rev * jax.nn.sigmoid(-(f_ref[...] + b_ref[...]))
        df_ref[...] = df
        part = jnp.sum(df, axis=1, keepdims=True)

        @pl.when(first)
        def _():
            db_ref[...] = part

        @pl.when(jnp.logical_not(first))
        def _():
            db_ref[...] += part

    return _call(body, name=name, grid=(nb,),
                 in_specs=[pl.BlockSpec((H, tb), lambda i: (0, nb - 1 - i)), pl.BlockSpec((H, tb), lambda i: (0, nb - 1 - i)),
                           pl.BlockSpec((H, 1), lambda i: (0, 0))],
                 out_specs=[pl.BlockSpec((H, tb), lambda i: (0, nb - 1 - i)), pl.BlockSpec((H, 1), lambda i: (0, 0))],
                 out_shape=[jax.ShapeDtypeStruct((H, S), F32), jax.ShapeDtypeStruct((H, 1), F32)],
                 scratch=[pltpu.VMEM((H, 1), F32)], sem=("arbitrary",))(dd_t, f_t, b_col)


def _attn_tile(S):
    return _tile(S, 512, LANES)


def _causal(t, transposed):
    r = lax.broadcasted_iota(jnp.int32, (t, t), 0)
    c = lax.broadcasted_iota(jnp.int32, (t, t), 1)
    return (r <= c) if transposed else (c <= r)


def _tri_pairs(n, key_major):
    if key_major:
        pairs = [(i, j) for j in range(n) for i in range(j, n)]
    else:
        pairs = [(i, j) for i in range(n) for j in range(i + 1)]
    return jnp.asarray([p[0] for p in pairs], jnp.int32), jnp.asarray([p[1] for p in pairs], jnp.int32)


def _split3(x):
    hi = lax.reduce_precision(x, 8, 7)
    r = x - hi
    mid = lax.reduce_precision(r, 8, 7)
    lo = lax.reduce_precision(r - mid, 8, 7)
    return hi.astype(BF16), mid.astype(BF16), lo.astype(BF16)


def _augment(xn, dcum, query):
    H, S, hd = xn.shape
    parts = list(_split3(dcum))
    vals = parts + [1.0] * 3 if query else [1.0] * 3 + [-p for p in parts]
    lane = lax.broadcasted_iota(jnp.int32, (1, 1, LANES), 2)
    out = jnp.pad(xn, ((0, 0), (0, 0), (0, LANES - hd)))
    for k, val in enumerate(vals):
        val = jnp.asarray(val, BF16)
        out = jnp.where(lane == hd + k, val[..., None] if val.ndim else val, out)
    return out


def _scores_t(k_ref, qt_ref, h, t, diag):
    st = jnp.dot(k_ref[h], qt_ref[h], preferred_element_type=F32)
    return jnp.where(_causal(t, True), st, MASKED) if diag else st


def _flash_fwd(ka, qat, vat, hd, name):
    H, S, da = ka.shape
    t = _attn_tile(S)
    hb = ATTN_HEADS_PER_STEP
    it, jt = _tri_pairs(S // t, False)

    def body(it_ref, jt_ref, k_ref, qt_ref, vt_ref, o_ref, lse_ref, m_sc, acc_sc):
        i, j = it_ref[pl.program_id(1)], jt_ref[pl.program_id(1)]

        @pl.when(j == 0)
        def _():
            m_sc[...] = jnp.full_like(m_sc, MASKED)
            acc_sc[...] = jnp.zeros_like(acc_sc)

        def step(diag):
            sts = [_scores_t(k_ref, qt_ref, h, t, diag) for h in range(hb)]
            pts, alphas = [], []
            for h in range(hb):
                m_prev = m_sc[h]
                m_new = jnp.maximum(m_prev, jnp.max(sts[h], axis=0, keepdims=True))
                pts.append(jnp.exp(sts[h] - m_new).astype(BF16))
                alphas.append(jnp.exp(m_prev - m_new))
                m_sc[h] = m_new
            for h in range(hb):
                acc_sc[h] = alphas[h] * acc_sc[h] + jnp.dot(vt_ref[h], pts[h], preferred_element_type=F32)

        @pl.when(j < i)
        def _():
            step(False)

        @pl.when(j == i)
        def _():
            step(True)
            for h in range(hb):
                l = acc_sc[h, hd:hd + 1, :]
                o_ref[h] = acc_sc[h, :hd, :] / l
                lse_ref[h] = m_sc[h] + jnp.log(l)

    qcol = lambda h, p, it_, jt_: (h, 0, it_[p])
    kcol = lambda h, p, it_, jt_: (h, 0, jt_[p])
    krow = lambda h, p, it_, jt_: (h, jt_[p], 0)
    return _call_prefetch(
        body, name=name, grid=(H // hb, it.shape[0]), n_prefetch=2,
        in_specs=[pl.BlockSpec((hb, t, da), krow), pl.BlockSpec((hb, da, t), qcol), pl.BlockSpec((hb, da, t), kcol)],
        out_specs=[pl.BlockSpec((hb, hd, t), qcol), pl.BlockSpec((hb, 1, t), qcol)],
        out_shape=[jax.ShapeDtypeStruct((H, hd, S), F32), jax.ShapeDtypeStruct((H, 1, S), F32)],
        scratch=[pltpu.VMEM((hb, 1, t), F32), pltpu.VMEM((hb, da, t), F32)],
        sem=("parallel", "arbitrary"))(it, jt, ka, qat, vat)


def _flash_bwd(ka, kat, qa, qat, v, do, dot, o_tr, lse_r, name):
    H, S, hd = v.shape
    da = ka.shape[2]
    t = _attn_tile(S)
    n = S // t
    hb = ATTN_HEADS_PER_STEP
    it, jt = _tri_pairs(n, True)

    def body(it_ref, jt_ref, k_ref, kt_ref, q_ref, qt_ref, v_ref, do_ref, dot_ref, o_ref, lse_ref,
             dq_ref, dk_ref, dv_ref, dd_ref, ddq_ref, dk_sc, dv_sc, dd_sc):
        i, j = it_ref[pl.program_id(1)], jt_ref[pl.program_id(1)]

        @pl.when(pl.program_id(1) == 0)
        def _():
            ddq_ref[...] = jnp.zeros_like(ddq_ref)
            dq_ref[...] = jnp.zeros_like(dq_ref)

        def step(diag):
            for h0 in range(0, hb, ATTN_STAGED_HEADS):
                hs = range(h0, min(h0 + ATTN_STAGED_HEADS, hb))
                sts = [_scores_t(k_ref, qt_ref, h, t, diag) for h in hs]
                dpts = [jnp.dot(v_ref[h], dot_ref[h], preferred_element_type=F32) for h in hs]
                tiles = []
                for h, st, dpt in zip(hs, sts, dpts):
                    dl = jnp.sum(dot_ref[h].astype(F32) * o_ref[h], axis=0, keepdims=True)
                    pt = jnp.exp(st - lse_ref[h])
                    tiles.append((pt.astype(BF16), pt * (dpt - dl)))
                for h, (ptb, dst) in zip(hs, tiles):
                    dsb = dst.astype(BF16)
                    dv_sc[h] += jnp.dot(ptb, do_ref[h], preferred_element_type=F32)
                    dk_sc[h] += jnp.dot(dsb, q_ref[h], preferred_element_type=F32)
                    dq_ref[h, i] += jnp.dot(kt_ref[h], dsb, preferred_element_type=F32)
                    part = dst[:, :LANES]
                    for c in range(1, t // LANES):
                        part = part + dst[:, c * LANES:(c + 1) * LANES]
                    dd_sc[h] += part
                    ddq_ref[h, i] += jnp.sum(dst, axis=0, keepdims=True)

        @pl.when(i == j)
        def _():
            dk_sc[...] = jnp.zeros_like(dk_sc)
            dv_sc[...] = jnp.zeros_like(dv_sc)
            dd_sc[...] = jnp.zeros_like(dd_sc)
            step(True)

        @pl.when(i > j)
        def _():
            step(False)

        @pl.when(i == n - 1)
        def _():
            dk_ref[...] = dk_sc[...]
            dv_ref[...] = dv_sc[...]
            for h in range(hb):
                dd_ref[h] = -jnp.sum(dd_sc[h], axis=1, keepdims=True)

    krow = lambda h, p, it_, jt_: (h, jt_[p], 0)
    kcol = lambda h, p, it_, jt_: (h, 0, jt_[p])
    qrow = lambda h, p, it_, jt_: (h, it_[p], 0)
    qcol = lambda h, p, it_, jt_: (h, 0, it_[p])
    whole = lambda h, p, it_, jt_: (h, 0, 0, 0)
    return _call_prefetch(
        body, name=name, grid=(H // hb, it.shape[0]), n_prefetch=2,
        in_specs=[pl.BlockSpec((hb, t, da), krow), pl.BlockSpec((hb, da, t), kcol), pl.BlockSpec((hb, t, da), qrow),
                  pl.BlockSpec((hb, da, t), qcol), pl.BlockSpec((hb, t, hd), krow), pl.BlockSpec((hb, t, hd), qrow),
                  pl.BlockSpec((hb, hd, t), qcol), pl.BlockSpec((hb, hd, t), qcol), pl.BlockSpec((hb, 1, t), qcol)],
        out_specs=[pl.BlockSpec((hb, n, da, t), whole), pl.BlockSpec((hb, t, da), krow), pl.BlockSpec((hb, t, hd), krow),
                   pl.BlockSpec((hb, t, 1), krow), pl.BlockSpec((hb, n, 1, t), whole)],
        out_shape=[jax.ShapeDtypeStruct((H, n, da, t), F32), jax.ShapeDtypeStruct((H, S, da), F32),
                   jax.ShapeDtypeStruct((H, S, hd), F32), jax.ShapeDtypeStruct((H, S, 1), F32),
                   jax.ShapeDtypeStruct((H, n, 1, t), F32)],
        scratch=[pltpu.VMEM((hb, t, da), F32), pltpu.VMEM((hb, t, hd), F32), pltpu.VMEM((hb, t, LANES), F32)],
        sem=("parallel", "arbitrary"))(it, jt, ka, kat, qa, qat, v, do, dot, o_tr, lse_r)


def _offsets(n_bits):
    return [tuple((k >> b) & 1 for b in reversed(range(n_bits))) for k in range(1, 1 << n_bits)]


def _own_slot(out, own, idx):
    return lax.dynamic_update_index_in_dim(out, own.astype(out.dtype), idx, 0)


def _gather8(arrs, name):
    n = len(arrs)
    offs = _offsets(3)

    def body(*refs):
        ins, outs = refs[:n], refs[n:2 * n]
        ssem, rsem = refs[2 * n:]
        x, y, c = _place()
        me = 4 * x + 2 * y + c
        copies = []
        for a in range(n):
            for k, (dx, dy, dcc) in enumerate(offs):
                cp = pltpu.make_async_remote_copy(
                    src_ref=ins[a], dst_ref=outs[a].at[me], send_sem=ssem.at[a, k], recv_sem=rsem.at[a, k],
                    device_id=((x + dx) % 2, (y + dy) % 2, (c + dcc) % 2), device_id_type=MESH)
                cp.start()
                copies.append(cp)
        for cp in copies:
            cp.wait()

    return _call(body, name=name, in_specs=[ANY] * n, out_specs=[ANY] * n,
                 out_shape=[jax.ShapeDtypeStruct((N_DEV,) + a.shape, a.dtype) for a in arrs],
                 scratch=[pltpu.SemaphoreType.DMA((n, 7)), pltpu.SemaphoreType.DMA((n, 7))])(*arrs)


def _chip_gather(arrs, halved, name):
    n = len(arrs)
    offs = _offsets(2)

    def body(*refs):
        ins, outs = refs[:n], refs[n:2 * n]
        ssem, rsem = refs[2 * n:]
        x, y, c = _place()
        chip = 2 * x + y
        copies = []
        for a in range(n):
            if halved:
                hn = arrs[a].shape[0] // 2
                src = ins[a].at[pl.ds(c * hn, hn)]
                dst = outs[a].at[chip, pl.ds(c * hn, hn)]
            else:
                src, dst = ins[a], outs[a].at[chip]
            for k, (dx, dy) in enumerate(offs):
                cp = pltpu.make_async_remote_copy(
                    src_ref=src, dst_ref=dst, send_sem=ssem.at[a, k], recv_sem=rsem.at[a, k],
                    device_id=((x + dx) % 2, (y + dy) % 2, c), device_id_type=MESH)
                cp.start()
                copies.append(cp)
        for cp in copies:
            cp.wait()

    return _call(body, name=name, in_specs=[ANY] * n, out_specs=[ANY] * n,
                 out_shape=[jax.ShapeDtypeStruct((N_CHIPS,) + a.shape, a.dtype) for a in arrs],
                 scratch=[pltpu.SemaphoreType.DMA((n, 3)), pltpu.SemaphoreType.DMA((n, 3))])(*arrs)


def _sibling_fill(bufs, name):
    n = len(bufs)
    offs = _offsets(2)

    def body(*refs):
        ins, outs = refs[:n], refs[n:2 * n]
        ssem, rsem = refs[2 * n:]
        x, y, c = _place()
        copies = []
        for a in range(n):
            hn = bufs[a].shape[1] // 2
            for k, (dx, dy) in enumerate(offs):
                chip = 2 * ((x + dx) % 2) + (y + dy) % 2
                cp = pltpu.make_async_remote_copy(
                    src_ref=ins[a].at[chip, pl.ds(c * hn, hn)], dst_ref=outs[a].at[chip, pl.ds(c * hn, hn)],
                    send_sem=ssem.at[a, k], recv_sem=rsem.at[a, k],
                    device_id=(x, y, 1 - c), device_id_type=MESH)
                cp.start()
                copies.append(cp)
        for cp in copies:
            cp.wait()

    return _call(body, name=name, in_specs=[ANY] * n, out_specs=[ANY] * n,
                 out_shape=[jax.ShapeDtypeStruct(b.shape, b.dtype) for b in bufs],
                 scratch=[pltpu.SemaphoreType.DMA((n, 3)), pltpu.SemaphoreType.DMA((n, 3))],
                 aliases={a: a for a in range(n)})(*bufs)


def _sibling_pair(arrs, name):
    n = len(arrs)

    def body(*refs):
        ins, outs = refs[:n], refs[n:2 * n]
        ssem, rsem = refs[2 * n:]
        x, y, c = _place()
        copies = []
        for a in range(n):
            cp = pltpu.make_async_remote_copy(
                src_ref=ins[a], dst_ref=outs[a].at[c], send_sem=ssem.at[a], recv_sem=rsem.at[a],
                device_id=(x, y, 1 - c), device_id_type=MESH)
            cp.start()
            copies.append(cp)
        for cp in copies:
            cp.wait()

    return _call(body, name=name, in_specs=[ANY] * n, out_specs=[ANY] * n,
                 out_shape=[jax.ShapeDtypeStruct((N_CORES,) + a.shape, a.dtype) for a in arrs],
                 scratch=[pltpu.SemaphoreType.DMA((n,)), pltpu.SemaphoreType.DMA((n,))])(*arrs)


def _piece(shape, spec, j, h):
    shard_ax, half_ax = spec
    w = shape[shard_ax] // N_CHIPS
    idx = [slice(None)] * len(shape)
    idx[shard_ax] = pl.ds(j * w, w)
    hn = (w if half_ax == shard_ax else shape[half_ax]) // 2
    assert half_ax != shard_ax
    idx[half_ax] = pl.ds(h * hn, hn)
    return tuple(idx)


def _piece_shape(shape, spec):
    shard_ax, half_ax = spec
    s = list(shape)
    s[shard_ax] //= N_CHIPS
    s[half_ax] //= 2
    return tuple(s)


def _own_pieces(g, spec, c):
    shard_ax, half_ax = spec
    hn = g.shape[half_ax] // 2
    half = lax.dynamic_slice_in_dim(g, c * hn, hn, axis=half_ax)
    shape = list(half.shape)
    shape[shard_ax:shard_ax + 1] = [N_CHIPS, shape[shard_ax] // N_CHIPS]
    return jnp.moveaxis(half.reshape(shape), shard_ax, 0)


def _sibling_scatter(arrs, specs, name):
    n = len(arrs)

    def body(*refs):
        ins, outs = refs[:n], refs[n:2 * n]
        ssem, rsem = refs[2 * n:]
        x, y, c = _place()
        for mine in range(N_CORES):
            @pl.when(c == mine)
            def _():
                copies = []
                for a in range(n):
                    for j in range(N_CHIPS):
                        cp = pltpu.make_async_remote_copy(
                            src_ref=ins[a].at[_piece(arrs[a].shape, specs[a], j, 1 - mine)], dst_ref=outs[a].at[j],
                            send_sem=ssem.at[a, j], recv_sem=rsem.at[a, j],
                            device_id=(x, y, 1 - mine), device_id_type=MESH)
                        cp.start()
                        copies.append(cp)
                for cp in copies:
                    cp.wait()

    return _call(body, name=name, in_specs=[ANY] * n, out_specs=[ANY] * n,
                 out_shape=[jax.ShapeDtypeStruct((N_CHIPS,) + _piece_shape(a.shape, s), a.dtype)
                            for a, s in zip(arrs, specs)],
                 scratch=[pltpu.SemaphoreType.DMA((n, N_CHIPS))] * 2)(*arrs)


def _chip_scatter(arrs, name):
    n = len(arrs)
    offs = _offsets(2)

    def body(*refs):
        ins, outs = refs[:n], refs[n:2 * n]
        ssem, rsem = refs[2 * n:]
        x, y, c = _place()
        chip = 2 * x + y
        copies = []
        for a in range(n):
            for k, (dx, dy) in enumerate(offs):
                tx, ty = (x + dx) % 2, (y + dy) % 2
                cp = pltpu.make_async_remote_copy(
                    src_ref=ins[a].at[2 * tx + ty], dst_ref=outs[a].at[chip], send_sem=ssem.at[a, k], recv_sem=rsem.at[a, k],
                    device_id=(tx, ty, c), device_id_type=MESH)
                cp.start()
                copies.append(cp)
        for cp in copies:
            cp.wait()

    return _call(body, name=name, in_specs=[ANY] * n, out_specs=[ANY] * n,
                 out_shape=[jax.ShapeDtypeStruct(a.shape, a.dtype) for a in arrs],
                 scratch=[pltpu.SemaphoreType.DMA((n, 3)), pltpu.SemaphoreType.DMA((n, 3))])(*arrs)


def kernel(x, c, ada_w, ada_b, pre_mix_g, post_mix_g, pre_ffn_g, post_ffn_g, ffn_w_gu, ffn_w_down, a_w_in, a_b_in, a_ln_g, a_ln_b, a_w_s, a_b_s, a_w_out, kv_ada_w, kv_ada_b, kv_norm_g, kv_w, kv_b_f, k_norm_g, b_w_qg, b_q_norm_g, b_w_o, loss_target, m_ada_w, m_ada_b, m_pre_mix_g, m_post_mix_g, m_pre_ffn_g, m_post_ffn_g, m_ffn_w_gu, m_ffn_w_down, m_a_w_in, m_a_b_in, m_a_ln_g, m_a_ln_b, m_a_w_s, m_a_b_s, m_a_w_out, m_kv_ada_w, m_kv_ada_b, m_kv_norm_g, m_kv_w, m_kv_b_f, m_k_norm_g, m_b_w_qg, m_b_q_norm_g, m_b_w_o, v_ada_w, v_ada_b, v_pre_mix_g, v_post_mix_g, v_pre_ffn_g, v_post_ffn_g, v_ffn_w_gu, v_ffn_w_down, v_a_w_in, v_a_b_in, v_a_ln_g, v_a_ln_b, v_a_w_s, v_a_b_s, v_a_w_out, v_kv_ada_w, v_kv_ada_b, v_kv_norm_g, v_kv_w, v_kv_b_f, v_k_norm_g, v_b_w_qg, v_b_q_norm_g, v_b_w_o):
    weights = dict(ada_w=ada_w, ada_b=ada_b, pre_mix_g=pre_mix_g, post_mix_g=post_mix_g, pre_ffn_g=pre_ffn_g,
                   post_ffn_g=post_ffn_g, ffn_w_gu=ffn_w_gu, ffn_w_down=ffn_w_down, a_w_in=a_w_in, a_b_in=a_b_in,
                   a_ln_g=a_ln_g, a_ln_b=a_ln_b, a_w_s=a_w_s, a_b_s=a_b_s, a_w_out=a_w_out, kv_ada_w=kv_ada_w,
                   kv_ada_b=kv_ada_b, kv_norm_g=kv_norm_g, kv_w=kv_w, kv_b_f=kv_b_f, k_norm_g=k_norm_g, b_w_qg=b_w_qg,
                   b_q_norm_g=b_q_norm_g, b_w_o=b_w_o)
    m_in = dict(ada_w=m_ada_w, ada_b=m_ada_b, pre_mix_g=m_pre_mix_g, post_mix_g=m_post_mix_g, pre_ffn_g=m_pre_ffn_g,
                post_ffn_g=m_post_ffn_g, ffn_w_gu=m_ffn_w_gu, ffn_w_down=m_ffn_w_down, a_w_in=m_a_w_in, a_b_in=m_a_b_in,
                a_ln_g=m_a_ln_g, a_ln_b=m_a_ln_b, a_w_s=m_a_w_s, a_b_s=m_a_b_s, a_w_out=m_a_w_out, kv_ada_w=m_kv_ada_w,
                kv_ada_b=m_kv_ada_b, kv_norm_g=m_kv_norm_g, kv_w=m_kv_w, kv_b_f=m_kv_b_f, k_norm_g=m_k_norm_g,
                b_w_qg=m_b_w_qg, b_q_norm_g=m_b_q_norm_g, b_w_o=m_b_w_o)
    v_in = dict(ada_w=v_ada_w, ada_b=v_ada_b, pre_mix_g=v_pre_mix_g, post_mix_g=v_post_mix_g, pre_ffn_g=v_pre_ffn_g,
                post_ffn_g=v_post_ffn_g, ffn_w_gu=v_ffn_w_gu, ffn_w_down=v_ffn_w_down, a_w_in=v_a_w_in, a_b_in=v_a_b_in,
                a_ln_g=v_a_ln_g, a_ln_b=v_a_ln_b, a_w_s=v_a_w_s, a_b_s=v_a_b_s, a_w_out=v_a_w_out, kv_ada_w=v_kv_ada_w,
                kv_ada_b=v_kv_ada_b, kv_norm_g=v_kv_norm_g, kv_w=v_kv_w, kv_b_f=v_kv_b_f, k_norm_g=v_k_norm_g,
                b_w_qg=v_b_w_qg, b_q_norm_g=v_b_q_norm_g, b_w_o=v_b_w_o)
    names = list(weights)

    S, D = x.shape[1], x.shape[2]
    L, NA, NB = ada_w.shape[0], a_w_in.shape[0], b_w_qg.shape[0]
    H = kv_b_f.shape[0]
    hd = D // H
    G, CH = a_w_s.shape[1], a_w_s.shape[2]
    GW = a_w_out.shape[1] * N_CHIPS
    F = ffn_w_down.shape[1] * N_CHIPS
    ada_cols = ada_w.shape[2]
    kvada_cols = kv_ada_w.shape[1]
    kv_cols = kv_w.shape[1]
    kv_pad = -(-(2 * D + H) // LANES) * LANES
    xi, yi, ci = _place()
    chip = 2 * xi + yi
    me = 2 * chip + ci
    x0 = x[0]
    tgt = loss_target[0]
    row = lambda t: t.reshape(1, -1)

    c_all = _own_slot(_gather8([c], "gather_c")[0], c, me).reshape(N_DEV, D)
    c_act = _silu_rows(jnp.pad(c_all, ((0, BF16_ROWS - N_DEV), (0, 0))), "silu_c")
    mod_sh = [_mm(c_act, (ada_w, l), "nn", F32, f"mod_proj_{l}") for l in range(L)]
    mod_sh.append(_mm(c_act, kv_ada_w, "nn", F32, "mod_proj_kv"))
    mod_sh = jnp.concatenate(mod_sh, axis=1)
    small_sh = [mod_sh, a_b_in, a_ln_g, a_ln_b]
    mod_all, b_in_all, ln_g_all, ln_b_all = [
        _own_slot(o, s, chip) for o, s in zip(_chip_gather(small_sh, False, "gather_mod"), small_sh)]
    mine = lax.dynamic_index_in_dim(mod_all, me, axis=1, keepdims=False)
    mod = [jnp.concatenate([mine[j, l * ada_cols:(l + 1) * ada_cols] for j in range(N_CHIPS)]) + ada_b[l] for l in range(L)]
    mod = [[row(t) for t in jnp.split(m_, 6)] for m_ in mod]
    mod_kv = jnp.concatenate([mine[j, L * ada_cols:] for j in range(N_CHIPS)]) + kv_ada_b
    kv_sh, kv_sc = [row(t) for t in jnp.split(mod_kv, 2)]
    cat_chips = lambda t, ax: jnp.concatenate([t[j] for j in range(N_CHIPS)], axis=ax)
    b_in_f = cat_chips(b_in_all, 1)
    ln_g_f, ln_b_f = cat_chips(ln_g_all, 1), cat_chips(ln_b_all, 1)

    big = ["ffn_w_gu", "ffn_w_down", "a_w_in", "a_w_out", "kv_w", "b_w_qg", "b_w_o"]
    own_w = [weights[n].astype(BF16) for n in big]
    gathered = _sibling_fill(_chip_gather(own_w, True, "gather_w"), "fill_w")
    gathered = {n: _own_slot(g, w, chip) for n, g, w in zip(big, gathered, own_w)}
    w_gu = cat_chips(gathered["ffn_w_gu"], 2)
    w_dn = cat_chips(gathered["ffn_w_down"], 1)
    w_in = cat_chips(gathered["a_w_in"], 2)
    w_out = cat_chips(gathered["a_w_out"], 1)
    w_kv = jnp.pad(cat_chips(gathered["kv_w"], 1), ((0, 0), (0, kv_pad - (2 * D + H))))
    w_qg = cat_chips(gathered["b_w_qg"], 2)
    w_o = cat_chips(gathered["b_w_o"], 1)

    causal = jnp.tril(jnp.ones((CH, CH), F32))
    ws_m = [(a_w_s[i] * causal).astype(BF16) for i in range(NA)]
    ws_mt = [jnp.swapaxes(w, 1, 2) for w in ws_m]
    bs_t = [a_b_s[i].T for i in range(NA)]

    heads = lambda t: t.reshape(S, H, hd).transpose(1, 0, 2)
    unheads = lambda t: t.transpose(1, 0, 2).reshape(S, D)

    saved = []
    kv = None
    xc = x0
    for l in range(L):
        sh_m, sc_m, g_m, sh_f, sc_f, g_f = mod[l]
        st = {"x0": xc}
        h1 = _norm_mod_fwd(xc, row(pre_mix_g[l]), sh_m, sc_m, f"pre_mix_{l}")
        st["h1"] = h1
        if l < NA:
            zp = _mm(h1, (w_in, l), "nn", BF16, f"gmlp_in_{l}")
            yg = _gmlp_fwd(zp, row(b_in_f[l]), row(ln_g_f[l]), row(ln_b_f[l]), ws_m[l], bs_t[l], f"gmlp_gate_{l}")
            y = _mm(yg, (w_out, l), "nn", F32, f"gmlp_out_{l}")
            st.update(zp=zp, yg=yg)
        else:
            jb = l - NA
            qg = _mm(h1, (w_qg, jb), "nn", BF16, f"fox_qg_{jb}")
            q_raw = heads(qg[:, :D]).reshape(H * S, hd)
            qn = _head_norm_fwd(q_raw, row(b_q_norm_g[jb]), hd ** -0.5, f"fox_qnorm_{jb}").reshape(H, S, hd)
            qa = _augment(qn, kv["dcum"], True)
            qat = jnp.swapaxes(qa, 1, 2)
            o_tr, lse_r = _flash_fwd(kv["ka"], qat, kv["vat"], hd, f"fox_attn_{jb}")
            o_t = o_tr.transpose(2, 0, 1).reshape(S, D)
            og = _out_gate_fwd(o_t, qg, f"fox_gate_{jb}")
            y = _mm(og, (w_o, jb), "nn", F32, f"fox_out_{jb}")
            st.update(qg=qg, q_raw=q_raw, qa=qa, qat=qat, o_tr=o_tr, lse_r=lse_r, o_t=o_t, og=og)
        st["y"] = y
        x1 = _post_fwd(xc, y, row(post_mix_g[l]), g_m, f"post_mix_{l}")
        st["x1"] = x1
        h2 = _norm_mod_fwd(x1, row(pre_ffn_g[l]), sh_f, sc_f, f"pre_ffn_{l}")
        gu = _mm(h2, (w_gu, l), "nn", BF16, f"ffn_gu_{l}")
        act = _swiglu_fwd(gu, f"ffn_act_{l}")
        y2 = _mm(act, (w_dn, l), "nn", F32, f"ffn_down_{l}")
        xc = _post_fwd(x1, y2, row(post_ffn_g[l]), g_f, f"post_ffn_{l}")
        st.update(h2=h2, gu=gu, act=act, y2=y2)
        saved.append(st)
        if l == NA - 1:
            hk = _norm_mod_fwd(xc, row(kv_norm_g), kv_sh, kv_sc, "kv_pre")
            kvf = _mm(hk, w_kv, "nn", F32, "kv_proj")
            k_raw = heads(kvf[:, :D]).reshape(H * S, hd)
            kn = _head_norm_fwd(k_raw, row(k_norm_g), 1.0, "kv_knorm").reshape(H, S, hd)
            vb = heads(kvf[:, D:2 * D]).astype(BF16)
            f_t = kvf[:, 2 * D:2 * D + H].T
            b_col = kv_b_f.reshape(H, 1)
            dcum = _dcum_fwd(f_t, b_col, "kv_dcum")
            vt = kvf[:, D:2 * D].astype(BF16).reshape(S, H, hd).transpose(1, 2, 0)
            vat = jnp.where(lax.broadcasted_iota(jnp.int32, (1, LANES, 1), 1) == hd, jnp.asarray(1, BF16),
                            jnp.pad(vt, ((0, 0), (0, LANES - hd), (0, 0))))
            ka = _augment(kn, dcum, False)
            kv = dict(x=xc, hk=hk, k_raw=k_raw, ka=ka, kat=jnp.swapaxes(ka, 1, 2), vb=vb, vat=vat,
                      f_t=f_t, b_col=b_col, dcum=dcum)

    dx, loss_part = _loss_bwd(xc, tgt, "loss")
    loss = lax.psum(loss_part[0, 0], ("x", "y", "c"))

    gl = {n: [None] * weights[n].shape[0] for n in
          ["pre_mix_g", "post_mix_g", "pre_ffn_g", "post_ffn_g", "ffn_w_gu", "ffn_w_down", "a_w_in", "a_b_in", "a_ln_g",
           "a_ln_b", "a_w_s", "a_b_s", "a_w_out", "b_w_qg", "b_q_norm_g", "b_w_o"]}
    dmod = [None] * L
    dkn = dvb = ddc = None
    gkv = {}
    for l in reversed(range(L)):
        st = saved[l]
        sh_m, sc_m, g_m, sh_f, sc_f, g_f = mod[l]
        if l == NA - 1:
            dk_raw, gkv["k_norm_g"] = _head_norm_bwd(dkn.reshape(H * S, hd), kv["k_raw"], row(k_norm_g), 1.0, "kv_knorm_bwd")
            df_t, db_f = _dcum_bwd(ddc.reshape(H, S), kv["f_t"], kv["b_col"], "kv_dcum_bwd")
            dkvf = jnp.concatenate([unheads(dk_raw.reshape(H, S, hd)), unheads(dvb), df_t.T,
                                    jnp.zeros((S, kv_pad - (2 * D + H)), F32)], axis=1).astype(BF16)
            gkv["kv_w"] = _mm(kv["hk"], dkvf, "tn", BF16, "kv_proj_dw")[:, :2 * D + H]
            dhk = _mm(dkvf, w_kv, "nt", F32, "kv_proj_dx")
            dx, gkv["kv_norm_g"], dsh, dsc = _norm_mod_bwd(dx, dhk, kv["x"], row(kv_norm_g), kv_sh, kv_sc, "kv_pre_bwd")
            gkv["kv_b_f"] = db_f.reshape(H)
            dmod_kv = jnp.concatenate([dsh, dsc], axis=1)
        dy2, gl["post_ffn_g"][l], dg_f = _post_bwd(dx, st["y2"], row(post_ffn_g[l]), g_f, f"post_ffn_bwd_{l}")
        gl["ffn_w_down"][l] = _mm(st["act"], dy2, "tn", BF16, f"ffn_down_dw_{l}")
        dact = _mm(dy2, (w_dn, l), "nt", BF16, f"ffn_down_dx_{l}")
        dgu = _swiglu_bwd(dact, st["gu"], f"ffn_act_bwd_{l}")
        gl["ffn_w_gu"][l] = _mm(st["h2"], dgu, "tn", BF16, f"ffn_gu_dw_{l}")
        dh2 = _mm(dgu, (w_gu, l), "nt", F32, f"ffn_gu_dx_{l}")
        dx, gl["pre_ffn_g"][l], dsh_f, dsc_f = _norm_mod_bwd(dx, dh2, st["x1"], row(pre_ffn_g[l]), sh_f, sc_f, f"pre_ffn_bwd_{l}")
        dy, gl["post_mix_g"][l], dg_m = _post_bwd(dx, st["y"], row(post_mix_g[l]), g_m, f"post_mix_bwd_{l}")
        if l < NA:
            gl["a_w_out"][l] = _mm(st["yg"], dy, "tn", BF16, f"gmlp_out_dw_{l}")
            dyg = _mm(dy, (w_out, l), "nt", BF16, f"gmlp_out_dx_{l}")
            dzp, db_in, dlg, dlb, dws, dbs_t = _gmlp_bwd(dyg, st["zp"], row(b_in_f[l]), row(ln_g_f[l]), row(ln_b_f[l]),
                                                           ws_m[l], ws_mt[l], bs_t[l], f"gmlp_gate_bwd_{l}")
            gl["a_b_in"][l], gl["a_ln_g"][l], gl["a_ln_b"][l] = db_in[0], dlg[0], dlb[0]
            gl["a_w_s"][l], gl["a_b_s"][l] = dws * causal, dbs_t.T
            gl["a_w_in"][l] = _mm(st["h1"], dzp, "tn", BF16, f"gmlp_in_dw_{l}")
            dh1 = _mm(dzp, (w_in, l), "nt", F32, f"gmlp_in_dx_{l}")
        else:
            jb = l - NA
            gl["b_w_o"][jb] = _mm(st["og"], dy, "tn", BF16, f"fox_out_dw_{jb}")
            dog = _mm(dy, (w_o, jb), "nt", F32, f"fox_out_dx_{jb}")
            do_t, dgl = _out_gate_bwd(dog, st["o_t"], st["qg"], f"fox_gate_bwd_{jb}")
            do = heads(do_t)
            dot = do_t.reshape(S, H, hd).transpose(1, 2, 0)
            dqa_tr, dk_j, dv_j, dd_k, dd_q = _flash_bwd(kv["ka"], kv["kat"], st["qa"], st["qat"], kv["vb"], do, dot,
                                                        st["o_tr"], st["lse_r"], f"fox_attn_bwd_{jb}")
            dqn = dqa_tr[:, :, :hd, :].transpose(0, 1, 3, 2).reshape(H, S, hd)
            dk_j = dk_j[:, :, :hd]
            dd_j = dd_k.reshape(H, S) + dd_q.reshape(H, S)
            dkn = dk_j if dkn is None else dkn + dk_j
            dvb = dv_j if dvb is None else dvb + dv_j
            ddc = dd_j if ddc is None else ddc + dd_j
            dq_raw, dgq = _head_norm_bwd(dqn.reshape(H * S, hd), st["q_raw"], row(b_q_norm_g[jb]), hd ** -0.5, f"fox_qnorm_bwd_{jb}")
            gl["b_q_norm_g"][jb] = dgq[0]
            dqg = jnp.concatenate([unheads(dq_raw.reshape(H, S, hd)).astype(BF16), dgl], axis=1)
            gl["b_w_qg"][jb] = _mm(st["h1"], dqg, "tn", BF16, f"fox_qg_dw_{jb}")
            dh1 = _mm(dqg, (w_qg, jb), "nt", F32, f"fox_qg_dx_{jb}")
        dx, gl["pre_mix_g"][l], dsh_m, dsc_m = _norm_mod_bwd(dx, dh1, st["x0"], row(pre_mix_g[l]), sh_m, sc_m, f"pre_mix_bwd_{l}")
        dmod[l] = jnp.concatenate([dsh_m, dsc_m, dg_m, dsh_f, dsc_f, dg_f], axis=1)
    grad_x = dx[None]

    stack = lambda n: jnp.stack([t.reshape(weights[n].shape[1:]) for t in gl[n]])
    small = {"dmod": jnp.concatenate(dmod, axis=1), "dmod_kv": dmod_kv}
    for n in ["pre_mix_g", "post_mix_g", "pre_ffn_g", "post_ffn_g", "a_w_s", "a_b_s", "b_q_norm_g"]:
        small[n] = stack(n)
    for n in ["a_b_in", "a_ln_g", "a_ln_b"]:
        small[n] = jnp.stack(gl[n])
    for n in ["kv_norm_g", "kv_b_f", "k_norm_g"]:
        small[n] = gkv[n]
    sizes = {n: t.size for n, t in small.items()}
    flat = jnp.concatenate([t.reshape(-1).astype(F32) for t in small.values()])
    rows_small = -(-flat.size // (LANES * BF16_ROWS)) * BF16_ROWS
    flat = jnp.pad(flat, (0, rows_small * LANES - flat.size)).reshape(rows_small, LANES)
    flat_all = _own_slot(_gather8([flat], "gather_small")[0], flat, me)
    flat_sum = _sum_slots(flat_all, "sum_small").reshape(-1)
    offs, o_ = {}, 0
    for n, sz in sizes.items():
        offs[n] = o_
        o_ += sz
    take = lambda n, shape: flat_sum[offs[n]:offs[n] + sizes[n]].reshape(shape)
    dmod_rows = flat_all.reshape(N_DEV, -1)[:, offs["dmod"]:offs["dmod"] + sizes["dmod"] + sizes["dmod_kv"]]
    dmod_rows = jnp.pad(dmod_rows, ((0, BF16_ROWS - N_DEV), (0, 0)))

    grads = {}
    grads["ada_b"] = take("dmod", (L, 6 * D))
    grads["kv_ada_b"] = take("dmod_kv", (2 * D,))
    for n in ["pre_mix_g", "post_mix_g", "pre_ffn_g", "post_ffn_g", "a_w_s", "a_b_s", "b_q_norm_g", "kv_norm_g", "kv_b_f", "k_norm_g"]:
        grads[n] = take(n, weights[n].shape)
    for n in ["a_b_in", "a_ln_g", "a_ln_b"]:
        full = take(n, small[n].shape)
        w = weights[n].shape[1]
        grads[n] = lax.dynamic_slice_in_dim(full, chip * w, w, axis=1)
    ada_g = []
    for l in range(L):
        cols = lax.dynamic_slice_in_dim(dmod_rows[:, l * 6 * D:(l + 1) * 6 * D], chip * ada_cols, ada_cols, axis=1)
        ada_g.append(_mm(c_act, cols, "tn", F32, f"mod_proj_dw_{l}"))
    grads["ada_w"] = jnp.stack(ada_g)
    cols = lax.dynamic_slice_in_dim(dmod_rows[:, L * 6 * D:], chip * kvada_cols, kvada_cols, axis=1)
    grads["kv_ada_w"] = _mm(c_act, cols, "tn", F32, "mod_proj_kv_dw")

    specs = {"ffn_w_gu": (2, 0), "ffn_w_down": (1, 0), "a_w_in": (2, 0), "a_w_out": (1, 0), "kv_w": (0, 1),
             "b_w_qg": (2, 0), "b_w_o": (1, 0)}
    full_g = {n: jnp.stack(gl[n]) for n in big if n != "kv_w"}
    full_g["kv_w"] = gkv["kv_w"].reshape(D, N_CHIPS, kv_cols).transpose(1, 0, 2)
    from_core = _sibling_scatter([full_g[n] for n in big], [specs[n] for n in big], "scatter_g_core")
    chip_sums = [_sum_pair(_own_pieces(full_g[n], specs[n], ci), r, f"sum_g_core_{n}") for n, r in zip(big, from_core)]
    recv = _chip_scatter(chip_sums, "scatter_g_chip")
    recv = [_own_slot(r, lax.dynamic_index_in_dim(p, chip, 0, keepdims=False), chip) for r, p in zip(recv, chip_sums)]
    halves = [_sum_slots(r, f"sum_g_{n}") for n, r in zip(big, recv)]
    pairs = _sibling_pair(halves, "pair_g")
    for n, p, hlf in zip(big, pairs, halves):
        grads[n] = _own_slot(p, hlf, ci).reshape(weights[n].shape)

    outs_d, outs_m, outs_v = {}, {}, {}
    for n in names:
        w2 = weights[n] if weights[n].ndim > 1 else weights[n].reshape(1, -1)
        shp = w2.shape
        d_, m_, v_ = _adamw(w2, grads[n].reshape(shp), m_in[n].reshape(shp), v_in[n].reshape(shp), f"adamw_{n}")
        outs_d[n], outs_m[n], outs_v[n] = (t.reshape(weights[n].shape) for t in (d_, m_, v_))
    return (loss, grad_x, *[grads[n] for n in names], *[outs_d[n] for n in names],
            *[outs_m[n] for n in names], *[outs_v[n] for n in names])
```

```python
import functools

import jax
import jax.numpy as jnp
from jax import lax
from jax.experimental import pallas as pl
from jax.experimental.pallas import tpu as pltpu

F32 = jnp.float32
BF16 = jnp.bfloat16
MESH = pl.DeviceIdType.MESH
NORM_EPS = 1e-6
MASKED = -1e30
LANES = 128
BF16_ROWS = 16
ROW_BLOCK_BYTES = 12 << 20
ADAM_LR, ADAM_B1, ADAM_B2, ADAM_EPS, ADAM_WD, ADAM_STEP = 0.001, 0.9, 0.999, 1e-08, 0.01, 10
N_CHIPS, N_CORES, N_DEV = 4, 2, 8
ATTN_HEADS_PER_STEP = 4
ATTN_STAGED_HEADS = 2
ANY = pl.BlockSpec(memory_space=pl.ANY)


def _tile(n, cap, quantum):
    best = None
    d = quantum
    while d <= min(n, cap):
        if n % d == 0:
            best = d
        d += quantum
    return n if best is None else best


def _call(body, *, name, out_shape, grid=(), in_specs=None, out_specs=None, scratch=(), sem=None, aliases=None):
    params = {} if sem is None else {"dimension_semantics": sem}
    return pl.pallas_call(
        body, name=name, grid=grid, in_specs=in_specs, out_specs=out_specs, out_shape=out_shape,
        scratch_shapes=list(scratch), input_output_aliases=aliases or {},
        compiler_params=pltpu.CompilerParams(**params))


def _call_prefetch(body, *, name, out_shape, grid, n_prefetch, in_specs, out_specs, scratch, sem):
    spec = pltpu.PrefetchScalarGridSpec(num_scalar_prefetch=n_prefetch, grid=grid, in_specs=in_specs,
                                        out_specs=out_specs, scratch_shapes=list(scratch))
    return pl.pallas_call(
        body, name=name, grid_spec=spec, out_shape=out_shape,
        compiler_params=pltpu.CompilerParams(dimension_semantics=sem))


def _place():
    x, y, c = lax.axis_index("x"), lax.axis_index("y"), lax.axis_index("c")
    return x, y, c


def _mm(a, b, mode, out_dtype, name):
    b_arr, b_idx = b if isinstance(b, tuple) else (b, None)
    bs = b_arr.shape[-2:]
    if mode == "nn":
        (M, K), (K2, N) = a.shape, bs
        dims = (((1,), (0,)), ((), ()))
    elif mode == "nt":
        (M, K), (N, K2) = a.shape, bs
        dims = (((1,), (1,)), ((), ()))
    else:
        (K, M), (K2, N) = a.shape, bs
        dims = (((0,), (0,)), ((), ()))
    assert K == K2, (name, a.shape, b_arr.shape)
    if mode == "tn":
        tm = _tile(M, 1408, LANES)
        tk = _tile(K, 2048, BF16_ROWS)
        tn = _tile(N, 512, LANES)
    else:
        tm = _tile(M, 1024, BF16_ROWS)
        tk = K if K <= 2816 else _tile(K, 2816, LANES)
        tn = _tile(N, 1408 if tk <= 1024 else 512, LANES)
    if tn < 256:
        tn = N
        tm = _tile(M, 512, LANES if mode == "tn" else BF16_ROWS)
    nk = K // tk
    grid = (M // tm, N // tn, nk)

    if mode == "tn":
        a_spec = pl.BlockSpec((tk, tm), lambda i, j, k: (k, i))
    else:
        a_spec = pl.BlockSpec((tm, tk), lambda i, j, k: (i, k))
    if mode == "nt":
        b_blk, b_map = (tn, tk), (lambda i, j, k: (j, k))
    else:
        b_blk, b_map = (tk, tn), (lambda i, j, k: (k, j))
    if b_idx is None:
        b_spec = pl.BlockSpec(b_blk, b_map)
    else:
        b_spec = pl.BlockSpec((None,) + b_blk, lambda i, j, k: (b_idx,) + b_map(i, j, k))

    def body(a_ref, b_ref, o_ref, *acc):
        r = lax.dot_general(a_ref[...].astype(BF16), b_ref[...].astype(BF16), dims, preferred_element_type=F32)
        if nk == 1:
            o_ref[...] = r.astype(o_ref.dtype)
        else:
            k = pl.program_id(2)

            @pl.when(k == 0)
            def _():
                acc[0][...] = r

            @pl.when(k > 0)
            def _():
                acc[0][...] += r

            @pl.when(k == nk - 1)
            def _():
                o_ref[...] = acc[0][...].astype(o_ref.dtype)

    return _call(
        body, name=name, grid=grid, in_specs=[a_spec, b_spec],
        out_specs=pl.BlockSpec((tm, tn), lambda i, j, k: (i, j)),
        out_shape=jax.ShapeDtypeStruct((M, N), out_dtype),
        scratch=[pltpu.VMEM((tm, tn), F32)] if nk > 1 else [],
        sem=("parallel", "parallel", "arbitrary"))(a, b_arr)


def _rowwise(fn, rows, pars, outs, pouts, name):
    R = rows[0].shape[0]
    row_bytes = 4 * (sum(max(r.shape[1], LANES) for r in rows) + sum(max(c, LANES) for c, _ in outs))
    tb = _tile(R, max(BF16_ROWS, ROW_BLOCK_BYTES // row_bytes), BF16_ROWS)
    nr, npar, no = len(rows), len(pars), len(outs)

    def body(*refs):
        r_in, p_in = refs[:nr], refs[nr:nr + npar]
        r_out, p_out = refs[nr + npar:nr + npar + no], refs[nr + npar + no:]
        ro, po = fn([r[...] for r in r_in], [p[...] for p in p_in])
        for ref, val in zip(r_out, ro):
            if isinstance(val, (tuple, list)):
                off = 0
                for piece in val:
                    w = piece.shape[1]
                    ref[:, off:off + w] = piece.astype(ref.dtype)
                    off += w
            else:
                ref[...] = val.astype(ref.dtype)
        if p_out:
            first = pl.program_id(0) == 0

            @pl.when(first)
            def _():
                for ref, val in zip(p_out, po):
                    ref[...] = val

            @pl.when(jnp.logical_not(first))
            def _():
                for ref, val in zip(p_out, po):
                    ref[...] += val

    res = _call(
        body, name=name, grid=(R // tb,),
        in_specs=[pl.BlockSpec((tb, r.shape[1]), lambda i: (i, 0)) for r in rows]
        + [pl.BlockSpec(p.shape, lambda i: (0, 0)) for p in pars],
        out_specs=[pl.BlockSpec((tb, c), lambda i: (i, 0)) for c, _ in outs]
        + [pl.BlockSpec(s, lambda i: (0, 0)) for s in pouts],
        out_shape=[jax.ShapeDtypeStruct((R, c), dt) for c, dt in outs]
        + [jax.ShapeDtypeStruct(s, F32) for s in pouts],
        sem=("arbitrary",) if pouts else ("parallel",))(*rows, *pars)
    return list(res)


def _rms(x, g):
    return x * lax.rsqrt(jnp.mean(x * x, axis=-1, keepdims=True) + NORM_EPS) * g


def _norm_mod(x, g, sh, sc):
    return _rms(x, g) * (1.0 + sc) + sh


def _gated_post(y, g, gate):
    return gate * _rms(y, g)


def _norm_mod_fwd(x, g, sh, sc, name):
    return _rowwise(lambda r, p: ([_norm_mod(r[0], *p)], []), [x], [g, sh, sc], [(x.shape[1], BF16)], [], name)[0]


def _norm_mod_bwd(dxo, dh, x, g, sh, sc, name):
    def fn(r, p):
        _, vjp = jax.vjp(_norm_mod, r[2], *p)
        dx, dg, dsh, dsc = vjp(r[1].astype(F32))
        return [r[0] + dx], [dg, dsh, dsc]
    c = x.shape[1]
    return _rowwise(fn, [dxo, dh, x], [g, sh, sc], [(c, F32)], [(1, c)] * 3, name)


def _post_fwd(x, y, g, gate, name):
    return _rowwise(lambda r, p: ([r[0] + _gated_post(r[1].astype(F32), *p)], []), [x, y], [g, gate],
                    [(x.shape[1], F32)], [], name)[0]


def _post_bwd(dxo, y, g, gate, name):
    def fn(r, p):
        _, vjp = jax.vjp(_gated_post, r[1].astype(F32), *p)
        dy, dg, dgate = vjp(r[0])
        return [dy], [dg, dgate]
    c = y.shape[1]
    return _rowwise(fn, [dxo, y], [g, gate], [(c, BF16)], [(1, c)] * 2, name)


def _swiglu(g, u):
    return jax.nn.silu(g) * u


def _swiglu_fwd(gu, name):
    f = gu.shape[1] // 2
    return _rowwise(lambda r, p: ([_swiglu(r[0][:, :f].astype(F32), r[0][:, f:].astype(F32))], []), [gu], [],
                    [(f, BF16)], [], name)[0]


def _swiglu_bwd(da, gu, name):
    f = gu.shape[1] // 2

    def fn(r, p):
        _, vjp = jax.vjp(_swiglu, r[1][:, :f].astype(F32), r[1][:, f:].astype(F32))
        return [vjp(r[0].astype(F32))], []
    return _rowwise(fn, [da, gu], [], [(2 * f, BF16)], [], name)[0]


def _silu_rows(c, name):
    return _rowwise(lambda r, p: ([jax.nn.silu(r[0])], []), [c], [], [(c.shape[1], F32)], [], name)[0]


def _head_norm(x, g, scale):
    return _rms(x, g) * scale


def _head_norm_fwd(x, g, scale, name):
    return _rowwise(lambda r, p: ([_head_norm(r[0].astype(F32), p[0], scale)], []), [x], [g],
                    [(x.shape[1], BF16)], [], name)[0]


def _head_norm_bwd(dy, x, g, scale, name):
    def fn(r, p):
        _, vjp = jax.vjp(lambda t, gg: _head_norm(t, gg, scale), r[1].astype(F32), p[0])
        dx, dg = vjp(r[0])
        return [dx], [dg]
    c = x.shape[1]
    return _rowwise(fn, [dy, x], [g], [(c, F32)], [(1, c)], name)


def _out_gate_fwd(o, qg, name):
    d = o.shape[1]
    return _rowwise(lambda r, p: ([r[0] * jax.nn.sigmoid(r[1][:, d:].astype(F32))], []), [o, qg], [],
                    [(d, BF16)], [], name)[0]


def _out_gate_bwd(dog, o, qg, name):
    d = o.shape[1]

    def fn(r, p):
        _, vjp = jax.vjp(lambda oo, gl: oo * jax.nn.sigmoid(gl), r[1], r[2][:, d:].astype(F32))
        do, dgl = vjp(r[0])
        return [do, dgl], []
    return _rowwise(fn, [dog, o, qg], [], [(d, BF16), (d, BF16)], [], name)


def _loss_bwd(y, tgt, name):
    n = y.shape[1]

    def fn(r, p):
        e = r[0] - r[1]
        part = jnp.sum(jnp.sum(e * e, axis=1, keepdims=True), axis=0, keepdims=True) * (0.5 / n)
        return [e * (1.0 / n)], [part]
    return _rowwise(fn, [y, tgt], [], [(n, F32)], [(1, 1)], name)


def _adamw(w, g, m, v, name):
    shape = w.shape
    c = shape[-1]
    flat = [t.reshape(-1, c) for t in (w, g, m, v)]

    def fn(r, p):
        w_, g_, m_, v_ = r
        m2 = ADAM_B1 * m_ + (1.0 - ADAM_B1) * g_
        v2 = ADAM_B2 * v_ + (1.0 - ADAM_B2) * (g_ * g_)
        m_hat = m2 / (1.0 - ADAM_B1 ** ADAM_STEP)
        v_hat = v2 / (1.0 - ADAM_B2 ** ADAM_STEP)
        delta = -ADAM_LR * (m_hat / (jnp.sqrt(v_hat) + ADAM_EPS) + ADAM_WD * w_)
        return [delta, m2, v2], []
    res = _rowwise(fn, flat, [], [(c, F32)] * 3, [], name)
    return [t.reshape(shape) for t in res]


def _sum_pair(a, b, name):
    c = a.shape[-1]
    out = _rowwise(lambda r, p: ([r[0].astype(F32) + r[1].astype(F32)], []), [a.reshape(-1, c), b.reshape(-1, c)], [],
                   [(c, BF16)], [], name)[0]
    return out.reshape(a.shape)


def _sum_slots(recv, name, out_dtype=F32):
    n = recv.shape[0]
    shape = recv.shape[1:]
    c = shape[-1]
    r3 = recv.reshape(n, -1, c)
    rows = r3.shape[1]
    tb = _tile(rows, max(BF16_ROWS, ROW_BLOCK_BYTES // (4 * c * (n + 1))), BF16_ROWS)

    def body(r_ref, o_ref):
        acc = r_ref[0].astype(F32)
        for s in range(1, n):
            acc = acc + r_ref[s].astype(F32)
        o_ref[...] = acc.astype(o_ref.dtype)

    out = _call(body, name=name, grid=(rows // tb,),
                in_specs=[pl.BlockSpec((n, tb, c), lambda i: (0, i, 0))],
                out_specs=pl.BlockSpec((tb, c), lambda i: (i, 0)),
                out_shape=jax.ShapeDtypeStruct((rows, c), out_dtype), sem=("parallel",))(r3)
    return out.reshape(shape)


def _gmlp_pre(zu, zv, b_u, b_v, ln_g, ln_b):
    u = jax.nn.gelu(zu + b_u, approximate=True)
    v = jax.nn.gelu(zv + b_v, approximate=True)
    xc = v - jnp.mean(v, axis=-1, keepdims=True)
    vn = xc * lax.rsqrt(jnp.mean(xc * xc, axis=-1, keepdims=True) + NORM_EPS) * ln_g + ln_b
    return u, vn


def _gmlp_fwd(zp, b_in, ln_g, ln_b, ws, bs_t, name):
    S, gw2 = zp.shape
    gw = gw2 // 2
    G, ch, _ = ws.shape
    gd = gw // G
    tb = 2 * ch

    def body(zp_ref, bin_ref, lg_ref, lb_ref, ws_ref, bs_ref, o_ref):
        u, vn = _gmlp_pre(zp_ref[:, :gw].astype(F32), zp_ref[:, gw:].astype(F32), bin_ref[:, :gw], bin_ref[:, gw:],
                          lg_ref[...], lb_ref[...])
        vnb = vn.astype(BF16)
        for c in range(tb // ch):
            for g in range(G):
                rs, cs = slice(c * ch, (c + 1) * ch), slice(g * gd, (g + 1) * gd)
                vv = jnp.dot(ws_ref[g], vnb[rs, cs], preferred_element_type=F32) + bs_ref[:, g:g + 1]
                o_ref[rs, cs] = (u[rs, cs] * vv).astype(o_ref.dtype)

    full = lambda a: pl.BlockSpec(a.shape, lambda i: (0,) * a.ndim)
    return _call(body, name=name, grid=(S // tb,),
                 in_specs=[pl.BlockSpec((tb, gw2), lambda i: (i, 0)), full(b_in), full(ln_g), full(ln_b), full(ws), full(bs_t)],
                 out_specs=pl.BlockSpec((tb, gw), lambda i: (i, 0)),
                 out_shape=jax.ShapeDtypeStruct((S, gw), BF16), sem=("parallel",))(zp, b_in, ln_g, ln_b, ws, bs_t)


def _gmlp_bwd(dyg, zp, b_in, ln_g, ln_b, ws, ws_t, bs_t, name):
    S, gw2 = zp.shape
    gw = gw2 // 2
    G, ch, _ = ws.shape
    gd = gw // G
    tb = 2 * ch

    def body(dy_ref, zp_ref, bin_ref, lg_ref, lb_ref, ws_ref, wst_ref, bs_ref,
             dzp_ref, dbin_ref, dlg_ref, dlb_ref, dws_ref, dbs_ref, du_sc, dvn_sc):
        (u, vn), vjp = jax.vjp(_gmlp_pre, zp_ref[:, :gw].astype(F32), zp_ref[:, gw:].astype(F32), bin_ref[:, :gw],
                               bin_ref[:, gw:], lg_ref[...], lb_ref[...])
        vnb = vn.astype(BF16)
        first = pl.program_id(0) == 0

        @pl.when(first)
        def _():
            dws_ref[...] = jnp.zeros_like(dws_ref)

        lane = lax.broadcasted_iota(jnp.int32, (ch, G), 1)
        dbs = jnp.zeros((ch, G), F32)
        for g in range(G):
            cs = slice(g * gd, (g + 1) * gd)
            dws_g = jnp.zeros((ch, ch), F32)
            col = jnp.zeros((ch, 1), F32)
            for c in range(tb // ch):
                rs = slice(c * ch, (c + 1) * ch)
                vnp = vnb[rs, cs]
                vv = jnp.dot(ws_ref[g], vnp, preferred_element_type=F32) + bs_ref[:, g:g + 1]
                dy = dy_ref[rs, cs].astype(F32)
                du_sc[rs, cs] = dy * vv
                dvv = dy * u[rs, cs]
                dvvb = dvv.astype(BF16)
                dvn_sc[rs, cs] = jnp.dot(wst_ref[g], dvvb, preferred_element_type=F32)
                dws_g = dws_g + lax.dot_general(dvvb, vnp, (((1,), (1,)), ((), ())), preferred_element_type=F32)
                col = col + jnp.sum(dvv, axis=1, keepdims=True)
            dws_ref[g] += dws_g
            dbs = jnp.where(lane == g, col, dbs)
        dzu, dzv, dbu, dbv, dlg, dlb = vjp((du_sc[...], dvn_sc[...]))
        dzp_ref[:, :gw] = dzu.astype(dzp_ref.dtype)
        dzp_ref[:, gw:] = dzv.astype(dzp_ref.dtype)

        @pl.when(first)
        def _():
            dbin_ref[:, :gw] = dbu
            dbin_ref[:, gw:] = dbv
            dlg_ref[...] = dlg
            dlb_ref[...] = dlb
            dbs_ref[...] = dbs

        @pl.when(jnp.logical_not(first))
        def _():
            dbin_ref[:, :gw] += dbu
            dbin_ref[:, gw:] += dbv
            dlg_ref[...] += dlg
            dlb_ref[...] += dlb
            dbs_ref[...] += dbs

    full = lambda a: pl.BlockSpec(a.shape, lambda i: (0,) * a.ndim)
    fshape = lambda s: pl.BlockSpec(s, lambda i: (0,) * len(s))
    return _call(
        body, name=name, grid=(S // tb,),
        in_specs=[pl.BlockSpec((tb, gw), lambda i: (i, 0)), pl.BlockSpec((tb, gw2), lambda i: (i, 0)),
                  full(b_in), full(ln_g), full(ln_b), full(ws), full(ws_t), full(bs_t)],
        out_specs=[pl.BlockSpec((tb, gw2), lambda i: (i, 0)), fshape((1, gw2)), fshape((1, gw)), fshape((1, gw)),
                   fshape((G, ch, ch)), fshape((ch, G))],
        out_shape=[jax.ShapeDtypeStruct((S, gw2), BF16), jax.ShapeDtypeStruct((1, gw2), F32),
                   jax.ShapeDtypeStruct((1, gw), F32), jax.ShapeDtypeStruct((1, gw), F32),
                   jax.ShapeDtypeStruct((G, ch, ch), F32), jax.ShapeDtypeStruct((ch, G), F32)],
        scratch=[pltpu.VMEM((tb, gw), F32), pltpu.VMEM((tb, gw), F32)],
        sem=("arbitrary",))(dyg, zp, b_in, ln_g, ln_b, ws, ws_t, bs_t)


def _dot_01(x, ones_bf16):
    hi = x.astype(BF16)
    r1 = x - hi.astype(F32)
    mid = r1.astype(BF16)
    lo = (r1 - mid.astype(F32)).astype(BF16)
    dot = lambda t: jnp.dot(t, ones_bf16, preferred_element_type=F32)
    return dot(hi) + dot(mid) + dot(lo)


def _log_sigmoid(x):
    return jnp.minimum(x, 0.0) - jnp.log1p(jnp.exp(-jnp.abs(x)))


def _dcum_fwd(f_t, b_col, name):
    H, S = f_t.shape
    tb = _tile(S, 512, LANES)

    def body(f_ref, b_ref, o_ref, carry):
        @pl.when(pl.program_id(0) == 0)
        def _():
            carry[...] = jnp.zeros_like(carry)

        ls = _log_sigmoid(f_ref[...] + b_ref[...])
        r = lax.broadcasted_iota(jnp.int32, (tb, tb), 0)
        c = lax.broadcasted_iota(jnp.int32, (tb, tb), 1)
        upper = (r <= c).astype(BF16)
        o_ref[...] = _dot_01(ls, upper) + carry[...]
        carry[...] += jnp.sum(ls, axis=1, keepdims=True)

    return _call(body, name=name, grid=(S // tb,),
                 in_specs=[pl.BlockSpec((H, tb), lambda i: (0, i)), pl.BlockSpec((H, 1), lambda i: (0, 0))],
                 out_specs=pl.BlockSpec((H, tb), lambda i: (0, i)),
                 out_shape=jax.ShapeDtypeStruct((H, S), F32),
                 scratch=[pltpu.VMEM((H, 1), F32)], sem=("arbitrary",))(f_t, b_col)


def _dcum_bwd(dd_t, f_t, b_col, name):
    H, S = f_t.shape
    tb = _tile(S, 512, LANES)
    nb = S // tb

    def body(dd_ref, f_ref, b_ref, df_ref, db_ref, carry):
        first = pl.program_id(0) == 0

        @pl.when(first)
        def _():
            carry[...] = jnp.zeros_like(carry)

        dd = dd_ref[...]
        r = lax.broadcasted_iota(jnp.int32, (tb, tb), 0)
        c = lax.broadcasted_iota(jnp.int32, (tb, tb), 1)
        lower = (r >= c).astype(BF16)
        rev = _dot_01(dd, lower) + carry[...]
        carry[...] += jnp.sum(dd, axis=1, keepdims=True)
        df = rev * jax.nn.sigmoid(-(f_ref[...] + b_ref[...]))
        df_ref[...] = df
        part = jnp.sum(df, axis=1, keepdims=True)

        @pl.when(first)
        def _():
            db_ref[...] = part

        @pl.when(jnp.logical_not(first))
        def _():
            db_ref[...] += part

    return _call(body, name=name, grid=(nb,),
                 in_specs=[pl.BlockSpec((H, tb), lambda i: (0, nb - 1 - i)), pl.BlockSpec((H, tb), lambda i: (0, nb - 1 - i)),
                           pl.BlockSpec((H, 1), lambda i: (0, 0))],
                 out_specs=[pl.BlockSpec((H, tb), lambda i: (0, nb - 1 - i)), pl.BlockSpec((H, 1), lambda i: (0, 0))],
                 out_shape=[jax.ShapeDtypeStruct((H, S), F32), jax.ShapeDtypeStruct((H, 1), F32)],
                 scratch=[pltpu.VMEM((H, 1), F32)], sem=("arbitrary",))(dd_t, f_t, b_col)


def _attn_tile(S):
    return _tile(S, 512, LANES)


def _causal(t, transposed):
    r = lax.broadcasted_iota(jnp.int32, (t, t), 0)
    c = lax.broadcasted_iota(jnp.int32, (t, t), 1)
    return (r <= c) if transposed else (c <= r)


def _tri_pairs(n, key_major):
    if key_major:
        pairs = [(i, j) for j in range(n) for i in range(j, n)]
    else:
        pairs = [(i, j) for i in range(n) for j in range(i + 1)]
    return jnp.asarray([p[0] for p in pairs], jnp.int32), jnp.asarray([p[1] for p in pairs], jnp.int32)


def _split3(x):
    hi = lax.reduce_precision(x, 8, 7)
    r = x - hi
    mid = lax.reduce_precision(r, 8, 7)
    lo = lax.reduce_precision(r - mid, 8, 7)
    return hi.astype(BF16), mid.astype(BF16), lo.astype(BF16)


def _augment(xn, dcum, query):
    H, S, hd = xn.shape
    parts = list(_split3(dcum))
    vals = parts + [1.0] * 3 if query else [1.0] * 3 + [-p for p in parts]
    lane = lax.broadcasted_iota(jnp.int32, (1, 1, LANES), 2)
    out = jnp.pad(xn, ((0, 0), (0, 0), (0, LANES - hd)))
    for k, val in enumerate(vals):
        val = jnp.asarray(val, BF16)
        out = jnp.where(lane == hd + k, val[..., None] if val.ndim else val, out)
    return out


def _scores_t(k_ref, qt_ref, h, t, diag):
    st = jnp.dot(k_ref[h], qt_ref[h], preferred_element_type=F32)
    return jnp.where(_causal(t, True), st, MASKED) if diag else st


def _flash_fwd(ka, qat, vat, hd, name):
    H, S, da = ka.shape
    t = _attn_tile(S)
    hb = ATTN_HEADS_PER_STEP
    it, jt = _tri_pairs(S // t, False)

    def body(it_ref, jt_ref, k_ref, qt_ref, vt_ref, o_ref, lse_ref, m_sc, acc_sc):
        i, j = it_ref[pl.program_id(1)], jt_ref[pl.program_id(1)]

        @pl.when(j == 0)
        def _():
            m_sc[...] = jnp.full_like(m_sc, MASKED)
            acc_sc[...] = jnp.zeros_like(acc_sc)

        def step(diag):
            sts = [_scores_t(k_ref, qt_ref, h, t, diag) for h in range(hb)]
            pts, alphas = [], []
            for h in range(hb):
                m_prev = m_sc[h]
                m_new = jnp.maximum(m_prev, jnp.max(sts[h], axis=0, keepdims=True))
                pts.append(jnp.exp(sts[h] - m_new).astype(BF16))
                alphas.append(jnp.exp(m_prev - m_new))
                m_sc[h] = m_new
            for h in range(hb):
                acc_sc[h] = alphas[h] * acc_sc[h] + jnp.dot(vt_ref[h], pts[h], preferred_element_type=F32)

        @pl.when(j < i)
        def _():
            step(False)

        @pl.when(j == i)
        def _():
            step(True)
            for h in range(hb):
                l = acc_sc[h, hd:hd + 1, :]
                o_ref[h] = acc_sc[h, :hd, :] / l
                lse_ref[h] = m_sc[h] + jnp.log(l)

    qcol = lambda h, p, it_, jt_: (h, 0, it_[p])
    kcol = lambda h, p, it_, jt_: (h, 0, jt_[p])
    krow = lambda h, p, it_, jt_: (h, jt_[p], 0)
    return _call_prefetch(
        body, name=name, grid=(H // hb, it.shape[0]), n_prefetch=2,
        in_specs=[pl.BlockSpec((hb, t, da), krow), pl.BlockSpec((hb, da, t), qcol), pl.BlockSpec((hb, da, t), kcol)],
        out_specs=[pl.BlockSpec((hb, hd, t), qcol), pl.BlockSpec((hb, 1, t), qcol)],
        out_shape=[jax.ShapeDtypeStruct((H, hd, S), F32), jax.ShapeDtypeStruct((H, 1, S), F32)],
        scratch=[pltpu.VMEM((hb, 1, t), F32), pltpu.VMEM((hb, da, t), F32)],
        sem=("parallel", "arbitrary"))(it, jt, ka, qat, vat)


def _flash_bwd(ka, kat, qat, v, dot, o_tr, lse_r, name):
    H, S, hd = v.shape
    da = ka.shape[2]
    t = _attn_tile(S)
    n = S // t
    hb = ATTN_HEADS_PER_STEP
    it, jt = _tri_pairs(n, True)
    over_queries = (((1,), (1,)), ((), ()))

    def body(it_ref, jt_ref, k_ref, kt_ref, qt_ref, v_ref, dot_ref, o_ref, lse_ref, dq_ref, dk_ref, dv_ref, dk_sc, dv_sc):
        i, j = it_ref[pl.program_id(1)], jt_ref[pl.program_id(1)]

        @pl.when(pl.program_id(1) == 0)
        def _():
            dq_ref[...] = jnp.zeros_like(dq_ref)

        def step(diag):
            for h0 in range(0, hb, ATTN_STAGED_HEADS):
                hs = range(h0, min(h0 + ATTN_STAGED_HEADS, hb))
                sts = [_scores_t(k_ref, qt_ref, h, t, diag) for h in hs]
                dpts = [jnp.dot(v_ref[h], dot_ref[h], preferred_element_type=F32) for h in hs]
                tiles = []
                for h, st, dpt in zip(hs, sts, dpts):
                    dl = jnp.sum(dot_ref[h].astype(F32) * o_ref[h], axis=0, keepdims=True)
                    pt = jnp.exp(st - lse_ref[h])
                    tiles.append((pt.astype(BF16), (pt * (dpt - dl)).astype(BF16)))
                for h, (ptb, dsb) in zip(hs, tiles):
                    dv_sc[h] += lax.dot_general(dot_ref[h], ptb, over_queries, preferred_element_type=F32)
                    dk_sc[h] += lax.dot_general(qt_ref[h], dsb, over_queries, preferred_element_type=F32)
                    dq_ref[h, i] += jnp.dot(kt_ref[h], dsb, preferred_element_type=F32)

        @pl.when(i == j)
        def _():
            dk_sc[...] = jnp.zeros_like(dk_sc)
            dv_sc[...] = jnp.zeros_like(dv_sc)
            step(True)

        @pl.when(i > j)
        def _():
            step(False)

        @pl.when(i == n - 1)
        def _():
            dk_ref[...] = dk_sc[...]
            dv_ref[...] = dv_sc[...]

    krow = lambda h, p, it_, jt_: (h, jt_[p], 0)
    kcol = lambda h, p, it_, jt_: (h, 0, jt_[p])
    qcol = lambda h, p, it_, jt_: (h, 0, it_[p])
    return _call_prefetch(
        body, name=name, grid=(H // hb, it.shape[0]), n_prefetch=2,
        in_specs=[pl.BlockSpec((hb, t, da), krow), pl.BlockSpec((hb, da, t), kcol), pl.BlockSpec((hb, da, t), qcol),
                  pl.BlockSpec((hb, t, hd), krow), pl.BlockSpec((hb, hd, t), qcol), pl.BlockSpec((hb, hd, t), qcol),
                  pl.BlockSpec((hb, 1, t), qcol)],
        out_specs=[pl.BlockSpec((hb, n, da, t), lambda h, p, it_, jt_: (h, 0, 0, 0)), pl.BlockSpec((hb, da, t), kcol),
                   pl.BlockSpec((hb, hd, t), kcol)],
        out_shape=[jax.ShapeDtypeStruct((H, n, da, t), F32), jax.ShapeDtypeStruct((H, da, S), F32),
                   jax.ShapeDtypeStruct((H, hd, S), F32)],
        scratch=[pltpu.VMEM((hb, da, t), F32), pltpu.VMEM((hb, hd, t), F32)],
        sem=("parallel", "arbitrary"))(it, jt, ka, kat, qat, v, dot, o_tr, lse_r)


def _offsets(n_bits):
    return [tuple((k >> b) & 1 for b in reversed(range(n_bits))) for k in range(1, 1 << n_bits)]


def _own_slot(out, own, idx):
    return lax.dynamic_update_index_in_dim(out, own.astype(out.dtype), idx, 0)


def _gather8(arrs, name):
    n = len(arrs)
    offs = _offsets(3)

    def body(*refs):
        ins, outs = refs[:n], refs[n:2 * n]
        ssem, rsem = refs[2 * n:]
        x, y, c = _place()
        me = 4 * x + 2 * y + c
        copies = []
        for a in range(n):
            for k, (dx, dy, dcc) in enumerate(offs):
                cp = pltpu.make_async_remote_copy(
                    src_ref=ins[a], dst_ref=outs[a].at[me], send_sem=ssem.at[a, k], recv_sem=rsem.at[a, k],
                    device_id=((x + dx) % 2, (y + dy) % 2, (c + dcc) % 2), device_id_type=MESH)
                cp.start()
                copies.append(cp)
        for cp in copies:
            cp.wait()

    return _call(body, name=name, in_specs=[ANY] * n, out_specs=[ANY] * n,
                 out_shape=[jax.ShapeDtypeStruct((N_DEV,) + a.shape, a.dtype) for a in arrs],
                 scratch=[pltpu.SemaphoreType.DMA((n, 7)), pltpu.SemaphoreType.DMA((n, 7))])(*arrs)


def _chip_gather(arrs, halved, name):
    n = len(arrs)
    offs = _offsets(2)

    def body(*refs):
        ins, outs = refs[:n], refs[n:2 * n]
        ssem, rsem = refs[2 * n:]
        x, y, c = _place()
        chip = 2 * x + y
        copies = []
        for a in range(n):
            if halved:
                hn = arrs[a].shape[0] // 2
                src = ins[a].at[pl.ds(c * hn, hn)]
                dst = outs[a].at[chip, pl.ds(c * hn, hn)]
            else:
                src, dst = ins[a], outs[a].at[chip]
            for k, (dx, dy) in enumerate(offs):
                cp = pltpu.make_async_remote_copy(
                    src_ref=src, dst_ref=dst, send_sem=ssem.at[a, k], recv_sem=rsem.at[a, k],
                    device_id=((x + dx) % 2, (y + dy) % 2, c), device_id_type=MESH)
                cp.start()
                copies.append(cp)
        for cp in copies:
            cp.wait()

    return _call(body, name=name, in_specs=[ANY] * n, out_specs=[ANY] * n,
                 out_shape=[jax.ShapeDtypeStruct((N_CHIPS,) + a.shape, a.dtype) for a in arrs],
                 scratch=[pltpu.SemaphoreType.DMA((n, 3)), pltpu.SemaphoreType.DMA((n, 3))])(*arrs)


def _sibling_fill(bufs, name):
    n = len(bufs)
    offs = _offsets(2)

    def body(*refs):
        ins, outs = refs[:n], refs[n:2 * n]
        ssem, rsem = refs[2 * n:]
        x, y, c = _place()
        copies = []
        for a in range(n):
            hn = bufs[a].shape[1] // 2
            for k, (dx, dy) in enumerate(offs):
                chip = 2 * ((x + dx) % 2) + (y + dy) % 2
                cp = pltpu.make_async_remote_copy(
                    src_ref=ins[a].at[chip, pl.ds(c * hn, hn)], dst_ref=outs[a].at[chip, pl.ds(c * hn, hn)],
                    send_sem=ssem.at[a, k], recv_sem=rsem.at[a, k],
                    device_id=(x, y, 1 - c), device_id_type=MESH)
                cp.start()
                copies.append(cp)
        for cp in copies:
            cp.wait()

    return _call(body, name=name, in_specs=[ANY] * n, out_specs=[ANY] * n,
                 out_shape=[jax.ShapeDtypeStruct(b.shape, b.dtype) for b in bufs],
                 scratch=[pltpu.SemaphoreType.DMA((n, 3)), pltpu.SemaphoreType.DMA((n, 3))],
                 aliases={a: a for a in range(n)})(*bufs)


def _sibling_pair(arrs, name):
    n = len(arrs)

    def body(*refs):
        ins, outs = refs[:n], refs[n:2 * n]
        ssem, rsem = refs[2 * n:]
        x, y, c = _place()
        copies = []
        for a in range(n):
            cp = pltpu.make_async_remote_copy(
                src_ref=ins[a], dst_ref=outs[a].at[c], send_sem=ssem.at[a], recv_sem=rsem.at[a],
                device_id=(x, y, 1 - c), device_id_type=MESH)
            cp.start()
            copies.append(cp)
        for cp in copies:
            cp.wait()

    return _call(body, name=name, in_specs=[ANY] * n, out_specs=[ANY] * n,
                 out_shape=[jax.ShapeDtypeStruct((N_CORES,) + a.shape, a.dtype) for a in arrs],
                 scratch=[pltpu.SemaphoreType.DMA((n,)), pltpu.SemaphoreType.DMA((n,))])(*arrs)


def _piece(shape, spec, j, h):
    shard_ax, half_ax = spec
    w = shape[shard_ax] // N_CHIPS
    idx = [slice(None)] * len(shape)
    idx[shard_ax] = pl.ds(j * w, w)
    hn = (w if half_ax == shard_ax else shape[half_ax]) // 2
    assert half_ax != shard_ax
    idx[half_ax] = pl.ds(h * hn, hn)
    return tuple(idx)


def _piece_shape(shape, spec):
    shard_ax, half_ax = spec
    s = list(shape)
    s[shard_ax] //= N_CHIPS
    s[half_ax] //= 2
    return tuple(s)


def _own_pieces(g, spec, c):
    shard_ax, half_ax = spec
    hn = g.shape[half_ax] // 2
    half = lax.dynamic_slice_in_dim(g, c * hn, hn, axis=half_ax)
    shape = list(half.shape)
    shape[shard_ax:shard_ax + 1] = [N_CHIPS, shape[shard_ax] // N_CHIPS]
    return jnp.moveaxis(half.reshape(shape), shard_ax, 0)


def _sibling_scatter(arrs, specs, name):
    n = len(arrs)

    def body(*refs):
        ins, outs = refs[:n], refs[n:2 * n]
        ssem, rsem = refs[2 * n:]
        x, y, c = _place()
        for mine in range(N_CORES):
            @pl.when(c == mine)
            def _():
                copies = []
                for a in range(n):
                    for j in range(N_CHIPS):
                        cp = pltpu.make_async_remote_copy(
                            src_ref=ins[a].at[_piece(arrs[a].shape, specs[a], j, 1 - mine)], dst_ref=outs[a].at[j],
                            send_sem=ssem.at[a, j], recv_sem=rsem.at[a, j],
                            device_id=(x, y, 1 - mine), device_id_type=MESH)
                        cp.start()
                        copies.append(cp)
                for cp in copies:
                    cp.wait()

    return _call(body, name=name, in_specs=[ANY] * n, out_specs=[ANY] * n,
                 out_shape=[jax.ShapeDtypeStruct((N_CHIPS,) + _piece_shape(a.shape, s), a.dtype)
                            for a, s in zip(arrs, specs)],
                 scratch=[pltpu.SemaphoreType.DMA((n, N_CHIPS))] * 2)(*arrs)


def _chip_scatter(arrs, name):
    n = len(arrs)
    offs = _offsets(2)

    def body(*refs):
        ins, outs = refs[:n], refs[n:2 * n]
        ssem, rsem = refs[2 * n:]
        x, y, c = _place()
        chip = 2 * x + y
        copies = []
        for a in range(n):
            for k, (dx, dy) in enumerate(offs):
                tx, ty = (x + dx) % 2, (y + dy) % 2
                cp = pltpu.make_async_remote_copy(
                    src_ref=ins[a].at[2 * tx + ty], dst_ref=outs[a].at[chip], send_sem=ssem.at[a, k], recv_sem=rsem.at[a, k],
                    device_id=(tx, ty, c), device_id_type=MESH)
                cp.start()
                copies.append(cp)
        for cp in copies:
            cp.wait()

    return _call(body, name=name, in_specs=[ANY] * n, out_specs=[ANY] * n,
                 out_shape=[jax.ShapeDtypeStruct(a.shape, a.dtype) for a in arrs],
                 scratch=[pltpu.SemaphoreType.DMA((n, 3)), pltpu.SemaphoreType.DMA((n, 3))])(*arrs)


def kernel(x, c, ada_w, ada_b, pre_mix_g, post_mix_g, pre_ffn_g, post_ffn_g, ffn_w_gu, ffn_w_down, a_w_in, a_b_in, a_ln_g, a_ln_b, a_w_s, a_b_s, a_w_out, kv_ada_w, kv_ada_b, kv_norm_g, kv_w, kv_b_f, k_norm_g, b_w_qg, b_q_norm_g, b_w_o, loss_target, m_ada_w, m_ada_b, m_pre_mix_g, m_post_mix_g, m_pre_ffn_g, m_post_ffn_g, m_ffn_w_gu, m_ffn_w_down, m_a_w_in, m_a_b_in, m_a_ln_g, m_a_ln_b, m_a_w_s, m_a_b_s, m_a_w_out, m_kv_ada_w, m_kv_ada_b, m_kv_norm_g, m_kv_w, m_kv_b_f, m_k_norm_g, m_b_w_qg, m_b_q_norm_g, m_b_w_o, v_ada_w, v_ada_b, v_pre_mix_g, v_post_mix_g, v_pre_ffn_g, v_post_ffn_g, v_ffn_w_gu, v_ffn_w_down, v_a_w_in, v_a_b_in, v_a_ln_g, v_a_ln_b, v_a_w_s, v_a_b_s, v_a_w_out, v_kv_ada_w, v_kv_ada_b, v_kv_norm_g, v_kv_w, v_kv_b_f, v_k_norm_g, v_b_w_qg, v_b_q_norm_g, v_b_w_o):
    weights = dict(ada_w=ada_w, ada_b=ada_b, pre_mix_g=pre_mix_g, post_mix_g=post_mix_g, pre_ffn_g=pre_ffn_g,
                   post_ffn_g=post_ffn_g, ffn_w_gu=ffn_w_gu, ffn_w_down=ffn_w_down, a_w_in=a_w_in, a_b_in=a_b_in,
                   a_ln_g=a_ln_g, a_ln_b=a_ln_b, a_w_s=a_w_s, a_b_s=a_b_s, a_w_out=a_w_out, kv_ada_w=kv_ada_w,
                   kv_ada_b=kv_ada_b, kv_norm_g=kv_norm_g, kv_w=kv_w, kv_b_f=kv_b_f, k_norm_g=k_norm_g, b_w_qg=b_w_qg,
                   b_q_norm_g=b_q_norm_g, b_w_o=b_w_o)
    m_in = dict(ada_w=m_ada_w, ada_b=m_ada_b, pre_mix_g=m_pre_mix_g, post_mix_g=m_post_mix_g, pre_ffn_g=m_pre_ffn_g,
                post_ffn_g=m_post_ffn_g, ffn_w_gu=m_ffn_w_gu, ffn_w_down=m_ffn_w_down, a_w_in=m_a_w_in, a_b_in=m_a_b_in,
                a_ln_g=m_a_ln_g, a_ln_b=m_a_ln_b, a_w_s=m_a_w_s, a_b_s=m_a_b_s, a_w_out=m_a_w_out, kv_ada_w=m_kv_ada_w,
                kv_ada_b=m_kv_ada_b, kv_norm_g=m_kv_norm_g, kv_w=m_kv_w, kv_b_f=m_kv_b_f, k_norm_g=m_k_norm_g,
                b_w_qg=m_b_w_qg, b_q_norm_g=m_b_q_norm_g, b_w_o=m_b_w_o)
    v_in = dict(ada_w=v_ada_w, ada_b=v_ada_b, pre_mix_g=v_pre_mix_g, post_mix_g=v_post_mix_g, pre_ffn_g=v_pre_ffn_g,
                post_ffn_g=v_post_ffn_g, ffn_w_gu=v_ffn_w_gu, ffn_w_down=v_ffn_w_down, a_w_in=v_a_w_in, a_b_in=v_a_b_in,
                a_ln_g=v_a_ln_g, a_ln_b=v_a_ln_b, a_w_s=v_a_w_s, a_b_s=v_a_b_s, a_w_out=v_a_w_out, kv_ada_w=v_kv_ada_w,
                kv_ada_b=v_kv_ada_b, kv_norm_g=v_kv_norm_g, kv_w=v_kv_w, kv_b_f=v_kv_b_f, k_norm_g=v_k_norm_g,
                b_w_qg=v_b_w_qg, b_q_norm_g=v_b_q_norm_g, b_w_o=v_b_w_o)
    names = list(weights)

    S, D = x.shape[1], x.shape[2]
    L, NA, NB = ada_w.shape[0], a_w_in.shape[0], b_w_qg.shape[0]
    H = kv_b_f.shape[0]
    hd = D // H
    G, CH = a_w_s.shape[1], a_w_s.shape[2]
    GW = a_w_out.shape[1] * N_CHIPS
    F = ffn_w_down.shape[1] * N_CHIPS
    ada_cols = ada_w.shape[2]
    kvada_cols = kv_ada_w.shape[1]
    kv_cols = kv_w.shape[1]
    kv_pad = -(-(2 * D + H) // LANES) * LANES
    xi, yi, ci = _place()
    chip = 2 * xi + yi
    me = 2 * chip + ci
    x0 = x[0]
    tgt = loss_target[0]
    row = lambda t: t.reshape(1, -1)

    c_all = _own_slot(_gather8([c], "gather_c")[0], c, me).reshape(N_DEV, D)
    c_act = _silu_rows(jnp.pad(c_all, ((0, BF16_ROWS - N_DEV), (0, 0))), "silu_c")
    mod_sh = [_mm(c_act, (ada_w, l), "nn", F32, f"mod_proj_{l}") for l in range(L)]
    mod_sh.append(_mm(c_act, kv_ada_w, "nn", F32, "mod_proj_kv"))
    mod_sh = jnp.concatenate(mod_sh, axis=1)
    small_sh = [mod_sh, a_b_in, a_ln_g, a_ln_b]
    mod_all, b_in_all, ln_g_all, ln_b_all = [
        _own_slot(o, s, chip) for o, s in zip(_chip_gather(small_sh, False, "gather_mod"), small_sh)]
    mine = lax.dynamic_index_in_dim(mod_all, me, axis=1, keepdims=False)
    mod = [jnp.concatenate([mine[j, l * ada_cols:(l + 1) * ada_cols] for j in range(N_CHIPS)]) + ada_b[l] for l in range(L)]
    mod = [[row(t) for t in jnp.split(m_, 6)] for m_ in mod]
    mod_kv = jnp.concatenate([mine[j, L * ada_cols:] for j in range(N_CHIPS)]) + kv_ada_b
    kv_sh, kv_sc = [row(t) for t in jnp.split(mod_kv, 2)]
    cat_chips = lambda t, ax: jnp.concatenate([t[j] for j in range(N_CHIPS)], axis=ax)
    b_in_f = cat_chips(b_in_all, 1)
    ln_g_f, ln_b_f = cat_chips(ln_g_all, 1), cat_chips(ln_b_all, 1)

    big = ["ffn_w_gu", "ffn_w_down", "a_w_in", "a_w_out", "kv_w", "b_w_qg", "b_w_o"]
    own_w = [weights[n].astype(BF16) for n in big]
    gathered = _sibling_fill(_chip_gather(own_w, True, "gather_w"), "fill_w")
    gathered = {n: _own_slot(g, w, chip) for n, g, w in zip(big, gathered, own_w)}
    w_gu = cat_chips(gathered["ffn_w_gu"], 2)
    w_dn = cat_chips(gathered["ffn_w_down"], 1)
    w_in = cat_chips(gathered["a_w_in"], 2)
    w_out = cat_chips(gathered["a_w_out"], 1)
    w_kv = jnp.pad(cat_chips(gathered["kv_w"], 1), ((0, 0), (0, kv_pad - (2 * D + H))))
    w_qg = cat_chips(gathered["b_w_qg"], 2)
    w_o = cat_chips(gathered["b_w_o"], 1)

    causal = jnp.tril(jnp.ones((CH, CH), F32))
    ws_m = [(a_w_s[i] * causal).astype(BF16) for i in range(NA)]
    ws_mt = [jnp.swapaxes(w, 1, 2) for w in ws_m]
    bs_t = [a_b_s[i].T for i in range(NA)]

    heads = lambda t: t.reshape(S, H, hd).transpose(1, 0, 2)
    unheads = lambda t: t.transpose(1, 0, 2).reshape(S, D)

    saved = []
    kv = None
    xc = x0
    for l in range(L):
        sh_m, sc_m, g_m, sh_f, sc_f, g_f = mod[l]
        st = {"x0": xc}
        h1 = _norm_mod_fwd(xc, row(pre_mix_g[l]), sh_m, sc_m, f"pre_mix_{l}")
        st["h1"] = h1
        if l < NA:
            zp = _mm(h1, (w_in, l), "nn", BF16, f"gmlp_in_{l}")
            yg = _gmlp_fwd(zp, row(b_in_f[l]), row(ln_g_f[l]), row(ln_b_f[l]), ws_m[l], bs_t[l], f"gmlp_gate_{l}")
            y = _mm(yg, (w_out, l), "nn", F32, f"gmlp_out_{l}")
            st.update(zp=zp, yg=yg)
        else:
            jb = l - NA
            qg = _mm(h1, (w_qg, jb), "nn", BF16, f"fox_qg_{jb}")
            q_raw = heads(qg[:, :D]).reshape(H * S, hd)
            qn = _head_norm_fwd(q_raw, row(b_q_norm_g[jb]), hd ** -0.5, f"fox_qnorm_{jb}").reshape(H, S, hd)
            qa = _augment(qn, kv["dcum"], True)
            qat = jnp.swapaxes(qa, 1, 2)
            o_tr, lse_r = _flash_fwd(kv["ka"], qat, kv["vat"], hd, f"fox_attn_{jb}")
            o_t = o_tr.transpose(2, 0, 1).reshape(S, D)
            og = _out_gate_fwd(o_t, qg, f"fox_gate_{jb}")
            y = _mm(og, (w_o, jb), "nn", F32, f"fox_out_{jb}")
            st.update(qg=qg, q_raw=q_raw, qat=qat, o_tr=o_tr, lse_r=lse_r, o_t=o_t, og=og)
        st["y"] = y
        x1 = _post_fwd(xc, y, row(post_mix_g[l]), g_m, f"post_mix_{l}")
        st["x1"] = x1
        h2 = _norm_mod_fwd(x1, row(pre_ffn_g[l]), sh_f, sc_f, f"pre_ffn_{l}")
        gu = _mm(h2, (w_gu, l), "nn", BF16, f"ffn_gu_{l}")
        act = _swiglu_fwd(gu, f"ffn_act_{l}")
        y2 = _mm(act, (w_dn, l), "nn", F32, f"ffn_down_{l}")
        xc = _post_fwd(x1, y2, row(post_ffn_g[l]), g_f, f"post_ffn_{l}")
        st.update(h2=h2, gu=gu, act=act, y2=y2)
        saved.append(st)
        if l == NA - 1:
            hk = _norm_mod_fwd(xc, row(kv_norm_g), kv_sh, kv_sc, "kv_pre")
            kvf = _mm(hk, w_kv, "nn", F32, "kv_proj")
            k_raw = heads(kvf[:, :D]).reshape(H * S, hd)
            kn = _head_norm_fwd(k_raw, row(k_norm_g), 1.0, "kv_knorm").reshape(H, S, hd)
            vb = heads(kvf[:, D:2 * D]).astype(BF16)
            f_t = kvf[:, 2 * D:2 * D + H].T
            b_col = kv_b_f.reshape(H, 1)
            dcum = _dcum_fwd(f_t, b_col, "kv_dcum")
            vt = kvf[:, D:2 * D].astype(BF16).reshape(S, H, hd).transpose(1, 2, 0)
            vat = jnp.where(lax.broadcasted_iota(jnp.int32, (1, LANES, 1), 1) == hd, jnp.asarray(1, BF16),
                            jnp.pad(vt, ((0, 0), (0, LANES - hd), (0, 0))))
            ka = _augment(kn, dcum, False)
            kv = dict(x=xc, hk=hk, k_raw=k_raw, ka=ka, kat=jnp.swapaxes(ka, 1, 2), vb=vb, vat=vat,
                      f_t=f_t, b_col=b_col, dcum=dcum)

    dx, loss_part = _loss_bwd(xc, tgt, "loss")
    loss = lax.psum(loss_part[0, 0], ("x", "y", "c"))

    gl = {n: [None] * weights[n].shape[0] for n in
          ["pre_mix_g", "post_mix_g", "pre_ffn_g", "post_ffn_g", "ffn_w_gu", "ffn_w_down", "a_w_in", "a_b_in", "a_ln_g",
           "a_ln_b", "a_w_s", "a_b_s", "a_w_out", "b_w_qg", "b_q_norm_g", "b_w_o"]}
    dmod = [None] * L
    dkn = dvb = ddc = None
    gkv = {}
    for l in reversed(range(L)):
        st = saved[l]
        sh_m, sc_m, g_m, sh_f, sc_f, g_f = mod[l]
        if l == NA - 1:
            dk_raw, gkv["k_norm_g"] = _head_norm_bwd(jnp.swapaxes(dkn, 1, 2).reshape(H * S, hd), kv["k_raw"], row(k_norm_g),
                                                     1.0, "kv_knorm_bwd")
            df_t, db_f = _dcum_bwd(ddc.reshape(H, S), kv["f_t"], kv["b_col"], "kv_dcum_bwd")
            dkvf = jnp.concatenate([unheads(dk_raw.reshape(H, S, hd)), dvb.transpose(2, 0, 1).reshape(S, D), df_t.T,
                                    jnp.zeros((S, kv_pad - (2 * D + H)), F32)], axis=1).astype(BF16)
            gkv["kv_w"] = _mm(kv["hk"], dkvf, "tn", BF16, "kv_proj_dw")[:, :2 * D + H]
            dhk = _mm(dkvf, w_kv, "nt", F32, "kv_proj_dx")
            dx, gkv["kv_norm_g"], dsh, dsc = _norm_mod_bwd(dx, dhk, kv["x"], row(kv_norm_g), kv_sh, kv_sc, "kv_pre_bwd")
            gkv["kv_b_f"] = db_f.reshape(H)
            dmod_kv = jnp.concatenate([dsh, dsc], axis=1)
        dy2, gl["post_ffn_g"][l], dg_f = _post_bwd(dx, st["y2"], row(post_ffn_g[l]), g_f, f"post_ffn_bwd_{l}")
        gl["ffn_w_down"][l] = _mm(st["act"], dy2, "tn", BF16, f"ffn_down_dw_{l}")
        dact = _mm(dy2, (w_dn, l), "nt", BF16, f"ffn_down_dx_{l}")
        dgu = _swiglu_bwd(dact, st["gu"], f"ffn_act_bwd_{l}")
        gl["ffn_w_gu"][l] = _mm(st["h2"], dgu, "tn", BF16, f"ffn_gu_dw_{l}")
        dh2 = _mm(dgu, (w_gu, l), "nt", F32, f"ffn_gu_dx_{l}")
        dx, gl["pre_ffn_g"][l], dsh_f, dsc_f = _norm_mod_bwd(dx, dh2, st["x1"], row(pre_ffn_g[l]), sh_f, sc_f, f"pre_ffn_bwd_{l}")
        dy, gl["post_mix_g"][l], dg_m = _post_bwd(dx, st["y"], row(post_mix_g[l]), g_m, f"post_mix_bwd_{l}")
        if l < NA:
            gl["a_w_out"][l] = _mm(st["yg"], dy, "tn", BF16, f"gmlp_out_dw_{l}")
            dyg = _mm(dy, (w_out, l), "nt", BF16, f"gmlp_out_dx_{l}")
            dzp, db_in, dlg, dlb, dws, dbs_t = _gmlp_bwd(dyg, st["zp"], row(b_in_f[l]), row(ln_g_f[l]), row(ln_b_f[l]),
                                                           ws_m[l], ws_mt[l], bs_t[l], f"gmlp_gate_bwd_{l}")
            gl["a_b_in"][l], gl["a_ln_g"][l], gl["a_ln_b"][l] = db_in[0], dlg[0], dlb[0]
            gl["a_w_s"][l], gl["a_b_s"][l] = dws * causal, dbs_t.T
            gl["a_w_in"][l] = _mm(st["h1"], dzp, "tn", BF16, f"gmlp_in_dw_{l}")
            dh1 = _mm(dzp, (w_in, l), "nt", F32, f"gmlp_in_dx_{l}")
        else:
            jb = l - NA
            gl["b_w_o"][jb] = _mm(st["og"], dy, "tn", BF16, f"fox_out_dw_{jb}")
            dog = _mm(dy, (w_o, jb), "nt", F32, f"fox_out_dx_{jb}")
            do_t, dgl = _out_gate_bwd(dog, st["o_t"], st["qg"], f"fox_gate_bwd_{jb}")
            dot = do_t.reshape(S, H, hd).transpose(1, 2, 0)
            dqa_tr, dka_tr, dv_j = _flash_bwd(kv["ka"], kv["kat"], st["qat"], kv["vb"], dot, st["o_tr"], st["lse_r"],
                                              f"fox_attn_bwd_{jb}")
            dqn = dqa_tr[:, :, :hd, :].transpose(0, 1, 3, 2).reshape(H, S, hd)
            dk_j = dka_tr[:, :hd, :]
            dd_j = dqa_tr[:, :, hd, :].reshape(H, S) - dka_tr[:, hd + 3, :]
            dkn = dk_j if dkn is None else dkn + dk_j
            dvb = dv_j if dvb is None else dvb + dv_j
            ddc = dd_j if ddc is None else ddc + dd_j
            dq_raw, dgq = _head_norm_bwd(dqn.reshape(H * S, hd), st["q_raw"], row(b_q_norm_g[jb]), hd ** -0.5, f"fox_qnorm_bwd_{jb}")
            gl["b_q_norm_g"][jb] = dgq[0]
            dqg = jnp.concatenate([unheads(dq_raw.reshape(H, S, hd)).astype(BF16), dgl], axis=1)
            gl["b_w_qg"][jb] = _mm(st["h1"], dqg, "tn", BF16, f"fox_qg_dw_{jb}")
            dh1 = _mm(dqg, (w_qg, jb), "nt", F32, f"fox_qg_dx_{jb}")
        dx, gl["pre_mix_g"][l], dsh_m, dsc_m = _norm_mod_bwd(dx, dh1, st["x0"], row(pre_mix_g[l]), sh_m, sc_m, f"pre_mix_bwd_{l}")
        dmod[l] = jnp.concatenate([dsh_m, dsc_m, dg_m, dsh_f, dsc_f, dg_f], axis=1)
    grad_x = dx[None]

    stack = lambda n: jnp.stack([t.reshape(weights[n].shape[1:]) for t in gl[n]])
    small = {"dmod": jnp.concatenate(dmod, axis=1), "dmod_kv": dmod_kv}
    for n in ["pre_mix_g", "post_mix_g", "pre_ffn_g", "post_ffn_g", "a_w_s", "a_b_s", "b_q_norm_g"]:
        small[n] = stack(n)
    for n in ["a_b_in", "a_ln_g", "a_ln_b"]:
        small[n] = jnp.stack(gl[n])
    for n in ["kv_norm_g", "kv_b_f", "k_norm_g"]:
        small[n] = gkv[n]
    sizes = {n: t.size for n, t in small.items()}
    flat = jnp.concatenate([t.reshape(-1).astype(F32) for t in small.values()])
    rows_small = -(-flat.size // (LANES * BF16_ROWS)) * BF16_ROWS
    flat = jnp.pad(flat, (0, rows_small * LANES - flat.size)).reshape(rows_small, LANES)
    flat_all = _own_slot(_gather8([flat], "gather_small")[0], flat, me)
    flat_sum = _sum_slots(flat_all, "sum_small").reshape(-1)
    offs, o_ = {}, 0
    for n, sz in sizes.items():
        offs[n] = o_
        o_ += sz
    take = lambda n, shape: flat_sum[offs[n]:offs[n] + sizes[n]].reshape(shape)
    dmod_rows = flat_all.reshape(N_DEV, -1)[:, offs["dmod"]:offs["dmod"] + sizes["dmod"] + sizes["dmod_kv"]]
    dmod_rows = jnp.pad(dmod_rows, ((0, BF16_ROWS - N_DEV), (0, 0)))

    grads = {}
    grads["ada_b"] = take("dmod", (L, 6 * D))
    grads["kv_ada_b"] = take("dmod_kv", (2 * D,))
    for n in ["pre_mix_g", "post_mix_g", "pre_ffn_g", "post_ffn_g", "a_w_s", "a_b_s", "b_q_norm_g", "kv_norm_g", "kv_b_f", "k_norm_g"]:
        grads[n] = take(n, weights[n].shape)
    for n in ["a_b_in", "a_ln_g", "a_ln_b"]:
        full = take(n, small[n].shape)
        w = weights[n].shape[1]
        grads[n] = lax.dynamic_slice_in_dim(full, chip * w, w, axis=1)
    ada_g = []
    for l in range(L):
        cols = lax.dynamic_slice_in_dim(dmod_rows[:, l * 6 * D:(l + 1) * 6 * D], chip * ada_cols, ada_cols, axis=1)
        ada_g.append(_mm(c_act, cols, "tn", F32, f"mod_proj_dw_{l}"))
    grads["ada_w"] = jnp.stack(ada_g)
    cols = lax.dynamic_slice_in_dim(dmod_rows[:, L * 6 * D:], chip * kvada_cols, kvada_cols, axis=1)
    grads["kv_ada_w"] = _mm(c_act, cols, "tn", F32, "mod_proj_kv_dw")

    specs = {"ffn_w_gu": (2, 0), "ffn_w_down": (1, 0), "a_w_in": (2, 0), "a_w_out": (1, 0), "kv_w": (0, 1),
             "b_w_qg": (2, 0), "b_w_o": (1, 0)}
    full_g = {n: jnp.stack(gl[n]) for n in big if n != "kv_w"}
    full_g["kv_w"] = gkv["kv_w"].reshape(D, N_CHIPS, kv_cols).transpose(1, 0, 2)
    from_core = _sibling_scatter([full_g[n] for n in big], [specs[n] for n in big], "scatter_g_core")
    chip_sums = [_sum_pair(_own_pieces(full_g[n], specs[n], ci), r, f"sum_g_core_{n}") for n, r in zip(big, from_core)]
    recv = _chip_scatter(chip_sums, "scatter_g_chip")
    recv = [_own_slot(r, lax.dynamic_index_in_dim(p, chip, 0, keepdims=False), chip) for r, p in zip(recv, chip_sums)]
    halves = [_sum_slots(r, f"sum_g_{n}") for n, r in zip(big, recv)]
    pairs = _sibling_pair(halves, "pair_g")
    for n, p, hlf in zip(big, pairs, halves):
        grads[n] = _own_slot(p, hlf, ci).reshape(weights[n].shape)

    outs_d, outs_m, outs_v = {}, {}, {}
    for n in names:
        w2 = weights[n] if weights[n].ndim > 1 else weights[n].reshape(1, -1)
        shp = w2.shape
        d_, m_, v_ = _adamw(w2, grads[n].reshape(shp), m_in[n].reshape(shp), v_in[n].reshape(shp), f"adamw_{n}")
        outs_d[n], outs_m[n], outs_v[n] = (t.reshape(weights[n].shape) for t in (d_, m_, v_))
    return (loss, grad_x, *[grads[n] for n in names], *[outs_d[n] for n in names],
            *[outs_m[n] for n in names], *[outs_v[n] for n in names])
```

```python
import functools

import jax
import jax.numpy as jnp
from jax import lax
from jax.experimental import pallas as pl
from jax.experimental.pallas import tpu as pltpu

F32 = jnp.float32
BF16 = jnp.bfloat16
MESH = pl.DeviceIdType.MESH
NORM_EPS = 1e-6
MASKED = -1e30
LANES = 128
BF16_ROWS = 16
ROW_BLOCK_BYTES = 12 << 20
ADAM_LR, ADAM_B1, ADAM_B2, ADAM_EPS, ADAM_WD, ADAM_STEP = 0.001, 0.9, 0.999, 1e-08, 0.01, 10
N_CHIPS, N_CORES, N_DEV = 4, 2, 8
ATTN_HEADS_PER_STEP = 4
ATTN_STAGED_HEADS = 2
ANY = pl.BlockSpec(memory_space=pl.ANY)


def _tile(n, cap, quantum):
    best = None
    d = quantum
    while d <= min(n, cap):
        if n % d == 0:
            best = d
        d += quantum
    return n if best is None else best


def _call(body, *, name, out_shape, grid=(), in_specs=None, out_specs=None, scratch=(), sem=None, aliases=None):
    params = {} if sem is None else {"dimension_semantics": sem}
    return pl.pallas_call(
        body, name=name, grid=grid, in_specs=in_specs, out_specs=out_specs, out_shape=out_shape,
        scratch_shapes=list(scratch), input_output_aliases=aliases or {},
        compiler_params=pltpu.CompilerParams(**params))


def _call_prefetch(body, *, name, out_shape, grid, n_prefetch, in_specs, out_specs, scratch, sem):
    spec = pltpu.PrefetchScalarGridSpec(num_scalar_prefetch=n_prefetch, grid=grid, in_specs=in_specs,
                                        out_specs=out_specs, scratch_shapes=list(scratch))
    return pl.pallas_call(
        body, name=name, grid_spec=spec, out_shape=out_shape,
        compiler_params=pltpu.CompilerParams(dimension_semantics=sem))


def _place():
    x, y, c = lax.axis_index("x"), lax.axis_index("y"), lax.axis_index("c")
    return x, y, c


def _mm(a, b, mode, out_dtype, name):
    b_arr, b_idx = b if isinstance(b, tuple) else (b, None)
    bs = b_arr.shape[-2:]
    if mode == "nn":
        (M, K), (K2, N) = a.shape, bs
        dims = (((1,), (0,)), ((), ()))
    elif mode == "nt":
        (M, K), (N, K2) = a.shape, bs
        dims = (((1,), (1,)), ((), ()))
    else:
        (K, M), (K2, N) = a.shape, bs
        dims = (((0,), (0,)), ((), ()))
    assert K == K2, (name, a.shape, b_arr.shape)
    if mode == "tn":
        tm = _tile(M, 1408, LANES)
        tk = _tile(K, 2048, BF16_ROWS)
        tn = _tile(N, 512, LANES)
    else:
        tm = _tile(M, 1024, BF16_ROWS)
        tk = K if K <= 2816 else _tile(K, 2816, LANES)
        tn = _tile(N, 1408 if tk <= 1024 else 512, LANES)
    if tn < 256:
        tn = N
        tm = _tile(M, 512, LANES if mode == "tn" else BF16_ROWS)
    nk = K // tk
    grid = (M // tm, N // tn, nk)

    if mode == "tn":
        a_spec = pl.BlockSpec((tk, tm), lambda i, j, k: (k, i))
    else:
        a_spec = pl.BlockSpec((tm, tk), lambda i, j, k: (i, k))
    if mode == "nt":
        b_blk, b_map = (tn, tk), (lambda i, j, k: (j, k))
    else:
        b_blk, b_map = (tk, tn), (lambda i, j, k: (k, j))
    if b_idx is None:
        b_spec = pl.BlockSpec(b_blk, b_map)
    else:
        b_spec = pl.BlockSpec((None,) + b_blk, lambda i, j, k: (b_idx,) + b_map(i, j, k))

    def body(a_ref, b_ref, o_ref, *acc):
        r = lax.dot_general(a_ref[...].astype(BF16), b_ref[...].astype(BF16), dims, preferred_element_type=F32)
        if nk == 1:
            o_ref[...] = r.astype(o_ref.dtype)
        else:
            k = pl.program_id(2)

            @pl.when(k == 0)
            def _():
                acc[0][...] = r

            @pl.when(k > 0)
            def _():
                acc[0][...] += r

            @pl.when(k == nk - 1)
            def _():
                o_ref[...] = acc[0][...].astype(o_ref.dtype)

    return _call(
        body, name=name, grid=grid, in_specs=[a_spec, b_spec],
        out_specs=pl.BlockSpec((tm, tn), lambda i, j, k: (i, j)),
        out_shape=jax.ShapeDtypeStruct((M, N), out_dtype),
        scratch=[pltpu.VMEM((tm, tn), F32)] if nk > 1 else [],
        sem=("parallel", "parallel", "arbitrary"))(a, b_arr)


def _rowwise(fn, rows, pars, outs, pouts, name):
    R = rows[0].shape[0]
    row_bytes = 4 * (sum(max(r.shape[1], LANES) for r in rows) + sum(max(c, LANES) for c, _ in outs))
    tb = _tile(R, max(BF16_ROWS, ROW_BLOCK_BYTES // row_bytes), BF16_ROWS)
    nr, npar, no = len(rows), len(pars), len(outs)

    def body(*refs):
        r_in, p_in = refs[:nr], refs[nr:nr + npar]
        r_out, p_out = refs[nr + npar:nr + npar + no], refs[nr + npar + no:]
        ro, po = fn([r[...] for r in r_in], [p[...] for p in p_in])
        for ref, val in zip(r_out, ro):
            if isinstance(val, (tuple, list)):
                off = 0
                for piece in val:
                    w = piece.shape[1]
                    ref[:, off:off + w] = piece.astype(ref.dtype)
                    off += w
            else:
                ref[...] = val.astype(ref.dtype)
        if p_out:
            first = pl.program_id(0) == 0

            @pl.when(first)
            def _():
                for ref, val in zip(p_out, po):
                    ref[...] = val

            @pl.when(jnp.logical_not(first))
            def _():
                for ref, val in zip(p_out, po):
                    ref[...] += val

    res = _call(
        body, name=name, grid=(R // tb,),
        in_specs=[pl.BlockSpec((tb, r.shape[1]), lambda i: (i, 0)) for r in rows]
        + [pl.BlockSpec(p.shape, lambda i: (0, 0)) for p in pars],
        out_specs=[pl.BlockSpec((tb, c), lambda i: (i, 0)) for c, _ in outs]
        + [pl.BlockSpec(s, lambda i: (0, 0)) for s in pouts],
        out_shape=[jax.ShapeDtypeStruct((R, c), dt) for c, dt in outs]
        + [jax.ShapeDtypeStruct(s, F32) for s in pouts],
        sem=("arbitrary",) if pouts else ("parallel",))(*rows, *pars)
    return list(res)


def _rms(x, g):
    return x * lax.rsqrt(jnp.mean(x * x, axis=-1, keepdims=True) + NORM_EPS) * g


def _norm_mod(x, g, sh, sc):
    return _rms(x, g) * (1.0 + sc) + sh


def _gated_post(y, g, gate):
    return gate * _rms(y, g)


def _norm_mod_fwd(x, g, sh, sc, name):
    return _rowwise(lambda r, p: ([_norm_mod(r[0], *p)], []), [x], [g, sh, sc], [(x.shape[1], BF16)], [], name)[0]


def _norm_mod_bwd(dxo, dh, x, g, sh, sc, name):
    def fn(r, p):
        _, vjp = jax.vjp(_norm_mod, r[2], *p)
        dx, dg, dsh, dsc = vjp(r[1].astype(F32))
        return [r[0] + dx], [dg, dsh, dsc]
    c = x.shape[1]
    return _rowwise(fn, [dxo, dh, x], [g, sh, sc], [(c, F32)], [(1, c)] * 3, name)


def _post_fwd(x, y, g, gate, name):
    return _rowwise(lambda r, p: ([r[0] + _gated_post(r[1].astype(F32), *p)], []), [x, y], [g, gate],
                    [(x.shape[1], F32)], [], name)[0]


def _post_bwd(dxo, y, g, gate, name):
    def fn(r, p):
        _, vjp = jax.vjp(_gated_post, r[1].astype(F32), *p)
        dy, dg, dgate = vjp(r[0])
        return [dy], [dg, dgate]
    c = y.shape[1]
    return _rowwise(fn, [dxo, y], [g, gate], [(c, BF16)], [(1, c)] * 2, name)


def _swiglu(g, u):
    return jax.nn.silu(g) * u


def _ffn_up(h, w, l, hw, name):
    S, D = h.shape
    nb = w.shape[2] // (2 * hw)
    tm = _tile(S, 512, BF16_ROWS)

    def body(h_ref, w_ref, gu_ref, act_ref):
        gu = jnp.dot(h_ref[...], w_ref[...], preferred_element_type=F32).astype(BF16)
        gu_ref[...] = gu
        act_ref[...] = _swiglu(gu[:, :hw].astype(F32), gu[:, hw:].astype(F32)).astype(BF16)

    return _call(body, name=name, grid=(S // tm, nb),
                 in_specs=[pl.BlockSpec((tm, D), lambda i, j: (i, 0)), pl.BlockSpec((None, D, 2 * hw), lambda i, j: (l, 0, j))],
                 out_specs=[pl.BlockSpec((tm, 2 * hw), lambda i, j: (i, j)), pl.BlockSpec((tm, hw), lambda i, j: (i, j))],
                 out_shape=[jax.ShapeDtypeStruct((S, 2 * hw * nb), BF16), jax.ShapeDtypeStruct((S, hw * nb), BF16)],
                 sem=("parallel", "parallel"))(h, w)


def _ffn_down_dx(dy, w_dn, l, gu, hw, name):
    S, D = dy.shape
    nb = gu.shape[1] // (2 * hw)
    tm = _tile(S, 512, BF16_ROWS)

    def body(dy_ref, w_ref, gu_ref, dgu_ref):
        dact = lax.dot_general(dy_ref[...], w_ref[...], (((1,), (1,)), ((), ())), preferred_element_type=F32)
        _, vjp = jax.vjp(_swiglu, gu_ref[:, :hw].astype(F32), gu_ref[:, hw:].astype(F32))
        dg, du = vjp(dact)
        dgu_ref[:, :hw] = dg.astype(BF16)
        dgu_ref[:, hw:] = du.astype(BF16)

    return _call(body, name=name, grid=(S // tm, nb),
                 in_specs=[pl.BlockSpec((tm, D), lambda i, j: (i, 0)), pl.BlockSpec((None, hw, D), lambda i, j: (l, j, 0)),
                           pl.BlockSpec((tm, 2 * hw), lambda i, j: (i, j))],
                 out_specs=pl.BlockSpec((tm, 2 * hw), lambda i, j: (i, j)),
                 out_shape=jax.ShapeDtypeStruct(gu.shape, BF16), sem=("parallel", "parallel"))(dy, w_dn, gu)


def _silu_rows(c, name):
    return _rowwise(lambda r, p: ([jax.nn.silu(r[0])], []), [c], [], [(c.shape[1], F32)], [], name)[0]


def _head_norm(x, g, scale):
    return _rms(x, g) * scale


def _head_norm_fwd(x, g, scale, name):
    return _rowwise(lambda r, p: ([_head_norm(r[0].astype(F32), p[0], scale)], []), [x], [g],
                    [(x.shape[1], BF16)], [], name)[0]


def _head_norm_bwd(dy, x, g, scale, name):
    def fn(r, p):
        _, vjp = jax.vjp(lambda t, gg: _head_norm(t, gg, scale), r[1].astype(F32), p[0])
        dx, dg = vjp(r[0])
        return [dx], [dg]
    c = x.shape[1]
    return _rowwise(fn, [dy, x], [g], [(c, F32)], [(1, c)], name)


def _out_gate_fwd(o, qg, name):
    d = o.shape[1]
    return _rowwise(lambda r, p: ([r[0] * jax.nn.sigmoid(r[1][:, d:].astype(F32))], []), [o, qg], [],
                    [(d, BF16)], [], name)[0]


def _out_gate_bwd(dog, o, qg, name):
    d = o.shape[1]

    def fn(r, p):
        _, vjp = jax.vjp(lambda oo, gl: oo * jax.nn.sigmoid(gl), r[1], r[2][:, d:].astype(F32))
        do, dgl = vjp(r[0])
        return [do, dgl], []
    return _rowwise(fn, [dog, o, qg], [], [(d, BF16), (d, BF16)], [], name)


def _loss_bwd(y, tgt, name):
    n = y.shape[1]

    def fn(r, p):
        e = r[0] - r[1]
        part = jnp.sum(jnp.sum(e * e, axis=1, keepdims=True), axis=0, keepdims=True) * (0.5 / n)
        return [e * (1.0 / n)], [part]
    return _rowwise(fn, [y, tgt], [], [(n, F32)], [(1, 1)], name)


def _adamw(w, g, m, v, name):
    shape = w.shape
    c = shape[-1]
    flat = [t.reshape(-1, c) for t in (w, g, m, v)]

    def fn(r, p):
        w_, g_, m_, v_ = r
        m2 = ADAM_B1 * m_ + (1.0 - ADAM_B1) * g_
        v2 = ADAM_B2 * v_ + (1.0 - ADAM_B2) * (g_ * g_)
        m_hat = m2 / (1.0 - ADAM_B1 ** ADAM_STEP)
        v_hat = v2 / (1.0 - ADAM_B2 ** ADAM_STEP)
        delta = -ADAM_LR * (m_hat / (jnp.sqrt(v_hat) + ADAM_EPS) + ADAM_WD * w_)
        return [delta, m2, v2], []
    res = _rowwise(fn, flat, [], [(c, F32)] * 3, [], name)
    return [t.reshape(shape) for t in res]


def _sum_pair(a, b, name):
    c = a.shape[-1]
    out = _rowwise(lambda r, p: ([r[0].astype(F32) + r[1].astype(F32)], []), [a.reshape(-1, c), b.reshape(-1, c)], [],
                   [(c, BF16)], [], name)[0]
    return out.reshape(a.shape)


def _sum_slots(recv, name, out_dtype=F32):
    n = recv.shape[0]
    shape = recv.shape[1:]
    c = shape[-1]
    r3 = recv.reshape(n, -1, c)
    rows = r3.shape[1]
    tb = _tile(rows, max(BF16_ROWS, ROW_BLOCK_BYTES // (4 * c * (n + 1))), BF16_ROWS)

    def body(r_ref, o_ref):
        acc = r_ref[0].astype(F32)
        for s in range(1, n):
            acc = acc + r_ref[s].astype(F32)
        o_ref[...] = acc.astype(o_ref.dtype)

    out = _call(body, name=name, grid=(rows // tb,),
                in_specs=[pl.BlockSpec((n, tb, c), lambda i: (0, i, 0))],
                out_specs=pl.BlockSpec((tb, c), lambda i: (i, 0)),
                out_shape=jax.ShapeDtypeStruct((rows, c), out_dtype), sem=("parallel",))(r3)
    return out.reshape(shape)


def _gmlp_pre(zu, zv, b_u, b_v, ln_g, ln_b):
    u = jax.nn.gelu(zu + b_u, approximate=True)
    v = jax.nn.gelu(zv + b_v, approximate=True)
    xc = v - jnp.mean(v, axis=-1, keepdims=True)
    vn = xc * lax.rsqrt(jnp.mean(xc * xc, axis=-1, keepdims=True) + NORM_EPS) * ln_g + ln_b
    return u, vn


def _gmlp_fwd(zp, b_in, ln_g, ln_b, ws, bs_t, name):
    S, gw2 = zp.shape
    gw = gw2 // 2
    G, ch, _ = ws.shape
    gd = gw // G
    tb = 2 * ch

    def body(zp_ref, bin_ref, lg_ref, lb_ref, ws_ref, bs_ref, o_ref):
        u, vn = _gmlp_pre(zp_ref[:, :gw].astype(F32), zp_ref[:, gw:].astype(F32), bin_ref[:, :gw], bin_ref[:, gw:],
                          lg_ref[...], lb_ref[...])
        vnb = vn.astype(BF16)
        for c in range(tb // ch):
            for g in range(G):
                rs, cs = slice(c * ch, (c + 1) * ch), slice(g * gd, (g + 1) * gd)
                vv = jnp.dot(ws_ref[g], vnb[rs, cs], preferred_element_type=F32) + bs_ref[:, g:g + 1]
                o_ref[rs, cs] = (u[rs, cs] * vv).astype(o_ref.dtype)

    full = lambda a: pl.BlockSpec(a.shape, lambda i: (0,) * a.ndim)
    return _call(body, name=name, grid=(S // tb,),
                 in_specs=[pl.BlockSpec((tb, gw2), lambda i: (i, 0)), full(b_in), full(ln_g), full(ln_b), full(ws), full(bs_t)],
                 out_specs=pl.BlockSpec((tb, gw), lambda i: (i, 0)),
                 out_shape=jax.ShapeDtypeStruct((S, gw), BF16), sem=("parallel",))(zp, b_in, ln_g, ln_b, ws, bs_t)


def _gmlp_bwd(dyg, zp, b_in, ln_g, ln_b, ws, ws_t, bs_t, name):
    S, gw2 = zp.shape
    gw = gw2 // 2
    G, ch, _ = ws.shape
    gd = gw // G
    tb = 2 * ch

    def body(dy_ref, zp_ref, bin_ref, lg_ref, lb_ref, ws_ref, wst_ref, bs_ref,
             dzp_ref, dbin_ref, dlg_ref, dlb_ref, dws_ref, dbs_ref, du_sc, dvn_sc):
        (u, vn), vjp = jax.vjp(_gmlp_pre, zp_ref[:, :gw].astype(F32), zp_ref[:, gw:].astype(F32), bin_ref[:, :gw],
                               bin_ref[:, gw:], lg_ref[...], lb_ref[...])
        vnb = vn.astype(BF16)
        first = pl.program_id(0) == 0

        @pl.when(first)
        def _():
            dws_ref[...] = jnp.zeros_like(dws_ref)

        lane = lax.broadcasted_iota(jnp.int32, (ch, G), 1)
        dbs = jnp.zeros((ch, G), F32)
        for g in range(G):
            cs = slice(g * gd, (g + 1) * gd)
            dws_g = jnp.zeros((ch, ch), F32)
            col = jnp.zeros((ch, 1), F32)
            for c in range(tb // ch):
                rs = slice(c * ch, (c + 1) * ch)
                vnp = vnb[rs, cs]
                vv = jnp.dot(ws_ref[g], vnp, preferred_element_type=F32) + bs_ref[:, g:g + 1]
                dy = dy_ref[rs, cs].astype(F32)
                du_sc[rs, cs] = dy * vv
                dvv = dy * u[rs, cs]
                dvvb = dvv.astype(BF16)
                dvn_sc[rs, cs] = jnp.dot(wst_ref[g], dvvb, preferred_element_type=F32)
                dws_g = dws_g + lax.dot_general(dvvb, vnp, (((1,), (1,)), ((), ())), preferred_element_type=F32)
                col = col + jnp.sum(dvv, axis=1, keepdims=True)
            dws_ref[g] += dws_g
            dbs = jnp.where(lane == g, col, dbs)
        dzu, dzv, dbu, dbv, dlg, dlb = vjp((du_sc[...], dvn_sc[...]))
        dzp_ref[:, :gw] = dzu.astype(dzp_ref.dtype)
        dzp_ref[:, gw:] = dzv.astype(dzp_ref.dtype)

        @pl.when(first)
        def _():
            dbin_ref[:, :gw] = dbu
            dbin_ref[:, gw:] = dbv
            dlg_ref[...] = dlg
            dlb_ref[...] = dlb
            dbs_ref[...] = dbs

        @pl.when(jnp.logical_not(first))
        def _():
            dbin_ref[:, :gw] += dbu
            dbin_ref[:, gw:] += dbv
            dlg_ref[...] += dlg
            dlb_ref[...] += dlb
            dbs_ref[...] += dbs

    full = lambda a: pl.BlockSpec(a.shape, lambda i: (0,) * a.ndim)
    fshape = lambda s: pl.BlockSpec(s, lambda i: (0,) * len(s))
    return _call(
        body, name=name, grid=(S // tb,),
        in_specs=[pl.BlockSpec((tb, gw), lambda i: (i, 0)), pl.BlockSpec((tb, gw2), lambda i: (i, 0)),
                  full(b_in), full(ln_g), full(ln_b), full(ws), full(ws_t), full(bs_t)],
        out_specs=[pl.BlockSpec((tb, gw2), lambda i: (i, 0)), fshape((1, gw2)), fshape((1, gw)), fshape((1, gw)),
                   fshape((G, ch, ch)), fshape((ch, G))],
        out_shape=[jax.ShapeDtypeStruct((S, gw2), BF16), jax.ShapeDtypeStruct((1, gw2), F32),
                   jax.ShapeDtypeStruct((1, gw), F32), jax.ShapeDtypeStruct((1, gw), F32),
                   jax.ShapeDtypeStruct((G, ch, ch), F32), jax.ShapeDtypeStruct((ch, G), F32)],
        scratch=[pltpu.VMEM((tb, gw), F32), pltpu.VMEM((tb, gw), F32)],
        sem=("arbitrary",))(dyg, zp, b_in, ln_g, ln_b, ws, ws_t, bs_t)


def _dot_01(x, ones_bf16):
    hi = x.astype(BF16)
    r1 = x - hi.astype(F32)
    mid = r1.astype(BF16)
    lo = (r1 - mid.astype(F32)).astype(BF16)
    dot = lambda t: jnp.dot(t, ones_bf16, preferred_element_type=F32)
    return dot(hi) + dot(mid) + dot(lo)


def _log_sigmoid(x):
    return jnp.minimum(x, 0.0) - jnp.log1p(jnp.exp(-jnp.abs(x)))


def _dcum_fwd(f_t, b_col, name):
    H, S = f_t.shape
    tb = _tile(S, 512, LANES)

    def body(f_ref, b_ref, o_ref, carry):
        @pl.when(pl.program_id(0) == 0)
        def _():
            carry[...] = jnp.zeros_like(carry)

        ls = _log_sigmoid(f_ref[...] + b_ref[...])
        r = lax.broadcasted_iota(jnp.int32, (tb, tb), 0)
        c = lax.broadcasted_iota(jnp.int32, (tb, tb), 1)
        upper = (r <= c).astype(BF16)
        o_ref[...] = _dot_01(ls, upper) + carry[...]
        carry[...] += jnp.sum(ls, axis=1, keepdims=True)

    return _call(body, name=name, grid=(S // tb,),
                 in_specs=[pl.BlockSpec((H, tb), lambda i: (0, i)), pl.BlockSpec((H, 1), lambda i: (0, 0))],
                 out_specs=pl.BlockSpec((H, tb), lambda i: (0, i)),
                 out_shape=jax.ShapeDtypeStruct((H, S), F32),
                 scratch=[pltpu.VMEM((H, 1), F32)], sem=("arbitrary",))(f_t, b_col)


def _dcum_bwd(dd_t, f_t, b_col, name):
    H, S = f_t.shape
    tb = _tile(S, 512, LANES)
    nb = S // tb

    def body(dd_ref, f_ref, b_ref, df_ref, db_ref, carry):
        first = pl.program_id(0) == 0

        @pl.when(first)
        def _():
            carry[...] = jnp.zeros_like(carry)

        dd = dd_ref[...]
        r = lax.broadcasted_iota(jnp.int32, (tb, tb), 0)
        c = lax.broadcasted_iota(jnp.int32, (tb, tb), 1)
        lower = (r >= c).astype(BF16)
        rev = _dot_01(dd, lower) + carry[...]
        carry[...] += jnp.sum(dd, axis=1, keepdims=True)
        df = rev * jax.nn.sigmoid(-(f_ref[...] + b_ref[...]))
        df_ref[...] = df
        part = jnp.sum(df, axis=1, keepdims=True)

        @pl.when(first)
        def _():
            db_ref[...] = part

        @pl.when(jnp.logical_not(first))
        def _():
            db_ref[...] += part

    return _call(body, name=name, grid=(nb,),
                 in_specs=[pl.BlockSpec((H, tb), lambda i: (0, nb - 1 - i)), pl.BlockSpec((H, tb), lambda i: (0, nb - 1 - i)),
                           pl.BlockSpec((H, 1), lambda i: (0, 0))],
                 out_specs=[pl.BlockSpec((H, tb), lambda i: (0, nb - 1 - i)), pl.BlockSpec((H, 1), lambda i: (0, 0))],
                 out_shape=[jax.ShapeDtypeStruct((H, S), F32), jax.ShapeDtypeStruct((H, 1), F32)],
                 scratch=[pltpu.VMEM((H, 1), F32)], sem=("arbitrary",))(dd_t, f_t, b_col)


def _attn_tile(S):
    return _tile(S, 512, LANES)


def _causal(t, transposed):
    r = lax.broadcasted_iota(jnp.int32, (t, t), 0)
    c = lax.broadcasted_iota(jnp.int32, (t, t), 1)
    return (r <= c) if transposed else (c <= r)


def _tri_pairs(n, key_major):
    if key_major:
        pairs = [(i, j) for j in range(n) for i in range(j, n)]
    else:
        pairs = [(i, j) for i in range(n) for j in range(i + 1)]
    return jnp.asarray([p[0] for p in pairs], jnp.int32), jnp.asarray([p[1] for p in pairs], jnp.int32)


def _split3(x):
    hi = lax.reduce_precision(x, 8, 7)
    r = x - hi
    mid = lax.reduce_precision(r, 8, 7)
    lo = lax.reduce_precision(r - mid, 8, 7)
    return hi.astype(BF16), mid.astype(BF16), lo.astype(BF16)


def _augment(xn, dcum, query):
    H, S, hd = xn.shape
    parts = list(_split3(dcum))
    vals = parts + [1.0] * 3 if query else [1.0] * 3 + [-p for p in parts]
    lane = lax.broadcasted_iota(jnp.int32, (1, 1, LANES), 2)
    out = jnp.pad(xn, ((0, 0), (0, 0), (0, LANES - hd)))
    for k, val in enumerate(vals):
        val = jnp.asarray(val, BF16)
        out = jnp.where(lane == hd + k, val[..., None] if val.ndim else val, out)
    return out


def _scores_t(k_ref, qt_ref, h, t, diag):
    st = jnp.dot(k_ref[h], qt_ref[h], preferred_element_type=F32)
    return jnp.where(_causal(t, True), st, MASKED) if diag else st


def _flash_fwd(ka, qat, vat, hd, name):
    H, S, da = ka.shape
    t = _attn_tile(S)
    hb = ATTN_HEADS_PER_STEP
    it, jt = _tri_pairs(S // t, False)

    def body(it_ref, jt_ref, k_ref, qt_ref, vt_ref, o_ref, lse_ref, m_sc, acc_sc):
        i, j = it_ref[pl.program_id(1)], jt_ref[pl.program_id(1)]

        @pl.when(j == 0)
        def _():
            m_sc[...] = jnp.full_like(m_sc, MASKED)
            acc_sc[...] = jnp.zeros_like(acc_sc)

        def step(diag):
            sts = [_scores_t(k_ref, qt_ref, h, t, diag) for h in range(hb)]
            pts, alphas = [], []
            for h in range(hb):
                m_prev = m_sc[h]
                m_new = jnp.maximum(m_prev, jnp.max(sts[h], axis=0, keepdims=True))
                pts.append(jnp.exp(sts[h] - m_new).astype(BF16))
                alphas.append(jnp.exp(m_prev - m_new))
                m_sc[h] = m_new
            for h in range(hb):
                acc_sc[h] = alphas[h] * acc_sc[h] + jnp.dot(vt_ref[h], pts[h], preferred_element_type=F32)

        @pl.when(j < i)
        def _():
            step(False)

        @pl.when(j == i)
        def _():
            step(True)
            for h in range(hb):
                l = acc_sc[h, hd:hd + 1, :]
                o_ref[h] = acc_sc[h, :hd, :] / l
                lse_ref[h] = m_sc[h] + jnp.log(l)

    qcol = lambda h, p, it_, jt_: (h, 0, it_[p])
    kcol = lambda h, p, it_, jt_: (h, 0, jt_[p])
    krow = lambda h, p, it_, jt_: (h, jt_[p], 0)
    return _call_prefetch(
        body, name=name, grid=(H // hb, it.shape[0]), n_prefetch=2,
        in_specs=[pl.BlockSpec((hb, t, da), krow), pl.BlockSpec((hb, da, t), qcol), pl.BlockSpec((hb, da, t), kcol)],
        out_specs=[pl.BlockSpec((hb, hd, t), qcol), pl.BlockSpec((hb, 1, t), qcol)],
        out_shape=[jax.ShapeDtypeStruct((H, hd, S), F32), jax.ShapeDtypeStruct((H, 1, S), F32)],
        scratch=[pltpu.VMEM((hb, 1, t), F32), pltpu.VMEM((hb, da, t), F32)],
        sem=("parallel", "arbitrary"))(it, jt, ka, qat, vat)


def _flash_bwd(ka, kat, qat, v, dot, o_tr, lse_r, name):
    H, S, hd = v.shape
    da = ka.shape[2]
    t = _attn_tile(S)
    n = S // t
    hb = ATTN_HEADS_PER_STEP
    it, jt = _tri_pairs(n, True)
    over_queries = (((1,), (1,)), ((), ()))

    def body(it_ref, jt_ref, k_ref, kt_ref, qt_ref, v_ref, dot_ref, o_ref, lse_ref, dq_ref, dk_ref, dv_ref, dk_sc, dv_sc):
        i, j = it_ref[pl.program_id(1)], jt_ref[pl.program_id(1)]

        @pl.when(pl.program_id(1) == 0)
        def _():
            dq_ref[...] = jnp.zeros_like(dq_ref)

        def step(diag):
            for h0 in range(0, hb, ATTN_STAGED_HEADS):
                hs = range(h0, min(h0 + ATTN_STAGED_HEADS, hb))
                sts = [_scores_t(k_ref, qt_ref, h, t, diag) for h in hs]
                dpts = [jnp.dot(v_ref[h], dot_ref[h], preferred_element_type=F32) for h in hs]
                tiles = []
                for h, st, dpt in zip(hs, sts, dpts):
                    dl = jnp.sum(dot_ref[h].astype(F32) * o_ref[h], axis=0, keepdims=True)
                    pt = jnp.exp(st - lse_ref[h])
                    tiles.append((pt.astype(BF16), (pt * (dpt - dl)).astype(BF16)))
                for h, (ptb, dsb) in zip(hs, tiles):
                    dv_sc[h] += lax.dot_general(dot_ref[h], ptb, over_queries, preferred_element_type=F32)
                    dk_sc[h] += lax.dot_general(qt_ref[h], dsb, over_queries, preferred_element_type=F32)
                    dq_ref[h, i] += jnp.dot(kt_ref[h], dsb, preferred_element_type=F32)

        @pl.when(i == j)
        def _():
            dk_sc[...] = jnp.zeros_like(dk_sc)
            dv_sc[...] = jnp.zeros_like(dv_sc)
            step(True)

        @pl.when(i > j)
        def _():
            step(False)

        @pl.when(i == n - 1)
        def _():
            dk_ref[...] = dk_sc[...]
            dv_ref[...] = dv_sc[...]

    krow = lambda h, p, it_, jt_: (h, jt_[p], 0)
    kcol = lambda h, p, it_, jt_: (h, 0, jt_[p])
    qcol = lambda h, p, it_, jt_: (h, 0, it_[p])
    return _call_prefetch(
        body, name=name, grid=(H // hb, it.shape[0]), n_prefetch=2,
        in_specs=[pl.BlockSpec((hb, t, da), krow), pl.BlockSpec((hb, da, t), kcol), pl.BlockSpec((hb, da, t), qcol),
                  pl.BlockSpec((hb, t, hd), krow), pl.BlockSpec((hb, hd, t), qcol), pl.BlockSpec((hb, hd, t), qcol),
                  pl.BlockSpec((hb, 1, t), qcol)],
        out_specs=[pl.BlockSpec((hb, n, da, t), lambda h, p, it_, jt_: (h, 0, 0, 0)), pl.BlockSpec((hb, da, t), kcol),
                   pl.BlockSpec((hb, hd, t), kcol)],
        out_shape=[jax.ShapeDtypeStruct((H, n, da, t), F32), jax.ShapeDtypeStruct((H, da, S), F32),
                   jax.ShapeDtypeStruct((H, hd, S), F32)],
        scratch=[pltpu.VMEM((hb, da, t), F32), pltpu.VMEM((hb, hd, t), F32)],
        sem=("parallel", "arbitrary"))(it, jt, ka, kat, qat, v, dot, o_tr, lse_r)


def _offsets(n_bits):
    return [tuple((k >> b) & 1 for b in reversed(range(n_bits))) for k in range(1, 1 << n_bits)]


def _own_slot(out, own, idx):
    return lax.dynamic_update_index_in_dim(out, own.astype(out.dtype), idx, 0)


def _gather8(arrs, name):
    n = len(arrs)
    offs = _offsets(3)

    def body(*refs):
        ins, outs = refs[:n], refs[n:2 * n]
        ssem, rsem = refs[2 * n:]
        x, y, c = _place()
        me = 4 * x + 2 * y + c
        copies = []
        for a in range(n):
            for k, (dx, dy, dcc) in enumerate(offs):
                cp = pltpu.make_async_remote_copy(
                    src_ref=ins[a], dst_ref=outs[a].at[me], send_sem=ssem.at[a, k], recv_sem=rsem.at[a, k],
                    device_id=((x + dx) % 2, (y + dy) % 2, (c + dcc) % 2), device_id_type=MESH)
                cp.start()
                copies.append(cp)
        for cp in copies:
            cp.wait()

    return _call(body, name=name, in_specs=[ANY] * n, out_specs=[ANY] * n,
                 out_shape=[jax.ShapeDtypeStruct((N_DEV,) + a.shape, a.dtype) for a in arrs],
                 scratch=[pltpu.SemaphoreType.DMA((n, 7)), pltpu.SemaphoreType.DMA((n, 7))])(*arrs)


def _chip_gather(arrs, halved, name):
    n = len(arrs)
    offs = _offsets(2)

    def body(*refs):
        ins, outs = refs[:n], refs[n:2 * n]
        ssem, rsem = refs[2 * n:]
        x, y, c = _place()
        chip = 2 * x + y
        copies = []
        for a in range(n):
            if halved:
                hn = arrs[a].shape[0] // 2
                src = ins[a].at[pl.ds(c * hn, hn)]
                dst = outs[a].at[chip, pl.ds(c * hn, hn)]
            else:
                src, dst = ins[a], outs[a].at[chip]
            for k, (dx, dy) in enumerate(offs):
                cp = pltpu.make_async_remote_copy(
                    src_ref=src, dst_ref=dst, send_sem=ssem.at[a, k], recv_sem=rsem.at[a, k],
                    device_id=((x + dx) % 2, (y + dy) % 2, c), device_id_type=MESH)
                cp.start()
                copies.append(cp)
        for cp in copies:
            cp.wait()

    return _call(body, name=name, in_specs=[ANY] * n, out_specs=[ANY] * n,
                 out_shape=[jax.ShapeDtypeStruct((N_CHIPS,) + a.shape, a.dtype) for a in arrs],
                 scratch=[pltpu.SemaphoreType.DMA((n, 3)), pltpu.SemaphoreType.DMA((n, 3))])(*arrs)


def _sibling_fill(bufs, name):
    n = len(bufs)
    offs = _offsets(2)

    def body(*refs):
        ins, outs = refs[:n], refs[n:2 * n]
        ssem, rsem = refs[2 * n:]
        x, y, c = _place()
        copies = []
        for a in range(n):
            hn = bufs[a].shape[1] // 2
            for k, (dx, dy) in enumerate(offs):
                chip = 2 * ((x + dx) % 2) + (y + dy) % 2
                cp = pltpu.make_async_remote_copy(
                    src_ref=ins[a].at[chip, pl.ds(c * hn, hn)], dst_ref=outs[a].at[chip, pl.ds(c * hn, hn)],
                    send_sem=ssem.at[a, k], recv_sem=rsem.at[a, k],
                    device_id=(x, y, 1 - c), device_id_type=MESH)
                cp.start()
                copies.append(cp)
        for cp in copies:
            cp.wait()

    return _call(body, name=name, in_specs=[ANY] * n, out_specs=[ANY] * n,
                 out_shape=[jax.ShapeDtypeStruct(b.shape, b.dtype) for b in bufs],
                 scratch=[pltpu.SemaphoreType.DMA((n, 3)), pltpu.SemaphoreType.DMA((n, 3))],
                 aliases={a: a for a in range(n)})(*bufs)


def _sibling_pair(arrs, name):
    n = len(arrs)

    def body(*refs):
        ins, outs = refs[:n], refs[n:2 * n]
        ssem, rsem = refs[2 * n:]
        x, y, c = _place()
        copies = []
        for a in range(n):
            cp = pltpu.make_async_remote_copy(
                src_ref=ins[a], dst_ref=outs[a].at[c], send_sem=ssem.at[a], recv_sem=rsem.at[a],
                device_id=(x, y, 1 - c), device_id_type=MESH)
            cp.start()
            copies.append(cp)
        for cp in copies:
            cp.wait()

    return _call(body, name=name, in_specs=[ANY] * n, out_specs=[ANY] * n,
                 out_shape=[jax.ShapeDtypeStruct((N_CORES,) + a.shape, a.dtype) for a in arrs],
                 scratch=[pltpu.SemaphoreType.DMA((n,)), pltpu.SemaphoreType.DMA((n,))])(*arrs)


def _piece(shape, spec, j, h):
    shard_ax, half_ax = spec
    w = shape[shard_ax] // N_CHIPS
    idx = [slice(None)] * len(shape)
    idx[shard_ax] = pl.ds(j * w, w)
    hn = (w if half_ax == shard_ax else shape[half_ax]) // 2
    assert half_ax != shard_ax
    idx[half_ax] = pl.ds(h * hn, hn)
    return tuple(idx)


def _piece_shape(shape, spec):
    shard_ax, half_ax = spec
    s = list(shape)
    s[shard_ax] //= N_CHIPS
    s[half_ax] //= 2
    return tuple(s)


def _own_pieces(g, spec, c):
    shard_ax, half_ax = spec
    hn = g.shape[half_ax] // 2
    half = lax.dynamic_slice_in_dim(g, c * hn, hn, axis=half_ax)
    shape = list(half.shape)
    shape[shard_ax:shard_ax + 1] = [N_CHIPS, shape[shard_ax] // N_CHIPS]
    return jnp.moveaxis(half.reshape(shape), shard_ax, 0)


def _sibling_scatter(arrs, specs, name):
    n = len(arrs)

    def body(*refs):
        ins, outs = refs[:n], refs[n:2 * n]
        ssem, rsem = refs[2 * n:]
        x, y, c = _place()
        for mine in range(N_CORES):
            @pl.when(c == mine)
            def _():
                copies = []
                for a in range(n):
                    for j in range(N_CHIPS):
                        cp = pltpu.make_async_remote_copy(
                            src_ref=ins[a].at[_piece(arrs[a].shape, specs[a], j, 1 - mine)], dst_ref=outs[a].at[j],
                            send_sem=ssem.at[a, j], recv_sem=rsem.at[a, j],
                            device_id=(x, y, 1 - mine), device_id_type=MESH)
                        cp.start()
                        copies.append(cp)
                for cp in copies:
                    cp.wait()

    return _call(body, name=name, in_specs=[ANY] * n, out_specs=[ANY] * n,
                 out_shape=[jax.ShapeDtypeStruct((N_CHIPS,) + _piece_shape(a.shape, s), a.dtype)
                            for a, s in zip(arrs, specs)],
                 scratch=[pltpu.SemaphoreType.DMA((n, N_CHIPS))] * 2)(*arrs)


def _chip_scatter(arrs, name):
    n = len(arrs)
    offs = _offsets(2)

    def body(*refs):
        ins, outs = refs[:n], refs[n:2 * n]
        ssem, rsem = refs[2 * n:]
        x, y, c = _place()
        chip = 2 * x + y
        copies = []
        for a in range(n):
            for k, (dx, dy) in enumerate(offs):
                tx, ty = (x + dx) % 2, (y + dy) % 2
                cp = pltpu.make_async_remote_copy(
                    src_ref=ins[a].at[2 * tx + ty], dst_ref=outs[a].at[chip], send_sem=ssem.at[a, k], recv_sem=rsem.at[a, k],
                    device_id=(tx, ty, c), device_id_type=MESH)
                cp.start()
                copies.append(cp)
        for cp in copies:
            cp.wait()

    return _call(body, name=name, in_specs=[ANY] * n, out_specs=[ANY] * n,
                 out_shape=[jax.ShapeDtypeStruct(a.shape, a.dtype) for a in arrs],
                 scratch=[pltpu.SemaphoreType.DMA((n, 3)), pltpu.SemaphoreType.DMA((n, 3))])(*arrs)


def kernel(x, c, ada_w, ada_b, pre_mix_g, post_mix_g, pre_ffn_g, post_ffn_g, ffn_w_gu, ffn_w_down, a_w_in, a_b_in, a_ln_g, a_ln_b, a_w_s, a_b_s, a_w_out, kv_ada_w, kv_ada_b, kv_norm_g, kv_w, kv_b_f, k_norm_g, b_w_qg, b_q_norm_g, b_w_o, loss_target, m_ada_w, m_ada_b, m_pre_mix_g, m_post_mix_g, m_pre_ffn_g, m_post_ffn_g, m_ffn_w_gu, m_ffn_w_down, m_a_w_in, m_a_b_in, m_a_ln_g, m_a_ln_b, m_a_w_s, m_a_b_s, m_a_w_out, m_kv_ada_w, m_kv_ada_b, m_kv_norm_g, m_kv_w, m_kv_b_f, m_k_norm_g, m_b_w_qg, m_b_q_norm_g, m_b_w_o, v_ada_w, v_ada_b, v_pre_mix_g, v_post_mix_g, v_pre_ffn_g, v_post_ffn_g, v_ffn_w_gu, v_ffn_w_down, v_a_w_in, v_a_b_in, v_a_ln_g, v_a_ln_b, v_a_w_s, v_a_b_s, v_a_w_out, v_kv_ada_w, v_kv_ada_b, v_kv_norm_g, v_kv_w, v_kv_b_f, v_k_norm_g, v_b_w_qg, v_b_q_norm_g, v_b_w_o):
    weights = dict(ada_w=ada_w, ada_b=ada_b, pre_mix_g=pre_mix_g, post_mix_g=post_mix_g, pre_ffn_g=pre_ffn_g,
                   post_ffn_g=post_ffn_g, ffn_w_gu=ffn_w_gu, ffn_w_down=ffn_w_down, a_w_in=a_w_in, a_b_in=a_b_in,
                   a_ln_g=a_ln_g, a_ln_b=a_ln_b, a_w_s=a_w_s, a_b_s=a_b_s, a_w_out=a_w_out, kv_ada_w=kv_ada_w,
                   kv_ada_b=kv_ada_b, kv_norm_g=kv_norm_g, kv_w=kv_w, kv_b_f=kv_b_f, k_norm_g=k_norm_g, b_w_qg=b_w_qg,
                   b_q_norm_g=b_q_norm_g, b_w_o=b_w_o)
    m_in = dict(ada_w=m_ada_w, ada_b=m_ada_b, pre_mix_g=m_pre_mix_g, post_mix_g=m_post_mix_g, pre_ffn_g=m_pre_ffn_g,
                post_ffn_g=m_post_ffn_g, ffn_w_gu=m_ffn_w_gu, ffn_w_down=m_ffn_w_down, a_w_in=m_a_w_in, a_b_in=m_a_b_in,
                a_ln_g=m_a_ln_g, a_ln_b=m_a_ln_b, a_w_s=m_a_w_s, a_b_s=m_a_b_s, a_w_out=m_a_w_out, kv_ada_w=m_kv_ada_w,
                kv_ada_b=m_kv_ada_b, kv_norm_g=m_kv_norm_g, kv_w=m_kv_w, kv_b_f=m_kv_b_f, k_norm_g=m_k_norm_g,
                b_w_qg=m_b_w_qg, b_q_norm_g=m_b_q_norm_g, b_w_o=m_b_w_o)
    v_in = dict(ada_w=v_ada_w, ada_b=v_ada_b, pre_mix_g=v_pre_mix_g, post_mix_g=v_post_mix_g, pre_ffn_g=v_pre_ffn_g,
                post_ffn_g=v_post_ffn_g, ffn_w_gu=v_ffn_w_gu, ffn_w_down=v_ffn_w_down, a_w_in=v_a_w_in, a_b_in=v_a_b_in,
                a_ln_g=v_a_ln_g, a_ln_b=v_a_ln_b, a_w_s=v_a_w_s, a_b_s=v_a_b_s, a_w_out=v_a_w_out, kv_ada_w=v_kv_ada_w,
                kv_ada_b=v_kv_ada_b, kv_norm_g=v_kv_norm_g, kv_w=v_kv_w, kv_b_f=v_kv_b_f, k_norm_g=v_k_norm_g,
                b_w_qg=v_b_w_qg, b_q_norm_g=v_b_q_norm_g, b_w_o=v_b_w_o)
    names = list(weights)

    S, D = x.shape[1], x.shape[2]
    L, NA, NB = ada_w.shape[0], a_w_in.shape[0], b_w_qg.shape[0]
    H = kv_b_f.shape[0]
    hd = D // H
    G, CH = a_w_s.shape[1], a_w_s.shape[2]
    GW = a_w_out.shape[1] * N_CHIPS
    F = ffn_w_down.shape[1] * N_CHIPS
    ada_cols = ada_w.shape[2]
    kvada_cols = kv_ada_w.shape[1]
    kv_cols = kv_w.shape[1]
    kv_pad = -(-(2 * D + H) // LANES) * LANES
    xi, yi, ci = _place()
    chip = 2 * xi + yi
    me = 2 * chip + ci
    x0 = x[0]
    tgt = loss_target[0]
    row = lambda t: t.reshape(1, -1)

    c_all = _own_slot(_gather8([c], "gather_c")[0], c, me).reshape(N_DEV, D)
    c_act = _silu_rows(jnp.pad(c_all, ((0, BF16_ROWS - N_DEV), (0, 0))), "silu_c")
    mod_sh = [_mm(c_act, (ada_w, l), "nn", F32, f"mod_proj_{l}") for l in range(L)]
    mod_sh.append(_mm(c_act, kv_ada_w, "nn", F32, "mod_proj_kv"))
    mod_sh = jnp.concatenate(mod_sh, axis=1)
    small_sh = [mod_sh, a_b_in, a_ln_g, a_ln_b]
    mod_all, b_in_all, ln_g_all, ln_b_all = [
        _own_slot(o, s, chip) for o, s in zip(_chip_gather(small_sh, False, "gather_mod"), small_sh)]
    mine = lax.dynamic_index_in_dim(mod_all, me, axis=1, keepdims=False)
    mod = [jnp.concatenate([mine[j, l * ada_cols:(l + 1) * ada_cols] for j in range(N_CHIPS)]) + ada_b[l] for l in range(L)]
    mod = [[row(t) for t in jnp.split(m_, 6)] for m_ in mod]
    mod_kv = jnp.concatenate([mine[j, L * ada_cols:] for j in range(N_CHIPS)]) + kv_ada_b
    kv_sh, kv_sc = [row(t) for t in jnp.split(mod_kv, 2)]
    cat_chips = lambda t, ax: jnp.concatenate([t[j] for j in range(N_CHIPS)], axis=ax)
    b_in_f = cat_chips(b_in_all, 1)
    ln_g_f, ln_b_f = cat_chips(ln_g_all, 1), cat_chips(ln_b_all, 1)

    big = ["ffn_w_gu", "ffn_w_down", "a_w_in", "a_w_out", "kv_w", "b_w_qg", "b_w_o"]
    own_w = [weights[n].astype(BF16) for n in big]
    gathered = _sibling_fill(_chip_gather(own_w, True, "gather_w"), "fill_w")
    gathered = {n: _own_slot(g, w, chip) for n, g, w in zip(big, gathered, own_w)}
    gu_hw = ffn_w_gu.shape[2]
    w_gu = jnp.concatenate([gathered["ffn_w_gu"][j] for j in (0, 2, 1, 3)], axis=2)
    w_dn = cat_chips(gathered["ffn_w_down"], 1)
    w_in = cat_chips(gathered["a_w_in"], 2)
    w_out = cat_chips(gathered["a_w_out"], 1)
    w_kv = jnp.pad(cat_chips(gathered["kv_w"], 1), ((0, 0), (0, kv_pad - (2 * D + H))))
    w_qg = cat_chips(gathered["b_w_qg"], 2)
    w_o = cat_chips(gathered["b_w_o"], 1)

    causal = jnp.tril(jnp.ones((CH, CH), F32))
    ws_m = [(a_w_s[i] * causal).astype(BF16) for i in range(NA)]
    ws_mt = [jnp.swapaxes(w, 1, 2) for w in ws_m]
    bs_t = [a_b_s[i].T for i in range(NA)]

    heads = lambda t: t.reshape(S, H, hd).transpose(1, 0, 2)
    unheads = lambda t: t.transpose(1, 0, 2).reshape(S, D)

    saved = []
    kv = None
    xc = x0
    for l in range(L):
        sh_m, sc_m, g_m, sh_f, sc_f, g_f = mod[l]
        st = {"x0": xc}
        h1 = _norm_mod_fwd(xc, row(pre_mix_g[l]), sh_m, sc_m, f"pre_mix_{l}")
        st["h1"] = h1
        if l < NA:
            zp = _mm(h1, (w_in, l), "nn", BF16, f"gmlp_in_{l}")
            yg = _gmlp_fwd(zp, row(b_in_f[l]), row(ln_g_f[l]), row(ln_b_f[l]), ws_m[l], bs_t[l], f"gmlp_gate_{l}")
            y = _mm(yg, (w_out, l), "nn", F32, f"gmlp_out_{l}")
            st.update(zp=zp, yg=yg)
        else:
            jb = l - NA
            qg = _mm(h1, (w_qg, jb), "nn", BF16, f"fox_qg_{jb}")
            q_raw = heads(qg[:, :D]).reshape(H * S, hd)
            qn = _head_norm_fwd(q_raw, row(b_q_norm_g[jb]), hd ** -0.5, f"fox_qnorm_{jb}").reshape(H, S, hd)
            qa = _augment(qn, kv["dcum"], True)
            qat = jnp.swapaxes(qa, 1, 2)
            o_tr, lse_r = _flash_fwd(kv["ka"], qat, kv["vat"], hd, f"fox_attn_{jb}")
            o_t = o_tr.transpose(2, 0, 1).reshape(S, D)
            og = _out_gate_fwd(o_t, qg, f"fox_gate_{jb}")
            y = _mm(og, (w_o, jb), "nn", F32, f"fox_out_{jb}")
            st.update(qg=qg, q_raw=q_raw, qat=qat, o_tr=o_tr, lse_r=lse_r, o_t=o_t, og=og)
        st["y"] = y
        x1 = _post_fwd(xc, y, row(post_mix_g[l]), g_m, f"post_mix_{l}")
        st["x1"] = x1
        h2 = _norm_mod_fwd(x1, row(pre_ffn_g[l]), sh_f, sc_f, f"pre_ffn_{l}")
        gu, act = _ffn_up(h2, w_gu, l, gu_hw, f"ffn_gu_{l}")
        y2 = _mm(act, (w_dn, l), "nn", F32, f"ffn_down_{l}")
        xc = _post_fwd(x1, y2, row(post_ffn_g[l]), g_f, f"post_ffn_{l}")
        st.update(h2=h2, gu=gu, act=act, y2=y2)
        saved.append(st)
        if l == NA - 1:
            hk = _norm_mod_fwd(xc, row(kv_norm_g), kv_sh, kv_sc, "kv_pre")
            kvf = _mm(hk, w_kv, "nn", F32, "kv_proj")
            k_raw = heads(kvf[:, :D]).reshape(H * S, hd)
            kn = _head_norm_fwd(k_raw, row(k_norm_g), 1.0, "kv_knorm").reshape(H, S, hd)
            vb = heads(kvf[:, D:2 * D]).astype(BF16)
            f_t = kvf[:, 2 * D:2 * D + H].T
            b_col = kv_b_f.reshape(H, 1)
            dcum = _dcum_fwd(f_t, b_col, "kv_dcum")
            vt = kvf[:, D:2 * D].astype(BF16).reshape(S, H, hd).transpose(1, 2, 0)
            vat = jnp.where(lax.broadcasted_iota(jnp.int32, (1, LANES, 1), 1) == hd, jnp.asarray(1, BF16),
                            jnp.pad(vt, ((0, 0), (0, LANES - hd), (0, 0))))
            ka = _augment(kn, dcum, False)
            kv = dict(x=xc, hk=hk, k_raw=k_raw, ka=ka, kat=jnp.swapaxes(ka, 1, 2), vb=vb, vat=vat,
                      f_t=f_t, b_col=b_col, dcum=dcum)

    dx, loss_part = _loss_bwd(xc, tgt, "loss")
    loss = lax.psum(loss_part[0, 0], ("x", "y", "c"))

    gl = {n: [None] * weights[n].shape[0] for n in
          ["pre_mix_g", "post_mix_g", "pre_ffn_g", "post_ffn_g", "ffn_w_gu", "ffn_w_down", "a_w_in", "a_b_in", "a_ln_g",
           "a_ln_b", "a_w_s", "a_b_s", "a_w_out", "b_w_qg", "b_q_norm_g", "b_w_o"]}
    dmod = [None] * L
    dkn = dvb = ddc = None
    gkv = {}
    for l in reversed(range(L)):
        st = saved[l]
        sh_m, sc_m, g_m, sh_f, sc_f, g_f = mod[l]
        if l == NA - 1:
            dk_raw, gkv["k_norm_g"] = _head_norm_bwd(jnp.swapaxes(dkn, 1, 2).reshape(H * S, hd), kv["k_raw"], row(k_norm_g),
                                                     1.0, "kv_knorm_bwd")
            df_t, db_f = _dcum_bwd(ddc.reshape(H, S), kv["f_t"], kv["b_col"], "kv_dcum_bwd")
            dkvf = jnp.concatenate([unheads(dk_raw.reshape(H, S, hd)), dvb.transpose(2, 0, 1).reshape(S, D), df_t.T,
                                    jnp.zeros((S, kv_pad - (2 * D + H)), F32)], axis=1).astype(BF16)
            gkv["kv_w"] = _mm(kv["hk"], dkvf, "tn", BF16, "kv_proj_dw")[:, :2 * D + H]
            dhk = _mm(dkvf, w_kv, "nt", F32, "kv_proj_dx")
            dx, gkv["kv_norm_g"], dsh, dsc = _norm_mod_bwd(dx, dhk, kv["x"], row(kv_norm_g), kv_sh, kv_sc, "kv_pre_bwd")
            gkv["kv_b_f"] = db_f.reshape(H)
            dmod_kv = jnp.concatenate([dsh, dsc], axis=1)
        dy2, gl["post_ffn_g"][l], dg_f = _post_bwd(dx, st["y2"], row(post_ffn_g[l]), g_f, f"post_ffn_bwd_{l}")
        gl["ffn_w_down"][l] = _mm(st["act"], dy2, "tn", BF16, f"ffn_down_dw_{l}")
        dgu = _ffn_down_dx(dy2, w_dn, l, st["gu"], gu_hw, f"ffn_down_dx_{l}")
        dw_blocked = _mm(st["h2"], dgu, "tn", BF16, f"ffn_gu_dw_{l}")
        gl["ffn_w_gu"][l] = dw_blocked.reshape(D, 2, 2, gu_hw).swapaxes(1, 2).reshape(D, 4 * gu_hw)
        dh2 = _mm(dgu, (w_gu, l), "nt", F32, f"ffn_gu_dx_{l}")
        dx, gl["pre_ffn_g"][l], dsh_f, dsc_f = _norm_mod_bwd(dx, dh2, st["x1"], row(pre_ffn_g[l]), sh_f, sc_f, f"pre_ffn_bwd_{l}")
        dy, gl["post_mix_g"][l], dg_m = _post_bwd(dx, st["y"], row(post_mix_g[l]), g_m, f"post_mix_bwd_{l}")
        if l < NA:
            gl["a_w_out"][l] = _mm(st["yg"], dy, "tn", BF16, f"gmlp_out_dw_{l}")
            dyg = _mm(dy, (w_out, l), "nt", BF16, f"gmlp_out_dx_{l}")
            dzp, db_in, dlg, dlb, dws, dbs_t = _gmlp_bwd(dyg, st["zp"], row(b_in_f[l]), row(ln_g_f[l]), row(ln_b_f[l]),
                                                           ws_m[l], ws_mt[l], bs_t[l], f"gmlp_gate_bwd_{l}")
            gl["a_b_in"][l], gl["a_ln_g"][l], gl["a_ln_b"][l] = db_in[0], dlg[0], dlb[0]
            gl["a_w_s"][l], gl["a_b_s"][l] = dws * causal, dbs_t.T
            gl["a_w_in"][l] = _mm(st["h1"], dzp, "tn", BF16, f"gmlp_in_dw_{l}")
            dh1 = _mm(dzp, (w_in, l), "nt", F32, f"gmlp_in_dx_{l}")
        else:
            jb = l - NA
            gl["b_w_o"][jb] = _mm(st["og"], dy, "tn", BF16, f"fox_out_dw_{jb}")
            dog = _mm(dy, (w_o, jb), "nt", F32, f"fox_out_dx_{jb}")
            do_t, dgl = _out_gate_bwd(dog, st["o_t"], st["qg"], f"fox_gate_bwd_{jb}")
            dot = do_t.reshape(S, H, hd).transpose(1, 2, 0)
            dqa_tr, dka_tr, dv_j = _flash_bwd(kv["ka"], kv["kat"], st["qat"], kv["vb"], dot, st["o_tr"], st["lse_r"],
                                              f"fox_attn_bwd_{jb}")
            dqn = dqa_tr[:, :, :hd, :].transpose(0, 1, 3, 2).reshape(H, S, hd)
            dk_j = dka_tr[:, :hd, :]
            dd_j = dqa_tr[:, :, hd, :].reshape(H, S) - dka_tr[:, hd + 3, :]
            dkn = dk_j if dkn is None else dkn + dk_j
            dvb = dv_j if dvb is None else dvb + dv_j
            ddc = dd_j if ddc is None else ddc + dd_j
            dq_raw, dgq = _head_norm_bwd(dqn.reshape(H * S, hd), st["q_raw"], row(b_q_norm_g[jb]), hd ** -0.5, f"fox_qnorm_bwd_{jb}")
            gl["b_q_norm_g"][jb] = dgq[0]
            dqg = jnp.concatenate([unheads(dq_raw.reshape(H, S, hd)).astype(BF16), dgl], axis=1)
            gl["b_w_qg"][jb] = _mm(st["h1"], dqg, "tn", BF16, f"fox_qg_dw_{jb}")
            dh1 = _mm(dqg, (w_qg, jb), "nt", F32, f"fox_qg_dx_{jb}")
        dx, gl["pre_mix_g"][l], dsh_m, dsc_m = _norm_mod_bwd(dx, dh1, st["x0"], row(pre_mix_g[l]), sh_m, sc_m, f"pre_mix_bwd_{l}")
        dmod[l] = jnp.concatenate([dsh_m, dsc_m, dg_m, dsh_f, dsc_f, dg_f], axis=1)
    grad_x = dx[None]

    stack = lambda n: jnp.stack([t.reshape(weights[n].shape[1:]) for t in gl[n]])
    small = {"dmod": jnp.concatenate(dmod, axis=1), "dmod_kv": dmod_kv}
    for n in ["pre_mix_g", "post_mix_g", "pre_ffn_g", "post_ffn_g", "a_w_s", "a_b_s", "b_q_norm_g"]:
        small[n] = stack(n)
    for n in ["a_b_in", "a_ln_g", "a_ln_b"]:
        small[n] = jnp.stack(gl[n])
    for n in ["kv_norm_g", "kv_b_f", "k_norm_g"]:
        small[n] = gkv[n]
    sizes = {n: t.size for n, t in small.items()}
    flat = jnp.concatenate([t.reshape(-1).astype(F32) for t in small.values()])
    rows_small = -(-flat.size // (LANES * BF16_ROWS)) * BF16_ROWS
    flat = jnp.pad(flat, (0, rows_small * LANES - flat.size)).reshape(rows_small, LANES)
    flat_all = _own_slot(_gather8([flat], "gather_small")[0], flat, me)
    flat_sum = _sum_slots(flat_all, "sum_small").reshape(-1)
    offs, o_ = {}, 0
    for n, sz in sizes.items():
        offs[n] = o_
        o_ += sz
    take = lambda n, shape: flat_sum[offs[n]:offs[n] + sizes[n]].reshape(shape)
    dmod_rows = flat_all.reshape(N_DEV, -1)[:, offs["dmod"]:offs["dmod"] + sizes["dmod"] + sizes["dmod_kv"]]
    dmod_rows = jnp.pad(dmod_rows, ((0, BF16_ROWS - N_DEV), (0, 0)))

    grads = {}
    grads["ada_b"] = take("dmod", (L, 6 * D))
    grads["kv_ada_b"] = take("dmod_kv", (2 * D,))
    for n in ["pre_mix_g", "post_mix_g", "pre_ffn_g", "post_ffn_g", "a_w_s", "a_b_s", "b_q_norm_g", "kv_norm_g", "kv_b_f", "k_norm_g"]:
        grads[n] = take(n, weights[n].shape)
    for n in ["a_b_in", "a_ln_g", "a_ln_b"]:
        full = take(n, small[n].shape)
        w = weights[n].shape[1]
        grads[n] = lax.dynamic_slice_in_dim(full, chip * w, w, axis=1)
    ada_g = []
    for l in range(L):
        cols = lax.dynamic_slice_in_dim(dmod_rows[:, l * 6 * D:(l + 1) * 6 * D], chip * ada_cols, ada_cols, axis=1)
        ada_g.append(_mm(c_act, cols, "tn", F32, f"mod_proj_dw_{l}"))
    grads["ada_w"] = jnp.stack(ada_g)
    cols = lax.dynamic_slice_in_dim(dmod_rows[:, L * 6 * D:], chip * kvada_cols, kvada_cols, axis=1)
    grads["kv_ada_w"] = _mm(c_act, cols, "tn", F32, "mod_proj_kv_dw")

    specs = {"ffn_w_gu": (2, 0), "ffn_w_down": (1, 0), "a_w_in": (2, 0), "a_w_out": (1, 0), "kv_w": (0, 1),
             "b_w_qg": (2, 0), "b_w_o": (1, 0)}
    full_g = {n: jnp.stack(gl[n]) for n in big if n != "kv_w"}
    full_g["kv_w"] = gkv["kv_w"].reshape(D, N_CHIPS, kv_cols).transpose(1, 0, 2)
    from_core = _sibling_scatter([full_g[n] for n in big], [specs[n] for n in big], "scatter_g_core")
    chip_sums = [_sum_pair(_own_pieces(full_g[n], specs[n], ci), r, f"sum_g_core_{n}") for n, r in zip(big, from_core)]
    recv = _chip_scatter(chip_sums, "scatter_g_chip")
    recv = [_own_slot(r, lax.dynamic_index_in_dim(p, chip, 0, keepdims=False), chip) for r, p in zip(recv, chip_sums)]
    halves = [_sum_slots(r, f"sum_g_{n}") for n, r in zip(big, recv)]
    pairs = _sibling_pair(halves, "pair_g")
    for n, p, hlf in zip(big, pairs, halves):
        grads[n] = _own_slot(p, hlf, ci).reshape(weights[n].shape)

    outs_d, outs_m, outs_v = {}, {}, {}
    for n in names:
        w2 = weights[n] if weights[n].ndim > 1 else weights[n].reshape(1, -1)
        shp = w2.shape
        d_, m_, v_ = _adamw(w2, grads[n].reshape(shp), m_in[n].reshape(shp), v_in[n].reshape(shp), f"adamw_{n}")
        outs_d[n], outs_m[n], outs_v[n] = (t.reshape(weights[n].shape) for t in (d_, m_, v_))
    return (loss, grad_x, *[grads[n] for n in names], *[outs_d[n] for n in names],
            *[outs_m[n] for n in names], *[outs_v[n] for n in names])
```

```python
import functools

import jax
import jax.numpy as jnp
from jax import lax
from jax.experimental import pallas as pl
from jax.experimental.pallas import tpu as pltpu

F32 = jnp.float32
BF16 = jnp.bfloat16
MESH = pl.DeviceIdType.MESH
NORM_EPS = 1e-6
MASKED = -1e30
LANES = 128
BF16_ROWS = 16
ROW_BLOCK_BYTES = 12 << 20
ADAM_LR, ADAM_B1, ADAM_B2, ADAM_EPS, ADAM_WD, ADAM_STEP = 0.001, 0.9, 0.999, 1e-08, 0.01, 10
N_CHIPS, N_CORES, N_DEV = 4, 2, 8
ATTN_HEADS_PER_STEP = 4
ATTN_STAGED_HEADS = 2
ANY = pl.BlockSpec(memory_space=pl.ANY)


def _tile(n, cap, quantum):
    best = None
    d = quantum
    while d <= min(n, cap):
        if n % d == 0:
            best = d
        d += quantum
    return n if best is None else best


def _call(body, *, name, out_shape, grid=(), in_specs=None, out_specs=None, scratch=(), sem=None, aliases=None):
    params = {} if sem is None else {"dimension_semantics": sem}
    return pl.pallas_call(
        body, name=name, grid=grid, in_specs=in_specs, out_specs=out_specs, out_shape=out_shape,
        scratch_shapes=list(scratch), input_output_aliases=aliases or {},
        compiler_params=pltpu.CompilerParams(**params))


def _call_prefetch(body, *, name, out_shape, grid, n_prefetch, in_specs, out_specs, scratch, sem):
    spec = pltpu.PrefetchScalarGridSpec(num_scalar_prefetch=n_prefetch, grid=grid, in_specs=in_specs,
                                        out_specs=out_specs, scratch_shapes=list(scratch))
    return pl.pallas_call(
        body, name=name, grid_spec=spec, out_shape=out_shape,
        compiler_params=pltpu.CompilerParams(dimension_semantics=sem))


def _place():
    x, y, c = lax.axis_index("x"), lax.axis_index("y"), lax.axis_index("c")
    return x, y, c


def _mm(a, b, mode, out_dtype, name):
    b_arr, b_idx = b if isinstance(b, tuple) else (b, None)
    bs = b_arr.shape[-2:]
    if mode == "nn":
        (M, K), (K2, N) = a.shape, bs
        dims = (((1,), (0,)), ((), ()))
    elif mode == "nt":
        (M, K), (N, K2) = a.shape, bs
        dims = (((1,), (1,)), ((), ()))
    else:
        (K, M), (K2, N) = a.shape, bs
        dims = (((0,), (0,)), ((), ()))
    assert K == K2, (name, a.shape, b_arr.shape)
    if mode == "tn":
        tm = _tile(M, 1408, LANES)
        tk = _tile(K, 2048, BF16_ROWS)
        tn = _tile(N, 512, LANES)
    else:
        tm = _tile(M, 1024, BF16_ROWS)
        tk = K if K <= 2816 else _tile(K, 2816, LANES)
        tn = _tile(N, 1408 if tk <= 1024 else 512, LANES)
    if tn < 256:
        tn = N
        tm = _tile(M, 512, LANES if mode == "tn" else BF16_ROWS)
    nk = K // tk
    grid = (M // tm, N // tn, nk)

    if mode == "tn":
        a_spec = pl.BlockSpec((tk, tm), lambda i, j, k: (k, i))
    else:
        a_spec = pl.BlockSpec((tm, tk), lambda i, j, k: (i, k))
    if mode == "nt":
        b_blk, b_map = (tn, tk), (lambda i, j, k: (j, k))
    else:
        b_blk, b_map = (tk, tn), (lambda i, j, k: (k, j))
    if b_idx is None:
        b_spec = pl.BlockSpec(b_blk, b_map)
    else:
        b_spec = pl.BlockSpec((None,) + b_blk, lambda i, j, k: (b_idx,) + b_map(i, j, k))

    def body(a_ref, b_ref, o_ref, *acc):
        r = lax.dot_general(a_ref[...].astype(BF16), b_ref[...].astype(BF16), dims, preferred_element_type=F32)
        if nk == 1:
            o_ref[...] = r.astype(o_ref.dtype)
        else:
            k = pl.program_id(2)

            @pl.when(k == 0)
            def _():
                acc[0][...] = r

            @pl.when(k > 0)
            def _():
                acc[0][...] += r

            @pl.when(k == nk - 1)
            def _():
                o_ref[...] = acc[0][...].astype(o_ref.dtype)

    return _call(
        body, name=name, grid=grid, in_specs=[a_spec, b_spec],
        out_specs=pl.BlockSpec((tm, tn), lambda i, j, k: (i, j)),
        out_shape=jax.ShapeDtypeStruct((M, N), out_dtype),
        scratch=[pltpu.VMEM((tm, tn), F32)] if nk > 1 else [],
        sem=("parallel", "parallel", "arbitrary"))(a, b_arr)


def _rowwise(fn, rows, pars, outs, pouts, name):
    R = rows[0].shape[0]
    row_bytes = 4 * (sum(max(r.shape[1], LANES) for r in rows) + sum(max(c, LANES) for c, _ in outs))
    tb = _tile(R, max(BF16_ROWS, ROW_BLOCK_BYTES // row_bytes), BF16_ROWS)
    nr, npar, no = len(rows), len(pars), len(outs)

    def body(*refs):
        r_in, p_in = refs[:nr], refs[nr:nr + npar]
        r_out, p_out = refs[nr + npar:nr + npar + no], refs[nr + npar + no:]
        ro, po = fn([r[...] for r in r_in], [p[...] for p in p_in])
        for ref, val in zip(r_out, ro):
            if isinstance(val, (tuple, list)):
                off = 0
                for piece in val:
                    w = piece.shape[1]
                    ref[:, off:off + w] = piece.astype(ref.dtype)
                    off += w
            else:
                ref[...] = val.astype(ref.dtype)
        if p_out:
            first = pl.program_id(0) == 0

            @pl.when(first)
            def _():
                for ref, val in zip(p_out, po):
                    ref[...] = val

            @pl.when(jnp.logical_not(first))
            def _():
                for ref, val in zip(p_out, po):
                    ref[...] += val

    res = _call(
        body, name=name, grid=(R // tb,),
        in_specs=[pl.BlockSpec((tb, r.shape[1]), lambda i: (i, 0)) for r in rows]
        + [pl.BlockSpec(p.shape, lambda i: (0, 0)) for p in pars],
        out_specs=[pl.BlockSpec((tb, c), lambda i: (i, 0)) for c, _ in outs]
        + [pl.BlockSpec(s, lambda i: (0, 0)) for s in pouts],
        out_shape=[jax.ShapeDtypeStruct((R, c), dt) for c, dt in outs]
        + [jax.ShapeDtypeStruct(s, F32) for s in pouts],
        sem=("arbitrary",) if pouts else ("parallel",))(*rows, *pars)
    return list(res)


def _rms(x, g):
    return x * lax.rsqrt(jnp.mean(x * x, axis=-1, keepdims=True) + NORM_EPS) * g


def _norm_mod(x, g, sh, sc):
    return _rms(x, g) * (1.0 + sc) + sh


def _gated_post(y, g, gate):
    return gate * _rms(y, g)


def _norm_mod_fwd(x, g, sh, sc, name):
    return _rowwise(lambda r, p: ([_norm_mod(r[0], *p)], []), [x], [g, sh, sc], [(x.shape[1], BF16)], [], name)[0]


def _norm_mod_bwd(dxo, dh, x, g, sh, sc, name):
    def fn(r, p):
        _, vjp = jax.vjp(_norm_mod, r[2], *p)
        dx, dg, dsh, dsc = vjp(r[1].astype(F32))
        return [r[0] + dx], [dg, dsh, dsc]
    c = x.shape[1]
    return _rowwise(fn, [dxo, dh, x], [g, sh, sc], [(c, F32)], [(1, c)] * 3, name)


def _post_fwd(x, y, g, gate, name):
    return _rowwise(lambda r, p: ([r[0] + _gated_post(r[1].astype(F32), *p)], []), [x, y], [g, gate],
                    [(x.shape[1], F32)], [], name)[0]


def _post_bwd(dxo, y, g, gate, name):
    def fn(r, p):
        _, vjp = jax.vjp(_gated_post, r[1].astype(F32), *p)
        dy, dg, dgate = vjp(r[0])
        return [dy], [dg, dgate]
    c = y.shape[1]
    return _rowwise(fn, [dxo, y], [g, gate], [(c, BF16)], [(1, c)] * 2, name)


def _swiglu(g, u):
    return jax.nn.silu(g) * u


def _ffn_up(h, w, l, hw, name):
    S, D = h.shape
    nb = w.shape[2] // (2 * hw)
    tm = _tile(S, 512, BF16_ROWS)

    def body(h_ref, w_ref, gu_ref, act_ref):
        gu = jnp.dot(h_ref[...], w_ref[...], preferred_element_type=F32).astype(BF16)
        gu_ref[...] = gu
        act_ref[...] = _swiglu(gu[:, :hw].astype(F32), gu[:, hw:].astype(F32)).astype(BF16)

    return _call(body, name=name, grid=(S // tm, nb),
                 in_specs=[pl.BlockSpec((tm, D), lambda i, j: (i, 0)), pl.BlockSpec((None, D, 2 * hw), lambda i, j: (l, 0, j))],
                 out_specs=[pl.BlockSpec((tm, 2 * hw), lambda i, j: (i, j)), pl.BlockSpec((tm, hw), lambda i, j: (i, j))],
                 out_shape=[jax.ShapeDtypeStruct((S, 2 * hw * nb), BF16), jax.ShapeDtypeStruct((S, hw * nb), BF16)],
                 sem=("parallel", "parallel"))(h, w)


def _ffn_down_dx(dy, w_dn, l, gu, hw, name):
    S, D = dy.shape
    nb = gu.shape[1] // (2 * hw)
    tm = _tile(S, 512, BF16_ROWS)

    def body(dy_ref, w_ref, gu_ref, dgu_ref):
        dact = lax.dot_general(dy_ref[...], w_ref[...], (((1,), (1,)), ((), ())), preferred_element_type=F32)
        _, vjp = jax.vjp(_swiglu, gu_ref[:, :hw].astype(F32), gu_ref[:, hw:].astype(F32))
        dg, du = vjp(dact)
        dgu_ref[:, :hw] = dg.astype(BF16)
        dgu_ref[:, hw:] = du.astype(BF16)

    return _call(body, name=name, grid=(S // tm, nb),
                 in_specs=[pl.BlockSpec((tm, D), lambda i, j: (i, 0)), pl.BlockSpec((None, hw, D), lambda i, j: (l, j, 0)),
                           pl.BlockSpec((tm, 2 * hw), lambda i, j: (i, j))],
                 out_specs=pl.BlockSpec((tm, 2 * hw), lambda i, j: (i, j)),
                 out_shape=jax.ShapeDtypeStruct(gu.shape, BF16), sem=("parallel", "parallel"))(dy, w_dn, gu)


def _silu_rows(c, name):
    return _rowwise(lambda r, p: ([jax.nn.silu(r[0])], []), [c], [], [(c.shape[1], F32)], [], name)[0]


def _head_norm(x, g, scale):
    return _rms(x, g) * scale


def _head_norm_fwd(x, g, scale, name):
    return _rowwise(lambda r, p: ([_head_norm(r[0].astype(F32), p[0], scale)], []), [x], [g],
                    [(x.shape[1], BF16)], [], name)[0]


def _head_norm_bwd(dy, x, g, scale, name):
    def fn(r, p):
        _, vjp = jax.vjp(lambda t, gg: _head_norm(t, gg, scale), r[1].astype(F32), p[0])
        dx, dg = vjp(r[0])
        return [dx], [dg]
    c = x.shape[1]
    return _rowwise(fn, [dy, x], [g], [(c, F32)], [(1, c)], name)


def _out_gate_fwd(o, qg, name):
    d = o.shape[1]
    return _rowwise(lambda r, p: ([r[0] * jax.nn.sigmoid(r[1][:, d:].astype(F32))], []), [o, qg], [],
                    [(d, BF16)], [], name)[0]


def _out_gate_bwd(dog, o, qg, name):
    d = o.shape[1]

    def fn(r, p):
        _, vjp = jax.vjp(lambda oo, gl: oo * jax.nn.sigmoid(gl), r[1], r[2][:, d:].astype(F32))
        do, dgl = vjp(r[0])
        return [do, dgl], []
    return _rowwise(fn, [dog, o, qg], [], [(d, BF16), (d, BF16)], [], name)


def _loss_bwd(y, tgt, name):
    n = y.shape[1]

    def fn(r, p):
        e = r[0] - r[1]
        part = jnp.sum(jnp.sum(e * e, axis=1, keepdims=True), axis=0, keepdims=True) * (0.5 / n)
        return [e * (1.0 / n)], [part]
    return _rowwise(fn, [y, tgt], [], [(n, F32)], [(1, 1)], name)


def _adamw(w, g, m, v, name):
    shape = w.shape
    c = shape[-1]
    flat = [t.reshape(-1, c) for t in (w, g, m, v)]

    def fn(r, p):
        w_, g_, m_, v_ = r
        m2 = ADAM_B1 * m_ + (1.0 - ADAM_B1) * g_
        v2 = ADAM_B2 * v_ + (1.0 - ADAM_B2) * (g_ * g_)
        m_hat = m2 / (1.0 - ADAM_B1 ** ADAM_STEP)
        v_hat = v2 / (1.0 - ADAM_B2 ** ADAM_STEP)
        delta = -ADAM_LR * (m_hat / (jnp.sqrt(v_hat) + ADAM_EPS) + ADAM_WD * w_)
        return [delta, m2, v2], []
    res = _rowwise(fn, flat, [], [(c, F32)] * 3, [], name)
    return [t.reshape(shape) for t in res]


def _sum_pair(a, b, name):
    c = a.shape[-1]
    out = _rowwise(lambda r, p: ([r[0].astype(F32) + r[1].astype(F32)], []), [a.reshape(-1, c), b.reshape(-1, c)], [],
                   [(c, BF16)], [], name)[0]
    return out.reshape(a.shape)


def _sum_slots(recv, name, out_dtype=F32):
    n = recv.shape[0]
    shape = recv.shape[1:]
    c = shape[-1]
    r3 = recv.reshape(n, -1, c)
    rows = r3.shape[1]
    tb = _tile(rows, max(BF16_ROWS, ROW_BLOCK_BYTES // (4 * c * (n + 1))), BF16_ROWS)

    def body(r_ref, o_ref):
        acc = r_ref[0].astype(F32)
        for s in range(1, n):
            acc = acc + r_ref[s].astype(F32)
        o_ref[...] = acc.astype(o_ref.dtype)

    out = _call(body, name=name, grid=(rows // tb,),
                in_specs=[pl.BlockSpec((n, tb, c), lambda i: (0, i, 0))],
                out_specs=pl.BlockSpec((tb, c), lambda i: (i, 0)),
                out_shape=jax.ShapeDtypeStruct((rows, c), out_dtype), sem=("parallel",))(r3)
    return out.reshape(shape)


def _gmlp_pre(zu, zv, b_u, b_v, ln_g, ln_b):
    u = jax.nn.gelu(zu + b_u, approximate=True)
    v = jax.nn.gelu(zv + b_v, approximate=True)
    xc = v - jnp.mean(v, axis=-1, keepdims=True)
    vn = xc * lax.rsqrt(jnp.mean(xc * xc, axis=-1, keepdims=True) + NORM_EPS) * ln_g + ln_b
    return u, vn


def _gmlp_fwd(zp, b_in, ln_g, ln_b, ws, bs_t, name):
    S, gw2 = zp.shape
    gw = gw2 // 2
    G, ch, _ = ws.shape
    gd = gw // G
    tb = 2 * ch

    def body(zp_ref, bin_ref, lg_ref, lb_ref, ws_ref, bs_ref, o_ref):
        u, vn = _gmlp_pre(zp_ref[:, :gw].astype(F32), zp_ref[:, gw:].astype(F32), bin_ref[:, :gw], bin_ref[:, gw:],
                          lg_ref[...], lb_ref[...])
        vnb = vn.astype(BF16)
        for c in range(tb // ch):
            for g in range(G):
                rs, cs = slice(c * ch, (c + 1) * ch), slice(g * gd, (g + 1) * gd)
                vv = jnp.dot(ws_ref[g], vnb[rs, cs], preferred_element_type=F32) + bs_ref[:, g:g + 1]
                o_ref[rs, cs] = (u[rs, cs] * vv).astype(o_ref.dtype)

    full = lambda a: pl.BlockSpec(a.shape, lambda i: (0,) * a.ndim)
    return _call(body, name=name, grid=(S // tb,),
                 in_specs=[pl.BlockSpec((tb, gw2), lambda i: (i, 0)), full(b_in), full(ln_g), full(ln_b), full(ws), full(bs_t)],
                 out_specs=pl.BlockSpec((tb, gw), lambda i: (i, 0)),
                 out_shape=jax.ShapeDtypeStruct((S, gw), BF16), sem=("parallel",))(zp, b_in, ln_g, ln_b, ws, bs_t)


def _gmlp_bwd(dyg, zp, b_in, ln_g, ln_b, ws, ws_t, bs_t, name):
    S, gw2 = zp.shape
    gw = gw2 // 2
    G, ch, _ = ws.shape
    gd = gw // G
    tb = 2 * ch

    def body(dy_ref, zp_ref, bin_ref, lg_ref, lb_ref, ws_ref, wst_ref, bs_ref,
             dzp_ref, dbin_ref, dlg_ref, dlb_ref, dws_ref, dbs_ref, du_sc, dvn_sc):
        (u, vn), vjp = jax.vjp(_gmlp_pre, zp_ref[:, :gw].astype(F32), zp_ref[:, gw:].astype(F32), bin_ref[:, :gw],
                               bin_ref[:, gw:], lg_ref[...], lb_ref[...])
        vnb = vn.astype(BF16)
        first = pl.program_id(0) == 0

        @pl.when(first)
        def _():
            dws_ref[...] = jnp.zeros_like(dws_ref)

        lane = lax.broadcasted_iota(jnp.int32, (ch, G), 1)
        dbs = jnp.zeros((ch, G), F32)
        for g in range(G):
            cs = slice(g * gd, (g + 1) * gd)
            dws_g = jnp.zeros((ch, ch), F32)
            col = jnp.zeros((ch, 1), F32)
            for c in range(tb // ch):
                rs = slice(c * ch, (c + 1) * ch)
                vnp = vnb[rs, cs]
                vv = jnp.dot(ws_ref[g], vnp, preferred_element_type=F32) + bs_ref[:, g:g + 1]
                dy = dy_ref[rs, cs].astype(F32)
                du_sc[rs, cs] = dy * vv
                dvv = dy * u[rs, cs]
                dvvb = dvv.astype(BF16)
                dvn_sc[rs, cs] = jnp.dot(wst_ref[g], dvvb, preferred_element_type=F32)
                dws_g = dws_g + lax.dot_general(dvvb, vnp, (((1,), (1,)), ((), ())), preferred_element_type=F32)
                col = col + jnp.sum(dvv, axis=1, keepdims=True)
            dws_ref[g] += dws_g
            dbs = jnp.where(lane == g, col, dbs)
        dzu, dzv, dbu, dbv, dlg, dlb = vjp((du_sc[...], dvn_sc[...]))
        dzp_ref[:, :gw] = dzu.astype(dzp_ref.dtype)
        dzp_ref[:, gw:] = dzv.astype(dzp_ref.dtype)

        @pl.when(first)
        def _():
            dbin_ref[:, :gw] = dbu
            dbin_ref[:, gw:] = dbv
            dlg_ref[...] = dlg
            dlb_ref[...] = dlb
            dbs_ref[...] = dbs

        @pl.when(jnp.logical_not(first))
        def _():
            dbin_ref[:, :gw] += dbu
            dbin_ref[:, gw:] += dbv
            dlg_ref[...] += dlg
            dlb_ref[...] += dlb
            dbs_ref[...] += dbs

    full = lambda a: pl.BlockSpec(a.shape, lambda i: (0,) * a.ndim)
    fshape = lambda s: pl.BlockSpec(s, lambda i: (0,) * len(s))
    return _call(
        body, name=name, grid=(S // tb,),
        in_specs=[pl.BlockSpec((tb, gw), lambda i: (i, 0)), pl.BlockSpec((tb, gw2), lambda i: (i, 0)),
                  full(b_in), full(ln_g), full(ln_b), full(ws), full(ws_t), full(bs_t)],
        out_specs=[pl.BlockSpec((tb, gw2), lambda i: (i, 0)), fshape((1, gw2)), fshape((1, gw)), fshape((1, gw)),
                   fshape((G, ch, ch)), fshape((ch, G))],
        out_shape=[jax.ShapeDtypeStruct((S, gw2), BF16), jax.ShapeDtypeStruct((1, gw2), F32),
                   jax.ShapeDtypeStruct((1, gw), F32), jax.ShapeDtypeStruct((1, gw), F32),
                   jax.ShapeDtypeStruct((G, ch, ch), F32), jax.ShapeDtypeStruct((ch, G), F32)],
        scratch=[pltpu.VMEM((tb, gw), F32), pltpu.VMEM((tb, gw), F32)],
        sem=("arbitrary",))(dyg, zp, b_in, ln_g, ln_b, ws, ws_t, bs_t)


def _dot_01(x, ones_bf16):
    hi = x.astype(BF16)
    r1 = x - hi.astype(F32)
    mid = r1.astype(BF16)
    lo = (r1 - mid.astype(F32)).astype(BF16)
    dot = lambda t: jnp.dot(t, ones_bf16, preferred_element_type=F32)
    return dot(hi) + dot(mid) + dot(lo)


def _log_sigmoid(x):
    return jnp.minimum(x, 0.0) - jnp.log1p(jnp.exp(-jnp.abs(x)))


def _dcum_fwd(f_t, b_col, name):
    H, S = f_t.shape
    tb = _tile(S, 512, LANES)

    def body(f_ref, b_ref, o_ref, carry):
        @pl.when(pl.program_id(0) == 0)
        def _():
            carry[...] = jnp.zeros_like(carry)

        ls = _log_sigmoid(f_ref[...] + b_ref[...])
        r = lax.broadcasted_iota(jnp.int32, (tb, tb), 0)
        c = lax.broadcasted_iota(jnp.int32, (tb, tb), 1)
        upper = (r <= c).astype(BF16)
        o_ref[...] = _dot_01(ls, upper) + carry[...]
        carry[...] += jnp.sum(ls, axis=1, keepdims=True)

    return _call(body, name=name, grid=(S // tb,),
                 in_specs=[pl.BlockSpec((H, tb), lambda i: (0, i)), pl.BlockSpec((H, 1), lambda i: (0, 0))],
                 out_specs=pl.BlockSpec((H, tb), lambda i: (0, i)),
                 out_shape=jax.ShapeDtypeStruct((H, S), F32),
                 scratch=[pltpu.VMEM((H, 1), F32)], sem=("arbitrary",))(f_t, b_col)


def _dcum_bwd(dd_t, f_t, b_col, name):
    H, S = f_t.shape
    tb = _tile(S, 512, LANES)
    nb = S // tb

    def body(dd_ref, f_ref, b_ref, df_ref, db_ref, carry):
        first = pl.program_id(0) == 0

        @pl.when(first)
        def _():
            carry[...] = jnp.zeros_like(carry)

        dd = dd_ref[...]
        r = lax.broadcasted_iota(jnp.int32, (tb, tb), 0)
        c = lax.broadcasted_iota(jnp.int32, (tb, tb), 1)
        lower = (r >= c).astype(BF16)
        rev = _dot_01(dd, lower) + carry[...]
        carry[...] += jnp.sum(dd, axis=1, keepdims=True)
        df = rev * jax.nn.sigmoid(-(f_ref[...] + b_ref[...]))
        df_ref[...] = df
        part = jnp.sum(df, axis=1, keepdims=True)

        @pl.when(first)
        def _():
            db_ref[...] = part

        @pl.when(jnp.logical_not(first))
        def _():
            db_ref[...] += part

    return _call(body, name=name, grid=(nb,),
                 in_specs=[pl.BlockSpec((H, tb), lambda i: (0, nb - 1 - i)), pl.BlockSpec((H, tb), lambda i: (0, nb - 1 - i)),
                           pl.BlockSpec((H, 1), lambda i: (0, 0))],
                 out_specs=[pl.BlockSpec((H, tb), lambda i: (0, nb - 1 - i)), pl.BlockSpec((H, 1), lambda i: (0, 0))],
                 out_shape=[jax.ShapeDtypeStruct((H, S), F32), jax.ShapeDtypeStruct((H, 1), F32)],
                 scratch=[pltpu.VMEM((H, 1), F32)], sem=("arbitrary",))(dd_t, f_t, b_col)


def _attn_tile(S):
    return _tile(S, 512, LANES)


def _causal(t, transposed):
    r = lax.broadcasted_iota(jnp.int32, (t, t), 0)
    c = lax.broadcasted_iota(jnp.int32, (t, t), 1)
    return (r <= c) if transposed else (c <= r)


def _tri_pairs(n, key_major):
    if key_major:
        pairs = [(i, j) for j in range(n) for i in range(j, n)]
    else:
        pairs = [(i, j) for i in range(n) for j in range(i + 1)]
    return jnp.asarray([p[0] for p in pairs], jnp.int32), jnp.asarray([p[1] for p in pairs], jnp.int32)


def _split3(x):
    hi = lax.reduce_precision(x, 8, 7)
    r = x - hi
    mid = lax.reduce_precision(r, 8, 7)
    lo = lax.reduce_precision(r - mid, 8, 7)
    return hi.astype(BF16), mid.astype(BF16), lo.astype(BF16)


def _augment(xn, dcum, query):
    H, S, hd = xn.shape
    parts = list(_split3(dcum))
    vals = parts + [1.0] * 3 if query else [1.0] * 3 + [-p for p in parts]
    lane = lax.broadcasted_iota(jnp.int32, (1, 1, LANES), 2)
    out = jnp.pad(xn, ((0, 0), (0, 0), (0, LANES - hd)))
    for k, val in enumerate(vals):
        val = jnp.asarray(val, BF16)
        out = jnp.where(lane == hd + k, val[..., None] if val.ndim else val, out)
    return out


def _scores_t(k_ref, qt_ref, h, t, diag):
    st = jnp.dot(k_ref[h], qt_ref[h], preferred_element_type=F32)
    return jnp.where(_causal(t, True), st, MASKED) if diag else st


def _flash_fwd(ka, qat, vat, hd, name):
    H, S, da = ka.shape
    t = _attn_tile(S)
    hb = ATTN_HEADS_PER_STEP
    it, jt = _tri_pairs(S // t, False)

    def body(it_ref, jt_ref, k_ref, qt_ref, vt_ref, o_ref, lse_ref, m_sc, acc_sc):
        i, j = it_ref[pl.program_id(1)], jt_ref[pl.program_id(1)]

        @pl.when(j == 0)
        def _():
            m_sc[...] = jnp.full_like(m_sc, MASKED)
            acc_sc[...] = jnp.zeros_like(acc_sc)

        def step(diag):
            sts = [_scores_t(k_ref, qt_ref, h, t, diag) for h in range(hb)]
            pts, alphas = [], []
            for h in range(hb):
                m_prev = m_sc[h]
                m_new = jnp.maximum(m_prev, jnp.max(sts[h], axis=0, keepdims=True))
                pts.append(jnp.exp(sts[h] - m_new).astype(BF16))
                alphas.append(jnp.exp(m_prev - m_new))
                m_sc[h] = m_new
            for h in range(hb):
                acc_sc[h] = alphas[h] * acc_sc[h] + jnp.dot(vt_ref[h], pts[h], preferred_element_type=F32)

        @pl.when(j < i)
        def _():
            step(False)

        @pl.when(j == i)
        def _():
            step(True)
            for h in range(hb):
                l = acc_sc[h, hd:hd + 1, :]
                o_ref[h] = acc_sc[h, :hd, :] / l
                lse_ref[h] = m_sc[h] + jnp.log(l)

    qcol = lambda h, p, it_, jt_: (h, 0, it_[p])
    kcol = lambda h, p, it_, jt_: (h, 0, jt_[p])
    krow = lambda h, p, it_, jt_: (h, jt_[p], 0)
    return _call_prefetch(
        body, name=name, grid=(H // hb, it.shape[0]), n_prefetch=2,
        in_specs=[pl.BlockSpec((hb, t, da), krow), pl.BlockSpec((hb, da, t), qcol), pl.BlockSpec((hb, da, t), kcol)],
        out_specs=[pl.BlockSpec((hb, hd, t), qcol), pl.BlockSpec((hb, 1, t), qcol)],
        out_shape=[jax.ShapeDtypeStruct((H, hd, S), F32), jax.ShapeDtypeStruct((H, 1, S), F32)],
        scratch=[pltpu.VMEM((hb, 1, t), F32), pltpu.VMEM((hb, da, t), F32)],
        sem=("parallel", "arbitrary"))(it, jt, ka, qat, vat)


def _flash_bwd(ka, kat, qat, v, dot, o_tr, lse_r, name):
    H, S, hd = v.shape
    da = ka.shape[2]
    t = _attn_tile(S)
    n = S // t
    hb = ATTN_HEADS_PER_STEP
    it, jt = _tri_pairs(n, True)
    over_queries = (((1,), (1,)), ((), ()))

    def body(it_ref, jt_ref, k_ref, kt_ref, qt_ref, v_ref, dot_ref, o_ref, lse_ref, dq_ref, dk_ref, dv_ref, dk_sc, dv_sc):
        i, j = it_ref[pl.program_id(1)], jt_ref[pl.program_id(1)]

        @pl.when(pl.program_id(1) == 0)
        def _():
            dq_ref[...] = jnp.zeros_like(dq_ref)

        def step(diag):
            for h0 in range(0, hb, ATTN_STAGED_HEADS):
                hs = range(h0, min(h0 + ATTN_STAGED_HEADS, hb))
                sts = [_scores_t(k_ref, qt_ref, h, t, diag) for h in hs]
                dpts = [jnp.dot(v_ref[h], dot_ref[h], preferred_element_type=F32) for h in hs]
                tiles = []
                for h, st, dpt in zip(hs, sts, dpts):
                    dl = jnp.sum(dot_ref[h].astype(F32) * o_ref[h], axis=0, keepdims=True)
                    pt = jnp.exp(st - lse_ref[h])
                    tiles.append((pt.astype(BF16), (pt * (dpt - dl)).astype(BF16)))
                for h, (ptb, dsb) in zip(hs, tiles):
                    dv_sc[h] += lax.dot_general(dot_ref[h], ptb, over_queries, preferred_element_type=F32)
                    dk_sc[h] += lax.dot_general(qt_ref[h], dsb, over_queries, preferred_element_type=F32)
                    dq_ref[h, i] += jnp.dot(kt_ref[h], dsb, preferred_element_type=F32)

        @pl.when(i == j)
        def _():
            dk_sc[...] = jnp.zeros_like(dk_sc)
            dv_sc[...] = jnp.zeros_like(dv_sc)
            step(True)

        @pl.when(i > j)
        def _():
            step(False)

        @pl.when(i == n - 1)
        def _():
            dk_ref[...] = dk_sc[...]
            dv_ref[...] = dv_sc[...]

    krow = lambda h, p, it_, jt_: (h, jt_[p], 0)
    kcol = lambda h, p, it_, jt_: (h, 0, jt_[p])
    qcol = lambda h, p, it_, jt_: (h, 0, it_[p])
    return _call_prefetch(
        body, name=name, grid=(H // hb, it.shape[0]), n_prefetch=2,
        in_specs=[pl.BlockSpec((hb, t, da), krow), pl.BlockSpec((hb, da, t), kcol), pl.BlockSpec((hb, da, t), qcol),
                  pl.BlockSpec((hb, t, hd), krow), pl.BlockSpec((hb, hd, t), qcol), pl.BlockSpec((hb, hd, t), qcol),
                  pl.BlockSpec((hb, 1, t), qcol)],
        out_specs=[pl.BlockSpec((hb, n, da, t), lambda h, p, it_, jt_: (h, 0, 0, 0)), pl.BlockSpec((hb, da, t), kcol),
                   pl.BlockSpec((hb, hd, t), kcol)],
        out_shape=[jax.ShapeDtypeStruct((H, n, da, t), F32), jax.ShapeDtypeStruct((H, da, S), F32),
                   jax.ShapeDtypeStruct((H, hd, S), F32)],
        scratch=[pltpu.VMEM((hb, da, t), F32), pltpu.VMEM((hb, hd, t), F32)],
        sem=("parallel", "arbitrary"))(it, jt, ka, kat, qat, v, dot, o_tr, lse_r)


def _offsets(n_bits):
    return [tuple((k >> b) & 1 for b in reversed(range(n_bits))) for k in range(1, 1 << n_bits)]


def _own_slot(out, own, idx):
    return lax.dynamic_update_index_in_dim(out, own.astype(out.dtype), idx, 0)


def _gather8(arrs, name):
    n = len(arrs)
    offs = _offsets(3)

    def body(*refs):
        ins, outs = refs[:n], refs[n:2 * n]
        ssem, rsem = refs[2 * n:]
        x, y, c = _place()
        me = 4 * x + 2 * y + c
        copies = []
        for a in range(n):
            for k, (dx, dy, dcc) in enumerate(offs):
                cp = pltpu.make_async_remote_copy(
                    src_ref=ins[a], dst_ref=outs[a].at[me], send_sem=ssem.at[a, k], recv_sem=rsem.at[a, k],
                    device_id=((x + dx) % 2, (y + dy) % 2, (c + dcc) % 2), device_id_type=MESH)
                cp.start()
                copies.append(cp)
        for cp in copies:
            cp.wait()

    return _call(body, name=name, in_specs=[ANY] * n, out_specs=[ANY] * n,
                 out_shape=[jax.ShapeDtypeStruct((N_DEV,) + a.shape, a.dtype) for a in arrs],
                 scratch=[pltpu.SemaphoreType.DMA((n, 7)), pltpu.SemaphoreType.DMA((n, 7))])(*arrs)


def _chip_gather(arrs, halved, name):
    n = len(arrs)
    offs = _offsets(2)

    def body(*refs):
        ins, outs = refs[:n], refs[n:2 * n]
        ssem, rsem = refs[2 * n:]
        x, y, c = _place()
        chip = 2 * x + y
        copies = []
        for a in range(n):
            if halved:
                hn = arrs[a].shape[0] // 2
                src = ins[a].at[pl.ds(c * hn, hn)]
                dst = outs[a].at[chip, pl.ds(c * hn, hn)]
            else:
                src, dst = ins[a], outs[a].at[chip]
            for k, (dx, dy) in enumerate(offs):
                cp = pltpu.make_async_remote_copy(
                    src_ref=src, dst_ref=dst, send_sem=ssem.at[a, k], recv_sem=rsem.at[a, k],
                    device_id=((x + dx) % 2, (y + dy) % 2, c), device_id_type=MESH)
                cp.start()
                copies.append(cp)
        for cp in copies:
            cp.wait()

    return _call(body, name=name, in_specs=[ANY] * n, out_specs=[ANY] * n,
                 out_shape=[jax.ShapeDtypeStruct((N_CHIPS,) + a.shape, a.dtype) for a in arrs],
                 scratch=[pltpu.SemaphoreType.DMA((n, 3)), pltpu.SemaphoreType.DMA((n, 3))])(*arrs)


def _sibling_fill(bufs, name):
    n = len(bufs)
    offs = _offsets(2)

    def body(*refs):
        ins, outs = refs[:n], refs[n:2 * n]
        ssem, rsem = refs[2 * n:]
        x, y, c = _place()
        copies = []
        for a in range(n):
            hn = bufs[a].shape[1] // 2
            for k, (dx, dy) in enumerate(offs):
                chip = 2 * ((x + dx) % 2) + (y + dy) % 2
                cp = pltpu.make_async_remote_copy(
                    src_ref=ins[a].at[chip, pl.ds(c * hn, hn)], dst_ref=outs[a].at[chip, pl.ds(c * hn, hn)],
                    send_sem=ssem.at[a, k], recv_sem=rsem.at[a, k],
                    device_id=(x, y, 1 - c), device_id_type=MESH)
                cp.start()
                copies.append(cp)
        for cp in copies:
            cp.wait()

    return _call(body, name=name, in_specs=[ANY] * n, out_specs=[ANY] * n,
                 out_shape=[jax.ShapeDtypeStruct(b.shape, b.dtype) for b in bufs],
                 scratch=[pltpu.SemaphoreType.DMA((n, 3)), pltpu.SemaphoreType.DMA((n, 3))],
                 aliases={a: a for a in range(n)})(*bufs)


def _sibling_pair(arrs, name):
    n = len(arrs)

    def body(*refs):
        ins, outs = refs[:n], refs[n:2 * n]
        ssem, rsem = refs[2 * n:]
        x, y, c = _place()
        copies = []
        for a in range(n):
            cp = pltpu.make_async_remote_copy(
                src_ref=ins[a], dst_ref=outs[a].at[c], send_sem=ssem.at[a], recv_sem=rsem.at[a],
                device_id=(x, y, 1 - c), device_id_type=MESH)
            cp.start()
            copies.append(cp)
        for cp in copies:
            cp.wait()

    return _call(body, name=name, in_specs=[ANY] * n, out_specs=[ANY] * n,
                 out_shape=[jax.ShapeDtypeStruct((N_CORES,) + a.shape, a.dtype) for a in arrs],
                 scratch=[pltpu.SemaphoreType.DMA((n,)), pltpu.SemaphoreType.DMA((n,))])(*arrs)


CHIP_ORDER = (0, 1, 2, 3)


def _piece(shape, spec, j, h):
    shard_ax, half_ax, order = spec
    w = shape[shard_ax] // N_CHIPS
    idx = [slice(None)] * len(shape)
    idx[shard_ax] = pl.ds(order[j] * w, w)
    assert half_ax != shard_ax
    hn = shape[half_ax] // 2
    idx[half_ax] = pl.ds(h * hn, hn)
    return tuple(idx)


def _piece_shape(shape, spec):
    shard_ax, half_ax, _ = spec
    s = list(shape)
    s[shard_ax] //= N_CHIPS
    s[half_ax] //= 2
    return tuple(s)


def _own_pieces(g, spec, c):
    shard_ax, half_ax, order = spec
    hn = g.shape[half_ax] // 2
    half = lax.dynamic_slice_in_dim(g, c * hn, hn, axis=half_ax)
    w = half.shape[shard_ax] // N_CHIPS
    return jnp.stack([lax.slice_in_dim(half, p * w, (p + 1) * w, axis=shard_ax) for p in order])


def _sibling_scatter(arrs, specs, name):
    n = len(arrs)

    def body(*refs):
        ins, outs = refs[:n], refs[n:2 * n]
        ssem, rsem = refs[2 * n:]
        x, y, c = _place()
        for mine in range(N_CORES):
            @pl.when(c == mine)
            def _():
                copies = []
                for a in range(n):
                    for j in range(N_CHIPS):
                        cp = pltpu.make_async_remote_copy(
                            src_ref=ins[a].at[_piece(arrs[a].shape, specs[a], j, 1 - mine)], dst_ref=outs[a].at[j],
                            send_sem=ssem.at[a, j], recv_sem=rsem.at[a, j],
                            device_id=(x, y, 1 - mine), device_id_type=MESH)
                        cp.start()
                        copies.append(cp)
                for cp in copies:
                    cp.wait()

    return _call(body, name=name, in_specs=[ANY] * n, out_specs=[ANY] * n,
                 out_shape=[jax.ShapeDtypeStruct((N_CHIPS,) + _piece_shape(a.shape, s), a.dtype)
                            for a, s in zip(arrs, specs)],
                 scratch=[pltpu.SemaphoreType.DMA((n, N_CHIPS))] * 2)(*arrs)


def _chip_scatter(arrs, name):
    n = len(arrs)
    offs = _offsets(2)

    def body(*refs):
        ins, outs = refs[:n], refs[n:2 * n]
        ssem, rsem = refs[2 * n:]
        x, y, c = _place()
        chip = 2 * x + y
        copies = []
        for a in range(n):
            for k, (dx, dy) in enumerate(offs):
                tx, ty = (x + dx) % 2, (y + dy) % 2
                cp = pltpu.make_async_remote_copy(
                    src_ref=ins[a].at[2 * tx + ty], dst_ref=outs[a].at[chip], send_sem=ssem.at[a, k], recv_sem=rsem.at[a, k],
                    device_id=(tx, ty, c), device_id_type=MESH)
                cp.start()
                copies.append(cp)
        for cp in copies:
            cp.wait()

    return _call(body, name=name, in_specs=[ANY] * n, out_specs=[ANY] * n,
                 out_shape=[jax.ShapeDtypeStruct(a.shape, a.dtype) for a in arrs],
                 scratch=[pltpu.SemaphoreType.DMA((n, 3)), pltpu.SemaphoreType.DMA((n, 3))])(*arrs)


def kernel(x, c, ada_w, ada_b, pre_mix_g, post_mix_g, pre_ffn_g, post_ffn_g, ffn_w_gu, ffn_w_down, a_w_in, a_b_in, a_ln_g, a_ln_b, a_w_s, a_b_s, a_w_out, kv_ada_w, kv_ada_b, kv_norm_g, kv_w, kv_b_f, k_norm_g, b_w_qg, b_q_norm_g, b_w_o, loss_target, m_ada_w, m_ada_b, m_pre_mix_g, m_post_mix_g, m_pre_ffn_g, m_post_ffn_g, m_ffn_w_gu, m_ffn_w_down, m_a_w_in, m_a_b_in, m_a_ln_g, m_a_ln_b, m_a_w_s, m_a_b_s, m_a_w_out, m_kv_ada_w, m_kv_ada_b, m_kv_norm_g, m_kv_w, m_kv_b_f, m_k_norm_g, m_b_w_qg, m_b_q_norm_g, m_b_w_o, v_ada_w, v_ada_b, v_pre_mix_g, v_post_mix_g, v_pre_ffn_g, v_post_ffn_g, v_ffn_w_gu, v_ffn_w_down, v_a_w_in, v_a_b_in, v_a_ln_g, v_a_ln_b, v_a_w_s, v_a_b_s, v_a_w_out, v_kv_ada_w, v_kv_ada_b, v_kv_norm_g, v_kv_w, v_kv_b_f, v_k_norm_g, v_b_w_qg, v_b_q_norm_g, v_b_w_o):
    weights = dict(ada_w=ada_w, ada_b=ada_b, pre_mix_g=pre_mix_g, post_mix_g=post_mix_g, pre_ffn_g=pre_ffn_g,
                   post_ffn_g=post_ffn_g, ffn_w_gu=ffn_w_gu, ffn_w_down=ffn_w_down, a_w_in=a_w_in, a_b_in=a_b_in,
                   a_ln_g=a_ln_g, a_ln_b=a_ln_b, a_w_s=a_w_s, a_b_s=a_b_s, a_w_out=a_w_out, kv_ada_w=kv_ada_w,
                   kv_ada_b=kv_ada_b, kv_norm_g=kv_norm_g, kv_w=kv_w, kv_b_f=kv_b_f, k_norm_g=k_norm_g, b_w_qg=b_w_qg,
                   b_q_norm_g=b_q_norm_g, b_w_o=b_w_o)
    m_in = dict(ada_w=m_ada_w, ada_b=m_ada_b, pre_mix_g=m_pre_mix_g, post_mix_g=m_post_mix_g, pre_ffn_g=m_pre_ffn_g,
                post_ffn_g=m_post_ffn_g, ffn_w_gu=m_ffn_w_gu, ffn_w_down=m_ffn_w_down, a_w_in=m_a_w_in, a_b_in=m_a_b_in,
                a_ln_g=m_a_ln_g, a_ln_b=m_a_ln_b, a_w_s=m_a_w_s, a_b_s=m_a_b_s, a_w_out=m_a_w_out, kv_ada_w=m_kv_ada_w,
                kv_ada_b=m_kv_ada_b, kv_norm_g=m_kv_norm_g, kv_w=m_kv_w, kv_b_f=m_kv_b_f, k_norm_g=m_k_norm_g,
                b_w_qg=m_b_w_qg, b_q_norm_g=m_b_q_norm_g, b_w_o=m_b_w_o)
    v_in = dict(ada_w=v_ada_w, ada_b=v_ada_b, pre_mix_g=v_pre_mix_g, post_mix_g=v_post_mix_g, pre_ffn_g=v_pre_ffn_g,
                post_ffn_g=v_post_ffn_g, ffn_w_gu=v_ffn_w_gu, ffn_w_down=v_ffn_w_down, a_w_in=v_a_w_in, a_b_in=v_a_b_in,
                a_ln_g=v_a_ln_g, a_ln_b=v_a_ln_b, a_w_s=v_a_w_s, a_b_s=v_a_b_s, a_w_out=v_a_w_out, kv_ada_w=v_kv_ada_w,
                kv_ada_b=v_kv_ada_b, kv_norm_g=v_kv_norm_g, kv_w=v_kv_w, kv_b_f=v_kv_b_f, k_norm_g=v_k_norm_g,
                b_w_qg=v_b_w_qg, b_q_norm_g=v_b_q_norm_g, b_w_o=v_b_w_o)
    names = list(weights)

    S, D = x.shape[1], x.shape[2]
    L, NA, NB = ada_w.shape[0], a_w_in.shape[0], b_w_qg.shape[0]
    H = kv_b_f.shape[0]
    hd = D // H
    G, CH = a_w_s.shape[1], a_w_s.shape[2]
    GW = a_w_out.shape[1] * N_CHIPS
    F = ffn_w_down.shape[1] * N_CHIPS
    ada_cols = ada_w.shape[2]
    kvada_cols = kv_ada_w.shape[1]
    kv_cols = kv_w.shape[1]
    kv_pad = -(-(2 * D + H) // LANES) * LANES
    xi, yi, ci = _place()
    chip = 2 * xi + yi
    me = 2 * chip + ci
    x0 = x[0]
    tgt = loss_target[0]
    row = lambda t: t.reshape(1, -1)

    c_all = _own_slot(_gather8([c], "gather_c")[0], c, me).reshape(N_DEV, D)
    c_act = _silu_rows(jnp.pad(c_all, ((0, BF16_ROWS - N_DEV), (0, 0))), "silu_c")
    mod_sh = [_mm(c_act, (ada_w, l), "nn", F32, f"mod_proj_{l}") for l in range(L)]
    mod_sh.append(_mm(c_act, kv_ada_w, "nn", F32, "mod_proj_kv"))
    mod_sh = jnp.concatenate(mod_sh, axis=1)
    small_sh = [mod_sh, a_b_in, a_ln_g, a_ln_b]
    mod_all, b_in_all, ln_g_all, ln_b_all = [
        _own_slot(o, s, chip) for o, s in zip(_chip_gather(small_sh, False, "gather_mod"), small_sh)]
    mine = lax.dynamic_index_in_dim(mod_all, me, axis=1, keepdims=False)
    mod = [jnp.concatenate([mine[j, l * ada_cols:(l + 1) * ada_cols] for j in range(N_CHIPS)]) + ada_b[l] for l in range(L)]
    mod = [[row(t) for t in jnp.split(m_, 6)] for m_ in mod]
    mod_kv = jnp.concatenate([mine[j, L * ada_cols:] for j in range(N_CHIPS)]) + kv_ada_b
    kv_sh, kv_sc = [row(t) for t in jnp.split(mod_kv, 2)]
    cat_chips = lambda t, ax: jnp.concatenate([t[j] for j in range(N_CHIPS)], axis=ax)
    b_in_f = cat_chips(b_in_all, 1)
    ln_g_f, ln_b_f = cat_chips(ln_g_all, 1), cat_chips(ln_b_all, 1)

    big = ["ffn_w_gu", "ffn_w_down", "a_w_in", "a_w_out", "kv_w", "b_w_qg", "b_w_o"]
    own_w = [weights[n].astype(BF16) for n in big]
    gathered = _sibling_fill(_chip_gather(own_w, True, "gather_w"), "fill_w")
    gathered = {n: _own_slot(g, w, chip) for n, g, w in zip(big, gathered, own_w)}
    gu_hw = ffn_w_gu.shape[2]
    w_gu = jnp.concatenate([gathered["ffn_w_gu"][j] for j in (0, 2, 1, 3)], axis=2)
    w_dn = cat_chips(gathered["ffn_w_down"], 1)
    w_in = cat_chips(gathered["a_w_in"], 2)
    w_out = cat_chips(gathered["a_w_out"], 1)
    w_kv = jnp.pad(cat_chips(gathered["kv_w"], 1), ((0, 0), (0, kv_pad - (2 * D + H))))
    w_qg = cat_chips(gathered["b_w_qg"], 2)
    w_o = cat_chips(gathered["b_w_o"], 1)

    causal = jnp.tril(jnp.ones((CH, CH), F32))
    ws_m = [(a_w_s[i] * causal).astype(BF16) for i in range(NA)]
    ws_mt = [jnp.swapaxes(w, 1, 2) for w in ws_m]
    bs_t = [a_b_s[i].T for i in range(NA)]

    heads = lambda t: t.reshape(S, H, hd).transpose(1, 0, 2)
    unheads = lambda t: t.transpose(1, 0, 2).reshape(S, D)

    saved = []
    kv = None
    xc = x0
    for l in range(L):
        sh_m, sc_m, g_m, sh_f, sc_f, g_f = mod[l]
        st = {"x0": xc}
        h1 = _norm_mod_fwd(xc, row(pre_mix_g[l]), sh_m, sc_m, f"pre_mix_{l}")
        st["h1"] = h1
        if l < NA:
            zp = _mm(h1, (w_in, l), "nn", BF16, f"gmlp_in_{l}")
            yg = _gmlp_fwd(zp, row(b_in_f[l]), row(ln_g_f[l]), row(ln_b_f[l]), ws_m[l], bs_t[l], f"gmlp_gate_{l}")
            y = _mm(yg, (w_out, l), "nn", F32, f"gmlp_out_{l}")
            st.update(zp=zp, yg=yg)
        else:
            jb = l - NA
            qg = _mm(h1, (w_qg, jb), "nn", BF16, f"fox_qg_{jb}")
            q_raw = heads(qg[:, :D]).reshape(H * S, hd)
            qn = _head_norm_fwd(q_raw, row(b_q_norm_g[jb]), hd ** -0.5, f"fox_qnorm_{jb}").reshape(H, S, hd)
            qa = _augment(qn, kv["dcum"], True)
            qat = jnp.swapaxes(qa, 1, 2)
            o_tr, lse_r = _flash_fwd(kv["ka"], qat, kv["vat"], hd, f"fox_attn_{jb}")
            o_t = o_tr.transpose(2, 0, 1).reshape(S, D)
            og = _out_gate_fwd(o_t, qg, f"fox_gate_{jb}")
            y = _mm(og, (w_o, jb), "nn", F32, f"fox_out_{jb}")
            st.update(qg=qg, q_raw=q_raw, qat=qat, o_tr=o_tr, lse_r=lse_r, o_t=o_t, og=og)
        st["y"] = y
        x1 = _post_fwd(xc, y, row(post_mix_g[l]), g_m, f"post_mix_{l}")
        st["x1"] = x1
        h2 = _norm_mod_fwd(x1, row(pre_ffn_g[l]), sh_f, sc_f, f"pre_ffn_{l}")
        gu, act = _ffn_up(h2, w_gu, l, gu_hw, f"ffn_gu_{l}")
        y2 = _mm(act, (w_dn, l), "nn", F32, f"ffn_down_{l}")
        xc = _post_fwd(x1, y2, row(post_ffn_g[l]), g_f, f"post_ffn_{l}")
        st.update(h2=h2, gu=gu, act=act, y2=y2)
        saved.append(st)
        if l == NA - 1:
            hk = _norm_mod_fwd(xc, row(kv_norm_g), kv_sh, kv_sc, "kv_pre")
            kvf = _mm(hk, w_kv, "nn", F32, "kv_proj")
            k_raw = heads(kvf[:, :D]).reshape(H * S, hd)
            kn = _head_norm_fwd(k_raw, row(k_norm_g), 1.0, "kv_knorm").reshape(H, S, hd)
            vb = heads(kvf[:, D:2 * D]).astype(BF16)
            f_t = kvf[:, 2 * D:2 * D + H].T
            b_col = kv_b_f.reshape(H, 1)
            dcum = _dcum_fwd(f_t, b_col, "kv_dcum")
            vt = kvf[:, D:2 * D].astype(BF16).reshape(S, H, hd).transpose(1, 2, 0)
            vat = jnp.where(lax.broadcasted_iota(jnp.int32, (1, LANES, 1), 1) == hd, jnp.asarray(1, BF16),
                            jnp.pad(vt, ((0, 0), (0, LANES - hd), (0, 0))))
            ka = _augment(kn, dcum, False)
            kv = dict(x=xc, hk=hk, k_raw=k_raw, ka=ka, kat=jnp.swapaxes(ka, 1, 2), vb=vb, vat=vat,
                      f_t=f_t, b_col=b_col, dcum=dcum)

    dx, loss_part = _loss_bwd(xc, tgt, "loss")
    loss = lax.psum(loss_part[0, 0], ("x", "y", "c"))

    gl = {n: [None] * weights[n].shape[0] for n in
          ["pre_mix_g", "post_mix_g", "pre_ffn_g", "post_ffn_g", "ffn_w_gu", "ffn_w_down", "a_w_in", "a_b_in", "a_ln_g",
           "a_ln_b", "a_w_s", "a_b_s", "a_w_out", "b_w_qg", "b_q_norm_g", "b_w_o"]}
    dmod = [None] * L
    dkn = dvb = ddc = None
    gkv = {}
    for l in reversed(range(L)):
        st = saved[l]
        sh_m, sc_m, g_m, sh_f, sc_f, g_f = mod[l]
        if l == NA - 1:
            dk_raw, gkv["k_norm_g"] = _head_norm_bwd(jnp.swapaxes(dkn, 1, 2).reshape(H * S, hd), kv["k_raw"], row(k_norm_g),
                                                     1.0, "kv_knorm_bwd")
            df_t, db_f = _dcum_bwd(ddc.reshape(H, S), kv["f_t"], kv["b_col"], "kv_dcum_bwd")
            dkvf = jnp.concatenate([unheads(dk_raw.reshape(H, S, hd)), dvb.transpose(2, 0, 1).reshape(S, D), df_t.T,
                                    jnp.zeros((S, kv_pad - (2 * D + H)), F32)], axis=1).astype(BF16)
            gkv["kv_w"] = _mm(kv["hk"], dkvf, "tn", BF16, "kv_proj_dw")[:, :2 * D + H]
            dhk = _mm(dkvf, w_kv, "nt", F32, "kv_proj_dx")
            dx, gkv["kv_norm_g"], dsh, dsc = _norm_mod_bwd(dx, dhk, kv["x"], row(kv_norm_g), kv_sh, kv_sc, "kv_pre_bwd")
            gkv["kv_b_f"] = db_f.reshape(H)
            dmod_kv = jnp.concatenate([dsh, dsc], axis=1)
        dy2, gl["post_ffn_g"][l], dg_f = _post_bwd(dx, st["y2"], row(post_ffn_g[l]), g_f, f"post_ffn_bwd_{l}")
        gl["ffn_w_down"][l] = _mm(st["act"], dy2, "tn", BF16, f"ffn_down_dw_{l}")
        dgu = _ffn_down_dx(dy2, w_dn, l, st["gu"], gu_hw, f"ffn_down_dx_{l}")
        gl["ffn_w_gu"][l] = _mm(st["h2"], dgu, "tn", BF16, f"ffn_gu_dw_{l}")
        dh2 = _mm(dgu, (w_gu, l), "nt", F32, f"ffn_gu_dx_{l}")
        dx, gl["pre_ffn_g"][l], dsh_f, dsc_f = _norm_mod_bwd(dx, dh2, st["x1"], row(pre_ffn_g[l]), sh_f, sc_f, f"pre_ffn_bwd_{l}")
        dy, gl["post_mix_g"][l], dg_m = _post_bwd(dx, st["y"], row(post_mix_g[l]), g_m, f"post_mix_bwd_{l}")
        if l < NA:
            gl["a_w_out"][l] = _mm(st["yg"], dy, "tn", BF16, f"gmlp_out_dw_{l}")
            dyg = _mm(dy, (w_out, l), "nt", BF16, f"gmlp_out_dx_{l}")
            dzp, db_in, dlg, dlb, dws, dbs_t = _gmlp_bwd(dyg, st["zp"], row(b_in_f[l]), row(ln_g_f[l]), row(ln_b_f[l]),
                                                           ws_m[l], ws_mt[l], bs_t[l], f"gmlp_gate_bwd_{l}")
            gl["a_b_in"][l], gl["a_ln_g"][l], gl["a_ln_b"][l] = db_in[0], dlg[0], dlb[0]
            gl["a_w_s"][l], gl["a_b_s"][l] = dws * causal, dbs_t.T
            gl["a_w_in"][l] = _mm(st["h1"], dzp, "tn", BF16, f"gmlp_in_dw_{l}")
            dh1 = _mm(dzp, (w_in, l), "nt", F32, f"gmlp_in_dx_{l}")
        else:
            jb = l - NA
            gl["b_w_o"][jb] = _mm(st["og"], dy, "tn", BF16, f"fox_out_dw_{jb}")
            dog = _mm(dy, (w_o, jb), "nt", F32, f"fox_out_dx_{jb}")
            do_t, dgl = _out_gate_bwd(dog, st["o_t"], st["qg"], f"fox_gate_bwd_{jb}")
            dot = do_t.reshape(S, H, hd).transpose(1, 2, 0)
            dqa_tr, dka_tr, dv_j = _flash_bwd(kv["ka"], kv["kat"], st["qat"], kv["vb"], dot, st["o_tr"], st["lse_r"],
                                              f"fox_attn_bwd_{jb}")
            dqn = dqa_tr[:, :, :hd, :].transpose(0, 1, 3, 2).reshape(H, S, hd)
            dk_j = dka_tr[:, :hd, :]
            dd_j = dqa_tr[:, :, hd, :].reshape(H, S) - dka_tr[:, hd + 3, :]
            dkn = dk_j if dkn is None else dkn + dk_j
            dvb = dv_j if dvb is None else dvb + dv_j
            ddc = dd_j if ddc is None else ddc + dd_j
            dq_raw, dgq = _head_norm_bwd(dqn.reshape(H * S, hd), st["q_raw"], row(b_q_norm_g[jb]), hd ** -0.5, f"fox_qnorm_bwd_{jb}")
            gl["b_q_norm_g"][jb] = dgq[0]
            dqg = jnp.concatenate([unheads(dq_raw.reshape(H, S, hd)).astype(BF16), dgl], axis=1)
            gl["b_w_qg"][jb] = _mm(st["h1"], dqg, "tn", BF16, f"fox_qg_dw_{jb}")
            dh1 = _mm(dqg, (w_qg, jb), "nt", F32, f"fox_qg_dx_{jb}")
        dx, gl["pre_mix_g"][l], dsh_m, dsc_m = _norm_mod_bwd(dx, dh1, st["x0"], row(pre_mix_g[l]), sh_m, sc_m, f"pre_mix_bwd_{l}")
        dmod[l] = jnp.concatenate([dsh_m, dsc_m, dg_m, dsh_f, dsc_f, dg_f], axis=1)
    grad_x = dx[None]

    stack = lambda n: jnp.stack([t.reshape(weights[n].shape[1:]) for t in gl[n]])
    small = {"dmod": jnp.concatenate(dmod, axis=1), "dmod_kv": dmod_kv}
    for n in ["pre_mix_g", "post_mix_g", "pre_ffn_g", "post_ffn_g", "a_w_s", "a_b_s", "b_q_norm_g"]:
        small[n] = stack(n)
    for n in ["a_b_in", "a_ln_g", "a_ln_b"]:
        small[n] = jnp.stack(gl[n])
    for n in ["kv_norm_g", "kv_b_f", "k_norm_g"]:
        small[n] = gkv[n]
    sizes = {n: t.size for n, t in small.items()}
    flat = jnp.concatenate([t.reshape(-1).astype(F32) for t in small.values()])
    rows_small = -(-flat.size // (LANES * BF16_ROWS)) * BF16_ROWS
    flat = jnp.pad(flat, (0, rows_small * LANES - flat.size)).reshape(rows_small, LANES)
    flat_all = _own_slot(_gather8([flat], "gather_small")[0], flat, me)
    flat_sum = _sum_slots(flat_all, "sum_small").reshape(-1)
    offs, o_ = {}, 0
    for n, sz in sizes.items():
        offs[n] = o_
        o_ += sz
    take = lambda n, shape: flat_sum[offs[n]:offs[n] + sizes[n]].reshape(shape)
    dmod_rows = flat_all.reshape(N_DEV, -1)[:, offs["dmod"]:offs["dmod"] + sizes["dmod"] + sizes["dmod_kv"]]
    dmod_rows = jnp.pad(dmod_rows, ((0, BF16_ROWS - N_DEV), (0, 0)))

    grads = {}
    grads["ada_b"] = take("dmod", (L, 6 * D))
    grads["kv_ada_b"] = take("dmod_kv", (2 * D,))
    for n in ["pre_mix_g", "post_mix_g", "pre_ffn_g", "post_ffn_g", "a_w_s", "a_b_s", "b_q_norm_g", "kv_norm_g", "kv_b_f", "k_norm_g"]:
        grads[n] = take(n, weights[n].shape)
    for n in ["a_b_in", "a_ln_g", "a_ln_b"]:
        full = take(n, small[n].shape)
        w = weights[n].shape[1]
        grads[n] = lax.dynamic_slice_in_dim(full, chip * w, w, axis=1)
    ada_g = []
    for l in range(L):
        cols = lax.dynamic_slice_in_dim(dmod_rows[:, l * 6 * D:(l + 1) * 6 * D], chip * ada_cols, ada_cols, axis=1)
        ada_g.append(_mm(c_act, cols, "tn", F32, f"mod_proj_dw_{l}"))
    grads["ada_w"] = jnp.stack(ada_g)
    cols = lax.dynamic_slice_in_dim(dmod_rows[:, L * 6 * D:], chip * kvada_cols, kvada_cols, axis=1)
    grads["kv_ada_w"] = _mm(c_act, cols, "tn", F32, "mod_proj_kv_dw")

    specs = {"ffn_w_gu": (2, 0, (0, 2, 1, 3)),
             "ffn_w_down": (1, 0, CHIP_ORDER), "a_w_in": (2, 0, CHIP_ORDER), "a_w_out": (1, 0, CHIP_ORDER),
             "kv_w": (0, 1, CHIP_ORDER), "b_w_qg": (2, 0, CHIP_ORDER), "b_w_o": (1, 0, CHIP_ORDER)}
    full_g = {n: jnp.stack(gl[n]) for n in big if n != "kv_w"}
    full_g["kv_w"] = gkv["kv_w"].reshape(D, N_CHIPS, kv_cols).transpose(1, 0, 2)
    from_core = _sibling_scatter([full_g[n] for n in big], [specs[n] for n in big], "scatter_g_core")
    chip_sums = [_sum_pair(_own_pieces(full_g[n], specs[n], ci), r, f"sum_g_core_{n}") for n, r in zip(big, from_core)]
    recv = _chip_scatter(chip_sums, "scatter_g_chip")
    recv = [_own_slot(r, lax.dynamic_index_in_dim(p, chip, 0, keepdims=False), chip) for r, p in zip(recv, chip_sums)]
    halves = [_sum_slots(r, f"sum_g_{n}") for n, r in zip(big, recv)]
    pairs = _sibling_pair(halves, "pair_g")
    for n, p, hlf in zip(big, pairs, halves):
        grads[n] = _own_slot(p, hlf, ci).reshape(weights[n].shape)

    outs_d, outs_m, outs_v = {}, {}, {}
    for n in names:
        w2 = weights[n] if weights[n].ndim > 1 else weights[n].reshape(1, -1)
        shp = w2.shape
        d_, m_, v_ = _adamw(w2, grads[n].reshape(shp), m_in[n].reshape(shp), v_in[n].reshape(shp), f"adamw_{n}")
        outs_d[n], outs_m[n], outs_v[n] = (t.reshape(weights[n].shape) for t in (d_, m_, v_))
    return (loss, grad_x, *[grads[n] for n in names], *[outs_d[n] for n in names],
            *[outs_m[n] for n in names], *[outs_v[n] for n in names])
```

```python
import functools

import jax
import jax.numpy as jnp
from jax import lax
from jax.experimental import pallas as pl
from jax.experimental.pallas import tpu as pltpu

F32 = jnp.float32
BF16 = jnp.bfloat16
MESH = pl.DeviceIdType.MESH
NORM_EPS = 1e-6
MASKED = -1e30
LANES = 128
BF16_ROWS = 16
ROW_BLOCK_BYTES = 12 << 20
ADAM_LR, ADAM_B1, ADAM_B2, ADAM_EPS, ADAM_WD, ADAM_STEP = 0.001, 0.9, 0.999, 1e-08, 0.01, 10
N_CHIPS, N_CORES, N_DEV = 4, 2, 8
ATTN_HEADS_PER_STEP = 4
ATTN_STAGED_HEADS = 2
ANY = pl.BlockSpec(memory_space=pl.ANY)


def _tile(n, cap, quantum):
    best = None
    d = quantum
    while d <= min(n, cap):
        if n % d == 0:
            best = d
        d += quantum
    return n if best is None else best


def _call(body, *, name, out_shape, grid=(), in_specs=None, out_specs=None, scratch=(), sem=None, aliases=None):
    params = {} if sem is None else {"dimension_semantics": sem}
    return pl.pallas_call(
        body, name=name, grid=grid, in_specs=in_specs, out_specs=out_specs, out_shape=out_shape,
        scratch_shapes=list(scratch), input_output_aliases=aliases or {},
        compiler_params=pltpu.CompilerParams(**params))


def _call_prefetch(body, *, name, out_shape, grid, n_prefetch, in_specs, out_specs, scratch, sem):
    spec = pltpu.PrefetchScalarGridSpec(num_scalar_prefetch=n_prefetch, grid=grid, in_specs=in_specs,
                                        out_specs=out_specs, scratch_shapes=list(scratch))
    return pl.pallas_call(
        body, name=name, grid_spec=spec, out_shape=out_shape,
        compiler_params=pltpu.CompilerParams(dimension_semantics=sem))


def _place():
    x, y, c = lax.axis_index("x"), lax.axis_index("y"), lax.axis_index("c")
    return x, y, c


def _mm(a, b, mode, out_dtype, name):
    b_arr, b_idx = b if isinstance(b, tuple) else (b, None)
    bs = b_arr.shape[-2:]
    if mode == "nn":
        (M, K), (K2, N) = a.shape, bs
        dims = (((1,), (0,)), ((), ()))
    elif mode == "nt":
        (M, K), (N, K2) = a.shape, bs
        dims = (((1,), (1,)), ((), ()))
    else:
        (K, M), (K2, N) = a.shape, bs
        dims = (((0,), (0,)), ((), ()))
    assert K == K2, (name, a.shape, b_arr.shape)
    if mode == "tn":
        tm = _tile(M, 1408, LANES)
        tk = _tile(K, 2048, BF16_ROWS)
        tn = _tile(N, 512, LANES)
    else:
        tm = _tile(M, 1024, BF16_ROWS)
        tk = K if K <= 2816 else _tile(K, 2816, LANES)
        tn = _tile(N, 1408 if tk <= 1024 else 512, LANES)
    if tn < 256:
        tn = N
        tm = _tile(M, 512, LANES if mode == "tn" else BF16_ROWS)
    nk = K // tk
    grid = (M // tm, N // tn, nk)

    if mode == "tn":
        a_spec = pl.BlockSpec((tk, tm), lambda i, j, k: (k, i))
    else:
        a_spec = pl.BlockSpec((tm, tk), lambda i, j, k: (i, k))
    if mode == "nt":
        b_blk, b_map = (tn, tk), (lambda i, j, k: (j, k))
    else:
        b_blk, b_map = (tk, tn), (lambda i, j, k: (k, j))
    if b_idx is None:
        b_spec = pl.BlockSpec(b_blk, b_map)
    else:
        b_spec = pl.BlockSpec((None,) + b_blk, lambda i, j, k: (b_idx,) + b_map(i, j, k))

    def body(a_ref, b_ref, o_ref, *acc):
        r = lax.dot_general(a_ref[...].astype(BF16), b_ref[...].astype(BF16), dims, preferred_element_type=F32)
        if nk == 1:
            o_ref[...] = r.astype(o_ref.dtype)
        else:
            k = pl.program_id(2)

            @pl.when(k == 0)
            def _():
                acc[0][...] = r

            @pl.when(k > 0)
            def _():
                acc[0][...] += r

            @pl.when(k == nk - 1)
            def _():
                o_ref[...] = acc[0][...].astype(o_ref.dtype)

    return _call(
        body, name=name, grid=grid, in_specs=[a_spec, b_spec],
        out_specs=pl.BlockSpec((tm, tn), lambda i, j, k: (i, j)),
        out_shape=jax.ShapeDtypeStruct((M, N), out_dtype),
        scratch=[pltpu.VMEM((tm, tn), F32)] if nk > 1 else [],
        sem=("parallel", "parallel", "arbitrary"))(a, b_arr)


def _rowwise(fn, rows, pars, outs, pouts, name):
    R = rows[0].shape[0]
    row_bytes = 4 * (sum(max(r.shape[1], LANES) for r in rows) + sum(max(c, LANES) for c, _ in outs))
    tb = _tile(R, max(BF16_ROWS, ROW_BLOCK_BYTES // row_bytes), BF16_ROWS)
    nr, npar, no = len(rows), len(pars), len(outs)

    def body(*refs):
        r_in, p_in = refs[:nr], refs[nr:nr + npar]
        r_out, p_out = refs[nr + npar:nr + npar + no], refs[nr + npar + no:]
        ro, po = fn([r[...] for r in r_in], [p[...] for p in p_in])
        for ref, val in zip(r_out, ro):
            if isinstance(val, (tuple, list)):
                off = 0
                for piece in val:
                    w = piece.shape[1]
                    ref[:, off:off + w] = piece.astype(ref.dtype)
                    off += w
            else:
                ref[...] = val.astype(ref.dtype)
        if p_out:
            first = pl.program_id(0) == 0

            @pl.when(first)
            def _():
                for ref, val in zip(p_out, po):
                    ref[...] = val

            @pl.when(jnp.logical_not(first))
            def _():
                for ref, val in zip(p_out, po):
                    ref[...] += val

    res = _call(
        body, name=name, grid=(R // tb,),
        in_specs=[pl.BlockSpec((tb, r.shape[1]), lambda i: (i, 0)) for r in rows]
        + [pl.BlockSpec(p.shape, lambda i: (0, 0)) for p in pars],
        out_specs=[pl.BlockSpec((tb, c), lambda i: (i, 0)) for c, _ in outs]
        + [pl.BlockSpec(s, lambda i: (0, 0)) for s in pouts],
        out_shape=[jax.ShapeDtypeStruct((R, c), dt) for c, dt in outs]
        + [jax.ShapeDtypeStruct(s, F32) for s in pouts],
        sem=("arbitrary",) if pouts else ("parallel",))(*rows, *pars)
    return list(res)


def _rms(x, g):
    return x * lax.rsqrt(jnp.mean(x * x, axis=-1, keepdims=True) + NORM_EPS) * g


def _norm_mod(x, g, sh, sc):
    return _rms(x, g) * (1.0 + sc) + sh


def _gated_post(y, g, gate):
    return gate * _rms(y, g)


def _norm_mod_fwd(x, g, sh, sc, name):
    return _rowwise(lambda r, p: ([_norm_mod(r[0], *p)], []), [x], [g, sh, sc], [(x.shape[1], BF16)], [], name)[0]


def _norm_mod_bwd(dxo, dh, x, g, sh, sc, name):
    def fn(r, p):
        _, vjp = jax.vjp(_norm_mod, r[2], *p)
        dx, dg, dsh, dsc = vjp(r[1].astype(F32))
        return [r[0] + dx], [dg, dsh, dsc]
    c = x.shape[1]
    return _rowwise(fn, [dxo, dh, x], [g, sh, sc], [(c, F32)], [(1, c)] * 3, name)


def _post_fwd(x, y, g, gate, name):
    return _rowwise(lambda r, p: ([r[0] + _gated_post(r[1].astype(F32), *p)], []), [x, y], [g, gate],
                    [(x.shape[1], F32)], [], name)[0]


def _post_bwd(dxo, y, g, gate, name):
    def fn(r, p):
        _, vjp = jax.vjp(_gated_post, r[1].astype(F32), *p)
        dy, dg, dgate = vjp(r[0])
        return [dy], [dg, dgate]
    c = y.shape[1]
    return _rowwise(fn, [dxo, y], [g, gate], [(c, BF16)], [(1, c)] * 2, name)


def _post_pre_fwd(x, y, g_post, gate, g_pre, sh, sc, name):
    def fn(r, p):
        x1 = r[0] + _gated_post(r[1].astype(F32), p[0], p[1])
        return [x1, _norm_mod(x1, p[2], p[3], p[4])], []
    c = x.shape[1]
    return _rowwise(fn, [x, y], [g_post, gate, g_pre, sh, sc], [(c, F32), (c, BF16)], [], name)


def _pre_post_bwd(dxo, dh, x, g_pre, sh, sc, y, g_post, gate, name):
    def fn(r, p):
        _, vjp_pre = jax.vjp(_norm_mod, r[2], p[0], p[1], p[2])
        dxn, dg_pre, dsh, dsc = vjp_pre(r[1].astype(F32))
        dx = r[0] + dxn
        _, vjp_post = jax.vjp(_gated_post, r[3].astype(F32), p[3], p[4])
        dy, dg_post, dgate = vjp_post(dx)
        return [dx, dy], [dg_pre, dsh, dsc, dg_post, dgate]
    c = x.shape[1]
    return _rowwise(fn, [dxo, dh, x, y], [g_pre, sh, sc, g_post, gate], [(c, F32), (c, BF16)], [(1, c)] * 5, name)


def _swiglu(g, u):
    return jax.nn.silu(g) * u


def _ffn_up(h, w, l, hw, name):
    S, D = h.shape
    nb = w.shape[2] // (2 * hw)
    tm = _tile(S, 512, BF16_ROWS)

    def body(h_ref, w_ref, gu_ref, act_ref):
        gu = jnp.dot(h_ref[...], w_ref[...], preferred_element_type=F32).astype(BF16)
        gu_ref[...] = gu
        act_ref[...] = _swiglu(gu[:, :hw].astype(F32), gu[:, hw:].astype(F32)).astype(BF16)

    return _call(body, name=name, grid=(S // tm, nb),
                 in_specs=[pl.BlockSpec((tm, D), lambda i, j: (i, 0)), pl.BlockSpec((None, D, 2 * hw), lambda i, j: (l, 0, j))],
                 out_specs=[pl.BlockSpec((tm, 2 * hw), lambda i, j: (i, j)), pl.BlockSpec((tm, hw), lambda i, j: (i, j))],
                 out_shape=[jax.ShapeDtypeStruct((S, 2 * hw * nb), BF16), jax.ShapeDtypeStruct((S, hw * nb), BF16)],
                 sem=("parallel", "parallel"))(h, w)


def _ffn_down_dx(dy, w_dn, l, gu, hw, name):
    S, D = dy.shape
    nb = gu.shape[1] // (2 * hw)
    tm = _tile(S, 512, BF16_ROWS)

    def body(dy_ref, w_ref, gu_ref, dgu_ref):
        dact = lax.dot_general(dy_ref[...], w_ref[...], (((1,), (1,)), ((), ())), preferred_element_type=F32)
        _, vjp = jax.vjp(_swiglu, gu_ref[:, :hw].astype(F32), gu_ref[:, hw:].astype(F32))
        dg, du = vjp(dact)
        dgu_ref[:, :hw] = dg.astype(BF16)
        dgu_ref[:, hw:] = du.astype(BF16)

    return _call(body, name=name, grid=(S // tm, nb),
                 in_specs=[pl.BlockSpec((tm, D), lambda i, j: (i, 0)), pl.BlockSpec((None, hw, D), lambda i, j: (l, j, 0)),
                           pl.BlockSpec((tm, 2 * hw), lambda i, j: (i, j))],
                 out_specs=pl.BlockSpec((tm, 2 * hw), lambda i, j: (i, j)),
                 out_shape=jax.ShapeDtypeStruct(gu.shape, BF16), sem=("parallel", "parallel"))(dy, w_dn, gu)


def _silu_rows(c, name):
    return _rowwise(lambda r, p: ([jax.nn.silu(r[0])], []), [c], [], [(c.shape[1], F32)], [], name)[0]


def _head_norm(x, g, scale):
    return _rms(x, g) * scale


def _head_norm_fwd(x, g, scale, name):
    return _rowwise(lambda r, p: ([_head_norm(r[0].astype(F32), p[0], scale)], []), [x], [g],
                    [(x.shape[1], BF16)], [], name)[0]


def _head_norm_bwd(dy, x, g, scale, name):
    def fn(r, p):
        _, vjp = jax.vjp(lambda t, gg: _head_norm(t, gg, scale), r[1].astype(F32), p[0])
        dx, dg = vjp(r[0])
        return [dx], [dg]
    c = x.shape[1]
    return _rowwise(fn, [dy, x], [g], [(c, F32)], [(1, c)], name)


def _out_gate_fwd(o, qg, name):
    d = o.shape[1]
    return _rowwise(lambda r, p: ([r[0] * jax.nn.sigmoid(r[1][:, d:].astype(F32))], []), [o, qg], [],
                    [(d, BF16)], [], name)[0]


def _out_gate_bwd(dog, o, qg, name):
    d = o.shape[1]

    def fn(r, p):
        _, vjp = jax.vjp(lambda oo, gl: oo * jax.nn.sigmoid(gl), r[1], r[2][:, d:].astype(F32))
        do, dgl = vjp(r[0])
        return [do, dgl], []
    return _rowwise(fn, [dog, o, qg], [], [(d, BF16), (d, BF16)], [], name)


def _loss_bwd(y, tgt, name):
    n = y.shape[1]

    def fn(r, p):
        e = r[0] - r[1]
        part = jnp.sum(jnp.sum(e * e, axis=1, keepdims=True), axis=0, keepdims=True) * (0.5 / n)
        return [e * (1.0 / n)], [part]
    return _rowwise(fn, [y, tgt], [], [(n, F32)], [(1, 1)], name)


def _adamw(w, g, m, v, name):
    shape = w.shape
    c = shape[-1]
    flat = [t.reshape(-1, c) for t in (w, g, m, v)]

    def fn(r, p):
        w_, g_, m_, v_ = r
        m2 = ADAM_B1 * m_ + (1.0 - ADAM_B1) * g_
        v2 = ADAM_B2 * v_ + (1.0 - ADAM_B2) * (g_ * g_)
        m_hat = m2 / (1.0 - ADAM_B1 ** ADAM_STEP)
        v_hat = v2 / (1.0 - ADAM_B2 ** ADAM_STEP)
        delta = -ADAM_LR * (m_hat / (jnp.sqrt(v_hat) + ADAM_EPS) + ADAM_WD * w_)
        return [delta, m2, v2], []
    res = _rowwise(fn, flat, [], [(c, F32)] * 3, [], name)
    return [t.reshape(shape) for t in res]


def _sum_pair(a, b, name):
    c = a.shape[-1]
    out = _rowwise(lambda r, p: ([r[0].astype(F32) + r[1].astype(F32)], []), [a.reshape(-1, c), b.reshape(-1, c)], [],
                   [(c, BF16)], [], name)[0]
    return out.reshape(a.shape)


def _sum_slots(recv, name, out_dtype=F32):
    n = recv.shape[0]
    shape = recv.shape[1:]
    c = shape[-1]
    r3 = recv.reshape(n, -1, c)
    rows = r3.shape[1]
    tb = _tile(rows, max(BF16_ROWS, ROW_BLOCK_BYTES // (4 * c * (n + 1))), BF16_ROWS)

    def body(r_ref, o_ref):
        acc = r_ref[0].astype(F32)
        for s in range(1, n):
            acc = acc + r_ref[s].astype(F32)
        o_ref[...] = acc.astype(o_ref.dtype)

    out = _call(body, name=name, grid=(rows // tb,),
                in_specs=[pl.BlockSpec((n, tb, c), lambda i: (0, i, 0))],
                out_specs=pl.BlockSpec((tb, c), lambda i: (i, 0)),
                out_shape=jax.ShapeDtypeStruct((rows, c), out_dtype), sem=("parallel",))(r3)
    return out.reshape(shape)


def _gmlp_pre(zu, zv, b_u, b_v, ln_g, ln_b):
    u = jax.nn.gelu(zu + b_u, approximate=True)
    v = jax.nn.gelu(zv + b_v, approximate=True)
    xc = v - jnp.mean(v, axis=-1, keepdims=True)
    vn = xc * lax.rsqrt(jnp.mean(xc * xc, axis=-1, keepdims=True) + NORM_EPS) * ln_g + ln_b
    return u, vn


def _gmlp_fwd(zp, b_in, ln_g, ln_b, ws, bs_t, name):
    S, gw2 = zp.shape
    gw = gw2 // 2
    G, ch, _ = ws.shape
    gd = gw // G
    tb = 2 * ch

    def body(zp_ref, bin_ref, lg_ref, lb_ref, ws_ref, bs_ref, o_ref):
        u, vn = _gmlp_pre(zp_ref[:, :gw].astype(F32), zp_ref[:, gw:].astype(F32), bin_ref[:, :gw], bin_ref[:, gw:],
                          lg_ref[...], lb_ref[...])
        vnb = vn.astype(BF16)
        for c in range(tb // ch):
            for g in range(G):
                rs, cs = slice(c * ch, (c + 1) * ch), slice(g * gd, (g + 1) * gd)
                vv = jnp.dot(ws_ref[g], vnb[rs, cs], preferred_element_type=F32) + bs_ref[:, g:g + 1]
                o_ref[rs, cs] = (u[rs, cs] * vv).astype(o_ref.dtype)

    full = lambda a: pl.BlockSpec(a.shape, lambda i: (0,) * a.ndim)
    return _call(body, name=name, grid=(S // tb,),
                 in_specs=[pl.BlockSpec((tb, gw2), lambda i: (i, 0)), full(b_in), full(ln_g), full(ln_b), full(ws), full(bs_t)],
                 out_specs=pl.BlockSpec((tb, gw), lambda i: (i, 0)),
                 out_shape=jax.ShapeDtypeStruct((S, gw), BF16), sem=("parallel",))(zp, b_in, ln_g, ln_b, ws, bs_t)


def _gmlp_bwd(dyg, zp, b_in, ln_g, ln_b, ws, ws_t, bs_t, name):
    S, gw2 = zp.shape
    gw = gw2 // 2
    G, ch, _ = ws.shape
    gd = gw // G
    tb = 2 * ch

    def body(dy_ref, zp_ref, bin_ref, lg_ref, lb_ref, ws_ref, wst_ref, bs_ref,
             dzp_ref, dbin_ref, dlg_ref, dlb_ref, dws_ref, dbs_ref, du_sc, dvn_sc):
        (u, vn), vjp = jax.vjp(_gmlp_pre, zp_ref[:, :gw].astype(F32), zp_ref[:, gw:].astype(F32), bin_ref[:, :gw],
                               bin_ref[:, gw:], lg_ref[...], lb_ref[...])
        vnb = vn.astype(BF16)
        first = pl.program_id(0) == 0

        @pl.when(first)
        def _():
            dws_ref[...] = jnp.zeros_like(dws_ref)

        lane = lax.broadcasted_iota(jnp.int32, (ch, G), 1)
        dbs = jnp.zeros((ch, G), F32)
        for g in range(G):
            cs = slice(g * gd, (g + 1) * gd)
            dws_g = jnp.zeros((ch, ch), F32)
            col = jnp.zeros((ch, 1), F32)
            for c in range(tb // ch):
                rs = slice(c * ch, (c + 1) * ch)
                vnp = vnb[rs, cs]
                vv = jnp.dot(ws_ref[g], vnp, preferred_element_type=F32) + bs_ref[:, g:g + 1]
                dy = dy_ref[rs, cs].astype(F32)
                du_sc[rs, cs] = dy * vv
                dvv = dy * u[rs, cs]
                dvvb = dvv.astype(BF16)
                dvn_sc[rs, cs] = jnp.dot(wst_ref[g], dvvb, preferred_element_type=F32)
                dws_g = dws_g + lax.dot_general(dvvb, vnp, (((1,), (1,)), ((), ())), preferred_element_type=F32)
                col = col + jnp.sum(dvv, axis=1, keepdims=True)
            dws_ref[g] += dws_g
            dbs = jnp.where(lane == g, col, dbs)
        dzu, dzv, dbu, dbv, dlg, dlb = vjp((du_sc[...], dvn_sc[...]))
        dzp_ref[:, :gw] = dzu.astype(dzp_ref.dtype)
        dzp_ref[:, gw:] = dzv.astype(dzp_ref.dtype)

        @pl.when(first)
        def _():
            dbin_ref[:, :gw] = dbu
            dbin_ref[:, gw:] = dbv
            dlg_ref[...] = dlg
            dlb_ref[...] = dlb
            dbs_ref[...] = dbs

        @pl.when(jnp.logical_not(first))
        def _():
            dbin_ref[:, :gw] += dbu
            dbin_ref[:, gw:] += dbv
            dlg_ref[...] += dlg
            dlb_ref[...] += dlb
            dbs_ref[...] += dbs

    full = lambda a: pl.BlockSpec(a.shape, lambda i: (0,) * a.ndim)
    fshape = lambda s: pl.BlockSpec(s, lambda i: (0,) * len(s))
    return _call(
        body, name=name, grid=(S // tb,),
        in_specs=[pl.BlockSpec((tb, gw), lambda i: (i, 0)), pl.BlockSpec((tb, gw2), lambda i: (i, 0)),
                  full(b_in), full(ln_g), full(ln_b), full(ws), full(ws_t), full(bs_t)],
        out_specs=[pl.BlockSpec((tb, gw2), lambda i: (i, 0)), fshape((1, gw2)), fshape((1, gw)), fshape((1, gw)),
                   fshape((G, ch, ch)), fshape((ch, G))],
        out_shape=[jax.ShapeDtypeStruct((S, gw2), BF16), jax.ShapeDtypeStruct((1, gw2), F32),
                   jax.ShapeDtypeStruct((1, gw), F32), jax.ShapeDtypeStruct((1, gw), F32),
                   jax.ShapeDtypeStruct((G, ch, ch), F32), jax.ShapeDtypeStruct((ch, G), F32)],
        scratch=[pltpu.VMEM((tb, gw), F32), pltpu.VMEM((tb, gw), F32)],
        sem=("arbitrary",))(dyg, zp, b_in, ln_g, ln_b, ws, ws_t, bs_t)


def _dot_01(x, ones_bf16):
    hi = x.astype(BF16)
    r1 = x - hi.astype(F32)
    mid = r1.astype(BF16)
    lo = (r1 - mid.astype(F32)).astype(BF16)
    dot = lambda t: jnp.dot(t, ones_bf16, preferred_element_type=F32)
    return dot(hi) + dot(mid) + dot(lo)


def _log_sigmoid(x):
    return jnp.minimum(x, 0.0) - jnp.log1p(jnp.exp(-jnp.abs(x)))


def _dcum_fwd(f_t, b_col, name):
    H, S = f_t.shape
    tb = _tile(S, 512, LANES)

    def body(f_ref, b_ref, o_ref, carry):
        @pl.when(pl.program_id(0) == 0)
        def _():
            carry[...] = jnp.zeros_like(carry)

        ls = _log_sigmoid(f_ref[...] + b_ref[...])
        r = lax.broadcasted_iota(jnp.int32, (tb, tb), 0)
        c = lax.broadcasted_iota(jnp.int32, (tb, tb), 1)
        upper = (r <= c).astype(BF16)
        o_ref[...] = _dot_01(ls, upper) + carry[...]
        carry[...] += jnp.sum(ls, axis=1, keepdims=True)

    return _call(body, name=name, grid=(S // tb,),
                 in_specs=[pl.BlockSpec((H, tb), lambda i: (0, i)), pl.BlockSpec((H, 1), lambda i: (0, 0))],
                 out_specs=pl.BlockSpec((H, tb), lambda i: (0, i)),
                 out_shape=jax.ShapeDtypeStruct((H, S), F32),
                 scratch=[pltpu.VMEM((H, 1), F32)], sem=("arbitrary",))(f_t, b_col)


def _dcum_bwd(dd_t, f_t, b_col, name):
    H, S = f_t.shape
    tb = _tile(S, 512, LANES)
    nb = S // tb

    def body(dd_ref, f_ref, b_ref, df_ref, db_ref, carry):
        first = pl.program_id(0) == 0

        @pl.when(first)
        def _():
            carry[...] = jnp.zeros_like(carry)

        dd = dd_ref[...]
        r = lax.broadcasted_iota(jnp.int32, (tb, tb), 0)
        c = lax.broadcasted_iota(jnp.int32, (tb, tb), 1)
        lower = (r >= c).astype(BF16)
        rev = _dot_01(dd, lower) + carry[...]
        carry[...] += jnp.sum(dd, axis=1, keepdims=True)
        df = rev * jax.nn.sigmoid(-(f_ref[...] + b_ref[...]))
        df_ref[...] = df
        part = jnp.sum(df, axis=1, keepdims=True)

        @pl.when(first)
        def _():
            db_ref[...] = part

        @pl.when(jnp.logical_not(first))
        def _():
            db_ref[...] += part

    return _call(body, name=name, grid=(nb,),
                 in_specs=[pl.BlockSpec((H, tb), lambda i: (0, nb - 1 - i)), pl.BlockSpec((H, tb), lambda i: (0, nb - 1 - i)),
                           pl.BlockSpec((H, 1), lambda i: (0, 0))],
                 out_specs=[pl.BlockSpec((H, tb), lambda i: (0, nb - 1 - i)), pl.BlockSpec((H, 1), lambda i: (0, 0))],
                 out_shape=[jax.ShapeDtypeStruct((H, S), F32), jax.ShapeDtypeStruct((H, 1), F32)],
                 scratch=[pltpu.VMEM((H, 1), F32)], sem=("arbitrary",))(dd_t, f_t, b_col)


def _attn_tile(S):
    return _tile(S, 512, LANES)


def _causal(t, transposed):
    r = lax.broadcasted_iota(jnp.int32, (t, t), 0)
    c = lax.broadcasted_iota(jnp.int32, (t, t), 1)
    return (r <= c) if transposed else (c <= r)


def _tri_pairs(n, key_major):
    if key_major:
        pairs = [(i, j) for j in range(n) for i in range(j, n)]
    else:
        pairs = [(i, j) for i in range(n) for j in range(i + 1)]
    return jnp.asarray([p[0] for p in pairs], jnp.int32), jnp.asarray([p[1] for p in pairs], jnp.int32)


def _split3(x):
    hi = lax.reduce_precision(x, 8, 7)
    r = x - hi
    mid = lax.reduce_precision(r, 8, 7)
    lo = lax.reduce_precision(r - mid, 8, 7)
    return hi.astype(BF16), mid.astype(BF16), lo.astype(BF16)


def _augment(xn, dcum, query):
    H, S, hd = xn.shape
    parts = list(_split3(dcum))
    vals = parts + [1.0] * 3 if query else [1.0] * 3 + [-p for p in parts]
    lane = lax.broadcasted_iota(jnp.int32, (1, 1, LANES), 2)
    out = jnp.pad(xn, ((0, 0), (0, 0), (0, LANES - hd)))
    for k, val in enumerate(vals):
        val = jnp.asarray(val, BF16)
        out = jnp.where(lane == hd + k, val[..., None] if val.ndim else val, out)
    return out


def _scores_t(k_ref, qt_ref, h, t, diag):
    st = jnp.dot(k_ref[h], qt_ref[h], preferred_element_type=F32)
    return jnp.where(_causal(t, True), st, MASKED) if diag else st


def _flash_fwd(ka, qat, vat, hd, name):
    H, S, da = ka.shape
    t = _attn_tile(S)
    hb = ATTN_HEADS_PER_STEP
    it, jt = _tri_pairs(S // t, False)

    def body(it_ref, jt_ref, k_ref, qt_ref, vt_ref, o_ref, lse_ref, m_sc, acc_sc):
        i, j = it_ref[pl.program_id(1)], jt_ref[pl.program_id(1)]

        @pl.when(j == 0)
        def _():
            m_sc[...] = jnp.full_like(m_sc, MASKED)
            acc_sc[...] = jnp.zeros_like(acc_sc)

        def step(diag):
            sts = [_scores_t(k_ref, qt_ref, h, t, diag) for h in range(hb)]
            pts, alphas = [], []
            for h in range(hb):
                m_prev = m_sc[h]
                m_new = jnp.maximum(m_prev, jnp.max(sts[h], axis=0, keepdims=True))
                pts.append(jnp.exp(sts[h] - m_new).astype(BF16))
                alphas.append(jnp.exp(m_prev - m_new))
                m_sc[h] = m_new
            for h in range(hb):
                acc_sc[h] = alphas[h] * acc_sc[h] + jnp.dot(vt_ref[h], pts[h], preferred_element_type=F32)

        @pl.when(j < i)
        def _():
            step(False)

        @pl.when(j == i)
        def _():
            step(True)
            for h in range(hb):
                l = acc_sc[h, hd:hd + 1, :]
                o_ref[h] = acc_sc[h, :hd, :] / l
                lse_ref[h] = m_sc[h] + jnp.log(l)

    qcol = lambda h, p, it_, jt_: (h, 0, it_[p])
    kcol = lambda h, p, it_, jt_: (h, 0, jt_[p])
    krow = lambda h, p, it_, jt_: (h, jt_[p], 0)
    return _call_prefetch(
        body, name=name, grid=(H // hb, it.shape[0]), n_prefetch=2,
        in_specs=[pl.BlockSpec((hb, t, da), krow), pl.BlockSpec((hb, da, t), qcol), pl.BlockSpec((hb, da, t), kcol)],
        out_specs=[pl.BlockSpec((hb, hd, t), qcol), pl.BlockSpec((hb, 1, t), qcol)],
        out_shape=[jax.ShapeDtypeStruct((H, hd, S), F32), jax.ShapeDtypeStruct((H, 1, S), F32)],
        scratch=[pltpu.VMEM((hb, 1, t), F32), pltpu.VMEM((hb, da, t), F32)],
        sem=("parallel", "arbitrary"))(it, jt, ka, qat, vat)


def _flash_bwd(ka, kat, qat, v, dot, o_tr, lse_r, name):
    H, S, hd = v.shape
    da = ka.shape[2]
    t = _attn_tile(S)
    n = S // t
    hb = ATTN_HEADS_PER_STEP
    it, jt = _tri_pairs(n, True)
    over_queries = (((1,), (1,)), ((), ()))

    def body(it_ref, jt_ref, k_ref, kt_ref, qt_ref, v_ref, dot_ref, o_ref, lse_ref, dq_ref, dk_ref, dv_ref, dk_sc, dv_sc):
        i, j = it_ref[pl.program_id(1)], jt_ref[pl.program_id(1)]

        @pl.when(pl.program_id(1) == 0)
        def _():
            dq_ref[...] = jnp.zeros_like(dq_ref)

        def step(diag):
            for h0 in range(0, hb, ATTN_STAGED_HEADS):
                hs = range(h0, min(h0 + ATTN_STAGED_HEADS, hb))
                sts = [_scores_t(k_ref, qt_ref, h, t, diag) for h in hs]
                dpts = [jnp.dot(v_ref[h], dot_ref[h], preferred_element_type=F32) for h in hs]
                tiles = []
                for h, st, dpt in zip(hs, sts, dpts):
                    dl = jnp.sum(dot_ref[h].astype(F32) * o_ref[h], axis=0, keepdims=True)
                    pt = jnp.exp(st - lse_ref[h])
                    tiles.append((pt.astype(BF16), (pt * (dpt - dl)).astype(BF16)))
                for h, (ptb, dsb) in zip(hs, tiles):
                    dv_sc[h] += lax.dot_general(dot_ref[h], ptb, over_queries, preferred_element_type=F32)
                    dk_sc[h] += lax.dot_general(qt_ref[h], dsb, over_queries, preferred_element_type=F32)
                    dq_ref[h, i] += jnp.dot(kt_ref[h], dsb, preferred_element_type=F32)

        @pl.when(i == j)
        def _():
            dk_sc[...] = jnp.zeros_like(dk_sc)
            dv_sc[...] = jnp.zeros_like(dv_sc)
            step(True)

        @pl.when(i > j)
        def _():
            step(False)

        @pl.when(i == n - 1)
        def _():
            dk_ref[...] = dk_sc[...]
            dv_ref[...] = dv_sc[...]

    krow = lambda h, p, it_, jt_: (h, jt_[p], 0)
    kcol = lambda h, p, it_, jt_: (h, 0, jt_[p])
    qcol = lambda h, p, it_, jt_: (h, 0, it_[p])
    return _call_prefetch(
        body, name=name, grid=(H // hb, it.shape[0]), n_prefetch=2,
        in_specs=[pl.BlockSpec((hb, t, da), krow), pl.BlockSpec((hb, da, t), kcol), pl.BlockSpec((hb, da, t), qcol),
                  pl.BlockSpec((hb, t, hd), krow), pl.BlockSpec((hb, hd, t), qcol), pl.BlockSpec((hb, hd, t), qcol),
                  pl.BlockSpec((hb, 1, t), qcol)],
        out_specs=[pl.BlockSpec((hb, n, da, t), lambda h, p, it_, jt_: (h, 0, 0, 0)), pl.BlockSpec((hb, da, t), kcol),
                   pl.BlockSpec((hb, hd, t), kcol)],
        out_shape=[jax.ShapeDtypeStruct((H, n, da, t), F32), jax.ShapeDtypeStruct((H, da, S), F32),
                   jax.ShapeDtypeStruct((H, hd, S), F32)],
        scratch=[pltpu.VMEM((hb, da, t), F32), pltpu.VMEM((hb, hd, t), F32)],
        sem=("parallel", "arbitrary"))(it, jt, ka, kat, qat, v, dot, o_tr, lse_r)


def _offsets(n_bits):
    return [tuple((k >> b) & 1 for b in reversed(range(n_bits))) for k in range(1, 1 << n_bits)]


def _own_slot(out, own, idx):
    return lax.dynamic_update_index_in_dim(out, own.astype(out.dtype), idx, 0)


def _gather8(arrs, name):
    n = len(arrs)
    offs = _offsets(3)

    def body(*refs):
        ins, outs = refs[:n], refs[n:2 * n]
        ssem, rsem = refs[2 * n:]
        x, y, c = _place()
        me = 4 * x + 2 * y + c
        copies = []
        for a in range(n):
            for k, (dx, dy, dcc) in enumerate(offs):
                cp = pltpu.make_async_remote_copy(
                    src_ref=ins[a], dst_ref=outs[a].at[me], send_sem=ssem.at[a, k], recv_sem=rsem.at[a, k],
                    device_id=((x + dx) % 2, (y + dy) % 2, (c + dcc) % 2), device_id_type=MESH)
                cp.start()
                copies.append(cp)
        for cp in copies:
            cp.wait()

    return _call(body, name=name, in_specs=[ANY] * n, out_specs=[ANY] * n,
                 out_shape=[jax.ShapeDtypeStruct((N_DEV,) + a.shape, a.dtype) for a in arrs],
                 scratch=[pltpu.SemaphoreType.DMA((n, 7)), pltpu.SemaphoreType.DMA((n, 7))])(*arrs)


def _chip_gather(arrs, halved, name):
    n = len(arrs)
    offs = _offsets(2)

    def body(*refs):
        ins, outs = refs[:n], refs[n:2 * n]
        ssem, rsem = refs[2 * n:]
        x, y, c = _place()
        chip = 2 * x + y
        copies = []
        for a in range(n):
            if halved:
                hn = arrs[a].shape[0] // 2
                src = ins[a].at[pl.ds(c * hn, hn)]
                dst = outs[a].at[chip, pl.ds(c * hn, hn)]
            else:
                src, dst = ins[a], outs[a].at[chip]
            for k, (dx, dy) in enumerate(offs):
                cp = pltpu.make_async_remote_copy(
                    src_ref=src, dst_ref=dst, send_sem=ssem.at[a, k], recv_sem=rsem.at[a, k],
                    device_id=((x + dx) % 2, (y + dy) % 2, c), device_id_type=MESH)
                cp.start()
                copies.append(cp)
        for cp in copies:
            cp.wait()

    return _call(body, name=name, in_specs=[ANY] * n, out_specs=[ANY] * n,
                 out_shape=[jax.ShapeDtypeStruct((N_CHIPS,) + a.shape, a.dtype) for a in arrs],
                 scratch=[pltpu.SemaphoreType.DMA((n, 3)), pltpu.SemaphoreType.DMA((n, 3))])(*arrs)


def _sibling_fill(bufs, name):
    n = len(bufs)
    offs = _offsets(2)

    def body(*refs):
        ins, outs = refs[:n], refs[n:2 * n]
        ssem, rsem = refs[2 * n:]
        x, y, c = _place()
        copies = []
        for a in range(n):
            hn = bufs[a].shape[1] // 2
            for k, (dx, dy) in enumerate(offs):
                chip = 2 * ((x + dx) % 2) + (y + dy) % 2
                cp = pltpu.make_async_remote_copy(
                    src_ref=ins[a].at[chip, pl.ds(c * hn, hn)], dst_ref=outs[a].at[chip, pl.ds(c * hn, hn)],
                    send_sem=ssem.at[a, k], recv_sem=rsem.at[a, k],
                    device_id=(x, y, 1 - c), device_id_type=MESH)
                cp.start()
                copies.append(cp)
        for cp in copies:
            cp.wait()

    return _call(body, name=name, in_specs=[ANY] * n, out_specs=[ANY] * n,
                 out_shape=[jax.ShapeDtypeStruct(b.shape, b.dtype) for b in bufs],
                 scratch=[pltpu.SemaphoreType.DMA((n, 3)), pltpu.SemaphoreType.DMA((n, 3))],
                 aliases={a: a for a in range(n)})(*bufs)


def _sibling_pair(arrs, name):
    n = len(arrs)

    def body(*refs):
        ins, outs = refs[:n], refs[n:2 * n]
        ssem, rsem = refs[2 * n:]
        x, y, c = _place()
        copies = []
        for a in range(n):
            cp = pltpu.make_async_remote_copy(
                src_ref=ins[a], dst_ref=outs[a].at[c], send_sem=ssem.at[a], recv_sem=rsem.at[a],
                device_id=(x, y, 1 - c), device_id_type=MESH)
            cp.start()
            copies.append(cp)
        for cp in copies:
            cp.wait()

    return _call(body, name=name, in_specs=[ANY] * n, out_specs=[ANY] * n,
                 out_shape=[jax.ShapeDtypeStruct((N_CORES,) + a.shape, a.dtype) for a in arrs],
                 scratch=[pltpu.SemaphoreType.DMA((n,)), pltpu.SemaphoreType.DMA((n,))])(*arrs)


CHIP_ORDER = (0, 1, 2, 3)


def _piece(shape, spec, j, h):
    shard_ax, half_ax, order = spec
    w = shape[shard_ax] // N_CHIPS
    idx = [slice(None)] * len(shape)
    idx[shard_ax] = pl.ds(order[j] * w, w)
    assert half_ax != shard_ax
    hn = shape[half_ax] // 2
    idx[half_ax] = pl.ds(h * hn, hn)
    return tuple(idx)


def _piece_shape(shape, spec):
    shard_ax, half_ax, _ = spec
    s = list(shape)
    s[shard_ax] //= N_CHIPS
    s[half_ax] //= 2
    return tuple(s)


def _own_pieces(g, spec, c):
    shard_ax, half_ax, order = spec
    hn = g.shape[half_ax] // 2
    half = lax.dynamic_slice_in_dim(g, c * hn, hn, axis=half_ax)
    w = half.shape[shard_ax] // N_CHIPS
    return jnp.stack([lax.slice_in_dim(half, p * w, (p + 1) * w, axis=shard_ax) for p in order])


def _sibling_scatter(arrs, specs, name):
    n = len(arrs)

    def body(*refs):
        ins, outs = refs[:n], refs[n:2 * n]
        ssem, rsem = refs[2 * n:]
        x, y, c = _place()
        for mine in range(N_CORES):
            @pl.when(c == mine)
            def _():
                copies = []
                for a in range(n):
                    for j in range(N_CHIPS):
                        cp = pltpu.make_async_remote_copy(
                            src_ref=ins[a].at[_piece(arrs[a].shape, specs[a], j, 1 - mine)], dst_ref=outs[a].at[j],
                            send_sem=ssem.at[a, j], recv_sem=rsem.at[a, j],
                            device_id=(x, y, 1 - mine), device_id_type=MESH)
                        cp.start()
                        copies.append(cp)
                for cp in copies:
                    cp.wait()

    return _call(body, name=name, in_specs=[ANY] * n, out_specs=[ANY] * n,
                 out_shape=[jax.ShapeDtypeStruct((N_CHIPS,) + _piece_shape(a.shape, s), a.dtype)
                            for a, s in zip(arrs, specs)],
                 scratch=[pltpu.SemaphoreType.DMA((n, N_CHIPS))] * 2)(*arrs)


def _chip_scatter(arrs, name):
    n = len(arrs)
    offs = _offsets(2)

    def body(*refs):
        ins, outs = refs[:n], refs[n:2 * n]
        ssem, rsem = refs[2 * n:]
        x, y, c = _place()
        chip = 2 * x + y
        copies = []
        for a in range(n):
            for k, (dx, dy) in enumerate(offs):
                tx, ty = (x + dx) % 2, (y + dy) % 2
                cp = pltpu.make_async_remote_copy(
                    src_ref=ins[a].at[2 * tx + ty], dst_ref=outs[a].at[chip], send_sem=ssem.at[a, k], recv_sem=rsem.at[a, k],
                    device_id=(tx, ty, c), device_id_type=MESH)
                cp.start()
                copies.append(cp)
        for cp in copies:
            cp.wait()

    return _call(body, name=name, in_specs=[ANY] * n, out_specs=[ANY] * n,
                 out_shape=[jax.ShapeDtypeStruct(a.shape, a.dtype) for a in arrs],
                 scratch=[pltpu.SemaphoreType.DMA((n, 3)), pltpu.SemaphoreType.DMA((n, 3))])(*arrs)


def kernel(x, c, ada_w, ada_b, pre_mix_g, post_mix_g, pre_ffn_g, post_ffn_g, ffn_w_gu, ffn_w_down, a_w_in, a_b_in, a_ln_g, a_ln_b, a_w_s, a_b_s, a_w_out, kv_ada_w, kv_ada_b, kv_norm_g, kv_w, kv_b_f, k_norm_g, b_w_qg, b_q_norm_g, b_w_o, loss_target, m_ada_w, m_ada_b, m_pre_mix_g, m_post_mix_g, m_pre_ffn_g, m_post_ffn_g, m_ffn_w_gu, m_ffn_w_down, m_a_w_in, m_a_b_in, m_a_ln_g, m_a_ln_b, m_a_w_s, m_a_b_s, m_a_w_out, m_kv_ada_w, m_kv_ada_b, m_kv_norm_g, m_kv_w, m_kv_b_f, m_k_norm_g, m_b_w_qg, m_b_q_norm_g, m_b_w_o, v_ada_w, v_ada_b, v_pre_mix_g, v_post_mix_g, v_pre_ffn_g, v_post_ffn_g, v_ffn_w_gu, v_ffn_w_down, v_a_w_in, v_a_b_in, v_a_ln_g, v_a_ln_b, v_a_w_s, v_a_b_s, v_a_w_out, v_kv_ada_w, v_kv_ada_b, v_kv_norm_g, v_kv_w, v_kv_b_f, v_k_norm_g, v_b_w_qg, v_b_q_norm_g, v_b_w_o):
    weights = dict(ada_w=ada_w, ada_b=ada_b, pre_mix_g=pre_mix_g, post_mix_g=post_mix_g, pre_ffn_g=pre_ffn_g,
                   post_ffn_g=post_ffn_g, ffn_w_gu=ffn_w_gu, ffn_w_down=ffn_w_down, a_w_in=a_w_in, a_b_in=a_b_in,
                   a_ln_g=a_ln_g, a_ln_b=a_ln_b, a_w_s=a_w_s, a_b_s=a_b_s, a_w_out=a_w_out, kv_ada_w=kv_ada_w,
                   kv_ada_b=kv_ada_b, kv_norm_g=kv_norm_g, kv_w=kv_w, kv_b_f=kv_b_f, k_norm_g=k_norm_g, b_w_qg=b_w_qg,
                   b_q_norm_g=b_q_norm_g, b_w_o=b_w_o)
    m_in = dict(ada_w=m_ada_w, ada_b=m_ada_b, pre_mix_g=m_pre_mix_g, post_mix_g=m_post_mix_g, pre_ffn_g=m_pre_ffn_g,
                post_ffn_g=m_post_ffn_g, ffn_w_gu=m_ffn_w_gu, ffn_w_down=m_ffn_w_down, a_w_in=m_a_w_in, a_b_in=m_a_b_in,
                a_ln_g=m_a_ln_g, a_ln_b=m_a_ln_b, a_w_s=m_a_w_s, a_b_s=m_a_b_s, a_w_out=m_a_w_out, kv_ada_w=m_kv_ada_w,
                kv_ada_b=m_kv_ada_b, kv_norm_g=m_kv_norm_g, kv_w=m_kv_w, kv_b_f=m_kv_b_f, k_norm_g=m_k_norm_g,
                b_w_qg=m_b_w_qg, b_q_norm_g=m_b_q_norm_g, b_w_o=m_b_w_o)
    v_in = dict(ada_w=v_ada_w, ada_b=v_ada_b, pre_mix_g=v_pre_mix_g, post_mix_g=v_post_mix_g, pre_ffn_g=v_pre_ffn_g,
                post_ffn_g=v_post_ffn_g, ffn_w_gu=v_ffn_w_gu, ffn_w_down=v_ffn_w_down, a_w_in=v_a_w_in, a_b_in=v_a_b_in,
                a_ln_g=v_a_ln_g, a_ln_b=v_a_ln_b, a_w_s=v_a_w_s, a_b_s=v_a_b_s, a_w_out=v_a_w_out, kv_ada_w=v_kv_ada_w,
                kv_ada_b=v_kv_ada_b, kv_norm_g=v_kv_norm_g, kv_w=v_kv_w, kv_b_f=v_kv_b_f, k_norm_g=v_k_norm_g,
                b_w_qg=v_b_w_qg, b_q_norm_g=v_b_q_norm_g, b_w_o=v_b_w_o)
    names = list(weights)

    S, D = x.shape[1], x.shape[2]
    L, NA, NB = ada_w.shape[0], a_w_in.shape[0], b_w_qg.shape[0]
    H = kv_b_f.shape[0]
    hd = D // H
    G, CH = a_w_s.shape[1], a_w_s.shape[2]
    GW = a_w_out.shape[1] * N_CHIPS
    F = ffn_w_down.shape[1] * N_CHIPS
    ada_cols = ada_w.shape[2]
    kvada_cols = kv_ada_w.shape[1]
    kv_cols = kv_w.shape[1]
    kv_pad = -(-(2 * D + H) // LANES) * LANES
    xi, yi, ci = _place()
    chip = 2 * xi + yi
    me = 2 * chip + ci
    x0 = x[0]
    tgt = loss_target[0]
    row = lambda t: t.reshape(1, -1)

    c_all = _own_slot(_gather8([c], "gather_c")[0], c, me).reshape(N_DEV, D)
    c_act = _silu_rows(jnp.pad(c_all, ((0, BF16_ROWS - N_DEV), (0, 0))), "silu_c")
    mod_sh = [_mm(c_act, (ada_w, l), "nn", F32, f"mod_proj_{l}") for l in range(L)]
    mod_sh.append(_mm(c_act, kv_ada_w, "nn", F32, "mod_proj_kv"))
    mod_sh = jnp.concatenate(mod_sh, axis=1)
    small_sh = [mod_sh, a_b_in, a_ln_g, a_ln_b]
    mod_all, b_in_all, ln_g_all, ln_b_all = [
        _own_slot(o, s, chip) for o, s in zip(_chip_gather(small_sh, False, "gather_mod"), small_sh)]
    mine = lax.dynamic_index_in_dim(mod_all, me, axis=1, keepdims=False)
    mod = [jnp.concatenate([mine[j, l * ada_cols:(l + 1) * ada_cols] for j in range(N_CHIPS)]) + ada_b[l] for l in range(L)]
    mod = [[row(t) for t in jnp.split(m_, 6)] for m_ in mod]
    mod_kv = jnp.concatenate([mine[j, L * ada_cols:] for j in range(N_CHIPS)]) + kv_ada_b
    kv_sh, kv_sc = [row(t) for t in jnp.split(mod_kv, 2)]
    cat_chips = lambda t, ax: jnp.concatenate([t[j] for j in range(N_CHIPS)], axis=ax)
    b_in_f = cat_chips(b_in_all, 1)
    ln_g_f, ln_b_f = cat_chips(ln_g_all, 1), cat_chips(ln_b_all, 1)

    big = ["ffn_w_gu", "ffn_w_down", "a_w_in", "a_w_out", "kv_w", "b_w_qg", "b_w_o"]
    own_w = [weights[n].astype(BF16) for n in big]
    gathered = _sibling_fill(_chip_gather(own_w, True, "gather_w"), "fill_w")
    gathered = {n: _own_slot(g, w, chip) for n, g, w in zip(big, gathered, own_w)}
    gu_hw = ffn_w_gu.shape[2]
    w_gu = jnp.concatenate([gathered["ffn_w_gu"][j] for j in (0, 2, 1, 3)], axis=2)
    w_dn = cat_chips(gathered["ffn_w_down"], 1)
    w_in = cat_chips(gathered["a_w_in"], 2)
    w_out = cat_chips(gathered["a_w_out"], 1)
    w_kv = jnp.pad(cat_chips(gathered["kv_w"], 1), ((0, 0), (0, kv_pad - (2 * D + H))))
    w_qg = cat_chips(gathered["b_w_qg"], 2)
    w_o = cat_chips(gathered["b_w_o"], 1)

    causal = jnp.tril(jnp.ones((CH, CH), F32))
    ws_m = [(a_w_s[i] * causal).astype(BF16) for i in range(NA)]
    ws_mt = [jnp.swapaxes(w, 1, 2) for w in ws_m]
    bs_t = [a_b_s[i].T for i in range(NA)]

    heads = lambda t: t.reshape(S, H, hd).transpose(1, 0, 2)
    unheads = lambda t: t.transpose(1, 0, 2).reshape(S, D)

    saved = []
    kv = None
    xc = x0
    h1 = _norm_mod_fwd(xc, row(pre_mix_g[0]), mod[0][0], mod[0][1], "pre_mix_0")
    for l in range(L):
        sh_m, sc_m, g_m, sh_f, sc_f, g_f = mod[l]
        st = {"x0": xc, "h1": h1}
        if l < NA:
            zp = _mm(h1, (w_in, l), "nn", BF16, f"gmlp_in_{l}")
            yg = _gmlp_fwd(zp, row(b_in_f[l]), row(ln_g_f[l]), row(ln_b_f[l]), ws_m[l], bs_t[l], f"gmlp_gate_{l}")
            y = _mm(yg, (w_out, l), "nn", F32, f"gmlp_out_{l}")
            st.update(zp=zp, yg=yg)
        else:
            jb = l - NA
            qg = _mm(h1, (w_qg, jb), "nn", BF16, f"fox_qg_{jb}")
            q_raw = heads(qg[:, :D]).reshape(H * S, hd)
            qn = _head_norm_fwd(q_raw, row(b_q_norm_g[jb]), hd ** -0.5, f"fox_qnorm_{jb}").reshape(H, S, hd)
            qa = _augment(qn, kv["dcum"], True)
            qat = jnp.swapaxes(qa, 1, 2)
            o_tr, lse_r = _flash_fwd(kv["ka"], qat, kv["vat"], hd, f"fox_attn_{jb}")
            o_t = o_tr.transpose(2, 0, 1).reshape(S, D)
            og = _out_gate_fwd(o_t, qg, f"fox_gate_{jb}")
            y = _mm(og, (w_o, jb), "nn", F32, f"fox_out_{jb}")
            st.update(qg=qg, q_raw=q_raw, qat=qat, o_tr=o_tr, lse_r=lse_r, o_t=o_t, og=og)
        st["y"] = y
        x1, h2 = _post_pre_fwd(xc, y, row(post_mix_g[l]), g_m, row(pre_ffn_g[l]), sh_f, sc_f, f"post_mix_{l}")
        st["x1"] = x1
        gu, act = _ffn_up(h2, w_gu, l, gu_hw, f"ffn_gu_{l}")
        y2 = _mm(act, (w_dn, l), "nn", F32, f"ffn_down_{l}")
        if l + 1 < L:
            xc, h1 = _post_pre_fwd(x1, y2, row(post_ffn_g[l]), g_f, row(pre_mix_g[l + 1]), mod[l + 1][0], mod[l + 1][1],
                                   f"post_ffn_{l}")
        else:
            xc = _post_fwd(x1, y2, row(post_ffn_g[l]), g_f, f"post_ffn_{l}")
        st.update(h2=h2, gu=gu, act=act, y2=y2)
        saved.append(st)
        if l == NA - 1:
            hk = _norm_mod_fwd(xc, row(kv_norm_g), kv_sh, kv_sc, "kv_pre")
            kvf = _mm(hk, w_kv, "nn", F32, "kv_proj")
            k_raw = heads(kvf[:, :D]).reshape(H * S, hd)
            kn = _head_norm_fwd(k_raw, row(k_norm_g), 1.0, "kv_knorm").reshape(H, S, hd)
            vb = heads(kvf[:, D:2 * D]).astype(BF16)
            f_t = kvf[:, 2 * D:2 * D + H].T
            b_col = kv_b_f.reshape(H, 1)
            dcum = _dcum_fwd(f_t, b_col, "kv_dcum")
            vt = kvf[:, D:2 * D].astype(BF16).reshape(S, H, hd).transpose(1, 2, 0)
            vat = jnp.where(lax.broadcasted_iota(jnp.int32, (1, LANES, 1), 1) == hd, jnp.asarray(1, BF16),
                            jnp.pad(vt, ((0, 0), (0, LANES - hd), (0, 0))))
            ka = _augment(kn, dcum, False)
            kv = dict(x=xc, hk=hk, k_raw=k_raw, ka=ka, kat=jnp.swapaxes(ka, 1, 2), vb=vb, vat=vat,
                      f_t=f_t, b_col=b_col, dcum=dcum)

    dx, loss_part = _loss_bwd(xc, tgt, "loss")

    gl = {n: [None] * weights[n].shape[0] for n in
          ["pre_mix_g", "post_mix_g", "pre_ffn_g", "post_ffn_g", "ffn_w_gu", "ffn_w_down", "a_w_in", "a_b_in", "a_ln_g",
           "a_ln_b", "a_w_s", "a_b_s", "a_w_out", "b_w_qg", "b_q_norm_g", "b_w_o"]}
    dmod = [None] * L
    dkn = dvb = ddc = None
    gkv = {}
    for l in reversed(range(L)):
        st = saved[l]
        sh_m, sc_m, g_m, sh_f, sc_f, g_f = mod[l]
        if l == NA - 1:
            dk_raw, gkv["k_norm_g"] = _head_norm_bwd(jnp.swapaxes(dkn, 1, 2).reshape(H * S, hd), kv["k_raw"], row(k_norm_g),
                                                     1.0, "kv_knorm_bwd")
            df_t, db_f = _dcum_bwd(ddc.reshape(H, S), kv["f_t"], kv["b_col"], "kv_dcum_bwd")
            dkvf = jnp.concatenate([unheads(dk_raw.reshape(H, S, hd)), dvb.transpose(2, 0, 1).reshape(S, D), df_t.T,
                                    jnp.zeros((S, kv_pad - (2 * D + H)), F32)], axis=1).astype(BF16)
            gkv["kv_w"] = _mm(kv["hk"], dkvf, "tn", BF16, "kv_proj_dw")[:, :2 * D + H]
            dhk = _mm(dkvf, w_kv, "nt", F32, "kv_proj_dx")
            dx, gkv["kv_norm_g"], dsh, dsc = _norm_mod_bwd(dx, dhk, kv["x"], row(kv_norm_g), kv_sh, kv_sc, "kv_pre_bwd")
            gkv["kv_b_f"] = db_f.reshape(H)
            dmod_kv = jnp.concatenate([dsh, dsc], axis=1)
        dy2, gl["post_ffn_g"][l], dg_f = _post_bwd(dx, st["y2"], row(post_ffn_g[l]), g_f, f"post_ffn_bwd_{l}")
        gl["ffn_w_down"][l] = _mm(st["act"], dy2, "tn", BF16, f"ffn_down_dw_{l}")
        dgu = _ffn_down_dx(dy2, w_dn, l, st["gu"], gu_hw, f"ffn_down_dx_{l}")
        gl["ffn_w_gu"][l] = _mm(st["h2"], dgu, "tn", BF16, f"ffn_gu_dw_{l}")
        dh2 = _mm(dgu, (w_gu, l), "nt", F32, f"ffn_gu_dx_{l}")
        dx, dy, gl["pre_ffn_g"][l], dsh_f, dsc_f, gl["post_mix_g"][l], dg_m = _pre_post_bwd(
            dx, dh2, st["x1"], row(pre_ffn_g[l]), sh_f, sc_f, st["y"], row(post_mix_g[l]), g_m, f"pre_ffn_bwd_{l}")
        if l < NA:
            gl["a_w_out"][l] = _mm(st["yg"], dy, "tn", BF16, f"gmlp_out_dw_{l}")
            dyg = _mm(dy, (w_out, l), "nt", BF16, f"gmlp_out_dx_{l}")
            dzp, db_in, dlg, dlb, dws, dbs_t = _gmlp_bwd(dyg, st["zp"], row(b_in_f[l]), row(ln_g_f[l]), row(ln_b_f[l]),
                                                           ws_m[l], ws_mt[l], bs_t[l], f"gmlp_gate_bwd_{l}")
            gl["a_b_in"][l], gl["a_ln_g"][l], gl["a_ln_b"][l] = db_in[0], dlg[0], dlb[0]
            gl["a_w_s"][l], gl["a_b_s"][l] = dws * causal, dbs_t.T
            gl["a_w_in"][l] = _mm(st["h1"], dzp, "tn", BF16, f"gmlp_in_dw_{l}")
            dh1 = _mm(dzp, (w_in, l), "nt", F32, f"gmlp_in_dx_{l}")
        else:
            jb = l - NA
            gl["b_w_o"][jb] = _mm(st["og"], dy, "tn", BF16, f"fox_out_dw_{jb}")
            dog = _mm(dy, (w_o, jb), "nt", F32, f"fox_out_dx_{jb}")
            do_t, dgl = _out_gate_bwd(dog, st["o_t"], st["qg"], f"fox_gate_bwd_{jb}")
            dot = do_t.reshape(S, H, hd).transpose(1, 2, 0)
            dqa_tr, dka_tr, dv_j = _flash_bwd(kv["ka"], kv["kat"], st["qat"], kv["vb"], dot, st["o_tr"], st["lse_r"],
                                              f"fox_attn_bwd_{jb}")
            dqn = dqa_tr[:, :, :hd, :].transpose(0, 1, 3, 2).reshape(H, S, hd)
            dk_j = dka_tr[:, :hd, :]
            dd_j = dqa_tr[:, :, hd, :].reshape(H, S) - dka_tr[:, hd + 3, :]
            dkn = dk_j if dkn is None else dkn + dk_j
            dvb = dv_j if dvb is None else dvb + dv_j
            ddc = dd_j if ddc is None else ddc + dd_j
            dq_raw, dgq = _head_norm_bwd(dqn.reshape(H * S, hd), st["q_raw"], row(b_q_norm_g[jb]), hd ** -0.5, f"fox_qnorm_bwd_{jb}")
            gl["b_q_norm_g"][jb] = dgq[0]
            dqg = jnp.concatenate([unheads(dq_raw.reshape(H, S, hd)).astype(BF16), dgl], axis=1)
            gl["b_w_qg"][jb] = _mm(st["h1"], dqg, "tn", BF16, f"fox_qg_dw_{jb}")
            dh1 = _mm(dqg, (w_qg, jb), "nt", F32, f"fox_qg_dx_{jb}")
        dx, gl["pre_mix_g"][l], dsh_m, dsc_m = _norm_mod_bwd(dx, dh1, st["x0"], row(pre_mix_g[l]), sh_m, sc_m, f"pre_mix_bwd_{l}")
        dmod[l] = jnp.concatenate([dsh_m, dsc_m, dg_m, dsh_f, dsc_f, dg_f], axis=1)
    grad_x = dx[None]

    stack = lambda n: jnp.stack([t.reshape(weights[n].shape[1:]) for t in gl[n]])
    small = {"dmod": jnp.concatenate(dmod, axis=1), "dmod_kv": dmod_kv}
    for n in ["pre_mix_g", "post_mix_g", "pre_ffn_g", "post_ffn_g", "a_w_s", "a_b_s", "b_q_norm_g"]:
        small[n] = stack(n)
    for n in ["a_b_in", "a_ln_g", "a_ln_b"]:
        small[n] = jnp.stack(gl[n])
    for n in ["kv_norm_g", "kv_b_f", "k_norm_g"]:
        small[n] = gkv[n]
    small["loss"] = loss_part
    sizes = {n: t.size for n, t in small.items()}
    flat = jnp.concatenate([t.reshape(-1).astype(F32) for t in small.values()])
    rows_small = -(-flat.size // (LANES * BF16_ROWS)) * BF16_ROWS
    flat = jnp.pad(flat, (0, rows_small * LANES - flat.size)).reshape(rows_small, LANES)
    flat_all = _own_slot(_gather8([flat], "gather_small")[0], flat, me)
    flat_sum = _sum_slots(flat_all, "sum_small").reshape(-1)
    offs, o_ = {}, 0
    for n, sz in sizes.items():
        offs[n] = o_
        o_ += sz
    take = lambda n, shape: flat_sum[offs[n]:offs[n] + sizes[n]].reshape(shape)
    dmod_rows = flat_all.reshape(N_DEV, -1)[:, offs["dmod"]:offs["dmod"] + sizes["dmod"] + sizes["dmod_kv"]]
    dmod_rows = jnp.pad(dmod_rows, ((0, BF16_ROWS - N_DEV), (0, 0)))

    grads = {}
    loss = take("loss", ())
    grads["ada_b"] = take("dmod", (L, 6 * D))
    grads["kv_ada_b"] = take("dmod_kv", (2 * D,))
    for n in ["pre_mix_g", "post_mix_g", "pre_ffn_g", "post_ffn_g", "a_w_s", "a_b_s", "b_q_norm_g", "kv_norm_g", "kv_b_f", "k_norm_g"]:
        grads[n] = take(n, weights[n].shape)
    for n in ["a_b_in", "a_ln_g", "a_ln_b"]:
        full = take(n, small[n].shape)
        w = weights[n].shape[1]
        grads[n] = lax.dynamic_slice_in_dim(full, chip * w, w, axis=1)
    ada_g = []
    for l in range(L):
        cols = lax.dynamic_slice_in_dim(dmod_rows[:, l * 6 * D:(l + 1) * 6 * D], chip * ada_cols, ada_cols, axis=1)
        ada_g.append(_mm(c_act, cols, "tn", F32, f"mod_proj_dw_{l}"))
    grads["ada_w"] = jnp.stack(ada_g)
    cols = lax.dynamic_slice_in_dim(dmod_rows[:, L * 6 * D:], chip * kvada_cols, kvada_cols, axis=1)
    grads["kv_ada_w"] = _mm(c_act, cols, "tn", F32, "mod_proj_kv_dw")

    specs = {"ffn_w_gu": (2, 0, (0, 2, 1, 3)),
             "ffn_w_down": (1, 0, CHIP_ORDER), "a_w_in": (2, 0, CHIP_ORDER), "a_w_out": (1, 0, CHIP_ORDER),
             "kv_w": (0, 1, CHIP_ORDER), "b_w_qg": (2, 0, CHIP_ORDER), "b_w_o": (1, 0, CHIP_ORDER)}
    full_g = {n: jnp.stack(gl[n]) for n in big if n != "kv_w"}
    full_g["kv_w"] = gkv["kv_w"].reshape(D, N_CHIPS, kv_cols).transpose(1, 0, 2)
    from_core = _sibling_scatter([full_g[n] for n in big], [specs[n] for n in big], "scatter_g_core")
    chip_sums = [_sum_pair(_own_pieces(full_g[n], specs[n], ci), r, f"sum_g_core_{n}") for n, r in zip(big, from_core)]
    recv = _chip_scatter(chip_sums, "scatter_g_chip")
    recv = [_own_slot(r, lax.dynamic_index_in_dim(p, chip, 0, keepdims=False), chip) for r, p in zip(recv, chip_sums)]
    halves = [_sum_slots(r, f"sum_g_{n}") for n, r in zip(big, recv)]
    pairs = _sibling_pair(halves, "pair_g")
    for n, p, hlf in zip(big, pairs, halves):
        grads[n] = _own_slot(p, hlf, ci).reshape(weights[n].shape)

    outs_d, outs_m, outs_v = {}, {}, {}
    for n in names:
        w2 = weights[n] if weights[n].ndim > 1 else weights[n].reshape(1, -1)
        shp = w2.shape
        d_, m_, v_ = _adamw(w2, grads[n].reshape(shp), m_in[n].reshape(shp), v_in[n].reshape(shp), f"adamw_{n}")
        outs_d[n], outs_m[n], outs_v[n] = (t.reshape(weights[n].shape) for t in (d_, m_, v_))
    return (loss, grad_x, *[grads[n] for n in names], *[outs_d[n] for n in names],
            *[outs_m[n] for n in names], *[outs_v[n] for n in names])
```

```python
import jax
import jax.numpy as jnp
from jax import lax
from jax.experimental import pallas as pl
from jax.experimental.pallas import tpu as pltpu

F32 = jnp.float32
BF16 = jnp.bfloat16
MESH = pl.DeviceIdType.MESH
NORM_EPS = 1e-6
MASKED = -1e30
LANES = 128
BF16_ROWS = 16
ROW_BLOCK_BYTES = 12 << 20
ADAM_LR, ADAM_B1, ADAM_B2, ADAM_EPS, ADAM_WD, ADAM_STEP = 0.001, 0.9, 0.999, 1e-08, 0.01, 10
N_CHIPS, N_CORES, N_DEV = 4, 2, 8
ATTN_HEADS_PER_STEP = 4
ATTN_STAGED_HEADS = 2
ANY = pl.BlockSpec(memory_space=pl.ANY)


def _tile(n, cap, quantum):
    best = None
    d = quantum
    while d <= min(n, cap):
        if n % d == 0:
            best = d
        d += quantum
    return n if best is None else best


def _call(body, *, name, out_shape, grid=(), in_specs=None, out_specs=None, scratch=(), sem=None, aliases=None):
    params = {} if sem is None else {"dimension_semantics": sem}
    return pl.pallas_call(
        body, name=name, grid=grid, in_specs=in_specs, out_specs=out_specs, out_shape=out_shape,
        scratch_shapes=list(scratch), input_output_aliases=aliases or {},
        compiler_params=pltpu.CompilerParams(**params))


def _call_prefetch(body, *, name, out_shape, grid, n_prefetch, in_specs, out_specs, scratch, sem):
    spec = pltpu.PrefetchScalarGridSpec(num_scalar_prefetch=n_prefetch, grid=grid, in_specs=in_specs,
                                        out_specs=out_specs, scratch_shapes=list(scratch))
    return pl.pallas_call(
        body, name=name, grid_spec=spec, out_shape=out_shape,
        compiler_params=pltpu.CompilerParams(dimension_semantics=sem))


def _place():
    x, y, c = lax.axis_index("x"), lax.axis_index("y"), lax.axis_index("c")
    return x, y, c


def _mm(a, b, mode, out_dtype, name):
    b_arr, b_idx = b if isinstance(b, tuple) else (b, None)
    bs = b_arr.shape[-2:]
    if mode == "nn":
        (M, K), (K2, N) = a.shape, bs
        dims = (((1,), (0,)), ((), ()))
    elif mode == "nt":
        (M, K), (N, K2) = a.shape, bs
        dims = (((1,), (1,)), ((), ()))
    else:
        (K, M), (K2, N) = a.shape, bs
        dims = (((0,), (0,)), ((), ()))
    assert K == K2, (name, a.shape, b_arr.shape)
    if mode == "tn":
        tm = _tile(M, 1408, LANES)
        tk = _tile(K, 2048, BF16_ROWS)
        tn = _tile(N, 512, LANES)
    else:
        tm = _tile(M, 1024, BF16_ROWS)
        tk = K if K <= 2816 else _tile(K, 2816, LANES)
        tn = _tile(N, 1408 if tk <= 1024 else 512, LANES)
    if tn < 256:
        tn = N
        tm = _tile(M, 512, LANES if mode == "tn" else BF16_ROWS)
    nk = K // tk
    grid = (M // tm, N // tn, nk)

    if mode == "tn":
        a_spec = pl.BlockSpec((tk, tm), lambda i, j, k: (k, i))
    else:
        a_spec = pl.BlockSpec((tm, tk), lambda i, j, k: (i, k))
    if mode == "nt":
        b_blk, b_map = (tn, tk), (lambda i, j, k: (j, k))
    else:
        b_blk, b_map = (tk, tn), (lambda i, j, k: (k, j))
    if b_idx is None:
        b_spec = pl.BlockSpec(b_blk, b_map)
    else:
        b_spec = pl.BlockSpec((None,) + b_blk, lambda i, j, k: (b_idx,) + b_map(i, j, k))

    def body(a_ref, b_ref, o_ref, *acc):
        r = lax.dot_general(a_ref[...].astype(BF16), b_ref[...].astype(BF16), dims, preferred_element_type=F32)
        if nk == 1:
            o_ref[...] = r.astype(o_ref.dtype)
        else:
            k = pl.program_id(2)

            @pl.when(k == 0)
            def _():
                acc[0][...] = r

            @pl.when(k > 0)
            def _():
                acc[0][...] += r

            @pl.when(k == nk - 1)
            def _():
                o_ref[...] = acc[0][...].astype(o_ref.dtype)

    return _call(
        body, name=name, grid=grid, in_specs=[a_spec, b_spec],
        out_specs=pl.BlockSpec((tm, tn), lambda i, j, k: (i, j)),
        out_shape=jax.ShapeDtypeStruct((M, N), out_dtype),
        scratch=[pltpu.VMEM((tm, tn), F32)] if nk > 1 else [],
        sem=("parallel", "parallel", "arbitrary"))(a, b_arr)


def _rowwise(fn, rows, pars, outs, pouts, name):
    R = rows[0].shape[0]
    row_bytes = 4 * (sum(max(r.shape[1], LANES) for r in rows) + sum(max(c, LANES) for c, _ in outs))
    tb = _tile(R, max(BF16_ROWS, ROW_BLOCK_BYTES // row_bytes), BF16_ROWS)
    nr, npar, no = len(rows), len(pars), len(outs)

    def body(*refs):
        r_in, p_in = refs[:nr], refs[nr:nr + npar]
        r_out, p_out = refs[nr + npar:nr + npar + no], refs[nr + npar + no:]
        ro, po = fn([r[...] for r in r_in], [p[...] for p in p_in])
        for ref, val in zip(r_out, ro):
            if isinstance(val, (tuple, list)):
                off = 0
                for piece in val:
                    w = piece.shape[1]
                    ref[:, off:off + w] = piece.astype(ref.dtype)
                    off += w
            else:
                ref[...] = val.astype(ref.dtype)
        if p_out:
            first = pl.program_id(0) == 0

            @pl.when(first)
            def _():
                for ref, val in zip(p_out, po):
                    ref[...] = val

            @pl.when(jnp.logical_not(first))
            def _():
                for ref, val in zip(p_out, po):
                    ref[...] += val

    res = _call(
        body, name=name, grid=(R // tb,),
        in_specs=[pl.BlockSpec((tb, r.shape[1]), lambda i: (i, 0)) for r in rows]
        + [pl.BlockSpec(p.shape, lambda i: (0, 0)) for p in pars],
        out_specs=[pl.BlockSpec((tb, c), lambda i: (i, 0)) for c, _ in outs]
        + [pl.BlockSpec(s, lambda i: (0, 0)) for s in pouts],
        out_shape=[jax.ShapeDtypeStruct((R, c), dt) for c, dt in outs]
        + [jax.ShapeDtypeStruct(s, F32) for s in pouts],
        sem=("arbitrary",) if pouts else ("parallel",))(*rows, *pars)
    return list(res)


def _rms(x, g):
    return x * lax.rsqrt(jnp.mean(x * x, axis=-1, keepdims=True) + NORM_EPS) * g


def _norm_mod(x, g, sh, sc):
    return _rms(x, g) * (1.0 + sc) + sh


def _gated_post(y, g, gate):
    return gate * _rms(y, g)


def _norm_mod_fwd(x, g, sh, sc, name):
    return _rowwise(lambda r, p: ([_norm_mod(r[0], *p)], []), [x], [g, sh, sc], [(x.shape[1], BF16)], [], name)[0]


def _norm_mod_bwd(dxo, dh, x, g, sh, sc, name):
    def fn(r, p):
        _, vjp = jax.vjp(_norm_mod, r[2], *p)
        dx, dg, dsh, dsc = vjp(r[1].astype(F32))
        return [r[0] + dx], [dg, dsh, dsc]
    c = x.shape[1]
    return _rowwise(fn, [dxo, dh, x], [g, sh, sc], [(c, F32)], [(1, c)] * 3, name)


def _post_fwd(x, y, g, gate, name):
    return _rowwise(lambda r, p: ([r[0] + _gated_post(r[1].astype(F32), *p)], []), [x, y], [g, gate],
                    [(x.shape[1], F32)], [], name)[0]


def _post_bwd(dxo, y, g, gate, name):
    def fn(r, p):
        _, vjp = jax.vjp(_gated_post, r[1].astype(F32), *p)
        dy, dg, dgate = vjp(r[0])
        return [dy], [dg, dgate]
    c = y.shape[1]
    return _rowwise(fn, [dxo, y], [g, gate], [(c, BF16)], [(1, c)] * 2, name)


def _post_pre_fwd(x, y, g_post, gate, g_pre, sh, sc, name):
    def fn(r, p):
        x1 = r[0] + _gated_post(r[1].astype(F32), p[0], p[1])
        return [x1, _norm_mod(x1, p[2], p[3], p[4])], []
    c = x.shape[1]
    return _rowwise(fn, [x, y], [g_post, gate, g_pre, sh, sc], [(c, F32), (c, BF16)], [], name)


def _pre_post_bwd(dxo, dh, x, g_pre, sh, sc, y, g_post, gate, name):
    def fn(r, p):
        _, vjp_pre = jax.vjp(_norm_mod, r[2], p[0], p[1], p[2])
        dxn, dg_pre, dsh, dsc = vjp_pre(r[1].astype(F32))
        dx = r[0] + dxn
        _, vjp_post = jax.vjp(_gated_post, r[3].astype(F32), p[3], p[4])
        dy, dg_post, dgate = vjp_post(dx)
        return [dx, dy], [dg_pre, dsh, dsc, dg_post, dgate]
    c = x.shape[1]
    return _rowwise(fn, [dxo, dh, x, y], [g_pre, sh, sc, g_post, gate], [(c, F32), (c, BF16)], [(1, c)] * 5, name)


def _swiglu(g, u):
    return jax.nn.silu(g) * u


def _ffn_up(h, w, l, hw, name):
    S, D = h.shape
    nb = w.shape[2] // (2 * hw)
    tm = _tile(S, 512, BF16_ROWS)

    def body(h_ref, w_ref, gu_ref, act_ref):
        gu = jnp.dot(h_ref[...], w_ref[...], preferred_element_type=F32).astype(BF16)
        gu_ref[...] = gu
        act_ref[...] = _swiglu(gu[:, :hw].astype(F32), gu[:, hw:].astype(F32)).astype(BF16)

    return _call(body, name=name, grid=(S // tm, nb),
                 in_specs=[pl.BlockSpec((tm, D), lambda i, j: (i, 0)), pl.BlockSpec((None, D, 2 * hw), lambda i, j: (l, 0, j))],
                 out_specs=[pl.BlockSpec((tm, 2 * hw), lambda i, j: (i, j)), pl.BlockSpec((tm, hw), lambda i, j: (i, j))],
                 out_shape=[jax.ShapeDtypeStruct((S, 2 * hw * nb), BF16), jax.ShapeDtypeStruct((S, hw * nb), BF16)],
                 sem=("parallel", "parallel"))(h, w)


def _ffn_down_dx(dy, w_dn, l, gu, hw, name):
    S, D = dy.shape
    nb = gu.shape[1] // (2 * hw)
    tm = _tile(S, 512, BF16_ROWS)

    def body(dy_ref, w_ref, gu_ref, dgu_ref):
        dact = lax.dot_general(dy_ref[...], w_ref[...], (((1,), (1,)), ((), ())), preferred_element_type=F32)
        _, vjp = jax.vjp(_swiglu, gu_ref[:, :hw].astype(F32), gu_ref[:, hw:].astype(F32))
        dg, du = vjp(dact)
        dgu_ref[:, :hw] = dg.astype(BF16)
        dgu_ref[:, hw:] = du.astype(BF16)

    return _call(body, name=name, grid=(S // tm, nb),
                 in_specs=[pl.BlockSpec((tm, D), lambda i, j: (i, 0)), pl.BlockSpec((None, hw, D), lambda i, j: (l, j, 0)),
                           pl.BlockSpec((tm, 2 * hw), lambda i, j: (i, j))],
                 out_specs=pl.BlockSpec((tm, 2 * hw), lambda i, j: (i, j)),
                 out_shape=jax.ShapeDtypeStruct(gu.shape, BF16), sem=("parallel", "parallel"))(dy, w_dn, gu)


def _silu_rows(c, name):
    return _rowwise(lambda r, p: ([jax.nn.silu(r[0])], []), [c], [], [(c.shape[1], F32)], [], name)[0]


def _head_norm(x, g, scale):
    return _rms(x, g) * scale


def _head_norm_fwd(x, g, scale, name):
    return _rowwise(lambda r, p: ([_head_norm(r[0].astype(F32), p[0], scale)], []), [x], [g],
                    [(x.shape[1], BF16)], [], name)[0]


def _head_norm_bwd(dy, x, g, scale, name):
    def fn(r, p):
        _, vjp = jax.vjp(lambda t, gg: _head_norm(t, gg, scale), r[1].astype(F32), p[0])
        dx, dg = vjp(r[0])
        return [dx], [dg]
    c = x.shape[1]
    return _rowwise(fn, [dy, x], [g], [(c, F32)], [(1, c)], name)


def _out_gate_fwd(o, qg, name):
    d = o.shape[1]
    return _rowwise(lambda r, p: ([r[0] * jax.nn.sigmoid(r[1][:, d:].astype(F32))], []), [o, qg], [],
                    [(d, BF16)], [], name)[0]


def _out_gate_bwd(dog, o, qg, name):
    d = o.shape[1]

    def fn(r, p):
        _, vjp = jax.vjp(lambda oo, gl: oo * jax.nn.sigmoid(gl), r[1], r[2][:, d:].astype(F32))
        do, dgl = vjp(r[0])
        return [do, dgl], []
    return _rowwise(fn, [dog, o, qg], [], [(d, BF16), (d, BF16)], [], name)


def _loss_bwd(y, tgt, name):
    n = y.shape[1]

    def fn(r, p):
        e = r[0] - r[1]
        part = jnp.sum(jnp.sum(e * e, axis=1, keepdims=True), axis=0, keepdims=True) * (0.5 / n)
        return [e * (1.0 / n)], [part]
    return _rowwise(fn, [y, tgt], [], [(n, F32)], [(1, 1)], name)


def _adamw(w, g, m, v, name):
    shape = w.shape
    c = shape[-1]
    flat = [t.reshape(-1, c) for t in (w, g, m, v)]

    def fn(r, p):
        w_, g_, m_, v_ = r
        m2 = ADAM_B1 * m_ + (1.0 - ADAM_B1) * g_
        v2 = ADAM_B2 * v_ + (1.0 - ADAM_B2) * (g_ * g_)
        m_hat = m2 / (1.0 - ADAM_B1 ** ADAM_STEP)
        v_hat = v2 / (1.0 - ADAM_B2 ** ADAM_STEP)
        delta = -ADAM_LR * (m_hat / (jnp.sqrt(v_hat) + ADAM_EPS) + ADAM_WD * w_)
        return [delta, m2, v2], []
    res = _rowwise(fn, flat, [], [(c, F32)] * 3, [], name)
    return [t.reshape(shape) for t in res]


def _sum_pair(a, b, name):
    c = a.shape[-1]
    out = _rowwise(lambda r, p: ([r[0].astype(F32) + r[1].astype(F32)], []), [a.reshape(-1, c), b.reshape(-1, c)], [],
                   [(c, BF16)], [], name)[0]
    return out.reshape(a.shape)


def _sum_slots(recv, name):
    n = recv.shape[0]
    shape = recv.shape[1:]
    c = shape[-1]
    r3 = recv.reshape(n, -1, c)
    rows = r3.shape[1]
    tb = _tile(rows, max(BF16_ROWS, ROW_BLOCK_BYTES // (4 * c * (n + 1))), BF16_ROWS)

    def body(r_ref, o_ref):
        acc = r_ref[0].astype(F32)
        for s in range(1, n):
            acc = acc + r_ref[s].astype(F32)
        o_ref[...] = acc

    out = _call(body, name=name, grid=(rows // tb,),
                in_specs=[pl.BlockSpec((n, tb, c), lambda i: (0, i, 0))],
                out_specs=pl.BlockSpec((tb, c), lambda i: (i, 0)),
                out_shape=jax.ShapeDtypeStruct((rows, c), F32), sem=("parallel",))(r3)
    return out.reshape(shape)


def _gmlp_pre(zu, zv, b_u, b_v, ln_g, ln_b):
    u = jax.nn.gelu(zu + b_u, approximate=True)
    v = jax.nn.gelu(zv + b_v, approximate=True)
    xc = v - jnp.mean(v, axis=-1, keepdims=True)
    vn = xc * lax.rsqrt(jnp.mean(xc * xc, axis=-1, keepdims=True) + NORM_EPS) * ln_g + ln_b
    return u, vn


def _gmlp_fwd(zp, b_in, ln_g, ln_b, ws, bs_t, name):
    S, gw2 = zp.shape
    gw = gw2 // 2
    G, ch, _ = ws.shape
    gd = gw // G
    tb = 2 * ch

    def body(zp_ref, bin_ref, lg_ref, lb_ref, ws_ref, bs_ref, o_ref):
        u, vn = _gmlp_pre(zp_ref[:, :gw].astype(F32), zp_ref[:, gw:].astype(F32), bin_ref[:, :gw], bin_ref[:, gw:],
                          lg_ref[...], lb_ref[...])
        vnb = vn.astype(BF16)
        for c in range(tb // ch):
            for g in range(G):
                rs, cs = slice(c * ch, (c + 1) * ch), slice(g * gd, (g + 1) * gd)
                vv = jnp.dot(ws_ref[g], vnb[rs, cs], preferred_element_type=F32) + bs_ref[:, g:g + 1]
                o_ref[rs, cs] = (u[rs, cs] * vv).astype(o_ref.dtype)

    full = lambda a: pl.BlockSpec(a.shape, lambda i: (0,) * a.ndim)
    return _call(body, name=name, grid=(S // tb,),
                 in_specs=[pl.BlockSpec((tb, gw2), lambda i: (i, 0)), full(b_in), full(ln_g), full(ln_b), full(ws), full(bs_t)],
                 out_specs=pl.BlockSpec((tb, gw), lambda i: (i, 0)),
                 out_shape=jax.ShapeDtypeStruct((S, gw), BF16), sem=("parallel",))(zp, b_in, ln_g, ln_b, ws, bs_t)


def _gmlp_bwd(dyg, zp, b_in, ln_g, ln_b, ws, ws_t, bs_t, name):
    S, gw2 = zp.shape
    gw = gw2 // 2
    G, ch, _ = ws.shape
    gd = gw // G
    tb = 2 * ch

    def body(dy_ref, zp_ref, bin_ref, lg_ref, lb_ref, ws_ref, wst_ref, bs_ref,
             dzp_ref, dbin_ref, dlg_ref, dlb_ref, dws_ref, dbs_ref, du_sc, dvn_sc):
        (u, vn), vjp = jax.vjp(_gmlp_pre, zp_ref[:, :gw].astype(F32), zp_ref[:, gw:].astype(F32), bin_ref[:, :gw],
                               bin_ref[:, gw:], lg_ref[...], lb_ref[...])
        vnb = vn.astype(BF16)
        first = pl.program_id(0) == 0

        @pl.when(first)
        def _():
            dws_ref[...] = jnp.zeros_like(dws_ref)

        lane = lax.broadcasted_iota(jnp.int32, (ch, G), 1)
        dbs = jnp.zeros((ch, G), F32)
        for g in range(G):
            cs = slice(g * gd, (g + 1) * gd)
            dws_g = jnp.zeros((ch, ch), F32)
            col = jnp.zeros((ch, 1), F32)
            for c in range(tb // ch):
                rs = slice(c * ch, (c + 1) * ch)
                vnp = vnb[rs, cs]
                vv = jnp.dot(ws_ref[g], vnp, preferred_element_type=F32) + bs_ref[:, g:g + 1]
                dy = dy_ref[rs, cs].astype(F32)
                du_sc[rs, cs] = dy * vv
                dvv = dy * u[rs, cs]
                dvvb = dvv.astype(BF16)
                dvn_sc[rs, cs] = jnp.dot(wst_ref[g], dvvb, preferred_element_type=F32)
                dws_g = dws_g + lax.dot_general(dvvb, vnp, (((1,), (1,)), ((), ())), preferred_element_type=F32)
                col = col + jnp.sum(dvv, axis=1, keepdims=True)
            dws_ref[g] += dws_g
            dbs = jnp.where(lane == g, col, dbs)
        dzu, dzv, dbu, dbv, dlg, dlb = vjp((du_sc[...], dvn_sc[...]))
        dzp_ref[:, :gw] = dzu.astype(dzp_ref.dtype)
        dzp_ref[:, gw:] = dzv.astype(dzp_ref.dtype)

        @pl.when(first)
        def _():
            dbin_ref[:, :gw] = dbu
            dbin_ref[:, gw:] = dbv
            dlg_ref[...] = dlg
            dlb_ref[...] = dlb
            dbs_ref[...] = dbs

        @pl.when(jnp.logical_not(first))
        def _():
            dbin_ref[:, :gw] += dbu
            dbin_ref[:, gw:] += dbv
            dlg_ref[...] += dlg
            dlb_ref[...] += dlb
            dbs_ref[...] += dbs

    full = lambda a: pl.BlockSpec(a.shape, lambda i: (0,) * a.ndim)
    fshape = lambda s: pl.BlockSpec(s, lambda i: (0,) * len(s))
    return _call(
        body, name=name, grid=(S // tb,),
        in_specs=[pl.BlockSpec((tb, gw), lambda i: (i, 0)), pl.BlockSpec((tb, gw2), lambda i: (i, 0)),
                  full(b_in), full(ln_g), full(ln_b), full(ws), full(ws_t), full(bs_t)],
        out_specs=[pl.BlockSpec((tb, gw2), lambda i: (i, 0)), fshape((1, gw2)), fshape((1, gw)), fshape((1, gw)),
                   fshape((G, ch, ch)), fshape((ch, G))],
        out_shape=[jax.ShapeDtypeStruct((S, gw2), BF16), jax.ShapeDtypeStruct((1, gw2), F32),
                   jax.ShapeDtypeStruct((1, gw), F32), jax.ShapeDtypeStruct((1, gw), F32),
                   jax.ShapeDtypeStruct((G, ch, ch), F32), jax.ShapeDtypeStruct((ch, G), F32)],
        scratch=[pltpu.VMEM((tb, gw), F32), pltpu.VMEM((tb, gw), F32)],
        sem=("arbitrary",))(dyg, zp, b_in, ln_g, ln_b, ws, ws_t, bs_t)


def _dot_01(x, ones_bf16):
    hi = x.astype(BF16)
    r1 = x - hi.astype(F32)
    mid = r1.astype(BF16)
    lo = (r1 - mid.astype(F32)).astype(BF16)
    dot = lambda t: jnp.dot(t, ones_bf16, preferred_element_type=F32)
    return dot(hi) + dot(mid) + dot(lo)


def _log_sigmoid(x):
    return jnp.minimum(x, 0.0) - jnp.log1p(jnp.exp(-jnp.abs(x)))


def _dcum_fwd(f_t, b_col, name):
    H, S = f_t.shape
    tb = _tile(S, 512, LANES)

    def body(f_ref, b_ref, o_ref, carry):
        @pl.when(pl.program_id(0) == 0)
        def _():
            carry[...] = jnp.zeros_like(carry)

        ls = _log_sigmoid(f_ref[...] + b_ref[...])
        r = lax.broadcasted_iota(jnp.int32, (tb, tb), 0)
        c = lax.broadcasted_iota(jnp.int32, (tb, tb), 1)
        upper = (r <= c).astype(BF16)
        o_ref[...] = _dot_01(ls, upper) + carry[...]
        carry[...] += jnp.sum(ls, axis=1, keepdims=True)

    return _call(body, name=name, grid=(S // tb,),
                 in_specs=[pl.BlockSpec((H, tb), lambda i: (0, i)), pl.BlockSpec((H, 1), lambda i: (0, 0))],
                 out_specs=pl.BlockSpec((H, tb), lambda i: (0, i)),
                 out_shape=jax.ShapeDtypeStruct((H, S), F32),
                 scratch=[pltpu.VMEM((H, 1), F32)], sem=("arbitrary",))(f_t, b_col)


def _dcum_bwd(dd_t, f_t, b_col, name):
    H, S = f_t.shape
    tb = _tile(S, 512, LANES)
    nb = S // tb

    def body(dd_ref, f_ref, b_ref, df_ref, db_ref, carry):
        first = pl.program_id(0) == 0

        @pl.when(first)
        def _():
            carry[...] = jnp.zeros_like(carry)

        dd = dd_ref[...]
        r = lax.broadcasted_iota(jnp.int32, (tb, tb), 0)
        c = lax.broadcasted_iota(jnp.int32, (tb, tb), 1)
        lower = (r >= c).astype(BF16)
        rev = _dot_01(dd, lower) + carry[...]
        carry[...] += jnp.sum(dd, axis=1, keepdims=True)
        df = rev * jax.nn.sigmoid(-(f_ref[...] + b_ref[...]))
        df_ref[...] = df
        part = jnp.sum(df, axis=1, keepdims=True)

        @pl.when(first)
        def _():
            db_ref[...] = part

        @pl.when(jnp.logical_not(first))
        def _():
            db_ref[...] += part

    return _call(body, name=name, grid=(nb,),
                 in_specs=[pl.BlockSpec((H, tb), lambda i: (0, nb - 1 - i)), pl.BlockSpec((H, tb), lambda i: (0, nb - 1 - i)),
                           pl.BlockSpec((H, 1), lambda i: (0, 0))],
                 out_specs=[pl.BlockSpec((H, tb), lambda i: (0, nb - 1 - i)), pl.BlockSpec((H, 1), lambda i: (0, 0))],
                 out_shape=[jax.ShapeDtypeStruct((H, S), F32), jax.ShapeDtypeStruct((H, 1), F32)],
                 scratch=[pltpu.VMEM((H, 1), F32)], sem=("arbitrary",))(dd_t, f_t, b_col)


def _attn_tile(S):
    return _tile(S, 512, LANES)


def _causal_t(t):
    return lax.broadcasted_iota(jnp.int32, (t, t), 0) <= lax.broadcasted_iota(jnp.int32, (t, t), 1)


def _tri_pairs(n, key_major):
    if key_major:
        pairs = [(i, j) for j in range(n) for i in range(j, n)]
    else:
        pairs = [(i, j) for i in range(n) for j in range(i + 1)]
    return jnp.asarray([p[0] for p in pairs], jnp.int32), jnp.asarray([p[1] for p in pairs], jnp.int32)


def _split3(x):
    hi = lax.reduce_precision(x, 8, 7)
    r = x - hi
    mid = lax.reduce_precision(r, 8, 7)
    lo = lax.reduce_precision(r - mid, 8, 7)
    return hi.astype(BF16), mid.astype(BF16), lo.astype(BF16)


def _augment(xn, dcum, query):
    H, S, hd = xn.shape
    parts = list(_split3(dcum))
    vals = parts + [1.0] * 3 if query else [1.0] * 3 + [-p for p in parts]
    lane = lax.broadcasted_iota(jnp.int32, (1, 1, LANES), 2)
    out = jnp.pad(xn, ((0, 0), (0, 0), (0, LANES - hd)))
    for k, val in enumerate(vals):
        val = jnp.asarray(val, BF16)
        out = jnp.where(lane == hd + k, val[..., None] if val.ndim else val, out)
    return out


def _scores_t(k_ref, qt_ref, h, t, diag):
    st = jnp.dot(k_ref[h], qt_ref[h], preferred_element_type=F32)
    return jnp.where(_causal_t(t), st, MASKED) if diag else st


def _flash_fwd(ka, qat, vat, hd, name):
    H, S, da = ka.shape
    t = _attn_tile(S)
    hb = ATTN_HEADS_PER_STEP
    it, jt = _tri_pairs(S // t, False)

    def body(it_ref, jt_ref, k_ref, qt_ref, vt_ref, o_ref, lse_ref, m_sc, acc_sc):
        i, j = it_ref[pl.program_id(1)], jt_ref[pl.program_id(1)]

        @pl.when(j == 0)
        def _():
            m_sc[...] = jnp.full_like(m_sc, MASKED)
            acc_sc[...] = jnp.zeros_like(acc_sc)

        def step(diag):
            sts = [_scores_t(k_ref, qt_ref, h, t, diag) for h in range(hb)]
            pts, alphas = [], []
            for h in range(hb):
                m_prev = m_sc[h]
                m_new = jnp.maximum(m_prev, jnp.max(sts[h], axis=0, keepdims=True))
                pts.append(jnp.exp(sts[h] - m_new).astype(BF16))
                alphas.append(jnp.exp(m_prev - m_new))
                m_sc[h] = m_new
            for h in range(hb):
                acc_sc[h] = alphas[h] * acc_sc[h] + jnp.dot(vt_ref[h], pts[h], preferred_element_type=F32)

        @pl.when(j < i)
        def _():
            step(False)

        @pl.when(j == i)
        def _():
            step(True)
            for h in range(hb):
                l = acc_sc[h, hd:hd + 1, :]
                o_ref[h] = acc_sc[h, :hd, :] / l
                lse_ref[h] = m_sc[h] + jnp.log(l)

    qcol = lambda h, p, it_, jt_: (h, 0, it_[p])
    kcol = lambda h, p, it_, jt_: (h, 0, jt_[p])
    krow = lambda h, p, it_, jt_: (h, jt_[p], 0)
    return _call_prefetch(
        body, name=name, grid=(H // hb, it.shape[0]), n_prefetch=2,
        in_specs=[pl.BlockSpec((hb, t, da), krow), pl.BlockSpec((hb, da, t), qcol), pl.BlockSpec((hb, da, t), kcol)],
        out_specs=[pl.BlockSpec((hb, hd, t), qcol), pl.BlockSpec((hb, 1, t), qcol)],
        out_shape=[jax.ShapeDtypeStruct((H, hd, S), F32), jax.ShapeDtypeStruct((H, 1, S), F32)],
        scratch=[pltpu.VMEM((hb, 1, t), F32), pltpu.VMEM((hb, da, t), F32)],
        sem=("parallel", "arbitrary"))(it, jt, ka, qat, vat)


def _flash_bwd(ka, kat, qat, v, dot, o_tr, lse_r, name):
    H, S, hd = v.shape
    da = ka.shape[2]
    t = _attn_tile(S)
    n = S // t
    hb = ATTN_HEADS_PER_STEP
    it, jt = _tri_pairs(n, True)
    over_queries = (((1,), (1,)), ((), ()))

    def body(it_ref, jt_ref, k_ref, kt_ref, qt_ref, v_ref, dot_ref, o_ref, lse_ref, dq_ref, dk_ref, dv_ref, dk_sc, dv_sc):
        i, j = it_ref[pl.program_id(1)], jt_ref[pl.program_id(1)]

        @pl.when(pl.program_id(1) == 0)
        def _():
            dq_ref[...] = jnp.zeros_like(dq_ref)

        def step(diag):
            for h0 in range(0, hb, ATTN_STAGED_HEADS):
                hs = range(h0, min(h0 + ATTN_STAGED_HEADS, hb))
                sts = [_scores_t(k_ref, qt_ref, h, t, diag) for h in hs]
                dpts = [jnp.dot(v_ref[h], dot_ref[h], preferred_element_type=F32) for h in hs]
                tiles = []
                for h, st, dpt in zip(hs, sts, dpts):
                    dl = jnp.sum(dot_ref[h].astype(F32) * o_ref[h], axis=0, keepdims=True)
                    pt = jnp.exp(st - lse_ref[h])
                    tiles.append((pt.astype(BF16), (pt * (dpt - dl)).astype(BF16)))
                for h, (ptb, dsb) in zip(hs, tiles):
                    dv_sc[h] += lax.dot_general(dot_ref[h], ptb, over_queries, preferred_element_type=F32)
                    dk_sc[h] += lax.dot_general(qt_ref[h], dsb, over_queries, preferred_element_type=F32)
                    dq_ref[h, i] += jnp.dot(kt_ref[h], dsb, preferred_element_type=F32)

        @pl.when(i == j)
        def _():
            dk_sc[...] = jnp.zeros_like(dk_sc)
            dv_sc[...] = jnp.zeros_like(dv_sc)
            step(True)

        @pl.when(i > j)
        def _():
            step(False)

        @pl.when(i == n - 1)
        def _():
            dk_ref[...] = dk_sc[...]
            dv_ref[...] = dv_sc[...]

    krow = lambda h, p, it_, jt_: (h, jt_[p], 0)
    kcol = lambda h, p, it_, jt_: (h, 0, jt_[p])
    qcol = lambda h, p, it_, jt_: (h, 0, it_[p])
    return _call_prefetch(
        body, name=name, grid=(H // hb, it.shape[0]), n_prefetch=2,
        in_specs=[pl.BlockSpec((hb, t, da), krow), pl.BlockSpec((hb, da, t), kcol), pl.BlockSpec((hb, da, t), qcol),
                  pl.BlockSpec((hb, t, hd), krow), pl.BlockSpec((hb, hd, t), qcol), pl.BlockSpec((hb, hd, t), qcol),
                  pl.BlockSpec((hb, 1, t), qcol)],
        out_specs=[pl.BlockSpec((hb, n, da, t), lambda h, p, it_, jt_: (h, 0, 0, 0)), pl.BlockSpec((hb, da, t), kcol),
                   pl.BlockSpec((hb, hd, t), kcol)],
        out_shape=[jax.ShapeDtypeStruct((H, n, da, t), F32), jax.ShapeDtypeStruct((H, da, S), F32),
                   jax.ShapeDtypeStruct((H, hd, S), F32)],
        scratch=[pltpu.VMEM((hb, da, t), F32), pltpu.VMEM((hb, hd, t), F32)],
        sem=("parallel", "arbitrary"))(it, jt, ka, kat, qat, v, dot, o_tr, lse_r)


def _offsets(n_bits):
    return [tuple((k >> b) & 1 for b in reversed(range(n_bits))) for k in range(1, 1 << n_bits)]


def _own_slot(out, own, idx):
    return lax.dynamic_update_index_in_dim(out, own.astype(out.dtype), idx, 0)


def _gather8(arrs, name):
    n = len(arrs)
    offs = _offsets(3)

    def body(*refs):
        ins, outs = refs[:n], refs[n:2 * n]
        ssem, rsem = refs[2 * n:]
        x, y, c = _place()
        me = 4 * x + 2 * y + c
        copies = []
        for a in range(n):
            for k, (dx, dy, dcc) in enumerate(offs):
                cp = pltpu.make_async_remote_copy(
                    src_ref=ins[a], dst_ref=outs[a].at[me], send_sem=ssem.at[a, k], recv_sem=rsem.at[a, k],
                    device_id=((x + dx) % 2, (y + dy) % 2, (c + dcc) % 2), device_id_type=MESH)
                cp.start()
                copies.append(cp)
        for cp in copies:
            cp.wait()

    return _call(body, name=name, in_specs=[ANY] * n, out_specs=[ANY] * n,
                 out_shape=[jax.ShapeDtypeStruct((N_DEV,) + a.shape, a.dtype) for a in arrs],
                 scratch=[pltpu.SemaphoreType.DMA((n, 7)), pltpu.SemaphoreType.DMA((n, 7))])(*arrs)


def _chip_position(chip, place):
    return sum(jnp.where(chip == s, place[1].index(s), 0) for s in range(N_CHIPS))


def _shard_slot(ref, shard_shape, place, chip, c):
    hn = shard_shape[0] // 2
    if place is None:
        return ref.at[chip, pl.ds(c * hn, hn)]
    ax = place[0]
    w = shard_shape[ax]
    idx = [slice(None)] * len(shard_shape)
    idx[0] = pl.ds(c * hn, hn)
    idx[ax] = pl.ds(pl.multiple_of(_chip_position(chip, place) * w, LANES if ax == len(shard_shape) - 1 else BF16_ROWS), w)
    return ref.at[tuple(idx)]


def _gathered_shape(shard_shape, place):
    if place is None:
        return (N_CHIPS,) + tuple(shard_shape)
    s = list(shard_shape)
    s[place[0]] *= N_CHIPS
    return tuple(s)


def _chip_gather(arrs, halved, name, places=None):
    n = len(arrs)
    offs = _offsets(2)
    places = places or [None] * n

    def body(*refs):
        ins, outs = refs[:n], refs[n:2 * n]
        ssem, rsem = refs[2 * n:]
        x, y, c = _place()
        chip = 2 * x + y
        copies = []
        for a in range(n):
            if halved:
                hn = arrs[a].shape[0] // 2
                src = ins[a].at[pl.ds(c * hn, hn)]
                dst = _shard_slot(outs[a], arrs[a].shape, places[a], chip, c)
            else:
                src, dst = ins[a], outs[a].at[chip]
            for k, (dx, dy) in enumerate(offs):
                cp = pltpu.make_async_remote_copy(
                    src_ref=src, dst_ref=dst, send_sem=ssem.at[a, k], recv_sem=rsem.at[a, k],
                    device_id=((x + dx) % 2, (y + dy) % 2, c), device_id_type=MESH)
                cp.start()
                copies.append(cp)
        for cp in copies:
            cp.wait()

    return _call(body, name=name, in_specs=[ANY] * n, out_specs=[ANY] * n,
                 out_shape=[jax.ShapeDtypeStruct(_gathered_shape(a.shape, p) if halved else (N_CHIPS,) + a.shape, a.dtype)
                            for a, p in zip(arrs, places)],
                 scratch=[pltpu.SemaphoreType.DMA((n, 3)), pltpu.SemaphoreType.DMA((n, 3))])(*arrs)


def _sibling_fill(bufs, shard_shapes, places, name):
    n = len(bufs)
    offs = _offsets(2)

    def body(*refs):
        ins, outs = refs[:n], refs[n:2 * n]
        ssem, rsem = refs[2 * n:]
        x, y, c = _place()
        copies = []
        for a in range(n):
            for k, (dx, dy) in enumerate(offs):
                chip = 2 * ((x + dx) % 2) + (y + dy) % 2
                cp = pltpu.make_async_remote_copy(
                    src_ref=_shard_slot(ins[a], shard_shapes[a], places[a], chip, c),
                    dst_ref=_shard_slot(outs[a], shard_shapes[a], places[a], chip, c),
                    send_sem=ssem.at[a, k], recv_sem=rsem.at[a, k],
                    device_id=(x, y, 1 - c), device_id_type=MESH)
                cp.start()
                copies.append(cp)
        for cp in copies:
            cp.wait()

    return _call(body, name=name, in_specs=[ANY] * n, out_specs=[ANY] * n,
                 out_shape=[jax.ShapeDtypeStruct(b.shape, b.dtype) for b in bufs],
                 scratch=[pltpu.SemaphoreType.DMA((n, 3)), pltpu.SemaphoreType.DMA((n, 3))],
                 aliases={a: a for a in range(n)})(*bufs)


def _sibling_pair(arrs, name):
    n = len(arrs)

    def body(*refs):
        ins, outs = refs[:n], refs[n:2 * n]
        ssem, rsem = refs[2 * n:]
        x, y, c = _place()
        copies = []
        for a in range(n):
            cp = pltpu.make_async_remote_copy(
                src_ref=ins[a], dst_ref=outs[a].at[c], send_sem=ssem.at[a], recv_sem=rsem.at[a],
                device_id=(x, y, 1 - c), device_id_type=MESH)
            cp.start()
            copies.append(cp)
        for cp in copies:
            cp.wait()

    return _call(body, name=name, in_specs=[ANY] * n, out_specs=[ANY] * n,
                 out_shape=[jax.ShapeDtypeStruct((N_CORES,) + a.shape, a.dtype) for a in arrs],
                 scratch=[pltpu.SemaphoreType.DMA((n,)), pltpu.SemaphoreType.DMA((n,))])(*arrs)


CHIP_ORDER = (0, 1, 2, 3)


def _piece(shape, spec, j, h):
    shard_ax, half_ax, order = spec
    w = shape[shard_ax] // N_CHIPS
    idx = [slice(None)] * len(shape)
    idx[shard_ax] = pl.ds(order[j] * w, w)
    assert half_ax != shard_ax
    hn = shape[half_ax] // 2
    idx[half_ax] = pl.ds(h * hn, hn)
    return tuple(idx)


def _piece_shape(shape, spec):
    shard_ax, half_ax, _ = spec
    s = list(shape)
    s[shard_ax] //= N_CHIPS
    s[half_ax] //= 2
    return tuple(s)


def _own_pieces(g, spec, c):
    shard_ax, half_ax, order = spec
    hn = g.shape[half_ax] // 2
    half = lax.dynamic_slice_in_dim(g, c * hn, hn, axis=half_ax)
    w = half.shape[shard_ax] // N_CHIPS
    return jnp.stack([lax.slice_in_dim(half, p * w, (p + 1) * w, axis=shard_ax) for p in order])


def _sibling_scatter(arrs, specs, name):
    n = len(arrs)

    def body(*refs):
        ins, outs = refs[:n], refs[n:2 * n]
        ssem, rsem = refs[2 * n:]
        x, y, c = _place()
        for mine in range(N_CORES):
            @pl.when(c == mine)
            def _():
                copies = []
                for a in range(n):
                    for j in range(N_CHIPS):
                        cp = pltpu.make_async_remote_copy(
                            src_ref=ins[a].at[_piece(arrs[a].shape, specs[a], j, 1 - mine)], dst_ref=outs[a].at[j],
                            send_sem=ssem.at[a, j], recv_sem=rsem.at[a, j],
                            device_id=(x, y, 1 - mine), device_id_type=MESH)
                        cp.start()
                        copies.append(cp)
                for cp in copies:
                    cp.wait()

    return _call(body, name=name, in_specs=[ANY] * n, out_specs=[ANY] * n,
                 out_shape=[jax.ShapeDtypeStruct((N_CHIPS,) + _piece_shape(a.shape, s), a.dtype)
                            for a, s in zip(arrs, specs)],
                 scratch=[pltpu.SemaphoreType.DMA((n, N_CHIPS))] * 2)(*arrs)


def _chip_scatter(arrs, name):
    n = len(arrs)
    offs = _offsets(2)

    def body(*refs):
        ins, outs = refs[:n], refs[n:2 * n]
        ssem, rsem = refs[2 * n:]
        x, y, c = _place()
        chip = 2 * x + y
        copies = []
        for a in range(n):
            for k, (dx, dy) in enumerate(offs):
                tx, ty = (x + dx) % 2, (y + dy) % 2
                cp = pltpu.make_async_remote_copy(
                    src_ref=ins[a].at[2 * tx + ty], dst_ref=outs[a].at[chip], send_sem=ssem.at[a, k], recv_sem=rsem.at[a, k],
                    device_id=(tx, ty, c), device_id_type=MESH)
                cp.start()
                copies.append(cp)
        for cp in copies:
            cp.wait()

    return _call(body, name=name, in_specs=[ANY] * n, out_specs=[ANY] * n,
                 out_shape=[jax.ShapeDtypeStruct(a.shape, a.dtype) for a in arrs],
                 scratch=[pltpu.SemaphoreType.DMA((n, 3)), pltpu.SemaphoreType.DMA((n, 3))])(*arrs)


def kernel(x, c, ada_w, ada_b, pre_mix_g, post_mix_g, pre_ffn_g, post_ffn_g, ffn_w_gu, ffn_w_down, a_w_in, a_b_in, a_ln_g, a_ln_b, a_w_s, a_b_s, a_w_out, kv_ada_w, kv_ada_b, kv_norm_g, kv_w, kv_b_f, k_norm_g, b_w_qg, b_q_norm_g, b_w_o, loss_target, m_ada_w, m_ada_b, m_pre_mix_g, m_post_mix_g, m_pre_ffn_g, m_post_ffn_g, m_ffn_w_gu, m_ffn_w_down, m_a_w_in, m_a_b_in, m_a_ln_g, m_a_ln_b, m_a_w_s, m_a_b_s, m_a_w_out, m_kv_ada_w, m_kv_ada_b, m_kv_norm_g, m_kv_w, m_kv_b_f, m_k_norm_g, m_b_w_qg, m_b_q_norm_g, m_b_w_o, v_ada_w, v_ada_b, v_pre_mix_g, v_post_mix_g, v_pre_ffn_g, v_post_ffn_g, v_ffn_w_gu, v_ffn_w_down, v_a_w_in, v_a_b_in, v_a_ln_g, v_a_ln_b, v_a_w_s, v_a_b_s, v_a_w_out, v_kv_ada_w, v_kv_ada_b, v_kv_norm_g, v_kv_w, v_kv_b_f, v_k_norm_g, v_b_w_qg, v_b_q_norm_g, v_b_w_o):
    weights = dict(ada_w=ada_w, ada_b=ada_b, pre_mix_g=pre_mix_g, post_mix_g=post_mix_g, pre_ffn_g=pre_ffn_g,
                   post_ffn_g=post_ffn_g, ffn_w_gu=ffn_w_gu, ffn_w_down=ffn_w_down, a_w_in=a_w_in, a_b_in=a_b_in,
                   a_ln_g=a_ln_g, a_ln_b=a_ln_b, a_w_s=a_w_s, a_b_s=a_b_s, a_w_out=a_w_out, kv_ada_w=kv_ada_w,
                   kv_ada_b=kv_ada_b, kv_norm_g=kv_norm_g, kv_w=kv_w, kv_b_f=kv_b_f, k_norm_g=k_norm_g, b_w_qg=b_w_qg,
                   b_q_norm_g=b_q_norm_g, b_w_o=b_w_o)
    m_in = dict(ada_w=m_ada_w, ada_b=m_ada_b, pre_mix_g=m_pre_mix_g, post_mix_g=m_post_mix_g, pre_ffn_g=m_pre_ffn_g,
                post_ffn_g=m_post_ffn_g, ffn_w_gu=m_ffn_w_gu, ffn_w_down=m_ffn_w_down, a_w_in=m_a_w_in, a_b_in=m_a_b_in,
                a_ln_g=m_a_ln_g, a_ln_b=m_a_ln_b, a_w_s=m_a_w_s, a_b_s=m_a_b_s, a_w_out=m_a_w_out, kv_ada_w=m_kv_ada_w,
                kv_ada_b=m_kv_ada_b, kv_norm_g=m_kv_norm_g, kv_w=m_kv_w, kv_b_f=m_kv_b_f, k_norm_g=m_k_norm_g,
                b_w_qg=m_b_w_qg, b_q_norm_g=m_b_q_norm_g, b_w_o=m_b_w_o)
    v_in = dict(ada_w=v_ada_w, ada_b=v_ada_b, pre_mix_g=v_pre_mix_g, post_mix_g=v_post_mix_g, pre_ffn_g=v_pre_ffn_g,
                post_ffn_g=v_post_ffn_g, ffn_w_gu=v_ffn_w_gu, ffn_w_down=v_ffn_w_down, a_w_in=v_a_w_in, a_b_in=v_a_b_in,
                a_ln_g=v_a_ln_g, a_ln_b=v_a_ln_b, a_w_s=v_a_w_s, a_b_s=v_a_b_s, a_w_out=v_a_w_out, kv_ada_w=v_kv_ada_w,
                kv_ada_b=v_kv_ada_b, kv_norm_g=v_kv_norm_g, kv_w=v_kv_w, kv_b_f=v_kv_b_f, k_norm_g=v_k_norm_g,
                b_w_qg=v_b_w_qg, b_q_norm_g=v_b_q_norm_g, b_w_o=v_b_w_o)
    names = list(weights)

    S, D = x.shape[1], x.shape[2]
    L, NA, NB = ada_w.shape[0], a_w_in.shape[0], b_w_qg.shape[0]
    H = kv_b_f.shape[0]
    hd = D // H
    G, CH = a_w_s.shape[1], a_w_s.shape[2]
    GW = a_w_out.shape[1] * N_CHIPS
    F = ffn_w_down.shape[1] * N_CHIPS
    ada_cols = ada_w.shape[2]
    kvada_cols = kv_ada_w.shape[1]
    kv_cols = kv_w.shape[1]
    kv_pad = -(-(2 * D + H) // LANES) * LANES
    xi, yi, ci = _place()
    chip = 2 * xi + yi
    me = 2 * chip + ci
    x0 = x[0]
    tgt = loss_target[0]
    row = lambda t: t.reshape(1, -1)

    c_all = _own_slot(_gather8([c], "gather_c")[0], c, me).reshape(N_DEV, D)
    c_act = _silu_rows(jnp.pad(c_all, ((0, BF16_ROWS - N_DEV), (0, 0))), "silu_c")
    mod_sh = [_mm(c_act, (ada_w, l), "nn", F32, f"mod_proj_{l}") for l in range(L)]
    mod_sh.append(_mm(c_act, kv_ada_w, "nn", F32, "mod_proj_kv"))
    mod_sh = jnp.concatenate(mod_sh, axis=1)
    small_sh = [mod_sh, a_b_in, a_ln_g, a_ln_b]
    mod_all, b_in_all, ln_g_all, ln_b_all = [
        _own_slot(o, s, chip) for o, s in zip(_chip_gather(small_sh, False, "gather_mod"), small_sh)]
    mine = lax.dynamic_index_in_dim(mod_all, me, axis=1, keepdims=False)
    mod = [jnp.concatenate([mine[j, l * ada_cols:(l + 1) * ada_cols] for j in range(N_CHIPS)]) + ada_b[l] for l in range(L)]
    mod = [[row(t) for t in jnp.split(m_, 6)] for m_ in mod]
    mod_kv = jnp.concatenate([mine[j, L * ada_cols:] for j in range(N_CHIPS)]) + kv_ada_b
    kv_sh, kv_sc = [row(t) for t in jnp.split(mod_kv, 2)]
    cat_chips = lambda t, ax: jnp.concatenate([t[j] for j in range(N_CHIPS)], axis=ax)
    b_in_f = cat_chips(b_in_all, 1)
    ln_g_f, ln_b_f = cat_chips(ln_g_all, 1), cat_chips(ln_b_all, 1)

    big = ["ffn_w_gu", "ffn_w_down", "a_w_in", "a_w_out", "kv_w", "b_w_qg", "b_w_o"]
    own_w = [weights[n].astype(BF16) for n in big]
    gu_order = (0, 2, 1, 3)
    places = {"ffn_w_gu": (2, gu_order), "ffn_w_down": (1, CHIP_ORDER), "a_w_in": (2, CHIP_ORDER),
              "a_w_out": (1, CHIP_ORDER), "kv_w": None, "b_w_qg": (2, CHIP_ORDER), "b_w_o": (1, CHIP_ORDER)}
    plist = [places[n] for n in big]
    gathered = _sibling_fill(_chip_gather(own_w, True, "gather_w", plist), [w.shape for w in own_w], plist, "fill_w")
    full_w = {}
    for n, g, w in zip(big, gathered, own_w):
        if places[n] is None:
            full_w[n] = _own_slot(g, w, chip)
        else:
            ax = places[n][0]
            full_w[n] = lax.dynamic_update_slice_in_dim(g, w, _chip_position(chip, places[n]) * w.shape[ax], axis=ax)
    gu_hw = ffn_w_gu.shape[2]
    w_gu, w_dn, w_in, w_out = full_w["ffn_w_gu"], full_w["ffn_w_down"], full_w["a_w_in"], full_w["a_w_out"]
    w_qg, w_o = full_w["b_w_qg"], full_w["b_w_o"]
    w_kv = jnp.pad(cat_chips(full_w["kv_w"], 1), ((0, 0), (0, kv_pad - (2 * D + H))))

    causal = jnp.tril(jnp.ones((CH, CH), F32))
    ws_m = [(a_w_s[i] * causal).astype(BF16) for i in range(NA)]
    ws_mt = [jnp.swapaxes(w, 1, 2) for w in ws_m]
    bs_t = [a_b_s[i].T for i in range(NA)]

    heads = lambda t: t.reshape(S, H, hd).transpose(1, 0, 2)
    unheads = lambda t: t.transpose(1, 0, 2).reshape(S, D)

    saved = []
    kv = None
    xc = x0
    h1 = _norm_mod_fwd(xc, row(pre_mix_g[0]), mod[0][0], mod[0][1], "pre_mix_0")
    for l in range(L):
        sh_m, sc_m, g_m, sh_f, sc_f, g_f = mod[l]
        st = {"x0": xc, "h1": h1}
        if l < NA:
            zp = _mm(h1, (w_in, l), "nn", BF16, f"gmlp_in_{l}")
            yg = _gmlp_fwd(zp, row(b_in_f[l]), row(ln_g_f[l]), row(ln_b_f[l]), ws_m[l], bs_t[l], f"gmlp_gate_{l}")
            y = _mm(yg, (w_out, l), "nn", F32, f"gmlp_out_{l}")
            st.update(zp=zp, yg=yg)
        else:
            jb = l - NA
            qg = _mm(h1, (w_qg, jb), "nn", BF16, f"fox_qg_{jb}")
            q_raw = heads(qg[:, :D]).reshape(H * S, hd)
            qn = _head_norm_fwd(q_raw, row(b_q_norm_g[jb]), hd ** -0.5, f"fox_qnorm_{jb}").reshape(H, S, hd)
            qa = _augment(qn, kv["dcum"], True)
            qat = jnp.swapaxes(qa, 1, 2)
            o_tr, lse_r = _flash_fwd(kv["ka"], qat, kv["vat"], hd, f"fox_attn_{jb}")
            o_t = o_tr.transpose(2, 0, 1).reshape(S, D)
            og = _out_gate_fwd(o_t, qg, f"fox_gate_{jb}")
            y = _mm(og, (w_o, jb), "nn", F32, f"fox_out_{jb}")
            st.update(qg=qg, q_raw=q_raw, qat=qat, o_tr=o_tr, lse_r=lse_r, o_t=o_t, og=og)
        st["y"] = y
        x1, h2 = _post_pre_fwd(xc, y, row(post_mix_g[l]), g_m, row(pre_ffn_g[l]), sh_f, sc_f, f"post_mix_{l}")
        st["x1"] = x1
        gu, act = _ffn_up(h2, w_gu, l, gu_hw, f"ffn_gu_{l}")
        y2 = _mm(act, (w_dn, l), "nn", F32, f"ffn_down_{l}")
        if l + 1 < L:
            xc, h1 = _post_pre_fwd(x1, y2, row(post_ffn_g[l]), g_f, row(pre_mix_g[l + 1]), mod[l + 1][0], mod[l + 1][1],
                                   f"post_ffn_{l}")
        else:
            xc = _post_fwd(x1, y2, row(post_ffn_g[l]), g_f, f"post_ffn_{l}")
        st.update(h2=h2, gu=gu, act=act, y2=y2)
        saved.append(st)
        if l == NA - 1:
            hk = _norm_mod_fwd(xc, row(kv_norm_g), kv_sh, kv_sc, "kv_pre")
            kvf = _mm(hk, w_kv, "nn", F32, "kv_proj")
            k_raw = heads(kvf[:, :D]).reshape(H * S, hd)
            kn = _head_norm_fwd(k_raw, row(k_norm_g), 1.0, "kv_knorm").reshape(H, S, hd)
            vb = heads(kvf[:, D:2 * D]).astype(BF16)
            f_t = kvf[:, 2 * D:2 * D + H].T
            b_col = kv_b_f.reshape(H, 1)
            dcum = _dcum_fwd(f_t, b_col, "kv_dcum")
            vt = kvf[:, D:2 * D].astype(BF16).reshape(S, H, hd).transpose(1, 2, 0)
            vat = jnp.where(lax.broadcasted_iota(jnp.int32, (1, LANES, 1), 1) == hd, jnp.asarray(1, BF16),
                            jnp.pad(vt, ((0, 0), (0, LANES - hd), (0, 0))))
            ka = _augment(kn, dcum, False)
            kv = dict(x=xc, hk=hk, k_raw=k_raw, ka=ka, kat=jnp.swapaxes(ka, 1, 2), vb=vb, vat=vat,
                      f_t=f_t, b_col=b_col, dcum=dcum)

    dx, loss_part = _loss_bwd(xc, tgt, "loss")

    gl = {n: [None] * weights[n].shape[0] for n in
          ["pre_mix_g", "post_mix_g", "pre_ffn_g", "post_ffn_g", "ffn_w_gu", "ffn_w_down", "a_w_in", "a_b_in", "a_ln_g",
           "a_ln_b", "a_w_s", "a_b_s", "a_w_out", "b_w_qg", "b_q_norm_g", "b_w_o"]}
    dmod = [None] * L
    dkn = dvb = ddc = None
    gkv = {}
    for l in reversed(range(L)):
        st = saved[l]
        sh_m, sc_m, g_m, sh_f, sc_f, g_f = mod[l]
        if l == NA - 1:
            dk_raw, gkv["k_norm_g"] = _head_norm_bwd(jnp.swapaxes(dkn, 1, 2).reshape(H * S, hd), kv["k_raw"], row(k_norm_g),
                                                     1.0, "kv_knorm_bwd")
            df_t, db_f = _dcum_bwd(ddc.reshape(H, S), kv["f_t"], kv["b_col"], "kv_dcum_bwd")
            dkvf = jnp.concatenate([unheads(dk_raw.reshape(H, S, hd)), dvb.transpose(2, 0, 1).reshape(S, D), df_t.T,
                                    jnp.zeros((S, kv_pad - (2 * D + H)), F32)], axis=1).astype(BF16)
            gkv["kv_w"] = _mm(kv["hk"], dkvf, "tn", BF16, "kv_proj_dw")[:, :2 * D + H]
            dhk = _mm(dkvf, w_kv, "nt", F32, "kv_proj_dx")
            dx, gkv["kv_norm_g"], dsh, dsc = _norm_mod_bwd(dx, dhk, kv["x"], row(kv_norm_g), kv_sh, kv_sc, "kv_pre_bwd")
            gkv["kv_b_f"] = db_f.reshape(H)
            dmod_kv = jnp.concatenate([dsh, dsc], axis=1)
        dy2, gl["post_ffn_g"][l], dg_f = _post_bwd(dx, st["y2"], row(post_ffn_g[l]), g_f, f"post_ffn_bwd_{l}")
        gl["ffn_w_down"][l] = _mm(st["act"], dy2, "tn", BF16, f"ffn_down_dw_{l}")
        dgu = _ffn_down_dx(dy2, w_dn, l, st["gu"], gu_hw, f"ffn_down_dx_{l}")
        gl["ffn_w_gu"][l] = _mm(st["h2"], dgu, "tn", BF16, f"ffn_gu_dw_{l}")
        dh2 = _mm(dgu, (w_gu, l), "nt", F32, f"ffn_gu_dx_{l}")
        dx, dy, gl["pre_ffn_g"][l], dsh_f, dsc_f, gl["post_mix_g"][l], dg_m = _pre_post_bwd(
            dx, dh2, st["x1"], row(pre_ffn_g[l]), sh_f, sc_f, st["y"], row(post_mix_g[l]), g_m, f"pre_ffn_bwd_{l}")
        if l < NA:
            gl["a_w_out"][l] = _mm(st["yg"], dy, "tn", BF16, f"gmlp_out_dw_{l}")
            dyg = _mm(dy, (w_out, l), "nt", BF16, f"gmlp_out_dx_{l}")
            dzp, db_in, dlg, dlb, dws, dbs_t = _gmlp_bwd(dyg, st["zp"], row(b_in_f[l]), row(ln_g_f[l]), row(ln_b_f[l]),
                                                           ws_m[l], ws_mt[l], bs_t[l], f"gmlp_gate_bwd_{l}")
            gl["a_b_in"][l], gl["a_ln_g"][l], gl["a_ln_b"][l] = db_in[0], dlg[0], dlb[0]
            gl["a_w_s"][l], gl["a_b_s"][l] = dws * causal, dbs_t.T
            gl["a_w_in"][l] = _mm(st["h1"], dzp, "tn", BF16, f"gmlp_in_dw_{l}")
            dh1 = _mm(dzp, (w_in, l), "nt", F32, f"gmlp_in_dx_{l}")
        else:
            jb = l - NA
            gl["b_w_o"][jb] = _mm(st["og"], dy, "tn", BF16, f"fox_out_dw_{jb}")
            dog = _mm(dy, (w_o, jb), "nt", F32, f"fox_out_dx_{jb}")
            do_t, dgl = _out_gate_bwd(dog, st["o_t"], st["qg"], f"fox_gate_bwd_{jb}")
            dot = do_t.reshape(S, H, hd).transpose(1, 2, 0)
            dqa_tr, dka_tr, dv_j = _flash_bwd(kv["ka"], kv["kat"], st["qat"], kv["vb"], dot, st["o_tr"], st["lse_r"],
                                              f"fox_attn_bwd_{jb}")
            dqn = dqa_tr[:, :, :hd, :].transpose(0, 1, 3, 2).reshape(H, S, hd)
            dk_j = dka_tr[:, :hd, :]
            dd_j = dqa_tr[:, :, hd, :].reshape(H, S) - dka_tr[:, hd + 3, :]
            dkn = dk_j if dkn is None else dkn + dk_j
            dvb = dv_j if dvb is None else dvb + dv_j
            ddc = dd_j if ddc is None else ddc + dd_j
            dq_raw, dgq = _head_norm_bwd(dqn.reshape(H * S, hd), st["q_raw"], row(b_q_norm_g[jb]), hd ** -0.5, f"fox_qnorm_bwd_{jb}")
            gl["b_q_norm_g"][jb] = dgq[0]
            dqg = jnp.concatenate([unheads(dq_raw.reshape(H, S, hd)).astype(BF16), dgl], axis=1)
            gl["b_w_qg"][jb] = _mm(st["h1"], dqg, "tn", BF16, f"fox_qg_dw_{jb}")
            dh1 = _mm(dqg, (w_qg, jb), "nt", F32, f"fox_qg_dx_{jb}")
        dx, gl["pre_mix_g"][l], dsh_m, dsc_m = _norm_mod_bwd(dx, dh1, st["x0"], row(pre_mix_g[l]), sh_m, sc_m, f"pre_mix_bwd_{l}")
        dmod[l] = jnp.concatenate([dsh_m, dsc_m, dg_m, dsh_f, dsc_f, dg_f], axis=1)
    grad_x = dx[None]

    stack = lambda n: jnp.stack([t.reshape(weights[n].shape[1:]) for t in gl[n]])
    small = {"dmod": jnp.concatenate(dmod, axis=1), "dmod_kv": dmod_kv}
    for n in ["pre_mix_g", "post_mix_g", "pre_ffn_g", "post_ffn_g", "a_w_s", "a_b_s", "b_q_norm_g"]:
        small[n] = stack(n)
    for n in ["a_b_in", "a_ln_g", "a_ln_b"]:
        small[n] = jnp.stack(gl[n])
    for n in ["kv_norm_g", "kv_b_f", "k_norm_g"]:
        small[n] = gkv[n]
    small["loss"] = loss_part
    sizes = {n: t.size for n, t in small.items()}
    flat = jnp.concatenate([t.reshape(-1).astype(F32) for t in small.values()])
    rows_small = -(-flat.size // (LANES * BF16_ROWS)) * BF16_ROWS
    flat = jnp.pad(flat, (0, rows_small * LANES - flat.size)).reshape(rows_small, LANES)
    flat_all = _own_slot(_gather8([flat], "gather_small")[0], flat, me)
    flat_sum = _sum_slots(flat_all, "sum_small").reshape(-1)
    offs, o_ = {}, 0
    for n, sz in sizes.items():
        offs[n] = o_
        o_ += sz
    take = lambda n, shape: flat_sum[offs[n]:offs[n] + sizes[n]].reshape(shape)
    dmod_rows = flat_all.reshape(N_DEV, -1)[:, offs["dmod"]:offs["dmod"] + sizes["dmod"] + sizes["dmod_kv"]]
    dmod_rows = jnp.pad(dmod_rows, ((0, BF16_ROWS - N_DEV), (0, 0)))

    grads = {}
    loss = take("loss", ())
    grads["ada_b"] = take("dmod", (L, 6 * D))
    grads["kv_ada_b"] = take("dmod_kv", (2 * D,))
    for n in ["pre_mix_g", "post_mix_g", "pre_ffn_g", "post_ffn_g", "a_w_s", "a_b_s", "b_q_norm_g", "kv_norm_g", "kv_b_f", "k_norm_g"]:
        grads[n] = take(n, weights[n].shape)
    for n in ["a_b_in", "a_ln_g", "a_ln_b"]:
        full = take(n, small[n].shape)
        w = weights[n].shape[1]
        grads[n] = lax.dynamic_slice_in_dim(full, chip * w, w, axis=1)
    ada_g = []
    for l in range(L):
        cols = lax.dynamic_slice_in_dim(dmod_rows[:, l * 6 * D:(l + 1) * 6 * D], chip * ada_cols, ada_cols, axis=1)
        ada_g.append(_mm(c_act, cols, "tn", F32, f"mod_proj_dw_{l}"))
    grads["ada_w"] = jnp.stack(ada_g)
    cols = lax.dynamic_slice_in_dim(dmod_rows[:, L * 6 * D:], chip * kvada_cols, kvada_cols, axis=1)
    grads["kv_ada_w"] = _mm(c_act, cols, "tn", F32, "mod_proj_kv_dw")

    specs = {"ffn_w_gu": (2, 0, (0, 2, 1, 3)),
             "ffn_w_down": (1, 0, CHIP_ORDER), "a_w_in": (2, 0, CHIP_ORDER), "a_w_out": (1, 0, CHIP_ORDER),
             "kv_w": (0, 1, CHIP_ORDER), "b_w_qg": (2, 0, CHIP_ORDER), "b_w_o": (1, 0, CHIP_ORDER)}
    full_g = {n: jnp.stack(gl[n]) for n in big if n != "kv_w"}
    full_g["kv_w"] = gkv["kv_w"].reshape(D, N_CHIPS, kv_cols).transpose(1, 0, 2)
    from_core = _sibling_scatter([full_g[n] for n in big], [specs[n] for n in big], "scatter_g_core")
    chip_sums = [_sum_pair(_own_pieces(full_g[n], specs[n], ci), r, f"sum_g_core_{n}") for n, r in zip(big, from_core)]
    recv = _chip_scatter(chip_sums, "scatter_g_chip")
    recv = [_own_slot(r, lax.dynamic_index_in_dim(p, chip, 0, keepdims=False), chip) for r, p in zip(recv, chip_sums)]
    halves = [_sum_slots(r, f"sum_g_{n}") for n, r in zip(big, recv)]
    pairs = _sibling_pair(halves, "pair_g")
    for n, p, hlf in zip(big, pairs, halves):
        grads[n] = _own_slot(p, hlf, ci).reshape(weights[n].shape)

    outs_d, outs_m, outs_v = {}, {}, {}
    for n in names:
        w2 = weights[n] if weights[n].ndim > 1 else weights[n].reshape(1, -1)
        shp = w2.shape
        d_, m_, v_ = _adamw(w2, grads[n].reshape(shp), m_in[n].reshape(shp), v_in[n].reshape(shp), f"adamw_{n}")
        outs_d[n], outs_m[n], outs_v[n] = (t.reshape(weights[n].shape) for t in (d_, m_, v_))
    return (loss, grad_x, *[grads[n] for n in names], *[outs_d[n] for n in names],
            *[outs_m[n] for n in names], *[outs_v[n] for n in names])
```

```python
import jax
import jax.numpy as jnp
from jax import lax
from jax.experimental import pallas as pl
from jax.experimental.pallas import tpu as pltpu

F32 = jnp.float32
BF16 = jnp.bfloat16
MESH = pl.DeviceIdType.MESH
NORM_EPS = 1e-6
MASKED = -1e30
LANES = 128
BF16_ROWS = 16
ROW_BLOCK_BYTES = 12 << 20
ADAM_LR, ADAM_B1, ADAM_B2, ADAM_EPS, ADAM_WD, ADAM_STEP = 0.001, 0.9, 0.999, 1e-08, 0.01, 10
N_CHIPS, N_CORES, N_DEV = 4, 2, 8
ATTN_HEADS_PER_STEP = 4
ATTN_STAGED_HEADS = 2
ANY = pl.BlockSpec(memory_space=pl.ANY)


def _tile(n, cap, quantum):
    best = None
    d = quantum
    while d <= min(n, cap):
        if n % d == 0:
            best = d
        d += quantum
    return n if best is None else best


def _call(body, *, name, out_shape, grid=(), in_specs=None, out_specs=None, scratch=(), sem=None, aliases=None):
    params = {} if sem is None else {"dimension_semantics": sem}
    return pl.pallas_call(
        body, name=name, grid=grid, in_specs=in_specs, out_specs=out_specs, out_shape=out_shape,
        scratch_shapes=list(scratch), input_output_aliases=aliases or {},
        compiler_params=pltpu.CompilerParams(**params))


def _call_prefetch(body, *, name, out_shape, grid, n_prefetch, in_specs, out_specs, scratch, sem):
    spec = pltpu.PrefetchScalarGridSpec(num_scalar_prefetch=n_prefetch, grid=grid, in_specs=in_specs,
                                        out_specs=out_specs, scratch_shapes=list(scratch))
    return pl.pallas_call(
        body, name=name, grid_spec=spec, out_shape=out_shape,
        compiler_params=pltpu.CompilerParams(dimension_semantics=sem))


def _place():
    x, y, c = lax.axis_index("x"), lax.axis_index("y"), lax.axis_index("c")
    return x, y, c


def _mm(a, b, mode, out_dtype, name):
    b_arr, b_idx = b if isinstance(b, tuple) else (b, None)
    bs = b_arr.shape[-2:]
    if mode == "nn":
        (M, K), (K2, N) = a.shape, bs
        dims = (((1,), (0,)), ((), ()))
    elif mode == "nt":
        (M, K), (N, K2) = a.shape, bs
        dims = (((1,), (1,)), ((), ()))
    else:
        (K, M), (K2, N) = a.shape, bs
        dims = (((0,), (0,)), ((), ()))
    assert K == K2, (name, a.shape, b_arr.shape)
    if mode == "tn":
        tm = _tile(M, 1408, LANES)
        tk = _tile(K, 2048, BF16_ROWS)
        tn = _tile(N, 512, LANES)
    else:
        tm = _tile(M, 1024, BF16_ROWS)
        tk = K if K <= 2816 else _tile(K, 2816, LANES)
        tn = _tile(N, 1408 if tk <= 1024 else 512, LANES)
    if tn < 256:
        tn = N
        tm = _tile(M, 512, LANES if mode == "tn" else BF16_ROWS)
    nk = K // tk
    grid = (M // tm, N // tn, nk)

    if mode == "tn":
        a_spec = pl.BlockSpec((tk, tm), lambda i, j, k: (k, i))
    else:
        a_spec = pl.BlockSpec((tm, tk), lambda i, j, k: (i, k))
    if mode == "nt":
        b_blk, b_map = (tn, tk), (lambda i, j, k: (j, k))
    else:
        b_blk, b_map = (tk, tn), (lambda i, j, k: (k, j))
    if b_idx is None:
        b_spec = pl.BlockSpec(b_blk, b_map)
    else:
        b_spec = pl.BlockSpec((None,) + b_blk, lambda i, j, k: (b_idx,) + b_map(i, j, k))

    def body(a_ref, b_ref, o_ref, *acc):
        r = lax.dot_general(a_ref[...].astype(BF16), b_ref[...].astype(BF16), dims, preferred_element_type=F32)
        if nk == 1:
            o_ref[...] = r.astype(o_ref.dtype)
        else:
            k = pl.program_id(2)

            @pl.when(k == 0)
            def _():
                acc[0][...] = r

            @pl.when(k > 0)
            def _():
                acc[0][...] += r

            @pl.when(k == nk - 1)
            def _():
                o_ref[...] = acc[0][...].astype(o_ref.dtype)

    return _call(
        body, name=name, grid=grid, in_specs=[a_spec, b_spec],
        out_specs=pl.BlockSpec((tm, tn), lambda i, j, k: (i, j)),
        out_shape=jax.ShapeDtypeStruct((M, N), out_dtype),
        scratch=[pltpu.VMEM((tm, tn), F32)] if nk > 1 else [],
        sem=("parallel", "parallel", "arbitrary"))(a, b_arr)


def _rowwise(fn, rows, pars, outs, pouts, name):
    R = rows[0].shape[0]
    row_bytes = 4 * (sum(max(r.shape[1], LANES) for r in rows) + sum(max(c, LANES) for c, _ in outs))
    tb = _tile(R, max(BF16_ROWS, ROW_BLOCK_BYTES // row_bytes), BF16_ROWS)
    nr, npar, no = len(rows), len(pars), len(outs)

    def body(*refs):
        r_in, p_in = refs[:nr], refs[nr:nr + npar]
        r_out, p_out = refs[nr + npar:nr + npar + no], refs[nr + npar + no:]
        ro, po = fn([r[...] for r in r_in], [p[...] for p in p_in])
        for ref, val in zip(r_out, ro):
            if isinstance(val, (tuple, list)):
                off = 0
                for piece in val:
                    w = piece.shape[1]
                    ref[:, off:off + w] = piece.astype(ref.dtype)
                    off += w
            else:
                ref[...] = val.astype(ref.dtype)
        if p_out:
            first = pl.program_id(0) == 0

            @pl.when(first)
            def _():
                for ref, val in zip(p_out, po):
                    ref[...] = val

            @pl.when(jnp.logical_not(first))
            def _():
                for ref, val in zip(p_out, po):
                    ref[...] += val

    res = _call(
        body, name=name, grid=(R // tb,),
        in_specs=[pl.BlockSpec((tb, r.shape[1]), lambda i: (i, 0)) for r in rows]
        + [pl.BlockSpec(p.shape, lambda i: (0, 0)) for p in pars],
        out_specs=[pl.BlockSpec((tb, c), lambda i: (i, 0)) for c, _ in outs]
        + [pl.BlockSpec(s, lambda i: (0, 0)) for s in pouts],
        out_shape=[jax.ShapeDtypeStruct((R, c), dt) for c, dt in outs]
        + [jax.ShapeDtypeStruct(s, F32) for s in pouts],
        sem=("arbitrary",) if pouts else ("parallel",))(*rows, *pars)
    return list(res)


def _rms(x, g):
    return x * lax.rsqrt(jnp.mean(x * x, axis=-1, keepdims=True) + NORM_EPS) * g


def _norm_mod(x, g, sh, sc):
    return _rms(x, g) * (1.0 + sc) + sh


def _gated_post(y, g, gate):
    return gate * _rms(y, g)


def _norm_mod_fwd(x, g, sh, sc, name):
    return _rowwise(lambda r, p: ([_norm_mod(r[0], *p)], []), [x], [g, sh, sc], [(x.shape[1], BF16)], [], name)[0]


def _norm_mod_bwd(dxo, dh, x, g, sh, sc, name):
    def fn(r, p):
        _, vjp = jax.vjp(_norm_mod, r[2], *p)
        dx, dg, dsh, dsc = vjp(r[1].astype(F32))
        return [r[0] + dx], [dg, dsh, dsc]
    c = x.shape[1]
    return _rowwise(fn, [dxo, dh, x], [g, sh, sc], [(c, F32)], [(1, c)] * 3, name)


def _post_fwd(x, y, g, gate, name):
    return _rowwise(lambda r, p: ([r[0] + _gated_post(r[1].astype(F32), *p)], []), [x, y], [g, gate],
                    [(x.shape[1], F32)], [], name)[0]


def _post_bwd(dxo, y, g, gate, name):
    def fn(r, p):
        _, vjp = jax.vjp(_gated_post, r[1].astype(F32), *p)
        dy, dg, dgate = vjp(r[0])
        return [dy], [dg, dgate]
    c = y.shape[1]
    return _rowwise(fn, [dxo, y], [g, gate], [(c, BF16)], [(1, c)] * 2, name)


def _post_pre_fwd(x, y, g_post, gate, g_pre, sh, sc, name):
    def fn(r, p):
        x1 = r[0] + _gated_post(r[1].astype(F32), p[0], p[1])
        return [x1, _norm_mod(x1, p[2], p[3], p[4])], []
    c = x.shape[1]
    return _rowwise(fn, [x, y], [g_post, gate, g_pre, sh, sc], [(c, F32), (c, BF16)], [], name)


def _pre_post_bwd(dxo, dh, x, g_pre, sh, sc, y, g_post, gate, name):
    def fn(r, p):
        _, vjp_pre = jax.vjp(_norm_mod, r[2], p[0], p[1], p[2])
        dxn, dg_pre, dsh, dsc = vjp_pre(r[1].astype(F32))
        dx = r[0] + dxn
        _, vjp_post = jax.vjp(_gated_post, r[3].astype(F32), p[3], p[4])
        dy, dg_post, dgate = vjp_post(dx)
        return [dx, dy], [dg_pre, dsh, dsc, dg_post, dgate]
    c = x.shape[1]
    return _rowwise(fn, [dxo, dh, x, y], [g_pre, sh, sc, g_post, gate], [(c, F32), (c, BF16)], [(1, c)] * 5, name)


def _swiglu(g, u):
    return jax.nn.silu(g) * u


def _ffn_up(h, w, l, hw, name):
    S, D = h.shape
    nb = w.shape[2] // (2 * hw)
    tm = _tile(S, 512, BF16_ROWS)

    def body(h_ref, w_ref, gu_ref, act_ref):
        gu = jnp.dot(h_ref[...], w_ref[...], preferred_element_type=F32).astype(BF16)
        gu_ref[...] = gu
        act_ref[...] = _swiglu(gu[:, :hw].astype(F32), gu[:, hw:].astype(F32)).astype(BF16)

    return _call(body, name=name, grid=(S // tm, nb),
                 in_specs=[pl.BlockSpec((tm, D), lambda i, j: (i, 0)), pl.BlockSpec((None, D, 2 * hw), lambda i, j: (l, 0, j))],
                 out_specs=[pl.BlockSpec((tm, 2 * hw), lambda i, j: (i, j)), pl.BlockSpec((tm, hw), lambda i, j: (i, j))],
                 out_shape=[jax.ShapeDtypeStruct((S, 2 * hw * nb), BF16), jax.ShapeDtypeStruct((S, hw * nb), BF16)],
                 sem=("parallel", "parallel"))(h, w)


def _ffn_down_dx(dy, w_dn, l, gu, hw, name):
    S, D = dy.shape
    nb = gu.shape[1] // (2 * hw)
    tm = _tile(S, 512, BF16_ROWS)

    def body(dy_ref, w_ref, gu_ref, dgu_ref):
        dact = lax.dot_general(dy_ref[...], w_ref[...], (((1,), (1,)), ((), ())), preferred_element_type=F32)
        _, vjp = jax.vjp(_swiglu, gu_ref[:, :hw].astype(F32), gu_ref[:, hw:].astype(F32))
        dg, du = vjp(dact)
        dgu_ref[:, :hw] = dg.astype(BF16)
        dgu_ref[:, hw:] = du.astype(BF16)

    return _call(body, name=name, grid=(S // tm, nb),
                 in_specs=[pl.BlockSpec((tm, D), lambda i, j: (i, 0)), pl.BlockSpec((None, hw, D), lambda i, j: (l, j, 0)),
                           pl.BlockSpec((tm, 2 * hw), lambda i, j: (i, j))],
                 out_specs=pl.BlockSpec((tm, 2 * hw), lambda i, j: (i, j)),
                 out_shape=jax.ShapeDtypeStruct(gu.shape, BF16), sem=("parallel", "parallel"))(dy, w_dn, gu)


def _silu_rows(c, name):
    return _rowwise(lambda r, p: ([jax.nn.silu(r[0])], []), [c], [], [(c.shape[1], F32)], [], name)[0]


def _head_norm(x, g, scale):
    return _rms(x, g) * scale


def _head_norm_fwd(x, g, scale, name):
    return _rowwise(lambda r, p: ([_head_norm(r[0].astype(F32), p[0], scale)], []), [x], [g],
                    [(x.shape[1], BF16)], [], name)[0]


def _head_norm_bwd(dy, x, g, scale, name):
    def fn(r, p):
        _, vjp = jax.vjp(lambda t, gg: _head_norm(t, gg, scale), r[1].astype(F32), p[0])
        dx, dg = vjp(r[0])
        return [dx], [dg]
    c = x.shape[1]
    return _rowwise(fn, [dy, x], [g], [(c, F32)], [(1, c)], name)


def _out_gate_fwd(o, qg, name):
    d = o.shape[1]
    return _rowwise(lambda r, p: ([r[0] * jax.nn.sigmoid(r[1][:, d:].astype(F32))], []), [o, qg], [],
                    [(d, BF16)], [], name)[0]


def _out_gate_bwd(dog, o, qg, name):
    d = o.shape[1]

    def fn(r, p):
        _, vjp = jax.vjp(lambda oo, gl: oo * jax.nn.sigmoid(gl), r[1], r[2][:, d:].astype(F32))
        do, dgl = vjp(r[0])
        return [do, dgl], []
    return _rowwise(fn, [dog, o, qg], [], [(d, BF16), (d, BF16)], [], name)


def _loss_bwd(y, tgt, name):
    n = y.shape[1]

    def fn(r, p):
        e = r[0] - r[1]
        part = jnp.sum(jnp.sum(e * e, axis=1, keepdims=True), axis=0, keepdims=True) * (0.5 / n)
        return [e * (1.0 / n)], [part]
    return _rowwise(fn, [y, tgt], [], [(n, F32)], [(1, 1)], name)


def _adamw(w, g, m, v, name):
    shape = w.shape
    c = shape[-1]
    flat = [t.reshape(-1, c) for t in (w, g, m, v)]

    def fn(r, p):
        w_, g_, m_, v_ = r
        m2 = ADAM_B1 * m_ + (1.0 - ADAM_B1) * g_
        v2 = ADAM_B2 * v_ + (1.0 - ADAM_B2) * (g_ * g_)
        m_hat = m2 / (1.0 - ADAM_B1 ** ADAM_STEP)
        v_hat = v2 / (1.0 - ADAM_B2 ** ADAM_STEP)
        delta = -ADAM_LR * (m_hat / (jnp.sqrt(v_hat) + ADAM_EPS) + ADAM_WD * w_)
        return [delta, m2, v2], []
    res = _rowwise(fn, flat, [], [(c, F32)] * 3, [], name)
    return [t.reshape(shape) for t in res]


def _sum_pair(a, b, name):
    c = a.shape[-1]
    out = _rowwise(lambda r, p: ([r[0].astype(F32) + r[1].astype(F32)], []), [a.reshape(-1, c), b.reshape(-1, c)], [],
                   [(c, BF16)], [], name)[0]
    return out.reshape(a.shape)


def _sum_slots(recv, name):
    n = recv.shape[0]
    shape = recv.shape[1:]
    c = shape[-1]
    r3 = recv.reshape(n, -1, c)
    rows = r3.shape[1]
    tb = _tile(rows, max(BF16_ROWS, ROW_BLOCK_BYTES // (4 * c * (n + 1))), BF16_ROWS)

    def body(r_ref, o_ref):
        acc = r_ref[0].astype(F32)
        for s in range(1, n):
            acc = acc + r_ref[s].astype(F32)
        o_ref[...] = acc

    out = _call(body, name=name, grid=(rows // tb,),
                in_specs=[pl.BlockSpec((n, tb, c), lambda i: (0, i, 0))],
                out_specs=pl.BlockSpec((tb, c), lambda i: (i, 0)),
                out_shape=jax.ShapeDtypeStruct((rows, c), F32), sem=("parallel",))(r3)
    return out.reshape(shape)


def _gmlp_pre(zu, zv, b_u, b_v, ln_g, ln_b):
    u = jax.nn.gelu(zu + b_u, approximate=True)
    v = jax.nn.gelu(zv + b_v, approximate=True)
    xc = v - jnp.mean(v, axis=-1, keepdims=True)
    vn = xc * lax.rsqrt(jnp.mean(xc * xc, axis=-1, keepdims=True) + NORM_EPS) * ln_g + ln_b
    return u, vn


def _gmlp_fwd(zp, b_in, ln_g, ln_b, ws, bs_t, name):
    S, gw2 = zp.shape
    gw = gw2 // 2
    G, ch, _ = ws.shape
    gd = gw // G
    tb = 2 * ch

    def body(zp_ref, bin_ref, lg_ref, lb_ref, ws_ref, bs_ref, o_ref):
        u, vn = _gmlp_pre(zp_ref[:, :gw].astype(F32), zp_ref[:, gw:].astype(F32), bin_ref[:, :gw], bin_ref[:, gw:],
                          lg_ref[...], lb_ref[...])
        vnb = vn.astype(BF16)
        for c in range(tb // ch):
            for g in range(G):
                rs, cs = slice(c * ch, (c + 1) * ch), slice(g * gd, (g + 1) * gd)
                vv = jnp.dot(ws_ref[g], vnb[rs, cs], preferred_element_type=F32) + bs_ref[:, g:g + 1]
                o_ref[rs, cs] = (u[rs, cs] * vv).astype(o_ref.dtype)

    full = lambda a: pl.BlockSpec(a.shape, lambda i: (0,) * a.ndim)
    return _call(body, name=name, grid=(S // tb,),
                 in_specs=[pl.BlockSpec((tb, gw2), lambda i: (i, 0)), full(b_in), full(ln_g), full(ln_b), full(ws), full(bs_t)],
                 out_specs=pl.BlockSpec((tb, gw), lambda i: (i, 0)),
                 out_shape=jax.ShapeDtypeStruct((S, gw), BF16), sem=("parallel",))(zp, b_in, ln_g, ln_b, ws, bs_t)


def _gmlp_bwd(dyg, zp, b_in, ln_g, ln_b, ws, ws_t, bs_t, name):
    S, gw2 = zp.shape
    gw = gw2 // 2
    G, ch, _ = ws.shape
    gd = gw // G
    tb = 2 * ch

    def body(dy_ref, zp_ref, bin_ref, lg_ref, lb_ref, ws_ref, wst_ref, bs_ref,
             dzp_ref, dbin_ref, dlg_ref, dlb_ref, dws_ref, dbs_ref, du_sc, dvn_sc):
        (u, vn), vjp = jax.vjp(_gmlp_pre, zp_ref[:, :gw].astype(F32), zp_ref[:, gw:].astype(F32), bin_ref[:, :gw],
                               bin_ref[:, gw:], lg_ref[...], lb_ref[...])
        vnb = vn.astype(BF16)
        first = pl.program_id(0) == 0

        @pl.when(first)
        def _():
            dws_ref[...] = jnp.zeros_like(dws_ref)

        lane = lax.broadcasted_iota(jnp.int32, (ch, G), 1)
        dbs = jnp.zeros((ch, G), F32)
        for g in range(G):
            cs = slice(g * gd, (g + 1) * gd)
            dws_g = jnp.zeros((ch, ch), F32)
            col = jnp.zeros((ch, 1), F32)
            for c in range(tb // ch):
                rs = slice(c * ch, (c + 1) * ch)
                vnp = vnb[rs, cs]
                vv = jnp.dot(ws_ref[g], vnp, preferred_element_type=F32) + bs_ref[:, g:g + 1]
                dy = dy_ref[rs, cs].astype(F32)
                du_sc[rs, cs] = dy * vv
                dvv = dy * u[rs, cs]
                dvvb = dvv.astype(BF16)
                dvn_sc[rs, cs] = jnp.dot(wst_ref[g], dvvb, preferred_element_type=F32)
                dws_g = dws_g + lax.dot_general(dvvb, vnp, (((1,), (1,)), ((), ())), preferred_element_type=F32)
                col = col + jnp.sum(dvv, axis=1, keepdims=True)
            dws_ref[g] += dws_g
            dbs = jnp.where(lane == g, col, dbs)
        dzu, dzv, dbu, dbv, dlg, dlb = vjp((du_sc[...], dvn_sc[...]))
        dzp_ref[:, :gw] = dzu.astype(dzp_ref.dtype)
        dzp_ref[:, gw:] = dzv.astype(dzp_ref.dtype)

        @pl.when(first)
        def _():
            dbin_ref[:, :gw] = dbu
            dbin_ref[:, gw:] = dbv
            dlg_ref[...] = dlg
            dlb_ref[...] = dlb
            dbs_ref[...] = dbs

        @pl.when(jnp.logical_not(first))
        def _():
            dbin_ref[:, :gw] += dbu
            dbin_ref[:, gw:] += dbv
            dlg_ref[...] += dlg
            dlb_ref[...] += dlb
            dbs_ref[...] += dbs

    full = lambda a: pl.BlockSpec(a.shape, lambda i: (0,) * a.ndim)
    fshape = lambda s: pl.BlockSpec(s, lambda i: (0,) * len(s))
    return _call(
        body, name=name, grid=(S // tb,),
        in_specs=[pl.BlockSpec((tb, gw), lambda i: (i, 0)), pl.BlockSpec((tb, gw2), lambda i: (i, 0)),
                  full(b_in), full(ln_g), full(ln_b), full(ws), full(ws_t), full(bs_t)],
        out_specs=[pl.BlockSpec((tb, gw2), lambda i: (i, 0)), fshape((1, gw2)), fshape((1, gw)), fshape((1, gw)),
                   fshape((G, ch, ch)), fshape((ch, G))],
        out_shape=[jax.ShapeDtypeStruct((S, gw2), BF16), jax.ShapeDtypeStruct((1, gw2), F32),
                   jax.ShapeDtypeStruct((1, gw), F32), jax.ShapeDtypeStruct((1, gw), F32),
                   jax.ShapeDtypeStruct((G, ch, ch), F32), jax.ShapeDtypeStruct((ch, G), F32)],
        scratch=[pltpu.VMEM((tb, gw), F32), pltpu.VMEM((tb, gw), F32)],
        sem=("arbitrary",))(dyg, zp, b_in, ln_g, ln_b, ws, ws_t, bs_t)


def _dot_01(x, ones_bf16):
    hi = x.astype(BF16)
    r1 = x - hi.astype(F32)
    mid = r1.astype(BF16)
    lo = (r1 - mid.astype(F32)).astype(BF16)
    dot = lambda t: jnp.dot(t, ones_bf16, preferred_element_type=F32)
    return dot(hi) + dot(mid) + dot(lo)


def _log_sigmoid(x):
    return jnp.minimum(x, 0.0) - jnp.log1p(jnp.exp(-jnp.abs(x)))


def _dcum_fwd(f_t, b_col, name):
    H, S = f_t.shape
    tb = _tile(S, 512, LANES)

    def body(f_ref, b_ref, o_ref, carry):
        @pl.when(pl.program_id(0) == 0)
        def _():
            carry[...] = jnp.zeros_like(carry)

        ls = _log_sigmoid(f_ref[...] + b_ref[...])
        r = lax.broadcasted_iota(jnp.int32, (tb, tb), 0)
        c = lax.broadcasted_iota(jnp.int32, (tb, tb), 1)
        upper = (r <= c).astype(BF16)
        o_ref[...] = _dot_01(ls, upper) + carry[...]
        carry[...] += jnp.sum(ls, axis=1, keepdims=True)

    return _call(body, name=name, grid=(S // tb,),
                 in_specs=[pl.BlockSpec((H, tb), lambda i: (0, i)), pl.BlockSpec((H, 1), lambda i: (0, 0))],
                 out_specs=pl.BlockSpec((H, tb), lambda i: (0, i)),
                 out_shape=jax.ShapeDtypeStruct((H, S), F32),
                 scratch=[pltpu.VMEM((H, 1), F32)], sem=("arbitrary",))(f_t, b_col)


def _dcum_bwd(dd_t, f_t, b_col, name):
    H, S = f_t.shape
    tb = _tile(S, 512, LANES)
    nb = S // tb

    def body(dd_ref, f_ref, b_ref, df_ref, db_ref, carry):
        first = pl.program_id(0) == 0

        @pl.when(first)
        def _():
            carry[...] = jnp.zeros_like(carry)

        dd = dd_ref[...]
        r = lax.broadcasted_iota(jnp.int32, (tb, tb), 0)
        c = lax.broadcasted_iota(jnp.int32, (tb, tb), 1)
        lower = (r >= c).astype(BF16)
        rev = _dot_01(dd, lower) + carry[...]
        carry[...] += jnp.sum(dd, axis=1, keepdims=True)
        df = rev * jax.nn.sigmoid(-(f_ref[...] + b_ref[...]))
        df_ref[...] = df
        part = jnp.sum(df, axis=1, keepdims=True)

        @pl.when(first)
        def _():
            db_ref[...] = part

        @pl.when(jnp.logical_not(first))
        def _():
            db_ref[...] += part

    return _call(body, name=name, grid=(nb,),
                 in_specs=[pl.BlockSpec((H, tb), lambda i: (0, nb - 1 - i)), pl.BlockSpec((H, tb), lambda i: (0, nb - 1 - i)),
                           pl.BlockSpec((H, 1), lambda i: (0, 0))],
                 out_specs=[pl.BlockSpec((H, tb), lambda i: (0, nb - 1 - i)), pl.BlockSpec((H, 1), lambda i: (0, 0))],
                 out_shape=[jax.ShapeDtypeStruct((H, S), F32), jax.ShapeDtypeStruct((H, 1), F32)],
                 scratch=[pltpu.VMEM((H, 1), F32)], sem=("arbitrary",))(dd_t, f_t, b_col)


def _attn_tile(S):
    return _tile(S, 512, LANES)


def _causal_t(t):
    return lax.broadcasted_iota(jnp.int32, (t, t), 0) <= lax.broadcasted_iota(jnp.int32, (t, t), 1)


def _tri_pairs(n, key_major):
    if key_major:
        pairs = [(i, j) for j in range(n) for i in range(j, n)]
    else:
        pairs = [(i, j) for i in range(n) for j in range(i + 1)]
    return jnp.asarray([p[0] for p in pairs], jnp.int32), jnp.asarray([p[1] for p in pairs], jnp.int32)


def _split3(x):
    hi = lax.reduce_precision(x, 8, 7)
    r = x - hi
    mid = lax.reduce_precision(r, 8, 7)
    lo = lax.reduce_precision(r - mid, 8, 7)
    return hi.astype(BF16), mid.astype(BF16), lo.astype(BF16)


def _augment(xn, dcum, query):
    H, S, hd = xn.shape
    parts = list(_split3(dcum))
    vals = parts + [1.0] * 3 if query else [1.0] * 3 + [-p for p in parts]
    lane = lax.broadcasted_iota(jnp.int32, (1, 1, LANES), 2)
    out = jnp.pad(xn, ((0, 0), (0, 0), (0, LANES - hd)))
    for k, val in enumerate(vals):
        val = jnp.asarray(val, BF16)
        out = jnp.where(lane == hd + k, val[..., None] if val.ndim else val, out)
    return out


def _scores_t(k_ref, qt_ref, h, t, diag):
    st = jnp.dot(k_ref[h], qt_ref[h], preferred_element_type=F32)
    return jnp.where(_causal_t(t), st, MASKED) if diag else st


def _flash_fwd(ka, qat, vat, hd, name):
    H, S, da = ka.shape
    t = _attn_tile(S)
    hb = ATTN_HEADS_PER_STEP
    it, jt = _tri_pairs(S // t, False)

    def body(it_ref, jt_ref, k_ref, qt_ref, vt_ref, o_ref, lse_ref, m_sc, acc_sc):
        i, j = it_ref[pl.program_id(1)], jt_ref[pl.program_id(1)]

        @pl.when(j == 0)
        def _():
            m_sc[...] = jnp.full_like(m_sc, MASKED)
            acc_sc[...] = jnp.zeros_like(acc_sc)

        def step(diag):
            sts = [_scores_t(k_ref, qt_ref, h, t, diag) for h in range(hb)]
            pts, alphas = [], []
            for h in range(hb):
                m_prev = m_sc[h]
                m_new = jnp.maximum(m_prev, jnp.max(sts[h], axis=0, keepdims=True))
                pts.append(jnp.exp(sts[h] - m_new).astype(BF16))
                alphas.append(jnp.exp(m_prev - m_new))
                m_sc[h] = m_new
            for h in range(hb):
                acc_sc[h] = alphas[h] * acc_sc[h] + jnp.dot(vt_ref[h], pts[h], preferred_element_type=F32)

        @pl.when(j < i)
        def _():
            step(False)

        @pl.when(j == i)
        def _():
            step(True)
            for h in range(hb):
                l = acc_sc[h, hd:hd + 1, :]
                o_ref[h] = acc_sc[h, :hd, :] / l
                lse_ref[h] = m_sc[h] + jnp.log(l)

    qcol = lambda h, p, it_, jt_: (h, 0, it_[p])
    kcol = lambda h, p, it_, jt_: (h, 0, jt_[p])
    krow = lambda h, p, it_, jt_: (h, jt_[p], 0)
    return _call_prefetch(
        body, name=name, grid=(H // hb, it.shape[0]), n_prefetch=2,
        in_specs=[pl.BlockSpec((hb, t, da), krow), pl.BlockSpec((hb, da, t), qcol), pl.BlockSpec((hb, da, t), kcol)],
        out_specs=[pl.BlockSpec((hb, hd, t), qcol), pl.BlockSpec((hb, 1, t), qcol)],
        out_shape=[jax.ShapeDtypeStruct((H, hd, S), F32), jax.ShapeDtypeStruct((H, 1, S), F32)],
        scratch=[pltpu.VMEM((hb, 1, t), F32), pltpu.VMEM((hb, da, t), F32)],
        sem=("parallel", "arbitrary"))(it, jt, ka, qat, vat)


def _flash_bwd(ka, kat, qat, v, dot, o_tr, lse_r, name):
    H, S, hd = v.shape
    da = ka.shape[2]
    t = _attn_tile(S)
    n = S // t
    hb = ATTN_HEADS_PER_STEP
    it, jt = _tri_pairs(n, True)
    over_queries = (((1,), (1,)), ((), ()))

    def body(it_ref, jt_ref, k_ref, kt_ref, qt_ref, v_ref, dot_ref, o_ref, lse_ref, dq_ref, dk_ref, dv_ref, dk_sc, dv_sc):
        i, j = it_ref[pl.program_id(1)], jt_ref[pl.program_id(1)]

        @pl.when(pl.program_id(1) == 0)
        def _():
            dq_ref[...] = jnp.zeros_like(dq_ref)

        def step(diag):
            for h0 in range(0, hb, ATTN_STAGED_HEADS):
                hs = range(h0, min(h0 + ATTN_STAGED_HEADS, hb))
                sts = [_scores_t(k_ref, qt_ref, h, t, diag) for h in hs]
                dpts = [jnp.dot(v_ref[h], dot_ref[h], preferred_element_type=F32) for h in hs]
                tiles = []
                for h, st, dpt in zip(hs, sts, dpts):
                    dl = jnp.sum(dot_ref[h].astype(F32) * o_ref[h], axis=0, keepdims=True)
                    pt = jnp.exp(st - lse_ref[h])
                    tiles.append((pt.astype(BF16), (pt * (dpt - dl)).astype(BF16)))
                for h, (ptb, dsb) in zip(hs, tiles):
                    dv_sc[h] += lax.dot_general(dot_ref[h], ptb, over_queries, preferred_element_type=F32)
                    dk_sc[h] += lax.dot_general(qt_ref[h], dsb, over_queries, preferred_element_type=F32)
                    dq_ref[h, i] += jnp.dot(kt_ref[h], dsb, preferred_element_type=F32)

        @pl.when(i == j)
        def _():
            dk_sc[...] = jnp.zeros_like(dk_sc)
            dv_sc[...] = jnp.zeros_like(dv_sc)
            step(True)

        @pl.when(i > j)
        def _():
            step(False)

        @pl.when(i == n - 1)
        def _():
            dk_ref[...] = dk_sc[...]
            dv_ref[...] = dv_sc[...]

    krow = lambda h, p, it_, jt_: (h, jt_[p], 0)
    kcol = lambda h, p, it_, jt_: (h, 0, jt_[p])
    qcol = lambda h, p, it_, jt_: (h, 0, it_[p])
    return _call_prefetch(
        body, name=name, grid=(H // hb, it.shape[0]), n_prefetch=2,
        in_specs=[pl.BlockSpec((hb, t, da), krow), pl.BlockSpec((hb, da, t), kcol), pl.BlockSpec((hb, da, t), qcol),
                  pl.BlockSpec((hb, t, hd), krow), pl.BlockSpec((hb, hd, t), qcol), pl.BlockSpec((hb, hd, t), qcol),
                  pl.BlockSpec((hb, 1, t), qcol)],
        out_specs=[pl.BlockSpec((hb, n, da, t), lambda h, p, it_, jt_: (h, 0, 0, 0)), pl.BlockSpec((hb, da, t), kcol),
                   pl.BlockSpec((hb, hd, t), kcol)],
        out_shape=[jax.ShapeDtypeStruct((H, n, da, t), F32), jax.ShapeDtypeStruct((H, da, S), F32),
                   jax.ShapeDtypeStruct((H, hd, S), F32)],
        scratch=[pltpu.VMEM((hb, da, t), F32), pltpu.VMEM((hb, hd, t), F32)],
        sem=("parallel", "arbitrary"))(it, jt, ka, kat, qat, v, dot, o_tr, lse_r)


def _offsets(n_bits):
    return [tuple((k >> b) & 1 for b in reversed(range(n_bits))) for k in range(1, 1 << n_bits)]


def _own_slot(out, own, idx):
    return lax.dynamic_update_index_in_dim(out, own.astype(out.dtype), idx, 0)


def _gather8(arrs, name):
    n = len(arrs)
    offs = _offsets(3)

    def body(*refs):
        ins, outs = refs[:n], refs[n:2 * n]
        ssem, rsem = refs[2 * n:]
        x, y, c = _place()
        me = 4 * x + 2 * y + c
        copies = []
        for a in range(n):
            for k, (dx, dy, dcc) in enumerate(offs):
                cp = pltpu.make_async_remote_copy(
                    src_ref=ins[a], dst_ref=outs[a].at[me], send_sem=ssem.at[a, k], recv_sem=rsem.at[a, k],
                    device_id=((x + dx) % 2, (y + dy) % 2, (c + dcc) % 2), device_id_type=MESH)
                cp.start()
                copies.append(cp)
        for cp in copies:
            cp.wait()

    return _call(body, name=name, in_specs=[ANY] * n, out_specs=[ANY] * n,
                 out_shape=[jax.ShapeDtypeStruct((N_DEV,) + a.shape, a.dtype) for a in arrs],
                 scratch=[pltpu.SemaphoreType.DMA((n, 7)), pltpu.SemaphoreType.DMA((n, 7))])(*arrs)


def _chip_position(chip, place):
    return sum(jnp.where(chip == s, place[1].index(s), 0) for s in range(N_CHIPS))


def _shard_slot(ref, shard_shape, place, chip, c=None):
    hn = shard_shape[0] // 2
    half = slice(None) if c is None else pl.ds(c * hn, hn)
    if place is None:
        return ref.at[chip, half]
    ax = place[0]
    w = shard_shape[ax]
    idx = [slice(None)] * len(shard_shape)
    idx[0] = half
    idx[ax] = pl.ds(pl.multiple_of(_chip_position(chip, place) * w, LANES if ax == len(shard_shape) - 1 else BF16_ROWS), w)
    return ref.at[tuple(idx)]


def _gathered_shape(shard_shape, place):
    if place is None:
        return (N_CHIPS,) + tuple(shard_shape)
    s = list(shard_shape)
    s[place[0]] *= N_CHIPS
    return tuple(s)


def _chip_gather(arrs, halved, name, places=None):
    n = len(arrs)
    offs = _offsets(2)
    places = places or [None] * n

    def body(*refs):
        ins, outs = refs[:n], refs[n:2 * n]
        ssem, rsem = refs[2 * n:]
        x, y, c = _place()
        chip = 2 * x + y
        copies = []
        for a in range(n):
            if halved:
                hn = arrs[a].shape[0] // 2
                src = ins[a].at[pl.ds(c * hn, hn)]
                dst = _shard_slot(outs[a], arrs[a].shape, places[a], chip, c)
            else:
                src, dst = ins[a], outs[a].at[chip]
            for k, (dx, dy) in enumerate(offs):
                cp = pltpu.make_async_remote_copy(
                    src_ref=src, dst_ref=dst, send_sem=ssem.at[a, k], recv_sem=rsem.at[a, k],
                    device_id=((x + dx) % 2, (y + dy) % 2, c), device_id_type=MESH)
                cp.start()
                copies.append(cp)
        for cp in copies:
            cp.wait()

    return _call(body, name=name, in_specs=[ANY] * n, out_specs=[ANY] * n,
                 out_shape=[jax.ShapeDtypeStruct(_gathered_shape(a.shape, p) if halved else (N_CHIPS,) + a.shape, a.dtype)
                            for a, p in zip(arrs, places)],
                 scratch=[pltpu.SemaphoreType.DMA((n, 3)), pltpu.SemaphoreType.DMA((n, 3))])(*arrs)


def _sibling_fill(bufs, owns, places, name):
    n = len(bufs)
    offs = _offsets(2)

    def body(*refs):
        ins, own, outs = refs[:n], refs[n:2 * n], refs[2 * n:3 * n]
        ssem, rsem = refs[3 * n:]
        x, y, c = _place()
        copies = []
        for a in range(n):
            shape = owns[a].shape
            for k, (dx, dy) in enumerate(offs):
                chip = 2 * ((x + dx) % 2) + (y + dy) % 2
                cp = pltpu.make_async_remote_copy(
                    src_ref=_shard_slot(ins[a], shape, places[a], chip, c),
                    dst_ref=_shard_slot(outs[a], shape, places[a], chip, c),
                    send_sem=ssem.at[a, k], recv_sem=rsem.at[a, k],
                    device_id=(x, y, 1 - c), device_id_type=MESH)
                cp.start()
                copies.append(cp)
            cp = pltpu.make_async_remote_copy(
                src_ref=own[a], dst_ref=_shard_slot(outs[a], shape, places[a], 2 * x + y),
                send_sem=ssem.at[a, 3], recv_sem=rsem.at[a, 3], device_id=(x, y, 1 - c), device_id_type=MESH)
            cp.start()
            copies.append(cp)
        for cp in copies:
            cp.wait()

    return _call(body, name=name, in_specs=[ANY] * (2 * n), out_specs=[ANY] * n,
                 out_shape=[jax.ShapeDtypeStruct(b.shape, b.dtype) for b in bufs],
                 scratch=[pltpu.SemaphoreType.DMA((n, 4)), pltpu.SemaphoreType.DMA((n, 4))],
                 aliases={a: a for a in range(n)})(*bufs, *owns)


def _sibling_pair(arrs, name):
    n = len(arrs)

    def body(*refs):
        ins, outs = refs[:n], refs[n:2 * n]
        ssem, rsem = refs[2 * n:]
        x, y, c = _place()
        copies = []
        for a in range(n):
            cp = pltpu.make_async_remote_copy(
                src_ref=ins[a], dst_ref=outs[a].at[c], send_sem=ssem.at[a], recv_sem=rsem.at[a],
                device_id=(x, y, 1 - c), device_id_type=MESH)
            cp.start()
            copies.append(cp)
        for cp in copies:
            cp.wait()

    return _call(body, name=name, in_specs=[ANY] * n, out_specs=[ANY] * n,
                 out_shape=[jax.ShapeDtypeStruct((N_CORES,) + a.shape, a.dtype) for a in arrs],
                 scratch=[pltpu.SemaphoreType.DMA((n,)), pltpu.SemaphoreType.DMA((n,))])(*arrs)


CHIP_ORDER = (0, 1, 2, 3)


def _piece(shape, spec, j, h):
    shard_ax, half_ax, order = spec
    w = shape[shard_ax] // N_CHIPS
    idx = [slice(None)] * len(shape)
    idx[shard_ax] = pl.ds(order[j] * w, w)
    assert half_ax != shard_ax
    hn = shape[half_ax] // 2
    idx[half_ax] = pl.ds(h * hn, hn)
    return tuple(idx)


def _piece_shape(shape, spec):
    shard_ax, half_ax, _ = spec
    s = list(shape)
    s[shard_ax] //= N_CHIPS
    s[half_ax] //= 2
    return tuple(s)


def _own_pieces(g, spec, c):
    shard_ax, half_ax, order = spec
    hn = g.shape[half_ax] // 2
    half = lax.dynamic_slice_in_dim(g, c * hn, hn, axis=half_ax)
    w = half.shape[shard_ax] // N_CHIPS
    return jnp.stack([lax.slice_in_dim(half, p * w, (p + 1) * w, axis=shard_ax) for p in order])


def _sibling_scatter(arrs, specs, name):
    n = len(arrs)

    def body(*refs):
        ins, outs = refs[:n], refs[n:2 * n]
        ssem, rsem = refs[2 * n:]
        x, y, c = _place()
        for mine in range(N_CORES):
            @pl.when(c == mine)
            def _():
                copies = []
                for a in range(n):
                    for j in range(N_CHIPS):
                        cp = pltpu.make_async_remote_copy(
                            src_ref=ins[a].at[_piece(arrs[a].shape, specs[a], j, 1 - mine)], dst_ref=outs[a].at[j],
                            send_sem=ssem.at[a, j], recv_sem=rsem.at[a, j],
                            device_id=(x, y, 1 - mine), device_id_type=MESH)
                        cp.start()
                        copies.append(cp)
                for cp in copies:
                    cp.wait()

    return _call(body, name=name, in_specs=[ANY] * n, out_specs=[ANY] * n,
                 out_shape=[jax.ShapeDtypeStruct((N_CHIPS,) + _piece_shape(a.shape, s), a.dtype)
                            for a, s in zip(arrs, specs)],
                 scratch=[pltpu.SemaphoreType.DMA((n, N_CHIPS))] * 2)(*arrs)


def _chip_scatter(arrs, name):
    n = len(arrs)
    offs = _offsets(2)

    def body(*refs):
        ins, outs = refs[:n], refs[n:2 * n]
        ssem, rsem = refs[2 * n:]
        x, y, c = _place()
        chip = 2 * x + y
        copies = []
        for a in range(n):
            for k, (dx, dy) in enumerate(offs):
                tx, ty = (x + dx) % 2, (y + dy) % 2
                cp = pltpu.make_async_remote_copy(
                    src_ref=ins[a].at[2 * tx + ty], dst_ref=outs[a].at[chip], send_sem=ssem.at[a, k], recv_sem=rsem.at[a, k],
                    device_id=(tx, ty, c), device_id_type=MESH)
                cp.start()
                copies.append(cp)
        for cp in copies:
            cp.wait()

    return _call(body, name=name, in_specs=[ANY] * n, out_specs=[ANY] * n,
                 out_shape=[jax.ShapeDtypeStruct(a.shape, a.dtype) for a in arrs],
                 scratch=[pltpu.SemaphoreType.DMA((n, 3)), pltpu.SemaphoreType.DMA((n, 3))])(*arrs)


def kernel(x, c, ada_w, ada_b, pre_mix_g, post_mix_g, pre_ffn_g, post_ffn_g, ffn_w_gu, ffn_w_down, a_w_in, a_b_in, a_ln_g, a_ln_b, a_w_s, a_b_s, a_w_out, kv_ada_w, kv_ada_b, kv_norm_g, kv_w, kv_b_f, k_norm_g, b_w_qg, b_q_norm_g, b_w_o, loss_target, m_ada_w, m_ada_b, m_pre_mix_g, m_post_mix_g, m_pre_ffn_g, m_post_ffn_g, m_ffn_w_gu, m_ffn_w_down, m_a_w_in, m_a_b_in, m_a_ln_g, m_a_ln_b, m_a_w_s, m_a_b_s, m_a_w_out, m_kv_ada_w, m_kv_ada_b, m_kv_norm_g, m_kv_w, m_kv_b_f, m_k_norm_g, m_b_w_qg, m_b_q_norm_g, m_b_w_o, v_ada_w, v_ada_b, v_pre_mix_g, v_post_mix_g, v_pre_ffn_g, v_post_ffn_g, v_ffn_w_gu, v_ffn_w_down, v_a_w_in, v_a_b_in, v_a_ln_g, v_a_ln_b, v_a_w_s, v_a_b_s, v_a_w_out, v_kv_ada_w, v_kv_ada_b, v_kv_norm_g, v_kv_w, v_kv_b_f, v_k_norm_g, v_b_w_qg, v_b_q_norm_g, v_b_w_o):
    weights = dict(ada_w=ada_w, ada_b=ada_b, pre_mix_g=pre_mix_g, post_mix_g=post_mix_g, pre_ffn_g=pre_ffn_g,
                   post_ffn_g=post_ffn_g, ffn_w_gu=ffn_w_gu, ffn_w_down=ffn_w_down, a_w_in=a_w_in, a_b_in=a_b_in,
                   a_ln_g=a_ln_g, a_ln_b=a_ln_b, a_w_s=a_w_s, a_b_s=a_b_s, a_w_out=a_w_out, kv_ada_w=kv_ada_w,
                   kv_ada_b=kv_ada_b, kv_norm_g=kv_norm_g, kv_w=kv_w, kv_b_f=kv_b_f, k_norm_g=k_norm_g, b_w_qg=b_w_qg,
                   b_q_norm_g=b_q_norm_g, b_w_o=b_w_o)
    m_in = dict(ada_w=m_ada_w, ada_b=m_ada_b, pre_mix_g=m_pre_mix_g, post_mix_g=m_post_mix_g, pre_ffn_g=m_pre_ffn_g,
                post_ffn_g=m_post_ffn_g, ffn_w_gu=m_ffn_w_gu, ffn_w_down=m_ffn_w_down, a_w_in=m_a_w_in, a_b_in=m_a_b_in,
                a_ln_g=m_a_ln_g, a_ln_b=m_a_ln_b, a_w_s=m_a_w_s, a_b_s=m_a_b_s, a_w_out=m_a_w_out, kv_ada_w=m_kv_ada_w,
                kv_ada_b=m_kv_ada_b, kv_norm_g=m_kv_norm_g, kv_w=m_kv_w, kv_b_f=m_kv_b_f, k_norm_g=m_k_norm_g,
                b_w_qg=m_b_w_qg, b_q_norm_g=m_b_q_norm_g, b_w_o=m_b_w_o)
    v_in = dict(ada_w=v_ada_w, ada_b=v_ada_b, pre_mix_g=v_pre_mix_g, post_mix_g=v_post_mix_g, pre_ffn_g=v_pre_ffn_g,
                post_ffn_g=v_post_ffn_g, ffn_w_gu=v_ffn_w_gu, ffn_w_down=v_ffn_w_down, a_w_in=v_a_w_in, a_b_in=v_a_b_in,
                a_ln_g=v_a_ln_g, a_ln_b=v_a_ln_b, a_w_s=v_a_w_s, a_b_s=v_a_b_s, a_w_out=v_a_w_out, kv_ada_w=v_kv_ada_w,
                kv_ada_b=v_kv_ada_b, kv_norm_g=v_kv_norm_g, kv_w=v_kv_w, kv_b_f=v_kv_b_f, k_norm_g=v_k_norm_g,
                b_w_qg=v_b_w_qg, b_q_norm_g=v_b_q_norm_g, b_w_o=v_b_w_o)
    names = list(weights)

    S, D = x.shape[1], x.shape[2]
    L, NA, NB = ada_w.shape[0], a_w_in.shape[0], b_w_qg.shape[0]
    H = kv_b_f.shape[0]
    hd = D // H
    G, CH = a_w_s.shape[1], a_w_s.shape[2]
    GW = a_w_out.shape[1] * N_CHIPS
    F = ffn_w_down.shape[1] * N_CHIPS
    ada_cols = ada_w.shape[2]
    kvada_cols = kv_ada_w.shape[1]
    kv_cols = kv_w.shape[1]
    kv_pad = -(-(2 * D + H) // LANES) * LANES
    xi, yi, ci = _place()
    chip = 2 * xi + yi
    me = 2 * chip + ci
    x0 = x[0]
    tgt = loss_target[0]
    row = lambda t: t.reshape(1, -1)

    c_all = _own_slot(_gather8([c], "gather_c")[0], c, me).reshape(N_DEV, D)
    c_act = _silu_rows(jnp.pad(c_all, ((0, BF16_ROWS - N_DEV), (0, 0))), "silu_c")
    mod_sh = [_mm(c_act, (ada_w, l), "nn", F32, f"mod_proj_{l}") for l in range(L)]
    mod_sh.append(_mm(c_act, kv_ada_w, "nn", F32, "mod_proj_kv"))
    mod_sh = jnp.concatenate(mod_sh, axis=1)
    small_sh = [mod_sh, a_b_in, a_ln_g, a_ln_b]
    mod_all, b_in_all, ln_g_all, ln_b_all = [
        _own_slot(o, s, chip) for o, s in zip(_chip_gather(small_sh, False, "gather_mod"), small_sh)]
    mine = lax.dynamic_index_in_dim(mod_all, me, axis=1, keepdims=False)
    mod = [jnp.concatenate([mine[j, l * ada_cols:(l + 1) * ada_cols] for j in range(N_CHIPS)]) + ada_b[l] for l in range(L)]
    mod = [[row(t) for t in jnp.split(m_, 6)] for m_ in mod]
    mod_kv = jnp.concatenate([mine[j, L * ada_cols:] for j in range(N_CHIPS)]) + kv_ada_b
    kv_sh, kv_sc = [row(t) for t in jnp.split(mod_kv, 2)]
    cat_chips = lambda t, ax: jnp.concatenate([t[j] for j in range(N_CHIPS)], axis=ax)
    b_in_f = cat_chips(b_in_all, 1)
    ln_g_f, ln_b_f = cat_chips(ln_g_all, 1), cat_chips(ln_b_all, 1)

    big = ["ffn_w_gu", "ffn_w_down", "a_w_in", "a_w_out", "kv_w", "b_w_qg", "b_w_o"]
    own_w = [weights[n].astype(BF16) for n in big]
    gu_order = (0, 2, 1, 3)
    places = {"ffn_w_gu": (2, gu_order), "ffn_w_down": (1, CHIP_ORDER), "a_w_in": (2, CHIP_ORDER),
              "a_w_out": (1, CHIP_ORDER), "kv_w": None, "b_w_qg": (2, CHIP_ORDER), "b_w_o": (1, CHIP_ORDER)}
    plist = [places[n] for n in big]
    full_w = dict(zip(big, _sibling_fill(_chip_gather(own_w, True, "gather_w", plist), own_w, plist, "fill_w")))
    gu_hw = ffn_w_gu.shape[2]
    w_gu, w_dn, w_in, w_out = full_w["ffn_w_gu"], full_w["ffn_w_down"], full_w["a_w_in"], full_w["a_w_out"]
    w_qg, w_o = full_w["b_w_qg"], full_w["b_w_o"]
    w_kv = jnp.pad(cat_chips(full_w["kv_w"], 1), ((0, 0), (0, kv_pad - (2 * D + H))))

    causal = jnp.tril(jnp.ones((CH, CH), F32))
    ws_m = [(a_w_s[i] * causal).astype(BF16) for i in range(NA)]
    ws_mt = [jnp.swapaxes(w, 1, 2) for w in ws_m]
    bs_t = [a_b_s[i].T for i in range(NA)]

    heads = lambda t: t.reshape(S, H, hd).transpose(1, 0, 2)
    unheads = lambda t: t.transpose(1, 0, 2).reshape(S, D)

    saved = []
    kv = None
    xc = x0
    h1 = _norm_mod_fwd(xc, row(pre_mix_g[0]), mod[0][0], mod[0][1], "pre_mix_0")
    for l in range(L):
        sh_m, sc_m, g_m, sh_f, sc_f, g_f = mod[l]
        st = {"x0": xc, "h1": h1}
        if l < NA:
            zp = _mm(h1, (w_in, l), "nn", BF16, f"gmlp_in_{l}")
            yg = _gmlp_fwd(zp, row(b_in_f[l]), row(ln_g_f[l]), row(ln_b_f[l]), ws_m[l], bs_t[l], f"gmlp_gate_{l}")
            y = _mm(yg, (w_out, l), "nn", F32, f"gmlp_out_{l}")
            st.update(zp=zp, yg=yg)
        else:
            jb = l - NA
            qg = _mm(h1, (w_qg, jb), "nn", BF16, f"fox_qg_{jb}")
            q_raw = heads(qg[:, :D]).reshape(H * S, hd)
            qn = _head_norm_fwd(q_raw, row(b_q_norm_g[jb]), hd ** -0.5, f"fox_qnorm_{jb}").reshape(H, S, hd)
            qa = _augment(qn, kv["dcum"], True)
            qat = jnp.swapaxes(qa, 1, 2)
            o_tr, lse_r = _flash_fwd(kv["ka"], qat, kv["vat"], hd, f"fox_attn_{jb}")
            o_t = o_tr.transpose(2, 0, 1).reshape(S, D)
            og = _out_gate_fwd(o_t, qg, f"fox_gate_{jb}")
            y = _mm(og, (w_o, jb), "nn", F32, f"fox_out_{jb}")
            st.update(qg=qg, q_raw=q_raw, qat=qat, o_tr=o_tr, lse_r=lse_r, o_t=o_t, og=og)
        st["y"] = y
        x1, h2 = _post_pre_fwd(xc, y, row(post_mix_g[l]), g_m, row(pre_ffn_g[l]), sh_f, sc_f, f"post_mix_{l}")
        st["x1"] = x1
        gu, act = _ffn_up(h2, w_gu, l, gu_hw, f"ffn_gu_{l}")
        y2 = _mm(act, (w_dn, l), "nn", F32, f"ffn_down_{l}")
        if l + 1 < L:
            xc, h1 = _post_pre_fwd(x1, y2, row(post_ffn_g[l]), g_f, row(pre_mix_g[l + 1]), mod[l + 1][0], mod[l + 1][1],
                                   f"post_ffn_{l}")
        else:
            xc = _post_fwd(x1, y2, row(post_ffn_g[l]), g_f, f"post_ffn_{l}")
        st.update(h2=h2, gu=gu, act=act, y2=y2)
        saved.append(st)
        if l == NA - 1:
            hk = _norm_mod_fwd(xc, row(kv_norm_g), kv_sh, kv_sc, "kv_pre")
            kvf = _mm(hk, w_kv, "nn", F32, "kv_proj")
            k_raw = heads(kvf[:, :D]).reshape(H * S, hd)
            kn = _head_norm_fwd(k_raw, row(k_norm_g), 1.0, "kv_knorm").reshape(H, S, hd)
            vb = heads(kvf[:, D:2 * D]).astype(BF16)
            f_t = kvf[:, 2 * D:2 * D + H].T
            b_col = kv_b_f.reshape(H, 1)
            dcum = _dcum_fwd(f_t, b_col, "kv_dcum")
            vt = kvf[:, D:2 * D].astype(BF16).reshape(S, H, hd).transpose(1, 2, 0)
            vat = jnp.where(lax.broadcasted_iota(jnp.int32, (1, LANES, 1), 1) == hd, jnp.asarray(1, BF16),
                            jnp.pad(vt, ((0, 0), (0, LANES - hd), (0, 0))))
            ka = _augment(kn, dcum, False)
            kv = dict(x=xc, hk=hk, k_raw=k_raw, ka=ka, kat=jnp.swapaxes(ka, 1, 2), vb=vb, vat=vat,
                      f_t=f_t, b_col=b_col, dcum=dcum)

    dx, loss_part = _loss_bwd(xc, tgt, "loss")

    gl = {n: [None] * weights[n].shape[0] for n in
          ["pre_mix_g", "post_mix_g", "pre_ffn_g", "post_ffn_g", "ffn_w_gu", "ffn_w_down", "a_w_in", "a_b_in", "a_ln_g",
           "a_ln_b", "a_w_s", "a_b_s", "a_w_out", "b_w_qg", "b_q_norm_g", "b_w_o"]}
    dmod = [None] * L
    dkn = dvb = ddc = None
    gkv = {}
    for l in reversed(range(L)):
        st = saved[l]
        sh_m, sc_m, g_m, sh_f, sc_f, g_f = mod[l]
        if l == NA - 1:
            dk_raw, gkv["k_norm_g"] = _head_norm_bwd(jnp.swapaxes(dkn, 1, 2).reshape(H * S, hd), kv["k_raw"], row(k_norm_g),
                                                     1.0, "kv_knorm_bwd")
            df_t, db_f = _dcum_bwd(ddc.reshape(H, S), kv["f_t"], kv["b_col"], "kv_dcum_bwd")
            dkvf = jnp.concatenate([unheads(dk_raw.reshape(H, S, hd)), dvb.transpose(2, 0, 1).reshape(S, D), df_t.T,
                                    jnp.zeros((S, kv_pad - (2 * D + H)), F32)], axis=1).astype(BF16)
            gkv["kv_w"] = _mm(kv["hk"], dkvf, "tn", BF16, "kv_proj_dw")[:, :2 * D + H]
            dhk = _mm(dkvf, w_kv, "nt", F32, "kv_proj_dx")
            dx, gkv["kv_norm_g"], dsh, dsc = _norm_mod_bwd(dx, dhk, kv["x"], row(kv_norm_g), kv_sh, kv_sc, "kv_pre_bwd")
            gkv["kv_b_f"] = db_f.reshape(H)
            dmod_kv = jnp.concatenate([dsh, dsc], axis=1)
        dy2, gl["post_ffn_g"][l], dg_f = _post_bwd(dx, st["y2"], row(post_ffn_g[l]), g_f, f"post_ffn_bwd_{l}")
        gl["ffn_w_down"][l] = _mm(st["act"], dy2, "tn", BF16, f"ffn_down_dw_{l}")
        dgu = _ffn_down_dx(dy2, w_dn, l, st["gu"], gu_hw, f"ffn_down_dx_{l}")
        gl["ffn_w_gu"][l] = _mm(st["h2"], dgu, "tn", BF16, f"ffn_gu_dw_{l}")
        dh2 = _mm(dgu, (w_gu, l), "nt", F32, f"ffn_gu_dx_{l}")
        dx, dy, gl["pre_ffn_g"][l], dsh_f, dsc_f, gl["post_mix_g"][l], dg_m = _pre_post_bwd(
            dx, dh2, st["x1"], row(pre_ffn_g[l]), sh_f, sc_f, st["y"], row(post_mix_g[l]), g_m, f"pre_ffn_bwd_{l}")
        if l < NA:
            gl["a_w_out"][l] = _mm(st["yg"], dy, "tn", BF16, f"gmlp_out_dw_{l}")
            dyg = _mm(dy, (w_out, l), "nt", BF16, f"gmlp_out_dx_{l}")
            dzp, db_in, dlg, dlb, dws, dbs_t = _gmlp_bwd(dyg, st["zp"], row(b_in_f[l]), row(ln_g_f[l]), row(ln_b_f[l]),
                                                           ws_m[l], ws_mt[l], bs_t[l], f"gmlp_gate_bwd_{l}")
            gl["a_b_in"][l], gl["a_ln_g"][l], gl["a_ln_b"][l] = db_in[0], dlg[0], dlb[0]
            gl["a_w_s"][l], gl["a_b_s"][l] = dws * causal, dbs_t.T
            gl["a_w_in"][l] = _mm(st["h1"], dzp, "tn", BF16, f"gmlp_in_dw_{l}")
            dh1 = _mm(dzp, (w_in, l), "nt", F32, f"gmlp_in_dx_{l}")
        else:
            jb = l - NA
            gl["b_w_o"][jb] = _mm(st["og"], dy, "tn", BF16, f"fox_out_dw_{jb}")
            dog = _mm(dy, (w_o, jb), "nt", F32, f"fox_out_dx_{jb}")
            do_t, dgl = _out_gate_bwd(dog, st["o_t"], st["qg"], f"fox_gate_bwd_{jb}")
            dot = do_t.reshape(S, H, hd).transpose(1, 2, 0)
            dqa_tr, dka_tr, dv_j = _flash_bwd(kv["ka"], kv["kat"], st["qat"], kv["vb"], dot, st["o_tr"], st["lse_r"],
                                              f"fox_attn_bwd_{jb}")
            dqn = dqa_tr[:, :, :hd, :].transpose(0, 1, 3, 2).reshape(H, S, hd)
            dk_j = dka_tr[:, :hd, :]
            dd_j = dqa_tr[:, :, hd, :].reshape(H, S) - dka_tr[:, hd + 3, :]
            dkn = dk_j if dkn is None else dkn + dk_j
            dvb = dv_j if dvb is None else dvb + dv_j
            ddc = dd_j if ddc is None else ddc + dd_j
            dq_raw, dgq = _head_norm_bwd(dqn.reshape(H * S, hd), st["q_raw"], row(b_q_norm_g[jb]), hd ** -0.5, f"fox_qnorm_bwd_{jb}")
            gl["b_q_norm_g"][jb] = dgq[0]
            dqg = jnp.concatenate([unheads(dq_raw.reshape(H, S, hd)).astype(BF16), dgl], axis=1)
            gl["b_w_qg"][jb] = _mm(st["h1"], dqg, "tn", BF16, f"fox_qg_dw_{jb}")
            dh1 = _mm(dqg, (w_qg, jb), "nt", F32, f"fox_qg_dx_{jb}")
        dx, gl["pre_mix_g"][l], dsh_m, dsc_m = _norm_mod_bwd(dx, dh1, st["x0"], row(pre_mix_g[l]), sh_m, sc_m, f"pre_mix_bwd_{l}")
        dmod[l] = jnp.concatenate([dsh_m, dsc_m, dg_m, dsh_f, dsc_f, dg_f], axis=1)
    grad_x = dx[None]

    stack = lambda n: jnp.stack([t.reshape(weights[n].shape[1:]) for t in gl[n]])
    small = {"dmod": jnp.concatenate(dmod, axis=1), "dmod_kv": dmod_kv}
    for n in ["pre_mix_g", "post_mix_g", "pre_ffn_g", "post_ffn_g", "a_w_s", "a_b_s", "b_q_norm_g"]:
        small[n] = stack(n)
    for n in ["a_b_in", "a_ln_g", "a_ln_b"]:
        small[n] = jnp.stack(gl[n])
    for n in ["kv_norm_g", "kv_b_f", "k_norm_g"]:
        small[n] = gkv[n]
    small["loss"] = loss_part
    sizes = {n: t.size for n, t in small.items()}
    flat = jnp.concatenate([t.reshape(-1).astype(F32) for t in small.values()])
    rows_small = -(-flat.size // (LANES * BF16_ROWS)) * BF16_ROWS
    flat = jnp.pad(flat, (0, rows_small * LANES - flat.size)).reshape(rows_small, LANES)
    flat_all = _own_slot(_gather8([flat], "gather_small")[0], flat, me)
    flat_sum = _sum_slots(flat_all, "sum_small").reshape(-1)
    offs, o_ = {}, 0
    for n, sz in sizes.items():
        offs[n] = o_
        o_ += sz
    take = lambda n, shape: flat_sum[offs[n]:offs[n] + sizes[n]].reshape(shape)
    dmod_rows = flat_all.reshape(N_DEV, -1)[:, offs["dmod"]:offs["dmod"] + sizes["dmod"] + sizes["dmod_kv"]]
    dmod_rows = jnp.pad(dmod_rows, ((0, BF16_ROWS - N_DEV), (0, 0)))

    grads = {}
    loss = take("loss", ())
    grads["ada_b"] = take("dmod", (L, 6 * D))
    grads["kv_ada_b"] = take("dmod_kv", (2 * D,))
    for n in ["pre_mix_g", "post_mix_g", "pre_ffn_g", "post_ffn_g", "a_w_s", "a_b_s", "b_q_norm_g", "kv_norm_g", "kv_b_f", "k_norm_g"]:
        grads[n] = take(n, weights[n].shape)
    for n in ["a_b_in", "a_ln_g", "a_ln_b"]:
        full = take(n, small[n].shape)
        w = weights[n].shape[1]
        grads[n] = lax.dynamic_slice_in_dim(full, chip * w, w, axis=1)
    ada_g = []
    for l in range(L):
        cols = lax.dynamic_slice_in_dim(dmod_rows[:, l * 6 * D:(l + 1) * 6 * D], chip * ada_cols, ada_cols, axis=1)
        ada_g.append(_mm(c_act, cols, "tn", F32, f"mod_proj_dw_{l}"))
    grads["ada_w"] = jnp.stack(ada_g)
    cols = lax.dynamic_slice_in_dim(dmod_rows[:, L * 6 * D:], chip * kvada_cols, kvada_cols, axis=1)
    grads["kv_ada_w"] = _mm(c_act, cols, "tn", F32, "mod_proj_kv_dw")

    specs = {"ffn_w_gu": (2, 0, (0, 2, 1, 3)),
             "ffn_w_down": (1, 0, CHIP_ORDER), "a_w_in": (2, 0, CHIP_ORDER), "a_w_out": (1, 0, CHIP_ORDER),
             "kv_w": (0, 1, CHIP_ORDER), "b_w_qg": (2, 0, CHIP_ORDER), "b_w_o": (1, 0, CHIP_ORDER)}
    full_g = {n: jnp.stack(gl[n]) for n in big if n != "kv_w"}
    full_g["kv_w"] = gkv["kv_w"].reshape(D, N_CHIPS, kv_cols).transpose(1, 0, 2)
    from_core = _sibling_scatter([full_g[n] for n in big], [specs[n] for n in big], "scatter_g_core")
    chip_sums = [_sum_pair(_own_pieces(full_g[n], specs[n], ci), r, f"sum_g_core_{n}") for n, r in zip(big, from_core)]
    recv = _chip_scatter(chip_sums, "scatter_g_chip")
    recv = [_own_slot(r, lax.dynamic_index_in_dim(p, chip, 0, keepdims=False), chip) for r, p in zip(recv, chip_sums)]
    halves = [_sum_slots(r, f"sum_g_{n}") for n, r in zip(big, recv)]
    pairs = _sibling_pair(halves, "pair_g")
    for n, p, hlf in zip(big, pairs, halves):
        grads[n] = _own_slot(p, hlf, ci).reshape(weights[n].shape)

    outs_d, outs_m, outs_v = {}, {}, {}
    for n in names:
        w2 = weights[n] if weights[n].ndim > 1 else weights[n].reshape(1, -1)
        shp = w2.shape
        d_, m_, v_ = _adamw(w2, grads[n].reshape(shp), m_in[n].reshape(shp), v_in[n].reshape(shp), f"adamw_{n}")
        outs_d[n], outs_m[n], outs_v[n] = (t.reshape(weights[n].shape) for t in (d_, m_, v_))
    return (loss, grad_x, *[grads[n] for n in names], *[outs_d[n] for n in names],
            *[outs_m[n] for n in names], *[outs_v[n] for n in names])
```

```python
import jax
import jax.numpy as jnp
from jax import lax
from jax.experimental import pallas as pl
from jax.experimental.pallas import tpu as pltpu

F32 = jnp.float32
BF16 = jnp.bfloat16
MESH = pl.DeviceIdType.MESH
NORM_EPS = 1e-6
MASKED = -1e30
LANES = 128
BF16_ROWS = 16
ROW_BLOCK_BYTES = 12 << 20
ADAM_LR, ADAM_B1, ADAM_B2, ADAM_EPS, ADAM_WD, ADAM_STEP = 0.001, 0.9, 0.999, 1e-08, 0.01, 10
N_CHIPS, N_CORES, N_DEV = 4, 2, 8
ATTN_HEADS_PER_STEP = 4
ATTN_FWD_HEADS_PER_STEP = 8
ATTN_STAGED_HEADS = 2
ANY = pl.BlockSpec(memory_space=pl.ANY)


def _tile(n, cap, quantum):
    best = None
    d = quantum
    while d <= min(n, cap):
        if n % d == 0:
            best = d
        d += quantum
    return n if best is None else best


def _call(body, *, name, out_shape, grid=(), in_specs=None, out_specs=None, scratch=(), sem=None, aliases=None):
    params = {} if sem is None else {"dimension_semantics": sem}
    return pl.pallas_call(
        body, name=name, grid=grid, in_specs=in_specs, out_specs=out_specs, out_shape=out_shape,
        scratch_shapes=list(scratch), input_output_aliases=aliases or {},
        compiler_params=pltpu.CompilerParams(**params))


def _call_prefetch(body, *, name, out_shape, grid, n_prefetch, in_specs, out_specs, scratch, sem):
    spec = pltpu.PrefetchScalarGridSpec(num_scalar_prefetch=n_prefetch, grid=grid, in_specs=in_specs,
                                        out_specs=out_specs, scratch_shapes=list(scratch))
    return pl.pallas_call(
        body, name=name, grid_spec=spec, out_shape=out_shape,
        compiler_params=pltpu.CompilerParams(dimension_semantics=sem))


def _place():
    x, y, c = lax.axis_index("x"), lax.axis_index("y"), lax.axis_index("c")
    return x, y, c


def _mm(a, b, mode, out_dtype, name):
    b_arr, b_idx = b if isinstance(b, tuple) else (b, None)
    bs = b_arr.shape[-2:]
    if mode == "nn":
        (M, K), (K2, N) = a.shape, bs
        dims = (((1,), (0,)), ((), ()))
    elif mode == "nt":
        (M, K), (N, K2) = a.shape, bs
        dims = (((1,), (1,)), ((), ()))
    else:
        (K, M), (K2, N) = a.shape, bs
        dims = (((0,), (0,)), ((), ()))
    assert K == K2, (name, a.shape, b_arr.shape)
    if mode == "tn":
        tm = _tile(M, 1408, LANES)
        tk = _tile(K, 2048, BF16_ROWS)
        tn = _tile(N, 512, LANES)
    else:
        tm = _tile(M, 1024, BF16_ROWS)
        tk = K if K <= 2816 else _tile(K, 2816, LANES)
        tn = _tile(N, 1408 if tk <= 1024 else 512, LANES)
    if tn < 256:
        tn = N
        tm = _tile(M, 512, LANES if mode == "tn" else BF16_ROWS)
    nk = K // tk
    grid = (M // tm, N // tn, nk)

    if mode == "tn":
        a_spec = pl.BlockSpec((tk, tm), lambda i, j, k: (k, i))
    else:
        a_spec = pl.BlockSpec((tm, tk), lambda i, j, k: (i, k))
    if mode == "nt":
        b_blk, b_map = (tn, tk), (lambda i, j, k: (j, k))
    else:
        b_blk, b_map = (tk, tn), (lambda i, j, k: (k, j))
    if b_idx is None:
        b_spec = pl.BlockSpec(b_blk, b_map)
    else:
        b_spec = pl.BlockSpec((None,) + b_blk, lambda i, j, k: (b_idx,) + b_map(i, j, k))

    def body(a_ref, b_ref, o_ref, *acc):
        r = lax.dot_general(a_ref[...].astype(BF16), b_ref[...].astype(BF16), dims, preferred_element_type=F32)
        if nk == 1:
            o_ref[...] = r.astype(o_ref.dtype)
        else:
            k = pl.program_id(2)

            @pl.when(k == 0)
            def _():
                acc[0][...] = r

            @pl.when(k > 0)
            def _():
                acc[0][...] += r

            @pl.when(k == nk - 1)
            def _():
                o_ref[...] = acc[0][...].astype(o_ref.dtype)

    return _call(
        body, name=name, grid=grid, in_specs=[a_spec, b_spec],
        out_specs=pl.BlockSpec((tm, tn), lambda i, j, k: (i, j)),
        out_shape=jax.ShapeDtypeStruct((M, N), out_dtype),
        scratch=[pltpu.VMEM((tm, tn), F32)] if nk > 1 else [],
        sem=("parallel", "parallel", "arbitrary"))(a, b_arr)


def _rowwise(fn, rows, pars, outs, pouts, name):
    R = rows[0].shape[0]
    row_bytes = 4 * (sum(max(r.shape[1], LANES) for r in rows) + sum(max(c, LANES) for c, _ in outs))
    tb = _tile(R, max(BF16_ROWS, ROW_BLOCK_BYTES // row_bytes), BF16_ROWS)
    nr, npar, no = len(rows), len(pars), len(outs)

    def body(*refs):
        r_in, p_in = refs[:nr], refs[nr:nr + npar]
        r_out, p_out = refs[nr + npar:nr + npar + no], refs[nr + npar + no:]
        ro, po = fn([r[...] for r in r_in], [p[...] for p in p_in])
        for ref, val in zip(r_out, ro):
            if isinstance(val, (tuple, list)):
                off = 0
                for piece in val:
                    w = piece.shape[1]
                    ref[:, off:off + w] = piece.astype(ref.dtype)
                    off += w
            else:
                ref[...] = val.astype(ref.dtype)
        if p_out:
            first = pl.program_id(0) == 0

            @pl.when(first)
            def _():
                for ref, val in zip(p_out, po):
                    ref[...] = val

            @pl.when(jnp.logical_not(first))
            def _():
                for ref, val in zip(p_out, po):
                    ref[...] += val

    res = _call(
        body, name=name, grid=(R // tb,),
        in_specs=[pl.BlockSpec((tb, r.shape[1]), lambda i: (i, 0)) for r in rows]
        + [pl.BlockSpec(p.shape, lambda i: (0, 0)) for p in pars],
        out_specs=[pl.BlockSpec((tb, c), lambda i: (i, 0)) for c, _ in outs]
        + [pl.BlockSpec(s, lambda i: (0, 0)) for s in pouts],
        out_shape=[jax.ShapeDtypeStruct((R, c), dt) for c, dt in outs]
        + [jax.ShapeDtypeStruct(s, F32) for s in pouts],
        sem=("arbitrary",) if pouts else ("parallel",))(*rows, *pars)
    return list(res)


def _rms(x, g):
    return x * lax.rsqrt(jnp.mean(x * x, axis=-1, keepdims=True) + NORM_EPS) * g


def _norm_mod(x, g, sh, sc):
    return _rms(x, g) * (1.0 + sc) + sh


def _gated_post(y, g, gate):
    return gate * _rms(y, g)


def _norm_mod_fwd(x, g, sh, sc, name):
    return _rowwise(lambda r, p: ([_norm_mod(r[0], *p)], []), [x], [g, sh, sc], [(x.shape[1], BF16)], [], name)[0]


def _norm_mod_bwd(dxo, dh, x, g, sh, sc, name):
    def fn(r, p):
        _, vjp = jax.vjp(_norm_mod, r[2], *p)
        dx, dg, dsh, dsc = vjp(r[1].astype(F32))
        return [r[0] + dx], [dg, dsh, dsc]
    c = x.shape[1]
    return _rowwise(fn, [dxo, dh, x], [g, sh, sc], [(c, F32)], [(1, c)] * 3, name)


def _post_fwd(x, y, g, gate, name):
    return _rowwise(lambda r, p: ([r[0] + _gated_post(r[1].astype(F32), *p)], []), [x, y], [g, gate],
                    [(x.shape[1], F32)], [], name)[0]


def _post_bwd(dxo, y, g, gate, name):
    def fn(r, p):
        _, vjp = jax.vjp(_gated_post, r[1].astype(F32), *p)
        dy, dg, dgate = vjp(r[0])
        return [dy], [dg, dgate]
    c = y.shape[1]
    return _rowwise(fn, [dxo, y], [g, gate], [(c, BF16)], [(1, c)] * 2, name)


def _post_pre_fwd(x, y, g_post, gate, g_pre, sh, sc, name):
    def fn(r, p):
        x1 = r[0] + _gated_post(r[1].astype(F32), p[0], p[1])
        return [x1, _norm_mod(x1, p[2], p[3], p[4])], []
    c = x.shape[1]
    return _rowwise(fn, [x, y], [g_post, gate, g_pre, sh, sc], [(c, F32), (c, BF16)], [], name)


def _pre_post_bwd(dxo, dh, x, g_pre, sh, sc, y, g_post, gate, name):
    def fn(r, p):
        _, vjp_pre = jax.vjp(_norm_mod, r[2], p[0], p[1], p[2])
        dxn, dg_pre, dsh, dsc = vjp_pre(r[1].astype(F32))
        dx = r[0] + dxn
        _, vjp_post = jax.vjp(_gated_post, r[3].astype(F32), p[3], p[4])
        dy, dg_post, dgate = vjp_post(dx)
        return [dx, dy], [dg_pre, dsh, dsc, dg_post, dgate]
    c = x.shape[1]
    return _rowwise(fn, [dxo, dh, x, y], [g_pre, sh, sc, g_post, gate], [(c, F32), (c, BF16)], [(1, c)] * 5, name)


def _swiglu(g, u):
    return jax.nn.silu(g) * u


def _ffn_up(h, w, l, hw, name):
    S, D = h.shape
    nb = w.shape[2] // (2 * hw)
    tm = _tile(S, 512, BF16_ROWS)

    def body(h_ref, w_ref, gu_ref, act_ref):
        gu = jnp.dot(h_ref[...], w_ref[...], preferred_element_type=F32).astype(BF16)
        gu_ref[...] = gu
        act_ref[...] = _swiglu(gu[:, :hw].astype(F32), gu[:, hw:].astype(F32)).astype(BF16)

    return _call(body, name=name, grid=(S // tm, nb),
                 in_specs=[pl.BlockSpec((tm, D), lambda i, j: (i, 0)), pl.BlockSpec((None, D, 2 * hw), lambda i, j: (l, 0, j))],
                 out_specs=[pl.BlockSpec((tm, 2 * hw), lambda i, j: (i, j)), pl.BlockSpec((tm, hw), lambda i, j: (i, j))],
                 out_shape=[jax.ShapeDtypeStruct((S, 2 * hw * nb), BF16), jax.ShapeDtypeStruct((S, hw * nb), BF16)],
                 sem=("parallel", "parallel"))(h, w)


def _ffn_down_dx(dy, w_dn, l, gu, hw, name):
    S, D = dy.shape
    nb = gu.shape[1] // (2 * hw)
    tm = _tile(S, 512, BF16_ROWS)

    def body(dy_ref, w_ref, gu_ref, dgu_ref):
        dact = lax.dot_general(dy_ref[...], w_ref[...], (((1,), (1,)), ((), ())), preferred_element_type=F32)
        _, vjp = jax.vjp(_swiglu, gu_ref[:, :hw].astype(F32), gu_ref[:, hw:].astype(F32))
        dg, du = vjp(dact)
        dgu_ref[:, :hw] = dg.astype(BF16)
        dgu_ref[:, hw:] = du.astype(BF16)

    return _call(body, name=name, grid=(S // tm, nb),
                 in_specs=[pl.BlockSpec((tm, D), lambda i, j: (i, 0)), pl.BlockSpec((None, hw, D), lambda i, j: (l, j, 0)),
                           pl.BlockSpec((tm, 2 * hw), lambda i, j: (i, j))],
                 out_specs=pl.BlockSpec((tm, 2 * hw), lambda i, j: (i, j)),
                 out_shape=jax.ShapeDtypeStruct(gu.shape, BF16), sem=("parallel", "parallel"))(dy, w_dn, gu)


def _silu_rows(c, name):
    return _rowwise(lambda r, p: ([jax.nn.silu(r[0])], []), [c], [], [(c.shape[1], F32)], [], name)[0]


def _head_norm(x, g, scale):
    return _rms(x, g) * scale


def _head_norm_fwd(x, g, scale, name):
    return _rowwise(lambda r, p: ([_head_norm(r[0].astype(F32), p[0], scale)], []), [x], [g],
                    [(x.shape[1], BF16)], [], name)[0]


def _head_norm_bwd(dy, x, g, scale, name):
    def fn(r, p):
        _, vjp = jax.vjp(lambda t, gg: _head_norm(t, gg, scale), r[1].astype(F32), p[0])
        dx, dg = vjp(r[0])
        return [dx], [dg]
    c = x.shape[1]
    return _rowwise(fn, [dy, x], [g], [(c, F32)], [(1, c)], name)


def _out_gate_fwd(o, qg, name):
    d = o.shape[1]
    return _rowwise(lambda r, p: ([r[0] * jax.nn.sigmoid(r[1][:, d:].astype(F32))], []), [o, qg], [],
                    [(d, BF16)], [], name)[0]


def _out_gate_bwd(dog, o, qg, name):
    d = o.shape[1]

    def fn(r, p):
        _, vjp = jax.vjp(lambda oo, gl: oo * jax.nn.sigmoid(gl), r[1], r[2][:, d:].astype(F32))
        do, dgl = vjp(r[0])
        return [do, dgl], []
    return _rowwise(fn, [dog, o, qg], [], [(d, BF16), (d, BF16)], [], name)


def _loss_bwd(y, tgt, name):
    n = y.shape[1]

    def fn(r, p):
        e = r[0] - r[1]
        part = jnp.sum(jnp.sum(e * e, axis=1, keepdims=True), axis=0, keepdims=True) * (0.5 / n)
        return [e * (1.0 / n)], [part]
    return _rowwise(fn, [y, tgt], [], [(n, F32)], [(1, 1)], name)


def _adamw(w, g, m, v, name):
    shape = w.shape
    c = shape[-1]
    flat = [t.reshape(-1, c) for t in (w, g, m, v)]

    def fn(r, p):
        w_, g_, m_, v_ = r
        m2 = ADAM_B1 * m_ + (1.0 - ADAM_B1) * g_
        v2 = ADAM_B2 * v_ + (1.0 - ADAM_B2) * (g_ * g_)
        m_hat = m2 / (1.0 - ADAM_B1 ** ADAM_STEP)
        v_hat = v2 / (1.0 - ADAM_B2 ** ADAM_STEP)
        delta = -ADAM_LR * (m_hat / (jnp.sqrt(v_hat) + ADAM_EPS) + ADAM_WD * w_)
        return [delta, m2, v2], []
    res = _rowwise(fn, flat, [], [(c, F32)] * 3, [], name)
    return [t.reshape(shape) for t in res]


def _sum_pair(a, b, name):
    c = a.shape[-1]
    out = _rowwise(lambda r, p: ([r[0].astype(F32) + r[1].astype(F32)], []), [a.reshape(-1, c), b.reshape(-1, c)], [],
                   [(c, BF16)], [], name)[0]
    return out.reshape(a.shape)


def _sum_slots(recv, name):
    n = recv.shape[0]
    shape = recv.shape[1:]
    c = shape[-1]
    r3 = recv.reshape(n, -1, c)
    rows = r3.shape[1]
    tb = _tile(rows, max(BF16_ROWS, ROW_BLOCK_BYTES // (4 * c * (n + 1))), BF16_ROWS)

    def body(r_ref, o_ref):
        acc = r_ref[0].astype(F32)
        for s in range(1, n):
            acc = acc + r_ref[s].astype(F32)
        o_ref[...] = acc

    out = _call(body, name=name, grid=(rows // tb,),
                in_specs=[pl.BlockSpec((n, tb, c), lambda i: (0, i, 0))],
                out_specs=pl.BlockSpec((tb, c), lambda i: (i, 0)),
                out_shape=jax.ShapeDtypeStruct((rows, c), F32), sem=("parallel",))(r3)
    return out.reshape(shape)


def _gmlp_pre(zu, zv, b_u, b_v, ln_g, ln_b):
    u = jax.nn.gelu(zu + b_u, approximate=True)
    v = jax.nn.gelu(zv + b_v, approximate=True)
    xc = v - jnp.mean(v, axis=-1, keepdims=True)
    vn = xc * lax.rsqrt(jnp.mean(xc * xc, axis=-1, keepdims=True) + NORM_EPS) * ln_g + ln_b
    return u, vn


def _gmlp_fwd(zp, b_in, ln_g, ln_b, ws, bs_t, name):
    S, gw2 = zp.shape
    gw = gw2 // 2
    G, ch, _ = ws.shape
    gd = gw // G
    tb = 2 * ch

    def body(zp_ref, bin_ref, lg_ref, lb_ref, ws_ref, bs_ref, o_ref):
        u, vn = _gmlp_pre(zp_ref[:, :gw].astype(F32), zp_ref[:, gw:].astype(F32), bin_ref[:, :gw], bin_ref[:, gw:],
                          lg_ref[...], lb_ref[...])
        vnb = vn.astype(BF16)
        for c in range(tb // ch):
            for g in range(G):
                rs, cs = slice(c * ch, (c + 1) * ch), slice(g * gd, (g + 1) * gd)
                vv = jnp.dot(ws_ref[g], vnb[rs, cs], preferred_element_type=F32) + bs_ref[:, g:g + 1]
                o_ref[rs, cs] = (u[rs, cs] * vv).astype(o_ref.dtype)

    full = lambda a: pl.BlockSpec(a.shape, lambda i: (0,) * a.ndim)
    return _call(body, name=name, grid=(S // tb,),
                 in_specs=[pl.BlockSpec((tb, gw2), lambda i: (i, 0)), full(b_in), full(ln_g), full(ln_b), full(ws), full(bs_t)],
                 out_specs=pl.BlockSpec((tb, gw), lambda i: (i, 0)),
                 out_shape=jax.ShapeDtypeStruct((S, gw), BF16), sem=("parallel",))(zp, b_in, ln_g, ln_b, ws, bs_t)


def _gmlp_bwd(dyg, zp, b_in, ln_g, ln_b, ws, ws_t, bs_t, name):
    S, gw2 = zp.shape
    gw = gw2 // 2
    G, ch, _ = ws.shape
    gd = gw // G
    tb = 2 * ch

    def body(dy_ref, zp_ref, bin_ref, lg_ref, lb_ref, ws_ref, wst_ref, bs_ref,
             dzp_ref, dbin_ref, dlg_ref, dlb_ref, dws_ref, dbs_ref, du_sc, dvn_sc):
        (u, vn), vjp = jax.vjp(_gmlp_pre, zp_ref[:, :gw].astype(F32), zp_ref[:, gw:].astype(F32), bin_ref[:, :gw],
                               bin_ref[:, gw:], lg_ref[...], lb_ref[...])
        vnb = vn.astype(BF16)
        first = pl.program_id(0) == 0

        @pl.when(first)
        def _():
            dws_ref[...] = jnp.zeros_like(dws_ref)

        lane = lax.broadcasted_iota(jnp.int32, (ch, G), 1)
        dbs = jnp.zeros((ch, G), F32)
        for g in range(G):
            cs = slice(g * gd, (g + 1) * gd)
            dws_g = jnp.zeros((ch, ch), F32)
            col = jnp.zeros((ch, 1), F32)
            for c in range(tb // ch):
                rs = slice(c * ch, (c + 1) * ch)
                vnp = vnb[rs, cs]
                vv = jnp.dot(ws_ref[g], vnp, preferred_element_type=F32) + bs_ref[:, g:g + 1]
                dy = dy_ref[rs, cs].astype(F32)
                du_sc[rs, cs] = dy * vv
                dvv = dy * u[rs, cs]
                dvvb = dvv.astype(BF16)
                dvn_sc[rs, cs] = jnp.dot(wst_ref[g], dvvb, preferred_element_type=F32)
                dws_g = dws_g + lax.dot_general(dvvb, vnp, (((1,), (1,)), ((), ())), preferred_element_type=F32)
                col = col + jnp.sum(dvv, axis=1, keepdims=True)
            dws_ref[g] += dws_g
            dbs = jnp.where(lane == g, col, dbs)
        dzu, dzv, dbu, dbv, dlg, dlb = vjp((du_sc[...], dvn_sc[...]))
        dzp_ref[:, :gw] = dzu.astype(dzp_ref.dtype)
        dzp_ref[:, gw:] = dzv.astype(dzp_ref.dtype)

        @pl.when(first)
        def _():
            dbin_ref[:, :gw] = dbu
            dbin_ref[:, gw:] = dbv
            dlg_ref[...] = dlg
            dlb_ref[...] = dlb
            dbs_ref[...] = dbs

        @pl.when(jnp.logical_not(first))
        def _():
            dbin_ref[:, :gw] += dbu
            dbin_ref[:, gw:] += dbv
            dlg_ref[...] += dlg
            dlb_ref[...] += dlb
            dbs_ref[...] += dbs

    full = lambda a: pl.BlockSpec(a.shape, lambda i: (0,) * a.ndim)
    fshape = lambda s: pl.BlockSpec(s, lambda i: (0,) * len(s))
    return _call(
        body, name=name, grid=(S // tb,),
        in_specs=[pl.BlockSpec((tb, gw), lambda i: (i, 0)), pl.BlockSpec((tb, gw2), lambda i: (i, 0)),
                  full(b_in), full(ln_g), full(ln_b), full(ws), full(ws_t), full(bs_t)],
        out_specs=[pl.BlockSpec((tb, gw2), lambda i: (i, 0)), fshape((1, gw2)), fshape((1, gw)), fshape((1, gw)),
                   fshape((G, ch, ch)), fshape((ch, G))],
        out_shape=[jax.ShapeDtypeStruct((S, gw2), BF16), jax.ShapeDtypeStruct((1, gw2), F32),
                   jax.ShapeDtypeStruct((1, gw), F32), jax.ShapeDtypeStruct((1, gw), F32),
                   jax.ShapeDtypeStruct((G, ch, ch), F32), jax.ShapeDtypeStruct((ch, G), F32)],
        scratch=[pltpu.VMEM((tb, gw), F32), pltpu.VMEM((tb, gw), F32)],
        sem=("arbitrary",))(dyg, zp, b_in, ln_g, ln_b, ws, ws_t, bs_t)


def _dot_01(x, ones_bf16):
    hi = x.astype(BF16)
    r1 = x - hi.astype(F32)
    mid = r1.astype(BF16)
    lo = (r1 - mid.astype(F32)).astype(BF16)
    dot = lambda t: jnp.dot(t, ones_bf16, preferred_element_type=F32)
    return dot(hi) + dot(mid) + dot(lo)


def _log_sigmoid(x):
    return jnp.minimum(x, 0.0) - jnp.log1p(jnp.exp(-jnp.abs(x)))


def _dcum_fwd(f_t, b_col, name):
    H, S = f_t.shape
    tb = _tile(S, 512, LANES)

    def body(f_ref, b_ref, o_ref, carry):
        @pl.when(pl.program_id(0) == 0)
        def _():
            carry[...] = jnp.zeros_like(carry)

        ls = _log_sigmoid(f_ref[...] + b_ref[...])
        r = lax.broadcasted_iota(jnp.int32, (tb, tb), 0)
        c = lax.broadcasted_iota(jnp.int32, (tb, tb), 1)
        upper = (r <= c).astype(BF16)
        o_ref[...] = _dot_01(ls, upper) + carry[...]
        carry[...] += jnp.sum(ls, axis=1, keepdims=True)

    return _call(body, name=name, grid=(S // tb,),
                 in_specs=[pl.BlockSpec((H, tb), lambda i: (0, i)), pl.BlockSpec((H, 1), lambda i: (0, 0))],
                 out_specs=pl.BlockSpec((H, tb), lambda i: (0, i)),
                 out_shape=jax.ShapeDtypeStruct((H, S), F32),
                 scratch=[pltpu.VMEM((H, 1), F32)], sem=("arbitrary",))(f_t, b_col)


def _dcum_bwd(dd_t, f_t, b_col, name):
    H, S = f_t.shape
    tb = _tile(S, 512, LANES)
    nb = S // tb

    def body(dd_ref, f_ref, b_ref, df_ref, db_ref, carry):
        first = pl.program_id(0) == 0

        @pl.when(first)
        def _():
            carry[...] = jnp.zeros_like(carry)

        dd = dd_ref[...]
        r = lax.broadcasted_iota(jnp.int32, (tb, tb), 0)
        c = lax.broadcasted_iota(jnp.int32, (tb, tb), 1)
        lower = (r >= c).astype(BF16)
        rev = _dot_01(dd, lower) + carry[...]
        carry[...] += jnp.sum(dd, axis=1, keepdims=True)
        df = rev * jax.nn.sigmoid(-(f_ref[...] + b_ref[...]))
        df_ref[...] = df
        part = jnp.sum(df, axis=1, keepdims=True)

        @pl.when(first)
        def _():
            db_ref[...] = part

        @pl.when(jnp.logical_not(first))
        def _():
            db_ref[...] += part

    return _call(body, name=name, grid=(nb,),
                 in_specs=[pl.BlockSpec((H, tb), lambda i: (0, nb - 1 - i)), pl.BlockSpec((H, tb), lambda i: (0, nb - 1 - i)),
                           pl.BlockSpec((H, 1), lambda i: (0, 0))],
                 out_specs=[pl.BlockSpec((H, tb), lambda i: (0, nb - 1 - i)), pl.BlockSpec((H, 1), lambda i: (0, 0))],
                 out_shape=[jax.ShapeDtypeStruct((H, S), F32), jax.ShapeDtypeStruct((H, 1), F32)],
                 scratch=[pltpu.VMEM((H, 1), F32)], sem=("arbitrary",))(dd_t, f_t, b_col)


def _attn_tile(S):
    return _tile(S, 512, LANES)


def _causal_t(t):
    return lax.broadcasted_iota(jnp.int32, (t, t), 0) <= lax.broadcasted_iota(jnp.int32, (t, t), 1)


def _tri_pairs(n, key_major):
    if key_major:
        pairs = [(i, j) for j in range(n) for i in range(j, n)]
    else:
        pairs = [(i, j) for i in range(n) for j in range(i + 1)]
    return jnp.asarray([p[0] for p in pairs], jnp.int32), jnp.asarray([p[1] for p in pairs], jnp.int32)


def _split3(x):
    hi = lax.reduce_precision(x, 8, 7)
    r = x - hi
    mid = lax.reduce_precision(r, 8, 7)
    lo = lax.reduce_precision(r - mid, 8, 7)
    return hi.astype(BF16), mid.astype(BF16), lo.astype(BF16)


def _augment(xn, dcum, query):
    H, S, hd = xn.shape
    parts = list(_split3(dcum))
    vals = parts + [1.0] * 3 if query else [1.0] * 3 + [-p for p in parts]
    lane = lax.broadcasted_iota(jnp.int32, (1, 1, LANES), 2)
    out = jnp.pad(xn, ((0, 0), (0, 0), (0, LANES - hd)))
    for k, val in enumerate(vals):
        val = jnp.asarray(val, BF16)
        out = jnp.where(lane == hd + k, val[..., None] if val.ndim else val, out)
    return out


def _scores_t(k_ref, qt_ref, h, t, diag):
    st = jnp.dot(k_ref[h], qt_ref[h], preferred_element_type=F32)
    return jnp.where(_causal_t(t), st, MASKED) if diag else st


def _flash_fwd(ka, qat, vat, hd, name):
    H, S, da = ka.shape
    t = _attn_tile(S)
    hb = min(H, ATTN_FWD_HEADS_PER_STEP)
    it, jt = _tri_pairs(S // t, False)

    def body(it_ref, jt_ref, k_ref, qt_ref, vt_ref, o_ref, lse_ref, m_sc, acc_sc):
        i, j = it_ref[pl.program_id(1)], jt_ref[pl.program_id(1)]

        @pl.when(j == 0)
        def _():
            m_sc[...] = jnp.full_like(m_sc, MASKED)
            acc_sc[...] = jnp.zeros_like(acc_sc)

        def step(diag):
            sts = [_scores_t(k_ref, qt_ref, h, t, diag) for h in range(hb)]
            pts, alphas = [], []
            for h in range(hb):
                m_prev = m_sc[h]
                m_new = jnp.maximum(m_prev, jnp.max(sts[h], axis=0, keepdims=True))
                pts.append(jnp.exp(sts[h] - m_new).astype(BF16))
                alphas.append(jnp.exp(m_prev - m_new))
                m_sc[h] = m_new
            for h in range(hb):
                acc_sc[h] = alphas[h] * acc_sc[h] + jnp.dot(vt_ref[h], pts[h], preferred_element_type=F32)

        @pl.when(j < i)
        def _():
            step(False)

        @pl.when(j == i)
        def _():
            step(True)
            for h in range(hb):
                l = acc_sc[h, hd:hd + 1, :]
                o_ref[h] = acc_sc[h, :hd, :] / l
                lse_ref[h] = m_sc[h] + jnp.log(l)

    qcol = lambda h, p, it_, jt_: (h, 0, it_[p])
    kcol = lambda h, p, it_, jt_: (h, 0, jt_[p])
    krow = lambda h, p, it_, jt_: (h, jt_[p], 0)
    return _call_prefetch(
        body, name=name, grid=(H // hb, it.shape[0]), n_prefetch=2,
        in_specs=[pl.BlockSpec((hb, t, da), krow), pl.BlockSpec((hb, da, t), qcol), pl.BlockSpec((hb, da, t), kcol)],
        out_specs=[pl.BlockSpec((hb, hd, t), qcol), pl.BlockSpec((hb, 1, t), qcol)],
        out_shape=[jax.ShapeDtypeStruct((H, hd, S), F32), jax.ShapeDtypeStruct((H, 1, S), F32)],
        scratch=[pltpu.VMEM((hb, 1, t), F32), pltpu.VMEM((hb, da, t), F32)],
        sem=("parallel", "arbitrary"))(it, jt, ka, qat, vat)


def _flash_bwd(ka, kat, qat, v, dot, o_tr, lse_r, name):
    H, S, hd = v.shape
    da = ka.shape[2]
    t = _attn_tile(S)
    n = S // t
    hb = ATTN_HEADS_PER_STEP
    it, jt = _tri_pairs(n, True)
    over_queries = (((1,), (1,)), ((), ()))

    def body(it_ref, jt_ref, k_ref, kt_ref, qt_ref, v_ref, dot_ref, o_ref, lse_ref, dq_ref, dk_ref, dv_ref, dk_sc, dv_sc):
        i, j = it_ref[pl.program_id(1)], jt_ref[pl.program_id(1)]

        @pl.when(pl.program_id(1) == 0)
        def _():
            dq_ref[...] = jnp.zeros_like(dq_ref)

        def step(diag):
            for h0 in range(0, hb, ATTN_STAGED_HEADS):
                hs = range(h0, min(h0 + ATTN_STAGED_HEADS, hb))
                sts = [_scores_t(k_ref, qt_ref, h, t, diag) for h in hs]
                dpts = [jnp.dot(v_ref[h], dot_ref[h], preferred_element_type=F32) for h in hs]
                tiles = []
                for h, st, dpt in zip(hs, sts, dpts):
                    dl = jnp.sum(dot_ref[h].astype(F32) * o_ref[h], axis=0, keepdims=True)
                    pt = jnp.exp(st - lse_ref[h])
                    tiles.append((pt.astype(BF16), (pt * (dpt - dl)).astype(BF16)))
                for h, (ptb, dsb) in zip(hs, tiles):
                    dv_sc[h] += lax.dot_general(dot_ref[h], ptb, over_queries, preferred_element_type=F32)
                    dk_sc[h] += lax.dot_general(qt_ref[h], dsb, over_queries, preferred_element_type=F32)
                    dq_ref[h, i] += jnp.dot(kt_ref[h], dsb, preferred_element_type=F32)

        @pl.when(i == j)
        def _():
            dk_sc[...] = jnp.zeros_like(dk_sc)
            dv_sc[...] = jnp.zeros_like(dv_sc)
            step(True)

        @pl.when(i > j)
        def _():
            step(False)

        @pl.when(i == n - 1)
        def _():
            dk_ref[...] = dk_sc[...]
            dv_ref[...] = dv_sc[...]

    krow = lambda h, p, it_, jt_: (h, jt_[p], 0)
    kcol = lambda h, p, it_, jt_: (h, 0, jt_[p])
    qcol = lambda h, p, it_, jt_: (h, 0, it_[p])
    return _call_prefetch(
        body, name=name, grid=(H // hb, it.shape[0]), n_prefetch=2,
        in_specs=[pl.BlockSpec((hb, t, da), krow), pl.BlockSpec((hb, da, t), kcol), pl.BlockSpec((hb, da, t), qcol),
                  pl.BlockSpec((hb, t, hd), krow), pl.BlockSpec((hb, hd, t), qcol), pl.BlockSpec((hb, hd, t), qcol),
                  pl.BlockSpec((hb, 1, t), qcol)],
        out_specs=[pl.BlockSpec((hb, n, da, t), lambda h, p, it_, jt_: (h, 0, 0, 0)), pl.BlockSpec((hb, da, t), kcol),
                   pl.BlockSpec((hb, hd, t), kcol)],
        out_shape=[jax.ShapeDtypeStruct((H, n, da, t), F32), jax.ShapeDtypeStruct((H, da, S), F32),
                   jax.ShapeDtypeStruct((H, hd, S), F32)],
        scratch=[pltpu.VMEM((hb, da, t), F32), pltpu.VMEM((hb, hd, t), F32)],
        sem=("parallel", "arbitrary"))(it, jt, ka, kat, qat, v, dot, o_tr, lse_r)


def _offsets(n_bits):
    return [tuple((k >> b) & 1 for b in reversed(range(n_bits))) for k in range(1, 1 << n_bits)]


def _own_slot(out, own, idx):
    return lax.dynamic_update_index_in_dim(out, own.astype(out.dtype), idx, 0)


def _gather8(arrs, name):
    n = len(arrs)
    offs = _offsets(3)

    def body(*refs):
        ins, outs = refs[:n], refs[n:2 * n]
        ssem, rsem = refs[2 * n:]
        x, y, c = _place()
        me = 4 * x + 2 * y + c
        copies = []
        for a in range(n):
            for k, (dx, dy, dcc) in enumerate(offs):
                cp = pltpu.make_async_remote_copy(
                    src_ref=ins[a], dst_ref=outs[a].at[me], send_sem=ssem.at[a, k], recv_sem=rsem.at[a, k],
                    device_id=((x + dx) % 2, (y + dy) % 2, (c + dcc) % 2), device_id_type=MESH)
                cp.start()
                copies.append(cp)
        for cp in copies:
            cp.wait()

    return _call(body, name=name, in_specs=[ANY] * n, out_specs=[ANY] * n,
                 out_shape=[jax.ShapeDtypeStruct((N_DEV,) + a.shape, a.dtype) for a in arrs],
                 scratch=[pltpu.SemaphoreType.DMA((n, 7)), pltpu.SemaphoreType.DMA((n, 7))])(*arrs)


def _chip_position(chip, place):
    return sum(jnp.where(chip == s, place[1].index(s), 0) for s in range(N_CHIPS))


def _shard_slot(ref, shard_shape, place, chip, c=None):
    hn = shard_shape[0] // 2
    half = slice(None) if c is None else pl.ds(c * hn, hn)
    if place is None:
        return ref.at[chip, half]
    ax = place[0]
    w = shard_shape[ax]
    idx = [slice(None)] * len(shard_shape)
    idx[0] = half
    idx[ax] = pl.ds(pl.multiple_of(_chip_position(chip, place) * w, LANES if ax == len(shard_shape) - 1 else BF16_ROWS), w)
    return ref.at[tuple(idx)]


def _gathered_shape(shard_shape, place):
    if place is None:
        return (N_CHIPS,) + tuple(shard_shape)
    s = list(shard_shape)
    s[place[0]] *= N_CHIPS
    return tuple(s)


def _chip_gather(arrs, halved, name, places=None):
    n = len(arrs)
    offs = _offsets(2)
    places = places or [None] * n

    def body(*refs):
        ins, outs = refs[:n], refs[n:2 * n]
        ssem, rsem = refs[2 * n:]
        x, y, c = _place()
        chip = 2 * x + y
        copies = []
        for a in range(n):
            if halved:
                hn = arrs[a].shape[0] // 2
                src = ins[a].at[pl.ds(c * hn, hn)]
                dst = _shard_slot(outs[a], arrs[a].shape, places[a], chip, c)
            else:
                src, dst = ins[a], outs[a].at[chip]
            for k, (dx, dy) in enumerate(offs):
                cp = pltpu.make_async_remote_copy(
                    src_ref=src, dst_ref=dst, send_sem=ssem.at[a, k], recv_sem=rsem.at[a, k],
                    device_id=((x + dx) % 2, (y + dy) % 2, c), device_id_type=MESH)
                cp.start()
                copies.append(cp)
        for cp in copies:
            cp.wait()

    return _call(body, name=name, in_specs=[ANY] * n, out_specs=[ANY] * n,
                 out_shape=[jax.ShapeDtypeStruct(_gathered_shape(a.shape, p) if halved else (N_CHIPS,) + a.shape, a.dtype)
                            for a, p in zip(arrs, places)],
                 scratch=[pltpu.SemaphoreType.DMA((n, 3)), pltpu.SemaphoreType.DMA((n, 3))])(*arrs)


def _sibling_fill(bufs, owns, places, name):
    n = len(bufs)
    offs = _offsets(2)

    def body(*refs):
        ins, own, outs = refs[:n], refs[n:2 * n], refs[2 * n:3 * n]
        ssem, rsem = refs[3 * n:]
        x, y, c = _place()
        copies = []
        for a in range(n):
            shape = owns[a].shape
            for k, (dx, dy) in enumerate(offs):
                chip = 2 * ((x + dx) % 2) + (y + dy) % 2
                cp = pltpu.make_async_remote_copy(
                    src_ref=_shard_slot(ins[a], shape, places[a], chip, c),
                    dst_ref=_shard_slot(outs[a], shape, places[a], chip, c),
                    send_sem=ssem.at[a, k], recv_sem=rsem.at[a, k],
                    device_id=(x, y, 1 - c), device_id_type=MESH)
                cp.start()
                copies.append(cp)
            cp = pltpu.make_async_remote_copy(
                src_ref=own[a], dst_ref=_shard_slot(outs[a], shape, places[a], 2 * x + y),
                send_sem=ssem.at[a, 3], recv_sem=rsem.at[a, 3], device_id=(x, y, 1 - c), device_id_type=MESH)
            cp.start()
            copies.append(cp)
        for cp in copies:
            cp.wait()

    return _call(body, name=name, in_specs=[ANY] * (2 * n), out_specs=[ANY] * n,
                 out_shape=[jax.ShapeDtypeStruct(b.shape, b.dtype) for b in bufs],
                 scratch=[pltpu.SemaphoreType.DMA((n, 4)), pltpu.SemaphoreType.DMA((n, 4))],
                 aliases={a: a for a in range(n)})(*bufs, *owns)


def _sibling_pair(arrs, name):
    n = len(arrs)

    def body(*refs):
        ins, outs = refs[:n], refs[n:2 * n]
        ssem, rsem = refs[2 * n:]
        x, y, c = _place()
        copies = []
        for a in range(n):
            cp = pltpu.make_async_remote_copy(
                src_ref=ins[a], dst_ref=outs[a].at[c], send_sem=ssem.at[a], recv_sem=rsem.at[a],
                device_id=(x, y, 1 - c), device_id_type=MESH)
            cp.start()
            copies.append(cp)
        for cp in copies:
            cp.wait()

    return _call(body, name=name, in_specs=[ANY] * n, out_specs=[ANY] * n,
                 out_shape=[jax.ShapeDtypeStruct((N_CORES,) + a.shape, a.dtype) for a in arrs],
                 scratch=[pltpu.SemaphoreType.DMA((n,)), pltpu.SemaphoreType.DMA((n,))])(*arrs)


CHIP_ORDER = (0, 1, 2, 3)


def _piece(shape, spec, j, h):
    shard_ax, half_ax, order = spec
    w = shape[shard_ax] // N_CHIPS
    idx = [slice(None)] * len(shape)
    idx[shard_ax] = pl.ds(order[j] * w, w)
    assert half_ax != shard_ax
    hn = shape[half_ax] // 2
    idx[half_ax] = pl.ds(h * hn, hn)
    return tuple(idx)


def _piece_shape(shape, spec):
    shard_ax, half_ax, _ = spec
    s = list(shape)
    s[shard_ax] //= N_CHIPS
    s[half_ax] //= 2
    return tuple(s)


def _own_pieces(g, spec, c):
    shard_ax, half_ax, order = spec
    hn = g.shape[half_ax] // 2
    half = lax.dynamic_slice_in_dim(g, c * hn, hn, axis=half_ax)
    w = half.shape[shard_ax] // N_CHIPS
    return jnp.stack([lax.slice_in_dim(half, p * w, (p + 1) * w, axis=shard_ax) for p in order])


def _sibling_scatter(arrs, specs, name):
    n = len(arrs)

    def body(*refs):
        ins, outs = refs[:n], refs[n:2 * n]
        ssem, rsem = refs[2 * n:]
        x, y, c = _place()
        for mine in range(N_CORES):
            @pl.when(c == mine)
            def _():
                copies = []
                for a in range(n):
                    for j in range(N_CHIPS):
                        cp = pltpu.make_async_remote_copy(
                            src_ref=ins[a].at[_piece(arrs[a].shape, specs[a], j, 1 - mine)], dst_ref=outs[a].at[j],
                            send_sem=ssem.at[a, j], recv_sem=rsem.at[a, j],
                            device_id=(x, y, 1 - mine), device_id_type=MESH)
                        cp.start()
                        copies.append(cp)
                for cp in copies:
                    cp.wait()

    return _call(body, name=name, in_specs=[ANY] * n, out_specs=[ANY] * n,
                 out_shape=[jax.ShapeDtypeStruct((N_CHIPS,) + _piece_shape(a.shape, s), a.dtype)
                            for a, s in zip(arrs, specs)],
                 scratch=[pltpu.SemaphoreType.DMA((n, N_CHIPS))] * 2)(*arrs)


def _chip_scatter(arrs, name):
    n = len(arrs)
    offs = _offsets(2)

    def body(*refs):
        ins, outs = refs[:n], refs[n:2 * n]
        ssem, rsem = refs[2 * n:]
        x, y, c = _place()
        chip = 2 * x + y
        copies = []
        for a in range(n):
            for k, (dx, dy) in enumerate(offs):
                tx, ty = (x + dx) % 2, (y + dy) % 2
                cp = pltpu.make_async_remote_copy(
                    src_ref=ins[a].at[2 * tx + ty], dst_ref=outs[a].at[chip], send_sem=ssem.at[a, k], recv_sem=rsem.at[a, k],
                    device_id=(tx, ty, c), device_id_type=MESH)
                cp.start()
                copies.append(cp)
        for cp in copies:
            cp.wait()

    return _call(body, name=name, in_specs=[ANY] * n, out_specs=[ANY] * n,
                 out_shape=[jax.ShapeDtypeStruct(a.shape, a.dtype) for a in arrs],
                 scratch=[pltpu.SemaphoreType.DMA((n, 3)), pltpu.SemaphoreType.DMA((n, 3))])(*arrs)


def kernel(x, c, ada_w, ada_b, pre_mix_g, post_mix_g, pre_ffn_g, post_ffn_g, ffn_w_gu, ffn_w_down, a_w_in, a_b_in, a_ln_g, a_ln_b, a_w_s, a_b_s, a_w_out, kv_ada_w, kv_ada_b, kv_norm_g, kv_w, kv_b_f, k_norm_g, b_w_qg, b_q_norm_g, b_w_o, loss_target, m_ada_w, m_ada_b, m_pre_mix_g, m_post_mix_g, m_pre_ffn_g, m_post_ffn_g, m_ffn_w_gu, m_ffn_w_down, m_a_w_in, m_a_b_in, m_a_ln_g, m_a_ln_b, m_a_w_s, m_a_b_s, m_a_w_out, m_kv_ada_w, m_kv_ada_b, m_kv_norm_g, m_kv_w, m_kv_b_f, m_k_norm_g, m_b_w_qg, m_b_q_norm_g, m_b_w_o, v_ada_w, v_ada_b, v_pre_mix_g, v_post_mix_g, v_pre_ffn_g, v_post_ffn_g, v_ffn_w_gu, v_ffn_w_down, v_a_w_in, v_a_b_in, v_a_ln_g, v_a_ln_b, v_a_w_s, v_a_b_s, v_a_w_out, v_kv_ada_w, v_kv_ada_b, v_kv_norm_g, v_kv_w, v_kv_b_f, v_k_norm_g, v_b_w_qg, v_b_q_norm_g, v_b_w_o):
    weights = dict(ada_w=ada_w, ada_b=ada_b, pre_mix_g=pre_mix_g, post_mix_g=post_mix_g, pre_ffn_g=pre_ffn_g,
                   post_ffn_g=post_ffn_g, ffn_w_gu=ffn_w_gu, ffn_w_down=ffn_w_down, a_w_in=a_w_in, a_b_in=a_b_in,
                   a_ln_g=a_ln_g, a_ln_b=a_ln_b, a_w_s=a_w_s, a_b_s=a_b_s, a_w_out=a_w_out, kv_ada_w=kv_ada_w,
                   kv_ada_b=kv_ada_b, kv_norm_g=kv_norm_g, kv_w=kv_w, kv_b_f=kv_b_f, k_norm_g=k_norm_g, b_w_qg=b_w_qg,
                   b_q_norm_g=b_q_norm_g, b_w_o=b_w_o)
    m_in = dict(ada_w=m_ada_w, ada_b=m_ada_b, pre_mix_g=m_pre_mix_g, post_mix_g=m_post_mix_g, pre_ffn_g=m_pre_ffn_g,
                post_ffn_g=m_post_ffn_g, ffn_w_gu=m_ffn_w_gu, ffn_w_down=m_ffn_w_down, a_w_in=m_a_w_in, a_b_in=m_a_b_in,
                a_ln_g=m_a_ln_g, a_ln_b=m_a_ln_b, a_w_s=m_a_w_s, a_b_s=m_a_b_s, a_w_out=m_a_w_out, kv_ada_w=m_kv_ada_w,
                kv_ada_b=m_kv_ada_b, kv_norm_g=m_kv_norm_g, kv_w=m_kv_w, kv_b_f=m_kv_b_f, k_norm_g=m_k_norm_g,
                b_w_qg=m_b_w_qg, b_q_norm_g=m_b_q_norm_g, b_w_o=m_b_w_o)
    v_in = dict(ada_w=v_ada_w, ada_b=v_ada_b, pre_mix_g=v_pre_mix_g, post_mix_g=v_post_mix_g, pre_ffn_g=v_pre_ffn_g,
                post_ffn_g=v_post_ffn_g, ffn_w_gu=v_ffn_w_gu, ffn_w_down=v_ffn_w_down, a_w_in=v_a_w_in, a_b_in=v_a_b_in,
                a_ln_g=v_a_ln_g, a_ln_b=v_a_ln_b, a_w_s=v_a_w_s, a_b_s=v_a_b_s, a_w_out=v_a_w_out, kv_ada_w=v_kv_ada_w,
                kv_ada_b=v_kv_ada_b, kv_norm_g=v_kv_norm_g, kv_w=v_kv_w, kv_b_f=v_kv_b_f, k_norm_g=v_k_norm_g,
                b_w_qg=v_b_w_qg, b_q_norm_g=v_b_q_norm_g, b_w_o=v_b_w_o)
    names = list(weights)

    S, D = x.shape[1], x.shape[2]
    L, NA, NB = ada_w.shape[0], a_w_in.shape[0], b_w_qg.shape[0]
    H = kv_b_f.shape[0]
    hd = D // H
    G, CH = a_w_s.shape[1], a_w_s.shape[2]
    GW = a_w_out.shape[1] * N_CHIPS
    F = ffn_w_down.shape[1] * N_CHIPS
    ada_cols = ada_w.shape[2]
    kvada_cols = kv_ada_w.shape[1]
    kv_cols = kv_w.shape[1]
    kv_pad = -(-(2 * D + H) // LANES) * LANES
    xi, yi, ci = _place()
    chip = 2 * xi + yi
    me = 2 * chip + ci
    x0 = x[0]
    tgt = loss_target[0]
    row = lambda t: t.reshape(1, -1)

    c_all = _own_slot(_gather8([c], "gather_c")[0], c, me).reshape(N_DEV, D)
    c_act = _silu_rows(jnp.pad(c_all, ((0, BF16_ROWS - N_DEV), (0, 0))), "silu_c")
    mod_sh = [_mm(c_act, (ada_w, l), "nn", F32, f"mod_proj_{l}") for l in range(L)]
    mod_sh.append(_mm(c_act, kv_ada_w, "nn", F32, "mod_proj_kv"))
    mod_sh = jnp.concatenate(mod_sh, axis=1)
    small_sh = [mod_sh, a_b_in, a_ln_g, a_ln_b]
    mod_all, b_in_all, ln_g_all, ln_b_all = [
        _own_slot(o, s, chip) for o, s in zip(_chip_gather(small_sh, False, "gather_mod"), small_sh)]
    mine = lax.dynamic_index_in_dim(mod_all, me, axis=1, keepdims=False)
    mod = [jnp.concatenate([mine[j, l * ada_cols:(l + 1) * ada_cols] for j in range(N_CHIPS)]) + ada_b[l] for l in range(L)]
    mod = [[row(t) for t in jnp.split(m_, 6)] for m_ in mod]
    mod_kv = jnp.concatenate([mine[j, L * ada_cols:] for j in range(N_CHIPS)]) + kv_ada_b
    kv_sh, kv_sc = [row(t) for t in jnp.split(mod_kv, 2)]
    cat_chips = lambda t, ax: jnp.concatenate([t[j] for j in range(N_CHIPS)], axis=ax)
    b_in_f = cat_chips(b_in_all, 1)
    ln_g_f, ln_b_f = cat_chips(ln_g_all, 1), cat_chips(ln_b_all, 1)

    big = ["ffn_w_gu", "ffn_w_down", "a_w_in", "a_w_out", "kv_w", "b_w_qg", "b_w_o"]
    own_w = [weights[n].astype(BF16) for n in big]
    gu_order = (0, 2, 1, 3)
    places = {"ffn_w_gu": (2, gu_order), "ffn_w_down": (1, CHIP_ORDER), "a_w_in": (2, CHIP_ORDER),
              "a_w_out": (1, CHIP_ORDER), "kv_w": None, "b_w_qg": (2, CHIP_ORDER), "b_w_o": (1, CHIP_ORDER)}
    plist = [places[n] for n in big]
    full_w = dict(zip(big, _sibling_fill(_chip_gather(own_w, True, "gather_w", plist), own_w, plist, "fill_w")))
    gu_hw = ffn_w_gu.shape[2]
    w_gu, w_dn, w_in, w_out = full_w["ffn_w_gu"], full_w["ffn_w_down"], full_w["a_w_in"], full_w["a_w_out"]
    w_qg, w_o = full_w["b_w_qg"], full_w["b_w_o"]
    w_kv = jnp.pad(cat_chips(full_w["kv_w"], 1), ((0, 0), (0, kv_pad - (2 * D + H))))

    causal = jnp.tril(jnp.ones((CH, CH), F32))
    ws_m = [(a_w_s[i] * causal).astype(BF16) for i in range(NA)]
    ws_mt = [jnp.swapaxes(w, 1, 2) for w in ws_m]
    bs_t = [a_b_s[i].T for i in range(NA)]

    heads = lambda t: t.reshape(S, H, hd).transpose(1, 0, 2)
    unheads = lambda t: t.transpose(1, 0, 2).reshape(S, D)

    saved = []
    kv = None
    xc = x0
    h1 = _norm_mod_fwd(xc, row(pre_mix_g[0]), mod[0][0], mod[0][1], "pre_mix_0")
    for l in range(L):
        sh_m, sc_m, g_m, sh_f, sc_f, g_f = mod[l]
        st = {"x0": xc, "h1": h1}
        if l < NA:
            zp = _mm(h1, (w_in, l), "nn", BF16, f"gmlp_in_{l}")
            yg = _gmlp_fwd(zp, row(b_in_f[l]), row(ln_g_f[l]), row(ln_b_f[l]), ws_m[l], bs_t[l], f"gmlp_gate_{l}")
            y = _mm(yg, (w_out, l), "nn", F32, f"gmlp_out_{l}")
            st.update(zp=zp, yg=yg)
        else:
            jb = l - NA
            qg = _mm(h1, (w_qg, jb), "nn", BF16, f"fox_qg_{jb}")
            q_raw = heads(qg[:, :D]).reshape(H * S, hd)
            qn = _head_norm_fwd(q_raw, row(b_q_norm_g[jb]), hd ** -0.5, f"fox_qnorm_{jb}").reshape(H, S, hd)
            qa = _augment(qn, kv["dcum"], True)
            qat = jnp.swapaxes(qa, 1, 2)
            o_tr, lse_r = _flash_fwd(kv["ka"], qat, kv["vat"], hd, f"fox_attn_{jb}")
            o_t = o_tr.transpose(2, 0, 1).reshape(S, D)
            og = _out_gate_fwd(o_t, qg, f"fox_gate_{jb}")
            y = _mm(og, (w_o, jb), "nn", F32, f"fox_out_{jb}")
            st.update(qg=qg, q_raw=q_raw, qat=qat, o_tr=o_tr, lse_r=lse_r, o_t=o_t, og=og)
        st["y"] = y
        x1, h2 = _post_pre_fwd(xc, y, row(post_mix_g[l]), g_m, row(pre_ffn_g[l]), sh_f, sc_f, f"post_mix_{l}")
        st["x1"] = x1
        gu, act = _ffn_up(h2, w_gu, l, gu_hw, f"ffn_gu_{l}")
        y2 = _mm(act, (w_dn, l), "nn", F32, f"ffn_down_{l}")
        if l + 1 < L:
            xc, h1 = _post_pre_fwd(x1, y2, row(post_ffn_g[l]), g_f, row(pre_mix_g[l + 1]), mod[l + 1][0], mod[l + 1][1],
                                   f"post_ffn_{l}")
        else:
            xc = _post_fwd(x1, y2, row(post_ffn_g[l]), g_f, f"post_ffn_{l}")
        st.update(h2=h2, gu=gu, act=act, y2=y2)
        saved.append(st)
        if l == NA - 1:
            hk = _norm_mod_fwd(xc, row(kv_norm_g), kv_sh, kv_sc, "kv_pre")
            kvf = _mm(hk, w_kv, "nn", F32, "kv_proj")
            k_raw = heads(kvf[:, :D]).reshape(H * S, hd)
            kn = _head_norm_fwd(k_raw, row(k_norm_g), 1.0, "kv_knorm").reshape(H, S, hd)
            vb = heads(kvf[:, D:2 * D]).astype(BF16)
            f_t = kvf[:, 2 * D:2 * D + H].T
            b_col = kv_b_f.reshape(H, 1)
            dcum = _dcum_fwd(f_t, b_col, "kv_dcum")
            vt = kvf[:, D:2 * D].astype(BF16).reshape(S, H, hd).transpose(1, 2, 0)
            vat = jnp.where(lax.broadcasted_iota(jnp.int32, (1, LANES, 1), 1) == hd, jnp.asarray(1, BF16),
                            jnp.pad(vt, ((0, 0), (0, LANES - hd), (0, 0))))
            ka = _augment(kn, dcum, False)
            kv = dict(x=xc, hk=hk, k_raw=k_raw, ka=ka, kat=jnp.swapaxes(ka, 1, 2), vb=vb, vat=vat,
                      f_t=f_t, b_col=b_col, dcum=dcum)

    dx, loss_part = _loss_bwd(xc, tgt, "loss")

    gl = {n: [None] * weights[n].shape[0] for n in
          ["pre_mix_g", "post_mix_g", "pre_ffn_g", "post_ffn_g", "ffn_w_gu", "ffn_w_down", "a_w_in", "a_b_in", "a_ln_g",
           "a_ln_b", "a_w_s", "a_b_s", "a_w_out", "b_w_qg", "b_q_norm_g", "b_w_o"]}
    dmod = [None] * L
    dkn = dvb = ddc = None
    gkv = {}
    for l in reversed(range(L)):
        st = saved[l]
        sh_m, sc_m, g_m, sh_f, sc_f, g_f = mod[l]
        if l == NA - 1:
            dk_raw, gkv["k_norm_g"] = _head_norm_bwd(jnp.swapaxes(dkn, 1, 2).reshape(H * S, hd), kv["k_raw"], row(k_norm_g),
                                                     1.0, "kv_knorm_bwd")
            df_t, db_f = _dcum_bwd(ddc.reshape(H, S), kv["f_t"], kv["b_col"], "kv_dcum_bwd")
            dkvf = jnp.concatenate([unheads(dk_raw.reshape(H, S, hd)), dvb.transpose(2, 0, 1).reshape(S, D), df_t.T,
                                    jnp.zeros((S, kv_pad - (2 * D + H)), F32)], axis=1).astype(BF16)
            gkv["kv_w"] = _mm(kv["hk"], dkvf, "tn", BF16, "kv_proj_dw")[:, :2 * D + H]
            dhk = _mm(dkvf, w_kv, "nt", F32, "kv_proj_dx")
            dx, gkv["kv_norm_g"], dsh, dsc = _norm_mod_bwd(dx, dhk, kv["x"], row(kv_norm_g), kv_sh, kv_sc, "kv_pre_bwd")
            gkv["kv_b_f"] = db_f.reshape(H)
            dmod_kv = jnp.concatenate([dsh, dsc], axis=1)
        dy2, gl["post_ffn_g"][l], dg_f = _post_bwd(dx, st["y2"], row(post_ffn_g[l]), g_f, f"post_ffn_bwd_{l}")
        gl["ffn_w_down"][l] = _mm(st["act"], dy2, "tn", BF16, f"ffn_down_dw_{l}")
        dgu = _ffn_down_dx(dy2, w_dn, l, st["gu"], gu_hw, f"ffn_down_dx_{l}")
        gl["ffn_w_gu"][l] = _mm(st["h2"], dgu, "tn", BF16, f"ffn_gu_dw_{l}")
        dh2 = _mm(dgu, (w_gu, l), "nt", F32, f"ffn_gu_dx_{l}")
        dx, dy, gl["pre_ffn_g"][l], dsh_f, dsc_f, gl["post_mix_g"][l], dg_m = _pre_post_bwd(
            dx, dh2, st["x1"], row(pre_ffn_g[l]), sh_f, sc_f, st["y"], row(post_mix_g[l]), g_m, f"pre_ffn_bwd_{l}")
        if l < NA:
            gl["a_w_out"][l] = _mm(st["yg"], dy, "tn", BF16, f"gmlp_out_dw_{l}")
            dyg = _mm(dy, (w_out, l), "nt", BF16, f"gmlp_out_dx_{l}")
            dzp, db_in, dlg, dlb, dws, dbs_t = _gmlp_bwd(dyg, st["zp"], row(b_in_f[l]), row(ln_g_f[l]), row(ln_b_f[l]),
                                                           ws_m[l], ws_mt[l], bs_t[l], f"gmlp_gate_bwd_{l}")
            gl["a_b_in"][l], gl["a_ln_g"][l], gl["a_ln_b"][l] = db_in[0], dlg[0], dlb[0]
            gl["a_w_s"][l], gl["a_b_s"][l] = dws * causal, dbs_t.T
            gl["a_w_in"][l] = _mm(st["h1"], dzp, "tn", BF16, f"gmlp_in_dw_{l}")
            dh1 = _mm(dzp, (w_in, l), "nt", F32, f"gmlp_in_dx_{l}")
        else:
            jb = l - NA
            gl["b_w_o"][jb] = _mm(st["og"], dy, "tn", BF16, f"fox_out_dw_{jb}")
            dog = _mm(dy, (w_o, jb), "nt", F32, f"fox_out_dx_{jb}")
            do_t, dgl = _out_gate_bwd(dog, st["o_t"], st["qg"], f"fox_gate_bwd_{jb}")
            dot = do_t.reshape(S, H, hd).transpose(1, 2, 0)
            dqa_tr, dka_tr, dv_j = _flash_bwd(kv["ka"], kv["kat"], st["qat"], kv["vb"], dot, st["o_tr"], st["lse_r"],
                                              f"fox_attn_bwd_{jb}")
            dqn = dqa_tr[:, :, :hd, :].transpose(0, 1, 3, 2).reshape(H, S, hd)
            dk_j = dka_tr[:, :hd, :]
            dd_j = dqa_tr[:, :, hd, :].reshape(H, S) - dka_tr[:, hd + 3, :]
            dkn = dk_j if dkn is None else dkn + dk_j
            dvb = dv_j if dvb is None else dvb + dv_j
            ddc = dd_j if ddc is None else ddc + dd_j
            dq_raw, dgq = _head_norm_bwd(dqn.reshape(H * S, hd), st["q_raw"], row(b_q_norm_g[jb]), hd ** -0.5, f"fox_qnorm_bwd_{jb}")
            gl["b_q_norm_g"][jb] = dgq[0]
            dqg = jnp.concatenate([unheads(dq_raw.reshape(H, S, hd)).astype(BF16), dgl], axis=1)
            gl["b_w_qg"][jb] = _mm(st["h1"], dqg, "tn", BF16, f"fox_qg_dw_{jb}")
            dh1 = _mm(dqg, (w_qg, jb), "nt", F32, f"fox_qg_dx_{jb}")
        dx, gl["pre_mix_g"][l], dsh_m, dsc_m = _norm_mod_bwd(dx, dh1, st["x0"], row(pre_mix_g[l]), sh_m, sc_m, f"pre_mix_bwd_{l}")
        dmod[l] = jnp.concatenate([dsh_m, dsc_m, dg_m, dsh_f, dsc_f, dg_f], axis=1)
    grad_x = dx[None]

    stack = lambda n: jnp.stack([t.reshape(weights[n].shape[1:]) for t in gl[n]])
    small = {"dmod": jnp.concatenate(dmod, axis=1), "dmod_kv": dmod_kv}
    for n in ["pre_mix_g", "post_mix_g", "pre_ffn_g", "post_ffn_g", "a_w_s", "a_b_s", "b_q_norm_g"]:
        small[n] = stack(n)
    for n in ["a_b_in", "a_ln_g", "a_ln_b"]:
        small[n] = jnp.stack(gl[n])
    for n in ["kv_norm_g", "kv_b_f", "k_norm_g"]:
        small[n] = gkv[n]
    small["loss"] = loss_part
    sizes = {n: t.size for n, t in small.items()}
    flat = jnp.concatenate([t.reshape(-1).astype(F32) for t in small.values()])
    rows_small = -(-flat.size // (LANES * BF16_ROWS)) * BF16_ROWS
    flat = jnp.pad(flat, (0, rows_small * LANES - flat.size)).reshape(rows_small, LANES)
    flat_all = _own_slot(_gather8([flat], "gather_small")[0], flat, me)
    flat_sum = _sum_slots(flat_all, "sum_small").reshape(-1)
    offs, o_ = {}, 0
    for n, sz in sizes.items():
        offs[n] = o_
        o_ += sz
    take = lambda n, shape: flat_sum[offs[n]:offs[n] + sizes[n]].reshape(shape)
    dmod_rows = flat_all.reshape(N_DEV, -1)[:, offs["dmod"]:offs["dmod"] + sizes["dmod"] + sizes["dmod_kv"]]
    dmod_rows = jnp.pad(dmod_rows, ((0, BF16_ROWS - N_DEV), (0, 0)))

    grads = {}
    loss = take("loss", ())
    grads["ada_b"] = take("dmod", (L, 6 * D))
    grads["kv_ada_b"] = take("dmod_kv", (2 * D,))
    for n in ["pre_mix_g", "post_mix_g", "pre_ffn_g", "post_ffn_g", "a_w_s", "a_b_s", "b_q_norm_g", "kv_norm_g", "kv_b_f", "k_norm_g"]:
        grads[n] = take(n, weights[n].shape)
    for n in ["a_b_in", "a_ln_g", "a_ln_b"]:
        full = take(n, small[n].shape)
        w = weights[n].shape[1]
        grads[n] = lax.dynamic_slice_in_dim(full, chip * w, w, axis=1)
    ada_g = []
    for l in range(L):
        cols = lax.dynamic_slice_in_dim(dmod_rows[:, l * 6 * D:(l + 1) * 6 * D], chip * ada_cols, ada_cols, axis=1)
        ada_g.append(_mm(c_act, cols, "tn", F32, f"mod_proj_dw_{l}"))
    grads["ada_w"] = jnp.stack(ada_g)
    cols = lax.dynamic_slice_in_dim(dmod_rows[:, L * 6 * D:], chip * kvada_cols, kvada_cols, axis=1)
    grads["kv_ada_w"] = _mm(c_act, cols, "tn", F32, "mod_proj_kv_dw")

    specs = {"ffn_w_gu": (2, 0, (0, 2, 1, 3)),
             "ffn_w_down": (1, 0, CHIP_ORDER), "a_w_in": (2, 0, CHIP_ORDER), "a_w_out": (1, 0, CHIP_ORDER),
             "kv_w": (0, 1, CHIP_ORDER), "b_w_qg": (2, 0, CHIP_ORDER), "b_w_o": (1, 0, CHIP_ORDER)}
    full_g = {n: jnp.stack(gl[n]) for n in big if n != "kv_w"}
    full_g["kv_w"] = gkv["kv_w"].reshape(D, N_CHIPS, kv_cols).transpose(1, 0, 2)
    from_core = _sibling_scatter([full_g[n] for n in big], [specs[n] for n in big], "scatter_g_core")
    chip_sums = [_sum_pair(_own_pieces(full_g[n], specs[n], ci), r, f"sum_g_core_{n}") for n, r in zip(big, from_core)]
    recv = _chip_scatter(chip_sums, "scatter_g_chip")
    recv = [_own_slot(r, lax.dynamic_index_in_dim(p, chip, 0, keepdims=False), chip) for r, p in zip(recv, chip_sums)]
    halves = [_sum_slots(r, f"sum_g_{n}") for n, r in zip(big, recv)]
    pairs = _sibling_pair(halves, "pair_g")
    for n, p, hlf in zip(big, pairs, halves):
        grads[n] = _own_slot(p, hlf, ci).reshape(weights[n].shape)

    outs_d, outs_m, outs_v = {}, {}, {}
    for n in names:
        w2 = weights[n] if weights[n].ndim > 1 else weights[n].reshape(1, -1)
        shp = w2.shape
        d_, m_, v_ = _adamw(w2, grads[n].reshape(shp), m_in[n].reshape(shp), v_in[n].reshape(shp), f"adamw_{n}")
        outs_d[n], outs_m[n], outs_v[n] = (t.reshape(weights[n].shape) for t in (d_, m_, v_))
    return (loss, grad_x, *[grads[n] for n in names], *[outs_d[n] for n in names],
            *[outs_m[n] for n in names], *[outs_v[n] for n in names])
```

```python
import jax
import jax.numpy as jnp
from jax import lax
from jax.experimental import pallas as pl
from jax.experimental.pallas import tpu as pltpu

F32 = jnp.float32
BF16 = jnp.bfloat16
MESH = pl.DeviceIdType.MESH
NORM_EPS = 1e-6
MASKED = -1e30
LANES = 128
BF16_ROWS = 16
ROW_BLOCK_BYTES = 12 << 20
ADAM_LR, ADAM_B1, ADAM_B2, ADAM_EPS, ADAM_WD, ADAM_STEP = 0.001, 0.9, 0.999, 1e-08, 0.01, 10
N_CHIPS, N_CORES, N_DEV = 4, 2, 8
ATTN_HEADS_PER_STEP = 4
ATTN_FWD_HEADS_PER_STEP = 8
ATTN_STAGED_HEADS = 2
ANY = pl.BlockSpec(memory_space=pl.ANY)


def _tile(n, cap, quantum):
    best = None
    d = quantum
    while d <= min(n, cap):
        if n % d == 0:
            best = d
        d += quantum
    return n if best is None else best


def _call(body, *, name, out_shape, grid=(), in_specs=None, out_specs=None, scratch=(), sem=None, aliases=None):
    params = {} if sem is None else {"dimension_semantics": sem}
    return pl.pallas_call(
        body, name=name, grid=grid, in_specs=in_specs, out_specs=out_specs, out_shape=out_shape,
        scratch_shapes=list(scratch), input_output_aliases=aliases or {},
        compiler_params=pltpu.CompilerParams(**params))


def _call_prefetch(body, *, name, out_shape, grid, n_prefetch, in_specs, out_specs, scratch, sem):
    spec = pltpu.PrefetchScalarGridSpec(num_scalar_prefetch=n_prefetch, grid=grid, in_specs=in_specs,
                                        out_specs=out_specs, scratch_shapes=list(scratch))
    return pl.pallas_call(
        body, name=name, grid_spec=spec, out_shape=out_shape,
        compiler_params=pltpu.CompilerParams(dimension_semantics=sem))


def _place():
    x, y, c = lax.axis_index("x"), lax.axis_index("y"), lax.axis_index("c")
    return x, y, c


def _mm(a, b, mode, out_dtype, name):
    b_arr, b_idx = b if isinstance(b, tuple) else (b, None)
    bs = b_arr.shape[-2:]
    if mode == "nn":
        (M, K), (K2, N) = a.shape, bs
        dims = (((1,), (0,)), ((), ()))
    elif mode == "nt":
        (M, K), (N, K2) = a.shape, bs
        dims = (((1,), (1,)), ((), ()))
    else:
        (K, M), (K2, N) = a.shape, bs
        dims = (((0,), (0,)), ((), ()))
    assert K == K2, (name, a.shape, b_arr.shape)
    if mode == "tn":
        tm = _tile(M, 1408, LANES)
        tk = _tile(K, 2048, BF16_ROWS)
        tn = _tile(N, 512, LANES)
    else:
        tm = _tile(M, 1024, BF16_ROWS)
        tk = K if K <= 2816 else _tile(K, 2816, LANES)
        tn = _tile(N, 1408 if tk <= 1024 else 512, LANES)
    if tn < 256:
        tn = N
        tm = _tile(M, 512, LANES if mode == "tn" else BF16_ROWS)
    nk = K // tk
    grid = (M // tm, N // tn, nk)

    if mode == "tn":
        a_spec = pl.BlockSpec((tk, tm), lambda i, j, k: (k, i))
    else:
        a_spec = pl.BlockSpec((tm, tk), lambda i, j, k: (i, k))
    if mode == "nt":
        b_blk, b_map = (tn, tk), (lambda i, j, k: (j, k))
    else:
        b_blk, b_map = (tk, tn), (lambda i, j, k: (k, j))
    if b_idx is None:
        b_spec = pl.BlockSpec(b_blk, b_map)
    else:
        b_spec = pl.BlockSpec((None,) + b_blk, lambda i, j, k: (b_idx,) + b_map(i, j, k))

    def body(a_ref, b_ref, o_ref, *acc):
        r = lax.dot_general(a_ref[...].astype(BF16), b_ref[...].astype(BF16), dims, preferred_element_type=F32)
        if nk == 1:
            o_ref[...] = r.astype(o_ref.dtype)
        else:
            k = pl.program_id(2)

            @pl.when(k == 0)
            def _():
                acc[0][...] = r

            @pl.when(k > 0)
            def _():
                acc[0][...] += r

            @pl.when(k == nk - 1)
            def _():
                o_ref[...] = acc[0][...].astype(o_ref.dtype)

    return _call(
        body, name=name, grid=grid, in_specs=[a_spec, b_spec],
        out_specs=pl.BlockSpec((tm, tn), lambda i, j, k: (i, j)),
        out_shape=jax.ShapeDtypeStruct((M, N), out_dtype),
        scratch=[pltpu.VMEM((tm, tn), F32)] if nk > 1 else [],
        sem=("parallel", "parallel", "arbitrary"))(a, b_arr)


def _rowwise(fn, rows, pars, outs, pouts, name):
    R = rows[0].shape[0]
    row_bytes = 4 * (sum(max(r.shape[1], LANES) for r in rows) + sum(max(c, LANES) for c, _ in outs))
    tb = _tile(R, max(BF16_ROWS, ROW_BLOCK_BYTES // row_bytes), BF16_ROWS)
    nr, npar, no = len(rows), len(pars), len(outs)

    def body(*refs):
        r_in, p_in = refs[:nr], refs[nr:nr + npar]
        r_out, p_out = refs[nr + npar:nr + npar + no], refs[nr + npar + no:]
        ro, po = fn([r[...] for r in r_in], [p[...] for p in p_in])
        for ref, val in zip(r_out, ro):
            if isinstance(val, (tuple, list)):
                off = 0
                for piece in val:
                    w = piece.shape[1]
                    ref[:, off:off + w] = piece.astype(ref.dtype)
                    off += w
            else:
                ref[...] = val.astype(ref.dtype)
        if p_out:
            first = pl.program_id(0) == 0

            @pl.when(first)
            def _():
                for ref, val in zip(p_out, po):
                    ref[...] = val

            @pl.when(jnp.logical_not(first))
            def _():
                for ref, val in zip(p_out, po):
                    ref[...] += val

    res = _call(
        body, name=name, grid=(R // tb,),
        in_specs=[pl.BlockSpec((tb, r.shape[1]), lambda i: (i, 0)) for r in rows]
        + [pl.BlockSpec(p.shape, lambda i: (0, 0)) for p in pars],
        out_specs=[pl.BlockSpec((tb, c), lambda i: (i, 0)) for c, _ in outs]
        + [pl.BlockSpec(s, lambda i: (0, 0)) for s in pouts],
        out_shape=[jax.ShapeDtypeStruct((R, c), dt) for c, dt in outs]
        + [jax.ShapeDtypeStruct(s, F32) for s in pouts],
        sem=("arbitrary",) if pouts else ("parallel",))(*rows, *pars)
    return list(res)


def _rms(x, g):
    return x * lax.rsqrt(jnp.mean(x * x, axis=-1, keepdims=True) + NORM_EPS) * g


def _norm_mod(x, g, sh, sc):
    return _rms(x, g) * (1.0 + sc) + sh


def _gated_post(y, g, gate):
    return gate * _rms(y, g)


def _norm_mod_fwd(x, g, sh, sc, name):
    return _rowwise(lambda r, p: ([_norm_mod(r[0], *p)], []), [x], [g, sh, sc], [(x.shape[1], BF16)], [], name)[0]


def _norm_mod_bwd(dxo, dh, x, g, sh, sc, name):
    def fn(r, p):
        _, vjp = jax.vjp(_norm_mod, r[2], *p)
        dx, dg, dsh, dsc = vjp(r[1].astype(F32))
        return [r[0] + dx], [dg, dsh, dsc]
    c = x.shape[1]
    return _rowwise(fn, [dxo, dh, x], [g, sh, sc], [(c, F32)], [(1, c)] * 3, name)


def _post_fwd(x, y, g, gate, name):
    return _rowwise(lambda r, p: ([r[0] + _gated_post(r[1].astype(F32), *p)], []), [x, y], [g, gate],
                    [(x.shape[1], F32)], [], name)[0]


def _post_bwd(dxo, y, g, gate, name):
    def fn(r, p):
        _, vjp = jax.vjp(_gated_post, r[1].astype(F32), *p)
        dy, dg, dgate = vjp(r[0])
        return [dy], [dg, dgate]
    c = y.shape[1]
    return _rowwise(fn, [dxo, y], [g, gate], [(c, BF16)], [(1, c)] * 2, name)


def _post_pre_fwd(x, y, g_post, gate, g_pre, sh, sc, name):
    def fn(r, p):
        x1 = r[0] + _gated_post(r[1].astype(F32), p[0], p[1])
        return [x1, _norm_mod(x1, p[2], p[3], p[4])], []
    c = x.shape[1]
    return _rowwise(fn, [x, y], [g_post, gate, g_pre, sh, sc], [(c, F32), (c, BF16)], [], name)


def _pre_post_bwd(dxo, dh, x, g_pre, sh, sc, y, g_post, gate, name):
    def fn(r, p):
        _, vjp_pre = jax.vjp(_norm_mod, r[2], p[0], p[1], p[2])
        dxn, dg_pre, dsh, dsc = vjp_pre(r[1].astype(F32))
        dx = r[0] + dxn
        _, vjp_post = jax.vjp(_gated_post, r[3].astype(F32), p[3], p[4])
        dy, dg_post, dgate = vjp_post(dx)
        return [dx, dy], [dg_pre, dsh, dsc, dg_post, dgate]
    c = x.shape[1]
    return _rowwise(fn, [dxo, dh, x, y], [g_pre, sh, sc, g_post, gate], [(c, F32), (c, BF16)], [(1, c)] * 5, name)


def _swiglu(g, u):
    return jax.nn.silu(g) * u


def _ffn_up(h, w, l, hw, name):
    S, D = h.shape
    nb = w.shape[2] // (2 * hw)
    tm = _tile(S, 512, BF16_ROWS)

    def body(h_ref, w_ref, gu_ref, act_ref):
        gu = jnp.dot(h_ref[...], w_ref[...], preferred_element_type=F32).astype(BF16)
        gu_ref[...] = gu
        act_ref[...] = _swiglu(gu[:, :hw].astype(F32), gu[:, hw:].astype(F32)).astype(BF16)

    return _call(body, name=name, grid=(S // tm, nb),
                 in_specs=[pl.BlockSpec((tm, D), lambda i, j: (i, 0)), pl.BlockSpec((None, D, 2 * hw), lambda i, j: (l, 0, j))],
                 out_specs=[pl.BlockSpec((tm, 2 * hw), lambda i, j: (i, j)), pl.BlockSpec((tm, hw), lambda i, j: (i, j))],
                 out_shape=[jax.ShapeDtypeStruct((S, 2 * hw * nb), BF16), jax.ShapeDtypeStruct((S, hw * nb), BF16)],
                 sem=("parallel", "parallel"))(h, w)


def _ffn_down_dx(dy, w_dn, l, gu, hw, name):
    S, D = dy.shape
    nb = gu.shape[1] // (2 * hw)
    tm = _tile(S, 512, BF16_ROWS)

    def body(dy_ref, w_ref, gu_ref, dgu_ref):
        dact = lax.dot_general(dy_ref[...], w_ref[...], (((1,), (1,)), ((), ())), preferred_element_type=F32)
        _, vjp = jax.vjp(_swiglu, gu_ref[:, :hw].astype(F32), gu_ref[:, hw:].astype(F32))
        dg, du = vjp(dact)
        dgu_ref[:, :hw] = dg.astype(BF16)
        dgu_ref[:, hw:] = du.astype(BF16)

    return _call(body, name=name, grid=(S // tm, nb),
                 in_specs=[pl.BlockSpec((tm, D), lambda i, j: (i, 0)), pl.BlockSpec((None, hw, D), lambda i, j: (l, j, 0)),
                           pl.BlockSpec((tm, 2 * hw), lambda i, j: (i, j))],
                 out_specs=pl.BlockSpec((tm, 2 * hw), lambda i, j: (i, j)),
                 out_shape=jax.ShapeDtypeStruct(gu.shape, BF16), sem=("parallel", "parallel"))(dy, w_dn, gu)


def _silu_rows(c, name):
    return _rowwise(lambda r, p: ([jax.nn.silu(r[0])], []), [c], [], [(c.shape[1], F32)], [], name)[0]


def _head_norm(x, g, scale):
    return _rms(x, g) * scale


def _head_norm_fwd(x, g, scale, name):
    return _rowwise(lambda r, p: ([_head_norm(r[0].astype(F32), p[0], scale)], []), [x], [g],
                    [(x.shape[1], BF16)], [], name)[0]


def _head_norm_bwd(dy, x, g, scale, name):
    def fn(r, p):
        _, vjp = jax.vjp(lambda t, gg: _head_norm(t, gg, scale), r[1].astype(F32), p[0])
        dx, dg = vjp(r[0])
        return [dx], [dg]
    c = x.shape[1]
    return _rowwise(fn, [dy, x], [g], [(c, F32)], [(1, c)], name)


def _out_gate_fwd(o, qg, name):
    d = o.shape[1]
    return _rowwise(lambda r, p: ([r[0] * jax.nn.sigmoid(r[1][:, d:].astype(F32))], []), [o, qg], [],
                    [(d, BF16)], [], name)[0]


def _out_gate_bwd(dog, o, qg, name):
    d = o.shape[1]

    def fn(r, p):
        _, vjp = jax.vjp(lambda oo, gl: oo * jax.nn.sigmoid(gl), r[1], r[2][:, d:].astype(F32))
        do, dgl = vjp(r[0])
        return [do, dgl], []
    return _rowwise(fn, [dog, o, qg], [], [(d, BF16), (d, BF16)], [], name)


def _loss_bwd(y, tgt, name):
    n = y.shape[1]

    def fn(r, p):
        e = r[0] - r[1]
        part = jnp.sum(jnp.sum(e * e, axis=1, keepdims=True), axis=0, keepdims=True) * (0.5 / n)
        return [e * (1.0 / n)], [part]
    return _rowwise(fn, [y, tgt], [], [(n, F32)], [(1, 1)], name)


def _adamw(w, g, m, v, name):
    shape = w.shape
    c = shape[-1]
    flat = [t.reshape(-1, c) for t in (w, g, m, v)]

    def fn(r, p):
        w_, g_, m_, v_ = r
        m2 = ADAM_B1 * m_ + (1.0 - ADAM_B1) * g_
        v2 = ADAM_B2 * v_ + (1.0 - ADAM_B2) * (g_ * g_)
        m_hat = m2 / (1.0 - ADAM_B1 ** ADAM_STEP)
        v_hat = v2 / (1.0 - ADAM_B2 ** ADAM_STEP)
        delta = -ADAM_LR * (m_hat / (jnp.sqrt(v_hat) + ADAM_EPS) + ADAM_WD * w_)
        return [delta, m2, v2], []
    res = _rowwise(fn, flat, [], [(c, F32)] * 3, [], name)
    return [t.reshape(shape) for t in res]


def _sum_pair(a, b, name):
    c = a.shape[-1]
    out = _rowwise(lambda r, p: ([r[0].astype(F32) + r[1].astype(F32)], []), [a.reshape(-1, c), b.reshape(-1, c)], [],
                   [(c, BF16)], [], name)[0]
    return out.reshape(a.shape)


def _sum_slots(recv, name):
    n = recv.shape[0]
    shape = recv.shape[1:]
    c = shape[-1]
    r3 = recv.reshape(n, -1, c)
    rows = r3.shape[1]
    tb = _tile(rows, max(BF16_ROWS, ROW_BLOCK_BYTES // (4 * c * (n + 1))), BF16_ROWS)

    def body(r_ref, o_ref):
        acc = r_ref[0].astype(F32)
        for s in range(1, n):
            acc = acc + r_ref[s].astype(F32)
        o_ref[...] = acc

    out = _call(body, name=name, grid=(rows // tb,),
                in_specs=[pl.BlockSpec((n, tb, c), lambda i: (0, i, 0))],
                out_specs=pl.BlockSpec((tb, c), lambda i: (i, 0)),
                out_shape=jax.ShapeDtypeStruct((rows, c), F32), sem=("parallel",))(r3)
    return out.reshape(shape)


def _gmlp_pre(zu, zv, b_u, b_v, ln_g, ln_b):
    u = jax.nn.gelu(zu + b_u, approximate=True)
    v = jax.nn.gelu(zv + b_v, approximate=True)
    xc = v - jnp.mean(v, axis=-1, keepdims=True)
    vn = xc * lax.rsqrt(jnp.mean(xc * xc, axis=-1, keepdims=True) + NORM_EPS) * ln_g + ln_b
    return u, vn


def _gmlp_fwd(zp, b_in, ln_g, ln_b, ws, bs_t, name):
    S, gw2 = zp.shape
    gw = gw2 // 2
    G, ch, _ = ws.shape
    gd = gw // G
    tb = 2 * ch

    def body(zp_ref, bin_ref, lg_ref, lb_ref, ws_ref, bs_ref, o_ref):
        u, vn = _gmlp_pre(zp_ref[:, :gw].astype(F32), zp_ref[:, gw:].astype(F32), bin_ref[:, :gw], bin_ref[:, gw:],
                          lg_ref[...], lb_ref[...])
        vnb = vn.astype(BF16)
        for c in range(tb // ch):
            for g in range(G):
                rs, cs = slice(c * ch, (c + 1) * ch), slice(g * gd, (g + 1) * gd)
                vv = jnp.dot(ws_ref[g], vnb[rs, cs], preferred_element_type=F32) + bs_ref[:, g:g + 1]
                o_ref[rs, cs] = (u[rs, cs] * vv).astype(o_ref.dtype)

    full = lambda a: pl.BlockSpec(a.shape, lambda i: (0,) * a.ndim)
    return _call(body, name=name, grid=(S // tb,),
                 in_specs=[pl.BlockSpec((tb, gw2), lambda i: (i, 0)), full(b_in), full(ln_g), full(ln_b), full(ws), full(bs_t)],
                 out_specs=pl.BlockSpec((tb, gw), lambda i: (i, 0)),
                 out_shape=jax.ShapeDtypeStruct((S, gw), BF16), sem=("parallel",))(zp, b_in, ln_g, ln_b, ws, bs_t)


def _gmlp_bwd(dyg, zp, b_in, ln_g, ln_b, ws, ws_t, bs_t, name):
    S, gw2 = zp.shape
    gw = gw2 // 2
    G, ch, _ = ws.shape
    gd = gw // G
    tb = 2 * ch

    def body(dy_ref, zp_ref, bin_ref, lg_ref, lb_ref, ws_ref, wst_ref, bs_ref,
             dzp_ref, dbin_ref, dlg_ref, dlb_ref, dws_ref, dbs_ref, du_sc, dvn_sc):
        (u, vn), vjp = jax.vjp(_gmlp_pre, zp_ref[:, :gw].astype(F32), zp_ref[:, gw:].astype(F32), bin_ref[:, :gw],
                               bin_ref[:, gw:], lg_ref[...], lb_ref[...])
        vnb = vn.astype(BF16)
        first = pl.program_id(0) == 0

        @pl.when(first)
        def _():
            dws_ref[...] = jnp.zeros_like(dws_ref)

        lane = lax.broadcasted_iota(jnp.int32, (ch, G), 1)
        dbs = jnp.zeros((ch, G), F32)
        for g in range(G):
            cs = slice(g * gd, (g + 1) * gd)
            dws_g = jnp.zeros((ch, ch), F32)
            col = jnp.zeros((ch, 1), F32)
            for c in range(tb // ch):
                rs = slice(c * ch, (c + 1) * ch)
                vnp = vnb[rs, cs]
                vv = jnp.dot(ws_ref[g], vnp, preferred_element_type=F32) + bs_ref[:, g:g + 1]
                dy = dy_ref[rs, cs].astype(F32)
                du_sc[rs, cs] = dy * vv
                dvv = dy * u[rs, cs]
                dvvb = dvv.astype(BF16)
                dvn_sc[rs, cs] = jnp.dot(wst_ref[g], dvvb, preferred_element_type=F32)
                dws_g = dws_g + lax.dot_general(dvvb, vnp, (((1,), (1,)), ((), ())), preferred_element_type=F32)
                col = col + jnp.sum(dvv, axis=1, keepdims=True)
            dws_ref[g] += dws_g
            dbs = jnp.where(lane == g, col, dbs)
        dzu, dzv, dbu, dbv, dlg, dlb = vjp((du_sc[...], dvn_sc[...]))
        dzp_ref[:, :gw] = dzu.astype(dzp_ref.dtype)
        dzp_ref[:, gw:] = dzv.astype(dzp_ref.dtype)

        @pl.when(first)
        def _():
            dbin_ref[:, :gw] = dbu
            dbin_ref[:, gw:] = dbv
            dlg_ref[...] = dlg
            dlb_ref[...] = dlb
            dbs_ref[...] = dbs

        @pl.when(jnp.logical_not(first))
        def _():
            dbin_ref[:, :gw] += dbu
            dbin_ref[:, gw:] += dbv
            dlg_ref[...] += dlg
            dlb_ref[...] += dlb
            dbs_ref[...] += dbs

    full = lambda a: pl.BlockSpec(a.shape, lambda i: (0,) * a.ndim)
    fshape = lambda s: pl.BlockSpec(s, lambda i: (0,) * len(s))
    return _call(
        body, name=name, grid=(S // tb,),
        in_specs=[pl.BlockSpec((tb, gw), lambda i: (i, 0)), pl.BlockSpec((tb, gw2), lambda i: (i, 0)),
                  full(b_in), full(ln_g), full(ln_b), full(ws), full(ws_t), full(bs_t)],
        out_specs=[pl.BlockSpec((tb, gw2), lambda i: (i, 0)), fshape((1, gw2)), fshape((1, gw)), fshape((1, gw)),
                   fshape((G, ch, ch)), fshape((ch, G))],
        out_shape=[jax.ShapeDtypeStruct((S, gw2), BF16), jax.ShapeDtypeStruct((1, gw2), F32),
                   jax.ShapeDtypeStruct((1, gw), F32), jax.ShapeDtypeStruct((1, gw), F32),
                   jax.ShapeDtypeStruct((G, ch, ch), F32), jax.ShapeDtypeStruct((ch, G), F32)],
        scratch=[pltpu.VMEM((tb, gw), F32), pltpu.VMEM((tb, gw), F32)],
        sem=("arbitrary",))(dyg, zp, b_in, ln_g, ln_b, ws, ws_t, bs_t)


def _dot_01(x, ones_bf16):
    hi = x.astype(BF16)
    r1 = x - hi.astype(F32)
    mid = r1.astype(BF16)
    lo = (r1 - mid.astype(F32)).astype(BF16)
    dot = lambda t: jnp.dot(t, ones_bf16, preferred_element_type=F32)
    return dot(hi) + dot(mid) + dot(lo)


def _log_sigmoid(x):
    return jnp.minimum(x, 0.0) - jnp.log1p(jnp.exp(-jnp.abs(x)))


def _dcum_fwd(f_t, b_col, name):
    H, S = f_t.shape
    tb = _tile(S, 512, LANES)

    def body(f_ref, b_ref, o_ref, carry):
        @pl.when(pl.program_id(0) == 0)
        def _():
            carry[...] = jnp.zeros_like(carry)

        ls = _log_sigmoid(f_ref[...] + b_ref[...])
        r = lax.broadcasted_iota(jnp.int32, (tb, tb), 0)
        c = lax.broadcasted_iota(jnp.int32, (tb, tb), 1)
        upper = (r <= c).astype(BF16)
        o_ref[...] = _dot_01(ls, upper) + carry[...]
        carry[...] += jnp.sum(ls, axis=1, keepdims=True)

    return _call(body, name=name, grid=(S // tb,),
                 in_specs=[pl.BlockSpec((H, tb), lambda i: (0, i)), pl.BlockSpec((H, 1), lambda i: (0, 0))],
                 out_specs=pl.BlockSpec((H, tb), lambda i: (0, i)),
                 out_shape=jax.ShapeDtypeStruct((H, S), F32),
                 scratch=[pltpu.VMEM((H, 1), F32)], sem=("arbitrary",))(f_t, b_col)


def _dcum_bwd(dd_t, f_t, b_col, name):
    H, S = f_t.shape
    tb = _tile(S, 512, LANES)
    nb = S // tb

    def body(dd_ref, f_ref, b_ref, df_ref, db_ref, carry):
        first = pl.program_id(0) == 0

        @pl.when(first)
        def _():
            carry[...] = jnp.zeros_like(carry)

        dd = dd_ref[...]
        r = lax.broadcasted_iota(jnp.int32, (tb, tb), 0)
        c = lax.broadcasted_iota(jnp.int32, (tb, tb), 1)
        lower = (r >= c).astype(BF16)
        rev = _dot_01(dd, lower) + carry[...]
        carry[...] += jnp.sum(dd, axis=1, keepdims=True)
        df = rev * jax.nn.sigmoid(-(f_ref[...] + b_ref[...]))
        df_ref[...] = df
        part = jnp.sum(df, axis=1, keepdims=True)

        @pl.when(first)
        def _():
            db_ref[...] = part

        @pl.when(jnp.logical_not(first))
        def _():
            db_ref[...] += part

    return _call(body, name=name, grid=(nb,),
                 in_specs=[pl.BlockSpec((H, tb), lambda i: (0, nb - 1 - i)), pl.BlockSpec((H, tb), lambda i: (0, nb - 1 - i)),
                           pl.BlockSpec((H, 1), lambda i: (0, 0))],
                 out_specs=[pl.BlockSpec((H, tb), lambda i: (0, nb - 1 - i)), pl.BlockSpec((H, 1), lambda i: (0, 0))],
                 out_shape=[jax.ShapeDtypeStruct((H, S), F32), jax.ShapeDtypeStruct((H, 1), F32)],
                 scratch=[pltpu.VMEM((H, 1), F32)], sem=("arbitrary",))(dd_t, f_t, b_col)


def _attn_tile(S):
    return _tile(S, 512, LANES)


def _causal_t(t):
    return lax.broadcasted_iota(jnp.int32, (t, t), 0) <= lax.broadcasted_iota(jnp.int32, (t, t), 1)


def _tri_pairs(n, key_major):
    if key_major:
        pairs = [(i, j) for j in range(n) for i in range(j, n)]
    else:
        pairs = [(i, j) for i in range(n) for j in range(i + 1)]
    return jnp.asarray([p[0] for p in pairs], jnp.int32), jnp.asarray([p[1] for p in pairs], jnp.int32)


def _split3(x):
    hi = lax.reduce_precision(x, 8, 7)
    r = x - hi
    mid = lax.reduce_precision(r, 8, 7)
    lo = lax.reduce_precision(r - mid, 8, 7)
    return hi.astype(BF16), mid.astype(BF16), lo.astype(BF16)


def _augment(xn, dcum, query):
    H, S, hd = xn.shape
    parts = list(_split3(dcum))
    vals = parts + [1.0] * 3 if query else [1.0] * 3 + [-p for p in parts]
    lane = lax.broadcasted_iota(jnp.int32, (1, 1, LANES), 2)
    out = jnp.pad(xn, ((0, 0), (0, 0), (0, LANES - hd)))
    for k, val in enumerate(vals):
        val = jnp.asarray(val, BF16)
        out = jnp.where(lane == hd + k, val[..., None] if val.ndim else val, out)
    return out


def _scores_t(k_ref, qt_ref, h, t, diag):
    st = jnp.dot(k_ref[h], qt_ref[h], preferred_element_type=F32)
    return jnp.where(_causal_t(t), st, MASKED) if diag else st


def _flash_fwd(ka, qat, vat, hd, name):
    H, S, da = ka.shape
    t = _attn_tile(S)
    hb = min(H, ATTN_FWD_HEADS_PER_STEP)
    it, jt = _tri_pairs(S // t, False)

    def body(it_ref, jt_ref, k_ref, qt_ref, vt_ref, o_ref, lse_ref, m_sc, acc_sc):
        i, j = it_ref[pl.program_id(1)], jt_ref[pl.program_id(1)]

        @pl.when(j == 0)
        def _():
            m_sc[...] = jnp.full_like(m_sc, MASKED)
            acc_sc[...] = jnp.zeros_like(acc_sc)

        def step(diag):
            sts = [_scores_t(k_ref, qt_ref, h, t, diag) for h in range(hb)]
            pts, alphas = [], []
            for h in range(hb):
                m_prev = m_sc[h]
                m_new = jnp.maximum(m_prev, jnp.max(sts[h], axis=0, keepdims=True))
                pts.append(jnp.exp(sts[h] - m_new).astype(BF16))
                alphas.append(jnp.exp(m_prev - m_new))
                m_sc[h] = m_new
            for h in range(hb):
                acc_sc[h] = alphas[h] * acc_sc[h] + jnp.dot(vt_ref[h], pts[h], preferred_element_type=F32)

        @pl.when(j < i)
        def _():
            step(False)

        @pl.when(j == i)
        def _():
            step(True)
            for h in range(hb):
                l = acc_sc[h, hd:hd + 1, :]
                o_ref[h] = acc_sc[h, :hd, :] / l
                lse_ref[h] = m_sc[h] + jnp.log(l)

    qcol = lambda h, p, it_, jt_: (h, 0, it_[p])
    kcol = lambda h, p, it_, jt_: (h, 0, jt_[p])
    krow = lambda h, p, it_, jt_: (h, jt_[p], 0)
    return _call_prefetch(
        body, name=name, grid=(H // hb, it.shape[0]), n_prefetch=2,
        in_specs=[pl.BlockSpec((hb, t, da), krow), pl.BlockSpec((hb, da, t), qcol), pl.BlockSpec((hb, da, t), kcol)],
        out_specs=[pl.BlockSpec((hb, hd, t), qcol), pl.BlockSpec((hb, 1, t), qcol)],
        out_shape=[jax.ShapeDtypeStruct((H, hd, S), F32), jax.ShapeDtypeStruct((H, 1, S), F32)],
        scratch=[pltpu.VMEM((hb, 1, t), F32), pltpu.VMEM((hb, da, t), F32)],
        sem=("parallel", "arbitrary"))(it, jt, ka, qat, vat)


def _flash_bwd(ka, kat, qat, v, dot, o_tr, lse_r, name):
    H, S, hd = v.shape
    da = ka.shape[2]
    t = _attn_tile(S)
    n = S // t
    hb = ATTN_HEADS_PER_STEP
    it, jt = _tri_pairs(n, True)
    over_queries = (((1,), (1,)), ((), ()))

    def body(it_ref, jt_ref, k_ref, kt_ref, qt_ref, v_ref, dot_ref, o_ref, lse_ref, dq_ref, dk_ref, dv_ref, dk_sc, dv_sc):
        i, j = it_ref[pl.program_id(1)], jt_ref[pl.program_id(1)]

        @pl.when(pl.program_id(1) == 0)
        def _():
            dq_ref[...] = jnp.zeros_like(dq_ref)

        def step(diag):
            for h0 in range(0, hb, ATTN_STAGED_HEADS):
                hs = range(h0, min(h0 + ATTN_STAGED_HEADS, hb))
                sts = [_scores_t(k_ref, qt_ref, h, t, diag) for h in hs]
                dpts = [jnp.dot(v_ref[h], dot_ref[h], preferred_element_type=F32) for h in hs]
                tiles = []
                for h, st, dpt in zip(hs, sts, dpts):
                    dl = jnp.sum(dot_ref[h].astype(F32) * o_ref[h], axis=0, keepdims=True)
                    pt = jnp.exp((st - lse_ref[h]).astype(BF16))
                    tiles.append((pt, pt * (dpt - dl).astype(BF16)))
                for h, (ptb, dsb) in zip(hs, tiles):
                    dv_sc[h] += lax.dot_general(dot_ref[h], ptb, over_queries, preferred_element_type=F32)
                    dk_sc[h] += lax.dot_general(qt_ref[h], dsb, over_queries, preferred_element_type=F32)
                    dq_ref[h, i] += jnp.dot(kt_ref[h], dsb, preferred_element_type=F32)

        @pl.when(i == j)
        def _():
            dk_sc[...] = jnp.zeros_like(dk_sc)
            dv_sc[...] = jnp.zeros_like(dv_sc)
            step(True)

        @pl.when(i > j)
        def _():
            step(False)

        @pl.when(i == n - 1)
        def _():
            dk_ref[...] = dk_sc[...]
            dv_ref[...] = dv_sc[...]

    krow = lambda h, p, it_, jt_: (h, jt_[p], 0)
    kcol = lambda h, p, it_, jt_: (h, 0, jt_[p])
    qcol = lambda h, p, it_, jt_: (h, 0, it_[p])
    return _call_prefetch(
        body, name=name, grid=(H // hb, it.shape[0]), n_prefetch=2,
        in_specs=[pl.BlockSpec((hb, t, da), krow), pl.BlockSpec((hb, da, t), kcol), pl.BlockSpec((hb, da, t), qcol),
                  pl.BlockSpec((hb, t, hd), krow), pl.BlockSpec((hb, hd, t), qcol), pl.BlockSpec((hb, hd, t), qcol),
                  pl.BlockSpec((hb, 1, t), qcol)],
        out_specs=[pl.BlockSpec((hb, n, da, t), lambda h, p, it_, jt_: (h, 0, 0, 0)), pl.BlockSpec((hb, da, t), kcol),
                   pl.BlockSpec((hb, hd, t), kcol)],
        out_shape=[jax.ShapeDtypeStruct((H, n, da, t), F32), jax.ShapeDtypeStruct((H, da, S), F32),
                   jax.ShapeDtypeStruct((H, hd, S), F32)],
        scratch=[pltpu.VMEM((hb, da, t), F32), pltpu.VMEM((hb, hd, t), F32)],
        sem=("parallel", "arbitrary"))(it, jt, ka, kat, qat, v, dot, o_tr, lse_r)


def _offsets(n_bits):
    return [tuple((k >> b) & 1 for b in reversed(range(n_bits))) for k in range(1, 1 << n_bits)]


def _own_slot(out, own, idx):
    return lax.dynamic_update_index_in_dim(out, own.astype(out.dtype), idx, 0)


def _gather8(arrs, name):
    n = len(arrs)
    offs = _offsets(3)

    def body(*refs):
        ins, outs = refs[:n], refs[n:2 * n]
        ssem, rsem = refs[2 * n:]
        x, y, c = _place()
        me = 4 * x + 2 * y + c
        copies = []
        for a in range(n):
            for k, (dx, dy, dcc) in enumerate(offs):
                cp = pltpu.make_async_remote_copy(
                    src_ref=ins[a], dst_ref=outs[a].at[me], send_sem=ssem.at[a, k], recv_sem=rsem.at[a, k],
                    device_id=((x + dx) % 2, (y + dy) % 2, (c + dcc) % 2), device_id_type=MESH)
                cp.start()
                copies.append(cp)
        for cp in copies:
            cp.wait()

    return _call(body, name=name, in_specs=[ANY] * n, out_specs=[ANY] * n,
                 out_shape=[jax.ShapeDtypeStruct((N_DEV,) + a.shape, a.dtype) for a in arrs],
                 scratch=[pltpu.SemaphoreType.DMA((n, 7)), pltpu.SemaphoreType.DMA((n, 7))])(*arrs)


def _chip_position(chip, place):
    return sum(jnp.where(chip == s, place[1].index(s), 0) for s in range(N_CHIPS))


def _shard_slot(ref, shard_shape, place, chip, c=None):
    hn = shard_shape[0] // 2
    half = slice(None) if c is None else pl.ds(c * hn, hn)
    if place is None:
        return ref.at[chip, half]
    ax = place[0]
    w = shard_shape[ax]
    idx = [slice(None)] * len(shard_shape)
    idx[0] = half
    idx[ax] = pl.ds(pl.multiple_of(_chip_position(chip, place) * w, LANES if ax == len(shard_shape) - 1 else BF16_ROWS), w)
    return ref.at[tuple(idx)]


def _gathered_shape(shard_shape, place):
    if place is None:
        return (N_CHIPS,) + tuple(shard_shape)
    s = list(shard_shape)
    s[place[0]] *= N_CHIPS
    return tuple(s)


def _chip_gather(arrs, halved, name, places=None):
    n = len(arrs)
    offs = _offsets(2)
    places = places or [None] * n

    def body(*refs):
        ins, outs = refs[:n], refs[n:2 * n]
        ssem, rsem = refs[2 * n:]
        x, y, c = _place()
        chip = 2 * x + y
        copies = []
        for a in range(n):
            if halved:
                hn = arrs[a].shape[0] // 2
                src = ins[a].at[pl.ds(c * hn, hn)]
                dst = _shard_slot(outs[a], arrs[a].shape, places[a], chip, c)
            else:
                src, dst = ins[a], outs[a].at[chip]
            for k, (dx, dy) in enumerate(offs):
                cp = pltpu.make_async_remote_copy(
                    src_ref=src, dst_ref=dst, send_sem=ssem.at[a, k], recv_sem=rsem.at[a, k],
                    device_id=((x + dx) % 2, (y + dy) % 2, c), device_id_type=MESH)
                cp.start()
                copies.append(cp)
        for cp in copies:
            cp.wait()

    return _call(body, name=name, in_specs=[ANY] * n, out_specs=[ANY] * n,
                 out_shape=[jax.ShapeDtypeStruct(_gathered_shape(a.shape, p) if halved else (N_CHIPS,) + a.shape, a.dtype)
                            for a, p in zip(arrs, places)],
                 scratch=[pltpu.SemaphoreType.DMA((n, 3)), pltpu.SemaphoreType.DMA((n, 3))])(*arrs)


def _sibling_fill(bufs, owns, places, name):
    n = len(bufs)
    offs = _offsets(2)

    def body(*refs):
        ins, own, outs = refs[:n], refs[n:2 * n], refs[2 * n:3 * n]
        ssem, rsem = refs[3 * n:]
        x, y, c = _place()
        copies = []
        for a in range(n):
            shape = owns[a].shape
            for k, (dx, dy) in enumerate(offs):
                chip = 2 * ((x + dx) % 2) + (y + dy) % 2
                cp = pltpu.make_async_remote_copy(
                    src_ref=_shard_slot(ins[a], shape, places[a], chip, c),
                    dst_ref=_shard_slot(outs[a], shape, places[a], chip, c),
                    send_sem=ssem.at[a, k], recv_sem=rsem.at[a, k],
                    device_id=(x, y, 1 - c), device_id_type=MESH)
                cp.start()
                copies.append(cp)
            cp = pltpu.make_async_remote_copy(
                src_ref=own[a], dst_ref=_shard_slot(outs[a], shape, places[a], 2 * x + y),
                send_sem=ssem.at[a, 3], recv_sem=rsem.at[a, 3], device_id=(x, y, 1 - c), device_id_type=MESH)
            cp.start()
            copies.append(cp)
        for cp in copies:
            cp.wait()

    return _call(body, name=name, in_specs=[ANY] * (2 * n), out_specs=[ANY] * n,
                 out_shape=[jax.ShapeDtypeStruct(b.shape, b.dtype) for b in bufs],
                 scratch=[pltpu.SemaphoreType.DMA((n, 4)), pltpu.SemaphoreType.DMA((n, 4))],
                 aliases={a: a for a in range(n)})(*bufs, *owns)


def _sibling_pair(arrs, name):
    n = len(arrs)

    def body(*refs):
        ins, outs = refs[:n], refs[n:2 * n]
        ssem, rsem = refs[2 * n:]
        x, y, c = _place()
        copies = []
        for a in range(n):
            cp = pltpu.make_async_remote_copy(
                src_ref=ins[a], dst_ref=outs[a].at[c], send_sem=ssem.at[a], recv_sem=rsem.at[a],
                device_id=(x, y, 1 - c), device_id_type=MESH)
            cp.start()
            copies.append(cp)
        for cp in copies:
            cp.wait()

    return _call(body, name=name, in_specs=[ANY] * n, out_specs=[ANY] * n,
                 out_shape=[jax.ShapeDtypeStruct((N_CORES,) + a.shape, a.dtype) for a in arrs],
                 scratch=[pltpu.SemaphoreType.DMA((n,)), pltpu.SemaphoreType.DMA((n,))])(*arrs)


CHIP_ORDER = (0, 1, 2, 3)


def _piece(shape, spec, j, h):
    shard_ax, half_ax, order = spec
    w = shape[shard_ax] // N_CHIPS
    idx = [slice(None)] * len(shape)
    idx[shard_ax] = pl.ds(order[j] * w, w)
    assert half_ax != shard_ax
    hn = shape[half_ax] // 2
    idx[half_ax] = pl.ds(h * hn, hn)
    return tuple(idx)


def _piece_shape(shape, spec):
    shard_ax, half_ax, _ = spec
    s = list(shape)
    s[shard_ax] //= N_CHIPS
    s[half_ax] //= 2
    return tuple(s)


def _own_pieces(g, spec, c):
    shard_ax, half_ax, order = spec
    hn = g.shape[half_ax] // 2
    half = lax.dynamic_slice_in_dim(g, c * hn, hn, axis=half_ax)
    w = half.shape[shard_ax] // N_CHIPS
    return jnp.stack([lax.slice_in_dim(half, p * w, (p + 1) * w, axis=shard_ax) for p in order])


def _sibling_scatter(arrs, specs, name):
    n = len(arrs)

    def body(*refs):
        ins, outs = refs[:n], refs[n:2 * n]
        ssem, rsem = refs[2 * n:]
        x, y, c = _place()
        for mine in range(N_CORES):
            @pl.when(c == mine)
            def _():
                copies = []
                for a in range(n):
                    for j in range(N_CHIPS):
                        cp = pltpu.make_async_remote_copy(
                            src_ref=ins[a].at[_piece(arrs[a].shape, specs[a], j, 1 - mine)], dst_ref=outs[a].at[j],
                            send_sem=ssem.at[a, j], recv_sem=rsem.at[a, j],
                            device_id=(x, y, 1 - mine), device_id_type=MESH)
                        cp.start()
                        copies.append(cp)
                for cp in copies:
                    cp.wait()

    return _call(body, name=name, in_specs=[ANY] * n, out_specs=[ANY] * n,
                 out_shape=[jax.ShapeDtypeStruct((N_CHIPS,) + _piece_shape(a.shape, s), a.dtype)
                            for a, s in zip(arrs, specs)],
                 scratch=[pltpu.SemaphoreType.DMA((n, N_CHIPS))] * 2)(*arrs)


def _chip_scatter(arrs, name):
    n = len(arrs)
    offs = _offsets(2)

    def body(*refs):
        ins, outs = refs[:n], refs[n:2 * n]
        ssem, rsem = refs[2 * n:]
        x, y, c = _place()
        chip = 2 * x + y
        copies = []
        for a in range(n):
            for k, (dx, dy) in enumerate(offs):
                tx, ty = (x + dx) % 2, (y + dy) % 2
                cp = pltpu.make_async_remote_copy(
                    src_ref=ins[a].at[2 * tx + ty], dst_ref=outs[a].at[chip], send_sem=ssem.at[a, k], recv_sem=rsem.at[a, k],
                    device_id=(tx, ty, c), device_id_type=MESH)
                cp.start()
                copies.append(cp)
        for cp in copies:
            cp.wait()

    return _call(body, name=name, in_specs=[ANY] * n, out_specs=[ANY] * n,
                 out_shape=[jax.ShapeDtypeStruct(a.shape, a.dtype) for a in arrs],
                 scratch=[pltpu.SemaphoreType.DMA((n, 3)), pltpu.SemaphoreType.DMA((n, 3))])(*arrs)


def kernel(x, c, ada_w, ada_b, pre_mix_g, post_mix_g, pre_ffn_g, post_ffn_g, ffn_w_gu, ffn_w_down, a_w_in, a_b_in, a_ln_g, a_ln_b, a_w_s, a_b_s, a_w_out, kv_ada_w, kv_ada_b, kv_norm_g, kv_w, kv_b_f, k_norm_g, b_w_qg, b_q_norm_g, b_w_o, loss_target, m_ada_w, m_ada_b, m_pre_mix_g, m_post_mix_g, m_pre_ffn_g, m_post_ffn_g, m_ffn_w_gu, m_ffn_w_down, m_a_w_in, m_a_b_in, m_a_ln_g, m_a_ln_b, m_a_w_s, m_a_b_s, m_a_w_out, m_kv_ada_w, m_kv_ada_b, m_kv_norm_g, m_kv_w, m_kv_b_f, m_k_norm_g, m_b_w_qg, m_b_q_norm_g, m_b_w_o, v_ada_w, v_ada_b, v_pre_mix_g, v_post_mix_g, v_pre_ffn_g, v_post_ffn_g, v_ffn_w_gu, v_ffn_w_down, v_a_w_in, v_a_b_in, v_a_ln_g, v_a_ln_b, v_a_w_s, v_a_b_s, v_a_w_out, v_kv_ada_w, v_kv_ada_b, v_kv_norm_g, v_kv_w, v_kv_b_f, v_k_norm_g, v_b_w_qg, v_b_q_norm_g, v_b_w_o):
    weights = dict(ada_w=ada_w, ada_b=ada_b, pre_mix_g=pre_mix_g, post_mix_g=post_mix_g, pre_ffn_g=pre_ffn_g,
                   post_ffn_g=post_ffn_g, ffn_w_gu=ffn_w_gu, ffn_w_down=ffn_w_down, a_w_in=a_w_in, a_b_in=a_b_in,
                   a_ln_g=a_ln_g, a_ln_b=a_ln_b, a_w_s=a_w_s, a_b_s=a_b_s, a_w_out=a_w_out, kv_ada_w=kv_ada_w,
                   kv_ada_b=kv_ada_b, kv_norm_g=kv_norm_g, kv_w=kv_w, kv_b_f=kv_b_f, k_norm_g=k_norm_g, b_w_qg=b_w_qg,
                   b_q_norm_g=b_q_norm_g, b_w_o=b_w_o)
    m_in = dict(ada_w=m_ada_w, ada_b=m_ada_b, pre_mix_g=m_pre_mix_g, post_mix_g=m_post_mix_g, pre_ffn_g=m_pre_ffn_g,
                post_ffn_g=m_post_ffn_g, ffn_w_gu=m_ffn_w_gu, ffn_w_down=m_ffn_w_down, a_w_in=m_a_w_in, a_b_in=m_a_b_in,
                a_ln_g=m_a_ln_g, a_ln_b=m_a_ln_b, a_w_s=m_a_w_s, a_b_s=m_a_b_s, a_w_out=m_a_w_out, kv_ada_w=m_kv_ada_w,
                kv_ada_b=m_kv_ada_b, kv_norm_g=m_kv_norm_g, kv_w=m_kv_w, kv_b_f=m_kv_b_f, k_norm_g=m_k_norm_g,
                b_w_qg=m_b_w_qg, b_q_norm_g=m_b_q_norm_g, b_w_o=m_b_w_o)
    v_in = dict(ada_w=v_ada_w, ada_b=v_ada_b, pre_mix_g=v_pre_mix_g, post_mix_g=v_post_mix_g, pre_ffn_g=v_pre_ffn_g,
                post_ffn_g=v_post_ffn_g, ffn_w_gu=v_ffn_w_gu, ffn_w_down=v_ffn_w_down, a_w_in=v_a_w_in, a_b_in=v_a_b_in,
                a_ln_g=v_a_ln_g, a_ln_b=v_a_ln_b, a_w_s=v_a_w_s, a_b_s=v_a_b_s, a_w_out=v_a_w_out, kv_ada_w=v_kv_ada_w,
                kv_ada_b=v_kv_ada_b, kv_norm_g=v_kv_norm_g, kv_w=v_kv_w, kv_b_f=v_kv_b_f, k_norm_g=v_k_norm_g,
                b_w_qg=v_b_w_qg, b_q_norm_g=v_b_q_norm_g, b_w_o=v_b_w_o)
    names = list(weights)

    S, D = x.shape[1], x.shape[2]
    L, NA, NB = ada_w.shape[0], a_w_in.shape[0], b_w_qg.shape[0]
    H = kv_b_f.shape[0]
    hd = D // H
    G, CH = a_w_s.shape[1], a_w_s.shape[2]
    GW = a_w_out.shape[1] * N_CHIPS
    F = ffn_w_down.shape[1] * N_CHIPS
    ada_cols = ada_w.shape[2]
    kvada_cols = kv_ada_w.shape[1]
    kv_cols = kv_w.shape[1]
    kv_pad = -(-(2 * D + H) // LANES) * LANES
    xi, yi, ci = _place()
    chip = 2 * xi + yi
    me = 2 * chip + ci
    x0 = x[0]
    tgt = loss_target[0]
    row = lambda t: t.reshape(1, -1)

    c_all = _own_slot(_gather8([c], "gather_c")[0], c, me).reshape(N_DEV, D)
    c_act = _silu_rows(jnp.pad(c_all, ((0, BF16_ROWS - N_DEV), (0, 0))), "silu_c")
    mod_sh = [_mm(c_act, (ada_w, l), "nn", F32, f"mod_proj_{l}") for l in range(L)]
    mod_sh.append(_mm(c_act, kv_ada_w, "nn", F32, "mod_proj_kv"))
    mod_sh = jnp.concatenate(mod_sh, axis=1)
    small_sh = [mod_sh, a_b_in, a_ln_g, a_ln_b]
    mod_all, b_in_all, ln_g_all, ln_b_all = [
        _own_slot(o, s, chip) for o, s in zip(_chip_gather(small_sh, False, "gather_mod"), small_sh)]
    mine = lax.dynamic_index_in_dim(mod_all, me, axis=1, keepdims=False)
    mod = [jnp.concatenate([mine[j, l * ada_cols:(l + 1) * ada_cols] for j in range(N_CHIPS)]) + ada_b[l] for l in range(L)]
    mod = [[row(t) for t in jnp.split(m_, 6)] for m_ in mod]
    mod_kv = jnp.concatenate([mine[j, L * ada_cols:] for j in range(N_CHIPS)]) + kv_ada_b
    kv_sh, kv_sc = [row(t) for t in jnp.split(mod_kv, 2)]
    cat_chips = lambda t, ax: jnp.concatenate([t[j] for j in range(N_CHIPS)], axis=ax)
    b_in_f = cat_chips(b_in_all, 1)
    ln_g_f, ln_b_f = cat_chips(ln_g_all, 1), cat_chips(ln_b_all, 1)

    big = ["ffn_w_gu", "ffn_w_down", "a_w_in", "a_w_out", "kv_w", "b_w_qg", "b_w_o"]
    own_w = [weights[n].astype(BF16) for n in big]
    gu_order = (0, 2, 1, 3)
    places = {"ffn_w_gu": (2, gu_order), "ffn_w_down": (1, CHIP_ORDER), "a_w_in": (2, CHIP_ORDER),
              "a_w_out": (1, CHIP_ORDER), "kv_w": None, "b_w_qg": (2, CHIP_ORDER), "b_w_o": (1, CHIP_ORDER)}
    plist = [places[n] for n in big]
    full_w = dict(zip(big, _sibling_fill(_chip_gather(own_w, True, "gather_w", plist), own_w, plist, "fill_w")))
    gu_hw = ffn_w_gu.shape[2]
    w_gu, w_dn, w_in, w_out = full_w["ffn_w_gu"], full_w["ffn_w_down"], full_w["a_w_in"], full_w["a_w_out"]
    w_qg, w_o = full_w["b_w_qg"], full_w["b_w_o"]
    w_kv = jnp.pad(cat_chips(full_w["kv_w"], 1), ((0, 0), (0, kv_pad - (2 * D + H))))

    causal = jnp.tril(jnp.ones((CH, CH), F32))
    ws_m = [(a_w_s[i] * causal).astype(BF16) for i in range(NA)]
    ws_mt = [jnp.swapaxes(w, 1, 2) for w in ws_m]
    bs_t = [a_b_s[i].T for i in range(NA)]

    heads = lambda t: t.reshape(S, H, hd).transpose(1, 0, 2)
    unheads = lambda t: t.transpose(1, 0, 2).reshape(S, D)

    saved = []
    kv = None
    xc = x0
    h1 = _norm_mod_fwd(xc, row(pre_mix_g[0]), mod[0][0], mod[0][1], "pre_mix_0")
    for l in range(L):
        sh_m, sc_m, g_m, sh_f, sc_f, g_f = mod[l]
        st = {"x0": xc, "h1": h1}
        if l < NA:
            zp = _mm(h1, (w_in, l), "nn", BF16, f"gmlp_in_{l}")
            yg = _gmlp_fwd(zp, row(b_in_f[l]), row(ln_g_f[l]), row(ln_b_f[l]), ws_m[l], bs_t[l], f"gmlp_gate_{l}")
            y = _mm(yg, (w_out, l), "nn", F32, f"gmlp_out_{l}")
            st.update(zp=zp, yg=yg)
        else:
            jb = l - NA
            qg = _mm(h1, (w_qg, jb), "nn", BF16, f"fox_qg_{jb}")
            q_raw = heads(qg[:, :D]).reshape(H * S, hd)
            qn = _head_norm_fwd(q_raw, row(b_q_norm_g[jb]), hd ** -0.5, f"fox_qnorm_{jb}").reshape(H, S, hd)
            qa = _augment(qn, kv["dcum"], True)
            qat = jnp.swapaxes(qa, 1, 2)
            o_tr, lse_r = _flash_fwd(kv["ka"], qat, kv["vat"], hd, f"fox_attn_{jb}")
            o_t = o_tr.transpose(2, 0, 1).reshape(S, D)
            og = _out_gate_fwd(o_t, qg, f"fox_gate_{jb}")
            y = _mm(og, (w_o, jb), "nn", F32, f"fox_out_{jb}")
            st.update(qg=qg, q_raw=q_raw, qat=qat, o_tr=o_tr, lse_r=lse_r, o_t=o_t, og=og)
        st["y"] = y
        x1, h2 = _post_pre_fwd(xc, y, row(post_mix_g[l]), g_m, row(pre_ffn_g[l]), sh_f, sc_f, f"post_mix_{l}")
        st["x1"] = x1
        gu, act = _ffn_up(h2, w_gu, l, gu_hw, f"ffn_gu_{l}")
        y2 = _mm(act, (w_dn, l), "nn", F32, f"ffn_down_{l}")
        if l + 1 < L:
            xc, h1 = _post_pre_fwd(x1, y2, row(post_ffn_g[l]), g_f, row(pre_mix_g[l + 1]), mod[l + 1][0], mod[l + 1][1],
                                   f"post_ffn_{l}")
        else:
            xc = _post_fwd(x1, y2, row(post_ffn_g[l]), g_f, f"post_ffn_{l}")
        st.update(h2=h2, gu=gu, act=act, y2=y2)
        saved.append(st)
        if l == NA - 1:
            hk = _norm_mod_fwd(xc, row(kv_norm_g), kv_sh, kv_sc, "kv_pre")
            kvf = _mm(hk, w_kv, "nn", F32, "kv_proj")
            k_raw = heads(kvf[:, :D]).reshape(H * S, hd)
            kn = _head_norm_fwd(k_raw, row(k_norm_g), 1.0, "kv_knorm").reshape(H, S, hd)
            vb = heads(kvf[:, D:2 * D]).astype(BF16)
            f_t = kvf[:, 2 * D:2 * D + H].T
            b_col = kv_b_f.reshape(H, 1)
            dcum = _dcum_fwd(f_t, b_col, "kv_dcum")
            vt = kvf[:, D:2 * D].astype(BF16).reshape(S, H, hd).transpose(1, 2, 0)
            vat = jnp.where(lax.broadcasted_iota(jnp.int32, (1, LANES, 1), 1) == hd, jnp.asarray(1, BF16),
                            jnp.pad(vt, ((0, 0), (0, LANES - hd), (0, 0))))
            ka = _augment(kn, dcum, False)
            kv = dict(x=xc, hk=hk, k_raw=k_raw, ka=ka, kat=jnp.swapaxes(ka, 1, 2), vb=vb, vat=vat,
                      f_t=f_t, b_col=b_col, dcum=dcum)

    dx, loss_part = _loss_bwd(xc, tgt, "loss")

    gl = {n: [None] * weights[n].shape[0] for n in
          ["pre_mix_g", "post_mix_g", "pre_ffn_g", "post_ffn_g", "ffn_w_gu", "ffn_w_down", "a_w_in", "a_b_in", "a_ln_g",
           "a_ln_b", "a_w_s", "a_b_s", "a_w_out", "b_w_qg", "b_q_norm_g", "b_w_o"]}
    dmod = [None] * L
    dkn = dvb = ddc = None
    gkv = {}
    for l in reversed(range(L)):
        st = saved[l]
        sh_m, sc_m, g_m, sh_f, sc_f, g_f = mod[l]
        if l == NA - 1:
            dk_raw, gkv["k_norm_g"] = _head_norm_bwd(jnp.swapaxes(dkn, 1, 2).reshape(H * S, hd), kv["k_raw"], row(k_norm_g),
                                                     1.0, "kv_knorm_bwd")
            df_t, db_f = _dcum_bwd(ddc.reshape(H, S), kv["f_t"], kv["b_col"], "kv_dcum_bwd")
            dkvf = jnp.concatenate([unheads(dk_raw.reshape(H, S, hd)), dvb.transpose(2, 0, 1).reshape(S, D), df_t.T,
                                    jnp.zeros((S, kv_pad - (2 * D + H)), F32)], axis=1).astype(BF16)
            gkv["kv_w"] = _mm(kv["hk"], dkvf, "tn", BF16, "kv_proj_dw")[:, :2 * D + H]
            dhk = _mm(dkvf, w_kv, "nt", F32, "kv_proj_dx")
            dx, gkv["kv_norm_g"], dsh, dsc = _norm_mod_bwd(dx, dhk, kv["x"], row(kv_norm_g), kv_sh, kv_sc, "kv_pre_bwd")
            gkv["kv_b_f"] = db_f.reshape(H)
            dmod_kv = jnp.concatenate([dsh, dsc], axis=1)
        dy2, gl["post_ffn_g"][l], dg_f = _post_bwd(dx, st["y2"], row(post_ffn_g[l]), g_f, f"post_ffn_bwd_{l}")
        gl["ffn_w_down"][l] = _mm(st["act"], dy2, "tn", BF16, f"ffn_down_dw_{l}")
        dgu = _ffn_down_dx(dy2, w_dn, l, st["gu"], gu_hw, f"ffn_down_dx_{l}")
        gl["ffn_w_gu"][l] = _mm(st["h2"], dgu, "tn", BF16, f"ffn_gu_dw_{l}")
        dh2 = _mm(dgu, (w_gu, l), "nt", F32, f"ffn_gu_dx_{l}")
        dx, dy, gl["pre_ffn_g"][l], dsh_f, dsc_f, gl["post_mix_g"][l], dg_m = _pre_post_bwd(
            dx, dh2, st["x1"], row(pre_ffn_g[l]), sh_f, sc_f, st["y"], row(post_mix_g[l]), g_m, f"pre_ffn_bwd_{l}")
        if l < NA:
            gl["a_w_out"][l] = _mm(st["yg"], dy, "tn", BF16, f"gmlp_out_dw_{l}")
            dyg = _mm(dy, (w_out, l), "nt", BF16, f"gmlp_out_dx_{l}")
            dzp, db_in, dlg, dlb, dws, dbs_t = _gmlp_bwd(dyg, st["zp"], row(b_in_f[l]), row(ln_g_f[l]), row(ln_b_f[l]),
                                                           ws_m[l], ws_mt[l], bs_t[l], f"gmlp_gate_bwd_{l}")
            gl["a_b_in"][l], gl["a_ln_g"][l], gl["a_ln_b"][l] = db_in[0], dlg[0], dlb[0]
            gl["a_w_s"][l], gl["a_b_s"][l] = dws * causal, dbs_t.T
            gl["a_w_in"][l] = _mm(st["h1"], dzp, "tn", BF16, f"gmlp_in_dw_{l}")
            dh1 = _mm(dzp, (w_in, l), "nt", F32, f"gmlp_in_dx_{l}")
        else:
            jb = l - NA
            gl["b_w_o"][jb] = _mm(st["og"], dy, "tn", BF16, f"fox_out_dw_{jb}")
            dog = _mm(dy, (w_o, jb), "nt", F32, f"fox_out_dx_{jb}")
            do_t, dgl = _out_gate_bwd(dog, st["o_t"], st["qg"], f"fox_gate_bwd_{jb}")
            dot = do_t.reshape(S, H, hd).transpose(1, 2, 0)
            dqa_tr, dka_tr, dv_j = _flash_bwd(kv["ka"], kv["kat"], st["qat"], kv["vb"], dot, st["o_tr"], st["lse_r"],
                                              f"fox_attn_bwd_{jb}")
            dqn = dqa_tr[:, :, :hd, :].transpose(0, 1, 3, 2).reshape(H, S, hd)
            dk_j = dka_tr[:, :hd, :]
            dd_j = dqa_tr[:, :, hd, :].reshape(H, S) - dka_tr[:, hd + 3, :]
            dkn = dk_j if dkn is None else dkn + dk_j
            dvb = dv_j if dvb is None else dvb + dv_j
            ddc = dd_j if ddc is None else ddc + dd_j
            dq_raw, dgq = _head_norm_bwd(dqn.reshape(H * S, hd), st["q_raw"], row(b_q_norm_g[jb]), hd ** -0.5, f"fox_qnorm_bwd_{jb}")
            gl["b_q_norm_g"][jb] = dgq[0]
            dqg = jnp.concatenate([unheads(dq_raw.reshape(H, S, hd)).astype(BF16), dgl], axis=1)
            gl["b_w_qg"][jb] = _mm(st["h1"], dqg, "tn", BF16, f"fox_qg_dw_{jb}")
            dh1 = _mm(dqg, (w_qg, jb), "nt", F32, f"fox_qg_dx_{jb}")
        dx, gl["pre_mix_g"][l], dsh_m, dsc_m = _norm_mod_bwd(dx, dh1, st["x0"], row(pre_mix_g[l]), sh_m, sc_m, f"pre_mix_bwd_{l}")
        dmod[l] = jnp.concatenate([dsh_m, dsc_m, dg_m, dsh_f, dsc_f, dg_f], axis=1)
    grad_x = dx[None]

    stack = lambda n: jnp.stack([t.reshape(weights[n].shape[1:]) for t in gl[n]])
    small = {"dmod": jnp.concatenate(dmod, axis=1), "dmod_kv": dmod_kv}
    for n in ["pre_mix_g", "post_mix_g", "pre_ffn_g", "post_ffn_g", "a_w_s", "a_b_s", "b_q_norm_g"]:
        small[n] = stack(n)
    for n in ["a_b_in", "a_ln_g", "a_ln_b"]:
        small[n] = jnp.stack(gl[n])
    for n in ["kv_norm_g", "kv_b_f", "k_norm_g"]:
        small[n] = gkv[n]
    small["loss"] = loss_part
    sizes = {n: t.size for n, t in small.items()}
    flat = jnp.concatenate([t.reshape(-1).astype(F32) for t in small.values()])
    rows_small = -(-flat.size // (LANES * BF16_ROWS)) * BF16_ROWS
    flat = jnp.pad(flat, (0, rows_small * LANES - flat.size)).reshape(rows_small, LANES)
    flat_all = _own_slot(_gather8([flat], "gather_small")[0], flat, me)
    flat_sum = _sum_slots(flat_all, "sum_small").reshape(-1)
    offs, o_ = {}, 0
    for n, sz in sizes.items():
        offs[n] = o_
        o_ += sz
    take = lambda n, shape: flat_sum[offs[n]:offs[n] + sizes[n]].reshape(shape)
    dmod_rows = flat_all.reshape(N_DEV, -1)[:, offs["dmod"]:offs["dmod"] + sizes["dmod"] + sizes["dmod_kv"]]
    dmod_rows = jnp.pad(dmod_rows, ((0, BF16_ROWS - N_DEV), (0, 0)))

    grads = {}
    loss = take("loss", ())
    grads["ada_b"] = take("dmod", (L, 6 * D))
    grads["kv_ada_b"] = take("dmod_kv", (2 * D,))
    for n in ["pre_mix_g", "post_mix_g", "pre_ffn_g", "post_ffn_g", "a_w_s", "a_b_s", "b_q_norm_g", "kv_norm_g", "kv_b_f", "k_norm_g"]:
        grads[n] = take(n, weights[n].shape)
    for n in ["a_b_in", "a_ln_g", "a_ln_b"]:
        full = take(n, small[n].shape)
        w = weights[n].shape[1]
        grads[n] = lax.dynamic_slice_in_dim(full, chip * w, w, axis=1)
    ada_g = []
    for l in range(L):
        cols = lax.dynamic_slice_in_dim(dmod_rows[:, l * 6 * D:(l + 1) * 6 * D], chip * ada_cols, ada_cols, axis=1)
        ada_g.append(_mm(c_act, cols, "tn", F32, f"mod_proj_dw_{l}"))
    grads["ada_w"] = jnp.stack(ada_g)
    cols = lax.dynamic_slice_in_dim(dmod_rows[:, L * 6 * D:], chip * kvada_cols, kvada_cols, axis=1)
    grads["kv_ada_w"] = _mm(c_act, cols, "tn", F32, "mod_proj_kv_dw")

    specs = {"ffn_w_gu": (2, 0, (0, 2, 1, 3)),
             "ffn_w_down": (1, 0, CHIP_ORDER), "a_w_in": (2, 0, CHIP_ORDER), "a_w_out": (1, 0, CHIP_ORDER),
             "kv_w": (0, 1, CHIP_ORDER), "b_w_qg": (2, 0, CHIP_ORDER), "b_w_o": (1, 0, CHIP_ORDER)}
    full_g = {n: jnp.stack(gl[n]) for n in big if n != "kv_w"}
    full_g["kv_w"] = gkv["kv_w"].reshape(D, N_CHIPS, kv_cols).transpose(1, 0, 2)
    from_core = _sibling_scatter([full_g[n] for n in big], [specs[n] for n in big], "scatter_g_core")
    chip_sums = [_sum_pair(_own_pieces(full_g[n], specs[n], ci), r, f"sum_g_core_{n}") for n, r in zip(big, from_core)]
    recv = _chip_scatter(chip_sums, "scatter_g_chip")
    recv = [_own_slot(r, lax.dynamic_index_in_dim(p, chip, 0, keepdims=False), chip) for r, p in zip(recv, chip_sums)]
    halves = [_sum_slots(r, f"sum_g_{n}") for n, r in zip(big, recv)]
    pairs = _sibling_pair(halves, "pair_g")
    for n, p, hlf in zip(big, pairs, halves):
        grads[n] = _own_slot(p, hlf, ci).reshape(weights[n].shape)

    outs_d, outs_m, outs_v = {}, {}, {}
    for n in names:
        w2 = weights[n] if weights[n].ndim > 1 else weights[n].reshape(1, -1)
        shp = w2.shape
        d_, m_, v_ = _adamw(w2, grads[n].reshape(shp), m_in[n].reshape(shp), v_in[n].reshape(shp), f"adamw_{n}")
        outs_d[n], outs_m[n], outs_v[n] = (t.reshape(weights[n].shape) for t in (d_, m_, v_))
    return (loss, grad_x, *[grads[n] for n in names], *[outs_d[n] for n in names],
            *[outs_m[n] for n in names], *[outs_v[n] for n in names])
```

```python
import jax
import jax.numpy as jnp
from jax import lax
from jax.experimental import pallas as pl
from jax.experimental.pallas import tpu as pltpu

F32 = jnp.float32
BF16 = jnp.bfloat16
MESH = pl.DeviceIdType.MESH
NORM_EPS = 1e-6
MASKED = -1e30
LANES = 128
BF16_ROWS = 16
ROW_BLOCK_BYTES = 12 << 20
ADAM_LR, ADAM_B1, ADAM_B2, ADAM_EPS, ADAM_WD, ADAM_STEP = 0.001, 0.9, 0.999, 1e-08, 0.01, 10
N_CHIPS, N_CORES, N_DEV = 4, 2, 8
ATTN_HEADS_PER_STEP = 4
ATTN_FWD_HEADS_PER_STEP = 8
ATTN_STAGED_HEADS = 2
ANY = pl.BlockSpec(memory_space=pl.ANY)


def _tile(n, cap, quantum):
    best = None
    d = quantum
    while d <= min(n, cap):
        if n % d == 0:
            best = d
        d += quantum
    return n if best is None else best


def _call(body, *, name, out_shape, grid=(), in_specs=None, out_specs=None, scratch=(), sem=None, aliases=None):
    params = {} if sem is None else {"dimension_semantics": sem}
    return pl.pallas_call(
        body, name=name, grid=grid, in_specs=in_specs, out_specs=out_specs, out_shape=out_shape,
        scratch_shapes=list(scratch), input_output_aliases=aliases or {},
        compiler_params=pltpu.CompilerParams(**params))


def _call_prefetch(body, *, name, out_shape, grid, n_prefetch, in_specs, out_specs, scratch, sem):
    spec = pltpu.PrefetchScalarGridSpec(num_scalar_prefetch=n_prefetch, grid=grid, in_specs=in_specs,
                                        out_specs=out_specs, scratch_shapes=list(scratch))
    return pl.pallas_call(
        body, name=name, grid_spec=spec, out_shape=out_shape,
        compiler_params=pltpu.CompilerParams(dimension_semantics=sem))


def _place():
    x, y, c = lax.axis_index("x"), lax.axis_index("y"), lax.axis_index("c")
    return x, y, c


def _mm(a, b, mode, out_dtype, name):
    b_arr, b_idx = b if isinstance(b, tuple) else (b, None)
    bs = b_arr.shape[-2:]
    if mode == "nn":
        (M, K), (K2, N) = a.shape, bs
        dims = (((1,), (0,)), ((), ()))
    elif mode == "nt":
        (M, K), (N, K2) = a.shape, bs
        dims = (((1,), (1,)), ((), ()))
    else:
        (K, M), (K2, N) = a.shape, bs
        dims = (((0,), (0,)), ((), ()))
    assert K == K2, (name, a.shape, b_arr.shape)
    if mode == "tn":
        tm = _tile(M, 1408, LANES)
        tk = _tile(K, 2048, BF16_ROWS)
        tn = _tile(N, 512, LANES)
    else:
        tm = _tile(M, 1024, BF16_ROWS)
        tk = K if K <= 2816 else _tile(K, 2816, LANES)
        tn = _tile(N, 1408 if tk <= 1024 else 512, LANES)
    if tn < 256:
        tn = N
        tm = _tile(M, 512, LANES if mode == "tn" else BF16_ROWS)
    nk = K // tk
    grid = (M // tm, N // tn, nk)

    if mode == "tn":
        a_spec = pl.BlockSpec((tk, tm), lambda i, j, k: (k, i))
    else:
        a_spec = pl.BlockSpec((tm, tk), lambda i, j, k: (i, k))
    if mode == "nt":
        b_blk, b_map = (tn, tk), (lambda i, j, k: (j, k))
    else:
        b_blk, b_map = (tk, tn), (lambda i, j, k: (k, j))
    if b_idx is None:
        b_spec = pl.BlockSpec(b_blk, b_map)
    else:
        b_spec = pl.BlockSpec((None,) + b_blk, lambda i, j, k: (b_idx,) + b_map(i, j, k))

    def body(a_ref, b_ref, o_ref, *acc):
        r = lax.dot_general(a_ref[...].astype(BF16), b_ref[...].astype(BF16), dims, preferred_element_type=F32)
        if nk == 1:
            o_ref[...] = r.astype(o_ref.dtype)
        else:
            k = pl.program_id(2)

            @pl.when(k == 0)
            def _():
                acc[0][...] = r

            @pl.when(k > 0)
            def _():
                acc[0][...] += r

            @pl.when(k == nk - 1)
            def _():
                o_ref[...] = acc[0][...].astype(o_ref.dtype)

    return _call(
        body, name=name, grid=grid, in_specs=[a_spec, b_spec],
        out_specs=pl.BlockSpec((tm, tn), lambda i, j, k: (i, j)),
        out_shape=jax.ShapeDtypeStruct((M, N), out_dtype),
        scratch=[pltpu.VMEM((tm, tn), F32)] if nk > 1 else [],
        sem=("parallel", "parallel", "arbitrary"))(a, b_arr)


def _rowwise(fn, rows, pars, outs, pouts, name):
    R = rows[0].shape[0]
    row_bytes = 4 * (sum(max(r.shape[1], LANES) for r in rows) + sum(max(c, LANES) for c, _ in outs))
    tb = _tile(R, max(BF16_ROWS, ROW_BLOCK_BYTES // row_bytes), BF16_ROWS)
    nr, npar, no = len(rows), len(pars), len(outs)

    def body(*refs):
        r_in, p_in = refs[:nr], refs[nr:nr + npar]
        r_out, p_out = refs[nr + npar:nr + npar + no], refs[nr + npar + no:]
        ro, po = fn([r[...] for r in r_in], [p[...] for p in p_in])
        for ref, val in zip(r_out, ro):
            if isinstance(val, (tuple, list)):
                off = 0
                for piece in val:
                    w = piece.shape[1]
                    ref[:, off:off + w] = piece.astype(ref.dtype)
                    off += w
            else:
                ref[...] = val.astype(ref.dtype)
        if p_out:
            first = pl.program_id(0) == 0

            @pl.when(first)
            def _():
                for ref, val in zip(p_out, po):
                    ref[...] = val

            @pl.when(jnp.logical_not(first))
            def _():
                for ref, val in zip(p_out, po):
                    ref[...] += val

    res = _call(
        body, name=name, grid=(R // tb,),
        in_specs=[pl.BlockSpec((tb, r.shape[1]), lambda i: (i, 0)) for r in rows]
        + [pl.BlockSpec(p.shape, lambda i: (0, 0)) for p in pars],
        out_specs=[pl.BlockSpec((tb, c), lambda i: (i, 0)) for c, _ in outs]
        + [pl.BlockSpec(s, lambda i: (0, 0)) for s in pouts],
        out_shape=[jax.ShapeDtypeStruct((R, c), dt) for c, dt in outs]
        + [jax.ShapeDtypeStruct(s, F32) for s in pouts],
        sem=("arbitrary",) if pouts else ("parallel",))(*rows, *pars)
    return list(res)


def _rms(x, g):
    return x * lax.rsqrt(jnp.mean(x * x, axis=-1, keepdims=True) + NORM_EPS) * g


def _norm_mod(x, g, sh, sc):
    return _rms(x, g) * (1.0 + sc) + sh


def _gated_post(y, g, gate):
    return gate * _rms(y, g)


def _norm_mod_fwd(x, g, sh, sc, name):
    return _rowwise(lambda r, p: ([_norm_mod(r[0], *p)], []), [x], [g, sh, sc], [(x.shape[1], BF16)], [], name)[0]


def _norm_mod_bwd(dxo, dh, x, g, sh, sc, name):
    def fn(r, p):
        _, vjp = jax.vjp(_norm_mod, r[2], *p)
        dx, dg, dsh, dsc = vjp(r[1].astype(F32))
        return [r[0] + dx], [dg, dsh, dsc]
    c = x.shape[1]
    return _rowwise(fn, [dxo, dh, x], [g, sh, sc], [(c, F32)], [(1, c)] * 3, name)


def _post_fwd(x, y, g, gate, name):
    return _rowwise(lambda r, p: ([r[0] + _gated_post(r[1].astype(F32), *p)], []), [x, y], [g, gate],
                    [(x.shape[1], F32)], [], name)[0]


def _post_bwd(dxo, y, g, gate, name):
    def fn(r, p):
        _, vjp = jax.vjp(_gated_post, r[1].astype(F32), *p)
        dy, dg, dgate = vjp(r[0])
        return [dy], [dg, dgate]
    c = y.shape[1]
    return _rowwise(fn, [dxo, y], [g, gate], [(c, BF16)], [(1, c)] * 2, name)


def _post_pre_fwd(x, y, g_post, gate, g_pre, sh, sc, name):
    def fn(r, p):
        x1 = r[0] + _gated_post(r[1].astype(F32), p[0], p[1])
        return [x1, _norm_mod(x1, p[2], p[3], p[4])], []
    c = x.shape[1]
    return _rowwise(fn, [x, y], [g_post, gate, g_pre, sh, sc], [(c, F32), (c, BF16)], [], name)


def _pre_post_bwd(dxo, dh, x, g_pre, sh, sc, y, g_post, gate, name):
    def fn(r, p):
        _, vjp_pre = jax.vjp(_norm_mod, r[2], p[0], p[1], p[2])
        dxn, dg_pre, dsh, dsc = vjp_pre(r[1].astype(F32))
        dx = r[0] + dxn
        _, vjp_post = jax.vjp(_gated_post, r[3].astype(F32), p[3], p[4])
        dy, dg_post, dgate = vjp_post(dx)
        return [dx, dy], [dg_pre, dsh, dsc, dg_post, dgate]
    c = x.shape[1]
    return _rowwise(fn, [dxo, dh, x, y], [g_pre, sh, sc, g_post, gate], [(c, F32), (c, BF16)], [(1, c)] * 5, name)


def _swiglu(g, u):
    return jax.nn.silu(g) * u


def _ffn_up(h, w, l, hw, name):
    S, D = h.shape
    nb = w.shape[2] // (2 * hw)
    tm = _tile(S, 512, BF16_ROWS)

    def body(h_ref, w_ref, gu_ref, act_ref):
        gu = jnp.dot(h_ref[...], w_ref[...], preferred_element_type=F32).astype(BF16)
        gu_ref[...] = gu
        act_ref[...] = _swiglu(gu[:, :hw].astype(F32), gu[:, hw:].astype(F32)).astype(BF16)

    return _call(body, name=name, grid=(S // tm, nb),
                 in_specs=[pl.BlockSpec((tm, D), lambda i, j: (i, 0)), pl.BlockSpec((None, D, 2 * hw), lambda i, j: (l, 0, j))],
                 out_specs=[pl.BlockSpec((tm, 2 * hw), lambda i, j: (i, j)), pl.BlockSpec((tm, hw), lambda i, j: (i, j))],
                 out_shape=[jax.ShapeDtypeStruct((S, 2 * hw * nb), BF16), jax.ShapeDtypeStruct((S, hw * nb), BF16)],
                 sem=("parallel", "parallel"))(h, w)


def _ffn_down_dx(dy, w_dn, l, gu, hw, name):
    S, D = dy.shape
    nb = gu.shape[1] // (2 * hw)
    tm = _tile(S, 512, BF16_ROWS)

    def body(dy_ref, w_ref, gu_ref, dgu_ref):
        dact = lax.dot_general(dy_ref[...], w_ref[...], (((1,), (1,)), ((), ())), preferred_element_type=F32)
        _, vjp = jax.vjp(_swiglu, gu_ref[:, :hw].astype(F32), gu_ref[:, hw:].astype(F32))
        dg, du = vjp(dact)
        dgu_ref[:, :hw] = dg.astype(BF16)
        dgu_ref[:, hw:] = du.astype(BF16)

    return _call(body, name=name, grid=(S // tm, nb),
                 in_specs=[pl.BlockSpec((tm, D), lambda i, j: (i, 0)), pl.BlockSpec((None, hw, D), lambda i, j: (l, j, 0)),
                           pl.BlockSpec((tm, 2 * hw), lambda i, j: (i, j))],
                 out_specs=pl.BlockSpec((tm, 2 * hw), lambda i, j: (i, j)),
                 out_shape=jax.ShapeDtypeStruct(gu.shape, BF16), sem=("parallel", "parallel"))(dy, w_dn, gu)


def _silu_rows(c, name):
    return _rowwise(lambda r, p: ([jax.nn.silu(r[0])], []), [c], [], [(c.shape[1], F32)], [], name)[0]


def _head_norm(x, g, scale):
    return _rms(x, g) * scale


def _head_norm_fwd(x, g, scale, name):
    return _rowwise(lambda r, p: ([_head_norm(r[0].astype(F32), p[0], scale)], []), [x], [g],
                    [(x.shape[1], BF16)], [], name)[0]


def _head_norm_bwd(dy, x, g, scale, name):
    def fn(r, p):
        _, vjp = jax.vjp(lambda t, gg: _head_norm(t, gg, scale), r[1].astype(F32), p[0])
        dx, dg = vjp(r[0])
        return [dx], [dg]
    c = x.shape[1]
    return _rowwise(fn, [dy, x], [g], [(c, F32)], [(1, c)], name)


def _out_gate_fwd(o, qg, name):
    d = o.shape[1]
    return _rowwise(lambda r, p: ([r[0] * jax.nn.sigmoid(r[1][:, d:].astype(F32))], []), [o, qg], [],
                    [(d, BF16)], [], name)[0]


def _out_gate_bwd(dog, o, qg, name):
    d = o.shape[1]

    def fn(r, p):
        _, vjp = jax.vjp(lambda oo, gl: oo * jax.nn.sigmoid(gl), r[1], r[2][:, d:].astype(F32))
        do, dgl = vjp(r[0])
        return [do, dgl], []
    return _rowwise(fn, [dog, o, qg], [], [(d, BF16), (d, BF16)], [], name)


def _loss_bwd(y, tgt, name):
    n = y.shape[1]

    def fn(r, p):
        e = r[0] - r[1]
        part = jnp.sum(jnp.sum(e * e, axis=1, keepdims=True), axis=0, keepdims=True) * (0.5 / n)
        return [e * (1.0 / n)], [part]
    return _rowwise(fn, [y, tgt], [], [(n, F32)], [(1, 1)], name)


def _adamw(w, g, m, v, name):
    shape = w.shape
    c = shape[-1]
    flat = [t.reshape(-1, c) for t in (w, g, m, v)]

    def fn(r, p):
        w_, g_, m_, v_ = r
        m2 = ADAM_B1 * m_ + (1.0 - ADAM_B1) * g_
        v2 = ADAM_B2 * v_ + (1.0 - ADAM_B2) * (g_ * g_)
        m_hat = m2 / (1.0 - ADAM_B1 ** ADAM_STEP)
        v_hat = v2 / (1.0 - ADAM_B2 ** ADAM_STEP)
        delta = -ADAM_LR * (m_hat / (jnp.sqrt(v_hat) + ADAM_EPS) + ADAM_WD * w_)
        return [delta, m2, v2], []
    res = _rowwise(fn, flat, [], [(c, F32)] * 3, [], name)
    return [t.reshape(shape) for t in res]


def _sum_pair(a, b, name):
    c = a.shape[-1]
    out = _rowwise(lambda r, p: ([r[0].astype(F32) + r[1].astype(F32)], []), [a.reshape(-1, c), b.reshape(-1, c)], [],
                   [(c, BF16)], [], name)[0]
    return out.reshape(a.shape)


def _sum_slots(recv, name):
    n = recv.shape[0]
    shape = recv.shape[1:]
    c = shape[-1]
    r3 = recv.reshape(n, -1, c)
    rows = r3.shape[1]
    tb = _tile(rows, max(BF16_ROWS, ROW_BLOCK_BYTES // (4 * c * (n + 1))), BF16_ROWS)

    def body(r_ref, o_ref):
        acc = r_ref[0].astype(F32)
        for s in range(1, n):
            acc = acc + r_ref[s].astype(F32)
        o_ref[...] = acc

    out = _call(body, name=name, grid=(rows // tb,),
                in_specs=[pl.BlockSpec((n, tb, c), lambda i: (0, i, 0))],
                out_specs=pl.BlockSpec((tb, c), lambda i: (i, 0)),
                out_shape=jax.ShapeDtypeStruct((rows, c), F32), sem=("parallel",))(r3)
    return out.reshape(shape)


def _gmlp_pre(zu, zv, b_u, b_v, ln_g, ln_b):
    u = jax.nn.gelu(zu + b_u, approximate=True)
    v = jax.nn.gelu(zv + b_v, approximate=True)
    xc = v - jnp.mean(v, axis=-1, keepdims=True)
    vn = xc * lax.rsqrt(jnp.mean(xc * xc, axis=-1, keepdims=True) + NORM_EPS) * ln_g + ln_b
    return u, vn


def _gmlp_fwd(zp, b_in, ln_g, ln_b, ws, bs_t, name):
    S, gw2 = zp.shape
    gw = gw2 // 2
    G, ch, _ = ws.shape
    gd = gw // G
    tb = 2 * ch

    def body(zp_ref, bin_ref, lg_ref, lb_ref, ws_ref, bs_ref, o_ref):
        u, vn = _gmlp_pre(zp_ref[:, :gw].astype(F32), zp_ref[:, gw:].astype(F32), bin_ref[:, :gw], bin_ref[:, gw:],
                          lg_ref[...], lb_ref[...])
        vnb = vn.astype(BF16)
        for c in range(tb // ch):
            for g in range(G):
                rs, cs = slice(c * ch, (c + 1) * ch), slice(g * gd, (g + 1) * gd)
                vv = jnp.dot(ws_ref[g], vnb[rs, cs], preferred_element_type=F32) + bs_ref[:, g:g + 1]
                o_ref[rs, cs] = (u[rs, cs] * vv).astype(o_ref.dtype)

    full = lambda a: pl.BlockSpec(a.shape, lambda i: (0,) * a.ndim)
    return _call(body, name=name, grid=(S // tb,),
                 in_specs=[pl.BlockSpec((tb, gw2), lambda i: (i, 0)), full(b_in), full(ln_g), full(ln_b), full(ws), full(bs_t)],
                 out_specs=pl.BlockSpec((tb, gw), lambda i: (i, 0)),
                 out_shape=jax.ShapeDtypeStruct((S, gw), BF16), sem=("parallel",))(zp, b_in, ln_g, ln_b, ws, bs_t)


def _gmlp_bwd(dyg, zp, b_in, ln_g, ln_b, ws, ws_t, bs_t, name):
    S, gw2 = zp.shape
    gw = gw2 // 2
    G, ch, _ = ws.shape
    gd = gw // G
    tb = 2 * ch

    def body(dy_ref, zp_ref, bin_ref, lg_ref, lb_ref, ws_ref, wst_ref, bs_ref,
             dzp_ref, dbin_ref, dlg_ref, dlb_ref, dws_ref, dbs_ref, du_sc, dvn_sc):
        (u, vn), vjp = jax.vjp(_gmlp_pre, zp_ref[:, :gw].astype(F32), zp_ref[:, gw:].astype(F32), bin_ref[:, :gw],
                               bin_ref[:, gw:], lg_ref[...], lb_ref[...])
        vnb = vn.astype(BF16)
        first = pl.program_id(0) == 0

        @pl.when(first)
        def _():
            dws_ref[...] = jnp.zeros_like(dws_ref)

        lane = lax.broadcasted_iota(jnp.int32, (ch, G), 1)
        dbs = jnp.zeros((ch, G), F32)
        for g in range(G):
            cs = slice(g * gd, (g + 1) * gd)
            dws_g = jnp.zeros((ch, ch), F32)
            col = jnp.zeros((ch, 1), F32)
            for c in range(tb // ch):
                rs = slice(c * ch, (c + 1) * ch)
                vnp = vnb[rs, cs]
                vv = jnp.dot(ws_ref[g], vnp, preferred_element_type=F32) + bs_ref[:, g:g + 1]
                dy = dy_ref[rs, cs].astype(F32)
                du_sc[rs, cs] = dy * vv
                dvv = dy * u[rs, cs]
                dvvb = dvv.astype(BF16)
                dvn_sc[rs, cs] = jnp.dot(wst_ref[g], dvvb, preferred_element_type=F32)
                dws_g = dws_g + lax.dot_general(dvvb, vnp, (((1,), (1,)), ((), ())), preferred_element_type=F32)
                col = col + jnp.sum(dvv, axis=1, keepdims=True)
            dws_ref[g] += dws_g
            dbs = jnp.where(lane == g, col, dbs)
        dzu, dzv, dbu, dbv, dlg, dlb = vjp((du_sc[...], dvn_sc[...]))
        dzp_ref[:, :gw] = dzu.astype(dzp_ref.dtype)
        dzp_ref[:, gw:] = dzv.astype(dzp_ref.dtype)

        @pl.when(first)
        def _():
            dbin_ref[:, :gw] = dbu
            dbin_ref[:, gw:] = dbv
            dlg_ref[...] = dlg
            dlb_ref[...] = dlb
            dbs_ref[...] = dbs

        @pl.when(jnp.logical_not(first))
        def _():
            dbin_ref[:, :gw] += dbu
            dbin_ref[:, gw:] += dbv
            dlg_ref[...] += dlg
            dlb_ref[...] += dlb
            dbs_ref[...] += dbs

    full = lambda a: pl.BlockSpec(a.shape, lambda i: (0,) * a.ndim)
    fshape = lambda s: pl.BlockSpec(s, lambda i: (0,) * len(s))
    return _call(
        body, name=name, grid=(S // tb,),
        in_specs=[pl.BlockSpec((tb, gw), lambda i: (i, 0)), pl.BlockSpec((tb, gw2), lambda i: (i, 0)),
                  full(b_in), full(ln_g), full(ln_b), full(ws), full(ws_t), full(bs_t)],
        out_specs=[pl.BlockSpec((tb, gw2), lambda i: (i, 0)), fshape((1, gw2)), fshape((1, gw)), fshape((1, gw)),
                   fshape((G, ch, ch)), fshape((ch, G))],
        out_shape=[jax.ShapeDtypeStruct((S, gw2), BF16), jax.ShapeDtypeStruct((1, gw2), F32),
                   jax.ShapeDtypeStruct((1, gw), F32), jax.ShapeDtypeStruct((1, gw), F32),
                   jax.ShapeDtypeStruct((G, ch, ch), F32), jax.ShapeDtypeStruct((ch, G), F32)],
        scratch=[pltpu.VMEM((tb, gw), F32), pltpu.VMEM((tb, gw), F32)],
        sem=("arbitrary",))(dyg, zp, b_in, ln_g, ln_b, ws, ws_t, bs_t)


def _dot_01(x, ones_bf16):
    hi = x.astype(BF16)
    r1 = x - hi.astype(F32)
    mid = r1.astype(BF16)
    lo = (r1 - mid.astype(F32)).astype(BF16)
    dot = lambda t: jnp.dot(t, ones_bf16, preferred_element_type=F32)
    return dot(hi) + dot(mid) + dot(lo)


def _log_sigmoid(x):
    return jnp.minimum(x, 0.0) - jnp.log1p(jnp.exp(-jnp.abs(x)))


def _dcum_fwd(f_t, b_col, name):
    H, S = f_t.shape
    tb = _tile(S, 512, LANES)

    def body(f_ref, b_ref, o_ref, carry):
        @pl.when(pl.program_id(0) == 0)
        def _():
            carry[...] = jnp.zeros_like(carry)

        ls = _log_sigmoid(f_ref[...] + b_ref[...])
        r = lax.broadcasted_iota(jnp.int32, (tb, tb), 0)
        c = lax.broadcasted_iota(jnp.int32, (tb, tb), 1)
        upper = (r <= c).astype(BF16)
        o_ref[...] = _dot_01(ls, upper) + carry[...]
        carry[...] += jnp.sum(ls, axis=1, keepdims=True)

    return _call(body, name=name, grid=(S // tb,),
                 in_specs=[pl.BlockSpec((H, tb), lambda i: (0, i)), pl.BlockSpec((H, 1), lambda i: (0, 0))],
                 out_specs=pl.BlockSpec((H, tb), lambda i: (0, i)),
                 out_shape=jax.ShapeDtypeStruct((H, S), F32),
                 scratch=[pltpu.VMEM((H, 1), F32)], sem=("arbitrary",))(f_t, b_col)


def _dcum_bwd(dd_t, f_t, b_col, name):
    H, S = f_t.shape
    tb = _tile(S, 512, LANES)
    nb = S // tb

    def body(dd_ref, f_ref, b_ref, df_ref, db_ref, carry):
        first = pl.program_id(0) == 0

        @pl.when(first)
        def _():
            carry[...] = jnp.zeros_like(carry)

        dd = dd_ref[...]
        r = lax.broadcasted_iota(jnp.int32, (tb, tb), 0)
        c = lax.broadcasted_iota(jnp.int32, (tb, tb), 1)
        lower = (r >= c).astype(BF16)
        rev = _dot_01(dd, lower) + carry[...]
        carry[...] += jnp.sum(dd, axis=1, keepdims=True)
        df = rev * jax.nn.sigmoid(-(f_ref[...] + b_ref[...]))
        df_ref[...] = df
        part = jnp.sum(df, axis=1, keepdims=True)

        @pl.when(first)
        def _():
            db_ref[...] = part

        @pl.when(jnp.logical_not(first))
        def _():
            db_ref[...] += part

    return _call(body, name=name, grid=(nb,),
                 in_specs=[pl.BlockSpec((H, tb), lambda i: (0, nb - 1 - i)), pl.BlockSpec((H, tb), lambda i: (0, nb - 1 - i)),
                           pl.BlockSpec((H, 1), lambda i: (0, 0))],
                 out_specs=[pl.BlockSpec((H, tb), lambda i: (0, nb - 1 - i)), pl.BlockSpec((H, 1), lambda i: (0, 0))],
                 out_shape=[jax.ShapeDtypeStruct((H, S), F32), jax.ShapeDtypeStruct((H, 1), F32)],
                 scratch=[pltpu.VMEM((H, 1), F32)], sem=("arbitrary",))(dd_t, f_t, b_col)


def _attn_tile(S):
    return _tile(S, 512, LANES)


def _causal_t(t):
    return lax.broadcasted_iota(jnp.int32, (t, t), 0) <= lax.broadcasted_iota(jnp.int32, (t, t), 1)


def _tri_pairs(n, key_major):
    if key_major:
        pairs = [(i, j) for j in range(n) for i in range(j, n)]
    else:
        pairs = [(i, j) for i in range(n) for j in range(i + 1)]
    return jnp.asarray([p[0] for p in pairs], jnp.int32), jnp.asarray([p[1] for p in pairs], jnp.int32)


def _split3(x):
    hi = lax.reduce_precision(x, 8, 7)
    r = x - hi
    mid = lax.reduce_precision(r, 8, 7)
    lo = lax.reduce_precision(r - mid, 8, 7)
    return hi.astype(BF16), mid.astype(BF16), lo.astype(BF16)


def _augment(xn, dcum, query):
    H, S, hd = xn.shape
    parts = list(_split3(dcum))
    vals = parts + [1.0] * 3 if query else [1.0] * 3 + [-p for p in parts]
    lane = lax.broadcasted_iota(jnp.int32, (1, 1, LANES), 2)
    out = jnp.pad(xn, ((0, 0), (0, 0), (0, LANES - hd)))
    for k, val in enumerate(vals):
        val = jnp.asarray(val, BF16)
        out = jnp.where(lane == hd + k, val[..., None] if val.ndim else val, out)
    return out


def _scores_t(k_ref, qt_ref, h, t, diag):
    st = jnp.dot(k_ref[h], qt_ref[h], preferred_element_type=F32)
    return jnp.where(_causal_t(t), st, MASKED) if diag else st


def _flash_fwd(ka, qat, vat, hd, name):
    H, S, da = ka.shape
    t = _attn_tile(S)
    hb = min(H, ATTN_FWD_HEADS_PER_STEP)
    it, jt = _tri_pairs(S // t, False)

    def body(it_ref, jt_ref, k_ref, qt_ref, vt_ref, o_ref, lse_ref, m_sc, acc_sc):
        i, j = it_ref[pl.program_id(1)], jt_ref[pl.program_id(1)]

        @pl.when(j == 0)
        def _():
            m_sc[...] = jnp.full_like(m_sc, MASKED)
            acc_sc[...] = jnp.zeros_like(acc_sc)

        def step(diag):
            sts = [_scores_t(k_ref, qt_ref, h, t, diag) for h in range(hb)]
            pts, alphas = [], []
            for h in range(hb):
                m_prev = m_sc[h]
                m_new = jnp.maximum(m_prev, jnp.max(sts[h], axis=0, keepdims=True))
                pts.append(jnp.exp(sts[h] - m_new).astype(BF16))
                alphas.append(jnp.exp(m_prev - m_new))
                m_sc[h] = m_new
            for h in range(hb):
                acc_sc[h] = alphas[h] * acc_sc[h] + jnp.dot(vt_ref[h], pts[h], preferred_element_type=F32)

        @pl.when(j < i)
        def _():
            step(False)

        @pl.when(j == i)
        def _():
            step(True)
            for h in range(hb):
                l = acc_sc[h, hd:hd + 1, :]
                o_ref[h] = acc_sc[h, :hd, :] / l
                lse_ref[h] = m_sc[h] + jnp.log(l)

    qcol = lambda h, p, it_, jt_: (h, 0, it_[p])
    kcol = lambda h, p, it_, jt_: (h, 0, jt_[p])
    krow = lambda h, p, it_, jt_: (h, jt_[p], 0)
    return _call_prefetch(
        body, name=name, grid=(H // hb, it.shape[0]), n_prefetch=2,
        in_specs=[pl.BlockSpec((hb, t, da), krow), pl.BlockSpec((hb, da, t), qcol), pl.BlockSpec((hb, da, t), kcol)],
        out_specs=[pl.BlockSpec((hb, hd, t), qcol), pl.BlockSpec((hb, 1, t), qcol)],
        out_shape=[jax.ShapeDtypeStruct((H, hd, S), F32), jax.ShapeDtypeStruct((H, 1, S), F32)],
        scratch=[pltpu.VMEM((hb, 1, t), F32), pltpu.VMEM((hb, da, t), F32)],
        sem=("parallel", "arbitrary"))(it, jt, ka, qat, vat)


def _flash_bwd(ka, kat, qat, v, dot, o_tr, lse_r, name):
    H, S, hd = v.shape
    da = ka.shape[2]
    t = _attn_tile(S)
    n = S // t
    hb = ATTN_HEADS_PER_STEP
    it, jt = _tri_pairs(n, True)
    over_queries = (((1,), (1,)), ((), ()))

    def body(it_ref, jt_ref, k_ref, kt_ref, qt_ref, v_ref, dot_ref, o_ref, lse_ref, dq_ref, dk_ref, dv_ref, dk_sc, dv_sc):
        i, j = it_ref[pl.program_id(1)], jt_ref[pl.program_id(1)]

        @pl.when(pl.program_id(1) == 0)
        def _():
            dq_ref[...] = jnp.zeros_like(dq_ref)

        def step(diag):
            for h0 in range(0, hb, ATTN_STAGED_HEADS):
                hs = range(h0, min(h0 + ATTN_STAGED_HEADS, hb))
                for q0 in range(0, t, t // 2):
                    qs = slice(q0, q0 + t // 2)
                    tiles = []
                    for h in hs:
                        st = jnp.dot(k_ref[h], qt_ref[h, :, qs], preferred_element_type=F32)
                        if diag:
                            keys = lax.broadcasted_iota(jnp.int32, st.shape, 0)
                            st = jnp.where(keys <= lax.broadcasted_iota(jnp.int32, st.shape, 1) + q0, st, MASKED)
                        dpt = jnp.dot(v_ref[h], dot_ref[h, :, qs], preferred_element_type=F32)
                        dl = jnp.sum(dot_ref[h, :, qs].astype(F32) * o_ref[h, :, qs], axis=0, keepdims=True)
                        pt = jnp.exp((st - lse_ref[h, :, qs]).astype(BF16))
                        tiles.append((pt, pt * (dpt - dl).astype(BF16)))
                    for h, (ptb, dsb) in zip(hs, tiles):
                        dv_sc[h] += lax.dot_general(dot_ref[h, :, qs], ptb, over_queries, preferred_element_type=F32)
                        dk_sc[h] += lax.dot_general(qt_ref[h, :, qs], dsb, over_queries, preferred_element_type=F32)
                        dq_ref[h, i, :, qs] += jnp.dot(kt_ref[h], dsb, preferred_element_type=F32)

        @pl.when(i == j)
        def _():
            dk_sc[...] = jnp.zeros_like(dk_sc)
            dv_sc[...] = jnp.zeros_like(dv_sc)
            step(True)

        @pl.when(i > j)
        def _():
            step(False)

        @pl.when(i == n - 1)
        def _():
            dk_ref[...] = dk_sc[...]
            dv_ref[...] = dv_sc[...]

    krow = lambda h, p, it_, jt_: (h, jt_[p], 0)
    kcol = lambda h, p, it_, jt_: (h, 0, jt_[p])
    qcol = lambda h, p, it_, jt_: (h, 0, it_[p])
    return _call_prefetch(
        body, name=name, grid=(H // hb, it.shape[0]), n_prefetch=2,
        in_specs=[pl.BlockSpec((hb, t, da), krow), pl.BlockSpec((hb, da, t), kcol), pl.BlockSpec((hb, da, t), qcol),
                  pl.BlockSpec((hb, t, hd), krow), pl.BlockSpec((hb, hd, t), qcol), pl.BlockSpec((hb, hd, t), qcol),
                  pl.BlockSpec((hb, 1, t), qcol)],
        out_specs=[pl.BlockSpec((hb, n, da, t), lambda h, p, it_, jt_: (h, 0, 0, 0)), pl.BlockSpec((hb, da, t), kcol),
                   pl.BlockSpec((hb, hd, t), kcol)],
        out_shape=[jax.ShapeDtypeStruct((H, n, da, t), F32), jax.ShapeDtypeStruct((H, da, S), F32),
                   jax.ShapeDtypeStruct((H, hd, S), F32)],
        scratch=[pltpu.VMEM((hb, da, t), F32), pltpu.VMEM((hb, hd, t), F32)],
        sem=("parallel", "arbitrary"))(it, jt, ka, kat, qat, v, dot, o_tr, lse_r)


def _offsets(n_bits):
    return [tuple((k >> b) & 1 for b in reversed(range(n_bits))) for k in range(1, 1 << n_bits)]


def _own_slot(out, own, idx):
    return lax.dynamic_update_index_in_dim(out, own.astype(out.dtype), idx, 0)


def _gather8(arrs, name):
    n = len(arrs)
    offs = _offsets(3)

    def body(*refs):
        ins, outs = refs[:n], refs[n:2 * n]
        ssem, rsem = refs[2 * n:]
        x, y, c = _place()
        me = 4 * x + 2 * y + c
        copies = []
        for a in range(n):
            for k, (dx, dy, dcc) in enumerate(offs):
                cp = pltpu.make_async_remote_copy(
                    src_ref=ins[a], dst_ref=outs[a].at[me], send_sem=ssem.at[a, k], recv_sem=rsem.at[a, k],
                    device_id=((x + dx) % 2, (y + dy) % 2, (c + dcc) % 2), device_id_type=MESH)
                cp.start()
                copies.append(cp)
        for cp in copies:
            cp.wait()

    return _call(body, name=name, in_specs=[ANY] * n, out_specs=[ANY] * n,
                 out_shape=[jax.ShapeDtypeStruct((N_DEV,) + a.shape, a.dtype) for a in arrs],
                 scratch=[pltpu.SemaphoreType.DMA((n, 7)), pltpu.SemaphoreType.DMA((n, 7))])(*arrs)


def _chip_position(chip, place):
    return sum(jnp.where(chip == s, place[1].index(s), 0) for s in range(N_CHIPS))


def _shard_slot(ref, shard_shape, place, chip, c=None):
    hn = shard_shape[0] // 2
    half = slice(None) if c is None else pl.ds(c * hn, hn)
    if place is None:
        return ref.at[chip, half]
    ax = place[0]
    w = shard_shape[ax]
    idx = [slice(None)] * len(shard_shape)
    idx[0] = half
    idx[ax] = pl.ds(pl.multiple_of(_chip_position(chip, place) * w, LANES if ax == len(shard_shape) - 1 else BF16_ROWS), w)
    return ref.at[tuple(idx)]


def _gathered_shape(shard_shape, place):
    if place is None:
        return (N_CHIPS,) + tuple(shard_shape)
    s = list(shard_shape)
    s[place[0]] *= N_CHIPS
    return tuple(s)


def _chip_gather(arrs, halved, name, places=None):
    n = len(arrs)
    offs = _offsets(2)
    places = places or [None] * n

    def body(*refs):
        ins, outs = refs[:n], refs[n:2 * n]
        ssem, rsem = refs[2 * n:]
        x, y, c = _place()
        chip = 2 * x + y
        copies = []
        for a in range(n):
            if halved:
                hn = arrs[a].shape[0] // 2
                src = ins[a].at[pl.ds(c * hn, hn)]
                dst = _shard_slot(outs[a], arrs[a].shape, places[a], chip, c)
            else:
                src, dst = ins[a], outs[a].at[chip]
            for k, (dx, dy) in enumerate(offs):
                cp = pltpu.make_async_remote_copy(
                    src_ref=src, dst_ref=dst, send_sem=ssem.at[a, k], recv_sem=rsem.at[a, k],
                    device_id=((x + dx) % 2, (y + dy) % 2, c), device_id_type=MESH)
                cp.start()
                copies.append(cp)
        for cp in copies:
            cp.wait()

    return _call(body, name=name, in_specs=[ANY] * n, out_specs=[ANY] * n,
                 out_shape=[jax.ShapeDtypeStruct(_gathered_shape(a.shape, p) if halved else (N_CHIPS,) + a.shape, a.dtype)
                            for a, p in zip(arrs, places)],
                 scratch=[pltpu.SemaphoreType.DMA((n, 3)), pltpu.SemaphoreType.DMA((n, 3))])(*arrs)


def _sibling_fill(bufs, owns, places, name):
    n = len(bufs)
    offs = _offsets(2)

    def body(*refs):
        ins, own, outs = refs[:n], refs[n:2 * n], refs[2 * n:3 * n]
        ssem, rsem = refs[3 * n:]
        x, y, c = _place()
        copies = []
        for a in range(n):
            shape = owns[a].shape
            for k, (dx, dy) in enumerate(offs):
                chip = 2 * ((x + dx) % 2) + (y + dy) % 2
                cp = pltpu.make_async_remote_copy(
                    src_ref=_shard_slot(ins[a], shape, places[a], chip, c),
                    dst_ref=_shard_slot(outs[a], shape, places[a], chip, c),
                    send_sem=ssem.at[a, k], recv_sem=rsem.at[a, k],
                    device_id=(x, y, 1 - c), device_id_type=MESH)
                cp.start()
                copies.append(cp)
            cp = pltpu.make_async_remote_copy(
                src_ref=own[a], dst_ref=_shard_slot(outs[a], shape, places[a], 2 * x + y),
                send_sem=ssem.at[a, 3], recv_sem=rsem.at[a, 3], device_id=(x, y, 1 - c), device_id_type=MESH)
            cp.start()
            copies.append(cp)
        for cp in copies:
            cp.wait()

    return _call(body, name=name, in_specs=[ANY] * (2 * n), out_specs=[ANY] * n,
                 out_shape=[jax.ShapeDtypeStruct(b.shape, b.dtype) for b in bufs],
                 scratch=[pltpu.SemaphoreType.DMA((n, 4)), pltpu.SemaphoreType.DMA((n, 4))],
                 aliases={a: a for a in range(n)})(*bufs, *owns)


def _sibling_pair(arrs, name):
    n = len(arrs)

    def body(*refs):
        ins, outs = refs[:n], refs[n:2 * n]
        ssem, rsem = refs[2 * n:]
        x, y, c = _place()
        copies = []
        for a in range(n):
            cp = pltpu.make_async_remote_copy(
                src_ref=ins[a], dst_ref=outs[a].at[c], send_sem=ssem.at[a], recv_sem=rsem.at[a],
                device_id=(x, y, 1 - c), device_id_type=MESH)
            cp.start()
            copies.append(cp)
        for cp in copies:
            cp.wait()

    return _call(body, name=name, in_specs=[ANY] * n, out_specs=[ANY] * n,
                 out_shape=[jax.ShapeDtypeStruct((N_CORES,) + a.shape, a.dtype) for a in arrs],
                 scratch=[pltpu.SemaphoreType.DMA((n,)), pltpu.SemaphoreType.DMA((n,))])(*arrs)


CHIP_ORDER = (0, 1, 2, 3)


def _piece(shape, spec, j, h):
    shard_ax, half_ax, order = spec
    w = shape[shard_ax] // N_CHIPS
    idx = [slice(None)] * len(shape)
    idx[shard_ax] = pl.ds(order[j] * w, w)
    assert half_ax != shard_ax
    hn = shape[half_ax] // 2
    idx[half_ax] = pl.ds(h * hn, hn)
    return tuple(idx)


def _piece_shape(shape, spec):
    shard_ax, half_ax, _ = spec
    s = list(shape)
    s[shard_ax] //= N_CHIPS
    s[half_ax] //= 2
    return tuple(s)


def _own_pieces(g, spec, c):
    shard_ax, half_ax, order = spec
    hn = g.shape[half_ax] // 2
    half = lax.dynamic_slice_in_dim(g, c * hn, hn, axis=half_ax)
    w = half.shape[shard_ax] // N_CHIPS
    return jnp.stack([lax.slice_in_dim(half, p * w, (p + 1) * w, axis=shard_ax) for p in order])


def _sibling_scatter(arrs, specs, name):
    n = len(arrs)

    def body(*refs):
        ins, outs = refs[:n], refs[n:2 * n]
        ssem, rsem = refs[2 * n:]
        x, y, c = _place()
        for mine in range(N_CORES):
            @pl.when(c == mine)
            def _():
                copies = []
                for a in range(n):
                    for j in range(N_CHIPS):
                        cp = pltpu.make_async_remote_copy(
                            src_ref=ins[a].at[_piece(arrs[a].shape, specs[a], j, 1 - mine)], dst_ref=outs[a].at[j],
                            send_sem=ssem.at[a, j], recv_sem=rsem.at[a, j],
                            device_id=(x, y, 1 - mine), device_id_type=MESH)
                        cp.start()
                        copies.append(cp)
                for cp in copies:
                    cp.wait()

    return _call(body, name=name, in_specs=[ANY] * n, out_specs=[ANY] * n,
                 out_shape=[jax.ShapeDtypeStruct((N_CHIPS,) + _piece_shape(a.shape, s), a.dtype)
                            for a, s in zip(arrs, specs)],
                 scratch=[pltpu.SemaphoreType.DMA((n, N_CHIPS))] * 2)(*arrs)


def _chip_scatter(arrs, name):
    n = len(arrs)
    offs = _offsets(2)

    def body(*refs):
        ins, outs = refs[:n], refs[n:2 * n]
        ssem, rsem = refs[2 * n:]
        x, y, c = _place()
        chip = 2 * x + y
        copies = []
        for a in range(n):
            for k, (dx, dy) in enumerate(offs):
                tx, ty = (x + dx) % 2, (y + dy) % 2
                cp = pltpu.make_async_remote_copy(
                    src_ref=ins[a].at[2 * tx + ty], dst_ref=outs[a].at[chip], send_sem=ssem.at[a, k], recv_sem=rsem.at[a, k],
                    device_id=(tx, ty, c), device_id_type=MESH)
                cp.start()
                copies.append(cp)
        for cp in copies:
            cp.wait()

    return _call(body, name=name, in_specs=[ANY] * n, out_specs=[ANY] * n,
                 out_shape=[jax.ShapeDtypeStruct(a.shape, a.dtype) for a in arrs],
                 scratch=[pltpu.SemaphoreType.DMA((n, 3)), pltpu.SemaphoreType.DMA((n, 3))])(*arrs)


def kernel(x, c, ada_w, ada_b, pre_mix_g, post_mix_g, pre_ffn_g, post_ffn_g, ffn_w_gu, ffn_w_down, a_w_in, a_b_in, a_ln_g, a_ln_b, a_w_s, a_b_s, a_w_out, kv_ada_w, kv_ada_b, kv_norm_g, kv_w, kv_b_f, k_norm_g, b_w_qg, b_q_norm_g, b_w_o, loss_target, m_ada_w, m_ada_b, m_pre_mix_g, m_post_mix_g, m_pre_ffn_g, m_post_ffn_g, m_ffn_w_gu, m_ffn_w_down, m_a_w_in, m_a_b_in, m_a_ln_g, m_a_ln_b, m_a_w_s, m_a_b_s, m_a_w_out, m_kv_ada_w, m_kv_ada_b, m_kv_norm_g, m_kv_w, m_kv_b_f, m_k_norm_g, m_b_w_qg, m_b_q_norm_g, m_b_w_o, v_ada_w, v_ada_b, v_pre_mix_g, v_post_mix_g, v_pre_ffn_g, v_post_ffn_g, v_ffn_w_gu, v_ffn_w_down, v_a_w_in, v_a_b_in, v_a_ln_g, v_a_ln_b, v_a_w_s, v_a_b_s, v_a_w_out, v_kv_ada_w, v_kv_ada_b, v_kv_norm_g, v_kv_w, v_kv_b_f, v_k_norm_g, v_b_w_qg, v_b_q_norm_g, v_b_w_o):
    weights = dict(ada_w=ada_w, ada_b=ada_b, pre_mix_g=pre_mix_g, post_mix_g=post_mix_g, pre_ffn_g=pre_ffn_g,
                   post_ffn_g=post_ffn_g, ffn_w_gu=ffn_w_gu, ffn_w_down=ffn_w_down, a_w_in=a_w_in, a_b_in=a_b_in,
                   a_ln_g=a_ln_g, a_ln_b=a_ln_b, a_w_s=a_w_s, a_b_s=a_b_s, a_w_out=a_w_out, kv_ada_w=kv_ada_w,
                   kv_ada_b=kv_ada_b, kv_norm_g=kv_norm_g, kv_w=kv_w, kv_b_f=kv_b_f, k_norm_g=k_norm_g, b_w_qg=b_w_qg,
                   b_q_norm_g=b_q_norm_g, b_w_o=b_w_o)
    m_in = dict(ada_w=m_ada_w, ada_b=m_ada_b, pre_mix_g=m_pre_mix_g, post_mix_g=m_post_mix_g, pre_ffn_g=m_pre_ffn_g,
                post_ffn_g=m_post_ffn_g, ffn_w_gu=m_ffn_w_gu, ffn_w_down=m_ffn_w_down, a_w_in=m_a_w_in, a_b_in=m_a_b_in,
                a_ln_g=m_a_ln_g, a_ln_b=m_a_ln_b, a_w_s=m_a_w_s, a_b_s=m_a_b_s, a_w_out=m_a_w_out, kv_ada_w=m_kv_ada_w,
                kv_ada_b=m_kv_ada_b, kv_norm_g=m_kv_norm_g, kv_w=m_kv_w, kv_b_f=m_kv_b_f, k_norm_g=m_k_norm_g,
                b_w_qg=m_b_w_qg, b_q_norm_g=m_b_q_norm_g, b_w_o=m_b_w_o)
    v_in = dict(ada_w=v_ada_w, ada_b=v_ada_b, pre_mix_g=v_pre_mix_g, post_mix_g=v_post_mix_g, pre_ffn_g=v_pre_ffn_g,
                post_ffn_g=v_post_ffn_g, ffn_w_gu=v_ffn_w_gu, ffn_w_down=v_ffn_w_down, a_w_in=v_a_w_in, a_b_in=v_a_b_in,
                a_ln_g=v_a_ln_g, a_ln_b=v_a_ln_b, a_w_s=v_a_w_s, a_b_s=v_a_b_s, a_w_out=v_a_w_out, kv_ada_w=v_kv_ada_w,
                kv_ada_b=v_kv_ada_b, kv_norm_g=v_kv_norm_g, kv_w=v_kv_w, kv_b_f=v_kv_b_f, k_norm_g=v_k_norm_g,
                b_w_qg=v_b_w_qg, b_q_norm_g=v_b_q_norm_g, b_w_o=v_b_w_o)
    names = list(weights)

    S, D = x.shape[1], x.shape[2]
    L, NA, NB = ada_w.shape[0], a_w_in.shape[0], b_w_qg.shape[0]
    H = kv_b_f.shape[0]
    hd = D // H
    G, CH = a_w_s.shape[1], a_w_s.shape[2]
    GW = a_w_out.shape[1] * N_CHIPS
    F = ffn_w_down.shape[1] * N_CHIPS
    ada_cols = ada_w.shape[2]
    kvada_cols = kv_ada_w.shape[1]
    kv_cols = kv_w.shape[1]
    kv_pad = -(-(2 * D + H) // LANES) * LANES
    xi, yi, ci = _place()
    chip = 2 * xi + yi
    me = 2 * chip + ci
    x0 = x[0]
    tgt = loss_target[0]
    row = lambda t: t.reshape(1, -1)

    c_all = _own_slot(_gather8([c], "gather_c")[0], c, me).reshape(N_DEV, D)
    c_act = _silu_rows(jnp.pad(c_all, ((0, BF16_ROWS - N_DEV), (0, 0))), "silu_c")
    mod_sh = [_mm(c_act, (ada_w, l), "nn", F32, f"mod_proj_{l}") for l in range(L)]
    mod_sh.append(_mm(c_act, kv_ada_w, "nn", F32, "mod_proj_kv"))
    mod_sh = jnp.concatenate(mod_sh, axis=1)
    small_sh = [mod_sh, a_b_in, a_ln_g, a_ln_b]
    mod_all, b_in_all, ln_g_all, ln_b_all = [
        _own_slot(o, s, chip) for o, s in zip(_chip_gather(small_sh, False, "gather_mod"), small_sh)]
    mine = lax.dynamic_index_in_dim(mod_all, me, axis=1, keepdims=False)
    mod = [jnp.concatenate([mine[j, l * ada_cols:(l + 1) * ada_cols] for j in range(N_CHIPS)]) + ada_b[l] for l in range(L)]
    mod = [[row(t) for t in jnp.split(m_, 6)] for m_ in mod]
    mod_kv = jnp.concatenate([mine[j, L * ada_cols:] for j in range(N_CHIPS)]) + kv_ada_b
    kv_sh, kv_sc = [row(t) for t in jnp.split(mod_kv, 2)]
    cat_chips = lambda t, ax: jnp.concatenate([t[j] for j in range(N_CHIPS)], axis=ax)
    b_in_f = cat_chips(b_in_all, 1)
    ln_g_f, ln_b_f = cat_chips(ln_g_all, 1), cat_chips(ln_b_all, 1)

    big = ["ffn_w_gu", "ffn_w_down", "a_w_in", "a_w_out", "kv_w", "b_w_qg", "b_w_o"]
    own_w = [weights[n].astype(BF16) for n in big]
    gu_order = (0, 2, 1, 3)
    places = {"ffn_w_gu": (2, gu_order), "ffn_w_down": (1, CHIP_ORDER), "a_w_in": (2, CHIP_ORDER),
              "a_w_out": (1, CHIP_ORDER), "kv_w": None, "b_w_qg": (2, CHIP_ORDER), "b_w_o": (1, CHIP_ORDER)}
    plist = [places[n] for n in big]
    full_w = dict(zip(big, _sibling_fill(_chip_gather(own_w, True, "gather_w", plist), own_w, plist, "fill_w")))
    gu_hw = ffn_w_gu.shape[2]
    w_gu, w_dn, w_in, w_out = full_w["ffn_w_gu"], full_w["ffn_w_down"], full_w["a_w_in"], full_w["a_w_out"]
    w_qg, w_o = full_w["b_w_qg"], full_w["b_w_o"]
    w_kv = jnp.pad(cat_chips(full_w["kv_w"], 1), ((0, 0), (0, kv_pad - (2 * D + H))))

    causal = jnp.tril(jnp.ones((CH, CH), F32))
    ws_m = [(a_w_s[i] * causal).astype(BF16) for i in range(NA)]
    ws_mt = [jnp.swapaxes(w, 1, 2) for w in ws_m]
    bs_t = [a_b_s[i].T for i in range(NA)]

    heads = lambda t: t.reshape(S, H, hd).transpose(1, 0, 2)
    unheads = lambda t: t.transpose(1, 0, 2).reshape(S, D)

    saved = []
    kv = None
    xc = x0
    h1 = _norm_mod_fwd(xc, row(pre_mix_g[0]), mod[0][0], mod[0][1], "pre_mix_0")
    for l in range(L):
        sh_m, sc_m, g_m, sh_f, sc_f, g_f = mod[l]
        st = {"x0": xc, "h1": h1}
        if l < NA:
            zp = _mm(h1, (w_in, l), "nn", BF16, f"gmlp_in_{l}")
            yg = _gmlp_fwd(zp, row(b_in_f[l]), row(ln_g_f[l]), row(ln_b_f[l]), ws_m[l], bs_t[l], f"gmlp_gate_{l}")
            y = _mm(yg, (w_out, l), "nn", F32, f"gmlp_out_{l}")
            st.update(zp=zp, yg=yg)
        else:
            jb = l - NA
            qg = _mm(h1, (w_qg, jb), "nn", BF16, f"fox_qg_{jb}")
            q_raw = heads(qg[:, :D]).reshape(H * S, hd)
            qn = _head_norm_fwd(q_raw, row(b_q_norm_g[jb]), hd ** -0.5, f"fox_qnorm_{jb}").reshape(H, S, hd)
            qa = _augment(qn, kv["dcum"], True)
            qat = jnp.swapaxes(qa, 1, 2)
            o_tr, lse_r = _flash_fwd(kv["ka"], qat, kv["vat"], hd, f"fox_attn_{jb}")
            o_t = o_tr.transpose(2, 0, 1).reshape(S, D)
            og = _out_gate_fwd(o_t, qg, f"fox_gate_{jb}")
            y = _mm(og, (w_o, jb), "nn", F32, f"fox_out_{jb}")
            st.update(qg=qg, q_raw=q_raw, qat=qat, o_tr=o_tr, lse_r=lse_r, o_t=o_t, og=og)
        st["y"] = y
        x1, h2 = _post_pre_fwd(xc, y, row(post_mix_g[l]), g_m, row(pre_ffn_g[l]), sh_f, sc_f, f"post_mix_{l}")
        st["x1"] = x1
        gu, act = _ffn_up(h2, w_gu, l, gu_hw, f"ffn_gu_{l}")
        y2 = _mm(act, (w_dn, l), "nn", F32, f"ffn_down_{l}")
        if l + 1 < L:
            xc, h1 = _post_pre_fwd(x1, y2, row(post_ffn_g[l]), g_f, row(pre_mix_g[l + 1]), mod[l + 1][0], mod[l + 1][1],
                                   f"post_ffn_{l}")
        else:
            xc = _post_fwd(x1, y2, row(post_ffn_g[l]), g_f, f"post_ffn_{l}")
        st.update(h2=h2, gu=gu, act=act, y2=y2)
        saved.append(st)
        if l == NA - 1:
            hk = _norm_mod_fwd(xc, row(kv_norm_g), kv_sh, kv_sc, "kv_pre")
            kvf = _mm(hk, w_kv, "nn", F32, "kv_proj")
            k_raw = heads(kvf[:, :D]).reshape(H * S, hd)
            kn = _head_norm_fwd(k_raw, row(k_norm_g), 1.0, "kv_knorm").reshape(H, S, hd)
            vb = heads(kvf[:, D:2 * D]).astype(BF16)
            f_t = kvf[:, 2 * D:2 * D + H].T
            b_col = kv_b_f.reshape(H, 1)
            dcum = _dcum_fwd(f_t, b_col, "kv_dcum")
            vt = kvf[:, D:2 * D].astype(BF16).reshape(S, H, hd).transpose(1, 2, 0)
            vat = jnp.where(lax.broadcasted_iota(jnp.int32, (1, LANES, 1), 1) == hd, jnp.asarray(1, BF16),
                            jnp.pad(vt, ((0, 0), (0, LANES - hd), (0, 0))))
            ka = _augment(kn, dcum, False)
            kv = dict(x=xc, hk=hk, k_raw=k_raw, ka=ka, kat=jnp.swapaxes(ka, 1, 2), vb=vb, vat=vat,
                      f_t=f_t, b_col=b_col, dcum=dcum)

    dx, loss_part = _loss_bwd(xc, tgt, "loss")

    gl = {n: [None] * weights[n].shape[0] for n in
          ["pre_mix_g", "post_mix_g", "pre_ffn_g", "post_ffn_g", "ffn_w_gu", "ffn_w_down", "a_w_in", "a_b_in", "a_ln_g",
           "a_ln_b", "a_w_s", "a_b_s", "a_w_out", "b_w_qg", "b_q_norm_g", "b_w_o"]}
    dmod = [None] * L
    dkn = dvb = ddc = None
    gkv = {}
    for l in reversed(range(L)):
        st = saved[l]
        sh_m, sc_m, g_m, sh_f, sc_f, g_f = mod[l]
        if l == NA - 1:
            dk_raw, gkv["k_norm_g"] = _head_norm_bwd(jnp.swapaxes(dkn, 1, 2).reshape(H * S, hd), kv["k_raw"], row(k_norm_g),
                                                     1.0, "kv_knorm_bwd")
            df_t, db_f = _dcum_bwd(ddc.reshape(H, S), kv["f_t"], kv["b_col"], "kv_dcum_bwd")
            dkvf = jnp.concatenate([unheads(dk_raw.reshape(H, S, hd)), dvb.transpose(2, 0, 1).reshape(S, D), df_t.T,
                                    jnp.zeros((S, kv_pad - (2 * D + H)), F32)], axis=1).astype(BF16)
            gkv["kv_w"] = _mm(kv["hk"], dkvf, "tn", BF16, "kv_proj_dw")[:, :2 * D + H]
            dhk = _mm(dkvf, w_kv, "nt", F32, "kv_proj_dx")
            dx, gkv["kv_norm_g"], dsh, dsc = _norm_mod_bwd(dx, dhk, kv["x"], row(kv_norm_g), kv_sh, kv_sc, "kv_pre_bwd")
            gkv["kv_b_f"] = db_f.reshape(H)
            dmod_kv = jnp.concatenate([dsh, dsc], axis=1)
        dy2, gl["post_ffn_g"][l], dg_f = _post_bwd(dx, st["y2"], row(post_ffn_g[l]), g_f, f"post_ffn_bwd_{l}")
        gl["ffn_w_down"][l] = _mm(st["act"], dy2, "tn", BF16, f"ffn_down_dw_{l}")
        dgu = _ffn_down_dx(dy2, w_dn, l, st["gu"], gu_hw, f"ffn_down_dx_{l}")
        gl["ffn_w_gu"][l] = _mm(st["h2"], dgu, "tn", BF16, f"ffn_gu_dw_{l}")
        dh2 = _mm(dgu, (w_gu, l), "nt", F32, f"ffn_gu_dx_{l}")
        dx, dy, gl["pre_ffn_g"][l], dsh_f, dsc_f, gl["post_mix_g"][l], dg_m = _pre_post_bwd(
            dx, dh2, st["x1"], row(pre_ffn_g[l]), sh_f, sc_f, st["y"], row(post_mix_g[l]), g_m, f"pre_ffn_bwd_{l}")
        if l < NA:
            gl["a_w_out"][l] = _mm(st["yg"], dy, "tn", BF16, f"gmlp_out_dw_{l}")
            dyg = _mm(dy, (w_out, l), "nt", BF16, f"gmlp_out_dx_{l}")
            dzp, db_in, dlg, dlb, dws, dbs_t = _gmlp_bwd(dyg, st["zp"], row(b_in_f[l]), row(ln_g_f[l]), row(ln_b_f[l]),
                                                           ws_m[l], ws_mt[l], bs_t[l], f"gmlp_gate_bwd_{l}")
            gl["a_b_in"][l], gl["a_ln_g"][l], gl["a_ln_b"][l] = db_in[0], dlg[0], dlb[0]
            gl["a_w_s"][l], gl["a_b_s"][l] = dws * causal, dbs_t.T
            gl["a_w_in"][l] = _mm(st["h1"], dzp, "tn", BF16, f"gmlp_in_dw_{l}")
            dh1 = _mm(dzp, (w_in, l), "nt", F32, f"gmlp_in_dx_{l}")
        else:
            jb = l - NA
            gl["b_w_o"][jb] = _mm(st["og"], dy, "tn", BF16, f"fox_out_dw_{jb}")
            dog = _mm(dy, (w_o, jb), "nt", F32, f"fox_out_dx_{jb}")
            do_t, dgl = _out_gate_bwd(dog, st["o_t"], st["qg"], f"fox_gate_bwd_{jb}")
            dot = do_t.reshape(S, H, hd).transpose(1, 2, 0)
            dqa_tr, dka_tr, dv_j = _flash_bwd(kv["ka"], kv["kat"], st["qat"], kv["vb"], dot, st["o_tr"], st["lse_r"],
                                              f"fox_attn_bwd_{jb}")
            dqn = dqa_tr[:, :, :hd, :].transpose(0, 1, 3, 2).reshape(H, S, hd)
            dk_j = dka_tr[:, :hd, :]
            dd_j = dqa_tr[:, :, hd, :].reshape(H, S) - dka_tr[:, hd + 3, :]
            dkn = dk_j if dkn is None else dkn + dk_j
            dvb = dv_j if dvb is None else dvb + dv_j
            ddc = dd_j if ddc is None else ddc + dd_j
            dq_raw, dgq = _head_norm_bwd(dqn.reshape(H * S, hd), st["q_raw"], row(b_q_norm_g[jb]), hd ** -0.5, f"fox_qnorm_bwd_{jb}")
            gl["b_q_norm_g"][jb] = dgq[0]
            dqg = jnp.concatenate([unheads(dq_raw.reshape(H, S, hd)).astype(BF16), dgl], axis=1)
            gl["b_w_qg"][jb] = _mm(st["h1"], dqg, "tn", BF16, f"fox_qg_dw_{jb}")
            dh1 = _mm(dqg, (w_qg, jb), "nt", F32, f"fox_qg_dx_{jb}")
        dx, gl["pre_mix_g"][l], dsh_m, dsc_m = _norm_mod_bwd(dx, dh1, st["x0"], row(pre_mix_g[l]), sh_m, sc_m, f"pre_mix_bwd_{l}")
        dmod[l] = jnp.concatenate([dsh_m, dsc_m, dg_m, dsh_f, dsc_f, dg_f], axis=1)
    grad_x = dx[None]

    stack = lambda n: jnp.stack([t.reshape(weights[n].shape[1:]) for t in gl[n]])
    small = {"dmod": jnp.concatenate(dmod, axis=1), "dmod_kv": dmod_kv}
    for n in ["pre_mix_g", "post_mix_g", "pre_ffn_g", "post_ffn_g", "a_w_s", "a_b_s", "b_q_norm_g"]:
        small[n] = stack(n)
    for n in ["a_b_in", "a_ln_g", "a_ln_b"]:
        small[n] = jnp.stack(gl[n])
    for n in ["kv_norm_g", "kv_b_f", "k_norm_g"]:
        small[n] = gkv[n]
    small["loss"] = loss_part
    sizes = {n: t.size for n, t in small.items()}
    flat = jnp.concatenate([t.reshape(-1).astype(F32) for t in small.values()])
    rows_small = -(-flat.size // (LANES * BF16_ROWS)) * BF16_ROWS
    flat = jnp.pad(flat, (0, rows_small * LANES - flat.size)).reshape(rows_small, LANES)
    flat_all = _own_slot(_gather8([flat], "gather_small")[0], flat, me)
    flat_sum = _sum_slots(flat_all, "sum_small").reshape(-1)
    offs, o_ = {}, 0
    for n, sz in sizes.items():
        offs[n] = o_
        o_ += sz
    take = lambda n, shape: flat_sum[offs[n]:offs[n] + sizes[n]].reshape(shape)
    dmod_rows = flat_all.reshape(N_DEV, -1)[:, offs["dmod"]:offs["dmod"] + sizes["dmod"] + sizes["dmod_kv"]]
    dmod_rows = jnp.pad(dmod_rows, ((0, BF16_ROWS - N_DEV), (0, 0)))

    grads = {}
    loss = take("loss", ())
    grads["ada_b"] = take("dmod", (L, 6 * D))
    grads["kv_ada_b"] = take("dmod_kv", (2 * D,))
    for n in ["pre_mix_g", "post_mix_g", "pre_ffn_g", "post_ffn_g", "a_w_s", "a_b_s", "b_q_norm_g", "kv_norm_g", "kv_b_f", "k_norm_g"]:
        grads[n] = take(n, weights[n].shape)
    for n in ["a_b_in", "a_ln_g", "a_ln_b"]:
        full = take(n, small[n].shape)
        w = weights[n].shape[1]
        grads[n] = lax.dynamic_slice_in_dim(full, chip * w, w, axis=1)
    ada_g = []
    for l in range(L):
        cols = lax.dynamic_slice_in_dim(dmod_rows[:, l * 6 * D:(l + 1) * 6 * D], chip * ada_cols, ada_cols, axis=1)
        ada_g.append(_mm(c_act, cols, "tn", F32, f"mod_proj_dw_{l}"))
    grads["ada_w"] = jnp.stack(ada_g)
    cols = lax.dynamic_slice_in_dim(dmod_rows[:, L * 6 * D:], chip * kvada_cols, kvada_cols, axis=1)
    grads["kv_ada_w"] = _mm(c_act, cols, "tn", F32, "mod_proj_kv_dw")

    specs = {"ffn_w_gu": (2, 0, (0, 2, 1, 3)),
             "ffn_w_down": (1, 0, CHIP_ORDER), "a_w_in": (2, 0, CHIP_ORDER), "a_w_out": (1, 0, CHIP_ORDER),
             "kv_w": (0, 1, CHIP_ORDER), "b_w_qg": (2, 0, CHIP_ORDER), "b_w_o": (1, 0, CHIP_ORDER)}
    full_g = {n: jnp.stack(gl[n]) for n in big if n != "kv_w"}
    full_g["kv_w"] = gkv["kv_w"].reshape(D, N_CHIPS, kv_cols).transpose(1, 0, 2)
    from_core = _sibling_scatter([full_g[n] for n in big], [specs[n] for n in big], "scatter_g_core")
    chip_sums = [_sum_pair(_own_pieces(full_g[n], specs[n], ci), r, f"sum_g_core_{n}") for n, r in zip(big, from_core)]
    recv = _chip_scatter(chip_sums, "scatter_g_chip")
    recv = [_own_slot(r, lax.dynamic_index_in_dim(p, chip, 0, keepdims=False), chip) for r, p in zip(recv, chip_sums)]
    halves = [_sum_slots(r, f"sum_g_{n}") for n, r in zip(big, recv)]
    pairs = _sibling_pair(halves, "pair_g")
    for n, p, hlf in zip(big, pairs, halves):
        grads[n] = _own_slot(p, hlf, ci).reshape(weights[n].shape)

    outs_d, outs_m, outs_v = {}, {}, {}
    for n in names:
        w2 = weights[n] if weights[n].ndim > 1 else weights[n].reshape(1, -1)
        shp = w2.shape
        d_, m_, v_ = _adamw(w2, grads[n].reshape(shp), m_in[n].reshape(shp), v_in[n].reshape(shp), f"adamw_{n}")
        outs_d[n], outs_m[n], outs_v[n] = (t.reshape(weights[n].shape) for t in (d_, m_, v_))
    return (loss, grad_x, *[grads[n] for n in names], *[outs_d[n] for n in names],
            *[outs_m[n] for n in names], *[outs_v[n] for n in names])
```
